```python
import math
import jax
import jax.numpy as jnp
from jax import lax
import numpy as np

D_MODEL = 1024
BATCH = 4
SEQ = 4096
DEPTH = 1

N_HEADS = 8
HEAD_DIM = 64
N_KV = 2
GQA_RATIO = N_HEADS // N_KV
CMP_BLOCK = 32
CMP_STRIDE = 16
CMP_HIDDEN = 256
SEL_BLOCK = 64
N_SELECT = 16
WINDOW = 512
Q_BLOCK = 128
ATTN_SCALE = HEAD_DIM ** -0.5
NSA_WIDTH = N_HEADS * HEAD_DIM
KV_WIDTH = N_KV * HEAD_DIM
SSM_WIDTH = 512
GROUP = 16
N_GROUPS = SSM_WIDTH // GROUP
STATE = 64
DT_MIN = 1e-3
DT_MAX = 1e-1
N_EXPERTS = 256
TOP_K = 8
D_EXPERT = 256
N_EXPERT_GROUPS = 8
TOPK_GROUPS = 4
ROUTE_SCALE = 2.5
DISPATCH_BLOCK = 128
EPS = 1e-6
ADA_SCALE = 0.3
IN_SPLITS = (NSA_WIDTH,) + (KV_WIDTH,) * 6 + (3 * N_HEADS, SSM_WIDTH, D_MODEL, D_MODEL)
IN_COLS = sum(IN_SPLITS)

kernel_name = 'hybrid_nsa_s5_moe_block'


def rms_norm(x, g):
    xf = x.astype(jnp.float32)
    y = xf * lax.rsqrt(jnp.mean(xf * xf, axis=-1, keepdims=True) + EPS)
    return (y * g.astype(jnp.float32)).astype(x.dtype)


def masked_softmax(s, mask):
    s = jnp.where(mask, s.astype(jnp.float32), -1e30)
    return jax.nn.softmax(s, axis=-1) * mask


def swiglu(x, wg, wu, wd):
    return (jax.nn.silu(x @ wg) * (x @ wu)) @ wd


def compress_blocks(k, pe, w1, w2):
    B, S, G, Dh = k.shape
    nc = (S - CMP_BLOCK) // CMP_STRIDE + 1
    idx = CMP_STRIDE * jnp.arange(nc)[:, None] + jnp.arange(CMP_BLOCK)[None, :]
    kb = k[:, idx] + pe[None, None, :, None, :]
    kb = jnp.transpose(kb, (0, 1, 3, 2, 4)).reshape(B, nc, G, CMP_BLOCK * Dh)
    return jax.nn.gelu(kb @ w1) @ w2


def compressed_attention(q, kc, vc):
    S = q.shape[1]
    nc = kc.shape[1]
    t = jnp.arange(S)
    block_end = CMP_STRIDE * jnp.arange(nc) + CMP_BLOCK - 1
    mask = block_end[None, :] <= t[:, None]
    s = jnp.einsum('bsgrd,bngd->bgrsn', q, kc) * ATTN_SCALE
    p = masked_softmax(s, mask)
    o = jnp.einsum('bgrsn,bngd->bsgrd', p.astype(vc.dtype), vc)
    return o, p


def select_blocks(p_cmp, S):
    nc = p_cmp.shape[-1]
    nsb = S // SEL_BLOCK
    cstart = CMP_STRIDE * jnp.arange(nc)
    sstart = SEL_BLOCK * jnp.arange(nsb)
    overlap = ((cstart[:, None] < sstart[None, :] + SEL_BLOCK)
               & (cstart[:, None] + CMP_BLOCK > sstart[None, :])).astype(jnp.float32)
    imp = jnp.einsum('bgrsn,nj->bgsj', p_cmp, overlap)
    cur = (jnp.arange(S) // SEL_BLOCK)[:, None]
    j = jnp.arange(nsb)[None, :]
    valid = j <= cur
    forced = (j == 0) | (j == cur) | (j == cur - 1)
    imp = jnp.where(forced, jnp.inf, jnp.where(valid, imp, -jnp.inf))
    n_sel = min(N_SELECT, nsb)
    _, idx = lax.top_k(imp, n_sel)
    return idx


def selected_attention(q, ks, vs, idx):
    B, S, G, R, Dh = q.shape
    nsb = S // SEL_BLOCK
    kb = ks.reshape(B, nsb, SEL_BLOCK, G, Dh).transpose(0, 3, 1, 2, 4)
    vb = vs.reshape(B, nsb, SEL_BLOCK, G, Dh).transpose(0, 3, 1, 2, 4)
    qc = q.reshape(B, nsb, SEL_BLOCK, G, R, Dh).transpose(1, 0, 2, 3, 4, 5)
    ic = idx.reshape(B, G, nsb, SEL_BLOCK, -1).transpose(2, 0, 1, 3, 4)
    bi = jnp.arange(B)[:, None, None, None]
    gi = jnp.arange(G)[None, :, None, None]
    offs = jnp.arange(SEL_BLOCK)

    def chunk(args):
        qb, ib, ci = args
        kg = kb[bi, gi, ib]
        vg = vb[bi, gi, ib]
        t = ci * SEL_BLOCK + offs
        kpos = ib[..., None] * SEL_BLOCK + offs
        mask = kpos <= t[None, None, :, None, None]
        s = jnp.einsum('bqgrd,bgqnkd->bgrqnk', qb, kg) * ATTN_SCALE
        shp = s.shape
        m = jnp.broadcast_to(mask[:, :, None], shp).reshape(shp[:4] + (-1,))
        p = masked_softmax(s.reshape(shp[:4] + (-1,)), m).reshape(shp)
        return jnp.einsum('bgrqnk,bgqnkd->bqgrd', p.astype(vg.dtype), vg)

    o = lax.map(chunk, (qc, ic, jnp.arange(nsb)))
    return o.transpose(1, 0, 2, 3, 4, 5).reshape(B, S, G, R, Dh)


def window_attention(q, kw, vw):
    B, S, G, R, Dh = q.shape
    nq = S // Q_BLOCK
    nback = WINDOW // Q_BLOCK
    pad = ((0, 0), (WINDOW, 0), (0, 0), (0, 0))
    kp = jnp.pad(kw, pad).reshape(B, nq + nback, Q_BLOCK, G, Dh)
    vp = jnp.pad(vw, pad).reshape(B, nq + nback, Q_BLOCK, G, Dh)
    kwin = jnp.concatenate([kp[:, j:j + nq] for j in range(nback + 1)], axis=2)
    vwin = jnp.concatenate([vp[:, j:j + nq] for j in range(nback + 1)], axis=2)
    qb = q.reshape(B, nq, Q_BLOCK, G, R, Dh)
    blk = jnp.arange(nq)[:, None] * Q_BLOCK
    t = blk + jnp.arange(Q_BLOCK)[None, :]
    kpos = blk - WINDOW + jnp.arange((nback + 1) * Q_BLOCK)[None, :]
    diff = t[:, :, None] - kpos[:, None, :]
    mask = (diff >= 0) & (diff < WINDOW) & (kpos[:, None, :] >= 0)
    s = jnp.einsum('bnqgrd,bnkgd->bgrnqk', qb, kwin) * ATTN_SCALE
    p = masked_softmax(s, mask)
    o = jnp.einsum('bgrnqk,bnkgd->bnqgrd', p.astype(vwin.dtype), vwin)
    return o.reshape(B, S, G, R, Dh)


def s5_mixer(u, a_re, a_im, log_dt, b_re, b_im, c_re, c_im, d_skip, w_glu, b_glu):
    B, S, _ = u.shape
    f32 = jnp.float32
    lam = lax.complex(a_re.astype(f32), a_im.astype(f32))
    dt = jnp.exp(log_dt.astype(f32))[:, None]
    lam_bar = jnp.exp(lam * dt)
    b_mat = lax.complex(b_re.astype(f32), b_im.astype(f32))
    b_bar = ((lam_bar - 1.0) / lam)[..., None] * b_mat
    c_mat = lax.complex(c_re.astype(f32), c_im.astype(f32))
    uf = u.astype(f32)
    ug = uf.reshape(B, S, N_GROUPS, GROUP).astype(jnp.complex64)
    bu = jnp.einsum('bsgp,gnp->bsgn', ug, b_bar)
    a = jnp.broadcast_to(lam_bar, bu.shape)

    def combine(left, right):
        a1, b1 = left
        a2, b2 = right
        return a1 * a2, a2 * b1 + b2

    _, states = lax.associative_scan(combine, (a, bu), axis=1)
    y = jnp.einsum('bsgn,gpn->bsgp', states, c_mat).real.reshape(B, S, SSM_WIDTH)
    y = (y + d_skip.astype(f32) * uf).astype(u.dtype)
    z = jax.nn.gelu(y)
    return z * jax.nn.sigmoid(z @ w_glu + b_glu)


def route(h, w_router, router_bias):
    f32 = jnp.float32
    n = h.shape[0]
    scores = jax.nn.sigmoid(h.astype(f32) @ w_router.astype(f32))
    biased = scores + router_bias.astype(f32)
    grp = biased.reshape(n, N_EXPERT_GROUPS, N_EXPERTS // N_EXPERT_GROUPS)
    grp_score = jnp.sum(lax.top_k(grp, 2)[0], axis=-1)
    _, gidx = lax.top_k(grp_score, TOPK_GROUPS)
    gmask = jnp.any(gidx[:, :, None] == jnp.arange(N_EXPERT_GROUPS)[None, None, :], axis=1)
    emask = jnp.repeat(gmask, N_EXPERTS // N_EXPERT_GROUPS, axis=1)
    _, eidx = lax.top_k(jnp.where(emask, biased, -jnp.inf), TOP_K)
    w = jnp.take_along_axis(scores, eidx, axis=1)
    w = w / jnp.sum(w, axis=-1, keepdims=True) * ROUTE_SCALE
    return eidx, w


def routed_experts(h, eidx, w, w_gate, w_up, w_down):
    n, d = h.shape
    nk = n * TOP_K
    e_flat = eidx.reshape(-1)
    tok_flat = jnp.repeat(jnp.arange(n, dtype=jnp.int32), TOP_K)
    w_flat = w.reshape(-1)
    order = jnp.argsort(e_flat)
    se = e_flat[order]
    stok = tok_flat[order]
    sw = w_flat[order]
    counts = jax.ops.segment_sum(jnp.ones_like(se), se, num_segments=N_EXPERTS)
    start = jnp.cumsum(counts) - counts
    padded = (counts + DISPATCH_BLOCK - 1) // DISPATCH_BLOCK * DISPATCH_BLOCK
    pad_end = jnp.cumsum(padded)
    pad_start = pad_end - padded
    dest = pad_start[se] + jnp.arange(nk, dtype=jnp.int32) - start[se]
    cap = (nk + N_EXPERTS * DISPATCH_BLOCK + DISPATCH_BLOCK - 1) // DISPATCH_BLOCK * DISPATCH_BLOCK
    nb = cap // DISPATCH_BLOCK
    buf_tok = jnp.full((cap,), n, jnp.int32).at[dest].set(stok)
    buf_w = jnp.zeros((cap,), h.dtype).at[dest].set(sw.astype(h.dtype))
    blk_e = jnp.minimum(jnp.searchsorted(pad_end, jnp.arange(nb, dtype=jnp.int32) * DISPATCH_BLOCK,
                                         side='right'), N_EXPERTS - 1)
    hp = jnp.concatenate([h, jnp.zeros((1, d), h.dtype)], axis=0)

    def run(args):
        toks, e = args
        return swiglu(hp[toks], w_gate[e], w_up[e], w_down[e])

    y = lax.map(run, (buf_tok.reshape(nb, DISPATCH_BLOCK), blk_e)).reshape(cap, d)
    y = y * buf_w[:, None]
    return jnp.zeros((n + 1, d), h.dtype).at[buf_tok].add(y)[:n]


def hybrid_block(x, c, w_ada, b_ada, g_norm1, g_norm2, w_in, q_gain, kc_gain, ks_gain, kw_gain,
                 pe_k, pe_v, w_cmp_k1, w_cmp_k2, w_cmp_v1, w_cmp_v2,
                 a_re, a_im, log_dt, b_re, b_im, c_re, c_im, d_skip, w_glu, b_glu,
                 w_up_attn, w_up_ssm, w_out, w_router, router_bias,
                 w_gate, w_up, w_down, ws_gate, ws_up, ws_down):
    B, S, D = x.shape
    mod = jax.nn.silu(c) @ w_ada + b_ada
    sh1, sc1, gt1, sh2, sc2, gt2 = jnp.split(mod[:, None, :], 6, axis=-1)

    h = rms_norm(x, g_norm1) * (1.0 + sc1) + sh1
    parts = jnp.split(h @ w_in, np.cumsum(IN_SPLITS)[:-1].tolist(), axis=-1)
    q, kc, vc, ks, vs, kw, vw, g_nsa, u, g_attn, g_ssm = parts
    q = rms_norm(q.reshape(B, S, N_HEADS, HEAD_DIM), q_gain).reshape(B, S, N_KV, GQA_RATIO, HEAD_DIM)
    kc = rms_norm(compress_blocks(kc.reshape(B, S, N_KV, HEAD_DIM), pe_k, w_cmp_k1, w_cmp_k2), kc_gain)
    vc = compress_blocks(vc.reshape(B, S, N_KV, HEAD_DIM), pe_v, w_cmp_v1, w_cmp_v2)
    ks = rms_norm(ks.reshape(B, S, N_KV, HEAD_DIM), ks_gain)
    vs = vs.reshape(B, S, N_KV, HEAD_DIM)
    kw = rms_norm(kw.reshape(B, S, N_KV, HEAD_DIM), kw_gain)
    vw = vw.reshape(B, S, N_KV, HEAD_DIM)
    o_cmp, p_cmp = compressed_attention(q, kc, vc)
    sel_idx = select_blocks(p_cmp, S)
    o_sel = selected_attention(q, ks, vs, sel_idx)
    o_win = window_attention(q, kw, vw)
    g = jax.nn.sigmoid(g_nsa).reshape(B, S, N_KV, GQA_RATIO, 3)
    o_nsa = (g[..., 0:1] * o_cmp + g[..., 1:2] * o_sel + g[..., 2:3] * o_win).reshape(B, S, NSA_WIDTH)
    y_ssm = s5_mixer(u, a_re, a_im, log_dt, b_re, b_im, c_re, c_im, d_skip, w_glu, b_glu)
    merged = (jax.nn.sigmoid(g_attn) * (o_nsa @ w_up_attn)
              + jax.nn.sigmoid(g_ssm) * (y_ssm @ w_up_ssm))
    x = x + gt1 * (merged @ w_out)

    h2 = rms_norm(x, g_norm2) * (1.0 + sc2) + sh2
    hf = h2.reshape(B * S, D)
    eidx, ew = route(hf, w_router, router_bias)
    y = swiglu(hf, ws_gate, ws_up, ws_down) + routed_experts(hf, eidx, ew, w_gate, w_up, w_down)
    return x + gt2 * y.reshape(B, S, D)


def setup_inputs(seed: int = 0) -> dict:
    key = jax.random.key(seed)
    keys = iter(jax.random.split(key, 48))
    f32 = jnp.float32
    L = DEPTH

    def nrm(shape, scale):
        return scale * jax.random.normal(next(keys), shape, f32)

    def gain(shape):
        return 1.0 + nrm(shape, 0.01)

    D = D_MODEL
    inp = {}
    inp['x'] = nrm((BATCH, SEQ, D), 1.0)
    inp['c'] = nrm((BATCH, D), 1.0)
    inp['w_ada'] = nrm((L, D, 6 * D), ADA_SCALE * D ** -0.5)
    inp['b_ada'] = nrm((L, 6 * D), 0.01)
    inp['g_norm1'] = gain((L, D))
    inp['g_norm2'] = gain((L, D))
    inp['w_in'] = nrm((L, D, IN_COLS), D ** -0.5)
    inp['q_gain'] = gain((L, HEAD_DIM))
    inp['kc_gain'] = gain((L, HEAD_DIM))
    inp['ks_gain'] = gain((L, HEAD_DIM))
    inp['kw_gain'] = gain((L, HEAD_DIM))
    inp['pe_k'] = nrm((L, CMP_BLOCK, HEAD_DIM), 0.02)
    inp['pe_v'] = nrm((L, CMP_BLOCK, HEAD_DIM), 0.02)
    inp['w_cmp_k1'] = nrm((L, CMP_BLOCK * HEAD_DIM, CMP_HIDDEN), (CMP_BLOCK * HEAD_DIM) ** -0.5)
    inp['w_cmp_k2'] = nrm((L, CMP_HIDDEN, HEAD_DIM), CMP_HIDDEN ** -0.5)
    inp['w_cmp_v1'] = nrm((L, CMP_BLOCK * HEAD_DIM, CMP_HIDDEN), (CMP_BLOCK * HEAD_DIM) ** -0.5)
    inp['w_cmp_v2'] = nrm((L, CMP_HIDDEN, HEAD_DIM), CMP_HIDDEN ** -0.5)
    inp['a_re'] = -0.5 + nrm((L, N_GROUPS, STATE), 0.01)
    inp['a_im'] = math.pi * jnp.arange(STATE, dtype=f32)[None, None, :] + nrm((L, N_GROUPS, STATE), 0.01)
    inp['log_dt'] = jax.random.uniform(next(keys), (L, N_GROUPS), f32,
                                       minval=math.log(DT_MIN), maxval=math.log(DT_MAX))
    inp['b_re'] = nrm((L, N_GROUPS, STATE, GROUP), (2 * GROUP) ** -0.5)
    inp['b_im'] = nrm((L, N_GROUPS, STATE, GROUP), (2 * GROUP) ** -0.5)
    inp['c_re'] = nrm((L, N_GROUPS, GROUP, STATE), STATE ** -0.5)
    inp['c_im'] = nrm((L, N_GROUPS, GROUP, STATE), STATE ** -0.5)
    inp['d_skip'] = nrm((L, SSM_WIDTH), 1.0)
    inp['w_glu'] = nrm((L, SSM_WIDTH, SSM_WIDTH), SSM_WIDTH ** -0.5)
    inp['b_glu'] = nrm((L, SSM_WIDTH), 0.01)
    inp['w_up_attn'] = nrm((L, NSA_WIDTH, D), NSA_WIDTH ** -0.5)
    inp['w_up_ssm'] = nrm((L, SSM_WIDTH, D), SSM_WIDTH ** -0.5)
    inp['w_out'] = nrm((L, D, D), D ** -0.5)
    inp['w_router'] = nrm((L, D, N_EXPERTS), D ** -0.5)
    inp['router_bias'] = nrm((L, N_EXPERTS), 0.01)
    inp['w_gate'] = nrm((L, N_EXPERTS, D, D_EXPERT), D ** -0.5)
    inp['w_up'] = nrm((L, N_EXPERTS, D, D_EXPERT), D ** -0.5)
    inp['w_down'] = nrm((L, N_EXPERTS, D_EXPERT, D), D_EXPERT ** -0.5)
    inp['ws_gate'] = nrm((L, D, D_EXPERT), D ** -0.5)
    inp['ws_up'] = nrm((L, D, D_EXPERT), D ** -0.5)
    inp['ws_down'] = nrm((L, D_EXPERT, D), D_EXPERT ** -0.5)
    return inp


def reference(x, c, w_ada, b_ada, g_norm1, g_norm2, w_in, q_gain, kc_gain, ks_gain, kw_gain,
              pe_k, pe_v, w_cmp_k1, w_cmp_k2, w_cmp_v1, w_cmp_v2,
              a_re, a_im, log_dt, b_re, b_im, c_re, c_im, d_skip, w_glu, b_glu,
              w_up_attn, w_up_ssm, w_out, w_router, router_bias,
              w_gate, w_up, w_down, ws_gate, ws_up, ws_down):
    layer_params = (w_ada, b_ada, g_norm1, g_norm2, w_in, q_gain, kc_gain, ks_gain, kw_gain,
                    pe_k, pe_v, w_cmp_k1, w_cmp_k2, w_cmp_v1, w_cmp_v2,
                    a_re, a_im, log_dt, b_re, b_im, c_re, c_im, d_skip, w_glu, b_glu,
                    w_up_attn, w_up_ssm, w_out, w_router, router_bias,
                    w_gate, w_up, w_down, ws_gate, ws_up, ws_down)
    for layer in range(DEPTH):
        x = hybrid_block(x, c, *[p[layer] for p in layer_params])
    return x
```

```python
import functools
import math

import jax
import jax.numpy as jnp
import numpy as np
from jax import lax
from jax.experimental import pallas as pl
from jax.experimental.pallas import tpu as pltpu

F32 = jnp.float32
BF16 = jnp.bfloat16
I32 = jnp.int32
HIGHEST = lax.Precision.HIGHEST

D_MODEL = 1024
BATCH = 4
SEQ = 4096
N_TOK = BATCH * SEQ
N_HEADS = 8
HEAD_DIM = 64
N_KV = 2
CMP_BLOCK = 32
CMP_STRIDE = 16
CMP_HIDDEN = 256
N_CMP = 256
SEL_BLOCK = 64
N_SEL_BLOCKS = SEQ // SEL_BLOCK
N_SELECT = 16
WINDOW = 512
ATTN_SCALE = HEAD_DIM ** -0.5
NSA_WIDTH = N_HEADS * HEAD_DIM
SSM_WIDTH = 512
GROUP = 16
N_GROUPS = SSM_WIDTH // GROUP
STATE = 64
N_EXPERTS = 256
TOP_K = 8
D_EXPERT = 256
N_EXPERT_GROUPS = 8
EXPERTS_PER_GROUP = N_EXPERTS // N_EXPERT_GROUPS
TOPK_GROUPS = 4
ROUTE_SCALE = 2.5
DISPATCH_BLOCK = 128
EPS = 1e-6
NEG = -1e30

LANES = 128
S5_T = 16
S5_SG = 4
S5_GL = N_GROUPS // S5_SG
S5_CH = N_TOK // S5_T
S5_CH_PER_BATCH = SEQ // S5_T
S5_NSTATE = S5_GL * STATE * 2

NK = N_TOK * TOP_K
CAP = (NK + N_EXPERTS * DISPATCH_BLOCK + DISPATCH_BLOCK - 1) // DISPATCH_BLOCK * DISPATCH_BLOCK
N_BLK = CAP // DISPATCH_BLOCK

VMEM_LIMIT = 48 * 1024 * 1024


def _cparams(*sem):
    return pltpu.CompilerParams(dimension_semantics=tuple(sem), vmem_limit_bytes=VMEM_LIMIT)


def _dot(a, b):
    return jnp.dot(a, b, preferred_element_type=F32)


def _dot_nt(a, b):
    return lax.dot_general(a, b, (((1,), (1,)), ((), ())), preferred_element_type=F32)


def _split_dot(v, w):
    hi = v.astype(BF16)
    lo = (v - hi.astype(F32)).astype(BF16)
    return _dot(hi, w) + _dot(lo, w)


def _seg_rms(v, bd, gain):
    ss = _split_dot(v * v, bd)
    return v * lax.rsqrt(ss * (1.0 / HEAD_DIM) + EPS) * gain


def _gelu(x):
    return 0.5 * x * (1.0 + jnp.tanh(0.7978845608028654 * (x + 0.044715 * (x * x * x))))


def _silu(x):
    return x * jax.nn.sigmoid(x)


def _ada_kernel(c_ref, w_ref, b_ref, o_ref):
    c = c_ref[...]
    o_ref[...] = jnp.dot(_silu(c), w_ref[...], preferred_element_type=F32, precision=HIGHEST) + b_ref[...]


def _ada(c, w_ada, b_ada):
    cp = jnp.pad(c, ((0, 8 - BATCH), (0, 0)))
    tn = 1536
    out = pl.pallas_call(
        _ada_kernel,
        grid=(6 * D_MODEL // tn,),
        in_specs=[pl.BlockSpec((8, D_MODEL), lambda j: (0, 0)),
                  pl.BlockSpec((D_MODEL, tn), lambda j: (0, j)),
                  pl.BlockSpec((1, tn), lambda j: (0, j))],
        out_specs=pl.BlockSpec((8, tn), lambda j: (0, j)),
        out_shape=jax.ShapeDtypeStruct((8, 6 * D_MODEL), F32),
        compiler_params=_cparams("arbitrary"),
        name="ada",
    )(cp, w_ada, b_ada.reshape(1, -1))
    return out.reshape(8, 6, D_MODEL)


_C_Q = 0
_C_KC = 512
_C_VC = 640
_C_KS = 768
_C_VS = 1024
_C_KW = 1280
_C_VW = 1536
_C_GN = 1792
_C_U = 1920
_C_GA = 2432
_C_GS = 3456
_C_END = 4480
PROJ_TM = 512


def _proj_kernel(x_ref, mod_ref, g1_ref, w_ref, qg_ref, ksg_ref, kwg_ref, bd512_ref, bd256_ref,
                 q_ref, kc_ref, vc_ref, ks_ref, vs_ref, kw_ref, vw_ref, gn_ref, u_ref, ga_ref, gs_ref):
    x = x_ref[...]
    ms = jnp.mean(x * x, axis=-1, keepdims=True)
    mod = mod_ref[0]
    h = (x * lax.rsqrt(ms + EPS) * g1_ref[...]) * (1.0 + mod[1:2]) + mod[0:1]
    hb = h.astype(BF16)

    def p(lo, hi):
        return _dot(hb, w_ref[:, lo:hi])

    q_ref[...] = _seg_rms(p(_C_Q, _C_KC), bd512_ref[...], qg_ref[...] * ATTN_SCALE).astype(BF16)
    kc_ref[...] = p(_C_KC, _C_VC).astype(BF16)
    vc_ref[...] = p(_C_VC, _C_KS).astype(BF16)
    ks_ref[...] = _seg_rms(p(_C_KS, _C_VS), bd256_ref[...], ksg_ref[...]).astype(BF16)
    vs_ref[...] = p(_C_VS, _C_KW).astype(BF16)
    kw_ref[...] = _seg_rms(p(_C_KW, _C_VW), bd256_ref[...], kwg_ref[...]).astype(BF16)
    vw_ref[...] = p(_C_VW, _C_GN).astype(BF16)
    gn_ref[...] = jax.nn.sigmoid(p(_C_GN, _C_U))
    u_ref[...] = p(_C_U, _C_GA)
    ga_ref[...] = jax.nn.sigmoid(p(_C_GA, _C_GS)).astype(BF16)
    gs_ref[...] = jax.nn.sigmoid(p(_C_GS, _C_END)).astype(BF16)


def _dup_cols(w):
    return jnp.concatenate([w[:, :64], w[:, :64], w[:, 64:], w[:, 64:]], axis=1)


def _block_ones(n):
    return jnp.kron(jnp.eye(n // HEAD_DIM, dtype=F32), jnp.ones((HEAD_DIM, HEAD_DIM), F32)).astype(BF16)


def _proj(x2, mod, g_norm1, w_in, q_gain, ks_gain, kw_gain):
    o = np.cumsum((0, 512, 128, 128, 128, 128, 128, 128, 24, 512, 1024, 1024))
    parts = [w_in[:, o[i]:o[i + 1]] for i in range(11)]
    wq, wkc, wvc, wks, wvs, wkw, wvw, wgn, wu, wga, wgs = parts
    w = jnp.concatenate([wq, wkc, wvc, _dup_cols(wks), _dup_cols(wvs), _dup_cols(wkw), _dup_cols(wvw),
                         jnp.pad(wgn, ((0, 0), (0, LANES - 24))), wu, wga, wgs], axis=1).astype(BF16)
    tm = PROJ_TM
    row = lambda i: (i, 0)
    fix = lambda i: (0, 0)
    outs = [(512, BF16), (128, BF16), (128, BF16), (256, BF16), (256, BF16), (256, BF16), (256, BF16),
            (128, F32), (512, F32), (1024, BF16), (1024, BF16)]
    return pl.pallas_call(
        _proj_kernel,
        grid=(N_TOK // tm,),
        in_specs=[pl.BlockSpec((tm, D_MODEL), row),
                  pl.BlockSpec((1, 6, D_MODEL), lambda i: (i // (SEQ // tm), 0, 0)),
                  pl.BlockSpec((1, D_MODEL), fix),
                  pl.BlockSpec((D_MODEL, _C_END), fix),
                  pl.BlockSpec((1, 512), fix), pl.BlockSpec((1, 256), fix), pl.BlockSpec((1, 256), fix),
                  pl.BlockSpec((512, 512), fix), pl.BlockSpec((256, 256), fix)],
        out_specs=[pl.BlockSpec((tm, wd), row) for wd, _ in outs],
        out_shape=[jax.ShapeDtypeStruct((N_TOK, wd), dt) for wd, dt in outs],
        compiler_params=_cparams("arbitrary"),
        name="proj",
    )(x2, mod, g_norm1.reshape(1, -1), w,
      jnp.tile(q_gain, N_HEADS).reshape(1, -1), jnp.tile(ks_gain, 4).reshape(1, -1),
      jnp.tile(kw_gain, 4).reshape(1, -1), _block_ones(512), _block_ones(256))


def _compress_kernel(r_ref, pe_ref, w1_ref, w2_ref, bd_ref, gain_ref, o_ref, *, do_norm):
    r = r_ref[0].astype(F32)
    p0 = _dot((r + pe_ref[0]).astype(BF16), w1_ref[0])
    p1 = _dot((r + pe_ref[1]).astype(BF16), w1_ref[1])
    hid = p0 + pltpu.roll(p1, N_CMP - 1, 0)
    c = _dot(_gelu(hid).astype(BF16), w2_ref[...])
    if do_norm:
        c = _seg_rms(c, bd_ref[...], gain_ref[...])
    o_ref[0] = c.astype(BF16)


def _compress(raw, pe, w1, w2, gain, do_norm):
    r = raw.reshape(BATCH, SEQ // CMP_STRIDE, CMP_STRIDE * LANES)
    eye = jnp.eye(N_KV, dtype=F32)
    w1r = w1.reshape(2, CMP_STRIDE, HEAD_DIM, CMP_HIDDEN)
    w1big = jnp.einsum('hldc,gk->hlgdkc', w1r, eye).reshape(2, CMP_STRIDE * LANES, N_KV * CMP_HIDDEN).astype(BF16)
    w2big = jnp.einsum('cd,gk->gckd', w2, eye)
    w2big = jnp.concatenate([w2big, w2big], axis=-1).reshape(N_KV * CMP_HIDDEN, 4 * HEAD_DIM).astype(BF16)
    pe_big = jnp.broadcast_to(pe.reshape(2, CMP_STRIDE, 1, HEAD_DIM), (2, CMP_STRIDE, N_KV, HEAD_DIM))
    pe_big = pe_big.reshape(2, 1, CMP_STRIDE * LANES)
    fix2 = lambda b: (0, 0)
    fix3 = lambda b: (0, 0, 0)
    return pl.pallas_call(
        functools.partial(_compress_kernel, do_norm=do_norm),
        grid=(BATCH,),
        in_specs=[pl.BlockSpec((1, N_CMP, CMP_STRIDE * LANES), lambda b: (b, 0, 0)),
                  pl.BlockSpec((2, 1, CMP_STRIDE * LANES), fix3),
                  pl.BlockSpec((2, CMP_STRIDE * LANES, N_KV * CMP_HIDDEN), fix3),
                  pl.BlockSpec((N_KV * CMP_HIDDEN, 256), fix2),
                  pl.BlockSpec((256, 256), fix2), pl.BlockSpec((1, 256), fix2)],
        out_specs=pl.BlockSpec((1, N_CMP, 256), lambda b: (b, 0, 0)),
        out_shape=jax.ShapeDtypeStruct((BATCH, N_CMP, 256), BF16),
        compiler_params=_cparams("arbitrary"),
        name="compress_k" if do_norm else "compress_v",
    )(r, pe_big, w1big, w2big, _block_ones(256), jnp.tile(gain, 4).reshape(1, -1))


ATT_TQ = 256


def _head_variants(qb):
    lane = lax.broadcasted_iota(I32, qb.shape, 1)
    z = jnp.zeros_like(qb)
    return jnp.where(lane < HEAD_DIM, qb, z), jnp.where(lane < HEAD_DIM, z, qb)


def _cmp_kernel(q_ref, kc_ref, vc_ref, ov_ref, o_ref, sel_ref):
    tq = ATT_TQ
    qi = pl.program_id(1)
    tpos = qi * tq + lax.broadcasted_iota(I32, (tq, N_CMP), 0)
    nidx = lax.broadcasted_iota(I32, (tq, N_CMP), 1)
    mask = (CMP_STRIDE * nidx + (CMP_BLOCK - 1)) <= tpos
    lane_lo = lax.broadcasted_iota(I32, (tq, LANES), 1) < HEAD_DIM
    for g in range(N_KV):
        kd = kc_ref[0, :, g * LANES:(g + 1) * LANES]
        vd = vc_ref[0, :, g * LANES:(g + 1) * LANES]
        psum = jnp.zeros((tq, N_CMP), F32)
        for jb in range(2):
            blk = 2 * g + jb
            pv = []
            for qv in _head_variants(q_ref[:, blk * LANES:(blk + 1) * LANES]):
                s = jnp.where(mask, _dot_nt(qv, kd), NEG)
                m = jnp.max(s, axis=-1, keepdims=True)
                e = jnp.where(mask, jnp.exp(s - m), 0.0)
                l = jnp.sum(e, axis=-1, keepdims=True)
                p = e / jnp.where(l > 0.0, l, 1.0)
                psum = psum + p
                pv.append(_dot(p.astype(BF16), vd))
            o_ref[:, blk * LANES:(blk + 1) * LANES] = jnp.where(lane_lo, pv[0], pv[1]).astype(BF16)
        imp = _split_dot(psum, ov_ref[...])
        imp_t = imp.T[:N_SEL_BLOCKS]
        j = lax.broadcasted_iota(I32, (N_SEL_BLOCKS, tq), 0)
        cur = jnp.right_shift(qi * tq + lax.broadcasted_iota(I32, (N_SEL_BLOCKS, tq), 1), 6)
        forced = (j == 0) | (j == cur) | (j == cur - 1)
        v = jnp.where(forced, jnp.inf, jnp.where(j <= cur, imp_t, -jnp.inf))
        rank = jnp.zeros((N_SEL_BLOCKS, tq), F32)
        for jp in range(N_SEL_BLOCKS):
            row = v[jp:jp + 1, :]
            tie = jnp.where(j > jp, 1.0, 0.0)
            rank = rank + jnp.where(row > v, 1.0, jnp.where(row == v, tie, 0.0))
        sel_t = jnp.where(rank < float(N_SELECT), 1.0, 0.0)
        sel_t = jnp.concatenate([sel_t, jnp.zeros_like(sel_t)], axis=0)
        sel_ref[:, g * LANES:(g + 1) * LANES] = sel_t.T.astype(BF16)


def _cmp_attn(q, kcn, vcn):
    nc = np.arange(N_CMP)
    sb = np.arange(LANES)
    ov = ((CMP_STRIDE * nc[:, None] < SEL_BLOCK * sb[None, :] + SEL_BLOCK)
          & (CMP_STRIDE * nc[:, None] + CMP_BLOCK > SEL_BLOCK * sb[None, :])
          & (nc[:, None] < N_CMP - 1) & (sb[None, :] < N_SEL_BLOCKS))
    ov = jnp.asarray(ov, BF16)
    tq = ATT_TQ
    nq = SEQ // tq
    row = lambda b, i: (b * nq + i, 0)
    return pl.pallas_call(
        _cmp_kernel,
        grid=(BATCH, nq),
        in_specs=[pl.BlockSpec((tq, NSA_WIDTH), row),
                  pl.BlockSpec((1, N_CMP, 256), lambda b, i: (b, 0, 0)),
                  pl.BlockSpec((1, N_CMP, 256), lambda b, i: (b, 0, 0)),
                  pl.BlockSpec((N_CMP, LANES), lambda b, i: (0, 0))],
        out_specs=[pl.BlockSpec((tq, NSA_WIDTH), row), pl.BlockSpec((tq, 256), row)],
        out_shape=[jax.ShapeDtypeStruct((N_TOK, NSA_WIDTH), BF16), jax.ShapeDtypeStruct((N_TOK, 256), BF16)],
        compiler_params=_cparams("arbitrary", "arbitrary"),
        name="cmp_attn",
    )(q, kcn, vcn, ov)


ATT_TK = 256


def _selwin_kernel(q_ref, ks_ref, vs_ref, kw_ref, vw_ref, sel_ref, osel_ref, owin_ref, m_ref, l_ref, acc_ref):
    tq, tk = ATT_TQ, ATT_TK
    qi = pl.program_id(1)
    tpos = qi * tq + lax.broadcasted_iota(I32, (tq, tk), 0)
    kloc = lax.broadcasted_iota(I32, (tq, tk), 1)
    lane_lo = lax.broadcasted_iota(I32, (tq, LANES), 1) < HEAD_DIM
    e_row = lax.broadcasted_iota(I32, (LANES, tk), 0)
    e_col = jnp.right_shift(lax.broadcasted_iota(I32, (LANES, tk), 1), 6)

    def reset():
        m_ref[...] = jnp.full(m_ref.shape, NEG, F32)
        l_ref[...] = jnp.zeros(l_ref.shape, F32)
        acc_ref[...] = jnp.zeros(acc_ref.shape, F32)

    def update(qvars, kd, vd, mask):
        for jb in range(2):
            alphas, pvs = [], []
            for par in range(2):
                h = 2 * jb + par
                s = jnp.where(mask, _dot_nt(qvars[jb][par], kd), NEG)
                m_old = m_ref[h]
                m_new = jnp.maximum(m_old, jnp.max(s, axis=-1, keepdims=True))
                alpha = jnp.exp(m_old - m_new)
                p = jnp.where(mask, jnp.exp(s - m_new), 0.0)
                l_ref[h] = alpha * l_ref[h] + jnp.sum(p, axis=-1, keepdims=True)
                m_ref[h] = m_new
                alphas.append(alpha)
                pvs.append(_dot(p.astype(BF16), vd))
            acc_ref[jb] = (acc_ref[jb] * jnp.where(lane_lo, alphas[0], alphas[1])
                           + jnp.where(lane_lo, pvs[0], pvs[1]))

    def finish(out_ref, g):
        for jb in range(2):
            blk = 2 * g + jb
            l = jnp.where(lane_lo, l_ref[2 * jb], l_ref[2 * jb + 1])
            out_ref[:, blk * LANES:(blk + 1) * LANES] = (acc_ref[jb] / l).astype(BF16)

    for g in range(N_KV):
        gl = slice(g * LANES, (g + 1) * LANES)
        qvars = [_head_variants(q_ref[:, (2 * g + jb) * LANES:(2 * g + jb + 1) * LANES]) for jb in range(2)]
        selg = sel_ref[:, gl]

        reset()

        def sel_step(kt, carry):
            k0 = pl.multiple_of(kt * tk, tk)
            expand = jnp.where(e_row == kt * (tk // SEL_BLOCK) + e_col, 1.0, 0.0).astype(BF16)
            chosen = _dot(selg, expand) > 0.5
            mask = chosen & (k0 + kloc <= tpos)
            update(qvars, ks_ref[0, pl.ds(k0, tk), gl], vs_ref[0, pl.ds(k0, tk), gl], mask)
            return carry

        lax.fori_loop(0, qi + 1, sel_step, 0)
        finish(osel_ref, g)

        reset()

        def win_step(kt, carry):
            k0 = pl.multiple_of(kt * tk, tk)
            diff = tpos - (k0 + kloc)
            mask = (diff >= 0) & (diff < WINDOW)
            update(qvars, kw_ref[0, pl.ds(k0, tk), gl], vw_ref[0, pl.ds(k0, tk), gl], mask)
            return carry

        lax.fori_loop(jnp.maximum(qi - WINDOW // tk, 0), qi + 1, win_step, 0)
        finish(owin_ref, g)


def _selwin(q, ks, vs, kw, vw, sel):
    tq = ATT_TQ
    nq = SEQ // tq
    row = lambda b, i: (b * nq + i, 0)
    kv = lambda b, i: (b, 0, 0)
    r3 = lambda a: a.reshape(BATCH, SEQ, 256)
    return pl.pallas_call(
        _selwin_kernel,
        grid=(BATCH, nq),
        in_specs=[pl.BlockSpec((tq, NSA_WIDTH), row)] + [pl.BlockSpec((1, SEQ, 256), kv)] * 4
                 + [pl.BlockSpec((tq, 256), row)],
        out_specs=[pl.BlockSpec((tq, NSA_WIDTH), row)] * 2,
        out_shape=[jax.ShapeDtypeStruct((N_TOK, NSA_WIDTH), BF16)] * 2,
        scratch_shapes=[pltpu.VMEM((4, tq, 1), F32), pltpu.VMEM((4, tq, 1), F32), pltpu.VMEM((2, tq, LANES), F32)],
        compiler_params=_cparams("arbitrary", "arbitrary"),
        name="selwin",
    )(q, r3(ks), r3(vs), r3(kw), r3(vw), sel)


def _s5_param_kernel(are_ref, aim_ref, ldt_ref, cre_ref, cim_ref, bre_ref, bim_ref,
                     clre_ref, clim_ref, wbre_ref, wbim_ref, bbre_ref, bbim_ref, ltre_ref, ltim_ref):
    are, aim = are_ref[...], aim_ref[...]
    dt = jnp.exp(ldt_ref[...])
    cre, cim = cre_ref[...], cim_ref[...]

    def lam_pow(tau):
        mag = jnp.exp(are * dt * float(tau))
        ang = aim * dt * float(tau)
        return mag * jnp.cos(ang), mag * jnp.sin(ang)

    lre, lim = lam_pow(1)
    den = are * are + aim * aim
    qre = ((lre - 1.0) * are + lim * aim) / den
    qim = (lim * are - (lre - 1.0) * aim) / den
    bre, bim = bre_ref[...], bim_ref[...]
    bbre = qre * bre - qim * bim
    bbim = qre * bim + qim * bre
    bbre_ref[...] = bbre
    bbim_ref[...] = bbim
    for tau in range(S5_T + 1):
        pr, pi = lam_pow(tau)
        clre_ref[tau] = cre * pr - cim * pi
        clim_ref[tau] = cre * pi + cim * pr
        if tau < S5_T:
            k = S5_T - 1 - tau
            wbre_ref[k] = pr * bbre - pi * bbim
            wbim_ref[k] = pr * bbim + pi * bbre
        else:
            ltre_ref[...] = pr
            ltim_ref[...] = pi


def _s5_kmat_kernel(l_ref, r_ref, o_ref):
    o_ref[0] = jnp.dot(l_ref[0], r_ref[0], preferred_element_type=F32, precision=HIGHEST)


def _s5_params(a_re, a_im, log_dt, b_re, b_im, c_re, c_im):
    T = S5_T
    pn = GROUP * STATE
    tile_p = lambda a: jnp.tile(a, (1, GROUP))
    args = (tile_p(a_re), tile_p(a_im), jnp.broadcast_to(log_dt[:, None], (N_GROUPS, pn)),
            c_re.reshape(N_GROUPS, pn), c_im.reshape(N_GROUPS, pn),
            jnp.swapaxes(b_re, 1, 2).reshape(N_GROUPS, pn), jnp.swapaxes(b_im, 1, 2).reshape(N_GROUPS, pn))
    full2 = pl.BlockSpec((N_GROUPS, pn), lambda: (0, 0))
    clre, clim, wbre, wbim, bbre, bbim, ltre, ltim = pl.pallas_call(
        _s5_param_kernel,
        in_specs=[full2] * 7,
        out_specs=[pl.BlockSpec((T + 1, N_GROUPS, pn), lambda: (0, 0, 0))] * 2
                  + [pl.BlockSpec((T, N_GROUPS, pn), lambda: (0, 0, 0))] * 2 + [full2] * 4,
        out_shape=[jax.ShapeDtypeStruct((T + 1, N_GROUPS, pn), F32)] * 2
                  + [jax.ShapeDtypeStruct((T, N_GROUPS, pn), F32)] * 2
                  + [jax.ShapeDtypeStruct((N_GROUPS, pn), F32)] * 4,
        name="s5_params",
    )(*args)

    r5 = lambda a, t: a[:t].reshape(t, N_GROUPS, GROUP, STATE)
    lhs = jnp.concatenate([r5(clre, T), -r5(clim, T)], axis=-1)
    lhs = jnp.transpose(lhs, (1, 0, 2, 3)).reshape(N_GROUPS, T * GROUP, 2 * STATE)
    bb = lambda a: jnp.swapaxes(a.reshape(N_GROUPS, GROUP, STATE), 1, 2)
    rhs = jnp.concatenate([bb(bbre), bb(bbim)], axis=1)
    kmat = pl.pallas_call(
        _s5_kmat_kernel,
        grid=(N_GROUPS,),
        in_specs=[pl.BlockSpec((1, T * GROUP, 2 * STATE), lambda g: (g, 0, 0)),
                  pl.BlockSpec((1, 2 * STATE, GROUP), lambda g: (g, 0, 0))],
        out_specs=pl.BlockSpec((1, T * GROUP, GROUP), lambda g: (g, 0, 0)),
        out_shape=jax.ShapeDtypeStruct((N_GROUPS, T * GROUP, GROUP), F32),
        compiler_params=_cparams("arbitrary"),
        name="s5_kmat",
    )(lhs, rhs)

    eye = jnp.eye(S5_GL, dtype=F32)
    kt = kmat.reshape(S5_SG, S5_GL, T, GROUP, GROUP)
    kbd = jnp.einsum('sgtpq,gh->stgqhp', kt, eye).reshape(S5_SG, T, LANES, LANES)
    krev = kbd[:, ::-1].reshape(S5_SG, T * LANES, LANES).astype(BF16)
    r6 = lambda a: a.reshape(T, S5_SG, S5_GL, GROUP, STATE)
    wb = jnp.stack([r6(wbre), r6(wbim)], axis=-2)
    wb = jnp.einsum('ksgpin,gh->skgpihn', wb, eye).reshape(S5_SG, T * LANES, S5_NSTATE).astype(BF16)
    wc = jnp.stack([r6(clre[1:]), -r6(clim[1:])], axis=-2)
    wc = jnp.einsum('tsgpin,gh->signthp', wc, eye).reshape(S5_SG, S5_NSTATE, T * LANES).astype(BF16)
    lt = lambda a: a.reshape(N_GROUPS, GROUP, STATE)[:, 0].reshape(S5_SG, 1, S5_GL * STATE)
    return krev, wb, wc, lt(ltre), lt(ltim)


S5_TC = 512


def _s5_state_kernel(x_ref, wb_ref, e_ref):
    e_ref[0] = _dot(x_ref[0], wb_ref[0])


def _s5_scan_kernel(e_ref, ltre_ref, ltim_ref, xs_ref):
    lr, li = ltre_ref[0], ltim_ref[0]
    half = S5_NSTATE // 2

    def step(c, carry):
        new = []
        for b in range(BATCH):
            xr, xi = carry[b]
            row = b * S5_CH_PER_BATCH + c
            xs_ref[0, pl.ds(row, 1), :half] = xr
            xs_ref[0, pl.ds(row, 1), half:] = xi
            e = e_ref[0, pl.ds(row, 1), :]
            new.append((lr * xr - li * xi + e[:, :half], lr * xi + li * xr + e[:, half:]))
        return tuple(new)

    zero = jnp.zeros((1, half), F32)
    lax.fori_loop(0, S5_CH_PER_BATCH, step, tuple((zero, zero) for _ in range(BATCH)))


def _s5_out_kernel(x_ref, xs_ref, krev_ref, wc_ref, y_ref):
    xsb = xs_ref[0].astype(BF16)
    for t in range(S5_T):
        cols = slice(t * LANES, (t + 1) * LANES)
        y_ref[0, :, cols] = (_dot(x_ref[0, :, :(t + 1) * LANES], krev_ref[0, (S5_T - 1 - t) * LANES:, :])
                             + _dot(xsb, wc_ref[0, :, cols]))


def _s5(u, krev, wb, wc, ltre, ltim):
    T, tc = S5_T, S5_TC
    xs = u.astype(BF16).reshape(S5_CH, T, S5_SG, LANES).transpose(2, 0, 1, 3).reshape(S5_SG, S5_CH, T * LANES)
    grid = (S5_SG, S5_CH // tc)
    rows = lambda s, i: (s, i, 0)
    per_sg = lambda s, i: (s, 0, 0)
    e = pl.pallas_call(
        _s5_state_kernel, grid=grid,
        in_specs=[pl.BlockSpec((1, tc, T * LANES), rows), pl.BlockSpec((1, T * LANES, S5_NSTATE), per_sg)],
        out_specs=pl.BlockSpec((1, tc, S5_NSTATE), rows),
        out_shape=jax.ShapeDtypeStruct((S5_SG, S5_CH, S5_NSTATE), F32),
        compiler_params=_cparams("arbitrary", "arbitrary"), name="s5_state",
    )(xs, wb)
    sg1 = lambda s: (s, 0, 0)
    xstart = pl.pallas_call(
        _s5_scan_kernel, grid=(S5_SG,),
        in_specs=[pl.BlockSpec((1, S5_CH, S5_NSTATE), sg1),
                  pl.BlockSpec((1, 1, S5_NSTATE // 2), sg1), pl.BlockSpec((1, 1, S5_NSTATE // 2), sg1)],
        out_specs=pl.BlockSpec((1, S5_CH, S5_NSTATE), sg1),
        out_shape=jax.ShapeDtypeStruct((S5_SG, S5_CH, S5_NSTATE), F32),
        compiler_params=_cparams("arbitrary"), name="s5_scan",
    )(e, ltre, ltim)
    y = pl.pallas_call(
        _s5_out_kernel, grid=grid,
        in_specs=[pl.BlockSpec((1, tc, T * LANES), rows), pl.BlockSpec((1, tc, S5_NSTATE), rows),
                  pl.BlockSpec((1, T * LANES, LANES), per_sg), pl.BlockSpec((1, S5_NSTATE, T * LANES), per_sg)],
        out_specs=pl.BlockSpec((1, tc, T * LANES), rows),
        out_shape=jax.ShapeDtypeStruct((S5_SG, S5_CH, T * LANES), F32),
        compiler_params=_cparams("arbitrary", "arbitrary"), name="s5_out",
    )(xs, xstart, krev, wc)
    return y.reshape(S5_SG, S5_CH, T, LANES).transpose(1, 2, 0, 3).reshape(N_TOK, SSM_WIDTH)


MERGE_TM = 256


def _merge_kernel(ocmp_ref, osel_ref, owin_ref, gn_ref, yssm_ref, u_ref, ga_ref, gs_ref, x_ref, mod_ref,
                  eg_ref, dskip_ref, wglu_ref, bglu_ref, wua_ref, wus_ref, wout_ref, g2_ref,
                  wrhi_ref, wrlo_ref, wsgu_ref, wsd_ref,
                  xpart_ref, h2_ref, logit_ref):
    mod = mod_ref[0]
    gnb = gn_ref[...].astype(BF16)
    o_nsa = (_dot(gnb, eg_ref[0]) * ocmp_ref[...].astype(F32)
             + _dot(gnb, eg_ref[1]) * osel_ref[...].astype(F32)
             + _dot(gnb, eg_ref[2]) * owin_ref[...].astype(F32))
    attn = _dot(o_nsa.astype(BF16), wua_ref[...])
    z = _gelu(yssm_ref[...] + dskip_ref[...] * u_ref[...])
    y_ssm = z * jax.nn.sigmoid(_dot(z.astype(BF16), wglu_ref[...]) + bglu_ref[...])
    ssm = _dot(y_ssm.astype(BF16), wus_ref[...])
    merged = ga_ref[...].astype(F32) * attn + gs_ref[...].astype(F32) * ssm
    x1 = x_ref[...] + mod[2:3] * _dot(merged.astype(BF16), wout_ref[...])

    ms = jnp.mean(x1 * x1, axis=-1, keepdims=True)
    h2 = (x1 * lax.rsqrt(ms + EPS) * g2_ref[...]) * (1.0 + mod[4:5]) + mod[3:4]
    h2_ref[...] = h2
    hi = h2.astype(BF16)
    lo = (h2 - hi.astype(F32)).astype(BF16)
    logit_ref[...] = _dot_nt(wrhi_ref[...], hi) + _dot_nt(wrhi_ref[...], lo) + _dot_nt(wrlo_ref[...], hi)
    gu = _dot(hi, wsgu_ref[...])
    shared = _dot((_silu(gu[:, :D_EXPERT]) * gu[:, D_EXPERT:]).astype(BF16), wsd_ref[...])
    xpart_ref[...] = x1 + mod[5:6] * shared


def _merge(ocmp, osel, owin, gn, yssm, u, ga, gs, x2, mod, d_skip, w_glu, b_glu, w_up_attn, w_up_ssm, w_out,
           g_norm2, w_router, ws_gate, ws_up, ws_down):
    tm = MERGE_TM
    eg = np.zeros((3, LANES, NSA_WIDTH), np.float32)
    for j in range(3):
        for h in range(N_HEADS):
            eg[j, 3 * h + j, h * HEAD_DIM:(h + 1) * HEAD_DIM] = 1.0
    wr_t = w_router.T
    wr_hi = wr_t.astype(BF16)
    wr_lo = (wr_t - wr_hi.astype(F32)).astype(BF16)
    row = lambda i: (i, 0)
    fix2 = lambda i: (0, 0)
    wspec = lambda a: pl.BlockSpec(a.shape, (lambda i: (0,) * a.ndim))
    weights = [jnp.asarray(eg, BF16), d_skip.reshape(1, -1), w_glu.astype(BF16), b_glu.reshape(1, -1),
               w_up_attn.astype(BF16), w_up_ssm.astype(BF16), w_out.astype(BF16), g_norm2.reshape(1, -1),
               wr_hi, wr_lo, jnp.concatenate([ws_gate, ws_up], axis=1).astype(BF16), ws_down.astype(BF16)]
    acts = [(ocmp, 512), (osel, 512), (owin, 512), (gn, 128), (yssm, 512), (u, 512), (ga, 1024), (gs, 1024),
            (x2, 1024)]
    return pl.pallas_call(
        _merge_kernel,
        grid=(N_TOK // tm,),
        in_specs=[pl.BlockSpec((tm, wd), row) for _, wd in acts]
                 + [pl.BlockSpec((1, 6, D_MODEL), lambda i: (i // (SEQ // tm), 0, 0))]
                 + [wspec(w) for w in weights],
        out_specs=[pl.BlockSpec((tm, D_MODEL), row), pl.BlockSpec((tm, D_MODEL), row),
                   pl.BlockSpec((N_EXPERTS, tm), lambda i: (0, i))],
        out_shape=[jax.ShapeDtypeStruct((N_TOK, D_MODEL), F32), jax.ShapeDtypeStruct((N_TOK, D_MODEL), F32),
                   jax.ShapeDtypeStruct((N_EXPERTS, N_TOK), F32)],
        compiler_params=_cparams("arbitrary"),
        name="merge",
    )(*[a for a, _ in acts], mod, *weights)


ROUTE_TN = 512


def _route_kernel(logit_ref, bias_ref, eidx_ref, w_ref, gscore_ref, masked_ref):
    tn = ROUTE_TN
    sc = jax.nn.sigmoid(logit_ref[...])
    biased = sc + bias_ref[...]
    gi = lax.broadcasted_iota(I32, (EXPERTS_PER_GROUP, tn), 0).astype(F32)
    for g in range(N_EXPERT_GROUPS):
        blk = biased[g * EXPERTS_PER_GROUP:(g + 1) * EXPERTS_PER_GROUP]
        m1 = jnp.max(blk, axis=0, keepdims=True)
        i1 = jnp.min(jnp.where(blk == m1, gi, float(EXPERTS_PER_GROUP)), axis=0, keepdims=True)
        m2 = jnp.max(jnp.where(gi == i1, -jnp.inf, blk), axis=0, keepdims=True)
        gscore_ref[g:g + 1, :] = m1 + m2
    gs = gscore_ref[...]
    gidx = lax.broadcasted_iota(I32, (N_EXPERT_GROUPS, tn), 0)
    grank = jnp.zeros((N_EXPERT_GROUPS, tn), F32)
    for gp in range(N_EXPERT_GROUPS):
        row = gs[gp:gp + 1, :]
        tie = jnp.where(gidx > gp, 1.0, 0.0)
        grank = grank + jnp.where(row > gs, 1.0, jnp.where(row == gs, tie, 0.0))
    for g in range(N_EXPERT_GROUPS):
        keep = grank[g:g + 1, :] < float(TOPK_GROUPS)
        sl = slice(g * EXPERTS_PER_GROUP, (g + 1) * EXPERTS_PER_GROUP)
        masked_ref[sl, :] = jnp.where(keep, biased[sl], -jnp.inf)
    cur = masked_ref[...]
    eidx = lax.broadcasted_iota(I32, (N_EXPERTS, tn), 0).astype(F32)
    wsum = jnp.zeros((1, tn), F32)
    for k in range(TOP_K):
        m = jnp.max(cur, axis=0, keepdims=True)
        idx = jnp.min(jnp.where(cur == m, eidx, float(N_EXPERTS)), axis=0, keepdims=True)
        hit = eidx == idx
        wk = jnp.sum(jnp.where(hit, sc, 0.0), axis=0, keepdims=True)
        cur = jnp.where(hit, -jnp.inf, cur)
        eidx_ref[k:k + 1, :] = idx.astype(I32)
        w_ref[k:k + 1, :] = wk
        wsum = wsum + wk
    w_ref[...] = w_ref[...] / wsum * ROUTE_SCALE


def _route(logits_t, router_bias):
    tn = ROUTE_TN
    return pl.pallas_call(
        _route_kernel,
        grid=(N_TOK // tn,),
        in_specs=[pl.BlockSpec((N_EXPERTS, tn), lambda i: (0, i)), pl.BlockSpec((N_EXPERTS, 1), lambda i: (0, 0))],
        out_specs=[pl.BlockSpec((TOP_K, tn), lambda i: (0, i))] * 2,
        out_shape=[jax.ShapeDtypeStruct((TOP_K, N_TOK), I32), jax.ShapeDtypeStruct((TOP_K, N_TOK), F32)],
        scratch_shapes=[pltpu.VMEM((N_EXPERT_GROUPS, tn), F32), pltpu.VMEM((N_EXPERTS, tn), F32)],
        compiler_params=_cparams("arbitrary"),
        name="route",
    )(logits_t, router_bias.reshape(-1, 1))


def _dispatch_plan(eidx, w):
    e_flat = eidx.reshape(-1)
    order = jnp.argsort(e_flat)
    se = e_flat[order]
    counts = jnp.zeros((N_EXPERTS,), I32).at[e_flat].add(1)
    start = jnp.cumsum(counts) - counts
    padded = (counts + DISPATCH_BLOCK - 1) // DISPATCH_BLOCK * DISPATCH_BLOCK
    pad_end = jnp.cumsum(padded)
    pad_start = pad_end - padded
    dest = pad_start[se] + jnp.arange(NK, dtype=I32) - start[se]
    pos = jnp.zeros((NK,), I32).at[order].set(dest)
    tok_flat = jnp.repeat(jnp.arange(N_TOK, dtype=I32), TOP_K)
    buf_tok = jnp.zeros((CAP,), I32).at[pos].set(tok_flat)
    buf_w = jnp.zeros((CAP,), F32).at[pos].set(w.reshape(-1))
    blk_e = jnp.minimum(jnp.searchsorted(pad_end, jnp.arange(N_BLK, dtype=I32) * DISPATCH_BLOCK, side='right'),
                        N_EXPERTS - 1).astype(I32)
    n_active = (pad_end[-1] // DISPATCH_BLOCK).astype(I32).reshape(1)
    return buf_tok, buf_w, blk_e, n_active, pos


def _gather_rows(src_hbm, dst, sem, idx_ref, n):
    for i in range(n):
        pltpu.make_async_copy(src_hbm.at[pl.ds(idx_ref[0, 0, i], 1)], dst.at[pl.ds(i, 1)], sem).start()


def _wait_rows(src_hbm, dst, sem, n):
    for i in range(n):
        pltpu.make_async_copy(src_hbm.at[pl.ds(0, 1)], dst.at[pl.ds(i, 1)], sem).wait()


def _moe_kernel(blk_e_ref, nact_ref, tok_ref, tokn_ref, w_ref, h2_hbm, wg_ref, wu_ref, wd_ref, y_ref, xbuf, sem):
    b = pl.program_id(0)
    nact = nact_ref[0]
    slot = b % 2
    rows = DISPATCH_BLOCK

    @pl.when(b == 0)
    def _():
        _gather_rows(h2_hbm, xbuf.at[0], sem.at[0], tok_ref, rows)

    @pl.when(b + 1 < nact)
    def _():
        _gather_rows(h2_hbm, xbuf.at[1 - slot], sem.at[1 - slot], tokn_ref, rows)

    @pl.when(b < nact)
    def _():
        _wait_rows(h2_hbm, xbuf.at[slot], sem.at[slot], rows)
        x = xbuf[slot].astype(BF16)
        gate = _dot(x, wg_ref[0].astype(BF16))
        up = _dot(x, wu_ref[0].astype(BF16))
        y = _dot((_silu(gate) * up).astype(BF16), wd_ref[0].astype(BF16))
        y_ref[...] = y * w_ref[...]

    @pl.when(b >= nact)
    def _():
        y_ref[...] = jnp.zeros(y_ref.shape, F32)


def _moe(h2, buf_tok, buf_w, blk_e, n_active, w_gate, w_up, w_down):
    tok3 = buf_tok.reshape(N_BLK, 1, DISPATCH_BLOCK)
    last = lambda b, be, na: jnp.minimum(b, na[0] - 1)
    smem_blk = lambda fn: pl.BlockSpec((1, 1, DISPATCH_BLOCK), fn, memory_space=pltpu.SMEM)
    grid_spec = pltpu.PrefetchScalarGridSpec(
        num_scalar_prefetch=2,
        grid=(N_BLK,),
        in_specs=[smem_blk(lambda b, be, na: (last(b, be, na), 0, 0)),
                  smem_blk(lambda b, be, na: (jnp.minimum(b + 1, na[0] - 1), 0, 0)),
                  pl.BlockSpec((DISPATCH_BLOCK, 1), lambda b, be, na: (last(b, be, na), 0)),
                  pl.BlockSpec(memory_space=pl.ANY),
                  pl.BlockSpec((1, D_MODEL, D_EXPERT), lambda b, be, na: (be[last(b, be, na)], 0, 0)),
                  pl.BlockSpec((1, D_MODEL, D_EXPERT), lambda b, be, na: (be[last(b, be, na)], 0, 0)),
                  pl.BlockSpec((1, D_EXPERT, D_MODEL), lambda b, be, na: (be[last(b, be, na)], 0, 0))],
        out_specs=pl.BlockSpec((DISPATCH_BLOCK, D_MODEL), lambda b, be, na: (b, 0)),
        scratch_shapes=[pltpu.VMEM((2, DISPATCH_BLOCK, D_MODEL), F32), pltpu.SemaphoreType.DMA((2,))],
    )
    return pl.pallas_call(
        _moe_kernel,
        grid_spec=grid_spec,
        out_shape=jax.ShapeDtypeStruct((CAP, D_MODEL), F32),
        compiler_params=_cparams("arbitrary"),
        name="moe",
    )(blk_e, n_active, tok3, tok3, buf_w.reshape(CAP, 1), h2, w_gate, w_up, w_down)


COMB_TC = 16


def _combine_kernel(pos_ref, posn_ref, xpart_ref, mod_ref, y_hbm, out_ref, ybuf, sem):
    i = pl.program_id(0)
    n = pl.num_programs(0)
    slot = i % 2
    rows = COMB_TC * TOP_K

    @pl.when(i == 0)
    def _():
        _gather_rows(y_hbm, ybuf.at[0], sem.at[0], pos_ref, rows)

    @pl.when(i + 1 < n)
    def _():
        _gather_rows(y_hbm, ybuf.at[1 - slot], sem.at[1 - slot], posn_ref, rows)

    _wait_rows(y_hbm, ybuf.at[slot], sem.at[slot], rows)
    routed = ybuf[slot, 0:COMB_TC]
    for k in range(1, TOP_K):
        routed = routed + ybuf[slot, k * COMB_TC:(k + 1) * COMB_TC]
    out_ref[...] = xpart_ref[...] + mod_ref[0][5:6] * routed


def _combine(xpart, mod, y, pos):
    tc = COMB_TC
    nsteps = N_TOK // tc
    pos3 = pos.reshape(nsteps, tc, TOP_K).transpose(0, 2, 1).reshape(nsteps, 1, tc * TOP_K)
    smem_blk = lambda fn: pl.BlockSpec((1, 1, tc * TOP_K), fn, memory_space=pltpu.SMEM)
    return pl.pallas_call(
        _combine_kernel,
        grid=(nsteps,),
        in_specs=[smem_blk(lambda i: (i, 0, 0)),
                  smem_blk(lambda i: (jnp.minimum(i + 1, nsteps - 1), 0, 0)),
                  pl.BlockSpec((tc, D_MODEL), lambda i: (i, 0)),
                  pl.BlockSpec((1, 6, D_MODEL), lambda i: (i // (SEQ // tc), 0, 0)),
                  pl.BlockSpec(memory_space=pl.ANY)],
        out_specs=pl.BlockSpec((tc, D_MODEL), lambda i: (i, 0)),
        out_shape=jax.ShapeDtypeStruct((N_TOK, D_MODEL), F32),
        scratch_shapes=[pltpu.VMEM((2, tc * TOP_K, D_MODEL), F32), pltpu.SemaphoreType.DMA((2,))],
        compiler_params=_cparams("arbitrary"),
        name="combine",
    )(pos3, pos3, xpart, mod, y)


def _layer(x, c, w_ada, b_ada, g_norm1, g_norm2, w_in, q_gain, kc_gain, ks_gain, kw_gain,
           pe_k, pe_v, w_cmp_k1, w_cmp_k2, w_cmp_v1, w_cmp_v2,
           a_re, a_im, log_dt, b_re, b_im, c_re, c_im, d_skip, w_glu, b_glu,
           w_up_attn, w_up_ssm, w_out, w_router, router_bias,
           w_gate, w_up, w_down, ws_gate, ws_up, ws_down):
    x2 = x.reshape(N_TOK, D_MODEL)
    mod = _ada(c, w_ada, b_ada)
    q, kc_raw, vc_raw, ks, vs, kw, vw, gn, u, ga, gs = _proj(x2, mod, g_norm1, w_in, q_gain, ks_gain, kw_gain)
    kcn = _compress(kc_raw, pe_k, w_cmp_k1, w_cmp_k2, kc_gain, True)
    vcn = _compress(vc_raw, pe_v, w_cmp_v1, w_cmp_v2, kc_gain, False)
    ocmp, sel = _cmp_attn(q, kcn, vcn)
    osel, owin = _selwin(q, ks, vs, kw, vw, sel)
    yssm = _s5(u, *_s5_params(a_re, a_im, log_dt, b_re, b_im, c_re, c_im))
    xpart, h2, logits_t = _merge(ocmp, osel, owin, gn, yssm, u, ga, gs, x2, mod, d_skip, w_glu, b_glu,
                                 w_up_attn, w_up_ssm, w_out, g_norm2, w_router, ws_gate, ws_up, ws_down)
    eidx_t, w_t = _route(logits_t, router_bias)
    buf_tok, buf_w, blk_e, n_active, pos = _dispatch_plan(eidx_t.T, w_t.T)
    y = _moe(h2, buf_tok, buf_w, blk_e, n_active, w_gate, w_up, w_down)
    return _combine(xpart, mod, y, pos).reshape(BATCH, SEQ, D_MODEL)


def kernel(x, c, w_ada, b_ada, g_norm1, g_norm2, w_in, q_gain, kc_gain, ks_gain, kw_gain, pe_k, pe_v, w_cmp_k1,
           w_cmp_k2, w_cmp_v1, w_cmp_v2, a_re, a_im, log_dt, b_re, b_im, c_re, c_im, d_skip, w_glu, b_glu,
           w_up_attn, w_up_ssm, w_out, w_router, router_bias, w_gate, w_up, w_down, ws_gate, ws_up, ws_down):
    params = (w_ada, b_ada, g_norm1, g_norm2, w_in, q_gain, kc_gain, ks_gain, kw_gain, pe_k, pe_v, w_cmp_k1,
              w_cmp_k2, w_cmp_v1, w_cmp_v2, a_re, a_im, log_dt, b_re, b_im, c_re, c_im, d_skip, w_glu, b_glu,
              w_up_attn, w_up_ssm, w_out, w_router, router_bias, w_gate, w_up, w_down, ws_gate, ws_up, ws_down)
    depth = w_ada.shape[0]
    for layer in range(depth):
        x = _layer(x, c, *[p[layer] for p in params])
    return x
```

```python
import functools
import math

import jax
import jax.numpy as jnp
import numpy as np
from jax import lax
from jax.experimental import pallas as pl
from jax.experimental.pallas import tpu as pltpu

F32 = jnp.float32
BF16 = jnp.bfloat16
I32 = jnp.int32
HIGHEST = lax.Precision.HIGHEST

D_MODEL = 1024
BATCH = 4
SEQ = 4096
N_TOK = BATCH * SEQ
N_HEADS = 8
HEAD_DIM = 64
N_KV = 2
CMP_BLOCK = 32
CMP_STRIDE = 16
CMP_HIDDEN = 256
N_CMP = 256
SEL_BLOCK = 64
N_SEL_BLOCKS = SEQ // SEL_BLOCK
N_SELECT = 16
WINDOW = 512
ATTN_SCALE = HEAD_DIM ** -0.5
NSA_WIDTH = N_HEADS * HEAD_DIM
SSM_WIDTH = 512
GROUP = 16
N_GROUPS = SSM_WIDTH // GROUP
STATE = 64
N_EXPERTS = 256
TOP_K = 8
D_EXPERT = 256
N_EXPERT_GROUPS = 8
EXPERTS_PER_GROUP = N_EXPERTS // N_EXPERT_GROUPS
TOPK_GROUPS = 4
ROUTE_SCALE = 2.5
DISPATCH_BLOCK = 128
EPS = 1e-6
NEG = -1e30

LANES = 128
S5_T = 16
S5_SG = 4
S5_GL = N_GROUPS // S5_SG
S5_CH = N_TOK // S5_T
S5_CH_PER_BATCH = SEQ // S5_T
S5_NSTATE = S5_GL * STATE * 2

NK = N_TOK * TOP_K
CAP = (NK + N_EXPERTS * DISPATCH_BLOCK + DISPATCH_BLOCK - 1) // DISPATCH_BLOCK * DISPATCH_BLOCK
N_BLK = CAP // DISPATCH_BLOCK

VMEM_LIMIT = 48 * 1024 * 1024


def _cparams(*sem):
    return pltpu.CompilerParams(dimension_semantics=tuple(sem), vmem_limit_bytes=VMEM_LIMIT)


def _dot(a, b):
    return jnp.dot(a, b, preferred_element_type=F32)


def _dot_nt(a, b):
    return lax.dot_general(a, b, (((1,), (1,)), ((), ())), preferred_element_type=F32)


def _split_dot(v, w):
    hi = v.astype(BF16)
    lo = (v - hi.astype(F32)).astype(BF16)
    return _dot(hi, w) + _dot(lo, w)


def _seg_rms(v, bd, gain):
    ss = _split_dot(v * v, bd)
    return v * lax.rsqrt(ss * (1.0 / HEAD_DIM) + EPS) * gain


def _gelu(x):
    return 0.5 * x * (1.0 + jnp.tanh(0.7978845608028654 * (x + 0.044715 * (x * x * x))))


def _silu(x):
    return x * jax.nn.sigmoid(x)


def _ada_kernel(c_ref, w_ref, b_ref, o_ref):
    c = c_ref[...]
    o_ref[...] = jnp.dot(_silu(c), w_ref[...], preferred_element_type=F32, precision=HIGHEST) + b_ref[...]


def _ada(c, w_ada, b_ada):
    cp = jnp.pad(c, ((0, 8 - BATCH), (0, 0)))
    tn = 1536
    out = pl.pallas_call(
        _ada_kernel,
        grid=(6 * D_MODEL // tn,),
        in_specs=[pl.BlockSpec((8, D_MODEL), lambda j: (0, 0)),
                  pl.BlockSpec((D_MODEL, tn), lambda j: (0, j)),
                  pl.BlockSpec((1, tn), lambda j: (0, j))],
        out_specs=pl.BlockSpec((8, tn), lambda j: (0, j)),
        out_shape=jax.ShapeDtypeStruct((8, 6 * D_MODEL), F32),
        compiler_params=_cparams("arbitrary"),
        name="ada",
    )(cp, w_ada, b_ada.reshape(1, -1))
    return out.reshape(8, 6, D_MODEL)


_C_Q = 0
_C_KC = 512
_C_VC = 640
_C_KS = 768
_C_KW = 1024
_C_GN = 1280
_C_U = 1408
_C_GA = 1920
_C_GS = 2944
_C_END = 3968
PROJ_TM = 512


def _proj_kernel(x_ref, mod_ref, g1_ref, w_ref, wvt_ref, qg_ref, ksg_ref, kwg_ref, bd512_ref, bd256_ref,
                 q_ref, kc_ref, vc_ref, ks_ref, kw_ref, vst_ref, vwt_ref, gn_ref, u_ref, ga_ref, gs_ref):
    x = x_ref[...]
    ms = jnp.mean(x * x, axis=-1, keepdims=True)
    mod = mod_ref[0]
    h = (x * lax.rsqrt(ms + EPS) * g1_ref[...]) * (1.0 + mod[1:2]) + mod[0:1]
    hb = h.astype(BF16)

    def p(lo, hi):
        return _dot(hb, w_ref[:, lo:hi])

    q_ref[...] = _seg_rms(p(_C_Q, _C_KC), bd512_ref[...], qg_ref[...] * ATTN_SCALE).astype(BF16)
    kc_ref[...] = p(_C_KC, _C_VC).astype(BF16)
    vc_ref[...] = p(_C_VC, _C_KS).astype(BF16)
    ks_ref[...] = _seg_rms(p(_C_KS, _C_KW), bd256_ref[...], ksg_ref[...]).astype(BF16)
    kw_ref[...] = _seg_rms(p(_C_KW, _C_GN), bd256_ref[...], kwg_ref[...]).astype(BF16)
    vt = _dot_nt(wvt_ref[...], hb)
    vst_ref[...] = vt[:LANES].astype(BF16)
    vwt_ref[...] = vt[LANES:].astype(BF16)
    gn_ref[...] = jax.nn.sigmoid(p(_C_GN, _C_U))
    u_ref[...] = p(_C_U, _C_GA)
    ga_ref[...] = jax.nn.sigmoid(p(_C_GA, _C_GS)).astype(BF16)
    gs_ref[...] = jax.nn.sigmoid(p(_C_GS, _C_END)).astype(BF16)


def _dup_cols(w):
    return jnp.concatenate([w[:, :64], w[:, :64], w[:, 64:], w[:, 64:]], axis=1)


def _block_ones(n):
    return jnp.kron(jnp.eye(n // HEAD_DIM, dtype=F32), jnp.ones((HEAD_DIM, HEAD_DIM), F32)).astype(BF16)


def _proj(x2, mod, g_norm1, w_in, q_gain, ks_gain, kw_gain):
    o = np.cumsum((0, 512, 128, 128, 128, 128, 128, 128, 24, 512, 1024, 1024))
    parts = [w_in[:, o[i]:o[i + 1]] for i in range(11)]
    wq, wkc, wvc, wks, wvs, wkw, wvw, wgn, wu, wga, wgs = parts
    w = jnp.concatenate([wq, wkc, wvc, _dup_cols(wks), _dup_cols(wkw),
                         jnp.pad(wgn, ((0, 0), (0, LANES - 24))), wu, wga, wgs], axis=1).astype(BF16)
    wvt = jnp.concatenate([wvs, wvw], axis=1).T.astype(BF16)
    tm = PROJ_TM
    row = lambda i: (i, 0)
    col = lambda i: (0, i)
    fix = lambda i: (0, 0)
    outs = [(512, BF16, row), (128, BF16, row), (128, BF16, row), (256, BF16, row), (256, BF16, row),
            (LANES, BF16, col), (LANES, BF16, col),
            (128, F32, row), (512, F32, row), (1024, BF16, row), (1024, BF16, row)]
    ospec = lambda wd, m: pl.BlockSpec((tm, wd), m) if m is row else pl.BlockSpec((wd, tm), m)
    oshape = lambda wd, dt, m: jax.ShapeDtypeStruct((N_TOK, wd) if m is row else (wd, N_TOK), dt)
    return pl.pallas_call(
        _proj_kernel,
        grid=(N_TOK // tm,),
        in_specs=[pl.BlockSpec((tm, D_MODEL), row),
                  pl.BlockSpec((1, 6, D_MODEL), lambda i: (i // (SEQ // tm), 0, 0)),
                  pl.BlockSpec((1, D_MODEL), fix),
                  pl.BlockSpec((D_MODEL, _C_END), fix),
                  pl.BlockSpec((2 * LANES, D_MODEL), fix),
                  pl.BlockSpec((1, 512), fix), pl.BlockSpec((1, 256), fix), pl.BlockSpec((1, 256), fix),
                  pl.BlockSpec((512, 512), fix), pl.BlockSpec((256, 256), fix)],
        out_specs=[ospec(wd, m) for wd, _, m in outs],
        out_shape=[oshape(wd, dt, m) for wd, dt, m in outs],
        compiler_params=_cparams("arbitrary"),
        name="proj",
    )(x2, mod, g_norm1.reshape(1, -1), w, wvt,
      jnp.tile(q_gain, N_HEADS).reshape(1, -1), jnp.tile(ks_gain, 4).reshape(1, -1),
      jnp.tile(kw_gain, 4).reshape(1, -1), _block_ones(512), _block_ones(256))


def _compress_kernel(r_ref, pe_ref, w1_ref, w2_ref, bd_ref, gain_ref, o_ref, *, do_norm):
    r = r_ref[0].astype(F32)
    p0 = _dot((r + pe_ref[0]).astype(BF16), w1_ref[0])
    p1 = _dot((r + pe_ref[1]).astype(BF16), w1_ref[1])
    hid = p0 + pltpu.roll(p1, N_CMP - 1, 0)
    c = _dot(_gelu(hid).astype(BF16), w2_ref[...])
    if do_norm:
        c = _seg_rms(c, bd_ref[...], gain_ref[...])
    o_ref[0] = c.astype(BF16)


def _compress(raw, pe, w1, w2, gain, do_norm):
    r = raw.reshape(BATCH, SEQ // CMP_STRIDE, CMP_STRIDE * LANES)
    eye = jnp.eye(N_KV, dtype=F32)
    w1r = w1.reshape(2, CMP_STRIDE, HEAD_DIM, CMP_HIDDEN)
    w1big = jnp.einsum('hldc,gk->hlgdkc', w1r, eye).reshape(2, CMP_STRIDE * LANES, N_KV * CMP_HIDDEN).astype(BF16)
    w2big = jnp.einsum('cd,gk->gckd', w2, eye)
    w2big = jnp.concatenate([w2big, w2big], axis=-1).reshape(N_KV * CMP_HIDDEN, 4 * HEAD_DIM).astype(BF16)
    pe_big = jnp.broadcast_to(pe.reshape(2, CMP_STRIDE, 1, HEAD_DIM), (2, CMP_STRIDE, N_KV, HEAD_DIM))
    pe_big = pe_big.reshape(2, 1, CMP_STRIDE * LANES)
    fix2 = lambda b: (0, 0)
    fix3 = lambda b: (0, 0, 0)
    return pl.pallas_call(
        functools.partial(_compress_kernel, do_norm=do_norm),
        grid=(BATCH,),
        in_specs=[pl.BlockSpec((1, N_CMP, CMP_STRIDE * LANES), lambda b: (b, 0, 0)),
                  pl.BlockSpec((2, 1, CMP_STRIDE * LANES), fix3),
                  pl.BlockSpec((2, CMP_STRIDE * LANES, N_KV * CMP_HIDDEN), fix3),
                  pl.BlockSpec((N_KV * CMP_HIDDEN, 256), fix2),
                  pl.BlockSpec((256, 256), fix2), pl.BlockSpec((1, 256), fix2)],
        out_specs=pl.BlockSpec((1, N_CMP, 256), lambda b: (b, 0, 0)),
        out_shape=jax.ShapeDtypeStruct((BATCH, N_CMP, 256), BF16),
        compiler_params=_cparams("arbitrary"),
        name="compress_k" if do_norm else "compress_v",
    )(r, pe_big, w1big, w2big, _block_ones(256), jnp.tile(gain, 4).reshape(1, -1))


ATT_TQ = 256


def _head_variants(qb):
    lane = lax.broadcasted_iota(I32, qb.shape, 1)
    z = jnp.zeros_like(qb)
    return jnp.where(lane < HEAD_DIM, qb, z), jnp.where(lane < HEAD_DIM, z, qb)


def _cmp_kernel(q_ref, kc_ref, vc_ref, ov_ref, o_ref, sel_ref):
    tq = ATT_TQ
    qi = pl.program_id(1)
    tpos = qi * tq + lax.broadcasted_iota(I32, (tq, N_CMP), 0)
    nidx = lax.broadcasted_iota(I32, (tq, N_CMP), 1)
    mask = (CMP_STRIDE * nidx + (CMP_BLOCK - 1)) <= tpos
    lane_lo = lax.broadcasted_iota(I32, (tq, LANES), 1) < HEAD_DIM
    for g in range(N_KV):
        kd = kc_ref[0, :, g * LANES:(g + 1) * LANES]
        vd = vc_ref[0, :, g * LANES:(g + 1) * LANES]
        psum = jnp.zeros((tq, N_CMP), F32)
        for jb in range(2):
            blk = 2 * g + jb
            pv = []
            for qv in _head_variants(q_ref[:, blk * LANES:(blk + 1) * LANES]):
                s = jnp.where(mask, _dot_nt(qv, kd), NEG)
                m = jnp.max(s, axis=-1, keepdims=True)
                e = jnp.where(mask, jnp.exp(s - m), 0.0)
                l = jnp.sum(e, axis=-1, keepdims=True)
                p = e / jnp.where(l > 0.0, l, 1.0)
                psum = psum + p
                pv.append(_dot(p.astype(BF16), vd))
            o_ref[:, blk * LANES:(blk + 1) * LANES] = jnp.where(lane_lo, pv[0], pv[1]).astype(BF16)
        imp = _split_dot(psum, ov_ref[...])
        imp_t = imp.T[:N_SEL_BLOCKS]
        j = lax.broadcasted_iota(I32, (N_SEL_BLOCKS, tq), 0)
        cur = jnp.right_shift(qi * tq + lax.broadcasted_iota(I32, (N_SEL_BLOCKS, tq), 1), 6)
        forced = (j == 0) | (j == cur) | (j == cur - 1)
        v = jnp.where(forced, jnp.inf, jnp.where(j <= cur, imp_t, -jnp.inf))
        rank = jnp.zeros((N_SEL_BLOCKS, tq), F32)
        for jp in range(N_SEL_BLOCKS):
            row = v[jp:jp + 1, :]
            tie = jnp.where(j > jp, 1.0, 0.0)
            rank = rank + jnp.where(row > v, 1.0, jnp.where(row == v, tie, 0.0))
        sel_ref[g * N_SEL_BLOCKS:(g + 1) * N_SEL_BLOCKS, :] = jnp.where(rank < float(N_SELECT), 0.0, NEG)


def _cmp_attn(q, kcn, vcn):
    nc = np.arange(N_CMP)
    sb = np.arange(LANES)
    ov = ((CMP_STRIDE * nc[:, None] < SEL_BLOCK * sb[None, :] + SEL_BLOCK)
          & (CMP_STRIDE * nc[:, None] + CMP_BLOCK > SEL_BLOCK * sb[None, :])
          & (nc[:, None] < N_CMP - 1) & (sb[None, :] < N_SEL_BLOCKS))
    ov = jnp.asarray(ov, BF16)
    tq = ATT_TQ
    nq = SEQ // tq
    row = lambda b, i: (b * nq + i, 0)
    return pl.pallas_call(
        _cmp_kernel,
        grid=(BATCH, nq),
        in_specs=[pl.BlockSpec((tq, NSA_WIDTH), row),
                  pl.BlockSpec((1, N_CMP, 256), lambda b, i: (b, 0, 0)),
                  pl.BlockSpec((1, N_CMP, 256), lambda b, i: (b, 0, 0)),
                  pl.BlockSpec((N_CMP, LANES), lambda b, i: (0, 0))],
        out_specs=[pl.BlockSpec((tq, NSA_WIDTH), row),
                   pl.BlockSpec((N_KV * N_SEL_BLOCKS, tq), lambda b, i: (0, b * nq + i))],
        out_shape=[jax.ShapeDtypeStruct((N_TOK, NSA_WIDTH), BF16),
                   jax.ShapeDtypeStruct((N_KV * N_SEL_BLOCKS, N_TOK), F32)],
        compiler_params=_cparams("arbitrary", "arbitrary"),
        name="cmp_attn",
    )(q, kcn, vcn, ov)


ATT_TK = 256


M_INIT = -1e29


def _selwin_kernel(q_ref, ks_ref, kw_ref, vst_ref, vwt_ref, selb_ref, osel_ref, owin_ref, m_ref, l_ref, acc_ref):
    tq, tk = ATT_TQ, ATT_TK
    qi = pl.program_id(1)
    krow = lax.broadcasted_iota(I32, (tk, tq), 0)
    qcol = lax.broadcasted_iota(I32, (tk, tq), 1)
    causal_bias = jnp.where(krow <= qcol, 0.0, NEG)
    far_bias = jnp.where(qcol < krow, 0.0, NEG)

    def reset():
        m_ref[...] = jnp.full(m_ref.shape, M_INIT, F32)
        l_ref[...] = jnp.zeros(l_ref.shape, F32)
        acc_ref[...] = jnp.zeros(acc_ref.shape, F32)

    def update(qvars, kd, vt, bias):
        for h in range(4):
            s = _dot_nt(kd, qvars[h])
            if bias is not None:
                s = s + bias
            m_old = m_ref[h]
            m_new = jnp.maximum(m_old, jnp.max(s, axis=0, keepdims=True))
            alpha = jnp.exp(m_old - m_new)
            p = jnp.exp(s - m_new)
            l_ref[h] = alpha * l_ref[h] + jnp.sum(p, axis=0, keepdims=True)
            m_ref[h] = m_new
            acc_ref[h] = alpha * acc_ref[h] + _dot(vt, p.astype(BF16))

    def finish(out_ref, g):
        for jb in range(2):
            blk = 2 * g + jb
            pair = jnp.concatenate([acc_ref[2 * jb] / l_ref[2 * jb], acc_ref[2 * jb + 1] / l_ref[2 * jb + 1]], axis=0)
            out_ref[:, blk * LANES:(blk + 1) * LANES] = pair.T.astype(BF16)

    def sel_bias(g, kt):
        rows = [jnp.broadcast_to(selb_ref[pl.ds(g * N_SEL_BLOCKS + kt * (tk // SEL_BLOCK) + r, 1), :],
                                 (SEL_BLOCK, tq)) for r in range(tk // SEL_BLOCK)]
        return jnp.concatenate(rows, axis=0)

    for g in range(N_KV):
        gl = slice(g * LANES, (g + 1) * LANES)
        gv = slice(g * HEAD_DIM, (g + 1) * HEAD_DIM)
        qvars = []
        for jb in range(2):
            qvars.extend(_head_variants(q_ref[:, (2 * g + jb) * LANES:(2 * g + jb + 1) * LANES]))

        def k_tile(ref, kt):
            return ref[0, pl.ds(pl.multiple_of(kt * tk, tk), tk), gl]

        def v_tile(ref, kt):
            return ref[gv, pl.ds(pl.multiple_of(kt * tk, tk), tk)]

        reset()

        def sel_step(kt, carry):
            update(qvars, k_tile(ks_ref, kt), v_tile(vst_ref, kt), sel_bias(g, kt))
            return carry

        lax.fori_loop(0, qi, sel_step, 0)
        update(qvars, k_tile(ks_ref, qi), v_tile(vst_ref, qi), sel_bias(g, qi) + causal_bias)
        finish(osel_ref, g)

        reset()

        @pl.when(qi >= 2)
        def _():
            update(qvars, k_tile(kw_ref, qi - 2), v_tile(vwt_ref, qi - 2), far_bias)

        @pl.when(qi >= 1)
        def _():
            update(qvars, k_tile(kw_ref, qi - 1), v_tile(vwt_ref, qi - 1), None)

        update(qvars, k_tile(kw_ref, qi), v_tile(vwt_ref, qi), causal_bias)
        finish(owin_ref, g)


def _selwin(q, ks, kw, vst, vwt, selb):
    tq = ATT_TQ
    nq = SEQ // tq
    assert WINDOW == 2 * ATT_TK and ATT_TQ == ATT_TK
    row = lambda b, i: (b * nq + i, 0)
    keys = pl.BlockSpec((1, SEQ, 256), lambda b, i: (b, 0, 0))
    vals = pl.BlockSpec((LANES, SEQ), lambda b, i: (0, b))
    r3 = lambda a: a.reshape(BATCH, SEQ, 256)
    return pl.pallas_call(
        _selwin_kernel,
        grid=(BATCH, nq),
        in_specs=[pl.BlockSpec((tq, NSA_WIDTH), row), keys, keys, vals, vals,
                  pl.BlockSpec((N_KV * N_SEL_BLOCKS, tq), lambda b, i: (0, b * nq + i))],
        out_specs=[pl.BlockSpec((tq, NSA_WIDTH), row)] * 2,
        out_shape=[jax.ShapeDtypeStruct((N_TOK, NSA_WIDTH), BF16)] * 2,
        scratch_shapes=[pltpu.VMEM((4, 1, tq), F32), pltpu.VMEM((4, 1, tq), F32),
                        pltpu.VMEM((4, HEAD_DIM, tq), F32)],
        compiler_params=_cparams("arbitrary", "arbitrary"),
        name="selwin",
    )(q, r3(ks), r3(kw), vst, vwt, selb)


def _s5_param_kernel(are_ref, aim_ref, ldt_ref, cre_ref, cim_ref, bre_ref, bim_ref,
                     clre_ref, clim_ref, wbre_ref, wbim_ref, bbre_ref, bbim_ref, ltre_ref, ltim_ref):
    are, aim = are_ref[...], aim_ref[...]
    dt = jnp.exp(ldt_ref[...])
    cre, cim = cre_ref[...], cim_ref[...]

    def lam_pow(tau):
        mag = jnp.exp(are * dt * float(tau))
        ang = aim * dt * float(tau)
        return mag * jnp.cos(ang), mag * jnp.sin(ang)

    lre, lim = lam_pow(1)
    den = are * are + aim * aim
    qre = ((lre - 1.0) * are + lim * aim) / den
    qim = (lim * are - (lre - 1.0) * aim) / den
    bre, bim = bre_ref[...], bim_ref[...]
    bbre = qre * bre - qim * bim
    bbim = qre * bim + qim * bre
    bbre_ref[...] = bbre
    bbim_ref[...] = bbim
    for tau in range(S5_T + 1):
        pr, pi = lam_pow(tau)
        clre_ref[tau] = cre * pr - cim * pi
        clim_ref[tau] = cre * pi + cim * pr
        if tau < S5_T:
            k = S5_T - 1 - tau
            wbre_ref[k] = pr * bbre - pi * bbim
            wbim_ref[k] = pr * bbim + pi * bbre
        else:
            ltre_ref[...] = pr
            ltim_ref[...] = pi


def _s5_kmat_kernel(l_ref, r_ref, o_ref):
    o_ref[0] = jnp.dot(l_ref[0], r_ref[0], preferred_element_type=F32, precision=HIGHEST)


def _s5_params(a_re, a_im, log_dt, b_re, b_im, c_re, c_im):
    T = S5_T
    pn = GROUP * STATE
    tile_p = lambda a: jnp.tile(a, (1, GROUP))
    args = (tile_p(a_re), tile_p(a_im), jnp.broadcast_to(log_dt[:, None], (N_GROUPS, pn)),
            c_re.reshape(N_GROUPS, pn), c_im.reshape(N_GROUPS, pn),
            jnp.swapaxes(b_re, 1, 2).reshape(N_GROUPS, pn), jnp.swapaxes(b_im, 1, 2).reshape(N_GROUPS, pn))
    full2 = pl.BlockSpec((N_GROUPS, pn), lambda: (0, 0))
    clre, clim, wbre, wbim, bbre, bbim, ltre, ltim = pl.pallas_call(
        _s5_param_kernel,
        in_specs=[full2] * 7,
        out_specs=[pl.BlockSpec((T + 1, N_GROUPS, pn), lambda: (0, 0, 0))] * 2
                  + [pl.BlockSpec((T, N_GROUPS, pn), lambda: (0, 0, 0))] * 2 + [full2] * 4,
        out_shape=[jax.ShapeDtypeStruct((T + 1, N_GROUPS, pn), F32)] * 2
                  + [jax.ShapeDtypeStruct((T, N_GROUPS, pn), F32)] * 2
                  + [jax.ShapeDtypeStruct((N_GROUPS, pn), F32)] * 4,
        name="s5_params",
    )(*args)

    r5 = lambda a, t: a[:t].reshape(t, N_GROUPS, GROUP, STATE)
    lhs = jnp.concatenate([r5(clre, T), -r5(clim, T)], axis=-1)
    lhs = jnp.transpose(lhs, (1, 0, 2, 3)).reshape(N_GROUPS, T * GROUP, 2 * STATE)
    bb = lambda a: jnp.swapaxes(a.reshape(N_GROUPS, GROUP, STATE), 1, 2)
    rhs = jnp.concatenate([bb(bbre), bb(bbim)], axis=1)
    kmat = pl.pallas_call(
        _s5_kmat_kernel,
        grid=(N_GROUPS,),
        in_specs=[pl.BlockSpec((1, T * GROUP, 2 * STATE), lambda g: (g, 0, 0)),
                  pl.BlockSpec((1, 2 * STATE, GROUP), lambda g: (g, 0, 0))],
        out_specs=pl.BlockSpec((1, T * GROUP, GROUP), lambda g: (g, 0, 0)),
        out_shape=jax.ShapeDtypeStruct((N_GROUPS, T * GROUP, GROUP), F32),
        compiler_params=_cparams("arbitrary"),
        name="s5_kmat",
    )(lhs, rhs)

    eye = jnp.eye(S5_GL, dtype=F32)
    kt = kmat.reshape(S5_SG, S5_GL, T, GROUP, GROUP)
    kbd = jnp.einsum('sgtpq,gh->stgqhp', kt, eye).reshape(S5_SG, T, LANES, LANES)
    krev = kbd[:, ::-1].reshape(S5_SG, T * LANES, LANES).astype(BF16)
    r6 = lambda a: a.reshape(T, S5_SG, S5_GL, GROUP, STATE)
    wb = jnp.stack([r6(wbre), r6(wbim)], axis=-2)
    wb = jnp.einsum('ksgpin,gh->skgpihn', wb, eye).reshape(S5_SG, T * LANES, S5_NSTATE).astype(BF16)
    wc = jnp.stack([r6(clre[1:]), -r6(clim[1:])], axis=-2)
    wc = jnp.einsum('tsgpin,gh->signthp', wc, eye).reshape(S5_SG, S5_NSTATE, T * LANES).astype(BF16)
    lt = lambda a: a.reshape(N_GROUPS, GROUP, STATE)[:, 0].reshape(S5_SG, 1, S5_GL * STATE)
    return krev, wb, wc, lt(ltre), lt(ltim)


S5_TC = 512


def _s5_state_kernel(x_ref, wb_ref, e_ref):
    e_ref[0] = _dot(x_ref[0], wb_ref[0])


def _s5_scan_kernel(e_ref, ltre_ref, ltim_ref, xs_ref):
    lr, li = ltre_ref[0], ltim_ref[0]
    half = S5_NSTATE // 2

    def step(c, carry):
        new = []
        for b in range(BATCH):
            xr, xi = carry[b]
            row = b * S5_CH_PER_BATCH + c
            xs_ref[0, pl.ds(row, 1), :half] = xr
            xs_ref[0, pl.ds(row, 1), half:] = xi
            e = e_ref[0, pl.ds(row, 1), :]
            new.append((lr * xr - li * xi + e[:, :half], lr * xi + li * xr + e[:, half:]))
        return tuple(new)

    zero = jnp.zeros((1, half), F32)
    lax.fori_loop(0, S5_CH_PER_BATCH, step, tuple((zero, zero) for _ in range(BATCH)))


def _s5_out_kernel(x_ref, xs_ref, krev_ref, wc_ref, y_ref):
    xsb = xs_ref[0].astype(BF16)
    for t in range(S5_T):
        cols = slice(t * LANES, (t + 1) * LANES)
        y_ref[0, :, cols] = (_dot(x_ref[0, :, :(t + 1) * LANES], krev_ref[0, (S5_T - 1 - t) * LANES:, :])
                             + _dot(xsb, wc_ref[0, :, cols]))


def _s5(u, krev, wb, wc, ltre, ltim):
    T, tc = S5_T, S5_TC
    xs = u.astype(BF16).reshape(S5_CH, T, S5_SG, LANES).transpose(2, 0, 1, 3).reshape(S5_SG, S5_CH, T * LANES)
    grid = (S5_SG, S5_CH // tc)
    rows = lambda s, i: (s, i, 0)
    per_sg = lambda s, i: (s, 0, 0)
    e = pl.pallas_call(
        _s5_state_kernel, grid=grid,
        in_specs=[pl.BlockSpec((1, tc, T * LANES), rows), pl.BlockSpec((1, T * LANES, S5_NSTATE), per_sg)],
        out_specs=pl.BlockSpec((1, tc, S5_NSTATE), rows),
        out_shape=jax.ShapeDtypeStruct((S5_SG, S5_CH, S5_NSTATE), F32),
        compiler_params=_cparams("arbitrary", "arbitrary"), name="s5_state",
    )(xs, wb)
    sg1 = lambda s: (s, 0, 0)
    xstart = pl.pallas_call(
        _s5_scan_kernel, grid=(S5_SG,),
        in_specs=[pl.BlockSpec((1, S5_CH, S5_NSTATE), sg1),
                  pl.BlockSpec((1, 1, S5_NSTATE // 2), sg1), pl.BlockSpec((1, 1, S5_NSTATE // 2), sg1)],
        out_specs=pl.BlockSpec((1, S5_CH, S5_NSTATE), sg1),
        out_shape=jax.ShapeDtypeStruct((S5_SG, S5_CH, S5_NSTATE), F32),
        compiler_params=_cparams("arbitrary"), name="s5_scan",
    )(e, ltre, ltim)
    y = pl.pallas_call(
        _s5_out_kernel, grid=grid,
        in_specs=[pl.BlockSpec((1, tc, T * LANES), rows), pl.BlockSpec((1, tc, S5_NSTATE), rows),
                  pl.BlockSpec((1, T * LANES, LANES), per_sg), pl.BlockSpec((1, S5_NSTATE, T * LANES), per_sg)],
        out_specs=pl.BlockSpec((1, tc, T * LANES), rows),
        out_shape=jax.ShapeDtypeStruct((S5_SG, S5_CH, T * LANES), F32),
        compiler_params=_cparams("arbitrary", "arbitrary"), name="s5_out",
    )(xs, xstart, krev, wc)
    return y.reshape(S5_SG, S5_CH, T, LANES).transpose(1, 2, 0, 3).reshape(N_TOK, SSM_WIDTH)


MERGE_TM = 256


def _merge_kernel(ocmp_ref, osel_ref, owin_ref, gn_ref, yssm_ref, u_ref, ga_ref, gs_ref, x_ref, mod_ref,
                  eg_ref, dskip_ref, wglu_ref, bglu_ref, wua_ref, wus_ref, wout_ref, g2_ref,
                  wrhi_ref, wrlo_ref, wsgu_ref, wsd_ref,
                  xpart_ref, h2_ref, logit_ref):
    mod = mod_ref[0]
    gnb = gn_ref[...].astype(BF16)
    o_nsa = (_dot(gnb, eg_ref[0]) * ocmp_ref[...].astype(F32)
             + _dot(gnb, eg_ref[1]) * osel_ref[...].astype(F32)
             + _dot(gnb, eg_ref[2]) * owin_ref[...].astype(F32))
    attn = _dot(o_nsa.astype(BF16), wua_ref[...])
    z = _gelu(yssm_ref[...] + dskip_ref[...] * u_ref[...])
    y_ssm = z * jax.nn.sigmoid(_dot(z.astype(BF16), wglu_ref[...]) + bglu_ref[...])
    ssm = _dot(y_ssm.astype(BF16), wus_ref[...])
    merged = ga_ref[...].astype(F32) * attn + gs_ref[...].astype(F32) * ssm
    x1 = x_ref[...] + mod[2:3] * _dot(merged.astype(BF16), wout_ref[...])

    ms = jnp.mean(x1 * x1, axis=-1, keepdims=True)
    h2 = (x1 * lax.rsqrt(ms + EPS) * g2_ref[...]) * (1.0 + mod[4:5]) + mod[3:4]
    hi = h2.astype(BF16)
    hi32 = hi.astype(F32)
    lo = (h2 - hi32).astype(BF16)
    bits = lax.bitcast_convert_type(hi32, jnp.uint32)
    half = D_MODEL // 2
    h2_ref[...] = bits[:, half:] | lax.shift_right_logical(bits[:, :half], jnp.uint32(16))
    logit_ref[...] = _dot_nt(wrhi_ref[...], hi) + _dot_nt(wrhi_ref[...], lo) + _dot_nt(wrlo_ref[...], hi)
    gu = _dot(hi, wsgu_ref[...])
    shared = _dot((_silu(gu[:, :D_EXPERT]) * gu[:, D_EXPERT:]).astype(BF16), wsd_ref[...])
    xpart_ref[...] = x1 + mod[5:6] * shared


def _merge(ocmp, osel, owin, gn, yssm, u, ga, gs, x2, mod, d_skip, w_glu, b_glu, w_up_attn, w_up_ssm, w_out,
           g_norm2, w_router, ws_gate, ws_up, ws_down):
    tm = MERGE_TM
    eg = np.zeros((3, LANES, NSA_WIDTH), np.float32)
    for j in range(3):
        for h in range(N_HEADS):
            eg[j, 3 * h + j, h * HEAD_DIM:(h + 1) * HEAD_DIM] = 1.0
    wr_t = w_router.T
    wr_hi = wr_t.astype(BF16)
    wr_lo = (wr_t - wr_hi.astype(F32)).astype(BF16)
    row = lambda i: (i, 0)
    fix2 = lambda i: (0, 0)
    wspec = lambda a: pl.BlockSpec(a.shape, (lambda i: (0,) * a.ndim))
    weights = [jnp.asarray(eg, BF16), d_skip.reshape(1, -1), w_glu.astype(BF16), b_glu.reshape(1, -1),
               w_up_attn.astype(BF16), w_up_ssm.astype(BF16), w_out.astype(BF16), g_norm2.reshape(1, -1),
               wr_hi, wr_lo, jnp.concatenate([ws_gate, ws_up], axis=1).astype(BF16), ws_down.astype(BF16)]
    acts = [(ocmp, 512), (osel, 512), (owin, 512), (gn, 128), (yssm, 512), (u, 512), (ga, 1024), (gs, 1024),
            (x2, 1024)]
    return pl.pallas_call(
        _merge_kernel,
        grid=(N_TOK // tm,),
        in_specs=[pl.BlockSpec((tm, wd), row) for _, wd in acts]
                 + [pl.BlockSpec((1, 6, D_MODEL), lambda i: (i // (SEQ // tm), 0, 0))]
                 + [wspec(w) for w in weights],
        out_specs=[pl.BlockSpec((tm, D_MODEL), row), pl.BlockSpec((tm, D_MODEL // 2), row),
                   pl.BlockSpec((N_EXPERTS, tm), lambda i: (0, i))],
        out_shape=[jax.ShapeDtypeStruct((N_TOK, D_MODEL), F32),
                   jax.ShapeDtypeStruct((N_TOK, D_MODEL // 2), jnp.uint32),
                   jax.ShapeDtypeStruct((N_EXPERTS, N_TOK), F32)],
        compiler_params=_cparams("arbitrary"),
        name="merge",
    )(*[a for a, _ in acts], mod, *weights)


ROUTE_TN = 512


def _route_kernel(logit_ref, bias_ref, eidx_ref, w_ref, gscore_ref, masked_ref):
    tn = ROUTE_TN
    sc = jax.nn.sigmoid(logit_ref[...])
    biased = sc + bias_ref[...]
    gi = lax.broadcasted_iota(I32, (EXPERTS_PER_GROUP, tn), 0).astype(F32)
    for g in range(N_EXPERT_GROUPS):
        blk = biased[g * EXPERTS_PER_GROUP:(g + 1) * EXPERTS_PER_GROUP]
        m1 = jnp.max(blk, axis=0, keepdims=True)
        i1 = jnp.min(jnp.where(blk == m1, gi, float(EXPERTS_PER_GROUP)), axis=0, keepdims=True)
        m2 = jnp.max(jnp.where(gi == i1, -jnp.inf, blk), axis=0, keepdims=True)
        gscore_ref[g:g + 1, :] = m1 + m2
    gs = gscore_ref[...]
    gidx = lax.broadcasted_iota(I32, (N_EXPERT_GROUPS, tn), 0)
    grank = jnp.zeros((N_EXPERT_GROUPS, tn), F32)
    for gp in range(N_EXPERT_GROUPS):
        row = gs[gp:gp + 1, :]
        tie = jnp.where(gidx > gp, 1.0, 0.0)
        grank = grank + jnp.where(row > gs, 1.0, jnp.where(row == gs, tie, 0.0))
    for g in range(N_EXPERT_GROUPS):
        keep = grank[g:g + 1, :] < float(TOPK_GROUPS)
        sl = slice(g * EXPERTS_PER_GROUP, (g + 1) * EXPERTS_PER_GROUP)
        masked_ref[sl, :] = jnp.where(keep, biased[sl], -jnp.inf)
    cur = masked_ref[...]
    eidx = lax.broadcasted_iota(I32, (N_EXPERTS, tn), 0).astype(F32)
    wsum = jnp.zeros((1, tn), F32)
    for k in range(TOP_K):
        m = jnp.max(cur, axis=0, keepdims=True)
        idx = jnp.min(jnp.where(cur == m, eidx, float(N_EXPERTS)), axis=0, keepdims=True)
        hit = eidx == idx
        wk = jnp.sum(jnp.where(hit, sc, 0.0), axis=0, keepdims=True)
        cur = jnp.where(hit, -jnp.inf, cur)
        eidx_ref[k:k + 1, :] = idx.astype(I32)
        w_ref[k:k + 1, :] = wk
        wsum = wsum + wk
    w_ref[...] = w_ref[...] / wsum * ROUTE_SCALE


def _route(logits_t, router_bias):
    tn = ROUTE_TN
    return pl.pallas_call(
        _route_kernel,
        grid=(N_TOK // tn,),
        in_specs=[pl.BlockSpec((N_EXPERTS, tn), lambda i: (0, i)), pl.BlockSpec((N_EXPERTS, 1), lambda i: (0, 0))],
        out_specs=[pl.BlockSpec((TOP_K, tn), lambda i: (0, i))] * 2,
        out_shape=[jax.ShapeDtypeStruct((TOP_K, N_TOK), I32), jax.ShapeDtypeStruct((TOP_K, N_TOK), F32)],
        scratch_shapes=[pltpu.VMEM((N_EXPERT_GROUPS, tn), F32), pltpu.VMEM((N_EXPERTS, tn), F32)],
        compiler_params=_cparams("arbitrary"),
        name="route",
    )(logits_t, router_bias.reshape(-1, 1))


N_MOE_BLK = NK // DISPATCH_BLOCK
N_ITEMS = N_MOE_BLK + N_EXPERTS
ASSIGN_BITS = 17


def _dispatch_plan(eidx):
    e_flat = eidx.reshape(-1)
    key = jnp.sort(e_flat * NK + jnp.arange(NK, dtype=I32))
    se = jnp.right_shift(key, ASSIGN_BITS)
    order = key & (NK - 1)
    start = jnp.searchsorted(se, jnp.arange(N_EXPERTS, dtype=I32), side='left').astype(I32)
    cuts = jnp.sort(jnp.concatenate([jnp.arange(N_MOE_BLK, dtype=I32) * DISPATCH_BLOCK, start]))
    lo = cuts
    hi = jnp.concatenate([cuts[1:], jnp.full((1,), NK, I32)])
    blk = jnp.minimum(lo // DISPATCH_BLOCK, N_MOE_BLK - 1)
    expert = jnp.clip(jnp.searchsorted(start, lo, side='right').astype(I32) - 1, 0, N_EXPERTS - 1)
    first = jnp.concatenate([jnp.ones((1,), I32), (blk[1:] != blk[:-1]).astype(I32)])
    last = jnp.concatenate([(blk[1:] != blk[:-1]).astype(I32), jnp.ones((1,), I32)])
    return (order.reshape(N_MOE_BLK, 1, DISPATCH_BLOCK), blk, expert,
            lo - blk * DISPATCH_BLOCK, hi - blk * DISPATCH_BLOCK, first, last)


def _moe_row_copy(ybuf, slots_hbm, sem, slot, i, dst_row):
    return pltpu.make_async_copy(ybuf.at[slot, pl.ds(i, 1)], slots_hbm.at[pl.ds(dst_row, 1)], sem.at[slot])


def _moe_kernel(blk_ref, exp_ref, lo_ref, hi_ref, first_ref, last_ref,
                ord_ref, h2_hbm, wg_ref, wu_ref, wd_ref, slots_hbm,
                h2v, xg, ybuf, ldsem, sem):
    it = pl.program_id(0)
    b = blk_ref[it]
    slot = b % 2
    rows = DISPATCH_BLOCK
    half = D_MODEL // 2

    @pl.when(it == 0)
    def _():
        cp = pltpu.make_async_copy(h2_hbm, h2v, ldsem.at[0])
        cp.start()
        cp.wait()

    @pl.when(first_ref[it] == 1)
    def _():
        @pl.when(b >= 2)
        def _():
            for i in range(rows):
                _moe_row_copy(ybuf, slots_hbm, sem, slot, i, 0).wait()
        for i in range(rows):
            tok = lax.shift_right_logical(ord_ref[0, 0, i], 3)
            xg[pl.ds(i, 1), :] = h2v[pl.ds(tok, 1), :]
        ybuf[slot] = jnp.zeros((rows, D_MODEL), F32)

    lo, hi = lo_ref[it], hi_ref[it]

    @pl.when(hi > lo)
    def _():
        ridx = lax.broadcasted_iota(I32, (rows, half), 0)
        mine = (ridx >= lo) & (ridx < hi)
        word = jnp.where(mine, xg[...], jnp.uint32(0))
        xlo = lax.bitcast_convert_type(lax.shift_left(word, jnp.uint32(16)), F32).astype(BF16)
        xhi = lax.bitcast_convert_type(word & jnp.uint32(0xFFFF0000), F32).astype(BF16)
        wg = wg_ref[0].astype(BF16)
        wu = wu_ref[0].astype(BF16)
        gate = _dot(xlo, wg[:half]) + _dot(xhi, wg[half:])
        up = _dot(xlo, wu[:half]) + _dot(xhi, wu[half:])
        ybuf[slot] = ybuf[slot] + _dot((_silu(gate) * up).astype(BF16), wd_ref[0].astype(BF16))

    @pl.when(last_ref[it] == 1)
    def _():
        for i in range(rows):
            _moe_row_copy(ybuf, slots_hbm, sem, slot, i, ord_ref[0, 0, i]).start()

    @pl.when(it == pl.num_programs(0) - 1)
    def _():
        for s in range(2):
            for i in range(rows):
                _moe_row_copy(ybuf, slots_hbm, sem, s, i, 0).wait()


def _moe(h2p, plan, w_gate, w_up, w_down):
    order3, blk, expert, lo, hi, first, last = plan
    by_blk = lambda it, blk, *_: (blk[it], 0, 0)
    by_exp = lambda it, blk, ex, *_: (ex[it], 0, 0)
    grid_spec = pltpu.PrefetchScalarGridSpec(
        num_scalar_prefetch=6,
        grid=(N_ITEMS,),
        in_specs=[pl.BlockSpec((1, 1, DISPATCH_BLOCK), by_blk, memory_space=pltpu.SMEM),
                  pl.BlockSpec(memory_space=pl.ANY),
                  pl.BlockSpec((1, D_MODEL, D_EXPERT), by_exp),
                  pl.BlockSpec((1, D_MODEL, D_EXPERT), by_exp),
                  pl.BlockSpec((1, D_EXPERT, D_MODEL), by_exp)],
        out_specs=pl.BlockSpec(memory_space=pl.ANY),
        scratch_shapes=[pltpu.VMEM((N_TOK, D_MODEL // 2), jnp.uint32),
                        pltpu.VMEM((DISPATCH_BLOCK, D_MODEL // 2), jnp.uint32),
                        pltpu.VMEM((2, DISPATCH_BLOCK, D_MODEL), F32),
                        pltpu.SemaphoreType.DMA((1,)), pltpu.SemaphoreType.DMA((2,))],
    )
    return pl.pallas_call(
        _moe_kernel,
        grid_spec=grid_spec,
        out_shape=jax.ShapeDtypeStruct((NK, D_MODEL), F32),
        compiler_params=pltpu.CompilerParams(dimension_semantics=("arbitrary",),
                                             vmem_limit_bytes=56 * 1024 * 1024),
        name="moe",
    )(blk, expert, lo, hi, first, last, order3, h2p, w_gate, w_up, w_down)


COMB_TC = 64


def _combine_kernel(slots_ref, w_ref, xpart_ref, mod_ref, out_ref):
    w = w_ref[...]
    routed = w[:, 0:1] * slots_ref[:, 0:D_MODEL]
    for k in range(1, TOP_K):
        routed = routed + w[:, k:k + 1] * slots_ref[:, k * D_MODEL:(k + 1) * D_MODEL]
    out_ref[...] = xpart_ref[...] + mod_ref[0][5:6] * routed


def _combine(xpart, mod, slots, w):
    tc = COMB_TC
    row = lambda i: (i, 0)
    return pl.pallas_call(
        _combine_kernel,
        grid=(N_TOK // tc,),
        in_specs=[pl.BlockSpec((tc, TOP_K * D_MODEL), row),
                  pl.BlockSpec((tc, TOP_K), row),
                  pl.BlockSpec((tc, D_MODEL), row),
                  pl.BlockSpec((1, 6, D_MODEL), lambda i: (i // (SEQ // tc), 0, 0))],
        out_specs=pl.BlockSpec((tc, D_MODEL), row),
        out_shape=jax.ShapeDtypeStruct((N_TOK, D_MODEL), F32),
        compiler_params=_cparams("arbitrary"),
        name="combine",
    )(slots.reshape(N_TOK, TOP_K * D_MODEL), w, xpart, mod)


def _layer(x, c, w_ada, b_ada, g_norm1, g_norm2, w_in, q_gain, kc_gain, ks_gain, kw_gain,
           pe_k, pe_v, w_cmp_k1, w_cmp_k2, w_cmp_v1, w_cmp_v2,
           a_re, a_im, log_dt, b_re, b_im, c_re, c_im, d_skip, w_glu, b_glu,
           w_up_attn, w_up_ssm, w_out, w_router, router_bias,
           w_gate, w_up, w_down, ws_gate, ws_up, ws_down):
    x2 = x.reshape(N_TOK, D_MODEL)
    mod = _ada(c, w_ada, b_ada)
    q, kc_raw, vc_raw, ks, kw, vst, vwt, gn, u, ga, gs = _proj(x2, mod, g_norm1, w_in, q_gain, ks_gain, kw_gain)
    kcn = _compress(kc_raw, pe_k, w_cmp_k1, w_cmp_k2, kc_gain, True)
    vcn = _compress(vc_raw, pe_v, w_cmp_v1, w_cmp_v2, kc_gain, False)
    ocmp, selb = _cmp_attn(q, kcn, vcn)
    osel, owin = _selwin(q, ks, kw, vst, vwt, selb)
    yssm = _s5(u, *_s5_params(a_re, a_im, log_dt, b_re, b_im, c_re, c_im))
    xpart, h2p, logits_t = _merge(ocmp, osel, owin, gn, yssm, u, ga, gs, x2, mod, d_skip, w_glu, b_glu,
                                  w_up_attn, w_up_ssm, w_out, g_norm2, w_router, ws_gate, ws_up, ws_down)
    eidx_t, w_t = _route(logits_t, router_bias)
    slots = _moe(h2p, _dispatch_plan(eidx_t.T), w_gate, w_up, w_down)
    return _combine(xpart, mod, slots, w_t.T).reshape(BATCH, SEQ, D_MODEL)


def kernel(x, c, w_ada, b_ada, g_norm1, g_norm2, w_in, q_gain, kc_gain, ks_gain, kw_gain, pe_k, pe_v, w_cmp_k1,
           w_cmp_k2, w_cmp_v1, w_cmp_v2, a_re, a_im, log_dt, b_re, b_im, c_re, c_im, d_skip, w_glu, b_glu,
           w_up_attn, w_up_ssm, w_out, w_router, router_bias, w_gate, w_up, w_down, ws_gate, ws_up, ws_down):
    params = (w_ada, b_ada, g_norm1, g_norm2, w_in, q_gain, kc_gain, ks_gain, kw_gain, pe_k, pe_v, w_cmp_k1,
              w_cmp_k2, w_cmp_v1, w_cmp_v2, a_re, a_im, log_dt, b_re, b_im, c_re, c_im, d_skip, w_glu, b_glu,
              w_up_attn, w_up_ssm, w_out, w_router, router_bias, w_gate, w_up, w_down, ws_gate, ws_up, ws_down)
    depth = w_ada.shape[0]
    for layer in range(depth):
        x = _layer(x, c, *[p[layer] for p in params])
    return x
```

```python
import functools
import math

import jax
import jax.numpy as jnp
import numpy as np
from jax import lax
from jax.experimental import pallas as pl
from jax.experimental.pallas import tpu as pltpu

F32 = jnp.float32
BF16 = jnp.bfloat16
I32 = jnp.int32
HIGHEST = lax.Precision.HIGHEST

D_MODEL = 1024
BATCH = 4
SEQ = 4096
N_TOK = BATCH * SEQ
N_HEADS = 8
HEAD_DIM = 64
N_KV = 2
CMP_BLOCK = 32
CMP_STRIDE = 16
CMP_HIDDEN = 256
N_CMP = 256
SEL_BLOCK = 64
N_SEL_BLOCKS = SEQ // SEL_BLOCK
N_SELECT = 16
WINDOW = 512
ATTN_SCALE = HEAD_DIM ** -0.5
NSA_WIDTH = N_HEADS * HEAD_DIM
SSM_WIDTH = 512
GROUP = 16
N_GROUPS = SSM_WIDTH // GROUP
STATE = 64
N_EXPERTS = 256
TOP_K = 8
D_EXPERT = 256
N_EXPERT_GROUPS = 8
EXPERTS_PER_GROUP = N_EXPERTS // N_EXPERT_GROUPS
TOPK_GROUPS = 4
ROUTE_SCALE = 2.5
DISPATCH_BLOCK = 128
EPS = 1e-6
NEG = -1e30

LANES = 128
S5_T = 16
S5_SG = 4
S5_GL = N_GROUPS // S5_SG
S5_CH = N_TOK // S5_T
S5_CH_PER_BATCH = SEQ // S5_T
S5_NSTATE = S5_GL * STATE * 2

NK = N_TOK * TOP_K
CAP = (NK + N_EXPERTS * DISPATCH_BLOCK + DISPATCH_BLOCK - 1) // DISPATCH_BLOCK * DISPATCH_BLOCK
N_BLK = CAP // DISPATCH_BLOCK

VMEM_LIMIT = 48 * 1024 * 1024


def _cparams(*sem):
    return pltpu.CompilerParams(dimension_semantics=tuple(sem), vmem_limit_bytes=VMEM_LIMIT)


def _dot(a, b):
    return jnp.dot(a, b, preferred_element_type=F32)


def _dot_nt(a, b):
    return lax.dot_general(a, b, (((1,), (1,)), ((), ())), preferred_element_type=F32)


def _split_dot(v, w):
    hi = v.astype(BF16)
    lo = (v - hi.astype(F32)).astype(BF16)
    return _dot(hi, w) + _dot(lo, w)


def _seg_rms(v, bd, gain):
    ss = _split_dot(v * v, bd)
    return v * lax.rsqrt(ss * (1.0 / HEAD_DIM) + EPS) * gain


def _gelu(x):
    return 0.5 * x * (1.0 + jnp.tanh(0.7978845608028654 * (x + 0.044715 * (x * x * x))))


def _silu(x):
    return x * jax.nn.sigmoid(x)


def _ada_kernel(c_ref, w_ref, b_ref, o_ref):
    c = c_ref[...]
    o_ref[...] = jnp.dot(_silu(c), w_ref[...], preferred_element_type=F32, precision=HIGHEST) + b_ref[...]


def _ada(c, w_ada, b_ada):
    cp = jnp.pad(c, ((0, 8 - BATCH), (0, 0)))
    tn = 1536
    out = pl.pallas_call(
        _ada_kernel,
        grid=(6 * D_MODEL // tn,),
        in_specs=[pl.BlockSpec((8, D_MODEL), lambda j: (0, 0)),
                  pl.BlockSpec((D_MODEL, tn), lambda j: (0, j)),
                  pl.BlockSpec((1, tn), lambda j: (0, j))],
        out_specs=pl.BlockSpec((8, tn), lambda j: (0, j)),
        out_shape=jax.ShapeDtypeStruct((8, 6 * D_MODEL), F32),
        compiler_params=_cparams("arbitrary"),
        name="ada",
    )(cp, w_ada, b_ada.reshape(1, -1))
    return out.reshape(8, 6, D_MODEL)


_C_Q = 0
_C_KC = 512
_C_VC = 640
_C_KS = 768
_C_KW = 1024
_C_GN = 1280
_C_U = 1408
_C_GA = 1920
_C_GS = 2944
_C_END = 3968
PROJ_TM = 512


def _proj_kernel(x_ref, mod_ref, g1_ref, w_ref, wvt_ref, qg_ref, ksg_ref, kwg_ref, bd512_ref, bd256_ref,
                 q_ref, kc_ref, vc_ref, ks_ref, kw_ref, vst_ref, vwt_ref, gn_ref, u_ref, ga_ref, gs_ref):
    x = x_ref[...]
    ms = jnp.mean(x * x, axis=-1, keepdims=True)
    mod = mod_ref[0]
    h = (x * lax.rsqrt(ms + EPS) * g1_ref[...]) * (1.0 + mod[1:2]) + mod[0:1]
    hb = h.astype(BF16)

    def p(lo, hi):
        return _dot(hb, w_ref[:, lo:hi])

    q_ref[...] = _seg_rms(p(_C_Q, _C_KC), bd512_ref[...], qg_ref[...] * ATTN_SCALE).astype(BF16)
    kc_ref[...] = p(_C_KC, _C_VC).astype(BF16)
    vc_ref[...] = p(_C_VC, _C_KS).astype(BF16)
    ks_ref[...] = _seg_rms(p(_C_KS, _C_KW), bd256_ref[...], ksg_ref[...]).astype(BF16)
    kw_ref[...] = _seg_rms(p(_C_KW, _C_GN), bd256_ref[...], kwg_ref[...]).astype(BF16)
    vt = _dot_nt(wvt_ref[...], hb)
    vst_ref[...] = vt[:LANES].astype(BF16)
    vwt_ref[...] = vt[LANES:].astype(BF16)
    gn_ref[...] = jax.nn.sigmoid(p(_C_GN, _C_U))
    u_ref[...] = p(_C_U, _C_GA)
    ga_ref[...] = jax.nn.sigmoid(p(_C_GA, _C_GS)).astype(BF16)
    gs_ref[...] = jax.nn.sigmoid(p(_C_GS, _C_END)).astype(BF16)


def _dup_cols(w):
    return jnp.concatenate([w[:, :64], w[:, :64], w[:, 64:], w[:, 64:]], axis=1)


def _block_ones(n):
    return jnp.kron(jnp.eye(n // HEAD_DIM, dtype=F32), jnp.ones((HEAD_DIM, HEAD_DIM), F32)).astype(BF16)


def _proj(x2, mod, g_norm1, w_in, q_gain, ks_gain, kw_gain):
    o = np.cumsum((0, 512, 128, 128, 128, 128, 128, 128, 24, 512, 1024, 1024))
    parts = [w_in[:, o[i]:o[i + 1]] for i in range(11)]
    wq, wkc, wvc, wks, wvs, wkw, wvw, wgn, wu, wga, wgs = parts
    w = jnp.concatenate([wq, wkc, wvc, _dup_cols(wks), _dup_cols(wkw),
                         jnp.pad(wgn, ((0, 0), (0, LANES - 24))), wu, wga, wgs], axis=1).astype(BF16)
    wvt = jnp.concatenate([wvs, wvw], axis=1).T.astype(BF16)
    tm = PROJ_TM
    row = lambda i: (i, 0)
    col = lambda i: (0, i)
    fix = lambda i: (0, 0)
    outs = [(512, BF16, row), (128, BF16, row), (128, BF16, row), (256, BF16, row), (256, BF16, row),
            (LANES, BF16, col), (LANES, BF16, col),
            (128, F32, row), (512, F32, row), (1024, BF16, row), (1024, BF16, row)]
    ospec = lambda wd, m: pl.BlockSpec((tm, wd), m) if m is row else pl.BlockSpec((wd, tm), m)
    oshape = lambda wd, dt, m: jax.ShapeDtypeStruct((N_TOK, wd) if m is row else (wd, N_TOK), dt)
    return pl.pallas_call(
        _proj_kernel,
        grid=(N_TOK // tm,),
        in_specs=[pl.BlockSpec((tm, D_MODEL), row),
                  pl.BlockSpec((1, 6, D_MODEL), lambda i: (i // (SEQ // tm), 0, 0)),
                  pl.BlockSpec((1, D_MODEL), fix),
                  pl.BlockSpec((D_MODEL, _C_END), fix),
                  pl.BlockSpec((2 * LANES, D_MODEL), fix),
                  pl.BlockSpec((1, 512), fix), pl.BlockSpec((1, 256), fix), pl.BlockSpec((1, 256), fix),
                  pl.BlockSpec((512, 512), fix), pl.BlockSpec((256, 256), fix)],
        out_specs=[ospec(wd, m) for wd, _, m in outs],
        out_shape=[oshape(wd, dt, m) for wd, dt, m in outs],
        compiler_params=_cparams("arbitrary"),
        name="proj",
    )(x2, mod, g_norm1.reshape(1, -1), w, wvt,
      jnp.tile(q_gain, N_HEADS).reshape(1, -1), jnp.tile(ks_gain, 4).reshape(1, -1),
      jnp.tile(kw_gain, 4).reshape(1, -1), _block_ones(512), _block_ones(256))


def _compress_kernel(r_ref, pe_ref, w1_ref, w2_ref, bd_ref, gain_ref, o_ref, *, do_norm):
    r = r_ref[0].astype(F32)
    p0 = _dot((r + pe_ref[0]).astype(BF16), w1_ref[0])
    p1 = _dot((r + pe_ref[1]).astype(BF16), w1_ref[1])
    hid = p0 + pltpu.roll(p1, N_CMP - 1, 0)
    c = _dot(_gelu(hid).astype(BF16), w2_ref[...])
    if do_norm:
        c = _seg_rms(c, bd_ref[...], gain_ref[...])
    o_ref[0] = c.astype(BF16)


def _compress(raw, pe, w1, w2, gain, do_norm):
    r = raw.reshape(BATCH, SEQ // CMP_STRIDE, CMP_STRIDE * LANES)
    eye = jnp.eye(N_KV, dtype=F32)
    w1r = w1.reshape(2, CMP_STRIDE, HEAD_DIM, CMP_HIDDEN)
    w1big = jnp.einsum('hldc,gk->hlgdkc', w1r, eye).reshape(2, CMP_STRIDE * LANES, N_KV * CMP_HIDDEN).astype(BF16)
    w2big = jnp.einsum('cd,gk->gckd', w2, eye)
    w2big = jnp.concatenate([w2big, w2big], axis=-1).reshape(N_KV * CMP_HIDDEN, 4 * HEAD_DIM).astype(BF16)
    pe_big = jnp.broadcast_to(pe.reshape(2, CMP_STRIDE, 1, HEAD_DIM), (2, CMP_STRIDE, N_KV, HEAD_DIM))
    pe_big = pe_big.reshape(2, 1, CMP_STRIDE * LANES)
    fix2 = lambda b: (0, 0)
    fix3 = lambda b: (0, 0, 0)
    return pl.pallas_call(
        functools.partial(_compress_kernel, do_norm=do_norm),
        grid=(BATCH,),
        in_specs=[pl.BlockSpec((1, N_CMP, CMP_STRIDE * LANES), lambda b: (b, 0, 0)),
                  pl.BlockSpec((2, 1, CMP_STRIDE * LANES), fix3),
                  pl.BlockSpec((2, CMP_STRIDE * LANES, N_KV * CMP_HIDDEN), fix3),
                  pl.BlockSpec((N_KV * CMP_HIDDEN, 256), fix2),
                  pl.BlockSpec((256, 256), fix2), pl.BlockSpec((1, 256), fix2)],
        out_specs=pl.BlockSpec((1, N_CMP, 256), lambda b: (b, 0, 0)),
        out_shape=jax.ShapeDtypeStruct((BATCH, N_CMP, 256), BF16),
        compiler_params=_cparams("arbitrary"),
        name="compress_k" if do_norm else "compress_v",
    )(r, pe_big, w1big, w2big, _block_ones(256), jnp.tile(gain, 4).reshape(1, -1))


ATT_TQ = 256


def _head_variants(qb):
    lane = lax.broadcasted_iota(I32, qb.shape, 1)
    z = jnp.zeros_like(qb)
    return jnp.where(lane < HEAD_DIM, qb, z), jnp.where(lane < HEAD_DIM, z, qb)


def _cmp_kernel(q_ref, kc_ref, vc_ref, ov_ref, o_ref, sel_ref):
    tq = ATT_TQ
    qi = pl.program_id(1)
    tpos = qi * tq + lax.broadcasted_iota(I32, (tq, N_CMP), 0)
    nidx = lax.broadcasted_iota(I32, (tq, N_CMP), 1)
    mask = (CMP_STRIDE * nidx + (CMP_BLOCK - 1)) <= tpos
    lane_lo = lax.broadcasted_iota(I32, (tq, LANES), 1) < HEAD_DIM
    for g in range(N_KV):
        kd = kc_ref[0, :, g * LANES:(g + 1) * LANES]
        vd = vc_ref[0, :, g * LANES:(g + 1) * LANES]
        psum = jnp.zeros((tq, N_CMP), F32)
        for jb in range(2):
            blk = 2 * g + jb
            pv = []
            for qv in _head_variants(q_ref[:, blk * LANES:(blk + 1) * LANES]):
                s = jnp.where(mask, _dot_nt(qv, kd), NEG)
                m = jnp.max(s, axis=-1, keepdims=True)
                e = jnp.where(mask, jnp.exp(s - m), 0.0)
                l = jnp.sum(e, axis=-1, keepdims=True)
                p = e / jnp.where(l > 0.0, l, 1.0)
                psum = psum + p
                pv.append(_dot(p.astype(BF16), vd))
            o_ref[:, blk * LANES:(blk + 1) * LANES] = jnp.where(lane_lo, pv[0], pv[1]).astype(BF16)
        imp = _split_dot(psum, ov_ref[...])
        imp_t = imp.T[:N_SEL_BLOCKS]
        j = lax.broadcasted_iota(I32, (N_SEL_BLOCKS, tq), 0)
        cur = jnp.right_shift(qi * tq + lax.broadcasted_iota(I32, (N_SEL_BLOCKS, tq), 1), 6)
        forced = (j == 0) | (j == cur) | (j == cur - 1)
        v = jnp.where(forced, jnp.inf, jnp.where(j <= cur, imp_t, -jnp.inf))
        rank = jnp.zeros((N_SEL_BLOCKS, tq), F32)
        for jp in range(N_SEL_BLOCKS):
            row = v[jp:jp + 1, :]
            tie = jnp.where(j > jp, 1.0, 0.0)
            rank = rank + jnp.where(row > v, 1.0, jnp.where(row == v, tie, 0.0))
        sel_ref[g * N_SEL_BLOCKS:(g + 1) * N_SEL_BLOCKS, :] = jnp.where(rank < float(N_SELECT), 0.0, NEG)


def _cmp_attn(q, kcn, vcn):
    nc = np.arange(N_CMP)
    sb = np.arange(LANES)
    ov = ((CMP_STRIDE * nc[:, None] < SEL_BLOCK * sb[None, :] + SEL_BLOCK)
          & (CMP_STRIDE * nc[:, None] + CMP_BLOCK > SEL_BLOCK * sb[None, :])
          & (nc[:, None] < N_CMP - 1) & (sb[None, :] < N_SEL_BLOCKS))
    ov = jnp.asarray(ov, BF16)
    tq = ATT_TQ
    nq = SEQ // tq
    row = lambda b, i: (b * nq + i, 0)
    return pl.pallas_call(
        _cmp_kernel,
        grid=(BATCH, nq),
        in_specs=[pl.BlockSpec((tq, NSA_WIDTH), row),
                  pl.BlockSpec((1, N_CMP, 256), lambda b, i: (b, 0, 0)),
                  pl.BlockSpec((1, N_CMP, 256), lambda b, i: (b, 0, 0)),
                  pl.BlockSpec((N_CMP, LANES), lambda b, i: (0, 0))],
        out_specs=[pl.BlockSpec((tq, NSA_WIDTH), row),
                   pl.BlockSpec((N_KV * N_SEL_BLOCKS, tq), lambda b, i: (0, b * nq + i))],
        out_shape=[jax.ShapeDtypeStruct((N_TOK, NSA_WIDTH), BF16),
                   jax.ShapeDtypeStruct((N_KV * N_SEL_BLOCKS, N_TOK), F32)],
        compiler_params=_cparams("arbitrary", "arbitrary"),
        name="cmp_attn",
    )(q, kcn, vcn, ov)


ATT_TK = 256


M_INIT = -1e29


def _selwin_kernel(q_ref, ks_ref, kw_ref, vst_ref, vwt_ref, selb_ref, osel_ref, owin_ref, m_ref, l_ref, acc_ref):
    tq, tk = ATT_TQ, ATT_TK
    qi = pl.program_id(1)
    krow = lax.broadcasted_iota(I32, (tk, tq), 0)
    qcol = lax.broadcasted_iota(I32, (tk, tq), 1)
    causal_bias = jnp.where(krow <= qcol, 0.0, NEG)
    far_bias = jnp.where(qcol < krow, 0.0, NEG)

    def reset():
        m_ref[...] = jnp.full(m_ref.shape, M_INIT, F32)
        l_ref[...] = jnp.zeros(l_ref.shape, F32)
        acc_ref[...] = jnp.zeros(acc_ref.shape, F32)

    def update(qvars, kd, vt, bias):
        s = _dot_nt(kd, qvars)
        if bias is not None:
            s = s + jnp.concatenate([bias] * 4, axis=1)
        m_old = m_ref[...]
        m_new = jnp.maximum(m_old, jnp.max(s, axis=0, keepdims=True))
        alpha = jnp.exp(m_old - m_new)
        p = jnp.exp(s - m_new)
        l_ref[...] = alpha * l_ref[...] + jnp.sum(p, axis=0, keepdims=True)
        m_ref[...] = m_new
        acc_ref[...] = alpha * acc_ref[...] + _dot(vt, p.astype(BF16))

    def finish(out_ref, g):
        o = acc_ref[...] / l_ref[...]
        for jb in range(2):
            blk = 2 * g + jb
            pair = jnp.concatenate([o[:, 2 * jb * tq:(2 * jb + 1) * tq], o[:, (2 * jb + 1) * tq:(2 * jb + 2) * tq]],
                                   axis=0)
            out_ref[:, blk * LANES:(blk + 1) * LANES] = pair.T.astype(BF16)

    def sel_bias(g, kt):
        rows = [jnp.broadcast_to(selb_ref[pl.ds(g * N_SEL_BLOCKS + kt * (tk // SEL_BLOCK) + r, 1), :],
                                 (SEL_BLOCK, tq)) for r in range(tk // SEL_BLOCK)]
        return jnp.concatenate(rows, axis=0)

    for g in range(N_KV):
        gl = slice(g * LANES, (g + 1) * LANES)
        gv = slice(g * HEAD_DIM, (g + 1) * HEAD_DIM)
        qvars = []
        for jb in range(2):
            qvars.extend(_head_variants(q_ref[:, (2 * g + jb) * LANES:(2 * g + jb + 1) * LANES]))
        qvars = jnp.concatenate(qvars, axis=0)

        def k_tile(ref, kt):
            return ref[0, pl.ds(pl.multiple_of(kt * tk, tk), tk), gl]

        def v_tile(ref, kt):
            return ref[gv, pl.ds(pl.multiple_of(kt * tk, tk), tk)]

        reset()

        def sel_step(kt, carry):
            update(qvars, k_tile(ks_ref, kt), v_tile(vst_ref, kt), sel_bias(g, kt))
            return carry

        lax.fori_loop(0, qi, sel_step, 0)
        update(qvars, k_tile(ks_ref, qi), v_tile(vst_ref, qi), sel_bias(g, qi) + causal_bias)
        finish(osel_ref, g)

        reset()

        @pl.when(qi >= 2)
        def _():
            update(qvars, k_tile(kw_ref, qi - 2), v_tile(vwt_ref, qi - 2), far_bias)

        @pl.when(qi >= 1)
        def _():
            update(qvars, k_tile(kw_ref, qi - 1), v_tile(vwt_ref, qi - 1), None)

        update(qvars, k_tile(kw_ref, qi), v_tile(vwt_ref, qi), causal_bias)
        finish(owin_ref, g)


def _selwin(q, ks, kw, vst, vwt, selb):
    tq = ATT_TQ
    nq = SEQ // tq
    assert WINDOW == 2 * ATT_TK and ATT_TQ == ATT_TK
    row = lambda b, i: (b * nq + i, 0)
    keys = pl.BlockSpec((1, SEQ, 256), lambda b, i: (b, 0, 0))
    vals = pl.BlockSpec((LANES, SEQ), lambda b, i: (0, b))
    r3 = lambda a: a.reshape(BATCH, SEQ, 256)
    return pl.pallas_call(
        _selwin_kernel,
        grid=(BATCH, nq),
        in_specs=[pl.BlockSpec((tq, NSA_WIDTH), row), keys, keys, vals, vals,
                  pl.BlockSpec((N_KV * N_SEL_BLOCKS, tq), lambda b, i: (0, b * nq + i))],
        out_specs=[pl.BlockSpec((tq, NSA_WIDTH), row)] * 2,
        out_shape=[jax.ShapeDtypeStruct((N_TOK, NSA_WIDTH), BF16)] * 2,
        scratch_shapes=[pltpu.VMEM((1, 4 * tq), F32), pltpu.VMEM((1, 4 * tq), F32),
                        pltpu.VMEM((HEAD_DIM, 4 * tq), F32)],
        compiler_params=_cparams("arbitrary", "arbitrary"),
        name="selwin",
    )(q, r3(ks), r3(kw), vst, vwt, selb)


def _s5_param_kernel(are_ref, aim_ref, ldt_ref, cre_ref, cim_ref, bre_ref, bim_ref,
                     clre_ref, clim_ref, wbre_ref, wbim_ref, bbre_ref, bbim_ref, ltre_ref, ltim_ref):
    are, aim = are_ref[...], aim_ref[...]
    dt = jnp.exp(ldt_ref[...])
    cre, cim = cre_ref[...], cim_ref[...]

    def lam_pow(tau):
        mag = jnp.exp(are * dt * float(tau))
        ang = aim * dt * float(tau)
        return mag * jnp.cos(ang), mag * jnp.sin(ang)

    lre, lim = lam_pow(1)
    den = are * are + aim * aim
    qre = ((lre - 1.0) * are + lim * aim) / den
    qim = (lim * are - (lre - 1.0) * aim) / den
    bre, bim = bre_ref[...], bim_ref[...]
    bbre = qre * bre - qim * bim
    bbim = qre * bim + qim * bre
    bbre_ref[...] = bbre
    bbim_ref[...] = bbim
    for tau in range(S5_T + 1):
        pr, pi = lam_pow(tau)
        clre_ref[tau] = cre * pr - cim * pi
        clim_ref[tau] = cre * pi + cim * pr
        if tau < S5_T:
            k = S5_T - 1 - tau
            wbre_ref[k] = pr * bbre - pi * bbim
            wbim_ref[k] = pr * bbim + pi * bbre
        else:
            ltre_ref[...] = pr
            ltim_ref[...] = pi


def _s5_kmat_kernel(l_ref, r_ref, o_ref):
    o_ref[0] = jnp.dot(l_ref[0], r_ref[0], preferred_element_type=F32, precision=HIGHEST)


def _s5_params(a_re, a_im, log_dt, b_re, b_im, c_re, c_im):
    T = S5_T
    pn = GROUP * STATE
    tile_p = lambda a: jnp.tile(a, (1, GROUP))
    args = (tile_p(a_re), tile_p(a_im), jnp.broadcast_to(log_dt[:, None], (N_GROUPS, pn)),
            c_re.reshape(N_GROUPS, pn), c_im.reshape(N_GROUPS, pn),
            jnp.swapaxes(b_re, 1, 2).reshape(N_GROUPS, pn), jnp.swapaxes(b_im, 1, 2).reshape(N_GROUPS, pn))
    full2 = pl.BlockSpec((N_GROUPS, pn), lambda: (0, 0))
    clre, clim, wbre, wbim, bbre, bbim, ltre, ltim = pl.pallas_call(
        _s5_param_kernel,
        in_specs=[full2] * 7,
        out_specs=[pl.BlockSpec((T + 1, N_GROUPS, pn), lambda: (0, 0, 0))] * 2
                  + [pl.BlockSpec((T, N_GROUPS, pn), lambda: (0, 0, 0))] * 2 + [full2] * 4,
        out_shape=[jax.ShapeDtypeStruct((T + 1, N_GROUPS, pn), F32)] * 2
                  + [jax.ShapeDtypeStruct((T, N_GROUPS, pn), F32)] * 2
                  + [jax.ShapeDtypeStruct((N_GROUPS, pn), F32)] * 4,
        name="s5_params",
    )(*args)

    r5 = lambda a, t: a[:t].reshape(t, N_GROUPS, GROUP, STATE)
    lhs = jnp.concatenate([r5(clre, T), -r5(clim, T)], axis=-1)
    lhs = jnp.transpose(lhs, (1, 0, 2, 3)).reshape(N_GROUPS, T * GROUP, 2 * STATE)
    bb = lambda a: jnp.swapaxes(a.reshape(N_GROUPS, GROUP, STATE), 1, 2)
    rhs = jnp.concatenate([bb(bbre), bb(bbim)], axis=1)
    kmat = pl.pallas_call(
        _s5_kmat_kernel,
        grid=(N_GROUPS,),
        in_specs=[pl.BlockSpec((1, T * GROUP, 2 * STATE), lambda g: (g, 0, 0)),
                  pl.BlockSpec((1, 2 * STATE, GROUP), lambda g: (g, 0, 0))],
        out_specs=pl.BlockSpec((1, T * GROUP, GROUP), lambda g: (g, 0, 0)),
        out_shape=jax.ShapeDtypeStruct((N_GROUPS, T * GROUP, GROUP), F32),
        compiler_params=_cparams("arbitrary"),
        name="s5_kmat",
    )(lhs, rhs)

    eye = jnp.eye(S5_GL, dtype=F32)
    kt = kmat.reshape(S5_SG, S5_GL, T, GROUP, GROUP)
    kbd = jnp.einsum('sgtpq,gh->stgqhp', kt, eye).reshape(S5_SG, T, LANES, LANES)
    krev = kbd[:, ::-1].reshape(S5_SG, T * LANES, LANES).astype(BF16)
    r6 = lambda a: a.reshape(T, S5_SG, S5_GL, GROUP, STATE)
    wb = jnp.stack([r6(wbre), r6(wbim)], axis=-2)
    wb = jnp.einsum('ksgpin,gh->skgpihn', wb, eye).reshape(S5_SG, T * LANES, S5_NSTATE).astype(BF16)
    wc = jnp.stack([r6(clre[1:]), -r6(clim[1:])], axis=-2)
    wc = jnp.einsum('tsgpin,gh->signthp', wc, eye).reshape(S5_SG, S5_NSTATE, T * LANES).astype(BF16)
    lt = lambda a: a.reshape(N_GROUPS, GROUP, STATE)[:, 0].reshape(S5_SG, 1, S5_GL * STATE)
    return krev, wb, wc, lt(ltre), lt(ltim)


S5_TC = 512


def _s5_state_kernel(x_ref, wb_ref, e_ref):
    e_ref[0] = _dot(x_ref[0], wb_ref[0])


def _s5_scan_kernel(e_ref, ltre_ref, ltim_ref, xs_ref):
    lr, li = ltre_ref[0], ltim_ref[0]
    half = S5_NSTATE // 2

    def step(c, carry):
        new = []
        for b in range(BATCH):
            xr, xi = carry[b]
            row = b * S5_CH_PER_BATCH + c
            xs_ref[0, pl.ds(row, 1), :half] = xr
            xs_ref[0, pl.ds(row, 1), half:] = xi
            e = e_ref[0, pl.ds(row, 1), :]
            new.append((lr * xr - li * xi + e[:, :half], lr * xi + li * xr + e[:, half:]))
        return tuple(new)

    zero = jnp.zeros((1, half), F32)
    lax.fori_loop(0, S5_CH_PER_BATCH, step, tuple((zero, zero) for _ in range(BATCH)))


def _s5_out_kernel(x_ref, xs_ref, krev_ref, wc_ref, y_ref):
    xsb = xs_ref[0].astype(BF16)
    for t in range(S5_T):
        cols = slice(t * LANES, (t + 1) * LANES)
        y_ref[0, :, cols] = (_dot(x_ref[0, :, :(t + 1) * LANES], krev_ref[0, (S5_T - 1 - t) * LANES:, :])
                             + _dot(xsb, wc_ref[0, :, cols]))


def _s5(u, krev, wb, wc, ltre, ltim):
    T, tc = S5_T, S5_TC
    xs = u.astype(BF16).reshape(S5_CH, T, S5_SG, LANES).transpose(2, 0, 1, 3).reshape(S5_SG, S5_CH, T * LANES)
    grid = (S5_SG, S5_CH // tc)
    rows = lambda s, i: (s, i, 0)
    per_sg = lambda s, i: (s, 0, 0)
    e = pl.pallas_call(
        _s5_state_kernel, grid=grid,
        in_specs=[pl.BlockSpec((1, tc, T * LANES), rows), pl.BlockSpec((1, T * LANES, S5_NSTATE), per_sg)],
        out_specs=pl.BlockSpec((1, tc, S5_NSTATE), rows),
        out_shape=jax.ShapeDtypeStruct((S5_SG, S5_CH, S5_NSTATE), F32),
        compiler_params=_cparams("arbitrary", "arbitrary"), name="s5_state",
    )(xs, wb)
    sg1 = lambda s: (s, 0, 0)
    xstart = pl.pallas_call(
        _s5_scan_kernel, grid=(S5_SG,),
        in_specs=[pl.BlockSpec((1, S5_CH, S5_NSTATE), sg1),
                  pl.BlockSpec((1, 1, S5_NSTATE // 2), sg1), pl.BlockSpec((1, 1, S5_NSTATE // 2), sg1)],
        out_specs=pl.BlockSpec((1, S5_CH, S5_NSTATE), sg1),
        out_shape=jax.ShapeDtypeStruct((S5_SG, S5_CH, S5_NSTATE), F32),
        compiler_params=_cparams("arbitrary"), name="s5_scan",
    )(e, ltre, ltim)
    y = pl.pallas_call(
        _s5_out_kernel, grid=grid,
        in_specs=[pl.BlockSpec((1, tc, T * LANES), rows), pl.BlockSpec((1, tc, S5_NSTATE), rows),
                  pl.BlockSpec((1, T * LANES, LANES), per_sg), pl.BlockSpec((1, S5_NSTATE, T * LANES), per_sg)],
        out_specs=pl.BlockSpec((1, tc, T * LANES), rows),
        out_shape=jax.ShapeDtypeStruct((S5_SG, S5_CH, T * LANES), F32),
        compiler_params=_cparams("arbitrary", "arbitrary"), name="s5_out",
    )(xs, xstart, krev, wc)
    return y.reshape(S5_SG, S5_CH, T, LANES).transpose(1, 2, 0, 3).reshape(N_TOK, SSM_WIDTH)


MERGE_TM = 256


def _merge_kernel(ocmp_ref, osel_ref, owin_ref, gn_ref, yssm_ref, u_ref, ga_ref, gs_ref, x_ref, mod_ref,
                  eg_ref, dskip_ref, wglu_ref, bglu_ref, wua_ref, wus_ref, wout_ref, g2_ref,
                  wrhi_ref, wrlo_ref, wsgu_ref, wsd_ref,
                  xpart_ref, h2_ref, logit_ref):
    mod = mod_ref[0]
    gnb = gn_ref[...].astype(BF16)
    o_nsa = (_dot(gnb, eg_ref[0]) * ocmp_ref[...].astype(F32)
             + _dot(gnb, eg_ref[1]) * osel_ref[...].astype(F32)
             + _dot(gnb, eg_ref[2]) * owin_ref[...].astype(F32))
    attn = _dot(o_nsa.astype(BF16), wua_ref[...])
    z = _gelu(yssm_ref[...] + dskip_ref[...] * u_ref[...])
    y_ssm = z * jax.nn.sigmoid(_dot(z.astype(BF16), wglu_ref[...]) + bglu_ref[...])
    ssm = _dot(y_ssm.astype(BF16), wus_ref[...])
    merged = ga_ref[...].astype(F32) * attn + gs_ref[...].astype(F32) * ssm
    x1 = x_ref[...] + mod[2:3] * _dot(merged.astype(BF16), wout_ref[...])

    ms = jnp.mean(x1 * x1, axis=-1, keepdims=True)
    h2 = (x1 * lax.rsqrt(ms + EPS) * g2_ref[...]) * (1.0 + mod[4:5]) + mod[3:4]
    hi = h2.astype(BF16)
    lo = (h2 - hi.astype(F32)).astype(BF16)
    for cchunk in range(D_MODEL // LANES):
        h2_ref[:, cchunk, :] = h2[:, cchunk * LANES:(cchunk + 1) * LANES]
    logit_ref[...] = _dot_nt(wrhi_ref[...], hi) + _dot_nt(wrhi_ref[...], lo) + _dot_nt(wrlo_ref[...], hi)
    gu = _dot(hi, wsgu_ref[...])
    shared = _dot((_silu(gu[:, :D_EXPERT]) * gu[:, D_EXPERT:]).astype(BF16), wsd_ref[...])
    xpart_ref[...] = x1 + mod[5:6] * shared


def _merge(ocmp, osel, owin, gn, yssm, u, ga, gs, x2, mod, d_skip, w_glu, b_glu, w_up_attn, w_up_ssm, w_out,
           g_norm2, w_router, ws_gate, ws_up, ws_down):
    tm = MERGE_TM
    eg = np.zeros((3, LANES, NSA_WIDTH), np.float32)
    for j in range(3):
        for h in range(N_HEADS):
            eg[j, 3 * h + j, h * HEAD_DIM:(h + 1) * HEAD_DIM] = 1.0
    wr_t = w_router.T
    wr_hi = wr_t.astype(BF16)
    wr_lo = (wr_t - wr_hi.astype(F32)).astype(BF16)
    row = lambda i: (i, 0)
    fix2 = lambda i: (0, 0)
    wspec = lambda a: pl.BlockSpec(a.shape, (lambda i: (0,) * a.ndim))
    weights = [jnp.asarray(eg, BF16), d_skip.reshape(1, -1), w_glu.astype(BF16), b_glu.reshape(1, -1),
               w_up_attn.astype(BF16), w_up_ssm.astype(BF16), w_out.astype(BF16), g_norm2.reshape(1, -1),
               wr_hi, wr_lo, jnp.concatenate([ws_gate, ws_up], axis=1).astype(BF16), ws_down.astype(BF16)]
    acts = [(ocmp, 512), (osel, 512), (owin, 512), (gn, 128), (yssm, 512), (u, 512), (ga, 1024), (gs, 1024),
            (x2, 1024)]
    return pl.pallas_call(
        _merge_kernel,
        grid=(N_TOK // tm,),
        in_specs=[pl.BlockSpec((tm, wd), row) for _, wd in acts]
                 + [pl.BlockSpec((1, 6, D_MODEL), lambda i: (i // (SEQ // tm), 0, 0))]
                 + [wspec(w) for w in weights],
        out_specs=[pl.BlockSpec((tm, D_MODEL), row),
                   pl.BlockSpec((tm, D_MODEL // LANES, LANES), lambda i: (i, 0, 0)),
                   pl.BlockSpec((N_EXPERTS, tm), lambda i: (0, i))],
        out_shape=[jax.ShapeDtypeStruct((N_TOK, D_MODEL), F32),
                   jax.ShapeDtypeStruct((N_TOK, D_MODEL // LANES, LANES), F32),
                   jax.ShapeDtypeStruct((N_EXPERTS, N_TOK), F32)],
        compiler_params=_cparams("arbitrary"),
        name="merge",
    )(*[a for a, _ in acts], mod, *weights)


ROUTE_TN = 512


def _route_kernel(logit_ref, bias_ref, eidx_ref, w_ref, count_ref, gscore_ref, masked_ref):
    tn = ROUTE_TN

    @pl.when(pl.program_id(0) == 0)
    def _():
        count_ref[...] = jnp.zeros(count_ref.shape, F32)

    sc = jax.nn.sigmoid(logit_ref[...])
    biased = sc + bias_ref[...]
    gi = lax.broadcasted_iota(I32, (EXPERTS_PER_GROUP, tn), 0).astype(F32)
    for g in range(N_EXPERT_GROUPS):
        blk = biased[g * EXPERTS_PER_GROUP:(g + 1) * EXPERTS_PER_GROUP]
        m1 = jnp.max(blk, axis=0, keepdims=True)
        i1 = jnp.min(jnp.where(blk == m1, gi, float(EXPERTS_PER_GROUP)), axis=0, keepdims=True)
        m2 = jnp.max(jnp.where(gi == i1, -jnp.inf, blk), axis=0, keepdims=True)
        gscore_ref[g:g + 1, :] = m1 + m2
    gs = gscore_ref[...]
    gidx = lax.broadcasted_iota(I32, (N_EXPERT_GROUPS, tn), 0)
    grank = jnp.zeros((N_EXPERT_GROUPS, tn), F32)
    for gp in range(N_EXPERT_GROUPS):
        row = gs[gp:gp + 1, :]
        tie = jnp.where(gidx > gp, 1.0, 0.0)
        grank = grank + jnp.where(row > gs, 1.0, jnp.where(row == gs, tie, 0.0))
    for g in range(N_EXPERT_GROUPS):
        keep = grank[g:g + 1, :] < float(TOPK_GROUPS)
        sl = slice(g * EXPERTS_PER_GROUP, (g + 1) * EXPERTS_PER_GROUP)
        masked_ref[sl, :] = jnp.where(keep, biased[sl], -jnp.inf)
    cur = masked_ref[...]
    eidx = lax.broadcasted_iota(I32, (N_EXPERTS, tn), 0).astype(F32)
    wsum = jnp.zeros((1, tn), F32)
    hits = jnp.zeros((N_EXPERTS, tn), F32)
    for k in range(TOP_K):
        m = jnp.max(cur, axis=0, keepdims=True)
        idx = jnp.min(jnp.where(cur == m, eidx, float(N_EXPERTS)), axis=0, keepdims=True)
        hit = eidx == idx
        wk = jnp.sum(jnp.where(hit, sc, 0.0), axis=0, keepdims=True)
        cur = jnp.where(hit, -jnp.inf, cur)
        hits = hits + jnp.where(hit, 1.0, 0.0)
        eidx_ref[k:k + 1, :] = idx.astype(I32)
        w_ref[k:k + 1, :] = wk
        wsum = wsum + wk
    w_ref[...] = w_ref[...] / wsum * ROUTE_SCALE
    count_ref[...] = count_ref[...] + jnp.sum(hits, axis=1, keepdims=True)


def _route(logits_t, router_bias):
    tn = ROUTE_TN
    return pl.pallas_call(
        _route_kernel,
        grid=(N_TOK // tn,),
        in_specs=[pl.BlockSpec((N_EXPERTS, tn), lambda i: (0, i)), pl.BlockSpec((N_EXPERTS, 1), lambda i: (0, 0))],
        out_specs=[pl.BlockSpec((TOP_K, tn), lambda i: (0, i))] * 2 + [pl.BlockSpec((N_EXPERTS, 1), lambda i: (0, 0))],
        out_shape=[jax.ShapeDtypeStruct((TOP_K, N_TOK), I32), jax.ShapeDtypeStruct((TOP_K, N_TOK), F32),
                   jax.ShapeDtypeStruct((N_EXPERTS, 1), F32)],
        scratch_shapes=[pltpu.VMEM((N_EXPERT_GROUPS, tn), F32), pltpu.VMEM((N_EXPERTS, tn), F32)],
        compiler_params=_cparams("arbitrary"),
        name="route",
    )(logits_t, router_bias.reshape(-1, 1))


N_MOE_BLK = NK // DISPATCH_BLOCK
N_ITEMS = N_MOE_BLK + N_EXPERTS
ASSIGN_BITS = 17


def _dispatch_plan(eidx, counts):
    e_flat = eidx.reshape(-1)
    key = jnp.sort(e_flat * NK + jnp.arange(NK, dtype=I32))
    order = key & (NK - 1)
    counts = counts.reshape(-1).astype(I32)
    start = jnp.cumsum(counts) - counts
    cuts = jnp.sort(jnp.concatenate([jnp.arange(N_MOE_BLK, dtype=I32) * DISPATCH_BLOCK, start]))
    lo = cuts
    hi = jnp.concatenate([cuts[1:], jnp.full((1,), NK, I32)])
    blk = jnp.minimum(lo // DISPATCH_BLOCK, N_MOE_BLK - 1)
    expert = jnp.clip(jnp.sum((start[None, :] <= lo[:, None]).astype(I32), axis=1) - 1, 0, N_EXPERTS - 1)
    first = jnp.concatenate([jnp.ones((1,), I32), (blk[1:] != blk[:-1]).astype(I32)])
    last = jnp.concatenate([(blk[1:] != blk[:-1]).astype(I32), jnp.ones((1,), I32)])
    return (order.reshape(N_MOE_BLK, 1, DISPATCH_BLOCK), blk, expert,
            lo - blk * DISPATCH_BLOCK, hi - blk * DISPATCH_BLOCK, first, last)


def _moe_row_copy(ybuf, slots_hbm, sem, slot, i, dst_row):
    return pltpu.make_async_copy(ybuf.at[slot, i], slots_hbm.at[dst_row], sem.at[slot])


def _moe_gather_copy(h2_hbm, xg, gsem, slot, i, tok):
    return pltpu.make_async_copy(h2_hbm.at[tok], xg.at[slot, i], gsem.at[slot])


def _moe_kernel(blk_ref, exp_ref, lo_ref, hi_ref, first_ref, last_ref,
                ord_ref, ordn_ref, h2_hbm, wg_ref, wu_ref, wd_ref, slots_hbm,
                xg, ybuf, gsem, sem):
    it = pl.program_id(0)
    b = blk_ref[it]
    slot = b % 2
    rows = DISPATCH_BLOCK

    def start_gather(order_ref, s):
        for i in range(rows):
            tok = lax.shift_right_logical(order_ref[0, 0, i], 3)
            _moe_gather_copy(h2_hbm, xg, gsem, s, i, tok).start()

    @pl.when(it == 0)
    def _():
        start_gather(ord_ref, 0)

    @pl.when(first_ref[it] == 1)
    def _():
        @pl.when(b >= 2)
        def _():
            for i in range(rows):
                _moe_row_copy(ybuf, slots_hbm, sem, slot, i, 0).wait()
        for i in range(rows):
            _moe_gather_copy(h2_hbm, xg, gsem, slot, i, 0).wait()

        @pl.when(b + 1 < N_MOE_BLK)
        def _():
            start_gather(ordn_ref, 1 - slot)
        ybuf[slot] = jnp.zeros(ybuf.shape[1:], F32)

    lo, hi = lo_ref[it], hi_ref[it]

    @pl.when(hi > lo)
    def _():
        ridx = lax.broadcasted_iota(I32, (rows, LANES), 0)
        mine = (ridx >= lo) & (ridx < hi)
        x = jnp.concatenate([jnp.where(mine, xg[slot, :, c, :], 0.0).astype(BF16)
                             for c in range(D_MODEL // LANES)], axis=1)
        gate = _dot(x, wg_ref[0].astype(BF16))
        up = _dot(x, wu_ref[0].astype(BF16))
        y = _dot((_silu(gate) * up).astype(BF16), wd_ref[0].astype(BF16))
        for c in range(D_MODEL // LANES):
            ybuf[slot, :, c, :] = ybuf[slot, :, c, :] + y[:, c * LANES:(c + 1) * LANES]

    @pl.when(last_ref[it] == 1)
    def _():
        for i in range(rows):
            _moe_row_copy(ybuf, slots_hbm, sem, slot, i, ord_ref[0, 0, i]).start()

    @pl.when(it == pl.num_programs(0) - 1)
    def _():
        for s in range(2):
            for i in range(rows):
                _moe_row_copy(ybuf, slots_hbm, sem, s, i, 0).wait()


def _moe(h2t, plan, w_gate, w_up, w_down):
    order3, blk, expert, lo, hi, first, last = plan
    by_blk = lambda it, blk, *_: (blk[it], 0, 0)
    by_next = lambda it, blk, *_: (jnp.minimum(blk[it] + 1, N_MOE_BLK - 1), 0, 0)
    by_exp = lambda it, blk, ex, *_: (ex[it], 0, 0)
    grid_spec = pltpu.PrefetchScalarGridSpec(
        num_scalar_prefetch=6,
        grid=(N_ITEMS,),
        in_specs=[pl.BlockSpec((1, 1, DISPATCH_BLOCK), by_blk, memory_space=pltpu.SMEM),
                  pl.BlockSpec((1, 1, DISPATCH_BLOCK), by_next, memory_space=pltpu.SMEM),
                  pl.BlockSpec(memory_space=pl.ANY),
                  pl.BlockSpec((1, D_MODEL, D_EXPERT), by_exp),
                  pl.BlockSpec((1, D_MODEL, D_EXPERT), by_exp),
                  pl.BlockSpec((1, D_EXPERT, D_MODEL), by_exp)],
        out_specs=pl.BlockSpec(memory_space=pl.ANY),
        scratch_shapes=[pltpu.VMEM((2, DISPATCH_BLOCK, D_MODEL // LANES, LANES), F32),
                        pltpu.VMEM((2, DISPATCH_BLOCK, D_MODEL // LANES, LANES), F32),
                        pltpu.SemaphoreType.DMA((2,)), pltpu.SemaphoreType.DMA((2,))],
    )
    return pl.pallas_call(
        _moe_kernel,
        grid_spec=grid_spec,
        out_shape=jax.ShapeDtypeStruct((NK, D_MODEL // LANES, LANES), F32),
        compiler_params=_cparams("arbitrary"),
        name="moe",
    )(blk, expert, lo, hi, first, last, order3, order3, h2t, w_gate, w_up, w_down)


COMB_TC = 64


def _combine_kernel(slots_ref, w_ref, xpart_ref, mod_ref, out_ref):
    w = w_ref[...]
    gate2 = mod_ref[0][5:6]
    for c in range(D_MODEL // LANES):
        cols = slice(c * LANES, (c + 1) * LANES)
        routed = w[:, 0:1] * slots_ref[:, 0, c, :]
        for k in range(1, TOP_K):
            routed = routed + w[:, k:k + 1] * slots_ref[:, k, c, :]
        out_ref[:, cols] = xpart_ref[:, cols] + gate2[:, cols] * routed


def _combine(xpart, mod, slots, w):
    tc = COMB_TC
    row = lambda i: (i, 0)
    return pl.pallas_call(
        _combine_kernel,
        grid=(N_TOK // tc,),
        in_specs=[pl.BlockSpec((tc, TOP_K, D_MODEL // LANES, LANES), lambda i: (i, 0, 0, 0)),
                  pl.BlockSpec((tc, TOP_K), row),
                  pl.BlockSpec((tc, D_MODEL), row),
                  pl.BlockSpec((1, 6, D_MODEL), lambda i: (i // (SEQ // tc), 0, 0))],
        out_specs=pl.BlockSpec((tc, D_MODEL), row),
        out_shape=jax.ShapeDtypeStruct((N_TOK, D_MODEL), F32),
        compiler_params=_cparams("arbitrary"),
        name="combine",
    )(slots.reshape(N_TOK, TOP_K, D_MODEL // LANES, LANES), w, xpart, mod)


def _layer(x, c, w_ada, b_ada, g_norm1, g_norm2, w_in, q_gain, kc_gain, ks_gain, kw_gain,
           pe_k, pe_v, w_cmp_k1, w_cmp_k2, w_cmp_v1, w_cmp_v2,
           a_re, a_im, log_dt, b_re, b_im, c_re, c_im, d_skip, w_glu, b_glu,
           w_up_attn, w_up_ssm, w_out, w_router, router_bias,
           w_gate, w_up, w_down, ws_gate, ws_up, ws_down):
    x2 = x.reshape(N_TOK, D_MODEL)
    mod = _ada(c, w_ada, b_ada)
    q, kc_raw, vc_raw, ks, kw, vst, vwt, gn, u, ga, gs = _proj(x2, mod, g_norm1, w_in, q_gain, ks_gain, kw_gain)
    kcn = _compress(kc_raw, pe_k, w_cmp_k1, w_cmp_k2, kc_gain, True)
    vcn = _compress(vc_raw, pe_v, w_cmp_v1, w_cmp_v2, kc_gain, False)
    ocmp, selb = _cmp_attn(q, kcn, vcn)
    osel, owin = _selwin(q, ks, kw, vst, vwt, selb)
    yssm = _s5(u, *_s5_params(a_re, a_im, log_dt, b_re, b_im, c_re, c_im))
    xpart, h2p, logits_t = _merge(ocmp, osel, owin, gn, yssm, u, ga, gs, x2, mod, d_skip, w_glu, b_glu,
                                  w_up_attn, w_up_ssm, w_out, g_norm2, w_router, ws_gate, ws_up, ws_down)
    eidx_t, w_t, counts = _route(logits_t, router_bias)
    slots = _moe(h2p, _dispatch_plan(eidx_t.T, counts), w_gate, w_up, w_down)
    return _combine(xpart, mod, slots, w_t.T).reshape(BATCH, SEQ, D_MODEL)


def kernel(x, c, w_ada, b_ada, g_norm1, g_norm2, w_in, q_gain, kc_gain, ks_gain, kw_gain, pe_k, pe_v, w_cmp_k1,
           w_cmp_k2, w_cmp_v1, w_cmp_v2, a_re, a_im, log_dt, b_re, b_im, c_re, c_im, d_skip, w_glu, b_glu,
           w_up_attn, w_up_ssm, w_out, w_router, router_bias, w_gate, w_up, w_down, ws_gate, ws_up, ws_down):
    params = (w_ada, b_ada, g_norm1, g_norm2, w_in, q_gain, kc_gain, ks_gain, kw_gain, pe_k, pe_v, w_cmp_k1,
              w_cmp_k2, w_cmp_v1, w_cmp_v2, a_re, a_im, log_dt, b_re, b_im, c_re, c_im, d_skip, w_glu, b_glu,
              w_up_attn, w_up_ssm, w_out, w_router, router_bias, w_gate, w_up, w_down, ws_gate, ws_up, ws_down)
    depth = w_ada.shape[0]
    for layer in range(depth):
        x = _layer(x, c, *[p[layer] for p in params])
    return x
```

```python
import functools
import math

import jax
import jax.numpy as jnp
import numpy as np
from jax import lax
from jax.experimental import pallas as pl
from jax.experimental.pallas import tpu as pltpu
from jax.experimental.pallas import tpu_sc as plsc

F32 = jnp.float32
BF16 = jnp.bfloat16
I32 = jnp.int32
HIGHEST = lax.Precision.HIGHEST

D_MODEL = 1024
BATCH = 4
SEQ = 4096
N_TOK = BATCH * SEQ
N_HEADS = 8
HEAD_DIM = 64
N_KV = 2
CMP_BLOCK = 32
CMP_STRIDE = 16
CMP_HIDDEN = 256
N_CMP = 256
SEL_BLOCK = 64
N_SEL_BLOCKS = SEQ // SEL_BLOCK
N_SELECT = 16
WINDOW = 512
ATTN_SCALE = HEAD_DIM ** -0.5
NSA_WIDTH = N_HEADS * HEAD_DIM
SSM_WIDTH = 512
GROUP = 16
N_GROUPS = SSM_WIDTH // GROUP
STATE = 64
N_EXPERTS = 256
TOP_K = 8
D_EXPERT = 256
N_EXPERT_GROUPS = 8
EXPERTS_PER_GROUP = N_EXPERTS // N_EXPERT_GROUPS
TOPK_GROUPS = 4
ROUTE_SCALE = 2.5
DISPATCH_BLOCK = 128
EPS = 1e-6
NEG = -1e30

LANES = 128
S5_T = 16
S5_SG = 4
S5_GL = N_GROUPS // S5_SG
S5_CH = N_TOK // S5_T
S5_CH_PER_BATCH = SEQ // S5_T
S5_NSTATE = S5_GL * STATE * 2

NK = N_TOK * TOP_K
CAP = (NK + N_EXPERTS * DISPATCH_BLOCK + DISPATCH_BLOCK - 1) // DISPATCH_BLOCK * DISPATCH_BLOCK
N_BLK = CAP // DISPATCH_BLOCK

VMEM_LIMIT = 48 * 1024 * 1024


def _cparams(*sem):
    return pltpu.CompilerParams(dimension_semantics=tuple(sem), vmem_limit_bytes=VMEM_LIMIT)


def _dot(a, b):
    return jnp.dot(a, b, preferred_element_type=F32)


def _dot_nt(a, b):
    return lax.dot_general(a, b, (((1,), (1,)), ((), ())), preferred_element_type=F32)


def _split_dot(v, w):
    hi = v.astype(BF16)
    lo = (v - hi.astype(F32)).astype(BF16)
    return _dot(hi, w) + _dot(lo, w)


def _seg_rms(v, bd, gain):
    ss = _split_dot(v * v, bd)
    return v * lax.rsqrt(ss * (1.0 / HEAD_DIM) + EPS) * gain


def _gelu(x):
    return 0.5 * x * (1.0 + jnp.tanh(0.7978845608028654 * (x + 0.044715 * (x * x * x))))


def _silu(x):
    return x * jax.nn.sigmoid(x)


def _ada_kernel(c_ref, w_ref, b_ref, o_ref):
    c = c_ref[...]
    o_ref[...] = jnp.dot(_silu(c), w_ref[...], preferred_element_type=F32, precision=HIGHEST) + b_ref[...]


def _ada(c, w_ada, b_ada):
    cp = jnp.pad(c, ((0, 8 - BATCH), (0, 0)))
    tn = 1536
    out = pl.pallas_call(
        _ada_kernel,
        grid=(6 * D_MODEL // tn,),
        in_specs=[pl.BlockSpec((8, D_MODEL), lambda j: (0, 0)),
                  pl.BlockSpec((D_MODEL, tn), lambda j: (0, j)),
                  pl.BlockSpec((1, tn), lambda j: (0, j))],
        out_specs=pl.BlockSpec((8, tn), lambda j: (0, j)),
        out_shape=jax.ShapeDtypeStruct((8, 6 * D_MODEL), F32),
        compiler_params=_cparams("arbitrary"),
        name="ada",
    )(cp, w_ada, b_ada.reshape(1, -1))
    return out.reshape(8, 6, D_MODEL)


_C_Q = 0
_C_KC = 512
_C_VC = 640
_C_KS = 768
_C_KW = 1024
_C_GN = 1280
_C_U = 1408
_C_GA = 1920
_C_GS = 2944
_C_END = 3968
PROJ_TM = 512


def _proj_kernel(x_ref, mod_ref, g1_ref, w_ref, wvt_ref, qg_ref, ksg_ref, kwg_ref, bd512_ref, bd256_ref,
                 q_ref, kc_ref, vc_ref, ks_ref, kw_ref, vst_ref, vwt_ref, gn_ref, u_ref, ga_ref, gs_ref):
    x = x_ref[...]
    ms = jnp.mean(x * x, axis=-1, keepdims=True)
    mod = mod_ref[0]
    h = (x * lax.rsqrt(ms + EPS) * g1_ref[...]) * (1.0 + mod[1:2]) + mod[0:1]
    hb = h.astype(BF16)

    def p(lo, hi):
        return _dot(hb, w_ref[:, lo:hi])

    q_ref[...] = _seg_rms(p(_C_Q, _C_KC), bd512_ref[...], qg_ref[...] * ATTN_SCALE).astype(BF16)
    kc_ref[...] = p(_C_KC, _C_VC).astype(BF16)
    vc_ref[...] = p(_C_VC, _C_KS).astype(BF16)
    ks_ref[...] = _seg_rms(p(_C_KS, _C_KW), bd256_ref[...], ksg_ref[...]).astype(BF16)
    kw_ref[...] = _seg_rms(p(_C_KW, _C_GN), bd256_ref[...], kwg_ref[...]).astype(BF16)
    vt = _dot_nt(wvt_ref[...], hb)
    vst_ref[...] = vt[:LANES].astype(BF16)
    vwt_ref[...] = vt[LANES:].astype(BF16)
    gn_ref[...] = jax.nn.sigmoid(p(_C_GN, _C_U))
    u_ref[...] = p(_C_U, _C_GA)
    ga_ref[...] = jax.nn.sigmoid(p(_C_GA, _C_GS)).astype(BF16)
    gs_ref[...] = jax.nn.sigmoid(p(_C_GS, _C_END)).astype(BF16)


def _dup_cols(w):
    return jnp.concatenate([w[:, :64], w[:, :64], w[:, 64:], w[:, 64:]], axis=1)


def _block_ones(n):
    return jnp.kron(jnp.eye(n // HEAD_DIM, dtype=F32), jnp.ones((HEAD_DIM, HEAD_DIM), F32)).astype(BF16)


def _proj(x2, mod, g_norm1, w_in, q_gain, ks_gain, kw_gain):
    o = np.cumsum((0, 512, 128, 128, 128, 128, 128, 128, 24, 512, 1024, 1024))
    parts = [w_in[:, o[i]:o[i + 1]] for i in range(11)]
    wq, wkc, wvc, wks, wvs, wkw, wvw, wgn, wu, wga, wgs = parts
    w = jnp.concatenate([wq, wkc, wvc, _dup_cols(wks), _dup_cols(wkw),
                         jnp.pad(wgn, ((0, 0), (0, LANES - 24))), wu, wga, wgs], axis=1).astype(BF16)
    wvt = jnp.concatenate([wvs, wvw], axis=1).T.astype(BF16)
    tm = PROJ_TM
    row = lambda i: (i, 0)
    col = lambda i: (0, i)
    fix = lambda i: (0, 0)
    outs = [(512, BF16, row), (128, BF16, row), (128, BF16, row), (256, BF16, row), (256, BF16, row),
            (LANES, BF16, col), (LANES, BF16, col),
            (128, F32, row), (512, F32, row), (1024, BF16, row), (1024, BF16, row)]
    ospec = lambda wd, m: pl.BlockSpec((tm, wd), m) if m is row else pl.BlockSpec((wd, tm), m)
    oshape = lambda wd, dt, m: jax.ShapeDtypeStruct((N_TOK, wd) if m is row else (wd, N_TOK), dt)
    return pl.pallas_call(
        _proj_kernel,
        grid=(N_TOK // tm,),
        in_specs=[pl.BlockSpec((tm, D_MODEL), row),
                  pl.BlockSpec((1, 6, D_MODEL), lambda i: (i // (SEQ // tm), 0, 0)),
                  pl.BlockSpec((1, D_MODEL), fix),
                  pl.BlockSpec((D_MODEL, _C_END), fix),
                  pl.BlockSpec((2 * LANES, D_MODEL), fix),
                  pl.BlockSpec((1, 512), fix), pl.BlockSpec((1, 256), fix), pl.BlockSpec((1, 256), fix),
                  pl.BlockSpec((512, 512), fix), pl.BlockSpec((256, 256), fix)],
        out_specs=[ospec(wd, m) for wd, _, m in outs],
        out_shape=[oshape(wd, dt, m) for wd, dt, m in outs],
        compiler_params=_cparams("arbitrary"),
        name="proj",
    )(x2, mod, g_norm1.reshape(1, -1), w, wvt,
      jnp.tile(q_gain, N_HEADS).reshape(1, -1), jnp.tile(ks_gain, 4).reshape(1, -1),
      jnp.tile(kw_gain, 4).reshape(1, -1), _block_ones(512), _block_ones(256))


def _compress_kernel(r_ref, pe_ref, w1_ref, w2_ref, bd_ref, gain_ref, o_ref, *, do_norm):
    r = r_ref[0].astype(F32)
    p0 = _dot((r + pe_ref[0]).astype(BF16), w1_ref[0])
    p1 = _dot((r + pe_ref[1]).astype(BF16), w1_ref[1])
    hid = p0 + pltpu.roll(p1, N_CMP - 1, 0)
    c = _dot(_gelu(hid).astype(BF16), w2_ref[...])
    if do_norm:
        c = _seg_rms(c, bd_ref[...], gain_ref[...])
    o_ref[0] = c.astype(BF16)


def _compress(raw, pe, w1, w2, gain, do_norm):
    r = raw.reshape(BATCH, SEQ // CMP_STRIDE, CMP_STRIDE * LANES)
    eye = jnp.eye(N_KV, dtype=F32)
    w1r = w1.reshape(2, CMP_STRIDE, HEAD_DIM, CMP_HIDDEN)
    w1big = jnp.einsum('hldc,gk->hlgdkc', w1r, eye).reshape(2, CMP_STRIDE * LANES, N_KV * CMP_HIDDEN).astype(BF16)
    w2big = jnp.einsum('cd,gk->gckd', w2, eye)
    w2big = jnp.concatenate([w2big, w2big], axis=-1).reshape(N_KV * CMP_HIDDEN, 4 * HEAD_DIM).astype(BF16)
    pe_big = jnp.broadcast_to(pe.reshape(2, CMP_STRIDE, 1, HEAD_DIM), (2, CMP_STRIDE, N_KV, HEAD_DIM))
    pe_big = pe_big.reshape(2, 1, CMP_STRIDE * LANES)
    fix2 = lambda b: (0, 0)
    fix3 = lambda b: (0, 0, 0)
    return pl.pallas_call(
        functools.partial(_compress_kernel, do_norm=do_norm),
        grid=(BATCH,),
        in_specs=[pl.BlockSpec((1, N_CMP, CMP_STRIDE * LANES), lambda b: (b, 0, 0)),
                  pl.BlockSpec((2, 1, CMP_STRIDE * LANES), fix3),
                  pl.BlockSpec((2, CMP_STRIDE * LANES, N_KV * CMP_HIDDEN), fix3),
                  pl.BlockSpec((N_KV * CMP_HIDDEN, 256), fix2),
                  pl.BlockSpec((256, 256), fix2), pl.BlockSpec((1, 256), fix2)],
        out_specs=pl.BlockSpec((1, N_CMP, 256), lambda b: (b, 0, 0)),
        out_shape=jax.ShapeDtypeStruct((BATCH, N_CMP, 256), BF16),
        compiler_params=_cparams("arbitrary"),
        name="compress_k" if do_norm else "compress_v",
    )(r, pe_big, w1big, w2big, _block_ones(256), jnp.tile(gain, 4).reshape(1, -1))


ATT_TQ = 256


def _head_variants(qb):
    lane = lax.broadcasted_iota(I32, qb.shape, 1)
    z = jnp.zeros_like(qb)
    return jnp.where(lane < HEAD_DIM, qb, z), jnp.where(lane < HEAD_DIM, z, qb)


def _cmp_kernel(q_ref, kc_ref, vc_ref, ov_ref, o_ref, sel_ref):
    tq = ATT_TQ
    qi = pl.program_id(1)
    tpos = qi * tq + lax.broadcasted_iota(I32, (tq, N_CMP), 0)
    nidx = lax.broadcasted_iota(I32, (tq, N_CMP), 1)
    mask = (CMP_STRIDE * nidx + (CMP_BLOCK - 1)) <= tpos
    lane_lo = lax.broadcasted_iota(I32, (tq, LANES), 1) < HEAD_DIM
    for g in range(N_KV):
        kd = kc_ref[0, :, g * LANES:(g + 1) * LANES]
        vd = vc_ref[0, :, g * LANES:(g + 1) * LANES]
        psum = jnp.zeros((tq, N_CMP), F32)
        for jb in range(2):
            blk = 2 * g + jb
            pv = []
            for qv in _head_variants(q_ref[:, blk * LANES:(blk + 1) * LANES]):
                s = jnp.where(mask, _dot_nt(qv, kd), NEG)
                m = jnp.max(s, axis=-1, keepdims=True)
                e = jnp.where(mask, jnp.exp(s - m), 0.0)
                l = jnp.sum(e, axis=-1, keepdims=True)
                p = e / jnp.where(l > 0.0, l, 1.0)
                psum = psum + p
                pv.append(_dot(p.astype(BF16), vd))
            o_ref[:, blk * LANES:(blk + 1) * LANES] = jnp.where(lane_lo, pv[0], pv[1]).astype(BF16)
        imp = _split_dot(psum, ov_ref[...])
        imp_t = imp.T[:N_SEL_BLOCKS]
        j = lax.broadcasted_iota(I32, (N_SEL_BLOCKS, tq), 0)
        cur = jnp.right_shift(qi * tq + lax.broadcasted_iota(I32, (N_SEL_BLOCKS, tq), 1), 6)
        forced = (j == 0) | (j == cur) | (j == cur - 1)
        v = jnp.where(forced, jnp.inf, jnp.where(j <= cur, imp_t, -jnp.inf))
        rank = jnp.zeros((N_SEL_BLOCKS, tq), F32)
        for jp in range(N_SEL_BLOCKS):
            row = v[jp:jp + 1, :]
            tie = jnp.where(j > jp, 1.0, 0.0)
            rank = rank + jnp.where(row > v, 1.0, jnp.where(row == v, tie, 0.0))
        sel_ref[g * N_SEL_BLOCKS:(g + 1) * N_SEL_BLOCKS, :] = jnp.where(rank < float(N_SELECT), 0.0, NEG)


def _cmp_attn(q, kcn, vcn):
    nc = np.arange(N_CMP)
    sb = np.arange(LANES)
    ov = ((CMP_STRIDE * nc[:, None] < SEL_BLOCK * sb[None, :] + SEL_BLOCK)
          & (CMP_STRIDE * nc[:, None] + CMP_BLOCK > SEL_BLOCK * sb[None, :])
          & (nc[:, None] < N_CMP - 1) & (sb[None, :] < N_SEL_BLOCKS))
    ov = jnp.asarray(ov, BF16)
    tq = ATT_TQ
    nq = SEQ // tq
    row = lambda b, i: (b * nq + i, 0)
    return pl.pallas_call(
        _cmp_kernel,
        grid=(BATCH, nq),
        in_specs=[pl.BlockSpec((tq, NSA_WIDTH), row),
                  pl.BlockSpec((1, N_CMP, 256), lambda b, i: (b, 0, 0)),
                  pl.BlockSpec((1, N_CMP, 256), lambda b, i: (b, 0, 0)),
                  pl.BlockSpec((N_CMP, LANES), lambda b, i: (0, 0))],
        out_specs=[pl.BlockSpec((tq, NSA_WIDTH), row),
                   pl.BlockSpec((N_KV * N_SEL_BLOCKS, tq), lambda b, i: (0, b * nq + i))],
        out_shape=[jax.ShapeDtypeStruct((N_TOK, NSA_WIDTH), BF16),
                   jax.ShapeDtypeStruct((N_KV * N_SEL_BLOCKS, N_TOK), F32)],
        compiler_params=_cparams("arbitrary", "arbitrary"),
        name="cmp_attn",
    )(q, kcn, vcn, ov)


ATT_TK = 256


M_INIT = -1e29


def _selwin_kernel(q_ref, ks_ref, kw_ref, vst_ref, vwt_ref, selb_ref, osel_ref, owin_ref, m_ref, l_ref, acc_ref):
    tq, tk = ATT_TQ, ATT_TK
    qi = pl.program_id(1)
    krow = lax.broadcasted_iota(I32, (tk, tq), 0)
    qcol = lax.broadcasted_iota(I32, (tk, tq), 1)
    causal_bias = jnp.where(krow <= qcol, 0.0, NEG)
    far_bias = jnp.where(qcol < krow, 0.0, NEG)

    def reset():
        m_ref[...] = jnp.full(m_ref.shape, M_INIT, F32)
        l_ref[...] = jnp.zeros(l_ref.shape, F32)
        acc_ref[...] = jnp.zeros(acc_ref.shape, F32)

    def update(qvars, kd, vt, bias):
        s = _dot_nt(kd, qvars)
        if bias is not None:
            s = s + jnp.concatenate([bias] * 4, axis=1)
        m_old = m_ref[...]
        m_new = jnp.maximum(m_old, jnp.max(s, axis=0, keepdims=True))
        alpha = jnp.exp(m_old - m_new)
        p = jnp.exp(s - m_new)
        l_ref[...] = alpha * l_ref[...] + jnp.sum(p, axis=0, keepdims=True)
        m_ref[...] = m_new
        acc_ref[...] = alpha * acc_ref[...] + _dot(vt, p.astype(BF16))

    def finish(out_ref, g):
        o = acc_ref[...] / l_ref[...]
        for jb in range(2):
            blk = 2 * g + jb
            pair = jnp.concatenate([o[:, 2 * jb * tq:(2 * jb + 1) * tq], o[:, (2 * jb + 1) * tq:(2 * jb + 2) * tq]],
                                   axis=0)
            out_ref[:, blk * LANES:(blk + 1) * LANES] = pair.T.astype(BF16)

    def sel_bias(g, kt):
        rows = [jnp.broadcast_to(selb_ref[pl.ds(g * N_SEL_BLOCKS + kt * (tk // SEL_BLOCK) + r, 1), :],
                                 (SEL_BLOCK, tq)) for r in range(tk // SEL_BLOCK)]
        return jnp.concatenate(rows, axis=0)

    for g in range(N_KV):
        gl = slice(g * LANES, (g + 1) * LANES)
        gv = slice(g * HEAD_DIM, (g + 1) * HEAD_DIM)
        qvars = []
        for jb in range(2):
            qvars.extend(_head_variants(q_ref[:, (2 * g + jb) * LANES:(2 * g + jb + 1) * LANES]))
        qvars = jnp.concatenate(qvars, axis=0)

        def k_tile(ref, kt):
            return ref[0, pl.ds(pl.multiple_of(kt * tk, tk), tk), gl]

        def v_tile(ref, kt):
            return ref[gv, pl.ds(pl.multiple_of(kt * tk, tk), tk)]

        reset()

        def sel_step(kt, carry):
            update(qvars, k_tile(ks_ref, kt), v_tile(vst_ref, kt), sel_bias(g, kt))
            return carry

        lax.fori_loop(0, qi, sel_step, 0)
        update(qvars, k_tile(ks_ref, qi), v_tile(vst_ref, qi), sel_bias(g, qi) + causal_bias)
        finish(osel_ref, g)

        reset()

        @pl.when(qi >= 2)
        def _():
            update(qvars, k_tile(kw_ref, qi - 2), v_tile(vwt_ref, qi - 2), far_bias)

        @pl.when(qi >= 1)
        def _():
            update(qvars, k_tile(kw_ref, qi - 1), v_tile(vwt_ref, qi - 1), None)

        update(qvars, k_tile(kw_ref, qi), v_tile(vwt_ref, qi), causal_bias)
        finish(owin_ref, g)


def _selwin(q, ks, kw, vst, vwt, selb):
    tq = ATT_TQ
    nq = SEQ // tq
    assert WINDOW == 2 * ATT_TK and ATT_TQ == ATT_TK
    row = lambda b, i: (b * nq + i, 0)
    keys = pl.BlockSpec((1, SEQ, 256), lambda b, i: (b, 0, 0))
    vals = pl.BlockSpec((LANES, SEQ), lambda b, i: (0, b))
    r3 = lambda a: a.reshape(BATCH, SEQ, 256)
    return pl.pallas_call(
        _selwin_kernel,
        grid=(BATCH, nq),
        in_specs=[pl.BlockSpec((tq, NSA_WIDTH), row), keys, keys, vals, vals,
                  pl.BlockSpec((N_KV * N_SEL_BLOCKS, tq), lambda b, i: (0, b * nq + i))],
        out_specs=[pl.BlockSpec((tq, NSA_WIDTH), row)] * 2,
        out_shape=[jax.ShapeDtypeStruct((N_TOK, NSA_WIDTH), BF16)] * 2,
        scratch_shapes=[pltpu.VMEM((1, 4 * tq), F32), pltpu.VMEM((1, 4 * tq), F32),
                        pltpu.VMEM((HEAD_DIM, 4 * tq), F32)],
        compiler_params=_cparams("arbitrary", "arbitrary"),
        name="selwin",
    )(q, r3(ks), r3(kw), vst, vwt, selb)


def _s5_param_kernel(are_ref, aim_ref, ldt_ref, cre_ref, cim_ref, bre_ref, bim_ref,
                     clre_ref, clim_ref, wbre_ref, wbim_ref, bbre_ref, bbim_ref, ltre_ref, ltim_ref):
    are, aim = are_ref[...], aim_ref[...]
    dt = jnp.exp(ldt_ref[...])
    cre, cim = cre_ref[...], cim_ref[...]

    def lam_pow(tau):
        mag = jnp.exp(are * dt * float(tau))
        ang = aim * dt * float(tau)
        return mag * jnp.cos(ang), mag * jnp.sin(ang)

    lre, lim = lam_pow(1)
    den = are * are + aim * aim
    qre = ((lre - 1.0) * are + lim * aim) / den
    qim = (lim * are - (lre - 1.0) * aim) / den
    bre, bim = bre_ref[...], bim_ref[...]
    bbre = qre * bre - qim * bim
    bbim = qre * bim + qim * bre
    bbre_ref[...] = bbre
    bbim_ref[...] = bbim
    for tau in range(S5_T + 1):
        pr, pi = lam_pow(tau)
        clre_ref[tau] = cre * pr - cim * pi
        clim_ref[tau] = cre * pi + cim * pr
        if tau < S5_T:
            k = S5_T - 1 - tau
            wbre_ref[k] = pr * bbre - pi * bbim
            wbim_ref[k] = pr * bbim + pi * bbre
        else:
            ltre_ref[...] = pr
            ltim_ref[...] = pi


def _s5_kmat_kernel(l_ref, r_ref, o_ref):
    o_ref[0] = jnp.dot(l_ref[0], r_ref[0], preferred_element_type=F32, precision=HIGHEST)


def _s5_params(a_re, a_im, log_dt, b_re, b_im, c_re, c_im):
    T = S5_T
    pn = GROUP * STATE
    tile_p = lambda a: jnp.tile(a, (1, GROUP))
    args = (tile_p(a_re), tile_p(a_im), jnp.broadcast_to(log_dt[:, None], (N_GROUPS, pn)),
            c_re.reshape(N_GROUPS, pn), c_im.reshape(N_GROUPS, pn),
            jnp.swapaxes(b_re, 1, 2).reshape(N_GROUPS, pn), jnp.swapaxes(b_im, 1, 2).reshape(N_GROUPS, pn))
    full2 = pl.BlockSpec((N_GROUPS, pn), lambda: (0, 0))
    clre, clim, wbre, wbim, bbre, bbim, ltre, ltim = pl.pallas_call(
        _s5_param_kernel,
        in_specs=[full2] * 7,
        out_specs=[pl.BlockSpec((T + 1, N_GROUPS, pn), lambda: (0, 0, 0))] * 2
                  + [pl.BlockSpec((T, N_GROUPS, pn), lambda: (0, 0, 0))] * 2 + [full2] * 4,
        out_shape=[jax.ShapeDtypeStruct((T + 1, N_GROUPS, pn), F32)] * 2
                  + [jax.ShapeDtypeStruct((T, N_GROUPS, pn), F32)] * 2
                  + [jax.ShapeDtypeStruct((N_GROUPS, pn), F32)] * 4,
        name="s5_params",
    )(*args)

    r5 = lambda a, t: a[:t].reshape(t, N_GROUPS, GROUP, STATE)
    lhs = jnp.concatenate([r5(clre, T), -r5(clim, T)], axis=-1)
    lhs = jnp.transpose(lhs, (1, 0, 2, 3)).reshape(N_GROUPS, T * GROUP, 2 * STATE)
    bb = lambda a: jnp.swapaxes(a.reshape(N_GROUPS, GROUP, STATE), 1, 2)
    rhs = jnp.concatenate([bb(bbre), bb(bbim)], axis=1)
    kmat = pl.pallas_call(
        _s5_kmat_kernel,
        grid=(N_GROUPS,),
        in_specs=[pl.BlockSpec((1, T * GROUP, 2 * STATE), lambda g: (g, 0, 0)),
                  pl.BlockSpec((1, 2 * STATE, GROUP), lambda g: (g, 0, 0))],
        out_specs=pl.BlockSpec((1, T * GROUP, GROUP), lambda g: (g, 0, 0)),
        out_shape=jax.ShapeDtypeStruct((N_GROUPS, T * GROUP, GROUP), F32),
        compiler_params=_cparams("arbitrary"),
        name="s5_kmat",
    )(lhs, rhs)

    eye = jnp.eye(S5_GL, dtype=F32)
    kt = kmat.reshape(S5_SG, S5_GL, T, GROUP, GROUP)
    kbd = jnp.einsum('sgtpq,gh->stgqhp', kt, eye).reshape(S5_SG, T, LANES, LANES)
    krev = kbd[:, ::-1].reshape(S5_SG, T * LANES, LANES).astype(BF16)
    r6 = lambda a: a.reshape(T, S5_SG, S5_GL, GROUP, STATE)
    wb = jnp.stack([r6(wbre), r6(wbim)], axis=-2)
    wb = jnp.einsum('ksgpin,gh->skgpihn', wb, eye).reshape(S5_SG, T * LANES, S5_NSTATE).astype(BF16)
    wc = jnp.stack([r6(clre[1:]), -r6(clim[1:])], axis=-2)
    wc = jnp.einsum('tsgpin,gh->signthp', wc, eye).reshape(S5_SG, S5_NSTATE, T * LANES).astype(BF16)
    lt = lambda a: a.reshape(N_GROUPS, GROUP, STATE)[:, 0].reshape(S5_SG, 1, S5_GL * STATE)
    return krev, wb, wc, lt(ltre), lt(ltim)


S5_TC = 512


def _s5_state_kernel(x_ref, wb_ref, e_ref):
    e_ref[0] = _dot(x_ref[0], wb_ref[0])


def _s5_scan_kernel(e_ref, ltre_ref, ltim_ref, xs_ref):
    lr, li = ltre_ref[0], ltim_ref[0]
    half = S5_NSTATE // 2

    def step(c, carry):
        new = []
        for b in range(BATCH):
            xr, xi = carry[b]
            row = b * S5_CH_PER_BATCH + c
            xs_ref[0, pl.ds(row, 1), :half] = xr
            xs_ref[0, pl.ds(row, 1), half:] = xi
            e = e_ref[0, pl.ds(row, 1), :]
            new.append((lr * xr - li * xi + e[:, :half], lr * xi + li * xr + e[:, half:]))
        return tuple(new)

    zero = jnp.zeros((1, half), F32)
    lax.fori_loop(0, S5_CH_PER_BATCH, step, tuple((zero, zero) for _ in range(BATCH)))


def _s5_out_kernel(x_ref, xs_ref, krev_ref, wc_ref, y_ref):
    xsb = xs_ref[0].astype(BF16)
    for t in range(S5_T):
        cols = slice(t * LANES, (t + 1) * LANES)
        y_ref[0, :, cols] = (_dot(x_ref[0, :, :(t + 1) * LANES], krev_ref[0, (S5_T - 1 - t) * LANES:, :])
                             + _dot(xsb, wc_ref[0, :, cols]))


def _s5(u, krev, wb, wc, ltre, ltim):
    T, tc = S5_T, S5_TC
    xs = u.astype(BF16).reshape(S5_CH, T, S5_SG, LANES).transpose(2, 0, 1, 3).reshape(S5_SG, S5_CH, T * LANES)
    grid = (S5_SG, S5_CH // tc)
    rows = lambda s, i: (s, i, 0)
    per_sg = lambda s, i: (s, 0, 0)
    e = pl.pallas_call(
        _s5_state_kernel, grid=grid,
        in_specs=[pl.BlockSpec((1, tc, T * LANES), rows), pl.BlockSpec((1, T * LANES, S5_NSTATE), per_sg)],
        out_specs=pl.BlockSpec((1, tc, S5_NSTATE), rows),
        out_shape=jax.ShapeDtypeStruct((S5_SG, S5_CH, S5_NSTATE), F32),
        compiler_params=_cparams("arbitrary", "arbitrary"), name="s5_state",
    )(xs, wb)
    sg1 = lambda s: (s, 0, 0)
    xstart = pl.pallas_call(
        _s5_scan_kernel, grid=(S5_SG,),
        in_specs=[pl.BlockSpec((1, S5_CH, S5_NSTATE), sg1),
                  pl.BlockSpec((1, 1, S5_NSTATE // 2), sg1), pl.BlockSpec((1, 1, S5_NSTATE // 2), sg1)],
        out_specs=pl.BlockSpec((1, S5_CH, S5_NSTATE), sg1),
        out_shape=jax.ShapeDtypeStruct((S5_SG, S5_CH, S5_NSTATE), F32),
        compiler_params=_cparams("arbitrary"), name="s5_scan",
    )(e, ltre, ltim)
    y = pl.pallas_call(
        _s5_out_kernel, grid=grid,
        in_specs=[pl.BlockSpec((1, tc, T * LANES), rows), pl.BlockSpec((1, tc, S5_NSTATE), rows),
                  pl.BlockSpec((1, T * LANES, LANES), per_sg), pl.BlockSpec((1, S5_NSTATE, T * LANES), per_sg)],
        out_specs=pl.BlockSpec((1, tc, T * LANES), rows),
        out_shape=jax.ShapeDtypeStruct((S5_SG, S5_CH, T * LANES), F32),
        compiler_params=_cparams("arbitrary", "arbitrary"), name="s5_out",
    )(xs, xstart, krev, wc)
    return y.reshape(S5_SG, S5_CH, T, LANES).transpose(1, 2, 0, 3).reshape(N_TOK, SSM_WIDTH)


MERGE_TM = 256


def _merge_kernel(ocmp_ref, osel_ref, owin_ref, gn_ref, yssm_ref, u_ref, ga_ref, gs_ref, x_ref, mod_ref,
                  eg_ref, dskip_ref, wglu_ref, bglu_ref, wua_ref, wus_ref, wout_ref, g2_ref,
                  wrhi_ref, wrlo_ref, wsgu_ref, wsd_ref,
                  xpart_ref, h2_ref, logit_ref):
    mod = mod_ref[0]
    gnb = gn_ref[...].astype(BF16)
    o_nsa = (_dot(gnb, eg_ref[0]) * ocmp_ref[...].astype(F32)
             + _dot(gnb, eg_ref[1]) * osel_ref[...].astype(F32)
             + _dot(gnb, eg_ref[2]) * owin_ref[...].astype(F32))
    attn = _dot(o_nsa.astype(BF16), wua_ref[...])
    z = _gelu(yssm_ref[...] + dskip_ref[...] * u_ref[...])
    y_ssm = z * jax.nn.sigmoid(_dot(z.astype(BF16), wglu_ref[...]) + bglu_ref[...])
    ssm = _dot(y_ssm.astype(BF16), wus_ref[...])
    merged = ga_ref[...].astype(F32) * attn + gs_ref[...].astype(F32) * ssm
    x1 = x_ref[...] + mod[2:3] * _dot(merged.astype(BF16), wout_ref[...])

    ms = jnp.mean(x1 * x1, axis=-1, keepdims=True)
    h2 = (x1 * lax.rsqrt(ms + EPS) * g2_ref[...]) * (1.0 + mod[4:5]) + mod[3:4]
    hi = h2.astype(BF16)
    lo = (h2 - hi.astype(F32)).astype(BF16)
    h2_ref[...] = h2
    logit_ref[...] = _dot_nt(wrhi_ref[...], hi) + _dot_nt(wrhi_ref[...], lo) + _dot_nt(wrlo_ref[...], hi)
    gu = _dot(hi, wsgu_ref[...])
    shared = _dot((_silu(gu[:, :D_EXPERT]) * gu[:, D_EXPERT:]).astype(BF16), wsd_ref[...])
    xpart_ref[...] = x1 + mod[5:6] * shared


def _merge(ocmp, osel, owin, gn, yssm, u, ga, gs, x2, mod, d_skip, w_glu, b_glu, w_up_attn, w_up_ssm, w_out,
           g_norm2, w_router, ws_gate, ws_up, ws_down):
    tm = MERGE_TM
    eg = np.zeros((3, LANES, NSA_WIDTH), np.float32)
    for j in range(3):
        for h in range(N_HEADS):
            eg[j, 3 * h + j, h * HEAD_DIM:(h + 1) * HEAD_DIM] = 1.0
    wr_t = w_router.T
    wr_hi = wr_t.astype(BF16)
    wr_lo = (wr_t - wr_hi.astype(F32)).astype(BF16)
    row = lambda i: (i, 0)
    fix2 = lambda i: (0, 0)
    wspec = lambda a: pl.BlockSpec(a.shape, (lambda i: (0,) * a.ndim))
    weights = [jnp.asarray(eg, BF16), d_skip.reshape(1, -1), w_glu.astype(BF16), b_glu.reshape(1, -1),
               w_up_attn.astype(BF16), w_up_ssm.astype(BF16), w_out.astype(BF16), g_norm2.reshape(1, -1),
               wr_hi, wr_lo, jnp.concatenate([ws_gate, ws_up], axis=1).astype(BF16), ws_down.astype(BF16)]
    acts = [(ocmp, 512), (osel, 512), (owin, 512), (gn, 128), (yssm, 512), (u, 512), (ga, 1024), (gs, 1024),
            (x2, 1024)]
    return pl.pallas_call(
        _merge_kernel,
        grid=(N_TOK // tm,),
        in_specs=[pl.BlockSpec((tm, wd), row) for _, wd in acts]
                 + [pl.BlockSpec((1, 6, D_MODEL), lambda i: (i // (SEQ // tm), 0, 0))]
                 + [wspec(w) for w in weights],
        out_specs=[pl.BlockSpec((tm, D_MODEL), row), pl.BlockSpec((tm, D_MODEL), row),
                   pl.BlockSpec((N_EXPERTS, tm), lambda i: (0, i))],
        out_shape=[jax.ShapeDtypeStruct((N_TOK, D_MODEL), F32), jax.ShapeDtypeStruct((N_TOK, D_MODEL), F32),
                   jax.ShapeDtypeStruct((N_EXPERTS, N_TOK), F32)],
        compiler_params=_cparams("arbitrary"),
        name="merge",
    )(*[a for a, _ in acts], mod, *weights)


ROUTE_TN = 512


def _route_kernel(logit_ref, bias_ref, eidx_ref, w_ref, count_ref, gscore_ref, masked_ref):
    tn = ROUTE_TN

    @pl.when(pl.program_id(0) == 0)
    def _():
        count_ref[...] = jnp.zeros(count_ref.shape, F32)

    sc = jax.nn.sigmoid(logit_ref[...])
    biased = sc + bias_ref[...]
    gi = lax.broadcasted_iota(I32, (EXPERTS_PER_GROUP, tn), 0).astype(F32)
    for g in range(N_EXPERT_GROUPS):
        blk = biased[g * EXPERTS_PER_GROUP:(g + 1) * EXPERTS_PER_GROUP]
        m1 = jnp.max(blk, axis=0, keepdims=True)
        i1 = jnp.min(jnp.where(blk == m1, gi, float(EXPERTS_PER_GROUP)), axis=0, keepdims=True)
        m2 = jnp.max(jnp.where(gi == i1, -jnp.inf, blk), axis=0, keepdims=True)
        gscore_ref[g:g + 1, :] = m1 + m2
    gs = gscore_ref[...]
    gidx = lax.broadcasted_iota(I32, (N_EXPERT_GROUPS, tn), 0)
    grank = jnp.zeros((N_EXPERT_GROUPS, tn), F32)
    for gp in range(N_EXPERT_GROUPS):
        row = gs[gp:gp + 1, :]
        tie = jnp.where(gidx > gp, 1.0, 0.0)
        grank = grank + jnp.where(row > gs, 1.0, jnp.where(row == gs, tie, 0.0))
    for g in range(N_EXPERT_GROUPS):
        keep = grank[g:g + 1, :] < float(TOPK_GROUPS)
        sl = slice(g * EXPERTS_PER_GROUP, (g + 1) * EXPERTS_PER_GROUP)
        masked_ref[sl, :] = jnp.where(keep, biased[sl], -jnp.inf)
    cur = masked_ref[...]
    eidx = lax.broadcasted_iota(I32, (N_EXPERTS, tn), 0).astype(F32)
    wsum = jnp.zeros((1, tn), F32)
    hits = jnp.zeros((N_EXPERTS, tn), F32)
    for k in range(TOP_K):
        m = jnp.max(cur, axis=0, keepdims=True)
        idx = jnp.min(jnp.where(cur == m, eidx, float(N_EXPERTS)), axis=0, keepdims=True)
        hit = eidx == idx
        wk = jnp.sum(jnp.where(hit, sc, 0.0), axis=0, keepdims=True)
        cur = jnp.where(hit, -jnp.inf, cur)
        hits = hits + jnp.where(hit, 1.0, 0.0)
        eidx_ref[k:k + 1, :] = idx.astype(I32)
        w_ref[k:k + 1, :] = wk
        wsum = wsum + wk
    w_ref[...] = w_ref[...] / wsum * ROUTE_SCALE
    count_ref[...] = count_ref[...] + jnp.sum(hits, axis=1, keepdims=True)


def _route(logits_t, router_bias):
    tn = ROUTE_TN
    return pl.pallas_call(
        _route_kernel,
        grid=(N_TOK // tn,),
        in_specs=[pl.BlockSpec((N_EXPERTS, tn), lambda i: (0, i)), pl.BlockSpec((N_EXPERTS, 1), lambda i: (0, 0))],
        out_specs=[pl.BlockSpec((TOP_K, tn), lambda i: (0, i))] * 2 + [pl.BlockSpec((N_EXPERTS, 1), lambda i: (0, 0))],
        out_shape=[jax.ShapeDtypeStruct((TOP_K, N_TOK), I32), jax.ShapeDtypeStruct((TOP_K, N_TOK), F32),
                   jax.ShapeDtypeStruct((N_EXPERTS, 1), F32)],
        scratch_shapes=[pltpu.VMEM((N_EXPERT_GROUPS, tn), F32), pltpu.VMEM((N_EXPERTS, tn), F32)],
        compiler_params=_cparams("arbitrary"),
        name="route",
    )(logits_t, router_bias.reshape(-1, 1))


N_MOE_BLK = NK // DISPATCH_BLOCK
N_ITEMS = N_MOE_BLK + N_EXPERTS
ASSIGN_BITS = 17


def _dispatch_plan(eidx, counts):
    e_flat = eidx.reshape(-1)
    key = jnp.sort(e_flat * NK + jnp.arange(NK, dtype=I32))
    order = key & (NK - 1)
    counts = counts.reshape(-1).astype(I32)
    start = jnp.cumsum(counts) - counts
    cuts = jnp.sort(jnp.concatenate([jnp.arange(N_MOE_BLK, dtype=I32) * DISPATCH_BLOCK, start]))
    lo = cuts
    hi = jnp.concatenate([cuts[1:], jnp.full((1,), NK, I32)])
    blk = jnp.minimum(lo // DISPATCH_BLOCK, N_MOE_BLK - 1)
    expert = jnp.clip(jnp.sum((start[None, :] <= lo[:, None]).astype(I32), axis=1) - 1, 0, N_EXPERTS - 1)
    first = jnp.concatenate([jnp.ones((1,), I32), (blk[1:] != blk[:-1]).astype(I32)])
    _, pos = lax.sort_key_val(order, jnp.arange(NK, dtype=I32))
    tok = jnp.right_shift(order, 3)
    return tok, pos, (blk, expert, lo - blk * DISPATCH_BLOCK, hi - blk * DISPATCH_BLOCK, first)


SC_CORES = 2
SC_SUBCORES = 16
SC_CHUNK = 64


def _sc_gather_rows(table, idx):
    n = idx.shape[0]
    workers = SC_CORES * SC_SUBCORES
    per_worker = n // workers
    n_chunks = per_worker // SC_CHUNK
    assert per_worker * workers == n and n_chunks * SC_CHUNK == per_worker
    mesh = plsc.VectorSubcoreMesh(core_axis_name="c", subcore_axis_name="s",
                                  num_cores=SC_CORES, num_subcores=SC_SUBCORES)

    def body(table_hbm, idx_hbm, out_hbm, idx_v, rows_v, sem):
        wid = lax.axis_index("s") * SC_CORES + lax.axis_index("c")
        base = wid * per_worker

        @pl.loop(0, n_chunks)
        def _(j):
            off = base + j * SC_CHUNK
            pltpu.sync_copy(idx_hbm.at[pl.ds(off, SC_CHUNK)], idx_v)
            pltpu.async_copy(table_hbm.at[idx_v], rows_v, sem).wait()
            pltpu.sync_copy(rows_v, out_hbm.at[pl.ds(off, SC_CHUNK)])

    return pl.kernel(
        body,
        out_type=jax.ShapeDtypeStruct((n, table.shape[1]), table.dtype),
        mesh=mesh,
        scratch_types=[pltpu.VMEM((SC_CHUNK,), I32), pltpu.VMEM((SC_CHUNK, table.shape[1]), table.dtype),
                       pltpu.SemaphoreType.DMA],
        name="sc_gather_rows",
    )(table, idx)


def _moe_kernel(blk_ref, exp_ref, lo_ref, hi_ref, first_ref, x_ref, wg_ref, wu_ref, wd_ref, y_ref):
    it = pl.program_id(0)
    lo, hi = lo_ref[it], hi_ref[it]

    @pl.when(first_ref[it] == 1)
    def _():
        y_ref[...] = jnp.zeros(y_ref.shape, F32)

    @pl.when(hi > lo)
    def _():
        ridx = lax.broadcasted_iota(I32, x_ref.shape, 0)
        mine = (ridx >= lo) & (ridx < hi)
        x = jnp.where(mine, x_ref[...], 0.0).astype(BF16)
        gate = _dot(x, wg_ref[0].astype(BF16))
        up = _dot(x, wu_ref[0].astype(BF16))
        y_ref[...] = y_ref[...] + _dot((_silu(gate) * up).astype(BF16), wd_ref[0].astype(BF16))


def _moe(xs, items, w_gate, w_up, w_down):
    blk, expert, lo, hi, first = items
    by_blk = lambda it, blk, *_: (blk[it], 0)
    by_exp = lambda it, blk, ex, *_: (ex[it], 0, 0)
    grid_spec = pltpu.PrefetchScalarGridSpec(
        num_scalar_prefetch=5,
        grid=(N_ITEMS,),
        in_specs=[pl.BlockSpec((DISPATCH_BLOCK, D_MODEL), by_blk),
                  pl.BlockSpec((1, D_MODEL, D_EXPERT), by_exp),
                  pl.BlockSpec((1, D_MODEL, D_EXPERT), by_exp),
                  pl.BlockSpec((1, D_EXPERT, D_MODEL), by_exp)],
        out_specs=pl.BlockSpec((DISPATCH_BLOCK, D_MODEL), by_blk),
    )
    return pl.pallas_call(
        _moe_kernel,
        grid_spec=grid_spec,
        out_shape=jax.ShapeDtypeStruct((NK, D_MODEL), F32),
        compiler_params=_cparams("arbitrary"),
        name="moe",
    )(blk, expert, lo, hi, first, xs, w_gate, w_up, w_down)


COMB_TC = 64


def _combine_kernel(slots_ref, w_ref, xpart_ref, mod_ref, out_ref):
    w = w_ref[...]
    routed = w[:, 0:1] * slots_ref[:, 0, :]
    for k in range(1, TOP_K):
        routed = routed + w[:, k:k + 1] * slots_ref[:, k, :]
    out_ref[...] = xpart_ref[...] + mod_ref[0][5:6] * routed


def _combine(xpart, mod, slots, w):
    tc = COMB_TC
    row = lambda i: (i, 0)
    return pl.pallas_call(
        _combine_kernel,
        grid=(N_TOK // tc,),
        in_specs=[pl.BlockSpec((tc, TOP_K, D_MODEL), lambda i: (i, 0, 0)),
                  pl.BlockSpec((tc, TOP_K), row),
                  pl.BlockSpec((tc, D_MODEL), row),
                  pl.BlockSpec((1, 6, D_MODEL), lambda i: (i // (SEQ // tc), 0, 0))],
        out_specs=pl.BlockSpec((tc, D_MODEL), row),
        out_shape=jax.ShapeDtypeStruct((N_TOK, D_MODEL), F32),
        compiler_params=_cparams("arbitrary"),
        name="combine",
    )(slots.reshape(N_TOK, TOP_K, D_MODEL), w, xpart, mod)


def _layer(x, c, w_ada, b_ada, g_norm1, g_norm2, w_in, q_gain, kc_gain, ks_gain, kw_gain,
           pe_k, pe_v, w_cmp_k1, w_cmp_k2, w_cmp_v1, w_cmp_v2,
           a_re, a_im, log_dt, b_re, b_im, c_re, c_im, d_skip, w_glu, b_glu,
           w_up_attn, w_up_ssm, w_out, w_router, router_bias,
           w_gate, w_up, w_down, ws_gate, ws_up, ws_down):
    x2 = x.reshape(N_TOK, D_MODEL)
    mod = _ada(c, w_ada, b_ada)
    q, kc_raw, vc_raw, ks, kw, vst, vwt, gn, u, ga, gs = _proj(x2, mod, g_norm1, w_in, q_gain, ks_gain, kw_gain)
    kcn = _compress(kc_raw, pe_k, w_cmp_k1, w_cmp_k2, kc_gain, True)
    vcn = _compress(vc_raw, pe_v, w_cmp_v1, w_cmp_v2, kc_gain, False)
    ocmp, selb = _cmp_attn(q, kcn, vcn)
    osel, owin = _selwin(q, ks, kw, vst, vwt, selb)
    yssm = _s5(u, *_s5_params(a_re, a_im, log_dt, b_re, b_im, c_re, c_im))
    xpart, h2, logits_t = _merge(ocmp, osel, owin, gn, yssm, u, ga, gs, x2, mod, d_skip, w_glu, b_glu,
                                  w_up_attn, w_up_ssm, w_out, g_norm2, w_router, ws_gate, ws_up, ws_down)
    eidx_t, w_t, counts = _route(logits_t, router_bias)
    tok, pos, items = _dispatch_plan(eidx_t.T, counts)
    y = _moe(_sc_gather_rows(h2, tok), items, w_gate, w_up, w_down)
    slots = _sc_gather_rows(y, pos)
    return _combine(xpart, mod, slots, w_t.T).reshape(BATCH, SEQ, D_MODEL)


def kernel(x, c, w_ada, b_ada, g_norm1, g_norm2, w_in, q_gain, kc_gain, ks_gain, kw_gain, pe_k, pe_v, w_cmp_k1,
           w_cmp_k2, w_cmp_v1, w_cmp_v2, a_re, a_im, log_dt, b_re, b_im, c_re, c_im, d_skip, w_glu, b_glu,
           w_up_attn, w_up_ssm, w_out, w_router, router_bias, w_gate, w_up, w_down, ws_gate, ws_up, ws_down):
    params = (w_ada, b_ada, g_norm1, g_norm2, w_in, q_gain, kc_gain, ks_gain, kw_gain, pe_k, pe_v, w_cmp_k1,
              w_cmp_k2, w_cmp_v1, w_cmp_v2, a_re, a_im, log_dt, b_re, b_im, c_re, c_im, d_skip, w_glu, b_glu,
              w_up_attn, w_up_ssm, w_out, w_router, router_bias, w_gate, w_up, w_down, ws_gate, ws_up, ws_down)
    depth = w_ada.shape[0]
    for layer in range(depth):
        x = _layer(x, c, *[p[layer] for p in params])
    return x
```

```python
import functools
import math

import jax
import jax.numpy as jnp
import numpy as np
from jax import lax
from jax.experimental import pallas as pl
from jax.experimental.pallas import tpu as pltpu
from jax.experimental.pallas import tpu_sc as plsc

F32 = jnp.float32
BF16 = jnp.bfloat16
I32 = jnp.int32
HIGHEST = lax.Precision.HIGHEST

D_MODEL = 1024
BATCH = 4
SEQ = 4096
N_TOK = BATCH * SEQ
N_HEADS = 8
HEAD_DIM = 64
N_KV = 2
CMP_BLOCK = 32
CMP_STRIDE = 16
CMP_HIDDEN = 256
N_CMP = 256
SEL_BLOCK = 64
N_SEL_BLOCKS = SEQ // SEL_BLOCK
N_SELECT = 16
WINDOW = 512
ATTN_SCALE = HEAD_DIM ** -0.5
NSA_WIDTH = N_HEADS * HEAD_DIM
SSM_WIDTH = 512
GROUP = 16
N_GROUPS = SSM_WIDTH // GROUP
STATE = 64
N_EXPERTS = 256
TOP_K = 8
D_EXPERT = 256
N_EXPERT_GROUPS = 8
EXPERTS_PER_GROUP = N_EXPERTS // N_EXPERT_GROUPS
TOPK_GROUPS = 4
ROUTE_SCALE = 2.5
DISPATCH_BLOCK = 256
EPS = 1e-6
NEG = -1e30

LANES = 128
S5_T = 16
S5_SG = 4
S5_GL = N_GROUPS // S5_SG
S5_CH = N_TOK // S5_T
S5_CH_PER_BATCH = SEQ // S5_T
S5_NSTATE = S5_GL * STATE * 2

NK = N_TOK * TOP_K
HALF = D_MODEL // 2

VMEM_LIMIT = 48 * 1024 * 1024


def _cparams(*sem):
    return pltpu.CompilerParams(dimension_semantics=tuple(sem), vmem_limit_bytes=VMEM_LIMIT)


def _dot(a, b):
    return jnp.dot(a, b, preferred_element_type=F32)


def _dot_nt(a, b):
    return lax.dot_general(a, b, (((1,), (1,)), ((), ())), preferred_element_type=F32)


def _split_dot(v, w):
    hi = v.astype(BF16)
    lo = (v - hi.astype(F32)).astype(BF16)
    return _dot(hi, w) + _dot(lo, w)


def _seg_rms(v, bd, gain):
    ss = _split_dot(v * v, bd)
    return v * lax.rsqrt(ss * (1.0 / HEAD_DIM) + EPS) * gain


def _gelu(x):
    return 0.5 * x * (1.0 + jnp.tanh(0.7978845608028654 * (x + 0.044715 * (x * x * x))))


def _silu(x):
    return x * jax.nn.sigmoid(x)


def _pack_bf16_pairs(v):
    bits = lax.bitcast_convert_type(v.astype(BF16).astype(F32), jnp.uint32)
    h = v.shape[1] // 2
    return bits[:, h:] | lax.shift_right_logical(bits[:, :h], jnp.uint32(16))


def _unpack_bf16_pairs(word):
    lo = lax.bitcast_convert_type(lax.shift_left(word, jnp.uint32(16)), F32)
    hi = lax.bitcast_convert_type(word & jnp.uint32(0xFFFF0000), F32)
    return lo, hi


def _ada_kernel(c_ref, w_ref, b_ref, o_ref):
    c = c_ref[...]
    o_ref[...] = jnp.dot(_silu(c), w_ref[...], preferred_element_type=F32, precision=HIGHEST) + b_ref[...]


def _ada(c, w_ada, b_ada):
    cp = jnp.pad(c, ((0, 8 - BATCH), (0, 0)))
    tn = 1536
    out = pl.pallas_call(
        _ada_kernel,
        grid=(6 * D_MODEL // tn,),
        in_specs=[pl.BlockSpec((8, D_MODEL), lambda j: (0, 0)),
                  pl.BlockSpec((D_MODEL, tn), lambda j: (0, j)),
                  pl.BlockSpec((1, tn), lambda j: (0, j))],
        out_specs=pl.BlockSpec((8, tn), lambda j: (0, j)),
        out_shape=jax.ShapeDtypeStruct((8, 6 * D_MODEL), F32),
        compiler_params=_cparams("arbitrary"),
        name="ada",
    )(cp, w_ada, b_ada.reshape(1, -1))
    return out.reshape(8, 6, D_MODEL)


_C_Q = 0
_C_KC = 512
_C_VC = 640
_C_KS = 768
_C_KW = 1024
_C_GN = 1280
_C_U = 1408
_C_GA = 1920
_C_GS = 2944
_C_END = 3968
PROJ_TM = 512


def _proj_kernel(x_ref, mod_ref, g1_ref, w_ref, wvt_ref, qg_ref, ksg_ref, kwg_ref, bd512_ref, bd256_ref,
                 q_ref, kc_ref, vc_ref, ks_ref, kw_ref, vst_ref, vwt_ref, gn_ref, u_ref, ga_ref, gs_ref):
    x = x_ref[...]
    ms = jnp.mean(x * x, axis=-1, keepdims=True)
    mod = mod_ref[0]
    h = (x * lax.rsqrt(ms + EPS) * g1_ref[...]) * (1.0 + mod[1:2]) + mod[0:1]
    hb = h.astype(BF16)

    def p(lo, hi):
        return _dot(hb, w_ref[:, lo:hi])

    q_ref[...] = _seg_rms(p(_C_Q, _C_KC), bd512_ref[...], qg_ref[...] * ATTN_SCALE).astype(BF16)
    kc_ref[...] = p(_C_KC, _C_VC).astype(BF16)
    vc_ref[...] = p(_C_VC, _C_KS).astype(BF16)
    ks_ref[...] = _seg_rms(p(_C_KS, _C_KW), bd256_ref[...], ksg_ref[...]).astype(BF16)
    kw_ref[...] = _seg_rms(p(_C_KW, _C_GN), bd256_ref[...], kwg_ref[...]).astype(BF16)
    vt = _dot_nt(wvt_ref[...], hb)
    vst_ref[...] = vt[:LANES].astype(BF16)
    vwt_ref[...] = vt[LANES:].astype(BF16)
    gn_ref[...] = jax.nn.sigmoid(p(_C_GN, _C_U))
    u_ref[...] = p(_C_U, _C_GA)
    ga_ref[...] = jax.nn.sigmoid(p(_C_GA, _C_GS)).astype(BF16)
    gs_ref[...] = jax.nn.sigmoid(p(_C_GS, _C_END)).astype(BF16)


def _dup_cols(w):
    return jnp.concatenate([w[:, :64], w[:, :64], w[:, 64:], w[:, 64:]], axis=1)


def _block_ones(n):
    return jnp.kron(jnp.eye(n // HEAD_DIM, dtype=F32), jnp.ones((HEAD_DIM, HEAD_DIM), F32)).astype(BF16)


def _proj(x2, mod, g_norm1, w_in, q_gain, ks_gain, kw_gain):
    o = np.cumsum((0, 512, 128, 128, 128, 128, 128, 128, 24, 512, 1024, 1024))
    parts = [w_in[:, o[i]:o[i + 1]] for i in range(11)]
    wq, wkc, wvc, wks, wvs, wkw, wvw, wgn, wu, wga, wgs = parts
    w = jnp.concatenate([wq, wkc, wvc, _dup_cols(wks), _dup_cols(wkw),
                         jnp.pad(wgn, ((0, 0), (0, LANES - 24))), wu, wga, wgs], axis=1).astype(BF16)
    wvt = jnp.concatenate([wvs, wvw], axis=1).T.astype(BF16)
    tm = PROJ_TM
    row = lambda i: (i, 0)
    col = lambda i: (0, i)
    fix = lambda i: (0, 0)
    outs = [(512, BF16, row), (128, BF16, row), (128, BF16, row), (256, BF16, row), (256, BF16, row),
            (LANES, BF16, col), (LANES, BF16, col),
            (128, F32, row), (512, F32, row), (1024, BF16, row), (1024, BF16, row)]
    ospec = lambda wd, m: pl.BlockSpec((tm, wd), m) if m is row else pl.BlockSpec((wd, tm), m)
    oshape = lambda wd, dt, m: jax.ShapeDtypeStruct((N_TOK, wd) if m is row else (wd, N_TOK), dt)
    return pl.pallas_call(
        _proj_kernel,
        grid=(N_TOK // tm,),
        in_specs=[pl.BlockSpec((tm, D_MODEL), row),
                  pl.BlockSpec((1, 6, D_MODEL), lambda i: (i // (SEQ // tm), 0, 0)),
                  pl.BlockSpec((1, D_MODEL), fix),
                  pl.BlockSpec((D_MODEL, _C_END), fix),
                  pl.BlockSpec((2 * LANES, D_MODEL), fix),
                  pl.BlockSpec((1, 512), fix), pl.BlockSpec((1, 256), fix), pl.BlockSpec((1, 256), fix),
                  pl.BlockSpec((512, 512), fix), pl.BlockSpec((256, 256), fix)],
        out_specs=[ospec(wd, m) for wd, _, m in outs],
        out_shape=[oshape(wd, dt, m) for wd, dt, m in outs],
        compiler_params=_cparams("arbitrary"),
        name="proj",
    )(x2, mod, g_norm1.reshape(1, -1), w, wvt,
      jnp.tile(q_gain, N_HEADS).reshape(1, -1), jnp.tile(ks_gain, 4).reshape(1, -1),
      jnp.tile(kw_gain, 4).reshape(1, -1), _block_ones(512), _block_ones(256))


def _compress_kernel(r_ref, pe_ref, w1_ref, w2_ref, bd_ref, gain_ref, o_ref, *, do_norm):
    r = r_ref[0].astype(F32)
    p0 = _dot((r + pe_ref[0]).astype(BF16), w1_ref[0])
    p1 = _dot((r + pe_ref[1]).astype(BF16), w1_ref[1])
    hid = p0 + pltpu.roll(p1, N_CMP - 1, 0)
    c = _dot(_gelu(hid).astype(BF16), w2_ref[...])
    if do_norm:
        c = _seg_rms(c, bd_ref[...], gain_ref[...])
    o_ref[0] = c.astype(BF16)


def _compress(raw, pe, w1, w2, gain, do_norm):
    r = raw.reshape(BATCH, SEQ // CMP_STRIDE, CMP_STRIDE * LANES)
    eye = jnp.eye(N_KV, dtype=F32)
    w1r = w1.reshape(2, CMP_STRIDE, HEAD_DIM, CMP_HIDDEN)
    w1big = jnp.einsum('hldc,gk->hlgdkc', w1r, eye).reshape(2, CMP_STRIDE * LANES, N_KV * CMP_HIDDEN).astype(BF16)
    w2big = jnp.einsum('cd,gk->gckd', w2, eye)
    w2big = jnp.concatenate([w2big, w2big], axis=-1).reshape(N_KV * CMP_HIDDEN, 4 * HEAD_DIM).astype(BF16)
    pe_big = jnp.broadcast_to(pe.reshape(2, CMP_STRIDE, 1, HEAD_DIM), (2, CMP_STRIDE, N_KV, HEAD_DIM))
    pe_big = pe_big.reshape(2, 1, CMP_STRIDE * LANES)
    fix2 = lambda b: (0, 0)
    fix3 = lambda b: (0, 0, 0)
    return pl.pallas_call(
        functools.partial(_compress_kernel, do_norm=do_norm),
        grid=(BATCH,),
        in_specs=[pl.BlockSpec((1, N_CMP, CMP_STRIDE * LANES), lambda b: (b, 0, 0)),
                  pl.BlockSpec((2, 1, CMP_STRIDE * LANES), fix3),
                  pl.BlockSpec((2, CMP_STRIDE * LANES, N_KV * CMP_HIDDEN), fix3),
                  pl.BlockSpec((N_KV * CMP_HIDDEN, 256), fix2),
                  pl.BlockSpec((256, 256), fix2), pl.BlockSpec((1, 256), fix2)],
        out_specs=pl.BlockSpec((1, N_CMP, 256), lambda b: (b, 0, 0)),
        out_shape=jax.ShapeDtypeStruct((BATCH, N_CMP, 256), BF16),
        compiler_params=_cparams("arbitrary"),
        name="compress_k" if do_norm else "compress_v",
    )(r, pe_big, w1big, w2big, _block_ones(256), jnp.tile(gain, 4).reshape(1, -1))


ATT_TQ = 256


def _head_variants(qb):
    lane = lax.broadcasted_iota(I32, qb.shape, 1)
    z = jnp.zeros_like(qb)
    return jnp.where(lane < HEAD_DIM, qb, z), jnp.where(lane < HEAD_DIM, z, qb)


def _cmp_kernel(q_ref, kc_ref, vc_ref, ov_ref, o_ref, sel_ref):
    tq = ATT_TQ
    qi = pl.program_id(1)
    tpos = qi * tq + lax.broadcasted_iota(I32, (tq, N_CMP), 0)
    nidx = lax.broadcasted_iota(I32, (tq, N_CMP), 1)
    mask = (CMP_STRIDE * nidx + (CMP_BLOCK - 1)) <= tpos
    lane_lo = lax.broadcasted_iota(I32, (tq, LANES), 1) < HEAD_DIM
    for g in range(N_KV):
        kd = kc_ref[0, :, g * LANES:(g + 1) * LANES]
        vd = vc_ref[0, :, g * LANES:(g + 1) * LANES]
        psum = jnp.zeros((tq, N_CMP), F32)
        for jb in range(2):
            blk = 2 * g + jb
            pv = []
            for qv in _head_variants(q_ref[:, blk * LANES:(blk + 1) * LANES]):
                s = jnp.where(mask, _dot_nt(qv, kd), NEG)
                m = jnp.max(s, axis=-1, keepdims=True)
                e = jnp.where(mask, jnp.exp(s - m), 0.0)
                l = jnp.sum(e, axis=-1, keepdims=True)
                p = e / jnp.where(l > 0.0, l, 1.0)
                psum = psum + p
                pv.append(_dot(p.astype(BF16), vd))
            o_ref[:, blk * LANES:(blk + 1) * LANES] = jnp.where(lane_lo, pv[0], pv[1]).astype(BF16)
        imp = _split_dot(psum, ov_ref[...])
        imp_t = imp.T[:N_SEL_BLOCKS]
        j = lax.broadcasted_iota(I32, (N_SEL_BLOCKS, tq), 0)
        cur = jnp.right_shift(qi * tq + lax.broadcasted_iota(I32, (N_SEL_BLOCKS, tq), 1), 6)
        forced = (j == 0) | (j == cur) | (j == cur - 1)
        v = jnp.where(forced, jnp.inf, jnp.where(j <= cur, imp_t, -jnp.inf))
        rank = jnp.zeros((N_SEL_BLOCKS, tq), F32)
        for jp in range(N_SEL_BLOCKS):
            row = v[jp:jp + 1, :]
            tie = jnp.where(j > jp, 1.0, 0.0)
            rank = rank + jnp.where(row > v, 1.0, jnp.where(row == v, tie, 0.0))
        sel_ref[g * N_SEL_BLOCKS:(g + 1) * N_SEL_BLOCKS, :] = jnp.where(rank < float(N_SELECT), 0.0, NEG)


def _cmp_attn(q, kcn, vcn):
    nc = np.arange(N_CMP)
    sb = np.arange(LANES)
    ov = ((CMP_STRIDE * nc[:, None] < SEL_BLOCK * sb[None, :] + SEL_BLOCK)
          & (CMP_STRIDE * nc[:, None] + CMP_BLOCK > SEL_BLOCK * sb[None, :])
          & (nc[:, None] < N_CMP - 1) & (sb[None, :] < N_SEL_BLOCKS))
    ov = jnp.asarray(ov, BF16)
    tq = ATT_TQ
    nq = SEQ // tq
    row = lambda b, i: (b * nq + i, 0)
    return pl.pallas_call(
        _cmp_kernel,
        grid=(BATCH, nq),
        in_specs=[pl.BlockSpec((tq, NSA_WIDTH), row),
                  pl.BlockSpec((1, N_CMP, 256), lambda b, i: (b, 0, 0)),
                  pl.BlockSpec((1, N_CMP, 256), lambda b, i: (b, 0, 0)),
                  pl.BlockSpec((N_CMP, LANES), lambda b, i: (0, 0))],
        out_specs=[pl.BlockSpec((tq, NSA_WIDTH), row),
                   pl.BlockSpec((N_KV * N_SEL_BLOCKS, tq), lambda b, i: (0, b * nq + i))],
        out_shape=[jax.ShapeDtypeStruct((N_TOK, NSA_WIDTH), BF16),
                   jax.ShapeDtypeStruct((N_KV * N_SEL_BLOCKS, N_TOK), F32)],
        compiler_params=_cparams("arbitrary", "arbitrary"),
        name="cmp_attn",
    )(q, kcn, vcn, ov)


ATT_TK = 256


M_INIT = -1e29


def _selwin_kernel(q_ref, ks_ref, kw_ref, vst_ref, vwt_ref, selb_ref, osel_ref, owin_ref, m_ref, l_ref, acc_ref):
    tq, tk = ATT_TQ, ATT_TK
    qi = pl.program_id(1)
    krow = lax.broadcasted_iota(I32, (tk, tq), 0)
    qcol = lax.broadcasted_iota(I32, (tk, tq), 1)
    causal_bias = jnp.where(krow <= qcol, 0.0, NEG)
    far_bias = jnp.where(qcol < krow, 0.0, NEG)

    def reset():
        m_ref[...] = jnp.full(m_ref.shape, M_INIT, F32)
        l_ref[...] = jnp.zeros(l_ref.shape, F32)
        acc_ref[...] = jnp.zeros(acc_ref.shape, F32)

    def update(qvars, kd, vt, bias):
        s = _dot_nt(kd, qvars)
        if bias is not None:
            s = s + jnp.concatenate([bias] * 4, axis=1)
        m_old = m_ref[...]
        m_new = jnp.maximum(m_old, jnp.max(s, axis=0, keepdims=True))
        alpha = jnp.exp(m_old - m_new)
        p = jnp.exp(s - m_new)
        l_ref[...] = alpha * l_ref[...] + jnp.sum(p, axis=0, keepdims=True)
        m_ref[...] = m_new
        acc_ref[...] = alpha * acc_ref[...] + _dot(vt, p.astype(BF16))

    def finish(out_ref, g):
        o = acc_ref[...] / l_ref[...]
        for jb in range(2):
            blk = 2 * g + jb
            pair = jnp.concatenate([o[:, 2 * jb * tq:(2 * jb + 1) * tq], o[:, (2 * jb + 1) * tq:(2 * jb + 2) * tq]],
                                   axis=0)
            out_ref[:, blk * LANES:(blk + 1) * LANES] = pair.T.astype(BF16)

    def sel_bias(g, kt):
        rows = [jnp.broadcast_to(selb_ref[pl.ds(g * N_SEL_BLOCKS + kt * (tk // SEL_BLOCK) + r, 1), :],
                                 (SEL_BLOCK, tq)) for r in range(tk // SEL_BLOCK)]
        return jnp.concatenate(rows, axis=0)

    for g in range(N_KV):
        gl = slice(g * LANES, (g + 1) * LANES)
        gv = slice(g * HEAD_DIM, (g + 1) * HEAD_DIM)
        qvars = []
        for jb in range(2):
            qvars.extend(_head_variants(q_ref[:, (2 * g + jb) * LANES:(2 * g + jb + 1) * LANES]))
        qvars = jnp.concatenate(qvars, axis=0)

        def k_tile(ref, kt):
            return ref[0, pl.ds(pl.multiple_of(kt * tk, tk), tk), gl]

        def v_tile(ref, kt):
            return ref[gv, pl.ds(pl.multiple_of(kt * tk, tk), tk)]

        reset()

        def sel_step(kt, carry):
            update(qvars, k_tile(ks_ref, kt), v_tile(vst_ref, kt), sel_bias(g, kt))
            return carry

        lax.fori_loop(0, qi, sel_step, 0)
        update(qvars, k_tile(ks_ref, qi), v_tile(vst_ref, qi), sel_bias(g, qi) + causal_bias)
        finish(osel_ref, g)

        reset()

        @pl.when(qi >= 2)
        def _():
            update(qvars, k_tile(kw_ref, qi - 2), v_tile(vwt_ref, qi - 2), far_bias)

        @pl.when(qi >= 1)
        def _():
            update(qvars, k_tile(kw_ref, qi - 1), v_tile(vwt_ref, qi - 1), None)

        update(qvars, k_tile(kw_ref, qi), v_tile(vwt_ref, qi), causal_bias)
        finish(owin_ref, g)


def _selwin(q, ks, kw, vst, vwt, selb):
    tq = ATT_TQ
    nq = SEQ // tq
    assert WINDOW == 2 * ATT_TK and ATT_TQ == ATT_TK
    row = lambda b, i: (b * nq + i, 0)
    keys = pl.BlockSpec((1, SEQ, 256), lambda b, i: (b, 0, 0))
    vals = pl.BlockSpec((LANES, SEQ), lambda b, i: (0, b))
    r3 = lambda a: a.reshape(BATCH, SEQ, 256)
    return pl.pallas_call(
        _selwin_kernel,
        grid=(BATCH, nq),
        in_specs=[pl.BlockSpec((tq, NSA_WIDTH), row), keys, keys, vals, vals,
                  pl.BlockSpec((N_KV * N_SEL_BLOCKS, tq), lambda b, i: (0, b * nq + i))],
        out_specs=[pl.BlockSpec((tq, NSA_WIDTH), row)] * 2,
        out_shape=[jax.ShapeDtypeStruct((N_TOK, NSA_WIDTH), BF16)] * 2,
        scratch_shapes=[pltpu.VMEM((1, 4 * tq), F32), pltpu.VMEM((1, 4 * tq), F32),
                        pltpu.VMEM((HEAD_DIM, 4 * tq), F32)],
        compiler_params=_cparams("arbitrary", "arbitrary"),
        name="selwin",
    )(q, r3(ks), r3(kw), vst, vwt, selb)


def _s5_param_kernel(are_ref, aim_ref, ldt_ref, cre_ref, cim_ref, bre_ref, bim_ref,
                     clre_ref, clim_ref, wbre_ref, wbim_ref, bbre_ref, bbim_ref, ltre_ref, ltim_ref):
    are, aim = are_ref[...], aim_ref[...]
    dt = jnp.exp(ldt_ref[...])
    cre, cim = cre_ref[...], cim_ref[...]

    def lam_pow(tau):
        mag = jnp.exp(are * dt * float(tau))
        ang = aim * dt * float(tau)
        return mag * jnp.cos(ang), mag * jnp.sin(ang)

    lre, lim = lam_pow(1)
    den = are * are + aim * aim
    qre = ((lre - 1.0) * are + lim * aim) / den
    qim = (lim * are - (lre - 1.0) * aim) / den
    bre, bim = bre_ref[...], bim_ref[...]
    bbre = qre * bre - qim * bim
    bbim = qre * bim + qim * bre
    bbre_ref[...] = bbre
    bbim_ref[...] = bbim
    for tau in range(S5_T + 1):
        pr, pi = lam_pow(tau)
        clre_ref[tau] = cre * pr - cim * pi
        clim_ref[tau] = cre * pi + cim * pr
        if tau < S5_T:
            k = S5_T - 1 - tau
            wbre_ref[k] = pr * bbre - pi * bbim
            wbim_ref[k] = pr * bbim + pi * bbre
        else:
            ltre_ref[...] = pr
            ltim_ref[...] = pi


def _s5_kmat_kernel(l_ref, r_ref, o_ref):
    o_ref[0] = jnp.dot(l_ref[0], r_ref[0], preferred_element_type=F32, precision=HIGHEST)


def _s5_params(a_re, a_im, log_dt, b_re, b_im, c_re, c_im):
    T = S5_T
    pn = GROUP * STATE
    tile_p = lambda a: jnp.tile(a, (1, GROUP))
    args = (tile_p(a_re), tile_p(a_im), jnp.broadcast_to(log_dt[:, None], (N_GROUPS, pn)),
            c_re.reshape(N_GROUPS, pn), c_im.reshape(N_GROUPS, pn),
            jnp.swapaxes(b_re, 1, 2).reshape(N_GROUPS, pn), jnp.swapaxes(b_im, 1, 2).reshape(N_GROUPS, pn))
    full2 = pl.BlockSpec((N_GROUPS, pn), lambda: (0, 0))
    clre, clim, wbre, wbim, bbre, bbim, ltre, ltim = pl.pallas_call(
        _s5_param_kernel,
        in_specs=[full2] * 7,
        out_specs=[pl.BlockSpec((T + 1, N_GROUPS, pn), lambda: (0, 0, 0))] * 2
                  + [pl.BlockSpec((T, N_GROUPS, pn), lambda: (0, 0, 0))] * 2 + [full2] * 4,
        out_shape=[jax.ShapeDtypeStruct((T + 1, N_GROUPS, pn), F32)] * 2
                  + [jax.ShapeDtypeStruct((T, N_GROUPS, pn), F32)] * 2
                  + [jax.ShapeDtypeStruct((N_GROUPS, pn), F32)] * 4,
        name="s5_params",
    )(*args)

    r5 = lambda a, t: a[:t].reshape(t, N_GROUPS, GROUP, STATE)
    lhs = jnp.concatenate([r5(clre, T), -r5(clim, T)], axis=-1)
    lhs = jnp.transpose(lhs, (1, 0, 2, 3)).reshape(N_GROUPS, T * GROUP, 2 * STATE)
    bb = lambda a: jnp.swapaxes(a.reshape(N_GROUPS, GROUP, STATE), 1, 2)
    rhs = jnp.concatenate([bb(bbre), bb(bbim)], axis=1)
    kmat = pl.pallas_call(
        _s5_kmat_kernel,
        grid=(N_GROUPS,),
        in_specs=[pl.BlockSpec((1, T * GROUP, 2 * STATE), lambda g: (g, 0, 0)),
                  pl.BlockSpec((1, 2 * STATE, GROUP), lambda g: (g, 0, 0))],
        out_specs=pl.BlockSpec((1, T * GROUP, GROUP), lambda g: (g, 0, 0)),
        out_shape=jax.ShapeDtypeStruct((N_GROUPS, T * GROUP, GROUP), F32),
        compiler_params=_cparams("arbitrary"),
        name="s5_kmat",
    )(lhs, rhs)

    eye = jnp.eye(S5_GL, dtype=F32)
    kt = kmat.reshape(S5_SG, S5_GL, T, GROUP, GROUP)
    kbd = jnp.einsum('sgtpq,gh->stgqhp', kt, eye).reshape(S5_SG, T, LANES, LANES)
    krev = kbd[:, ::-1].reshape(S5_SG, T * LANES, LANES).astype(BF16)
    r6 = lambda a: a.reshape(T, S5_SG, S5_GL, GROUP, STATE)
    wb = jnp.stack([r6(wbre), r6(wbim)], axis=-2)
    wb = jnp.einsum('ksgpin,gh->skgpihn', wb, eye).reshape(S5_SG, T * LANES, S5_NSTATE).astype(BF16)
    wc = jnp.stack([r6(clre[1:]), -r6(clim[1:])], axis=-2)
    wc = jnp.einsum('tsgpin,gh->signthp', wc, eye).reshape(S5_SG, S5_NSTATE, T * LANES).astype(BF16)
    lt = lambda a: a.reshape(N_GROUPS, GROUP, STATE)[:, 0].reshape(S5_SG, 1, S5_GL * STATE)
    return krev, wb, wc, lt(ltre), lt(ltim)


S5_TC = 512


def _s5_state_kernel(x_ref, wb_ref, e_ref):
    e_ref[0] = _dot(x_ref[0], wb_ref[0])


def _s5_scan_kernel(e_ref, ltre_ref, ltim_ref, xs_ref):
    lr, li = ltre_ref[0], ltim_ref[0]
    half = S5_NSTATE // 2

    def step(c, carry):
        new = []
        for b in range(BATCH):
            xr, xi = carry[b]
            row = b * S5_CH_PER_BATCH + c
            xs_ref[0, pl.ds(row, 1), :half] = xr
            xs_ref[0, pl.ds(row, 1), half:] = xi
            e = e_ref[0, pl.ds(row, 1), :]
            new.append((lr * xr - li * xi + e[:, :half], lr * xi + li * xr + e[:, half:]))
        return tuple(new)

    zero = jnp.zeros((1, half), F32)
    lax.fori_loop(0, S5_CH_PER_BATCH, step, tuple((zero, zero) for _ in range(BATCH)))


def _s5_out_kernel(x_ref, xs_ref, krev_ref, wc_ref, y_ref):
    xsb = xs_ref[0].astype(BF16)
    for t in range(S5_T):
        cols = slice(t * LANES, (t + 1) * LANES)
        y_ref[0, :, cols] = (_dot(x_ref[0, :, :(t + 1) * LANES], krev_ref[0, (S5_T - 1 - t) * LANES:, :])
                             + _dot(xsb, wc_ref[0, :, cols]))


def _s5(u, krev, wb, wc, ltre, ltim):
    T, tc = S5_T, S5_TC
    xs = u.astype(BF16).reshape(S5_CH, T, S5_SG, LANES).transpose(2, 0, 1, 3).reshape(S5_SG, S5_CH, T * LANES)
    grid = (S5_SG, S5_CH // tc)
    rows = lambda s, i: (s, i, 0)
    per_sg = lambda s, i: (s, 0, 0)
    e = pl.pallas_call(
        _s5_state_kernel, grid=grid,
        in_specs=[pl.BlockSpec((1, tc, T * LANES), rows), pl.BlockSpec((1, T * LANES, S5_NSTATE), per_sg)],
        out_specs=pl.BlockSpec((1, tc, S5_NSTATE), rows),
        out_shape=jax.ShapeDtypeStruct((S5_SG, S5_CH, S5_NSTATE), F32),
        compiler_params=_cparams("arbitrary", "arbitrary"), name="s5_state",
    )(xs, wb)
    sg1 = lambda s: (s, 0, 0)
    xstart = pl.pallas_call(
        _s5_scan_kernel, grid=(S5_SG,),
        in_specs=[pl.BlockSpec((1, S5_CH, S5_NSTATE), sg1),
                  pl.BlockSpec((1, 1, S5_NSTATE // 2), sg1), pl.BlockSpec((1, 1, S5_NSTATE // 2), sg1)],
        out_specs=pl.BlockSpec((1, S5_CH, S5_NSTATE), sg1),
        out_shape=jax.ShapeDtypeStruct((S5_SG, S5_CH, S5_NSTATE), F32),
        compiler_params=_cparams("arbitrary"), name="s5_scan",
    )(e, ltre, ltim)
    y = pl.pallas_call(
        _s5_out_kernel, grid=grid,
        in_specs=[pl.BlockSpec((1, tc, T * LANES), rows), pl.BlockSpec((1, tc, S5_NSTATE), rows),
                  pl.BlockSpec((1, T * LANES, LANES), per_sg), pl.BlockSpec((1, S5_NSTATE, T * LANES), per_sg)],
        out_specs=pl.BlockSpec((1, tc, T * LANES), rows),
        out_shape=jax.ShapeDtypeStruct((S5_SG, S5_CH, T * LANES), F32),
        compiler_params=_cparams("arbitrary", "arbitrary"), name="s5_out",
    )(xs, xstart, krev, wc)
    return y.reshape(S5_SG, S5_CH, T, LANES).transpose(1, 2, 0, 3).reshape(N_TOK, SSM_WIDTH)


MERGE_TM = 256


def _merge_kernel(ocmp_ref, osel_ref, owin_ref, gn_ref, yssm_ref, u_ref, ga_ref, gs_ref, x_ref, mod_ref,
                  eg_ref, dskip_ref, wglu_ref, bglu_ref, wua_ref, wus_ref, wout_ref, g2_ref,
                  wrhi_ref, wrlo_ref, wsgu_ref, wsd_ref,
                  xpart_ref, h2_ref, logit_ref):
    mod = mod_ref[0]
    gnb = gn_ref[...].astype(BF16)
    o_nsa = (_dot(gnb, eg_ref[0]) * ocmp_ref[...].astype(F32)
             + _dot(gnb, eg_ref[1]) * osel_ref[...].astype(F32)
             + _dot(gnb, eg_ref[2]) * owin_ref[...].astype(F32))
    attn = _dot(o_nsa.astype(BF16), wua_ref[...])
    z = _gelu(yssm_ref[...] + dskip_ref[...] * u_ref[...])
    y_ssm = z * jax.nn.sigmoid(_dot(z.astype(BF16), wglu_ref[...]) + bglu_ref[...])
    ssm = _dot(y_ssm.astype(BF16), wus_ref[...])
    merged = ga_ref[...].astype(F32) * attn + gs_ref[...].astype(F32) * ssm
    x1 = x_ref[...] + mod[2:3] * _dot(merged.astype(BF16), wout_ref[...])

    ms = jnp.mean(x1 * x1, axis=-1, keepdims=True)
    h2 = (x1 * lax.rsqrt(ms + EPS) * g2_ref[...]) * (1.0 + mod[4:5]) + mod[3:4]
    hi = h2.astype(BF16)
    lo = (h2 - hi.astype(F32)).astype(BF16)
    h2_ref[...] = _pack_bf16_pairs(h2)
    logit_ref[...] = _dot_nt(wrhi_ref[...], hi) + _dot_nt(wrhi_ref[...], lo) + _dot_nt(wrlo_ref[...], hi)
    gu = _dot(hi, wsgu_ref[...])
    shared = _dot((_silu(gu[:, :D_EXPERT]) * gu[:, D_EXPERT:]).astype(BF16), wsd_ref[...])
    xpart_ref[...] = x1 + mod[5:6] * shared


def _merge(ocmp, osel, owin, gn, yssm, u, ga, gs, x2, mod, d_skip, w_glu, b_glu, w_up_attn, w_up_ssm, w_out,
           g_norm2, w_router, ws_gate, ws_up, ws_down):
    tm = MERGE_TM
    eg = np.zeros((3, LANES, NSA_WIDTH), np.float32)
    for j in range(3):
        for h in range(N_HEADS):
            eg[j, 3 * h + j, h * HEAD_DIM:(h + 1) * HEAD_DIM] = 1.0
    wr_t = w_router.T
    wr_hi = wr_t.astype(BF16)
    wr_lo = (wr_t - wr_hi.astype(F32)).astype(BF16)
    row = lambda i: (i, 0)
    fix2 = lambda i: (0, 0)
    wspec = lambda a: pl.BlockSpec(a.shape, (lambda i: (0,) * a.ndim))
    weights = [jnp.asarray(eg, BF16), d_skip.reshape(1, -1), w_glu.astype(BF16), b_glu.reshape(1, -1),
               w_up_attn.astype(BF16), w_up_ssm.astype(BF16), w_out.astype(BF16), g_norm2.reshape(1, -1),
               wr_hi, wr_lo, jnp.concatenate([ws_gate, ws_up], axis=1).astype(BF16), ws_down.astype(BF16)]
    acts = [(ocmp, 512), (osel, 512), (owin, 512), (gn, 128), (yssm, 512), (u, 512), (ga, 1024), (gs, 1024),
            (x2, 1024)]
    return pl.pallas_call(
        _merge_kernel,
        grid=(N_TOK // tm,),
        in_specs=[pl.BlockSpec((tm, wd), row) for _, wd in acts]
                 + [pl.BlockSpec((1, 6, D_MODEL), lambda i: (i // (SEQ // tm), 0, 0))]
                 + [wspec(w) for w in weights],
        out_specs=[pl.BlockSpec((tm, D_MODEL), row), pl.BlockSpec((tm, HALF), row),
                   pl.BlockSpec((N_EXPERTS, tm), lambda i: (0, i))],
        out_shape=[jax.ShapeDtypeStruct((N_TOK, D_MODEL), F32), jax.ShapeDtypeStruct((N_TOK, HALF), jnp.uint32),
                   jax.ShapeDtypeStruct((N_EXPERTS, N_TOK), F32)],
        compiler_params=_cparams("arbitrary"),
        name="merge",
    )(*[a for a, _ in acts], mod, *weights)


ROUTE_TN = 512


def _route_kernel(logit_ref, bias_ref, eidx_ref, w_ref, count_ref, gscore_ref, masked_ref):
    tn = ROUTE_TN

    @pl.when(pl.program_id(0) == 0)
    def _():
        count_ref[...] = jnp.zeros(count_ref.shape, F32)

    sc = jax.nn.sigmoid(logit_ref[...])
    biased = sc + bias_ref[...]
    gi = lax.broadcasted_iota(I32, (EXPERTS_PER_GROUP, tn), 0).astype(F32)
    for g in range(N_EXPERT_GROUPS):
        blk = biased[g * EXPERTS_PER_GROUP:(g + 1) * EXPERTS_PER_GROUP]
        m1 = jnp.max(blk, axis=0, keepdims=True)
        i1 = jnp.min(jnp.where(blk == m1, gi, float(EXPERTS_PER_GROUP)), axis=0, keepdims=True)
        m2 = jnp.max(jnp.where(gi == i1, -jnp.inf, blk), axis=0, keepdims=True)
        gscore_ref[g:g + 1, :] = m1 + m2
    gs = gscore_ref[...]
    gidx = lax.broadcasted_iota(I32, (N_EXPERT_GROUPS, tn), 0)
    grank = jnp.zeros((N_EXPERT_GROUPS, tn), F32)
    for gp in range(N_EXPERT_GROUPS):
        row = gs[gp:gp + 1, :]
        tie = jnp.where(gidx > gp, 1.0, 0.0)
        grank = grank + jnp.where(row > gs, 1.0, jnp.where(row == gs, tie, 0.0))
    for g in range(N_EXPERT_GROUPS):
        keep = grank[g:g + 1, :] < float(TOPK_GROUPS)
        sl = slice(g * EXPERTS_PER_GROUP, (g + 1) * EXPERTS_PER_GROUP)
        masked_ref[sl, :] = jnp.where(keep, biased[sl], -jnp.inf)
    cur = masked_ref[...]
    eidx = lax.broadcasted_iota(I32, (N_EXPERTS, tn), 0).astype(F32)
    wsum = jnp.zeros((1, tn), F32)
    hits = jnp.zeros((N_EXPERTS, tn), F32)
    for k in range(TOP_K):
        m = jnp.max(cur, axis=0, keepdims=True)
        idx = jnp.min(jnp.where(cur == m, eidx, float(N_EXPERTS)), axis=0, keepdims=True)
        hit = eidx == idx
        wk = jnp.sum(jnp.where(hit, sc, 0.0), axis=0, keepdims=True)
        cur = jnp.where(hit, -jnp.inf, cur)
        hits = hits + jnp.where(hit, 1.0, 0.0)
        eidx_ref[k:k + 1, :] = idx.astype(I32)
        w_ref[k:k + 1, :] = wk
        wsum = wsum + wk
    w_ref[...] = w_ref[...] / wsum * ROUTE_SCALE
    count_ref[...] = count_ref[...] + jnp.sum(hits, axis=1, keepdims=True)


def _route(logits_t, router_bias):
    tn = ROUTE_TN
    return pl.pallas_call(
        _route_kernel,
        grid=(N_TOK // tn,),
        in_specs=[pl.BlockSpec((N_EXPERTS, tn), lambda i: (0, i)), pl.BlockSpec((N_EXPERTS, 1), lambda i: (0, 0))],
        out_specs=[pl.BlockSpec((TOP_K, tn), lambda i: (0, i))] * 2 + [pl.BlockSpec((N_EXPERTS, 1), lambda i: (0, 0))],
        out_shape=[jax.ShapeDtypeStruct((TOP_K, N_TOK), I32), jax.ShapeDtypeStruct((TOP_K, N_TOK), F32),
                   jax.ShapeDtypeStruct((N_EXPERTS, 1), F32)],
        scratch_shapes=[pltpu.VMEM((N_EXPERT_GROUPS, tn), F32), pltpu.VMEM((N_EXPERTS, tn), F32)],
        compiler_params=_cparams("arbitrary"),
        name="route",
    )(logits_t, router_bias.reshape(-1, 1))


N_MOE_BLK = NK // DISPATCH_BLOCK
N_ITEMS = N_MOE_BLK + N_EXPERTS
ASSIGN_BITS = 17


def _dispatch_plan(eidx, counts):
    e_flat = eidx.reshape(-1)
    key = jnp.sort(e_flat * NK + jnp.arange(NK, dtype=I32))
    order = key & (NK - 1)
    counts = counts.reshape(-1).astype(I32)
    start = jnp.cumsum(counts) - counts
    cuts = jnp.sort(jnp.concatenate([jnp.arange(N_MOE_BLK, dtype=I32) * DISPATCH_BLOCK, start]))
    lo = cuts
    hi = jnp.concatenate([cuts[1:], jnp.full((1,), NK, I32)])
    blk = jnp.minimum(lo // DISPATCH_BLOCK, N_MOE_BLK - 1)
    expert = jnp.clip(jnp.sum((start[None, :] <= lo[:, None]).astype(I32), axis=1) - 1, 0, N_EXPERTS - 1)
    one = jnp.ones((1,), I32)
    first = jnp.concatenate([one, (blk[1:] != blk[:-1]).astype(I32)])
    last = jnp.concatenate([(blk[1:] != blk[:-1]).astype(I32), one])
    new_expert = jnp.concatenate([one, (expert[1:] != expert[:-1]).astype(I32)])
    _, pos = lax.sort_key_val(order, jnp.arange(NK, dtype=I32))
    tok = jnp.right_shift(order, 3)
    return tok, pos, (blk, expert, lo - blk * DISPATCH_BLOCK, hi - blk * DISPATCH_BLOCK, first, last, new_expert)


SC_CORES = 2
SC_SUBCORES = 16
SC_CHUNK = 128


def _sc_gather_rows(table, idx):
    n = idx.shape[0]
    workers = SC_CORES * SC_SUBCORES
    per_worker = n // workers
    n_chunks = per_worker // SC_CHUNK
    assert per_worker * workers == n and n_chunks * SC_CHUNK == per_worker
    mesh = plsc.VectorSubcoreMesh(core_axis_name="c", subcore_axis_name="s",
                                  num_cores=SC_CORES, num_subcores=SC_SUBCORES)

    def body(table_hbm, idx_hbm, out_hbm, idx_v, rows_v, sem):
        wid = lax.axis_index("s") * SC_CORES + lax.axis_index("c")
        base = wid * per_worker

        @pl.loop(0, n_chunks)
        def _(j):
            off = base + j * SC_CHUNK
            pltpu.sync_copy(idx_hbm.at[pl.ds(off, SC_CHUNK)], idx_v)
            pltpu.async_copy(table_hbm.at[idx_v], rows_v, sem).wait()
            pltpu.sync_copy(rows_v, out_hbm.at[pl.ds(off, SC_CHUNK)])

    return pl.kernel(
        body,
        out_type=jax.ShapeDtypeStruct((n, table.shape[1]), table.dtype),
        mesh=mesh,
        scratch_types=[pltpu.VMEM((SC_CHUNK,), I32), pltpu.VMEM((SC_CHUNK, table.shape[1]), table.dtype),
                       pltpu.SemaphoreType.DMA],
        name="sc_gather_rows",
    )(table, idx)


def _moe_kernel(blk_ref, exp_ref, lo_ref, hi_ref, first_ref, last_ref, newexp_ref,
                x_ref, wg_ref, wu_ref, wd_ref, y_ref, acc_ref, wgb_ref, wub_ref, wdb_ref):
    it = pl.program_id(0)
    lo, hi = lo_ref[it], hi_ref[it]

    @pl.when(newexp_ref[it] == 1)
    def _():
        wgb_ref[...] = wg_ref[0].astype(BF16)
        wub_ref[...] = wu_ref[0].astype(BF16)
        wdb_ref[...] = wd_ref[0].astype(BF16)

    @pl.when(first_ref[it] == 1)
    def _():
        acc_ref[...] = jnp.zeros(acc_ref.shape, F32)

    @pl.when(hi > lo)
    def _():
        ridx = lax.broadcasted_iota(I32, x_ref.shape, 0)
        mine = (ridx >= lo) & (ridx < hi)
        xlo, xhi = _unpack_bf16_pairs(jnp.where(mine, x_ref[...], jnp.uint32(0)))
        xlo, xhi = xlo.astype(BF16), xhi.astype(BF16)
        gate = _dot(xlo, wgb_ref[:HALF]) + _dot(xhi, wgb_ref[HALF:])
        up = _dot(xlo, wub_ref[:HALF]) + _dot(xhi, wub_ref[HALF:])
        acc_ref[...] = acc_ref[...] + _dot((_silu(gate) * up).astype(BF16), wdb_ref[...])

    @pl.when(last_ref[it] == 1)
    def _():
        y_ref[...] = _pack_bf16_pairs(acc_ref[...])


def _moe(xs, items, w_gate, w_up, w_down):
    by_blk = lambda it, blk, *_: (blk[it], 0)
    by_exp = lambda it, blk, ex, *_: (ex[it], 0, 0)
    grid_spec = pltpu.PrefetchScalarGridSpec(
        num_scalar_prefetch=len(items),
        grid=(N_ITEMS,),
        in_specs=[pl.BlockSpec((DISPATCH_BLOCK, HALF), by_blk),
                  pl.BlockSpec((1, D_MODEL, D_EXPERT), by_exp),
                  pl.BlockSpec((1, D_MODEL, D_EXPERT), by_exp),
                  pl.BlockSpec((1, D_EXPERT, D_MODEL), by_exp)],
        out_specs=pl.BlockSpec((DISPATCH_BLOCK, HALF), by_blk),
        scratch_shapes=[pltpu.VMEM((DISPATCH_BLOCK, D_MODEL), F32),
                        pltpu.VMEM((D_MODEL, D_EXPERT), BF16), pltpu.VMEM((D_MODEL, D_EXPERT), BF16),
                        pltpu.VMEM((D_EXPERT, D_MODEL), BF16)],
    )
    return pl.pallas_call(
        _moe_kernel,
        grid_spec=grid_spec,
        out_shape=jax.ShapeDtypeStruct((NK, HALF), jnp.uint32),
        compiler_params=_cparams("arbitrary"),
        name="moe",
    )(*items, xs, w_gate, w_up, w_down)


COMB_TC = 64


def _combine_kernel(slots_ref, w_ref, xpart_ref, mod_ref, out_ref):
    w = w_ref[...]
    lo = jnp.zeros((w.shape[0], HALF), F32)
    hi = jnp.zeros((w.shape[0], HALF), F32)
    for k in range(TOP_K):
        klo, khi = _unpack_bf16_pairs(slots_ref[:, k, :])
        lo = lo + w[:, k:k + 1] * klo
        hi = hi + w[:, k:k + 1] * khi
    gate2 = mod_ref[0][5:6]
    out_ref[:, :HALF] = xpart_ref[:, :HALF] + gate2[:, :HALF] * lo
    out_ref[:, HALF:] = xpart_ref[:, HALF:] + gate2[:, HALF:] * hi


def _combine(xpart, mod, slots, w):
    tc = COMB_TC
    row = lambda i: (i, 0)
    return pl.pallas_call(
        _combine_kernel,
        grid=(N_TOK // tc,),
        in_specs=[pl.BlockSpec((tc, TOP_K, HALF), lambda i: (i, 0, 0)),
                  pl.BlockSpec((tc, TOP_K), row),
                  pl.BlockSpec((tc, D_MODEL), row),
                  pl.BlockSpec((1, 6, D_MODEL), lambda i: (i // (SEQ // tc), 0, 0))],
        out_specs=pl.BlockSpec((tc, D_MODEL), row),
        out_shape=jax.ShapeDtypeStruct((N_TOK, D_MODEL), F32),
        compiler_params=_cparams("arbitrary"),
        name="combine",
    )(slots.reshape(N_TOK, TOP_K, HALF), w, xpart, mod)


def _layer(x, c, w_ada, b_ada, g_norm1, g_norm2, w_in, q_gain, kc_gain, ks_gain, kw_gain,
           pe_k, pe_v, w_cmp_k1, w_cmp_k2, w_cmp_v1, w_cmp_v2,
           a_re, a_im, log_dt, b_re, b_im, c_re, c_im, d_skip, w_glu, b_glu,
           w_up_attn, w_up_ssm, w_out, w_router, router_bias,
           w_gate, w_up, w_down, ws_gate, ws_up, ws_down):
    x2 = x.reshape(N_TOK, D_MODEL)
    mod = _ada(c, w_ada, b_ada)
    q, kc_raw, vc_raw, ks, kw, vst, vwt, gn, u, ga, gs = _proj(x2, mod, g_norm1, w_in, q_gain, ks_gain, kw_gain)
    kcn = _compress(kc_raw, pe_k, w_cmp_k1, w_cmp_k2, kc_gain, True)
    vcn = _compress(vc_raw, pe_v, w_cmp_v1, w_cmp_v2, kc_gain, False)
    ocmp, selb = _cmp_attn(q, kcn, vcn)
    osel, owin = _selwin(q, ks, kw, vst, vwt, selb)
    yssm = _s5(u, *_s5_params(a_re, a_im, log_dt, b_re, b_im, c_re, c_im))
    xpart, h2, logits_t = _merge(ocmp, osel, owin, gn, yssm, u, ga, gs, x2, mod, d_skip, w_glu, b_glu,
                                  w_up_attn, w_up_ssm, w_out, g_norm2, w_router, ws_gate, ws_up, ws_down)
    eidx_t, w_t, counts = _route(logits_t, router_bias)
    tok, pos, items = _dispatch_plan(eidx_t.T, counts)
    y = _moe(_sc_gather_rows(h2, tok), items, w_gate, w_up, w_down)
    slots = _sc_gather_rows(y, pos)
    return _combine(xpart, mod, slots, w_t.T).reshape(BATCH, SEQ, D_MODEL)


def kernel(x, c, w_ada, b_ada, g_norm1, g_norm2, w_in, q_gain, kc_gain, ks_gain, kw_gain, pe_k, pe_v, w_cmp_k1,
           w_cmp_k2, w_cmp_v1, w_cmp_v2, a_re, a_im, log_dt, b_re, b_im, c_re, c_im, d_skip, w_glu, b_glu,
           w_up_attn, w_up_ssm, w_out, w_router, router_bias, w_gate, w_up, w_down, ws_gate, ws_up, ws_down):
    params = (w_ada, b_ada, g_norm1, g_norm2, w_in, q_gain, kc_gain, ks_gain, kw_gain, pe_k, pe_v, w_cmp_k1,
              w_cmp_k2, w_cmp_v1, w_cmp_v2, a_re, a_im, log_dt, b_re, b_im, c_re, c_im, d_skip, w_glu, b_glu,
              w_up_attn, w_up_ssm, w_out, w_router, router_bias, w_gate, w_up, w_down, ws_gate, ws_up, ws_down)
    depth = w_ada.shape[0]
    for layer in range(depth):
        x = _layer(x, c, *[p[layer] for p in params])
    return x
```

```python
import functools
import math

import jax
import jax.numpy as jnp
import numpy as np
from jax import lax
from jax.experimental import pallas as pl
from jax.experimental.pallas import tpu as pltpu
from jax.experimental.pallas import tpu_sc as plsc

F32 = jnp.float32
BF16 = jnp.bfloat16
I32 = jnp.int32
HIGHEST = lax.Precision.HIGHEST

D_MODEL = 1024
BATCH = 4
SEQ = 4096
N_TOK = BATCH * SEQ
N_HEADS = 8
HEAD_DIM = 64
N_KV = 2
CMP_BLOCK = 32
CMP_STRIDE = 16
CMP_HIDDEN = 256
N_CMP = 256
SEL_BLOCK = 64
N_SEL_BLOCKS = SEQ // SEL_BLOCK
N_SELECT = 16
WINDOW = 512
ATTN_SCALE = HEAD_DIM ** -0.5
LOG2E = 1.4426950408889634
NSA_WIDTH = N_HEADS * HEAD_DIM
SSM_WIDTH = 512
GROUP = 16
N_GROUPS = SSM_WIDTH // GROUP
STATE = 64
N_EXPERTS = 256
TOP_K = 8
D_EXPERT = 256
N_EXPERT_GROUPS = 8
EXPERTS_PER_GROUP = N_EXPERTS // N_EXPERT_GROUPS
TOPK_GROUPS = 4
ROUTE_SCALE = 2.5
DISPATCH_BLOCK = 256
EPS = 1e-6
NEG = -1e30

LANES = 128
S5_T = 16
S5_SG = 4
S5_GL = N_GROUPS // S5_SG
S5_CH = N_TOK // S5_T
S5_CH_PER_BATCH = SEQ // S5_T
S5_NSTATE = S5_GL * STATE * 2

NK = N_TOK * TOP_K
HALF = D_MODEL // 2

VMEM_LIMIT = 48 * 1024 * 1024


def _cparams(*sem):
    return pltpu.CompilerParams(dimension_semantics=tuple(sem), vmem_limit_bytes=VMEM_LIMIT)


def _dot(a, b):
    return jnp.dot(a, b, preferred_element_type=F32)


def _dot_nt(a, b):
    return lax.dot_general(a, b, (((1,), (1,)), ((), ())), preferred_element_type=F32)


def _split_dot(v, w):
    hi = v.astype(BF16)
    lo = (v - hi.astype(F32)).astype(BF16)
    return _dot(hi, w) + _dot(lo, w)


def _seg_rms(v, bd, gain):
    ss = _split_dot(v * v, bd)
    return v * lax.rsqrt(ss * (1.0 / HEAD_DIM) + EPS) * gain


def _gelu(x):
    return 0.5 * x * (1.0 + jnp.tanh(0.7978845608028654 * (x + 0.044715 * (x * x * x))))


def _silu(x):
    return x * jax.nn.sigmoid(x)


def _pack_bf16_pairs(v):
    bits = lax.bitcast_convert_type(v.astype(BF16).astype(F32), jnp.uint32)
    h = v.shape[1] // 2
    return bits[:, h:] | lax.shift_right_logical(bits[:, :h], jnp.uint32(16))


def _unpack_bf16_pairs(word):
    lo = lax.bitcast_convert_type(lax.shift_left(word, jnp.uint32(16)), F32)
    hi = lax.bitcast_convert_type(word & jnp.uint32(0xFFFF0000), F32)
    return lo, hi


def _ada_kernel(c_ref, w_ref, b_ref, o_ref):
    c = c_ref[...]
    o_ref[...] = jnp.dot(_silu(c), w_ref[...], preferred_element_type=F32, precision=HIGHEST) + b_ref[...]


def _ada(c, w_ada, b_ada):
    cp = jnp.pad(c, ((0, 8 - BATCH), (0, 0)))
    tn = 1536
    out = pl.pallas_call(
        _ada_kernel,
        grid=(6 * D_MODEL // tn,),
        in_specs=[pl.BlockSpec((8, D_MODEL), lambda j: (0, 0)),
                  pl.BlockSpec((D_MODEL, tn), lambda j: (0, j)),
                  pl.BlockSpec((1, tn), lambda j: (0, j))],
        out_specs=pl.BlockSpec((8, tn), lambda j: (0, j)),
        out_shape=jax.ShapeDtypeStruct((8, 6 * D_MODEL), F32),
        compiler_params=_cparams("arbitrary"),
        name="ada",
    )(cp, w_ada, b_ada.reshape(1, -1))
    return out.reshape(8, 6, D_MODEL)


_C_Q = 0
_C_KC = 512
_C_VC = 640
_C_KS = 768
_C_KW = 1024
_C_GN = 1280
_C_U = 1408
_C_GA = 1920
_C_GS = 2944
_C_END = 3968
PROJ_TM = 512


def _proj_kernel(x_ref, mod_ref, g1_ref, w_ref, wvt_ref, qg_ref, ksg_ref, kwg_ref, bd512_ref, bd256_ref,
                 q_ref, kc_ref, vc_ref, ks_ref, kw_ref, vst_ref, vwt_ref, gn_ref, u_ref, ga_ref, gs_ref):
    x = x_ref[...]
    ms = jnp.mean(x * x, axis=-1, keepdims=True)
    mod = mod_ref[0]
    h = (x * lax.rsqrt(ms + EPS) * g1_ref[...]) * (1.0 + mod[1:2]) + mod[0:1]
    hb = h.astype(BF16)

    def p(lo, hi):
        return _dot(hb, w_ref[:, lo:hi])

    q_ref[...] = _seg_rms(p(_C_Q, _C_KC), bd512_ref[...], qg_ref[...] * (ATTN_SCALE * LOG2E)).astype(BF16)
    kc_ref[...] = p(_C_KC, _C_VC).astype(BF16)
    vc_ref[...] = p(_C_VC, _C_KS).astype(BF16)
    ks_ref[...] = _seg_rms(p(_C_KS, _C_KW), bd256_ref[...], ksg_ref[...]).astype(BF16)
    kw_ref[...] = _seg_rms(p(_C_KW, _C_GN), bd256_ref[...], kwg_ref[...]).astype(BF16)
    vt = _dot_nt(wvt_ref[...], hb)
    vst_ref[...] = vt[:LANES].astype(BF16)
    vwt_ref[...] = vt[LANES:].astype(BF16)
    gn_ref[...] = jax.nn.sigmoid(p(_C_GN, _C_U))
    u_ref[...] = p(_C_U, _C_GA)
    ga_ref[...] = jax.nn.sigmoid(p(_C_GA, _C_GS)).astype(BF16)
    gs_ref[...] = jax.nn.sigmoid(p(_C_GS, _C_END)).astype(BF16)


def _dup_cols(w):
    return jnp.concatenate([w[:, :64], w[:, :64], w[:, 64:], w[:, 64:]], axis=1)


def _block_ones(n):
    return jnp.kron(jnp.eye(n // HEAD_DIM, dtype=F32), jnp.ones((HEAD_DIM, HEAD_DIM), F32)).astype(BF16)


def _proj(x2, mod, g_norm1, w_in, q_gain, ks_gain, kw_gain):
    o = np.cumsum((0, 512, 128, 128, 128, 128, 128, 128, 24, 512, 1024, 1024))
    parts = [w_in[:, o[i]:o[i + 1]] for i in range(11)]
    wq, wkc, wvc, wks, wvs, wkw, wvw, wgn, wu, wga, wgs = parts
    w = jnp.concatenate([wq, wkc, wvc, _dup_cols(wks), _dup_cols(wkw),
                         jnp.pad(wgn, ((0, 0), (0, LANES - 24))), wu, wga, wgs], axis=1).astype(BF16)
    wvt = jnp.concatenate([wvs, wvw], axis=1).T.astype(BF16)
    tm = PROJ_TM
    row = lambda i: (i, 0)
    col = lambda i: (0, i)
    fix = lambda i: (0, 0)
    outs = [(512, BF16, row), (128, BF16, row), (128, BF16, row), (256, BF16, row), (256, BF16, row),
            (LANES, BF16, col), (LANES, BF16, col),
            (128, F32, row), (512, F32, row), (1024, BF16, row), (1024, BF16, row)]
    ospec = lambda wd, m: pl.BlockSpec((tm, wd), m) if m is row else pl.BlockSpec((wd, tm), m)
    oshape = lambda wd, dt, m: jax.ShapeDtypeStruct((N_TOK, wd) if m is row else (wd, N_TOK), dt)
    return pl.pallas_call(
        _proj_kernel,
        grid=(N_TOK // tm,),
        in_specs=[pl.BlockSpec((tm, D_MODEL), row),
                  pl.BlockSpec((1, 6, D_MODEL), lambda i: (i // (SEQ // tm), 0, 0)),
                  pl.BlockSpec((1, D_MODEL), fix),
                  pl.BlockSpec((D_MODEL, _C_END), fix),
                  pl.BlockSpec((2 * LANES, D_MODEL), fix),
                  pl.BlockSpec((1, 512), fix), pl.BlockSpec((1, 256), fix), pl.BlockSpec((1, 256), fix),
                  pl.BlockSpec((512, 512), fix), pl.BlockSpec((256, 256), fix)],
        out_specs=[ospec(wd, m) for wd, _, m in outs],
        out_shape=[oshape(wd, dt, m) for wd, dt, m in outs],
        compiler_params=_cparams("arbitrary"),
        name="proj",
    )(x2, mod, g_norm1.reshape(1, -1), w, wvt,
      jnp.tile(q_gain, N_HEADS).reshape(1, -1), jnp.tile(ks_gain, 4).reshape(1, -1),
      jnp.tile(kw_gain, 4).reshape(1, -1), _block_ones(512), _block_ones(256))


def _compress_kernel(r_ref, pe_ref, w1_ref, w2_ref, bd_ref, gain_ref, o_ref, *, do_norm):
    r = r_ref[0].astype(F32)
    p0 = _dot((r + pe_ref[0]).astype(BF16), w1_ref[0])
    p1 = _dot((r + pe_ref[1]).astype(BF16), w1_ref[1])
    hid = p0 + pltpu.roll(p1, N_CMP - 1, 0)
    c = _dot(_gelu(hid).astype(BF16), w2_ref[...])
    if do_norm:
        c = _seg_rms(c, bd_ref[...], gain_ref[...])
    o_ref[0] = c.astype(BF16)


def _compress(raw, pe, w1, w2, gain, do_norm):
    r = raw.reshape(BATCH, SEQ // CMP_STRIDE, CMP_STRIDE * LANES)
    eye = jnp.eye(N_KV, dtype=F32)
    w1r = w1.reshape(2, CMP_STRIDE, HEAD_DIM, CMP_HIDDEN)
    w1big = jnp.einsum('hldc,gk->hlgdkc', w1r, eye).reshape(2, CMP_STRIDE * LANES, N_KV * CMP_HIDDEN).astype(BF16)
    w2big = jnp.einsum('cd,gk->gckd', w2, eye)
    w2big = jnp.concatenate([w2big, w2big], axis=-1).reshape(N_KV * CMP_HIDDEN, 4 * HEAD_DIM).astype(BF16)
    pe_big = jnp.broadcast_to(pe.reshape(2, CMP_STRIDE, 1, HEAD_DIM), (2, CMP_STRIDE, N_KV, HEAD_DIM))
    pe_big = pe_big.reshape(2, 1, CMP_STRIDE * LANES)
    fix2 = lambda b: (0, 0)
    fix3 = lambda b: (0, 0, 0)
    return pl.pallas_call(
        functools.partial(_compress_kernel, do_norm=do_norm),
        grid=(BATCH,),
        in_specs=[pl.BlockSpec((1, N_CMP, CMP_STRIDE * LANES), lambda b: (b, 0, 0)),
                  pl.BlockSpec((2, 1, CMP_STRIDE * LANES), fix3),
                  pl.BlockSpec((2, CMP_STRIDE * LANES, N_KV * CMP_HIDDEN), fix3),
                  pl.BlockSpec((N_KV * CMP_HIDDEN, 256), fix2),
                  pl.BlockSpec((256, 256), fix2), pl.BlockSpec((1, 256), fix2)],
        out_specs=pl.BlockSpec((1, N_CMP, 256), lambda b: (b, 0, 0)),
        out_shape=jax.ShapeDtypeStruct((BATCH, N_CMP, 256), BF16),
        compiler_params=_cparams("arbitrary"),
        name="compress_k" if do_norm else "compress_v",
    )(r, pe_big, w1big, w2big, _block_ones(256), jnp.tile(gain, 4).reshape(1, -1))


ATT_TQ = 256


def _head_variants(qb):
    lane = lax.broadcasted_iota(I32, qb.shape, 1)
    z = jnp.zeros_like(qb)
    return jnp.where(lane < HEAD_DIM, qb, z), jnp.where(lane < HEAD_DIM, z, qb)


def _cmp_kernel(q_ref, kc_ref, vc_ref, ov_ref, o_ref, sel_ref):
    tq = ATT_TQ
    qi = pl.program_id(1)
    tpos = qi * tq + lax.broadcasted_iota(I32, (tq, N_CMP), 0)
    nidx = lax.broadcasted_iota(I32, (tq, N_CMP), 1)
    mask = (CMP_STRIDE * nidx + (CMP_BLOCK - 1)) <= tpos
    lane_lo = lax.broadcasted_iota(I32, (tq, LANES), 1) < HEAD_DIM
    for g in range(N_KV):
        kd = kc_ref[0, :, g * LANES:(g + 1) * LANES]
        vd = vc_ref[0, :, g * LANES:(g + 1) * LANES]
        psum = jnp.zeros((tq, N_CMP), F32)
        for jb in range(2):
            blk = 2 * g + jb
            pv = []
            for qv in _head_variants(q_ref[:, blk * LANES:(blk + 1) * LANES]):
                s = jnp.where(mask, _dot_nt(qv, kd), NEG)
                m = jnp.max(s, axis=-1, keepdims=True)
                e = jnp.where(mask, jnp.exp2(s - m), 0.0)
                l = jnp.sum(e, axis=-1, keepdims=True)
                p = e / jnp.where(l > 0.0, l, 1.0)
                psum = psum + p
                pv.append(_dot(p.astype(BF16), vd))
            o_ref[:, blk * LANES:(blk + 1) * LANES] = jnp.where(lane_lo, pv[0], pv[1]).astype(BF16)
        imp = _split_dot(psum, ov_ref[...])
        imp_t = imp.T[:N_SEL_BLOCKS]
        j = lax.broadcasted_iota(I32, (N_SEL_BLOCKS, tq), 0)
        cur = jnp.right_shift(qi * tq + lax.broadcasted_iota(I32, (N_SEL_BLOCKS, tq), 1), 6)
        forced = (j == 0) | (j == cur) | (j == cur - 1)
        v = jnp.where(forced, jnp.inf, jnp.where(j <= cur, imp_t, -jnp.inf))
        rank = jnp.zeros((N_SEL_BLOCKS, tq), F32)
        for jp in range(N_SEL_BLOCKS):
            row = v[jp:jp + 1, :]
            tie = jnp.where(j > jp, 1.0, 0.0)
            rank = rank + jnp.where(row > v, 1.0, jnp.where(row == v, tie, 0.0))
        sel_ref[g * N_SEL_BLOCKS:(g + 1) * N_SEL_BLOCKS, :] = jnp.where(rank < float(N_SELECT), 0.0, NEG)


def _cmp_attn(q, kcn, vcn):
    nc = np.arange(N_CMP)
    sb = np.arange(LANES)
    ov = ((CMP_STRIDE * nc[:, None] < SEL_BLOCK * sb[None, :] + SEL_BLOCK)
          & (CMP_STRIDE * nc[:, None] + CMP_BLOCK > SEL_BLOCK * sb[None, :])
          & (nc[:, None] < N_CMP - 1) & (sb[None, :] < N_SEL_BLOCKS))
    ov = jnp.asarray(ov, BF16)
    tq = ATT_TQ
    nq = SEQ // tq
    row = lambda b, i: (b * nq + i, 0)
    return pl.pallas_call(
        _cmp_kernel,
        grid=(BATCH, nq),
        in_specs=[pl.BlockSpec((tq, NSA_WIDTH), row),
                  pl.BlockSpec((1, N_CMP, 256), lambda b, i: (b, 0, 0)),
                  pl.BlockSpec((1, N_CMP, 256), lambda b, i: (b, 0, 0)),
                  pl.BlockSpec((N_CMP, LANES), lambda b, i: (0, 0))],
        out_specs=[pl.BlockSpec((tq, NSA_WIDTH), row),
                   pl.BlockSpec((N_KV * N_SEL_BLOCKS, tq), lambda b, i: (0, b * nq + i))],
        out_shape=[jax.ShapeDtypeStruct((N_TOK, NSA_WIDTH), BF16),
                   jax.ShapeDtypeStruct((N_KV * N_SEL_BLOCKS, N_TOK), F32)],
        compiler_params=_cparams("arbitrary", "arbitrary"),
        name="cmp_attn",
    )(q, kcn, vcn, ov)


ATT_TK = 256


M_INIT = -1e29


def _selwin_kernel(q_ref, ks_ref, kw_ref, vst_ref, vwt_ref, selb_ref, osel_ref, owin_ref, m_ref, l_ref, acc_ref):
    tq, tk = ATT_TQ, ATT_TK
    qi = pl.program_id(1)
    krow = lax.broadcasted_iota(I32, (tk, tq), 0)
    qcol = lax.broadcasted_iota(I32, (tk, tq), 1)
    causal_bias = jnp.where(krow <= qcol, 0.0, NEG)
    far_bias = jnp.where(qcol < krow, 0.0, NEG)

    def reset():
        m_ref[...] = jnp.full(m_ref.shape, M_INIT, F32)
        l_ref[...] = jnp.zeros(l_ref.shape, F32)
        acc_ref[...] = jnp.zeros(acc_ref.shape, F32)

    def update(qvars, kd, vt, bias):
        s = _dot_nt(kd, qvars)
        if bias is not None:
            s = s + jnp.concatenate([bias] * 4, axis=1)
        m_old = m_ref[...]
        m_new = jnp.maximum(m_old, jnp.max(s, axis=0, keepdims=True))
        alpha = jnp.exp2(m_old - m_new)
        p = jnp.exp2(s - m_new)
        l_ref[...] = alpha * l_ref[...] + jnp.sum(p, axis=0, keepdims=True)
        m_ref[...] = m_new
        acc_ref[...] = alpha * acc_ref[...] + _dot(vt, p.astype(BF16))

    def finish(out_ref, g):
        o = acc_ref[...] / l_ref[...]
        for jb in range(2):
            blk = 2 * g + jb
            pair = jnp.concatenate([o[:, 2 * jb * tq:(2 * jb + 1) * tq], o[:, (2 * jb + 1) * tq:(2 * jb + 2) * tq]],
                                   axis=0)
            out_ref[:, blk * LANES:(blk + 1) * LANES] = pair.T.astype(BF16)

    def sel_bias(g, kt):
        rows = [jnp.broadcast_to(selb_ref[pl.ds(g * N_SEL_BLOCKS + kt * (tk // SEL_BLOCK) + r, 1), :],
                                 (SEL_BLOCK, tq)) for r in range(tk // SEL_BLOCK)]
        return jnp.concatenate(rows, axis=0)

    for g in range(N_KV):
        gl = slice(g * LANES, (g + 1) * LANES)
        gv = slice(g * HEAD_DIM, (g + 1) * HEAD_DIM)
        qvars = []
        for jb in range(2):
            qvars.extend(_head_variants(q_ref[:, (2 * g + jb) * LANES:(2 * g + jb + 1) * LANES]))
        qvars = jnp.concatenate(qvars, axis=0)

        def k_tile(ref, kt):
            return ref[0, pl.ds(pl.multiple_of(kt * tk, tk), tk), gl]

        def v_tile(ref, kt):
            return ref[gv, pl.ds(pl.multiple_of(kt * tk, tk), tk)]

        reset()

        def sel_step(kt, carry):
            update(qvars, k_tile(ks_ref, kt), v_tile(vst_ref, kt), sel_bias(g, kt))
            return carry

        lax.fori_loop(0, qi, sel_step, 0)
        update(qvars, k_tile(ks_ref, qi), v_tile(vst_ref, qi), sel_bias(g, qi) + causal_bias)
        finish(osel_ref, g)

        reset()

        @pl.when(qi >= 2)
        def _():
            update(qvars, k_tile(kw_ref, qi - 2), v_tile(vwt_ref, qi - 2), far_bias)

        @pl.when(qi >= 1)
        def _():
            update(qvars, k_tile(kw_ref, qi - 1), v_tile(vwt_ref, qi - 1), None)

        update(qvars, k_tile(kw_ref, qi), v_tile(vwt_ref, qi), causal_bias)
        finish(owin_ref, g)


def _selwin(q, ks, kw, vst, vwt, selb):
    tq = ATT_TQ
    nq = SEQ // tq
    assert WINDOW == 2 * ATT_TK and ATT_TQ == ATT_TK
    row = lambda b, i: (b * nq + i, 0)
    keys = pl.BlockSpec((1, SEQ, 256), lambda b, i: (b, 0, 0))
    vals = pl.BlockSpec((LANES, SEQ), lambda b, i: (0, b))
    r3 = lambda a: a.reshape(BATCH, SEQ, 256)
    return pl.pallas_call(
        _selwin_kernel,
        grid=(BATCH, nq),
        in_specs=[pl.BlockSpec((tq, NSA_WIDTH), row), keys, keys, vals, vals,
                  pl.BlockSpec((N_KV * N_SEL_BLOCKS, tq), lambda b, i: (0, b * nq + i))],
        out_specs=[pl.BlockSpec((tq, NSA_WIDTH), row)] * 2,
        out_shape=[jax.ShapeDtypeStruct((N_TOK, NSA_WIDTH), BF16)] * 2,
        scratch_shapes=[pltpu.VMEM((1, 4 * tq), F32), pltpu.VMEM((1, 4 * tq), F32),
                        pltpu.VMEM((HEAD_DIM, 4 * tq), F32)],
        compiler_params=_cparams("arbitrary", "arbitrary"),
        name="selwin",
    )(q, r3(ks), r3(kw), vst, vwt, selb)


def _s5_param_kernel(are_ref, aim_ref, ldt_ref, cre_ref, cim_ref, bre_ref, bim_ref,
                     clre_ref, clim_ref, wbre_ref, wbim_ref, bbre_ref, bbim_ref, ltre_ref, ltim_ref):
    are, aim = are_ref[...], aim_ref[...]
    dt = jnp.exp(ldt_ref[...])
    cre, cim = cre_ref[...], cim_ref[...]

    def lam_pow(tau):
        mag = jnp.exp(are * dt * float(tau))
        ang = aim * dt * float(tau)
        return mag * jnp.cos(ang), mag * jnp.sin(ang)

    lre, lim = lam_pow(1)
    den = are * are + aim * aim
    qre = ((lre - 1.0) * are + lim * aim) / den
    qim = (lim * are - (lre - 1.0) * aim) / den
    bre, bim = bre_ref[...], bim_ref[...]
    bbre = qre * bre - qim * bim
    bbim = qre * bim + qim * bre
    bbre_ref[...] = bbre
    bbim_ref[...] = bbim
    for tau in range(S5_T + 1):
        pr, pi = lam_pow(tau)
        clre_ref[tau] = cre * pr - cim * pi
        clim_ref[tau] = cre * pi + cim * pr
        if tau < S5_T:
            k = S5_T - 1 - tau
            wbre_ref[k] = pr * bbre - pi * bbim
            wbim_ref[k] = pr * bbim + pi * bbre
        else:
            ltre_ref[...] = pr
            ltim_ref[...] = pi


def _s5_kmat_kernel(l_ref, r_ref, o_ref):
    o_ref[0] = jnp.dot(l_ref[0], r_ref[0], preferred_element_type=F32, precision=HIGHEST)


def _s5_params(a_re, a_im, log_dt, b_re, b_im, c_re, c_im):
    T = S5_T
    pn = GROUP * STATE
    tile_p = lambda a: jnp.tile(a, (1, GROUP))
    args = (tile_p(a_re), tile_p(a_im), jnp.broadcast_to(log_dt[:, None], (N_GROUPS, pn)),
            c_re.reshape(N_GROUPS, pn), c_im.reshape(N_GROUPS, pn),
            jnp.swapaxes(b_re, 1, 2).reshape(N_GROUPS, pn), jnp.swapaxes(b_im, 1, 2).reshape(N_GROUPS, pn))
    full2 = pl.BlockSpec((N_GROUPS, pn), lambda: (0, 0))
    clre, clim, wbre, wbim, bbre, bbim, ltre, ltim = pl.pallas_call(
        _s5_param_kernel,
        in_specs=[full2] * 7,
        out_specs=[pl.BlockSpec((T + 1, N_GROUPS, pn), lambda: (0, 0, 0))] * 2
                  + [pl.BlockSpec((T, N_GROUPS, pn), lambda: (0, 0, 0))] * 2 + [full2] * 4,
        out_shape=[jax.ShapeDtypeStruct((T + 1, N_GROUPS, pn), F32)] * 2
                  + [jax.ShapeDtypeStruct((T, N_GROUPS, pn), F32)] * 2
                  + [jax.ShapeDtypeStruct((N_GROUPS, pn), F32)] * 4,
        name="s5_params",
    )(*args)

    r5 = lambda a, t: a[:t].reshape(t, N_GROUPS, GROUP, STATE)
    lhs = jnp.concatenate([r5(clre, T), -r5(clim, T)], axis=-1)
    lhs = jnp.transpose(lhs, (1, 0, 2, 3)).reshape(N_GROUPS, T * GROUP, 2 * STATE)
    bb = lambda a: jnp.swapaxes(a.reshape(N_GROUPS, GROUP, STATE), 1, 2)
    rhs = jnp.concatenate([bb(bbre), bb(bbim)], axis=1)
    kmat = pl.pallas_call(
        _s5_kmat_kernel,
        grid=(N_GROUPS,),
        in_specs=[pl.BlockSpec((1, T * GROUP, 2 * STATE), lambda g: (g, 0, 0)),
                  pl.BlockSpec((1, 2 * STATE, GROUP), lambda g: (g, 0, 0))],
        out_specs=pl.BlockSpec((1, T * GROUP, GROUP), lambda g: (g, 0, 0)),
        out_shape=jax.ShapeDtypeStruct((N_GROUPS, T * GROUP, GROUP), F32),
        compiler_params=_cparams("arbitrary"),
        name="s5_kmat",
    )(lhs, rhs)

    eye = jnp.eye(S5_GL, dtype=F32)
    kt = kmat.reshape(S5_SG, S5_GL, T, GROUP, GROUP)
    kbd = jnp.einsum('sgtpq,gh->stgqhp', kt, eye).reshape(S5_SG, T, LANES, LANES)
    krev = kbd[:, ::-1].reshape(S5_SG, T * LANES, LANES).astype(BF16)
    r6 = lambda a: a.reshape(T, S5_SG, S5_GL, GROUP, STATE)
    wb = jnp.stack([r6(wbre), r6(wbim)], axis=-2)
    wb = jnp.einsum('ksgpin,gh->skgpihn', wb, eye).reshape(S5_SG, T * LANES, S5_NSTATE).astype(BF16)
    wc = jnp.stack([r6(clre[1:]), -r6(clim[1:])], axis=-2)
    wc = jnp.einsum('tsgpin,gh->signthp', wc, eye).reshape(S5_SG, S5_NSTATE, T * LANES).astype(BF16)
    lt = lambda a: a.reshape(N_GROUPS, GROUP, STATE)[:, 0].reshape(S5_SG, 1, S5_GL * STATE)
    return krev, wb, wc, lt(ltre), lt(ltim)


S5_TC = 512


def _s5_state_kernel(x_ref, wb_ref, e_ref):
    e_ref[0] = _dot(x_ref[0], wb_ref[0])


def _s5_scan_kernel(e_ref, ltre_ref, ltim_ref, xs_ref):
    lr, li = ltre_ref[0], ltim_ref[0]
    half = S5_NSTATE // 2

    def step(c, carry):
        new = []
        for b in range(BATCH):
            xr, xi = carry[b]
            row = b * S5_CH_PER_BATCH + c
            xs_ref[0, pl.ds(row, 1), :half] = xr
            xs_ref[0, pl.ds(row, 1), half:] = xi
            e = e_ref[0, pl.ds(row, 1), :]
            new.append((lr * xr - li * xi + e[:, :half], lr * xi + li * xr + e[:, half:]))
        return tuple(new)

    zero = jnp.zeros((1, half), F32)
    lax.fori_loop(0, S5_CH_PER_BATCH, step, tuple((zero, zero) for _ in range(BATCH)))


def _s5_out_kernel(x_ref, xs_ref, krev_ref, wc_ref, y_ref):
    xsb = xs_ref[0].astype(BF16)
    for t in range(S5_T):
        cols = slice(t * LANES, (t + 1) * LANES)
        y_ref[0, :, cols] = (_dot(x_ref[0, :, :(t + 1) * LANES], krev_ref[0, (S5_T - 1 - t) * LANES:, :])
                             + _dot(xsb, wc_ref[0, :, cols]))


def _s5(u, krev, wb, wc, ltre, ltim):
    T, tc = S5_T, S5_TC
    xs = u.astype(BF16).reshape(S5_CH, T, S5_SG, LANES).transpose(2, 0, 1, 3).reshape(S5_SG, S5_CH, T * LANES)
    grid = (S5_SG, S5_CH // tc)
    rows = lambda s, i: (s, i, 0)
    per_sg = lambda s, i: (s, 0, 0)
    e = pl.pallas_call(
        _s5_state_kernel, grid=grid,
        in_specs=[pl.BlockSpec((1, tc, T * LANES), rows), pl.BlockSpec((1, T * LANES, S5_NSTATE), per_sg)],
        out_specs=pl.BlockSpec((1, tc, S5_NSTATE), rows),
        out_shape=jax.ShapeDtypeStruct((S5_SG, S5_CH, S5_NSTATE), F32),
        compiler_params=_cparams("arbitrary", "arbitrary"), name="s5_state",
    )(xs, wb)
    sg1 = lambda s: (s, 0, 0)
    xstart = pl.pallas_call(
        _s5_scan_kernel, grid=(S5_SG,),
        in_specs=[pl.BlockSpec((1, S5_CH, S5_NSTATE), sg1),
                  pl.BlockSpec((1, 1, S5_NSTATE // 2), sg1), pl.BlockSpec((1, 1, S5_NSTATE // 2), sg1)],
        out_specs=pl.BlockSpec((1, S5_CH, S5_NSTATE), sg1),
        out_shape=jax.ShapeDtypeStruct((S5_SG, S5_CH, S5_NSTATE), F32),
        compiler_params=_cparams("arbitrary"), name="s5_scan",
    )(e, ltre, ltim)
    y = pl.pallas_call(
        _s5_out_kernel, grid=grid,
        in_specs=[pl.BlockSpec((1, tc, T * LANES), rows), pl.BlockSpec((1, tc, S5_NSTATE), rows),
                  pl.BlockSpec((1, T * LANES, LANES), per_sg), pl.BlockSpec((1, S5_NSTATE, T * LANES), per_sg)],
        out_specs=pl.BlockSpec((1, tc, T * LANES), rows),
        out_shape=jax.ShapeDtypeStruct((S5_SG, S5_CH, T * LANES), F32),
        compiler_params=_cparams("arbitrary", "arbitrary"), name="s5_out",
    )(xs, xstart, krev, wc)
    return y.reshape(S5_SG, S5_CH, T, LANES).transpose(1, 2, 0, 3).reshape(N_TOK, SSM_WIDTH)


MERGE_TM = 256


def _merge_kernel(ocmp_ref, osel_ref, owin_ref, gn_ref, yssm_ref, u_ref, ga_ref, gs_ref, x_ref, mod_ref,
                  eg_ref, dskip_ref, wglu_ref, bglu_ref, wua_ref, wus_ref, wout_ref, g2_ref,
                  wrhi_ref, wrlo_ref, wsgu_ref, wsd_ref,
                  xpart_ref, h2_ref, logit_ref):
    mod = mod_ref[0]
    gnb = gn_ref[...].astype(BF16)
    o_nsa = (_dot(gnb, eg_ref[0]) * ocmp_ref[...].astype(F32)
             + _dot(gnb, eg_ref[1]) * osel_ref[...].astype(F32)
             + _dot(gnb, eg_ref[2]) * owin_ref[...].astype(F32))
    attn = _dot(o_nsa.astype(BF16), wua_ref[...])
    z = _gelu(yssm_ref[...] + dskip_ref[...] * u_ref[...])
    y_ssm = z * jax.nn.sigmoid(_dot(z.astype(BF16), wglu_ref[...]) + bglu_ref[...])
    ssm = _dot(y_ssm.astype(BF16), wus_ref[...])
    merged = ga_ref[...].astype(F32) * attn + gs_ref[...].astype(F32) * ssm
    x1 = x_ref[...] + mod[2:3] * _dot(merged.astype(BF16), wout_ref[...])

    ms = jnp.mean(x1 * x1, axis=-1, keepdims=True)
    h2 = (x1 * lax.rsqrt(ms + EPS) * g2_ref[...]) * (1.0 + mod[4:5]) + mod[3:4]
    hi = h2.astype(BF16)
    lo = (h2 - hi.astype(F32)).astype(BF16)
    h2_ref[...] = _pack_bf16_pairs(h2)
    logit_ref[...] = _dot_nt(wrhi_ref[...], hi) + _dot_nt(wrhi_ref[...], lo) + _dot_nt(wrlo_ref[...], hi)
    gu = _dot(hi, wsgu_ref[...])
    shared = _dot((_silu(gu[:, :D_EXPERT]) * gu[:, D_EXPERT:]).astype(BF16), wsd_ref[...])
    xpart_ref[...] = x1 + mod[5:6] * shared


def _merge(ocmp, osel, owin, gn, yssm, u, ga, gs, x2, mod, d_skip, w_glu, b_glu, w_up_attn, w_up_ssm, w_out,
           g_norm2, w_router, ws_gate, ws_up, ws_down):
    tm = MERGE_TM
    eg = np.zeros((3, LANES, NSA_WIDTH), np.float32)
    for j in range(3):
        for h in range(N_HEADS):
            eg[j, 3 * h + j, h * HEAD_DIM:(h + 1) * HEAD_DIM] = 1.0
    wr_t = w_router.T
    wr_hi = wr_t.astype(BF16)
    wr_lo = (wr_t - wr_hi.astype(F32)).astype(BF16)
    row = lambda i: (i, 0)
    fix2 = lambda i: (0, 0)
    wspec = lambda a: pl.BlockSpec(a.shape, (lambda i: (0,) * a.ndim))
    weights = [jnp.asarray(eg, BF16), d_skip.reshape(1, -1), w_glu.astype(BF16), b_glu.reshape(1, -1),
               w_up_attn.astype(BF16), w_up_ssm.astype(BF16), w_out.astype(BF16), g_norm2.reshape(1, -1),
               wr_hi, wr_lo, jnp.concatenate([ws_gate, ws_up], axis=1).astype(BF16), ws_down.astype(BF16)]
    acts = [(ocmp, 512), (osel, 512), (owin, 512), (gn, 128), (yssm, 512), (u, 512), (ga, 1024), (gs, 1024),
            (x2, 1024)]
    return pl.pallas_call(
        _merge_kernel,
        grid=(N_TOK // tm,),
        in_specs=[pl.BlockSpec((tm, wd), row) for _, wd in acts]
                 + [pl.BlockSpec((1, 6, D_MODEL), lambda i: (i // (SEQ // tm), 0, 0))]
                 + [wspec(w) for w in weights],
        out_specs=[pl.BlockSpec((tm, D_MODEL), row), pl.BlockSpec((tm, HALF), row),
                   pl.BlockSpec((N_EXPERTS, tm), lambda i: (0, i))],
        out_shape=[jax.ShapeDtypeStruct((N_TOK, D_MODEL), F32), jax.ShapeDtypeStruct((N_TOK, HALF), jnp.uint32),
                   jax.ShapeDtypeStruct((N_EXPERTS, N_TOK), F32)],
        compiler_params=_cparams("arbitrary"),
        name="merge",
    )(*[a for a, _ in acts], mod, *weights)


ROUTE_TN = 512


def _route_kernel(logit_ref, bias_ref, eidx_ref, w_ref, count_ref, gscore_ref, masked_ref):
    tn = ROUTE_TN

    @pl.when(pl.program_id(0) == 0)
    def _():
        count_ref[...] = jnp.zeros(count_ref.shape, F32)

    sc = jax.nn.sigmoid(logit_ref[...])
    biased = sc + bias_ref[...]
    gi = lax.broadcasted_iota(I32, (EXPERTS_PER_GROUP, tn), 0).astype(F32)
    for g in range(N_EXPERT_GROUPS):
        blk = biased[g * EXPERTS_PER_GROUP:(g + 1) * EXPERTS_PER_GROUP]
        m1 = jnp.max(blk, axis=0, keepdims=True)
        i1 = jnp.min(jnp.where(blk == m1, gi, float(EXPERTS_PER_GROUP)), axis=0, keepdims=True)
        m2 = jnp.max(jnp.where(gi == i1, -jnp.inf, blk), axis=0, keepdims=True)
        gscore_ref[g:g + 1, :] = m1 + m2
    gs = gscore_ref[...]
    gidx = lax.broadcasted_iota(I32, (N_EXPERT_GROUPS, tn), 0)
    grank = jnp.zeros((N_EXPERT_GROUPS, tn), F32)
    for gp in range(N_EXPERT_GROUPS):
        row = gs[gp:gp + 1, :]
        tie = jnp.where(gidx > gp, 1.0, 0.0)
        grank = grank + jnp.where(row > gs, 1.0, jnp.where(row == gs, tie, 0.0))
    for g in range(N_EXPERT_GROUPS):
        keep = grank[g:g + 1, :] < float(TOPK_GROUPS)
        sl = slice(g * EXPERTS_PER_GROUP, (g + 1) * EXPERTS_PER_GROUP)
        masked_ref[sl, :] = jnp.where(keep, biased[sl], -jnp.inf)
    cur = masked_ref[...]
    eidx = lax.broadcasted_iota(I32, (N_EXPERTS, tn), 0).astype(F32)
    wsum = jnp.zeros((1, tn), F32)
    hits = jnp.zeros((N_EXPERTS, tn), F32)
    for k in range(TOP_K):
        m = jnp.max(cur, axis=0, keepdims=True)
        idx = jnp.min(jnp.where(cur == m, eidx, float(N_EXPERTS)), axis=0, keepdims=True)
        hit = eidx == idx
        wk = jnp.sum(jnp.where(hit, sc, 0.0), axis=0, keepdims=True)
        cur = jnp.where(hit, -jnp.inf, cur)
        hits = hits + jnp.where(hit, 1.0, 0.0)
        eidx_ref[k:k + 1, :] = idx.astype(I32)
        w_ref[k:k + 1, :] = wk
        wsum = wsum + wk
    w_ref[...] = w_ref[...] / wsum * ROUTE_SCALE
    count_ref[...] = count_ref[...] + jnp.sum(hits, axis=1, keepdims=True)


def _route(logits_t, router_bias):
    tn = ROUTE_TN
    return pl.pallas_call(
        _route_kernel,
        grid=(N_TOK // tn,),
        in_specs=[pl.BlockSpec((N_EXPERTS, tn), lambda i: (0, i)), pl.BlockSpec((N_EXPERTS, 1), lambda i: (0, 0))],
        out_specs=[pl.BlockSpec((TOP_K, tn), lambda i: (0, i))] * 2 + [pl.BlockSpec((N_EXPERTS, 1), lambda i: (0, 0))],
        out_shape=[jax.ShapeDtypeStruct((TOP_K, N_TOK), I32), jax.ShapeDtypeStruct((TOP_K, N_TOK), F32),
                   jax.ShapeDtypeStruct((N_EXPERTS, 1), F32)],
        scratch_shapes=[pltpu.VMEM((N_EXPERT_GROUPS, tn), F32), pltpu.VMEM((N_EXPERTS, tn), F32)],
        compiler_params=_cparams("arbitrary"),
        name="route",
    )(logits_t, router_bias.reshape(-1, 1))


N_MOE_BLK = NK // DISPATCH_BLOCK
N_ITEMS = N_MOE_BLK + N_EXPERTS
ASSIGN_BITS = 17


def _dispatch_plan(eidx, counts):
    e_flat = eidx.reshape(-1)
    key = jnp.sort(e_flat * NK + jnp.arange(NK, dtype=I32))
    order = key & (NK - 1)
    counts = counts.reshape(-1).astype(I32)
    start = jnp.cumsum(counts) - counts
    cuts = jnp.sort(jnp.concatenate([jnp.arange(N_MOE_BLK, dtype=I32) * DISPATCH_BLOCK, start]))
    lo = cuts
    hi = jnp.concatenate([cuts[1:], jnp.full((1,), NK, I32)])
    blk = jnp.minimum(lo // DISPATCH_BLOCK, N_MOE_BLK - 1)
    expert = jnp.clip(jnp.sum((start[None, :] <= lo[:, None]).astype(I32), axis=1) - 1, 0, N_EXPERTS - 1)
    one = jnp.ones((1,), I32)
    first = jnp.concatenate([one, (blk[1:] != blk[:-1]).astype(I32)])
    last = jnp.concatenate([(blk[1:] != blk[:-1]).astype(I32), one])
    new_expert = jnp.concatenate([one, (expert[1:] != expert[:-1]).astype(I32)])
    run_id = jnp.cumsum(new_expert) - 1
    n_runs = run_id[-1] + 1
    item = jnp.arange(N_ITEMS, dtype=I32)
    run_first_item = jnp.sort(jnp.where(new_expert == 1, item, N_ITEMS))
    run_expert = expert[jnp.minimum(run_first_item, N_ITEMS - 1)]
    ahead = run_id + (WEIGHT_RING - 1)
    ahead_expert = run_expert[jnp.minimum(ahead, N_ITEMS - 1)]
    ahead_valid = (ahead < n_runs).astype(I32)
    second_expert = run_expert[1:2]
    prologue = jnp.concatenate([second_expert, (n_runs > 1).astype(I32).reshape(1)])
    _, pos = lax.sort_key_val(order, jnp.arange(NK, dtype=I32))
    tok = jnp.right_shift(order, 3)
    return tok, pos, (blk, expert, lo - blk * DISPATCH_BLOCK, hi - blk * DISPATCH_BLOCK, first, last, new_expert,
                      run_id % WEIGHT_RING, ahead_expert, ahead_valid, prologue)


SC_CORES = 2
SC_SUBCORES = 16
SC_CHUNK = 128


def _sc_gather_rows(table, idx):
    n = idx.shape[0]
    workers = SC_CORES * SC_SUBCORES
    per_worker = n // workers
    n_chunks = per_worker // SC_CHUNK
    assert per_worker * workers == n and n_chunks * SC_CHUNK == per_worker
    mesh = plsc.VectorSubcoreMesh(core_axis_name="c", subcore_axis_name="s",
                                  num_cores=SC_CORES, num_subcores=SC_SUBCORES)

    def body(table_hbm, idx_hbm, out_hbm, idx_v, rows_v, sem):
        wid = lax.axis_index("s") * SC_CORES + lax.axis_index("c")
        base = wid * per_worker

        @pl.loop(0, n_chunks)
        def _(j):
            off = base + j * SC_CHUNK
            pltpu.sync_copy(idx_hbm.at[pl.ds(off, SC_CHUNK)], idx_v)
            pltpu.async_copy(table_hbm.at[idx_v], rows_v, sem).wait()
            pltpu.sync_copy(rows_v, out_hbm.at[pl.ds(off, SC_CHUNK)])

    return pl.kernel(
        body,
        out_type=jax.ShapeDtypeStruct((n, table.shape[1]), table.dtype),
        mesh=mesh,
        scratch_types=[pltpu.VMEM((SC_CHUNK,), I32), pltpu.VMEM((SC_CHUNK, table.shape[1]), table.dtype),
                       pltpu.SemaphoreType.DMA],
        name="sc_gather_rows",
    )(table, idx)


WEIGHT_RING = 3
WEIGHT_CHUNKS = 4


def _expert_weight_copies(w_hbm, wbuf, sem, expert, slot):
    rows = w_hbm.shape[1] // WEIGHT_CHUNKS
    return [pltpu.make_async_copy(w_hbm.at[expert, pl.ds(c * rows, rows)],
                                  wbuf.at[slot, pl.ds(c * rows, rows)], sem.at[slot])
            for c in range(WEIGHT_CHUNKS)]


def _moe_kernel(blk_ref, exp_ref, lo_ref, hi_ref, first_ref, last_ref, newexp_ref,
                slot_ref, ahead_exp_ref, ahead_ok_ref, prologue_ref,
                x_ref, wg_hbm, wu_hbm, wd_hbm, y_ref,
                acc_ref, wgf_ref, wuf_ref, wdf_ref, wgb_ref, wub_ref, wdb_ref, wsem):
    it = pl.program_id(0)
    lo, hi = lo_ref[it], hi_ref[it]
    streams = ((wg_hbm, wgf_ref), (wu_hbm, wuf_ref), (wd_hbm, wdf_ref))

    def request(expert, slot):
        for w_hbm, wbuf in streams:
            for cp in _expert_weight_copies(w_hbm, wbuf, wsem, expert, slot):
                cp.start()

    @pl.when(it == 0)
    def _():
        request(exp_ref[0], 0)

        @pl.when(prologue_ref[1] == 1)
        def _():
            request(prologue_ref[0], 1)

    @pl.when(newexp_ref[it] == 1)
    def _():
        slot = slot_ref[it]
        for w_hbm, wbuf in streams:
            for cp in _expert_weight_copies(w_hbm, wbuf, wsem, 0, slot):
                cp.wait()
        wgb_ref[...] = wgf_ref[slot].astype(BF16)
        wub_ref[...] = wuf_ref[slot].astype(BF16)
        wdb_ref[...] = wdf_ref[slot].astype(BF16)

        @pl.when(ahead_ok_ref[it] == 1)
        def _():
            ahead_slot = slot + (WEIGHT_RING - 1)
            request(ahead_exp_ref[it], jnp.where(ahead_slot >= WEIGHT_RING, ahead_slot - WEIGHT_RING, ahead_slot))

    @pl.when(first_ref[it] == 1)
    def _():
        acc_ref[...] = jnp.zeros(acc_ref.shape, F32)

    @pl.when(hi > lo)
    def _():
        ridx = lax.broadcasted_iota(I32, x_ref.shape, 0)
        mine = (ridx >= lo) & (ridx < hi)
        xlo, xhi = _unpack_bf16_pairs(jnp.where(mine, x_ref[...], jnp.uint32(0)))
        xlo, xhi = xlo.astype(BF16), xhi.astype(BF16)
        gate = _dot(xlo, wgb_ref[:HALF]) + _dot(xhi, wgb_ref[HALF:])
        up = _dot(xlo, wub_ref[:HALF]) + _dot(xhi, wub_ref[HALF:])
        acc_ref[...] = acc_ref[...] + _dot((_silu(gate) * up).astype(BF16), wdb_ref[...])

    @pl.when(last_ref[it] == 1)
    def _():
        y_ref[...] = _pack_bf16_pairs(acc_ref[...])


def _moe(xs, items, w_gate, w_up, w_down):
    by_blk = lambda it, blk, *_: (blk[it], 0)
    any_space = pl.BlockSpec(memory_space=pl.ANY)
    grid_spec = pltpu.PrefetchScalarGridSpec(
        num_scalar_prefetch=len(items),
        grid=(N_ITEMS,),
        in_specs=[pl.BlockSpec((DISPATCH_BLOCK, HALF), by_blk), any_space, any_space, any_space],
        out_specs=pl.BlockSpec((DISPATCH_BLOCK, HALF), by_blk),
        scratch_shapes=[pltpu.VMEM((DISPATCH_BLOCK, D_MODEL), F32),
                        pltpu.VMEM((WEIGHT_RING, D_MODEL, D_EXPERT), F32),
                        pltpu.VMEM((WEIGHT_RING, D_MODEL, D_EXPERT), F32),
                        pltpu.VMEM((WEIGHT_RING, D_EXPERT, D_MODEL), F32),
                        pltpu.VMEM((D_MODEL, D_EXPERT), BF16), pltpu.VMEM((D_MODEL, D_EXPERT), BF16),
                        pltpu.VMEM((D_EXPERT, D_MODEL), BF16),
                        pltpu.SemaphoreType.DMA((WEIGHT_RING,))],
    )
    return pl.pallas_call(
        _moe_kernel,
        grid_spec=grid_spec,
        out_shape=jax.ShapeDtypeStruct((NK, HALF), jnp.uint32),
        compiler_params=_cparams("arbitrary"),
        name="moe",
    )(*items, xs, w_gate, w_up, w_down)


COMB_TC = 64


def _combine_kernel(slots_ref, w_ref, xpart_ref, mod_ref, out_ref):
    w = w_ref[...]
    lo = jnp.zeros((w.shape[0], HALF), F32)
    hi = jnp.zeros((w.shape[0], HALF), F32)
    for k in range(TOP_K):
        klo, khi = _unpack_bf16_pairs(slots_ref[k])
        lo = lo + w[:, k:k + 1] * klo
        hi = hi + w[:, k:k + 1] * khi
    gate2 = mod_ref[0][5:6]
    out_ref[:, :HALF] = xpart_ref[:, :HALF] + gate2[:, :HALF] * lo
    out_ref[:, HALF:] = xpart_ref[:, HALF:] + gate2[:, HALF:] * hi


def _combine(xpart, mod, slots, w):
    tc = COMB_TC
    row = lambda i: (i, 0)
    return pl.pallas_call(
        _combine_kernel,
        grid=(N_TOK // tc,),
        in_specs=[pl.BlockSpec((TOP_K, tc, HALF), lambda i: (0, i, 0)),
                  pl.BlockSpec((tc, TOP_K), row),
                  pl.BlockSpec((tc, D_MODEL), row),
                  pl.BlockSpec((1, 6, D_MODEL), lambda i: (i // (SEQ // tc), 0, 0))],
        out_specs=pl.BlockSpec((tc, D_MODEL), row),
        out_shape=jax.ShapeDtypeStruct((N_TOK, D_MODEL), F32),
        compiler_params=_cparams("arbitrary"),
        name="combine",
    )(slots.reshape(TOP_K, N_TOK, HALF), w, xpart, mod)


def _layer(x, c, w_ada, b_ada, g_norm1, g_norm2, w_in, q_gain, kc_gain, ks_gain, kw_gain,
           pe_k, pe_v, w_cmp_k1, w_cmp_k2, w_cmp_v1, w_cmp_v2,
           a_re, a_im, log_dt, b_re, b_im, c_re, c_im, d_skip, w_glu, b_glu,
           w_up_attn, w_up_ssm, w_out, w_router, router_bias,
           w_gate, w_up, w_down, ws_gate, ws_up, ws_down):
    x2 = x.reshape(N_TOK, D_MODEL)
    mod = _ada(c, w_ada, b_ada)
    q, kc_raw, vc_raw, ks, kw, vst, vwt, gn, u, ga, gs = _proj(x2, mod, g_norm1, w_in, q_gain, ks_gain, kw_gain)
    kcn = _compress(kc_raw, pe_k, w_cmp_k1, w_cmp_k2, kc_gain, True)
    vcn = _compress(vc_raw, pe_v, w_cmp_v1, w_cmp_v2, kc_gain, False)
    ocmp, selb = _cmp_attn(q, kcn, vcn)
    osel, owin = _selwin(q, ks, kw, vst, vwt, selb)
    yssm = _s5(u, *_s5_params(a_re, a_im, log_dt, b_re, b_im, c_re, c_im))
    xpart, h2, logits_t = _merge(ocmp, osel, owin, gn, yssm, u, ga, gs, x2, mod, d_skip, w_glu, b_glu,
                                  w_up_attn, w_up_ssm, w_out, g_norm2, w_router, ws_gate, ws_up, ws_down)
    eidx_t, w_t, counts = _route(logits_t, router_bias)
    tok, pos, items = _dispatch_plan(eidx_t.T, counts)
    y = _moe(_sc_gather_rows(h2, tok), items, w_gate, w_up, w_down)
    slots = _sc_gather_rows(y, pos.reshape(N_TOK, TOP_K).T.reshape(-1))
    return _combine(xpart, mod, slots, w_t.T).reshape(BATCH, SEQ, D_MODEL)


def kernel(x, c, w_ada, b_ada, g_norm1, g_norm2, w_in, q_gain, kc_gain, ks_gain, kw_gain, pe_k, pe_v, w_cmp_k1,
           w_cmp_k2, w_cmp_v1, w_cmp_v2, a_re, a_im, log_dt, b_re, b_im, c_re, c_im, d_skip, w_glu, b_glu,
           w_up_attn, w_up_ssm, w_out, w_router, router_bias, w_gate, w_up, w_down, ws_gate, ws_up, ws_down):
    params = (w_ada, b_ada, g_norm1, g_norm2, w_in, q_gain, kc_gain, ks_gain, kw_gain, pe_k, pe_v, w_cmp_k1,
              w_cmp_k2, w_cmp_v1, w_cmp_v2, a_re, a_im, log_dt, b_re, b_im, c_re, c_im, d_skip, w_glu, b_glu,
              w_up_attn, w_up_ssm, w_out, w_router, router_bias, w_gate, w_up, w_down, ws_gate, ws_up, ws_down)
    depth = w_ada.shape[0]
    for layer in range(depth):
        x = _layer(x, c, *[p[layer] for p in params])
    return x
```

```python
import functools
import math

import jax
import jax.numpy as jnp
import numpy as np
from jax import lax
from jax.experimental import pallas as pl
from jax.experimental.pallas import tpu as pltpu
from jax.experimental.pallas import tpu_sc as plsc

F32 = jnp.float32
BF16 = jnp.bfloat16
I32 = jnp.int32
HIGHEST = lax.Precision.HIGHEST

D_MODEL = 1024
BATCH = 4
SEQ = 4096
N_TOK = BATCH * SEQ
N_HEADS = 8
HEAD_DIM = 64
N_KV = 2
CMP_BLOCK = 32
CMP_STRIDE = 16
CMP_HIDDEN = 256
N_CMP = 256
SEL_BLOCK = 64
N_SEL_BLOCKS = SEQ // SEL_BLOCK
N_SELECT = 16
WINDOW = 512
ATTN_SCALE = HEAD_DIM ** -0.5
LOG2E = 1.4426950408889634
NSA_WIDTH = N_HEADS * HEAD_DIM
SSM_WIDTH = 512
GROUP = 16
N_GROUPS = SSM_WIDTH // GROUP
STATE = 64
N_EXPERTS = 256
TOP_K = 8
D_EXPERT = 256
N_EXPERT_GROUPS = 8
EXPERTS_PER_GROUP = N_EXPERTS // N_EXPERT_GROUPS
TOPK_GROUPS = 4
ROUTE_SCALE = 2.5
DISPATCH_BLOCK = 256
EPS = 1e-6
NEG = -1e30

LANES = 128
S5_T = 16
S5_SG = 4
S5_GL = N_GROUPS // S5_SG
S5_CH = N_TOK // S5_T
S5_CH_PER_BATCH = SEQ // S5_T
S5_NSTATE = S5_GL * STATE * 2

NK = N_TOK * TOP_K
HALF = D_MODEL // 2

VMEM_LIMIT = 48 * 1024 * 1024


def _cparams(*sem):
    return pltpu.CompilerParams(dimension_semantics=tuple(sem), vmem_limit_bytes=VMEM_LIMIT)


def _dot(a, b):
    return jnp.dot(a, b, preferred_element_type=F32)


def _dot_nt(a, b):
    return lax.dot_general(a, b, (((1,), (1,)), ((), ())), preferred_element_type=F32)


def _split_dot(v, w):
    hi = v.astype(BF16)
    lo = (v - hi.astype(F32)).astype(BF16)
    return _dot(hi, w) + _dot(lo, w)


def _seg_rms(v, bd, gain):
    ss = _split_dot(v * v, bd)
    return v * lax.rsqrt(ss * (1.0 / HEAD_DIM) + EPS) * gain


def _gelu(x):
    return 0.5 * x * (1.0 + jnp.tanh(0.7978845608028654 * (x + 0.044715 * (x * x * x))))


def _silu(x):
    return x * jax.nn.sigmoid(x)


def _pack_bf16_pairs(v):
    bits = lax.bitcast_convert_type(v.astype(BF16).astype(F32), jnp.uint32)
    h = v.shape[1] // 2
    return bits[:, h:] | lax.shift_right_logical(bits[:, :h], jnp.uint32(16))


def _unpack_bf16_pairs(word):
    lo = lax.bitcast_convert_type(lax.shift_left(word, jnp.uint32(16)), F32)
    hi = lax.bitcast_convert_type(word & jnp.uint32(0xFFFF0000), F32)
    return lo, hi


def _ada_kernel(c_ref, w_ref, b_ref, o_ref):
    c = c_ref[...]
    o_ref[...] = jnp.dot(_silu(c), w_ref[...], preferred_element_type=F32, precision=HIGHEST) + b_ref[...]


def _ada(c, w_ada, b_ada):
    cp = jnp.pad(c, ((0, 8 - BATCH), (0, 0)))
    tn = 1536
    out = pl.pallas_call(
        _ada_kernel,
        grid=(6 * D_MODEL // tn,),
        in_specs=[pl.BlockSpec((8, D_MODEL), lambda j: (0, 0)),
                  pl.BlockSpec((D_MODEL, tn), lambda j: (0, j)),
                  pl.BlockSpec((1, tn), lambda j: (0, j))],
        out_specs=pl.BlockSpec((8, tn), lambda j: (0, j)),
        out_shape=jax.ShapeDtypeStruct((8, 6 * D_MODEL), F32),
        compiler_params=_cparams("arbitrary"),
        name="ada",
    )(cp, w_ada, b_ada.reshape(1, -1))
    return out.reshape(8, 6, D_MODEL)


_C_Q = 0
_C_KC = 512
_C_VC = 640
_C_KS = 768
_C_KW = 1024
_C_GN = 1280
_C_U = 1408
_C_GA = 1920
_C_GS = 2944
_C_END = 3968
PROJ_TM = 512


def _proj_kernel(x_ref, mod_ref, g1_ref, w_ref, wvt_ref, qg_ref, ksg_ref, kwg_ref, bd512_ref, bd256_ref,
                 q_ref, kc_ref, vc_ref, ks_ref, kw_ref, vst_ref, vwt_ref, gn_ref, u_ref, ga_ref, gs_ref):
    x = x_ref[...]
    ms = jnp.mean(x * x, axis=-1, keepdims=True)
    mod = mod_ref[0]
    h = (x * lax.rsqrt(ms + EPS) * g1_ref[...]) * (1.0 + mod[1:2]) + mod[0:1]
    hb = h.astype(BF16)

    def p(lo, hi):
        return _dot(hb, w_ref[:, lo:hi])

    q_ref[...] = _seg_rms(p(_C_Q, _C_KC), bd512_ref[...], qg_ref[...] * (ATTN_SCALE * LOG2E)).astype(BF16)
    kc_ref[...] = p(_C_KC, _C_VC).astype(BF16)
    vc_ref[...] = p(_C_VC, _C_KS).astype(BF16)
    ks_ref[...] = _seg_rms(p(_C_KS, _C_KW), bd256_ref[...], ksg_ref[...]).astype(BF16)
    kw_ref[...] = _seg_rms(p(_C_KW, _C_GN), bd256_ref[...], kwg_ref[...]).astype(BF16)
    vt = _dot_nt(wvt_ref[...], hb)
    vst_ref[...] = vt[:LANES].astype(BF16)
    vwt_ref[...] = vt[LANES:].astype(BF16)
    gn_ref[...] = jax.nn.sigmoid(p(_C_GN, _C_U))
    u_ref[...] = p(_C_U, _C_GA)
    ga_ref[...] = jax.nn.sigmoid(p(_C_GA, _C_GS)).astype(BF16)
    gs_ref[...] = jax.nn.sigmoid(p(_C_GS, _C_END)).astype(BF16)


def _dup_cols(w):
    return jnp.concatenate([w[:, :64], w[:, :64], w[:, 64:], w[:, 64:]], axis=1)


def _block_ones(n):
    return jnp.kron(jnp.eye(n // HEAD_DIM, dtype=F32), jnp.ones((HEAD_DIM, HEAD_DIM), F32)).astype(BF16)


def _proj(x2, mod, g_norm1, w_in, q_gain, ks_gain, kw_gain):
    o = np.cumsum((0, 512, 128, 128, 128, 128, 128, 128, 24, 512, 1024, 1024))
    parts = [w_in[:, o[i]:o[i + 1]] for i in range(11)]
    wq, wkc, wvc, wks, wvs, wkw, wvw, wgn, wu, wga, wgs = parts
    w = jnp.concatenate([wq, wkc, wvc, _dup_cols(wks), _dup_cols(wkw),
                         jnp.pad(wgn, ((0, 0), (0, LANES - 24))), wu, wga, wgs], axis=1).astype(BF16)
    wvt = jnp.concatenate([wvs, wvw], axis=1).T.astype(BF16)
    tm = PROJ_TM
    row = lambda i: (i, 0)
    col = lambda i: (0, i)
    fix = lambda i: (0, 0)
    outs = [(512, BF16, row), (128, BF16, row), (128, BF16, row), (256, BF16, row), (256, BF16, row),
            (LANES, BF16, col), (LANES, BF16, col),
            (128, F32, row), (512, F32, row), (1024, BF16, row), (1024, BF16, row)]
    ospec = lambda wd, m: pl.BlockSpec((tm, wd), m) if m is row else pl.BlockSpec((wd, tm), m)
    oshape = lambda wd, dt, m: jax.ShapeDtypeStruct((N_TOK, wd) if m is row else (wd, N_TOK), dt)
    return pl.pallas_call(
        _proj_kernel,
        grid=(N_TOK // tm,),
        in_specs=[pl.BlockSpec((tm, D_MODEL), row),
                  pl.BlockSpec((1, 6, D_MODEL), lambda i: (i // (SEQ // tm), 0, 0)),
                  pl.BlockSpec((1, D_MODEL), fix),
                  pl.BlockSpec((D_MODEL, _C_END), fix),
                  pl.BlockSpec((2 * LANES, D_MODEL), fix),
                  pl.BlockSpec((1, 512), fix), pl.BlockSpec((1, 256), fix), pl.BlockSpec((1, 256), fix),
                  pl.BlockSpec((512, 512), fix), pl.BlockSpec((256, 256), fix)],
        out_specs=[ospec(wd, m) for wd, _, m in outs],
        out_shape=[oshape(wd, dt, m) for wd, dt, m in outs],
        compiler_params=_cparams("arbitrary"),
        name="proj",
    )(x2, mod, g_norm1.reshape(1, -1), w, wvt,
      jnp.tile(q_gain, N_HEADS).reshape(1, -1), jnp.tile(ks_gain, 4).reshape(1, -1),
      jnp.tile(kw_gain, 4).reshape(1, -1), _block_ones(512), _block_ones(256))


def _compress_kernel(r_ref, pe_ref, w1_ref, w2_ref, bd_ref, gain_ref, o_ref, *, do_norm):
    r = r_ref[0].astype(F32)
    p0 = _dot((r + pe_ref[0]).astype(BF16), w1_ref[0])
    p1 = _dot((r + pe_ref[1]).astype(BF16), w1_ref[1])
    hid = p0 + pltpu.roll(p1, N_CMP - 1, 0)
    c = _dot(_gelu(hid).astype(BF16), w2_ref[...])
    if do_norm:
        c = _seg_rms(c, bd_ref[...], gain_ref[...])
    o_ref[0] = c.astype(BF16)


def _compress(raw, pe, w1, w2, gain, do_norm):
    r = raw.reshape(BATCH, SEQ // CMP_STRIDE, CMP_STRIDE * LANES)
    eye = jnp.eye(N_KV, dtype=F32)
    w1r = w1.reshape(2, CMP_STRIDE, HEAD_DIM, CMP_HIDDEN)
    w1big = jnp.einsum('hldc,gk->hlgdkc', w1r, eye).reshape(2, CMP_STRIDE * LANES, N_KV * CMP_HIDDEN).astype(BF16)
    w2big = jnp.einsum('cd,gk->gckd', w2, eye)
    w2big = jnp.concatenate([w2big, w2big], axis=-1).reshape(N_KV * CMP_HIDDEN, 4 * HEAD_DIM).astype(BF16)
    pe_big = jnp.broadcast_to(pe.reshape(2, CMP_STRIDE, 1, HEAD_DIM), (2, CMP_STRIDE, N_KV, HEAD_DIM))
    pe_big = pe_big.reshape(2, 1, CMP_STRIDE * LANES)
    fix2 = lambda b: (0, 0)
    fix3 = lambda b: (0, 0, 0)
    return pl.pallas_call(
        functools.partial(_compress_kernel, do_norm=do_norm),
        grid=(BATCH,),
        in_specs=[pl.BlockSpec((1, N_CMP, CMP_STRIDE * LANES), lambda b: (b, 0, 0)),
                  pl.BlockSpec((2, 1, CMP_STRIDE * LANES), fix3),
                  pl.BlockSpec((2, CMP_STRIDE * LANES, N_KV * CMP_HIDDEN), fix3),
                  pl.BlockSpec((N_KV * CMP_HIDDEN, 256), fix2),
                  pl.BlockSpec((256, 256), fix2), pl.BlockSpec((1, 256), fix2)],
        out_specs=pl.BlockSpec((1, N_CMP, 256), lambda b: (b, 0, 0)),
        out_shape=jax.ShapeDtypeStruct((BATCH, N_CMP, 256), BF16),
        compiler_params=_cparams("arbitrary"),
        name="compress_k" if do_norm else "compress_v",
    )(r, pe_big, w1big, w2big, _block_ones(256), jnp.tile(gain, 4).reshape(1, -1))


ATT_TQ = 256


def _head_variants(qb):
    lane = lax.broadcasted_iota(I32, qb.shape, 1)
    z = jnp.zeros_like(qb)
    return jnp.where(lane < HEAD_DIM, qb, z), jnp.where(lane < HEAD_DIM, z, qb)


def _cmp_kernel(q_ref, kc_ref, vc_ref, ov_ref, o_ref, sel_ref, vrank_ref):
    tq = ATT_TQ
    qi = pl.program_id(1)
    tpos = qi * tq + lax.broadcasted_iota(I32, (tq, N_CMP), 0)
    nidx = lax.broadcasted_iota(I32, (tq, N_CMP), 1)
    mask = (CMP_STRIDE * nidx + (CMP_BLOCK - 1)) <= tpos
    lane_lo = lax.broadcasted_iota(I32, (tq, LANES), 1) < HEAD_DIM
    for g in range(N_KV):
        kd = kc_ref[0, :, g * LANES:(g + 1) * LANES]
        vd = vc_ref[0, :, g * LANES:(g + 1) * LANES]
        psum = jnp.zeros((tq, N_CMP), F32)
        for jb in range(2):
            blk = 2 * g + jb
            pv = []
            for qv in _head_variants(q_ref[:, blk * LANES:(blk + 1) * LANES]):
                s = jnp.where(mask, _dot_nt(qv, kd), NEG)
                m = jnp.max(s, axis=-1, keepdims=True)
                e = jnp.where(mask, jnp.exp2(s - m), 0.0)
                l = jnp.sum(e, axis=-1, keepdims=True)
                p = e / jnp.where(l > 0.0, l, 1.0)
                psum = psum + p
                pv.append(_dot(p.astype(BF16), vd))
            o_ref[:, blk * LANES:(blk + 1) * LANES] = jnp.where(lane_lo, pv[0], pv[1]).astype(BF16)
        imp = _split_dot(psum, ov_ref[...])
        imp_t = imp.T[:N_SEL_BLOCKS]
        j = lax.broadcasted_iota(I32, (N_SEL_BLOCKS, tq), 0)
        cur = jnp.right_shift(qi * tq + lax.broadcasted_iota(I32, (N_SEL_BLOCKS, tq), 1), 6)
        forced = (j == 0) | (j == cur) | (j == cur - 1)
        v = jnp.where(forced, jnp.inf, jnp.where(j <= cur, imp_t, -jnp.inf))
        vrank_ref[...] = v

        def count_above(jp, rank):
            row = vrank_ref[pl.ds(jp, 1), :]
            tie = jnp.where(j > jp, 1.0, 0.0)
            return rank + jnp.where(row > v, 1.0, jnp.where(row == v, tie, 0.0))

        n_live = (qi + 1) * (tq // SEL_BLOCK)
        rank = lax.fori_loop(0, n_live, count_above, jnp.zeros((N_SEL_BLOCKS, tq), F32))
        sel_ref[g * N_SEL_BLOCKS:(g + 1) * N_SEL_BLOCKS, :] = jnp.where(rank < float(N_SELECT), 0.0, NEG)


def _cmp_attn(q, kcn, vcn):
    nc = np.arange(N_CMP)
    sb = np.arange(LANES)
    ov = ((CMP_STRIDE * nc[:, None] < SEL_BLOCK * sb[None, :] + SEL_BLOCK)
          & (CMP_STRIDE * nc[:, None] + CMP_BLOCK > SEL_BLOCK * sb[None, :])
          & (nc[:, None] < N_CMP - 1) & (sb[None, :] < N_SEL_BLOCKS))
    ov = jnp.asarray(ov, BF16)
    tq = ATT_TQ
    nq = SEQ // tq
    row = lambda b, i: (b * nq + i, 0)
    return pl.pallas_call(
        _cmp_kernel,
        grid=(BATCH, nq),
        in_specs=[pl.BlockSpec((tq, NSA_WIDTH), row),
                  pl.BlockSpec((1, N_CMP, 256), lambda b, i: (b, 0, 0)),
                  pl.BlockSpec((1, N_CMP, 256), lambda b, i: (b, 0, 0)),
                  pl.BlockSpec((N_CMP, LANES), lambda b, i: (0, 0))],
        out_specs=[pl.BlockSpec((tq, NSA_WIDTH), row),
                   pl.BlockSpec((N_KV * N_SEL_BLOCKS, tq), lambda b, i: (0, b * nq + i))],
        out_shape=[jax.ShapeDtypeStruct((N_TOK, NSA_WIDTH), BF16),
                   jax.ShapeDtypeStruct((N_KV * N_SEL_BLOCKS, N_TOK), F32)],
        scratch_shapes=[pltpu.VMEM((N_SEL_BLOCKS, tq), F32)],
        compiler_params=_cparams("arbitrary", "arbitrary"),
        name="cmp_attn",
    )(q, kcn, vcn, ov)


ATT_TK = 256


M_INIT = -1e29


SUM_ROWS = 16


def _selwin_kernel(q_ref, ks_ref, kw_ref, vst_ref, vwt_ref, selb_ref, osel_ref, owin_ref, m_ref, acc_ref):
    tq, tk = ATT_TQ, ATT_TK
    qi = pl.program_id(1)
    krow = lax.broadcasted_iota(I32, (tk, tq), 0)
    qcol = lax.broadcasted_iota(I32, (tk, tq), 1)
    causal_bias = jnp.where(krow <= qcol, 0.0, NEG)
    far_bias = jnp.where(qcol < krow, 0.0, NEG)

    ones_rows = jnp.ones((SUM_ROWS, tk), BF16)

    def reset():
        m_ref[...] = jnp.full(m_ref.shape, M_INIT, F32)
        acc_ref[...] = jnp.zeros(acc_ref.shape, F32)

    def update(qvars, kd, vt, bias):
        s = _dot_nt(kd, qvars)
        if bias is not None:
            s = s + jnp.concatenate([bias] * 4, axis=1)
        m_old = m_ref[...]
        m_new = jnp.maximum(m_old, jnp.max(s, axis=0, keepdims=True))
        alpha = jnp.exp2(m_old - m_new)
        p = jnp.exp2(s - m_new)
        m_ref[...] = m_new
        vte = jnp.concatenate([vt, ones_rows], axis=0)
        acc_ref[...] = alpha * acc_ref[...] + _dot(vte, p.astype(BF16))

    def finish(out_ref, g):
        o = acc_ref[:HEAD_DIM, :] / acc_ref[HEAD_DIM:HEAD_DIM + 1, :]
        for jb in range(2):
            blk = 2 * g + jb
            pair = jnp.concatenate([o[:, 2 * jb * tq:(2 * jb + 1) * tq], o[:, (2 * jb + 1) * tq:(2 * jb + 2) * tq]],
                                   axis=0)
            out_ref[:, blk * LANES:(blk + 1) * LANES] = pair.T.astype(BF16)

    def sel_bias(g, kt):
        rows = [jnp.broadcast_to(selb_ref[pl.ds(g * N_SEL_BLOCKS + kt * (tk // SEL_BLOCK) + r, 1), :],
                                 (SEL_BLOCK, tq)) for r in range(tk // SEL_BLOCK)]
        return jnp.concatenate(rows, axis=0)

    for g in range(N_KV):
        gl = slice(g * LANES, (g + 1) * LANES)
        gv = slice(g * HEAD_DIM, (g + 1) * HEAD_DIM)
        qvars = []
        for jb in range(2):
            qvars.extend(_head_variants(q_ref[:, (2 * g + jb) * LANES:(2 * g + jb + 1) * LANES]))
        qvars = jnp.concatenate(qvars, axis=0)

        def k_tile(ref, kt):
            return ref[0, pl.ds(pl.multiple_of(kt * tk, tk), tk), gl]

        def v_tile(ref, kt):
            return ref[gv, pl.ds(pl.multiple_of(kt * tk, tk), tk)]

        reset()

        def sel_step(kt, carry):
            update(qvars, k_tile(ks_ref, kt), v_tile(vst_ref, kt), sel_bias(g, kt))
            return carry

        lax.fori_loop(0, qi, sel_step, 0)
        update(qvars, k_tile(ks_ref, qi), v_tile(vst_ref, qi), sel_bias(g, qi) + causal_bias)
        finish(osel_ref, g)

        reset()

        @pl.when(qi >= 2)
        def _():
            update(qvars, k_tile(kw_ref, qi - 2), v_tile(vwt_ref, qi - 2), far_bias)

        @pl.when(qi >= 1)
        def _():
            update(qvars, k_tile(kw_ref, qi - 1), v_tile(vwt_ref, qi - 1), None)

        update(qvars, k_tile(kw_ref, qi), v_tile(vwt_ref, qi), causal_bias)
        finish(owin_ref, g)


def _selwin(q, ks, kw, vst, vwt, selb):
    tq = ATT_TQ
    nq = SEQ // tq
    assert WINDOW == 2 * ATT_TK and ATT_TQ == ATT_TK
    row = lambda b, i: (b * nq + i, 0)
    keys = pl.BlockSpec((1, SEQ, 256), lambda b, i: (b, 0, 0))
    vals = pl.BlockSpec((LANES, SEQ), lambda b, i: (0, b))
    r3 = lambda a: a.reshape(BATCH, SEQ, 256)
    return pl.pallas_call(
        _selwin_kernel,
        grid=(BATCH, nq),
        in_specs=[pl.BlockSpec((tq, NSA_WIDTH), row), keys, keys, vals, vals,
                  pl.BlockSpec((N_KV * N_SEL_BLOCKS, tq), lambda b, i: (0, b * nq + i))],
        out_specs=[pl.BlockSpec((tq, NSA_WIDTH), row)] * 2,
        out_shape=[jax.ShapeDtypeStruct((N_TOK, NSA_WIDTH), BF16)] * 2,
        scratch_shapes=[pltpu.VMEM((1, 4 * tq), F32), pltpu.VMEM((HEAD_DIM + SUM_ROWS, 4 * tq), F32)],
        compiler_params=_cparams("arbitrary", "arbitrary"),
        name="selwin",
    )(q, r3(ks), r3(kw), vst, vwt, selb)


def _s5_param_kernel(are_ref, aim_ref, ldt_ref, cre_ref, cim_ref, bre_ref, bim_ref,
                     clre_ref, clim_ref, wbre_ref, wbim_ref, bbre_ref, bbim_ref, ltre_ref, ltim_ref):
    are, aim = are_ref[...], aim_ref[...]
    dt = jnp.exp(ldt_ref[...])
    cre, cim = cre_ref[...], cim_ref[...]

    def lam_pow(tau):
        mag = jnp.exp(are * dt * float(tau))
        ang = aim * dt * float(tau)
        return mag * jnp.cos(ang), mag * jnp.sin(ang)

    lre, lim = lam_pow(1)
    den = are * are + aim * aim
    qre = ((lre - 1.0) * are + lim * aim) / den
    qim = (lim * are - (lre - 1.0) * aim) / den
    bre, bim = bre_ref[...], bim_ref[...]
    bbre = qre * bre - qim * bim
    bbim = qre * bim + qim * bre
    bbre_ref[...] = bbre
    bbim_ref[...] = bbim
    for tau in range(S5_T + 1):
        pr, pi = lam_pow(tau)
        clre_ref[tau] = cre * pr - cim * pi
        clim_ref[tau] = cre * pi + cim * pr
        if tau < S5_T:
            k = S5_T - 1 - tau
            wbre_ref[k] = pr * bbre - pi * bbim
            wbim_ref[k] = pr * bbim + pi * bbre
        else:
            ltre_ref[...] = pr
            ltim_ref[...] = pi


def _s5_kmat_kernel(l_ref, r_ref, o_ref):
    o_ref[0] = jnp.dot(l_ref[0], r_ref[0], preferred_element_type=F32, precision=HIGHEST)


def _s5_params(a_re, a_im, log_dt, b_re, b_im, c_re, c_im):
    T = S5_T
    pn = GROUP * STATE
    tile_p = lambda a: jnp.tile(a, (1, GROUP))
    args = (tile_p(a_re), tile_p(a_im), jnp.broadcast_to(log_dt[:, None], (N_GROUPS, pn)),
            c_re.reshape(N_GROUPS, pn), c_im.reshape(N_GROUPS, pn),
            jnp.swapaxes(b_re, 1, 2).reshape(N_GROUPS, pn), jnp.swapaxes(b_im, 1, 2).reshape(N_GROUPS, pn))
    full2 = pl.BlockSpec((N_GROUPS, pn), lambda: (0, 0))
    clre, clim, wbre, wbim, bbre, bbim, ltre, ltim = pl.pallas_call(
        _s5_param_kernel,
        in_specs=[full2] * 7,
        out_specs=[pl.BlockSpec((T + 1, N_GROUPS, pn), lambda: (0, 0, 0))] * 2
                  + [pl.BlockSpec((T, N_GROUPS, pn), lambda: (0, 0, 0))] * 2 + [full2] * 4,
        out_shape=[jax.ShapeDtypeStruct((T + 1, N_GROUPS, pn), F32)] * 2
                  + [jax.ShapeDtypeStruct((T, N_GROUPS, pn), F32)] * 2
                  + [jax.ShapeDtypeStruct((N_GROUPS, pn), F32)] * 4,
        name="s5_params",
    )(*args)

    r5 = lambda a, t: a[:t].reshape(t, N_GROUPS, GROUP, STATE)
    lhs = jnp.concatenate([r5(clre, T), -r5(clim, T)], axis=-1)
    lhs = jnp.transpose(lhs, (1, 0, 2, 3)).reshape(N_GROUPS, T * GROUP, 2 * STATE)
    bb = lambda a: jnp.swapaxes(a.reshape(N_GROUPS, GROUP, STATE), 1, 2)
    rhs = jnp.concatenate([bb(bbre), bb(bbim)], axis=1)
    kmat = pl.pallas_call(
        _s5_kmat_kernel,
        grid=(N_GROUPS,),
        in_specs=[pl.BlockSpec((1, T * GROUP, 2 * STATE), lambda g: (g, 0, 0)),
                  pl.BlockSpec((1, 2 * STATE, GROUP), lambda g: (g, 0, 0))],
        out_specs=pl.BlockSpec((1, T * GROUP, GROUP), lambda g: (g, 0, 0)),
        out_shape=jax.ShapeDtypeStruct((N_GROUPS, T * GROUP, GROUP), F32),
        compiler_params=_cparams("arbitrary"),
        name="s5_kmat",
    )(lhs, rhs)

    eye = jnp.eye(S5_GL, dtype=F32)
    kt = kmat.reshape(S5_SG, S5_GL, T, GROUP, GROUP)
    kbd = jnp.einsum('sgtpq,gh->stgqhp', kt, eye).reshape(S5_SG, T, LANES, LANES)
    krev = kbd[:, ::-1].reshape(S5_SG, T * LANES, LANES).astype(BF16)
    r6 = lambda a: a.reshape(T, S5_SG, S5_GL, GROUP, STATE)
    wb = jnp.stack([r6(wbre), r6(wbim)], axis=-2)
    wb = jnp.einsum('ksgpin,gh->skgpihn', wb, eye).reshape(S5_SG, T * LANES, S5_NSTATE).astype(BF16)
    wc = jnp.stack([r6(clre[1:]), -r6(clim[1:])], axis=-2)
    wc = jnp.einsum('tsgpin,gh->signthp', wc, eye).reshape(S5_SG, S5_NSTATE, T * LANES).astype(BF16)
    lt = lambda a: a.reshape(N_GROUPS, GROUP, STATE)[:, 0].reshape(S5_SG, 1, S5_GL * STATE)
    return krev, wb, wc, lt(ltre), lt(ltim)


S5_TC = 512


def _s5_state_kernel(x_ref, wb_ref, e_ref):
    e_ref[0] = _dot(x_ref[0], wb_ref[0])


def _s5_scan_kernel(e_ref, ltre_ref, ltim_ref, xs_ref):
    lr, li = ltre_ref[0], ltim_ref[0]
    half = S5_NSTATE // 2

    def step(c, carry):
        new = []
        for b in range(BATCH):
            xr, xi = carry[b]
            row = b * S5_CH_PER_BATCH + c
            xs_ref[0, pl.ds(row, 1), :half] = xr
            xs_ref[0, pl.ds(row, 1), half:] = xi
            e = e_ref[0, pl.ds(row, 1), :]
            new.append((lr * xr - li * xi + e[:, :half], lr * xi + li * xr + e[:, half:]))
        return tuple(new)

    zero = jnp.zeros((1, half), F32)
    lax.fori_loop(0, S5_CH_PER_BATCH, step, tuple((zero, zero) for _ in range(BATCH)))


def _s5_out_kernel(x_ref, xs_ref, krev_ref, wc_ref, y_ref):
    xsb = xs_ref[0].astype(BF16)
    for t in range(S5_T):
        cols = slice(t * LANES, (t + 1) * LANES)
        y_ref[0, :, cols] = (_dot(x_ref[0, :, :(t + 1) * LANES], krev_ref[0, (S5_T - 1 - t) * LANES:, :])
                             + _dot(xsb, wc_ref[0, :, cols]))


def _s5(u, krev, wb, wc, ltre, ltim):
    T, tc = S5_T, S5_TC
    xs = u.astype(BF16).reshape(S5_CH, T, S5_SG, LANES).transpose(2, 0, 1, 3).reshape(S5_SG, S5_CH, T * LANES)
    grid = (S5_SG, S5_CH // tc)
    rows = lambda s, i: (s, i, 0)
    per_sg = lambda s, i: (s, 0, 0)
    e = pl.pallas_call(
        _s5_state_kernel, grid=grid,
        in_specs=[pl.BlockSpec((1, tc, T * LANES), rows), pl.BlockSpec((1, T * LANES, S5_NSTATE), per_sg)],
        out_specs=pl.BlockSpec((1, tc, S5_NSTATE), rows),
        out_shape=jax.ShapeDtypeStruct((S5_SG, S5_CH, S5_NSTATE), F32),
        compiler_params=_cparams("arbitrary", "arbitrary"), name="s5_state",
    )(xs, wb)
    sg1 = lambda s: (s, 0, 0)
    xstart = pl.pallas_call(
        _s5_scan_kernel, grid=(S5_SG,),
        in_specs=[pl.BlockSpec((1, S5_CH, S5_NSTATE), sg1),
                  pl.BlockSpec((1, 1, S5_NSTATE // 2), sg1), pl.BlockSpec((1, 1, S5_NSTATE // 2), sg1)],
        out_specs=pl.BlockSpec((1, S5_CH, S5_NSTATE), sg1),
        out_shape=jax.ShapeDtypeStruct((S5_SG, S5_CH, S5_NSTATE), F32),
        compiler_params=_cparams("arbitrary"), name="s5_scan",
    )(e, ltre, ltim)
    y = pl.pallas_call(
        _s5_out_kernel, grid=grid,
        in_specs=[pl.BlockSpec((1, tc, T * LANES), rows), pl.BlockSpec((1, tc, S5_NSTATE), rows),
                  pl.BlockSpec((1, T * LANES, LANES), per_sg), pl.BlockSpec((1, S5_NSTATE, T * LANES), per_sg)],
        out_specs=pl.BlockSpec((1, tc, T * LANES), rows),
        out_shape=jax.ShapeDtypeStruct((S5_SG, S5_CH, T * LANES), F32),
        compiler_params=_cparams("arbitrary", "arbitrary"), name="s5_out",
    )(xs, xstart, krev, wc)
    return y.reshape(S5_SG, S5_CH, T, LANES).transpose(1, 2, 0, 3).reshape(N_TOK, SSM_WIDTH)


MERGE_TM = 256


def _merge_kernel(ocmp_ref, osel_ref, owin_ref, gn_ref, yssm_ref, u_ref, ga_ref, gs_ref, x_ref, mod_ref,
                  eg_ref, dskip_ref, wglu_ref, bglu_ref, wua_ref, wus_ref, wout_ref, g2_ref,
                  wrhi_ref, wrlo_ref, wsgu_ref, wsd_ref,
                  xpart_ref, h2_ref, logit_ref):
    mod = mod_ref[0]
    gnb = gn_ref[...].astype(BF16)
    o_nsa = (_dot(gnb, eg_ref[0]) * ocmp_ref[...].astype(F32)
             + _dot(gnb, eg_ref[1]) * osel_ref[...].astype(F32)
             + _dot(gnb, eg_ref[2]) * owin_ref[...].astype(F32))
    attn = _dot(o_nsa.astype(BF16), wua_ref[...])
    z = _gelu(yssm_ref[...] + dskip_ref[...] * u_ref[...])
    y_ssm = z * jax.nn.sigmoid(_dot(z.astype(BF16), wglu_ref[...]) + bglu_ref[...])
    ssm = _dot(y_ssm.astype(BF16), wus_ref[...])
    merged = ga_ref[...].astype(F32) * attn + gs_ref[...].astype(F32) * ssm
    x1 = x_ref[...] + mod[2:3] * _dot(merged.astype(BF16), wout_ref[...])

    ms = jnp.mean(x1 * x1, axis=-1, keepdims=True)
    h2 = (x1 * lax.rsqrt(ms + EPS) * g2_ref[...]) * (1.0 + mod[4:5]) + mod[3:4]
    hi = h2.astype(BF16)
    lo = (h2 - hi.astype(F32)).astype(BF16)
    h2_ref[...] = _pack_bf16_pairs(h2)
    logit_ref[...] = _dot_nt(wrhi_ref[...], hi) + _dot_nt(wrhi_ref[...], lo) + _dot_nt(wrlo_ref[...], hi)
    gu = _dot(hi, wsgu_ref[...])
    shared = _dot((_silu(gu[:, :D_EXPERT]) * gu[:, D_EXPERT:]).astype(BF16), wsd_ref[...])
    xpart_ref[...] = x1 + mod[5:6] * shared


def _merge(ocmp, osel, owin, gn, yssm, u, ga, gs, x2, mod, d_skip, w_glu, b_glu, w_up_attn, w_up_ssm, w_out,
           g_norm2, w_router, ws_gate, ws_up, ws_down):
    tm = MERGE_TM
    eg = np.zeros((3, LANES, NSA_WIDTH), np.float32)
    for j in range(3):
        for h in range(N_HEADS):
            eg[j, 3 * h + j, h * HEAD_DIM:(h + 1) * HEAD_DIM] = 1.0
    wr_t = w_router.T
    wr_hi = wr_t.astype(BF16)
    wr_lo = (wr_t - wr_hi.astype(F32)).astype(BF16)
    row = lambda i: (i, 0)
    fix2 = lambda i: (0, 0)
    wspec = lambda a: pl.BlockSpec(a.shape, (lambda i: (0,) * a.ndim))
    weights = [jnp.asarray(eg, BF16), d_skip.reshape(1, -1), w_glu.astype(BF16), b_glu.reshape(1, -1),
               w_up_attn.astype(BF16), w_up_ssm.astype(BF16), w_out.astype(BF16), g_norm2.reshape(1, -1),
               wr_hi, wr_lo, jnp.concatenate([ws_gate, ws_up], axis=1).astype(BF16), ws_down.astype(BF16)]
    acts = [(ocmp, 512), (osel, 512), (owin, 512), (gn, 128), (yssm, 512), (u, 512), (ga, 1024), (gs, 1024),
            (x2, 1024)]
    return pl.pallas_call(
        _merge_kernel,
        grid=(N_TOK // tm,),
        in_specs=[pl.BlockSpec((tm, wd), row) for _, wd in acts]
                 + [pl.BlockSpec((1, 6, D_MODEL), lambda i: (i // (SEQ // tm), 0, 0))]
                 + [wspec(w) for w in weights],
        out_specs=[pl.BlockSpec((tm, D_MODEL), row), pl.BlockSpec((tm, HALF), row),
                   pl.BlockSpec((N_EXPERTS, tm), lambda i: (0, i))],
        out_shape=[jax.ShapeDtypeStruct((N_TOK, D_MODEL), F32), jax.ShapeDtypeStruct((N_TOK, HALF), jnp.uint32),
                   jax.ShapeDtypeStruct((N_EXPERTS, N_TOK), F32)],
        compiler_params=_cparams("arbitrary"),
        name="merge",
    )(*[a for a, _ in acts], mod, *weights)


ROUTE_TN = 512


def _route_kernel(logit_ref, bias_ref, eidx_ref, w_ref, count_ref, gscore_ref, masked_ref):
    tn = ROUTE_TN

    @pl.when(pl.program_id(0) == 0)
    def _():
        count_ref[...] = jnp.zeros(count_ref.shape, F32)

    sc = jax.nn.sigmoid(logit_ref[...])
    biased = sc + bias_ref[...]
    gi = lax.broadcasted_iota(I32, (EXPERTS_PER_GROUP, tn), 0).astype(F32)
    for g in range(N_EXPERT_GROUPS):
        blk = biased[g * EXPERTS_PER_GROUP:(g + 1) * EXPERTS_PER_GROUP]
        m1 = jnp.max(blk, axis=0, keepdims=True)
        i1 = jnp.min(jnp.where(blk == m1, gi, float(EXPERTS_PER_GROUP)), axis=0, keepdims=True)
        m2 = jnp.max(jnp.where(gi == i1, -jnp.inf, blk), axis=0, keepdims=True)
        gscore_ref[g:g + 1, :] = m1 + m2
    gs = gscore_ref[...]
    gidx = lax.broadcasted_iota(I32, (N_EXPERT_GROUPS, tn), 0)
    grank = jnp.zeros((N_EXPERT_GROUPS, tn), F32)
    for gp in range(N_EXPERT_GROUPS):
        row = gs[gp:gp + 1, :]
        tie = jnp.where(gidx > gp, 1.0, 0.0)
        grank = grank + jnp.where(row > gs, 1.0, jnp.where(row == gs, tie, 0.0))
    for g in range(N_EXPERT_GROUPS):
        keep = grank[g:g + 1, :] < float(TOPK_GROUPS)
        sl = slice(g * EXPERTS_PER_GROUP, (g + 1) * EXPERTS_PER_GROUP)
        masked_ref[sl, :] = jnp.where(keep, biased[sl], -jnp.inf)
    cur = masked_ref[...]
    eidx = lax.broadcasted_iota(I32, (N_EXPERTS, tn), 0).astype(F32)
    wsum = jnp.zeros((1, tn), F32)
    hits = jnp.zeros((N_EXPERTS, tn), F32)
    for k in range(TOP_K):
        m = jnp.max(cur, axis=0, keepdims=True)
        idx = jnp.min(jnp.where(cur == m, eidx, float(N_EXPERTS)), axis=0, keepdims=True)
        hit = eidx == idx
        wk = jnp.sum(jnp.where(hit, sc, 0.0), axis=0, keepdims=True)
        cur = jnp.where(hit, -jnp.inf, cur)
        hits = hits + jnp.where(hit, 1.0, 0.0)
        eidx_ref[k:k + 1, :] = idx.astype(I32)
        w_ref[k:k + 1, :] = wk
        wsum = wsum + wk
    w_ref[...] = w_ref[...] / wsum * ROUTE_SCALE
    count_ref[...] = count_ref[...] + jnp.sum(hits, axis=1, keepdims=True)


def _route(logits_t, router_bias):
    tn = ROUTE_TN
    return pl.pallas_call(
        _route_kernel,
        grid=(N_TOK // tn,),
        in_specs=[pl.BlockSpec((N_EXPERTS, tn), lambda i: (0, i)), pl.BlockSpec((N_EXPERTS, 1), lambda i: (0, 0))],
        out_specs=[pl.BlockSpec((TOP_K, tn), lambda i: (0, i))] * 2 + [pl.BlockSpec((N_EXPERTS, 1), lambda i: (0, 0))],
        out_shape=[jax.ShapeDtypeStruct((TOP_K, N_TOK), I32), jax.ShapeDtypeStruct((TOP_K, N_TOK), F32),
                   jax.ShapeDtypeStruct((N_EXPERTS, 1), F32)],
        scratch_shapes=[pltpu.VMEM((N_EXPERT_GROUPS, tn), F32), pltpu.VMEM((N_EXPERTS, tn), F32)],
        compiler_params=_cparams("arbitrary"),
        name="route",
    )(logits_t, router_bias.reshape(-1, 1))


N_MOE_BLK = NK // DISPATCH_BLOCK
N_ITEMS = N_MOE_BLK + N_EXPERTS
ASSIGN_BITS = 17


def _dispatch_plan(eidx, counts):
    e_flat = eidx.reshape(-1)
    key = jnp.sort(e_flat * NK + jnp.arange(NK, dtype=I32))
    order = key & (NK - 1)
    counts = counts.reshape(-1).astype(I32)
    start = jnp.cumsum(counts) - counts
    cuts = jnp.sort(jnp.concatenate([jnp.arange(N_MOE_BLK, dtype=I32) * DISPATCH_BLOCK, start]))
    lo = cuts
    hi = jnp.concatenate([cuts[1:], jnp.full((1,), NK, I32)])
    blk = jnp.minimum(lo // DISPATCH_BLOCK, N_MOE_BLK - 1)
    expert = jnp.clip(jnp.sum((start[None, :] <= lo[:, None]).astype(I32), axis=1) - 1, 0, N_EXPERTS - 1)
    one = jnp.ones((1,), I32)
    first = jnp.concatenate([one, (blk[1:] != blk[:-1]).astype(I32)])
    last = jnp.concatenate([(blk[1:] != blk[:-1]).astype(I32), one])
    new_expert = jnp.concatenate([one, (expert[1:] != expert[:-1]).astype(I32)])
    run_id = jnp.cumsum(new_expert) - 1
    n_runs = run_id[-1] + 1
    item = jnp.arange(N_ITEMS, dtype=I32)
    run_first_item = jnp.sort(jnp.where(new_expert == 1, item, N_ITEMS))
    run_expert = expert[jnp.minimum(run_first_item, N_ITEMS - 1)]
    ahead = run_id + (WEIGHT_RING - 1)
    ahead_expert = run_expert[jnp.minimum(ahead, N_ITEMS - 1)]
    ahead_valid = (ahead < n_runs).astype(I32)
    second_expert = run_expert[1:2]
    prologue = jnp.concatenate([second_expert, (n_runs > 1).astype(I32).reshape(1)])
    _, pos = lax.sort_key_val(order, jnp.arange(NK, dtype=I32))
    tok = jnp.right_shift(order, 3)
    return tok, pos, (blk, expert, lo - blk * DISPATCH_BLOCK, hi - blk * DISPATCH_BLOCK, first, last, new_expert,
                      run_id % WEIGHT_RING, ahead_expert, ahead_valid, prologue)


SC_CORES = 2
SC_SUBCORES = 16
SC_CHUNK = 128


def _sc_gather_rows(table, idx):
    n = idx.shape[0]
    workers = SC_CORES * SC_SUBCORES
    per_worker = n // workers
    n_chunks = per_worker // SC_CHUNK
    assert per_worker * workers == n and n_chunks * SC_CHUNK == per_worker
    mesh = plsc.VectorSubcoreMesh(core_axis_name="c", subcore_axis_name="s",
                                  num_cores=SC_CORES, num_subcores=SC_SUBCORES)

    def body(table_hbm, idx_hbm, out_hbm, idx_v, rows_v, sem):
        wid = lax.axis_index("s") * SC_CORES + lax.axis_index("c")
        base = wid * per_worker

        @pl.loop(0, n_chunks)
        def _(j):
            off = base + j * SC_CHUNK
            pltpu.sync_copy(idx_hbm.at[pl.ds(off, SC_CHUNK)], idx_v)
            pltpu.async_copy(table_hbm.at[idx_v], rows_v, sem).wait()
            pltpu.sync_copy(rows_v, out_hbm.at[pl.ds(off, SC_CHUNK)])

    return pl.kernel(
        body,
        out_type=jax.ShapeDtypeStruct((n, table.shape[1]), table.dtype),
        mesh=mesh,
        scratch_types=[pltpu.VMEM((SC_CHUNK,), I32), pltpu.VMEM((SC_CHUNK, table.shape[1]), table.dtype),
                       pltpu.SemaphoreType.DMA],
        name="sc_gather_rows",
    )(table, idx)


WEIGHT_RING = 3
WEIGHT_CHUNKS = 4


def _expert_weight_copies(w_hbm, wbuf, sem, expert, slot):
    rows = w_hbm.shape[1] // WEIGHT_CHUNKS
    return [pltpu.make_async_copy(w_hbm.at[expert, pl.ds(c * rows, rows)],
                                  wbuf.at[slot, pl.ds(c * rows, rows)], sem.at[slot])
            for c in range(WEIGHT_CHUNKS)]


def _moe_kernel(blk_ref, exp_ref, lo_ref, hi_ref, first_ref, last_ref, newexp_ref,
                slot_ref, ahead_exp_ref, ahead_ok_ref, prologue_ref,
                x_ref, wg_hbm, wu_hbm, wd_hbm, y_ref,
                acc_ref, wgf_ref, wuf_ref, wdf_ref, wgb_ref, wub_ref, wdb_ref, wsem):
    it = pl.program_id(0)
    lo, hi = lo_ref[it], hi_ref[it]
    streams = ((wg_hbm, wgf_ref), (wu_hbm, wuf_ref), (wd_hbm, wdf_ref))

    def request(expert, slot):
        for w_hbm, wbuf in streams:
            for cp in _expert_weight_copies(w_hbm, wbuf, wsem, expert, slot):
                cp.start()

    @pl.when(it == 0)
    def _():
        request(exp_ref[0], 0)

        @pl.when(prologue_ref[1] == 1)
        def _():
            request(prologue_ref[0], 1)

    @pl.when(newexp_ref[it] == 1)
    def _():
        slot = slot_ref[it]
        for w_hbm, wbuf in streams:
            for cp in _expert_weight_copies(w_hbm, wbuf, wsem, 0, slot):
                cp.wait()
        wgb_ref[...] = wgf_ref[slot].astype(BF16)
        wub_ref[...] = wuf_ref[slot].astype(BF16)
        wdb_ref[...] = wdf_ref[slot].astype(BF16)

        @pl.when(ahead_ok_ref[it] == 1)
        def _():
            ahead_slot = slot + (WEIGHT_RING - 1)
            request(ahead_exp_ref[it], jnp.where(ahead_slot >= WEIGHT_RING, ahead_slot - WEIGHT_RING, ahead_slot))

    @pl.when(first_ref[it] == 1)
    def _():
        acc_ref[...] = jnp.zeros(acc_ref.shape, F32)

    def expert_pass(r0, nrows):
        rows = slice(r0, r0 + nrows)
        ridx = r0 + lax.broadcasted_iota(I32, (nrows, HALF), 0)
        mine = (ridx >= lo) & (ridx < hi)
        xlo, xhi = _unpack_bf16_pairs(jnp.where(mine, x_ref[rows, :], jnp.uint32(0)))
        xlo, xhi = xlo.astype(BF16), xhi.astype(BF16)
        gate = _dot(xlo, wgb_ref[:HALF]) + _dot(xhi, wgb_ref[HALF:])
        up = _dot(xlo, wub_ref[:HALF]) + _dot(xhi, wub_ref[HALF:])
        acc_ref[rows, :] = acc_ref[rows, :] + _dot((_silu(gate) * up).astype(BF16), wdb_ref[...])

    mid = DISPATCH_BLOCK // 2
    pl.when((lo < mid) & (hi > mid))(lambda: expert_pass(0, DISPATCH_BLOCK))
    pl.when((hi > lo) & (hi <= mid))(lambda: expert_pass(0, mid))
    pl.when((hi > lo) & (lo >= mid))(lambda: expert_pass(mid, mid))

    @pl.when(last_ref[it] == 1)
    def _():
        y_ref[...] = _pack_bf16_pairs(acc_ref[...])


def _moe(xs, items, w_gate, w_up, w_down):
    by_blk = lambda it, blk, *_: (blk[it], 0)
    any_space = pl.BlockSpec(memory_space=pl.ANY)
    grid_spec = pltpu.PrefetchScalarGridSpec(
        num_scalar_prefetch=len(items),
        grid=(N_ITEMS,),
        in_specs=[pl.BlockSpec((DISPATCH_BLOCK, HALF), by_blk), any_space, any_space, any_space],
        out_specs=pl.BlockSpec((DISPATCH_BLOCK, HALF), by_blk),
        scratch_shapes=[pltpu.VMEM((DISPATCH_BLOCK, D_MODEL), F32),
                        pltpu.VMEM((WEIGHT_RING, D_MODEL, D_EXPERT), F32),
                        pltpu.VMEM((WEIGHT_RING, D_MODEL, D_EXPERT), F32),
                        pltpu.VMEM((WEIGHT_RING, D_EXPERT, D_MODEL), F32),
                        pltpu.VMEM((D_MODEL, D_EXPERT), BF16), pltpu.VMEM((D_MODEL, D_EXPERT), BF16),
                        pltpu.VMEM((D_EXPERT, D_MODEL), BF16),
                        pltpu.SemaphoreType.DMA((WEIGHT_RING,))],
    )
    return pl.pallas_call(
        _moe_kernel,
        grid_spec=grid_spec,
        out_shape=jax.ShapeDtypeStruct((NK, HALF), jnp.uint32),
        compiler_params=_cparams("arbitrary"),
        name="moe",
    )(*items, xs, w_gate, w_up, w_down)


COMB_TC = 64


def _combine_kernel(slots_ref, w_ref, xpart_ref, mod_ref, out_ref):
    w = w_ref[...]
    lo = jnp.zeros((w.shape[0], HALF), F32)
    hi = jnp.zeros((w.shape[0], HALF), F32)
    for k in range(TOP_K):
        klo, khi = _unpack_bf16_pairs(slots_ref[k])
        lo = lo + w[:, k:k + 1] * klo
        hi = hi + w[:, k:k + 1] * khi
    gate2 = mod_ref[0][5:6]
    out_ref[:, :HALF] = xpart_ref[:, :HALF] + gate2[:, :HALF] * lo
    out_ref[:, HALF:] = xpart_ref[:, HALF:] + gate2[:, HALF:] * hi


def _combine(xpart, mod, slots, w):
    tc = COMB_TC
    row = lambda i: (i, 0)
    return pl.pallas_call(
        _combine_kernel,
        grid=(N_TOK // tc,),
        in_specs=[pl.BlockSpec((TOP_K, tc, HALF), lambda i: (0, i, 0)),
                  pl.BlockSpec((tc, TOP_K), row),
                  pl.BlockSpec((tc, D_MODEL), row),
                  pl.BlockSpec((1, 6, D_MODEL), lambda i: (i // (SEQ // tc), 0, 0))],
        out_specs=pl.BlockSpec((tc, D_MODEL), row),
        out_shape=jax.ShapeDtypeStruct((N_TOK, D_MODEL), F32),
        compiler_params=_cparams("arbitrary"),
        name="combine",
    )(slots.reshape(TOP_K, N_TOK, HALF), w, xpart, mod)


def _layer(x, c, w_ada, b_ada, g_norm1, g_norm2, w_in, q_gain, kc_gain, ks_gain, kw_gain,
           pe_k, pe_v, w_cmp_k1, w_cmp_k2, w_cmp_v1, w_cmp_v2,
           a_re, a_im, log_dt, b_re, b_im, c_re, c_im, d_skip, w_glu, b_glu,
           w_up_attn, w_up_ssm, w_out, w_router, router_bias,
           w_gate, w_up, w_down, ws_gate, ws_up, ws_down):
    x2 = x.reshape(N_TOK, D_MODEL)
    mod = _ada(c, w_ada, b_ada)
    q, kc_raw, vc_raw, ks, kw, vst, vwt, gn, u, ga, gs = _proj(x2, mod, g_norm1, w_in, q_gain, ks_gain, kw_gain)
    kcn = _compress(kc_raw, pe_k, w_cmp_k1, w_cmp_k2, kc_gain, True)
    vcn = _compress(vc_raw, pe_v, w_cmp_v1, w_cmp_v2, kc_gain, False)
    ocmp, selb = _cmp_attn(q, kcn, vcn)
    osel, owin = _selwin(q, ks, kw, vst, vwt, selb)
    yssm = _s5(u, *_s5_params(a_re, a_im, log_dt, b_re, b_im, c_re, c_im))
    xpart, h2, logits_t = _merge(ocmp, osel, owin, gn, yssm, u, ga, gs, x2, mod, d_skip, w_glu, b_glu,
                                  w_up_attn, w_up_ssm, w_out, g_norm2, w_router, ws_gate, ws_up, ws_down)
    eidx_t, w_t, counts = _route(logits_t, router_bias)
    tok, pos, items = _dispatch_plan(eidx_t.T, counts)
    y = _moe(_sc_gather_rows(h2, tok), items, w_gate, w_up, w_down)
    slots = _sc_gather_rows(y, pos.reshape(N_TOK, TOP_K).T.reshape(-1))
    return _combine(xpart, mod, slots, w_t.T).reshape(BATCH, SEQ, D_MODEL)


def kernel(x, c, w_ada, b_ada, g_norm1, g_norm2, w_in, q_gain, kc_gain, ks_gain, kw_gain, pe_k, pe_v, w_cmp_k1,
           w_cmp_k2, w_cmp_v1, w_cmp_v2, a_re, a_im, log_dt, b_re, b_im, c_re, c_im, d_skip, w_glu, b_glu,
           w_up_attn, w_up_ssm, w_out, w_router, router_bias, w_gate, w_up, w_down, ws_gate, ws_up, ws_down):
    params = (w_ada, b_ada, g_norm1, g_norm2, w_in, q_gain, kc_gain, ks_gain, kw_gain, pe_k, pe_v, w_cmp_k1,
              w_cmp_k2, w_cmp_v1, w_cmp_v2, a_re, a_im, log_dt, b_re, b_im, c_re, c_im, d_skip, w_glu, b_glu,
              w_up_attn, w_up_ssm, w_out, w_router, router_bias, w_gate, w_up, w_down, ws_gate, ws_up, ws_down)
    depth = w_ada.shape[0]
    for layer in range(depth):
        x = _layer(x, c, *[p[layer] for p in params])
    return x
```

```python
import functools
import math

import jax
import jax.numpy as jnp
import numpy as np
from jax import lax
from jax.experimental import pallas as pl
from jax.experimental.pallas import tpu as pltpu
from jax.experimental.pallas import tpu_sc as plsc

F32 = jnp.float32
BF16 = jnp.bfloat16
I32 = jnp.int32
HIGHEST = lax.Precision.HIGHEST

D_MODEL = 1024
BATCH = 4
SEQ = 4096
N_TOK = BATCH * SEQ
N_HEADS = 8
HEAD_DIM = 64
N_KV = 2
CMP_BLOCK = 32
CMP_STRIDE = 16
CMP_HIDDEN = 256
N_CMP = 256
SEL_BLOCK = 64
N_SEL_BLOCKS = SEQ // SEL_BLOCK
N_SELECT = 16
WINDOW = 512
ATTN_SCALE = HEAD_DIM ** -0.5
LOG2E = 1.4426950408889634
NSA_WIDTH = N_HEADS * HEAD_DIM
SSM_WIDTH = 512
GROUP = 16
N_GROUPS = SSM_WIDTH // GROUP
STATE = 64
N_EXPERTS = 256
TOP_K = 8
D_EXPERT = 256
N_EXPERT_GROUPS = 8
EXPERTS_PER_GROUP = N_EXPERTS // N_EXPERT_GROUPS
TOPK_GROUPS = 4
ROUTE_SCALE = 2.5
DISPATCH_BLOCK = 256
EPS = 1e-6
NEG = -1e30

LANES = 128
S5_T = 16
S5_SG = 4
S5_GL = N_GROUPS // S5_SG
S5_CH = N_TOK // S5_T
S5_CH_PER_BATCH = SEQ // S5_T
S5_NSTATE = S5_GL * STATE * 2

NK = N_TOK * TOP_K
HALF = D_MODEL // 2

VMEM_LIMIT = 48 * 1024 * 1024


def _cparams(*sem):
    return pltpu.CompilerParams(dimension_semantics=tuple(sem), vmem_limit_bytes=VMEM_LIMIT)


def _dot(a, b):
    return jnp.dot(a, b, preferred_element_type=F32)


def _dot_nt(a, b):
    return lax.dot_general(a, b, (((1,), (1,)), ((), ())), preferred_element_type=F32)


def _split_dot(v, w):
    hi = v.astype(BF16)
    lo = (v - hi.astype(F32)).astype(BF16)
    return _dot(hi, w) + _dot(lo, w)


def _seg_rms(v, bd, gain):
    ss = _split_dot(v * v, bd)
    return v * lax.rsqrt(ss * (1.0 / HEAD_DIM) + EPS) * gain


def _gelu(x):
    return 0.5 * x * (1.0 + jnp.tanh(0.7978845608028654 * (x + 0.044715 * (x * x * x))))


def _silu(x):
    return x * jax.nn.sigmoid(x)


def _pack_bf16_pairs(v):
    bits = lax.bitcast_convert_type(v.astype(BF16).astype(F32), jnp.uint32)
    h = v.shape[1] // 2
    return bits[:, h:] | lax.shift_right_logical(bits[:, :h], jnp.uint32(16))


def _unpack_bf16_pairs(word):
    lo = lax.bitcast_convert_type(lax.shift_left(word, jnp.uint32(16)), F32)
    hi = lax.bitcast_convert_type(word & jnp.uint32(0xFFFF0000), F32)
    return lo, hi


def _ada_kernel(c_ref, w_ref, b_ref, o_ref):
    c = c_ref[...]
    o_ref[...] = jnp.dot(_silu(c), w_ref[...], preferred_element_type=F32, precision=HIGHEST) + b_ref[...]


def _ada(c, w_ada, b_ada):
    cp = jnp.pad(c, ((0, 8 - BATCH), (0, 0)))
    tn = 1536
    out = pl.pallas_call(
        _ada_kernel,
        grid=(6 * D_MODEL // tn,),
        in_specs=[pl.BlockSpec((8, D_MODEL), lambda j: (0, 0)),
                  pl.BlockSpec((D_MODEL, tn), lambda j: (0, j)),
                  pl.BlockSpec((1, tn), lambda j: (0, j))],
        out_specs=pl.BlockSpec((8, tn), lambda j: (0, j)),
        out_shape=jax.ShapeDtypeStruct((8, 6 * D_MODEL), F32),
        compiler_params=_cparams("arbitrary"),
        name="ada",
    )(cp, w_ada, b_ada.reshape(1, -1))
    return out.reshape(8, 6, D_MODEL)


_C_Q = 0
_C_KC = 512
_C_VC = 640
_C_KS = 768
_C_KW = 1024
_C_GN = 1280
_C_U = 1408
_C_GA = 1920
_C_GS = 2944
_C_END = 3968
PROJ_TM = 512


def _proj_kernel(x_ref, mod_ref, g1_ref, w_ref, wvt_ref, qg_ref, ksg_ref, kwg_ref, bd512_ref, bd256_ref,
                 q_ref, kc_ref, vc_ref, ks_ref, kw_ref, vst_ref, vwt_ref, gn_ref, u_ref, ga_ref, gs_ref):
    x = x_ref[...]
    ms = jnp.mean(x * x, axis=-1, keepdims=True)
    mod = mod_ref[0]
    h = (x * lax.rsqrt(ms + EPS) * g1_ref[...]) * (1.0 + mod[1:2]) + mod[0:1]
    hb = h.astype(BF16)

    def p(lo, hi):
        return _dot(hb, w_ref[:, lo:hi])

    q_ref[...] = _seg_rms(p(_C_Q, _C_KC), bd512_ref[...], qg_ref[...] * (ATTN_SCALE * LOG2E)).astype(BF16)
    kc_ref[...] = p(_C_KC, _C_VC).astype(BF16)
    vc_ref[...] = p(_C_VC, _C_KS).astype(BF16)
    ks_ref[...] = _seg_rms(p(_C_KS, _C_KW), bd256_ref[...], ksg_ref[...]).astype(BF16)
    kw_ref[...] = _seg_rms(p(_C_KW, _C_GN), bd256_ref[...], kwg_ref[...]).astype(BF16)
    vt = _dot_nt(wvt_ref[...], hb)
    vst_ref[...] = vt[:LANES].astype(BF16)
    vwt_ref[...] = vt[LANES:].astype(BF16)
    gn_ref[...] = jax.nn.sigmoid(p(_C_GN, _C_U))
    u_ref[...] = p(_C_U, _C_GA)
    ga_ref[...] = jax.nn.sigmoid(p(_C_GA, _C_GS)).astype(BF16)
    gs_ref[...] = jax.nn.sigmoid(p(_C_GS, _C_END)).astype(BF16)


def _dup_cols(w):
    return jnp.concatenate([w[:, :64], w[:, :64], w[:, 64:], w[:, 64:]], axis=1)


def _block_ones(n):
    return jnp.kron(jnp.eye(n // HEAD_DIM, dtype=F32), jnp.ones((HEAD_DIM, HEAD_DIM), F32)).astype(BF16)


def _proj(x2, mod, g_norm1, w_in, q_gain, ks_gain, kw_gain):
    o = np.cumsum((0, 512, 128, 128, 128, 128, 128, 128, 24, 512, 1024, 1024))
    parts = [w_in[:, o[i]:o[i + 1]] for i in range(11)]
    wq, wkc, wvc, wks, wvs, wkw, wvw, wgn, wu, wga, wgs = parts
    w = jnp.concatenate([wq, wkc, wvc, _dup_cols(wks), _dup_cols(wkw),
                         jnp.pad(wgn, ((0, 0), (0, LANES - 24))), wu, wga, wgs], axis=1).astype(BF16)
    wvt = jnp.concatenate([wvs, wvw], axis=1).T.astype(BF16)
    tm = PROJ_TM
    row = lambda i: (i, 0)
    col = lambda i: (0, i)
    fix = lambda i: (0, 0)
    outs = [(512, BF16, row), (128, BF16, row), (128, BF16, row), (256, BF16, row), (256, BF16, row),
            (LANES, BF16, col), (LANES, BF16, col),
            (128, F32, row), (512, F32, row), (1024, BF16, row), (1024, BF16, row)]
    ospec = lambda wd, m: pl.BlockSpec((tm, wd), m) if m is row else pl.BlockSpec((wd, tm), m)
    oshape = lambda wd, dt, m: jax.ShapeDtypeStruct((N_TOK, wd) if m is row else (wd, N_TOK), dt)
    return pl.pallas_call(
        _proj_kernel,
        grid=(N_TOK // tm,),
        in_specs=[pl.BlockSpec((tm, D_MODEL), row),
                  pl.BlockSpec((1, 6, D_MODEL), lambda i: (i // (SEQ // tm), 0, 0)),
                  pl.BlockSpec((1, D_MODEL), fix),
                  pl.BlockSpec((D_MODEL, _C_END), fix),
                  pl.BlockSpec((2 * LANES, D_MODEL), fix),
                  pl.BlockSpec((1, 512), fix), pl.BlockSpec((1, 256), fix), pl.BlockSpec((1, 256), fix),
                  pl.BlockSpec((512, 512), fix), pl.BlockSpec((256, 256), fix)],
        out_specs=[ospec(wd, m) for wd, _, m in outs],
        out_shape=[oshape(wd, dt, m) for wd, dt, m in outs],
        compiler_params=_cparams("arbitrary"),
        name="proj",
    )(x2, mod, g_norm1.reshape(1, -1), w, wvt,
      jnp.tile(q_gain, N_HEADS).reshape(1, -1), jnp.tile(ks_gain, 4).reshape(1, -1),
      jnp.tile(kw_gain, 4).reshape(1, -1), _block_ones(512), _block_ones(256))


def _compress_kernel(r_ref, pe_ref, w1_ref, w2_ref, bd_ref, gain_ref, o_ref, *, do_norm):
    r = r_ref[0].astype(F32)
    p0 = _dot((r + pe_ref[0]).astype(BF16), w1_ref[0])
    p1 = _dot((r + pe_ref[1]).astype(BF16), w1_ref[1])
    hid = p0 + pltpu.roll(p1, N_CMP - 1, 0)
    c = _dot(_gelu(hid).astype(BF16), w2_ref[...])
    if do_norm:
        c = _seg_rms(c, bd_ref[...], gain_ref[...])
    o_ref[0] = c.astype(BF16)


def _compress(raw, pe, w1, w2, gain, do_norm):
    r = raw.reshape(BATCH, SEQ // CMP_STRIDE, CMP_STRIDE * LANES)
    eye = jnp.eye(N_KV, dtype=F32)
    w1r = w1.reshape(2, CMP_STRIDE, HEAD_DIM, CMP_HIDDEN)
    w1big = jnp.einsum('hldc,gk->hlgdkc', w1r, eye).reshape(2, CMP_STRIDE * LANES, N_KV * CMP_HIDDEN).astype(BF16)
    w2big = jnp.einsum('cd,gk->gckd', w2, eye)
    w2big = jnp.concatenate([w2big, w2big], axis=-1).reshape(N_KV * CMP_HIDDEN, 4 * HEAD_DIM).astype(BF16)
    pe_big = jnp.broadcast_to(pe.reshape(2, CMP_STRIDE, 1, HEAD_DIM), (2, CMP_STRIDE, N_KV, HEAD_DIM))
    pe_big = pe_big.reshape(2, 1, CMP_STRIDE * LANES)
    fix2 = lambda b: (0, 0)
    fix3 = lambda b: (0, 0, 0)
    return pl.pallas_call(
        functools.partial(_compress_kernel, do_norm=do_norm),
        grid=(BATCH,),
        in_specs=[pl.BlockSpec((1, N_CMP, CMP_STRIDE * LANES), lambda b: (b, 0, 0)),
                  pl.BlockSpec((2, 1, CMP_STRIDE * LANES), fix3),
                  pl.BlockSpec((2, CMP_STRIDE * LANES, N_KV * CMP_HIDDEN), fix3),
                  pl.BlockSpec((N_KV * CMP_HIDDEN, 256), fix2),
                  pl.BlockSpec((256, 256), fix2), pl.BlockSpec((1, 256), fix2)],
        out_specs=pl.BlockSpec((1, N_CMP, 256), lambda b: (b, 0, 0)),
        out_shape=jax.ShapeDtypeStruct((BATCH, N_CMP, 256), BF16),
        compiler_params=_cparams("arbitrary"),
        name="compress_k" if do_norm else "compress_v",
    )(r, pe_big, w1big, w2big, _block_ones(256), jnp.tile(gain, 4).reshape(1, -1))


ATT_TQ = 256
RANK_CHUNK = 16


def _head_variants(qb):
    lane = lax.broadcasted_iota(I32, qb.shape, 1)
    z = jnp.zeros_like(qb)
    return jnp.where(lane < HEAD_DIM, qb, z), jnp.where(lane < HEAD_DIM, z, qb)


def _cmp_kernel(q_ref, kc_ref, vc_ref, ov_ref, o_ref, sel_ref, vrank_ref):
    tq = ATT_TQ
    qi = pl.program_id(1)
    tpos = qi * tq + lax.broadcasted_iota(I32, (tq, N_CMP), 0)
    nidx = lax.broadcasted_iota(I32, (tq, N_CMP), 1)
    mask = (CMP_STRIDE * nidx + (CMP_BLOCK - 1)) <= tpos
    lane_lo = lax.broadcasted_iota(I32, (tq, LANES), 1) < HEAD_DIM
    for g in range(N_KV):
        kd = kc_ref[0, :, g * LANES:(g + 1) * LANES]
        vd = vc_ref[0, :, g * LANES:(g + 1) * LANES]
        psum = jnp.zeros((tq, N_CMP), F32)
        for jb in range(2):
            blk = 2 * g + jb
            pv = []
            for qv in _head_variants(q_ref[:, blk * LANES:(blk + 1) * LANES]):
                s = jnp.where(mask, _dot_nt(qv, kd), NEG)
                m = jnp.max(s, axis=-1, keepdims=True)
                e = jnp.where(mask, jnp.exp2(s - m), 0.0)
                l = jnp.sum(e, axis=-1, keepdims=True)
                p = e / jnp.where(l > 0.0, l, 1.0)
                psum = psum + p
                pv.append(_dot(p.astype(BF16), vd))
            o_ref[:, blk * LANES:(blk + 1) * LANES] = jnp.where(lane_lo, pv[0], pv[1]).astype(BF16)
        imp = _split_dot(psum, ov_ref[...])
        imp_t = imp.T[:N_SEL_BLOCKS]
        j = lax.broadcasted_iota(I32, (N_SEL_BLOCKS, tq), 0)
        cur = jnp.right_shift(qi * tq + lax.broadcasted_iota(I32, (N_SEL_BLOCKS, tq), 1), 6)
        forced = (j == 0) | (j == cur) | (j == cur - 1)
        v = jnp.where(forced, jnp.inf, jnp.where(j <= cur, imp_t, -jnp.inf))
        vrank_ref[...] = jnp.zeros((N_SEL_BLOCKS, tq), F32)
        n_live = (qi + 1) * (tq // SEL_BLOCK)
        for c0 in range(0, N_SEL_BLOCKS, RANK_CHUNK):
            @pl.when(c0 < n_live)
            def _():
                rank = vrank_ref[...]
                for jp in range(c0, c0 + RANK_CHUNK):
                    row = v[jp:jp + 1, :]
                    tie = jnp.where(j > jp, 1.0, 0.0)
                    rank = rank + jnp.where(row > v, 1.0, jnp.where(row == v, tie, 0.0))
                vrank_ref[...] = rank
        rank = vrank_ref[...]
        sel_ref[g * N_SEL_BLOCKS:(g + 1) * N_SEL_BLOCKS, :] = jnp.where(rank < float(N_SELECT), 0.0, NEG)


def _cmp_attn(q, kcn, vcn):
    nc = np.arange(N_CMP)
    sb = np.arange(LANES)
    ov = ((CMP_STRIDE * nc[:, None] < SEL_BLOCK * sb[None, :] + SEL_BLOCK)
          & (CMP_STRIDE * nc[:, None] + CMP_BLOCK > SEL_BLOCK * sb[None, :])
          & (nc[:, None] < N_CMP - 1) & (sb[None, :] < N_SEL_BLOCKS))
    ov = jnp.asarray(ov, BF16)
    tq = ATT_TQ
    nq = SEQ // tq
    row = lambda b, i: (b * nq + i, 0)
    return pl.pallas_call(
        _cmp_kernel,
        grid=(BATCH, nq),
        in_specs=[pl.BlockSpec((tq, NSA_WIDTH), row),
                  pl.BlockSpec((1, N_CMP, 256), lambda b, i: (b, 0, 0)),
                  pl.BlockSpec((1, N_CMP, 256), lambda b, i: (b, 0, 0)),
                  pl.BlockSpec((N_CMP, LANES), lambda b, i: (0, 0))],
        out_specs=[pl.BlockSpec((tq, NSA_WIDTH), row),
                   pl.BlockSpec((N_KV * N_SEL_BLOCKS, tq), lambda b, i: (0, b * nq + i))],
        out_shape=[jax.ShapeDtypeStruct((N_TOK, NSA_WIDTH), BF16),
                   jax.ShapeDtypeStruct((N_KV * N_SEL_BLOCKS, N_TOK), F32)],
        scratch_shapes=[pltpu.VMEM((N_SEL_BLOCKS, tq), F32)],
        compiler_params=_cparams("arbitrary", "arbitrary"),
        name="cmp_attn",
    )(q, kcn, vcn, ov)


ATT_TK = 256


M_INIT = -1e29


SUM_ROWS = 16


def _selwin_kernel(q_ref, ks_ref, kw_ref, vst_ref, vwt_ref, selb_ref, osel_ref, owin_ref, m_ref, acc_ref):
    tq, tk = ATT_TQ, ATT_TK
    qi = pl.program_id(1)
    krow = lax.broadcasted_iota(I32, (tk, tq), 0)
    qcol = lax.broadcasted_iota(I32, (tk, tq), 1)
    causal_bias = jnp.where(krow <= qcol, 0.0, NEG)
    far_bias = jnp.where(qcol < krow, 0.0, NEG)

    ones_rows = jnp.ones((SUM_ROWS, tk), BF16)

    def reset():
        m_ref[...] = jnp.full(m_ref.shape, M_INIT, F32)
        acc_ref[...] = jnp.zeros(acc_ref.shape, F32)

    def update(qvars, kd, vt, bias):
        s = _dot_nt(kd, qvars)
        if bias is not None:
            s = s + jnp.concatenate([bias] * 4, axis=1)
        m_old = m_ref[...]
        m_new = jnp.maximum(m_old, jnp.max(s, axis=0, keepdims=True))
        alpha = jnp.exp2(m_old - m_new)
        p = jnp.exp2(s - m_new)
        m_ref[...] = m_new
        vte = jnp.concatenate([vt, ones_rows], axis=0)
        acc_ref[...] = alpha * acc_ref[...] + _dot(vte, p.astype(BF16))

    def finish(out_ref, g):
        o = acc_ref[:HEAD_DIM, :] / acc_ref[HEAD_DIM:HEAD_DIM + 1, :]
        for jb in range(2):
            blk = 2 * g + jb
            pair = jnp.concatenate([o[:, 2 * jb * tq:(2 * jb + 1) * tq], o[:, (2 * jb + 1) * tq:(2 * jb + 2) * tq]],
                                   axis=0)
            out_ref[:, blk * LANES:(blk + 1) * LANES] = pair.T.astype(BF16)

    def sel_bias(g, kt):
        rows = [jnp.broadcast_to(selb_ref[pl.ds(g * N_SEL_BLOCKS + kt * (tk // SEL_BLOCK) + r, 1), :],
                                 (SEL_BLOCK, tq)) for r in range(tk // SEL_BLOCK)]
        return jnp.concatenate(rows, axis=0)

    for g in range(N_KV):
        gl = slice(g * LANES, (g + 1) * LANES)
        gv = slice(g * HEAD_DIM, (g + 1) * HEAD_DIM)
        qvars = []
        for jb in range(2):
            qvars.extend(_head_variants(q_ref[:, (2 * g + jb) * LANES:(2 * g + jb + 1) * LANES]))
        qvars = jnp.concatenate(qvars, axis=0)

        def k_tile(ref, kt):
            return ref[0, pl.ds(pl.multiple_of(kt * tk, tk), tk), gl]

        def v_tile(ref, kt):
            return ref[gv, pl.ds(pl.multiple_of(kt * tk, tk), tk)]

        reset()

        def sel_step(kt, carry):
            update(qvars, k_tile(ks_ref, kt), v_tile(vst_ref, kt), sel_bias(g, kt))
            return carry

        lax.fori_loop(0, qi, sel_step, 0)
        update(qvars, k_tile(ks_ref, qi), v_tile(vst_ref, qi), sel_bias(g, qi) + causal_bias)
        finish(osel_ref, g)

        reset()

        @pl.when(qi >= 2)
        def _():
            update(qvars, k_tile(kw_ref, qi - 2), v_tile(vwt_ref, qi - 2), far_bias)

        @pl.when(qi >= 1)
        def _():
            update(qvars, k_tile(kw_ref, qi - 1), v_tile(vwt_ref, qi - 1), None)

        update(qvars, k_tile(kw_ref, qi), v_tile(vwt_ref, qi), causal_bias)
        finish(owin_ref, g)


def _selwin(q, ks, kw, vst, vwt, selb):
    tq = ATT_TQ
    nq = SEQ // tq
    assert WINDOW == 2 * ATT_TK and ATT_TQ == ATT_TK
    row = lambda b, i: (b * nq + i, 0)
    keys = pl.BlockSpec((1, SEQ, 256), lambda b, i: (b, 0, 0))
    vals = pl.BlockSpec((LANES, SEQ), lambda b, i: (0, b))
    r3 = lambda a: a.reshape(BATCH, SEQ, 256)
    return pl.pallas_call(
        _selwin_kernel,
        grid=(BATCH, nq),
        in_specs=[pl.BlockSpec((tq, NSA_WIDTH), row), keys, keys, vals, vals,
                  pl.BlockSpec((N_KV * N_SEL_BLOCKS, tq), lambda b, i: (0, b * nq + i))],
        out_specs=[pl.BlockSpec((tq, NSA_WIDTH), row)] * 2,
        out_shape=[jax.ShapeDtypeStruct((N_TOK, NSA_WIDTH), BF16)] * 2,
        scratch_shapes=[pltpu.VMEM((1, 4 * tq), F32), pltpu.VMEM((HEAD_DIM + SUM_ROWS, 4 * tq), F32)],
        compiler_params=_cparams("arbitrary", "arbitrary"),
        name="selwin",
    )(q, r3(ks), r3(kw), vst, vwt, selb)


def _s5_param_kernel(are_ref, aim_ref, ldt_ref, cre_ref, cim_ref, bre_ref, bim_ref,
                     clre_ref, clim_ref, wbre_ref, wbim_ref, bbre_ref, bbim_ref, ltre_ref, ltim_ref):
    are, aim = are_ref[...], aim_ref[...]
    dt = jnp.exp(ldt_ref[...])
    cre, cim = cre_ref[...], cim_ref[...]

    def lam_pow(tau):
        mag = jnp.exp(are * dt * float(tau))
        ang = aim * dt * float(tau)
        return mag * jnp.cos(ang), mag * jnp.sin(ang)

    lre, lim = lam_pow(1)
    den = are * are + aim * aim
    qre = ((lre - 1.0) * are + lim * aim) / den
    qim = (lim * are - (lre - 1.0) * aim) / den
    bre, bim = bre_ref[...], bim_ref[...]
    bbre = qre * bre - qim * bim
    bbim = qre * bim + qim * bre
    bbre_ref[...] = bbre
    bbim_ref[...] = bbim
    for tau in range(S5_T + 1):
        pr, pi = lam_pow(tau)
        clre_ref[tau] = cre * pr - cim * pi
        clim_ref[tau] = cre * pi + cim * pr
        if tau < S5_T:
            k = S5_T - 1 - tau
            wbre_ref[k] = pr * bbre - pi * bbim
            wbim_ref[k] = pr * bbim + pi * bbre
        else:
            ltre_ref[...] = pr
            ltim_ref[...] = pi


def _s5_kmat_kernel(l_ref, r_ref, o_ref):
    o_ref[0] = jnp.dot(l_ref[0], r_ref[0], preferred_element_type=F32, precision=HIGHEST)


def _s5_params(a_re, a_im, log_dt, b_re, b_im, c_re, c_im):
    T = S5_T
    pn = GROUP * STATE
    tile_p = lambda a: jnp.tile(a, (1, GROUP))
    args = (tile_p(a_re), tile_p(a_im), jnp.broadcast_to(log_dt[:, None], (N_GROUPS, pn)),
            c_re.reshape(N_GROUPS, pn), c_im.reshape(N_GROUPS, pn),
            jnp.swapaxes(b_re, 1, 2).reshape(N_GROUPS, pn), jnp.swapaxes(b_im, 1, 2).reshape(N_GROUPS, pn))
    full2 = pl.BlockSpec((N_GROUPS, pn), lambda: (0, 0))
    clre, clim, wbre, wbim, bbre, bbim, ltre, ltim = pl.pallas_call(
        _s5_param_kernel,
        in_specs=[full2] * 7,
        out_specs=[pl.BlockSpec((T + 1, N_GROUPS, pn), lambda: (0, 0, 0))] * 2
                  + [pl.BlockSpec((T, N_GROUPS, pn), lambda: (0, 0, 0))] * 2 + [full2] * 4,
        out_shape=[jax.ShapeDtypeStruct((T + 1, N_GROUPS, pn), F32)] * 2
                  + [jax.ShapeDtypeStruct((T, N_GROUPS, pn), F32)] * 2
                  + [jax.ShapeDtypeStruct((N_GROUPS, pn), F32)] * 4,
        name="s5_params",
    )(*args)

    r5 = lambda a, t: a[:t].reshape(t, N_GROUPS, GROUP, STATE)
    lhs = jnp.concatenate([r5(clre, T), -r5(clim, T)], axis=-1)
    lhs = jnp.transpose(lhs, (1, 0, 2, 3)).reshape(N_GROUPS, T * GROUP, 2 * STATE)
    bb = lambda a: jnp.swapaxes(a.reshape(N_GROUPS, GROUP, STATE), 1, 2)
    rhs = jnp.concatenate([bb(bbre), bb(bbim)], axis=1)
    kmat = pl.pallas_call(
        _s5_kmat_kernel,
        grid=(N_GROUPS,),
        in_specs=[pl.BlockSpec((1, T * GROUP, 2 * STATE), lambda g: (g, 0, 0)),
                  pl.BlockSpec((1, 2 * STATE, GROUP), lambda g: (g, 0, 0))],
        out_specs=pl.BlockSpec((1, T * GROUP, GROUP), lambda g: (g, 0, 0)),
        out_shape=jax.ShapeDtypeStruct((N_GROUPS, T * GROUP, GROUP), F32),
        compiler_params=_cparams("arbitrary"),
        name="s5_kmat",
    )(lhs, rhs)

    eye = jnp.eye(S5_GL, dtype=F32)
    kt = kmat.reshape(S5_SG, S5_GL, T, GROUP, GROUP)
    kbd = jnp.einsum('sgtpq,gh->stgqhp', kt, eye).reshape(S5_SG, T, LANES, LANES)
    krev = kbd[:, ::-1].reshape(S5_SG, T * LANES, LANES).astype(BF16)
    r6 = lambda a: a.reshape(T, S5_SG, S5_GL, GROUP, STATE)
    wb = jnp.stack([r6(wbre), r6(wbim)], axis=-2)
    wb = jnp.einsum('ksgpin,gh->skgpihn', wb, eye).reshape(S5_SG, T * LANES, S5_NSTATE).astype(BF16)
    wc = jnp.stack([r6(clre[1:]), -r6(clim[1:])], axis=-2)
    wc = jnp.einsum('tsgpin,gh->signthp', wc, eye).reshape(S5_SG, S5_NSTATE, T * LANES).astype(BF16)
    lt = lambda a: a.reshape(N_GROUPS, GROUP, STATE)[:, 0].reshape(S5_SG, 1, S5_GL * STATE)
    return krev, wb, wc, lt(ltre), lt(ltim)


S5_TC = 512


def _s5_state_kernel(x_ref, wb_ref, e_ref):
    e_ref[0] = _dot(x_ref[0], wb_ref[0])


def _s5_scan_kernel(e_ref, ltre_ref, ltim_ref, xs_ref):
    lr, li = ltre_ref[0], ltim_ref[0]
    half = S5_NSTATE // 2

    def step(c, carry):
        new = []
        for b in range(BATCH):
            xr, xi = carry[b]
            row = b * S5_CH_PER_BATCH + c
            xs_ref[0, pl.ds(row, 1), :half] = xr
            xs_ref[0, pl.ds(row, 1), half:] = xi
            e = e_ref[0, pl.ds(row, 1), :]
            new.append((lr * xr - li * xi + e[:, :half], lr * xi + li * xr + e[:, half:]))
        return tuple(new)

    zero = jnp.zeros((1, half), F32)
    lax.fori_loop(0, S5_CH_PER_BATCH, step, tuple((zero, zero) for _ in range(BATCH)))


def _s5_out_kernel(x_ref, xs_ref, krev_ref, wc_ref, y_ref):
    xsb = xs_ref[0].astype(BF16)
    for t in range(S5_T):
        cols = slice(t * LANES, (t + 1) * LANES)
        y_ref[0, :, cols] = (_dot(x_ref[0, :, :(t + 1) * LANES], krev_ref[0, (S5_T - 1 - t) * LANES:, :])
                             + _dot(xsb, wc_ref[0, :, cols]))


def _s5(u, krev, wb, wc, ltre, ltim):
    T, tc = S5_T, S5_TC
    xs = u.astype(BF16).reshape(S5_CH, T, S5_SG, LANES).transpose(2, 0, 1, 3).reshape(S5_SG, S5_CH, T * LANES)
    grid = (S5_SG, S5_CH // tc)
    rows = lambda s, i: (s, i, 0)
    per_sg = lambda s, i: (s, 0, 0)
    e = pl.pallas_call(
        _s5_state_kernel, grid=grid,
        in_specs=[pl.BlockSpec((1, tc, T * LANES), rows), pl.BlockSpec((1, T * LANES, S5_NSTATE), per_sg)],
        out_specs=pl.BlockSpec((1, tc, S5_NSTATE), rows),
        out_shape=jax.ShapeDtypeStruct((S5_SG, S5_CH, S5_NSTATE), F32),
        compiler_params=_cparams("arbitrary", "arbitrary"), name="s5_state",
    )(xs, wb)
    sg1 = lambda s: (s, 0, 0)
    xstart = pl.pallas_call(
        _s5_scan_kernel, grid=(S5_SG,),
        in_specs=[pl.BlockSpec((1, S5_CH, S5_NSTATE), sg1),
                  pl.BlockSpec((1, 1, S5_NSTATE // 2), sg1), pl.BlockSpec((1, 1, S5_NSTATE // 2), sg1)],
        out_specs=pl.BlockSpec((1, S5_CH, S5_NSTATE), sg1),
        out_shape=jax.ShapeDtypeStruct((S5_SG, S5_CH, S5_NSTATE), F32),
        compiler_params=_cparams("arbitrary"), name="s5_scan",
    )(e, ltre, ltim)
    y = pl.pallas_call(
        _s5_out_kernel, grid=grid,
        in_specs=[pl.BlockSpec((1, tc, T * LANES), rows), pl.BlockSpec((1, tc, S5_NSTATE), rows),
                  pl.BlockSpec((1, T * LANES, LANES), per_sg), pl.BlockSpec((1, S5_NSTATE, T * LANES), per_sg)],
        out_specs=pl.BlockSpec((1, tc, T * LANES), rows),
        out_shape=jax.ShapeDtypeStruct((S5_SG, S5_CH, T * LANES), F32),
        compiler_params=_cparams("arbitrary", "arbitrary"), name="s5_out",
    )(xs, xstart, krev, wc)
    return y.reshape(S5_SG, S5_CH, T, LANES).transpose(1, 2, 0, 3).reshape(N_TOK, SSM_WIDTH)


MERGE_TM = 256


def _merge_kernel(ocmp_ref, osel_ref, owin_ref, gn_ref, yssm_ref, u_ref, ga_ref, gs_ref, x_ref, mod_ref,
                  eg_ref, dskip_ref, wglu_ref, bglu_ref, wua_ref, wus_ref, wout_ref, g2_ref,
                  wrhi_ref, wrlo_ref, wsgu_ref, wsd_ref,
                  xpart_ref, h2_ref, logit_ref):
    mod = mod_ref[0]
    gnb = gn_ref[...].astype(BF16)
    o_nsa = (_dot(gnb, eg_ref[0]) * ocmp_ref[...].astype(F32)
             + _dot(gnb, eg_ref[1]) * osel_ref[...].astype(F32)
             + _dot(gnb, eg_ref[2]) * owin_ref[...].astype(F32))
    attn = _dot(o_nsa.astype(BF16), wua_ref[...])
    z = _gelu(yssm_ref[...] + dskip_ref[...] * u_ref[...])
    y_ssm = z * jax.nn.sigmoid(_dot(z.astype(BF16), wglu_ref[...]) + bglu_ref[...])
    ssm = _dot(y_ssm.astype(BF16), wus_ref[...])
    merged = ga_ref[...].astype(F32) * attn + gs_ref[...].astype(F32) * ssm
    x1 = x_ref[...] + mod[2:3] * _dot(merged.astype(BF16), wout_ref[...])

    ms = jnp.mean(x1 * x1, axis=-1, keepdims=True)
    h2 = (x1 * lax.rsqrt(ms + EPS) * g2_ref[...]) * (1.0 + mod[4:5]) + mod[3:4]
    hi = h2.astype(BF16)
    lo = (h2 - hi.astype(F32)).astype(BF16)
    h2_ref[...] = _pack_bf16_pairs(h2)
    logit_ref[...] = _dot_nt(wrhi_ref[...], hi) + _dot_nt(wrhi_ref[...], lo) + _dot_nt(wrlo_ref[...], hi)
    gu = _dot(hi, wsgu_ref[...])
    shared = _dot((_silu(gu[:, :D_EXPERT]) * gu[:, D_EXPERT:]).astype(BF16), wsd_ref[...])
    xpart_ref[...] = x1 + mod[5:6] * shared


def _merge(ocmp, osel, owin, gn, yssm, u, ga, gs, x2, mod, d_skip, w_glu, b_glu, w_up_attn, w_up_ssm, w_out,
           g_norm2, w_router, ws_gate, ws_up, ws_down):
    tm = MERGE_TM
    eg = np.zeros((3, LANES, NSA_WIDTH), np.float32)
    for j in range(3):
        for h in range(N_HEADS):
            eg[j, 3 * h + j, h * HEAD_DIM:(h + 1) * HEAD_DIM] = 1.0
    wr_t = w_router.T
    wr_hi = wr_t.astype(BF16)
    wr_lo = (wr_t - wr_hi.astype(F32)).astype(BF16)
    row = lambda i: (i, 0)
    fix2 = lambda i: (0, 0)
    wspec = lambda a: pl.BlockSpec(a.shape, (lambda i: (0,) * a.ndim))
    weights = [jnp.asarray(eg, BF16), d_skip.reshape(1, -1), w_glu.astype(BF16), b_glu.reshape(1, -1),
               w_up_attn.astype(BF16), w_up_ssm.astype(BF16), w_out.astype(BF16), g_norm2.reshape(1, -1),
               wr_hi, wr_lo, jnp.concatenate([ws_gate, ws_up], axis=1).astype(BF16), ws_down.astype(BF16)]
    acts = [(ocmp, 512), (osel, 512), (owin, 512), (gn, 128), (yssm, 512), (u, 512), (ga, 1024), (gs, 1024),
            (x2, 1024)]
    return pl.pallas_call(
        _merge_kernel,
        grid=(N_TOK // tm,),
        in_specs=[pl.BlockSpec((tm, wd), row) for _, wd in acts]
                 + [pl.BlockSpec((1, 6, D_MODEL), lambda i: (i // (SEQ // tm), 0, 0))]
                 + [wspec(w) for w in weights],
        out_specs=[pl.BlockSpec((tm, D_MODEL), row), pl.BlockSpec((tm, HALF), row),
                   pl.BlockSpec((N_EXPERTS, tm), lambda i: (0, i))],
        out_shape=[jax.ShapeDtypeStruct((N_TOK, D_MODEL), F32), jax.ShapeDtypeStruct((N_TOK, HALF), jnp.uint32),
                   jax.ShapeDtypeStruct((N_EXPERTS, N_TOK), F32)],
        compiler_params=_cparams("arbitrary"),
        name="merge",
    )(*[a for a, _ in acts], mod, *weights)


ROUTE_TN = 512


def _route_kernel(logit_ref, bias_ref, eidx_ref, w_ref, count_ref, gscore_ref, masked_ref):
    tn = ROUTE_TN

    @pl.when(pl.program_id(0) == 0)
    def _():
        count_ref[...] = jnp.zeros(count_ref.shape, F32)

    sc = jax.nn.sigmoid(logit_ref[...])
    biased = sc + bias_ref[...]
    gi = lax.broadcasted_iota(I32, (EXPERTS_PER_GROUP, tn), 0).astype(F32)
    for g in range(N_EXPERT_GROUPS):
        blk = biased[g * EXPERTS_PER_GROUP:(g + 1) * EXPERTS_PER_GROUP]
        m1 = jnp.max(blk, axis=0, keepdims=True)
        i1 = jnp.min(jnp.where(blk == m1, gi, float(EXPERTS_PER_GROUP)), axis=0, keepdims=True)
        m2 = jnp.max(jnp.where(gi == i1, -jnp.inf, blk), axis=0, keepdims=True)
        gscore_ref[g:g + 1, :] = m1 + m2
    gs = gscore_ref[...]
    gidx = lax.broadcasted_iota(I32, (N_EXPERT_GROUPS, tn), 0)
    grank = jnp.zeros((N_EXPERT_GROUPS, tn), F32)
    for gp in range(N_EXPERT_GROUPS):
        row = gs[gp:gp + 1, :]
        tie = jnp.where(gidx > gp, 1.0, 0.0)
        grank = grank + jnp.where(row > gs, 1.0, jnp.where(row == gs, tie, 0.0))
    for g in range(N_EXPERT_GROUPS):
        keep = grank[g:g + 1, :] < float(TOPK_GROUPS)
        sl = slice(g * EXPERTS_PER_GROUP, (g + 1) * EXPERTS_PER_GROUP)
        masked_ref[sl, :] = jnp.where(keep, biased[sl], -jnp.inf)
    cur = masked_ref[...]
    eidx = lax.broadcasted_iota(I32, (N_EXPERTS, tn), 0).astype(F32)
    wsum = jnp.zeros((1, tn), F32)
    hits = jnp.zeros((N_EXPERTS, tn), F32)
    for k in range(TOP_K):
        m = jnp.max(cur, axis=0, keepdims=True)
        idx = jnp.min(jnp.where(cur == m, eidx, float(N_EXPERTS)), axis=0, keepdims=True)
        hit = eidx == idx
        wk = jnp.sum(jnp.where(hit, sc, 0.0), axis=0, keepdims=True)
        cur = jnp.where(hit, -jnp.inf, cur)
        hits = hits + jnp.where(hit, 1.0, 0.0)
        eidx_ref[k:k + 1, :] = idx.astype(I32)
        w_ref[k:k + 1, :] = wk
        wsum = wsum + wk
    w_ref[...] = w_ref[...] / wsum * ROUTE_SCALE
    count_ref[...] = count_ref[...] + jnp.sum(hits, axis=1, keepdims=True)


def _route(logits_t, router_bias):
    tn = ROUTE_TN
    return pl.pallas_call(
        _route_kernel,
        grid=(N_TOK // tn,),
        in_specs=[pl.BlockSpec((N_EXPERTS, tn), lambda i: (0, i)), pl.BlockSpec((N_EXPERTS, 1), lambda i: (0, 0))],
        out_specs=[pl.BlockSpec((TOP_K, tn), lambda i: (0, i))] * 2 + [pl.BlockSpec((N_EXPERTS, 1), lambda i: (0, 0))],
        out_shape=[jax.ShapeDtypeStruct((TOP_K, N_TOK), I32), jax.ShapeDtypeStruct((TOP_K, N_TOK), F32),
                   jax.ShapeDtypeStruct((N_EXPERTS, 1), F32)],
        scratch_shapes=[pltpu.VMEM((N_EXPERT_GROUPS, tn), F32), pltpu.VMEM((N_EXPERTS, tn), F32)],
        compiler_params=_cparams("arbitrary"),
        name="route",
    )(logits_t, router_bias.reshape(-1, 1))


N_MOE_BLK = NK // DISPATCH_BLOCK
N_ITEMS = N_MOE_BLK + N_EXPERTS
ASSIGN_BITS = 17


def _dispatch_plan(eidx, counts):
    e_flat = eidx.reshape(-1)
    key = jnp.sort(e_flat * NK + jnp.arange(NK, dtype=I32))
    order = key & (NK - 1)
    counts = counts.reshape(-1).astype(I32)
    start = jnp.cumsum(counts) - counts
    cuts = jnp.sort(jnp.concatenate([jnp.arange(N_MOE_BLK, dtype=I32) * DISPATCH_BLOCK, start]))
    lo = cuts
    hi = jnp.concatenate([cuts[1:], jnp.full((1,), NK, I32)])
    blk = jnp.minimum(lo // DISPATCH_BLOCK, N_MOE_BLK - 1)
    expert = jnp.clip(jnp.sum((start[None, :] <= lo[:, None]).astype(I32), axis=1) - 1, 0, N_EXPERTS - 1)
    one = jnp.ones((1,), I32)
    first = jnp.concatenate([one, (blk[1:] != blk[:-1]).astype(I32)])
    last = jnp.concatenate([(blk[1:] != blk[:-1]).astype(I32), one])
    new_expert = jnp.concatenate([one, (expert[1:] != expert[:-1]).astype(I32)])
    run_id = jnp.cumsum(new_expert) - 1
    n_runs = run_id[-1] + 1
    item = jnp.arange(N_ITEMS, dtype=I32)
    run_first_item = jnp.sort(jnp.where(new_expert == 1, item, N_ITEMS))
    run_expert = expert[jnp.minimum(run_first_item, N_ITEMS - 1)]
    ahead = run_id + (WEIGHT_RING - 1)
    ahead_expert = run_expert[jnp.minimum(ahead, N_ITEMS - 1)]
    ahead_valid = (ahead < n_runs).astype(I32)
    second_expert = run_expert[1:2]
    prologue = jnp.concatenate([second_expert, (n_runs > 1).astype(I32).reshape(1)])
    tok = jnp.right_shift(order, 3)
    home = (order & (TOP_K - 1)) * N_TOK + tok
    return tok, home, (blk, expert, lo - blk * DISPATCH_BLOCK, hi - blk * DISPATCH_BLOCK, first, last, new_expert,
                      run_id % WEIGHT_RING, ahead_expert, ahead_valid, prologue)


SC_CORES = 2
SC_SUBCORES = 16
SC_CHUNK = 128


def _sc_move_rows(table, idx, scatter):
    n = idx.shape[0]
    workers = SC_CORES * SC_SUBCORES
    per_worker = n // workers
    n_chunks = per_worker // SC_CHUNK
    assert per_worker * workers == n and n_chunks * SC_CHUNK == per_worker
    mesh = plsc.VectorSubcoreMesh(core_axis_name="c", subcore_axis_name="s",
                                  num_cores=SC_CORES, num_subcores=SC_SUBCORES)

    def body(table_hbm, idx_hbm, out_hbm, idx_v, rows_v, sem):
        wid = lax.axis_index("s") * SC_CORES + lax.axis_index("c")
        base = wid * per_worker

        @pl.loop(0, n_chunks)
        def _(j):
            off = base + j * SC_CHUNK
            pltpu.sync_copy(idx_hbm.at[pl.ds(off, SC_CHUNK)], idx_v)
            if scatter:
                pltpu.sync_copy(table_hbm.at[pl.ds(off, SC_CHUNK)], rows_v)
                pltpu.async_copy(rows_v, out_hbm.at[idx_v], sem).wait()
            else:
                pltpu.async_copy(table_hbm.at[idx_v], rows_v, sem).wait()
                pltpu.sync_copy(rows_v, out_hbm.at[pl.ds(off, SC_CHUNK)])

    return pl.kernel(
        body,
        out_type=jax.ShapeDtypeStruct((n, table.shape[1]), table.dtype),
        mesh=mesh,
        scratch_types=[pltpu.VMEM((SC_CHUNK,), I32), pltpu.VMEM((SC_CHUNK, table.shape[1]), table.dtype),
                       pltpu.SemaphoreType.DMA],
        name="sc_scatter_rows" if scatter else "sc_gather_rows",
    )(table, idx)


WEIGHT_RING = 3
WEIGHT_CHUNKS = 4


def _expert_weight_copies(w_hbm, wbuf, sem, expert, slot):
    rows = w_hbm.shape[1] // WEIGHT_CHUNKS
    return [pltpu.make_async_copy(w_hbm.at[expert, pl.ds(c * rows, rows)],
                                  wbuf.at[slot, pl.ds(c * rows, rows)], sem.at[slot])
            for c in range(WEIGHT_CHUNKS)]


def _moe_kernel(blk_ref, exp_ref, lo_ref, hi_ref, first_ref, last_ref, newexp_ref,
                slot_ref, ahead_exp_ref, ahead_ok_ref, prologue_ref,
                x_ref, wg_hbm, wu_hbm, wd_hbm, y_ref,
                acc_ref, wgf_ref, wuf_ref, wdf_ref, wgb_ref, wub_ref, wdb_ref, wsem):
    it = pl.program_id(0)
    lo, hi = lo_ref[it], hi_ref[it]
    streams = ((wg_hbm, wgf_ref), (wu_hbm, wuf_ref), (wd_hbm, wdf_ref))

    def request(expert, slot):
        for w_hbm, wbuf in streams:
            for cp in _expert_weight_copies(w_hbm, wbuf, wsem, expert, slot):
                cp.start()

    @pl.when(it == 0)
    def _():
        request(exp_ref[0], 0)

        @pl.when(prologue_ref[1] == 1)
        def _():
            request(prologue_ref[0], 1)

    @pl.when(newexp_ref[it] == 1)
    def _():
        slot = slot_ref[it]
        for w_hbm, wbuf in streams:
            for cp in _expert_weight_copies(w_hbm, wbuf, wsem, 0, slot):
                cp.wait()
        wgb_ref[...] = wgf_ref[slot].astype(BF16)
        wub_ref[...] = wuf_ref[slot].astype(BF16)
        wdb_ref[...] = wdf_ref[slot].astype(BF16)

        @pl.when(ahead_ok_ref[it] == 1)
        def _():
            ahead_slot = slot + (WEIGHT_RING - 1)
            request(ahead_exp_ref[it], jnp.where(ahead_slot >= WEIGHT_RING, ahead_slot - WEIGHT_RING, ahead_slot))

    @pl.when(first_ref[it] == 1)
    def _():
        acc_ref[...] = jnp.zeros(acc_ref.shape, F32)

    def expert_pass(r0, nrows):
        rows = slice(r0, r0 + nrows)
        ridx = r0 + lax.broadcasted_iota(I32, (nrows, HALF), 0)
        mine = (ridx >= lo) & (ridx < hi)
        xlo, xhi = _unpack_bf16_pairs(jnp.where(mine, x_ref[rows, :], jnp.uint32(0)))
        xlo, xhi = xlo.astype(BF16), xhi.astype(BF16)
        gate = _dot(xlo, wgb_ref[:HALF]) + _dot(xhi, wgb_ref[HALF:])
        up = _dot(xlo, wub_ref[:HALF]) + _dot(xhi, wub_ref[HALF:])
        acc_ref[rows, :] = acc_ref[rows, :] + _dot((_silu(gate) * up).astype(BF16), wdb_ref[...])

    mid = DISPATCH_BLOCK // 2
    pl.when((lo < mid) & (hi > mid))(lambda: expert_pass(0, DISPATCH_BLOCK))
    pl.when((hi > lo) & (hi <= mid))(lambda: expert_pass(0, mid))
    pl.when((hi > lo) & (lo >= mid))(lambda: expert_pass(mid, mid))

    @pl.when(last_ref[it] == 1)
    def _():
        y_ref[...] = _pack_bf16_pairs(acc_ref[...])


def _moe(xs, items, w_gate, w_up, w_down):
    by_blk = lambda it, blk, *_: (blk[it], 0)
    any_space = pl.BlockSpec(memory_space=pl.ANY)
    grid_spec = pltpu.PrefetchScalarGridSpec(
        num_scalar_prefetch=len(items),
        grid=(N_ITEMS,),
        in_specs=[pl.BlockSpec((DISPATCH_BLOCK, HALF), by_blk), any_space, any_space, any_space],
        out_specs=pl.BlockSpec((DISPATCH_BLOCK, HALF), by_blk),
        scratch_shapes=[pltpu.VMEM((DISPATCH_BLOCK, D_MODEL), F32),
                        pltpu.VMEM((WEIGHT_RING, D_MODEL, D_EXPERT), F32),
                        pltpu.VMEM((WEIGHT_RING, D_MODEL, D_EXPERT), F32),
                        pltpu.VMEM((WEIGHT_RING, D_EXPERT, D_MODEL), F32),
                        pltpu.VMEM((D_MODEL, D_EXPERT), BF16), pltpu.VMEM((D_MODEL, D_EXPERT), BF16),
                        pltpu.VMEM((D_EXPERT, D_MODEL), BF16),
                        pltpu.SemaphoreType.DMA((WEIGHT_RING,))],
    )
    return pl.pallas_call(
        _moe_kernel,
        grid_spec=grid_spec,
        out_shape=jax.ShapeDtypeStruct((NK, HALF), jnp.uint32),
        compiler_params=_cparams("arbitrary"),
        name="moe",
    )(*items, xs, w_gate, w_up, w_down)


COMB_TC = 64


def _combine_kernel(slots_ref, w_ref, xpart_ref, mod_ref, out_ref):
    w = w_ref[...]
    lo = jnp.zeros((w.shape[0], HALF), F32)
    hi = jnp.zeros((w.shape[0], HALF), F32)
    for k in range(TOP_K):
        klo, khi = _unpack_bf16_pairs(slots_ref[k])
        lo = lo + w[:, k:k + 1] * klo
        hi = hi + w[:, k:k + 1] * khi
    gate2 = mod_ref[0][5:6]
    out_ref[:, :HALF] = xpart_ref[:, :HALF] + gate2[:, :HALF] * lo
    out_ref[:, HALF:] = xpart_ref[:, HALF:] + gate2[:, HALF:] * hi


def _combine(xpart, mod, slots, w):
    tc = COMB_TC
    row = lambda i: (i, 0)
    return pl.pallas_call(
        _combine_kernel,
        grid=(N_TOK // tc,),
        in_specs=[pl.BlockSpec((TOP_K, tc, HALF), lambda i: (0, i, 0)),
                  pl.BlockSpec((tc, TOP_K), row),
                  pl.BlockSpec((tc, D_MODEL), row),
                  pl.BlockSpec((1, 6, D_MODEL), lambda i: (i // (SEQ // tc), 0, 0))],
        out_specs=pl.BlockSpec((tc, D_MODEL), row),
        out_shape=jax.ShapeDtypeStruct((N_TOK, D_MODEL), F32),
        compiler_params=_cparams("arbitrary"),
        name="combine",
    )(slots.reshape(TOP_K, N_TOK, HALF), w, xpart, mod)


def _layer(x, c, w_ada, b_ada, g_norm1, g_norm2, w_in, q_gain, kc_gain, ks_gain, kw_gain,
           pe_k, pe_v, w_cmp_k1, w_cmp_k2, w_cmp_v1, w_cmp_v2,
           a_re, a_im, log_dt, b_re, b_im, c_re, c_im, d_skip, w_glu, b_glu,
           w_up_attn, w_up_ssm, w_out, w_router, router_bias,
           w_gate, w_up, w_down, ws_gate, ws_up, ws_down):
    x2 = x.reshape(N_TOK, D_MODEL)
    mod = _ada(c, w_ada, b_ada)
    q, kc_raw, vc_raw, ks, kw, vst, vwt, gn, u, ga, gs = _proj(x2, mod, g_norm1, w_in, q_gain, ks_gain, kw_gain)
    kcn = _compress(kc_raw, pe_k, w_cmp_k1, w_cmp_k2, kc_gain, True)
    vcn = _compress(vc_raw, pe_v, w_cmp_v1, w_cmp_v2, kc_gain, False)
    ocmp, selb = _cmp_attn(q, kcn, vcn)
    osel, owin = _selwin(q, ks, kw, vst, vwt, selb)
    yssm = _s5(u, *_s5_params(a_re, a_im, log_dt, b_re, b_im, c_re, c_im))
    xpart, h2, logits_t = _merge(ocmp, osel, owin, gn, yssm, u, ga, gs, x2, mod, d_skip, w_glu, b_glu,
                                  w_up_attn, w_up_ssm, w_out, g_norm2, w_router, ws_gate, ws_up, ws_down)
    eidx_t, w_t, counts = _route(logits_t, router_bias)
    tok, home, items = _dispatch_plan(eidx_t.T, counts)
    y = _moe(_sc_move_rows(h2, tok, scatter=False), items, w_gate, w_up, w_down)
    slots = _sc_move_rows(y, home, scatter=True)
    return _combine(xpart, mod, slots, w_t.T).reshape(BATCH, SEQ, D_MODEL)


def kernel(x, c, w_ada, b_ada, g_norm1, g_norm2, w_in, q_gain, kc_gain, ks_gain, kw_gain, pe_k, pe_v, w_cmp_k1,
           w_cmp_k2, w_cmp_v1, w_cmp_v2, a_re, a_im, log_dt, b_re, b_im, c_re, c_im, d_skip, w_glu, b_glu,
           w_up_attn, w_up_ssm, w_out, w_router, router_bias, w_gate, w_up, w_down, ws_gate, ws_up, ws_down):
    params = (w_ada, b_ada, g_norm1, g_norm2, w_in, q_gain, kc_gain, ks_gain, kw_gain, pe_k, pe_v, w_cmp_k1,
              w_cmp_k2, w_cmp_v1, w_cmp_v2, a_re, a_im, log_dt, b_re, b_im, c_re, c_im, d_skip, w_glu, b_glu,
              w_up_attn, w_up_ssm, w_out, w_router, router_bias, w_gate, w_up, w_down, ws_gate, ws_up, ws_down)
    depth = w_ada.shape[0]
    for layer in range(depth):
        x = _layer(x, c, *[p[layer] for p in params])
    return x
```

```python
import functools
import math

import jax
import jax.numpy as jnp
import numpy as np
from jax import lax
from jax.experimental import pallas as pl
from jax.experimental.pallas import tpu as pltpu
from jax.experimental.pallas import tpu_sc as plsc

F32 = jnp.float32
BF16 = jnp.bfloat16
I32 = jnp.int32
HIGHEST = lax.Precision.HIGHEST

D_MODEL = 1024
BATCH = 4
SEQ = 4096
N_TOK = BATCH * SEQ
N_HEADS = 8
HEAD_DIM = 64
N_KV = 2
CMP_BLOCK = 32
CMP_STRIDE = 16
CMP_HIDDEN = 256
N_CMP = 256
SEL_BLOCK = 64
N_SEL_BLOCKS = SEQ // SEL_BLOCK
N_SELECT = 16
WINDOW = 512
ATTN_SCALE = HEAD_DIM ** -0.5
LOG2E = 1.4426950408889634
NSA_WIDTH = N_HEADS * HEAD_DIM
SSM_WIDTH = 512
GROUP = 16
N_GROUPS = SSM_WIDTH // GROUP
STATE = 64
N_EXPERTS = 256
TOP_K = 8
D_EXPERT = 256
N_EXPERT_GROUPS = 8
EXPERTS_PER_GROUP = N_EXPERTS // N_EXPERT_GROUPS
TOPK_GROUPS = 4
ROUTE_SCALE = 2.5
DISPATCH_BLOCK = 256
EPS = 1e-6
NEG = -1e30

LANES = 128
S5_T = 16
S5_SG = 4
S5_GL = N_GROUPS // S5_SG
S5_CH = N_TOK // S5_T
S5_CH_PER_BATCH = SEQ // S5_T
S5_NSTATE = S5_GL * STATE * 2

NK = N_TOK * TOP_K
HALF = D_MODEL // 2

VMEM_LIMIT = 48 * 1024 * 1024


def _cparams(*sem):
    return pltpu.CompilerParams(dimension_semantics=tuple(sem), vmem_limit_bytes=VMEM_LIMIT)


def _dot(a, b):
    return jnp.dot(a, b, preferred_element_type=F32)


def _dot_nt(a, b):
    return lax.dot_general(a, b, (((1,), (1,)), ((), ())), preferred_element_type=F32)


def _split_dot(v, w):
    hi = v.astype(BF16)
    lo = (v - hi.astype(F32)).astype(BF16)
    return _dot(hi, w) + _dot(lo, w)


def _seg_rms(v, bd, gain):
    ss = _split_dot(v * v, bd)
    return v * lax.rsqrt(ss * (1.0 / HEAD_DIM) + EPS) * gain


def _gelu(x):
    return 0.5 * x * (1.0 + jnp.tanh(0.7978845608028654 * (x + 0.044715 * (x * x * x))))


def _silu(x):
    return x * jax.nn.sigmoid(x)


def _pack_bf16_pairs(v):
    bits = lax.bitcast_convert_type(v.astype(BF16).astype(F32), jnp.uint32)
    h = v.shape[1] // 2
    return bits[:, h:] | lax.shift_right_logical(bits[:, :h], jnp.uint32(16))


def _unpack_bf16_pairs(word):
    lo = lax.bitcast_convert_type(lax.shift_left(word, jnp.uint32(16)), F32)
    hi = lax.bitcast_convert_type(word & jnp.uint32(0xFFFF0000), F32)
    return lo, hi


def _ada_kernel(c_ref, w_ref, b_ref, o_ref):
    c = c_ref[...]
    o_ref[...] = jnp.dot(_silu(c), w_ref[...], preferred_element_type=F32, precision=HIGHEST) + b_ref[...]


def _ada(c, w_ada, b_ada):
    cp = jnp.pad(c, ((0, 8 - BATCH), (0, 0)))
    tn = 1536
    out = pl.pallas_call(
        _ada_kernel,
        grid=(6 * D_MODEL // tn,),
        in_specs=[pl.BlockSpec((8, D_MODEL), lambda j: (0, 0)),
                  pl.BlockSpec((D_MODEL, tn), lambda j: (0, j)),
                  pl.BlockSpec((1, tn), lambda j: (0, j))],
        out_specs=pl.BlockSpec((8, tn), lambda j: (0, j)),
        out_shape=jax.ShapeDtypeStruct((8, 6 * D_MODEL), F32),
        compiler_params=_cparams("arbitrary"),
        name="ada",
    )(cp, w_ada, b_ada.reshape(1, -1))
    return out.reshape(8, 6, D_MODEL)


_C_Q = 0
_C_KC = 512
_C_VC = 640
_C_KS = 768
_C_KW = 1024
_C_GN = 1280
_C_U = 1408
_C_GA = 1920
_C_GS = 2944
_C_END = 3968
PROJ_TM = 512


def _proj_kernel(x_ref, mod_ref, g1_ref, w_ref, wvt_ref, qg_ref, ksg_ref, kwg_ref, bd512_ref, bd256_ref,
                 q_ref, kc_ref, vc_ref, ks_ref, kw_ref, vst_ref, vwt_ref, gn_ref, u_ref, ga_ref, gs_ref):
    x = x_ref[...]
    ms = jnp.mean(x * x, axis=-1, keepdims=True)
    mod = mod_ref[0]
    h = (x * lax.rsqrt(ms + EPS) * g1_ref[...]) * (1.0 + mod[1:2]) + mod[0:1]
    hb = h.astype(BF16)

    def p(lo, hi):
        return _dot(hb, w_ref[:, lo:hi])

    q_ref[...] = _seg_rms(p(_C_Q, _C_KC), bd512_ref[...], qg_ref[...] * (ATTN_SCALE * LOG2E)).astype(BF16)
    kc_ref[...] = p(_C_KC, _C_VC).astype(BF16)
    vc_ref[...] = p(_C_VC, _C_KS).astype(BF16)
    ks_ref[...] = _seg_rms(p(_C_KS, _C_KW), bd256_ref[...], ksg_ref[...]).astype(BF16)
    kw_ref[...] = _seg_rms(p(_C_KW, _C_GN), bd256_ref[...], kwg_ref[...]).astype(BF16)
    vt = _dot_nt(wvt_ref[...], hb)
    vst_ref[...] = vt[:LANES].astype(BF16)
    vwt_ref[...] = vt[LANES:].astype(BF16)
    gn_ref[...] = jax.nn.sigmoid(p(_C_GN, _C_U))
    u_ref[...] = p(_C_U, _C_GA)
    ga_ref[...] = jax.nn.sigmoid(p(_C_GA, _C_GS)).astype(BF16)
    gs_ref[...] = jax.nn.sigmoid(p(_C_GS, _C_END)).astype(BF16)


def _dup_cols(w):
    return jnp.concatenate([w[:, :64], w[:, :64], w[:, 64:], w[:, 64:]], axis=1)


def _block_ones(n):
    return jnp.kron(jnp.eye(n // HEAD_DIM, dtype=F32), jnp.ones((HEAD_DIM, HEAD_DIM), F32)).astype(BF16)


def _proj(x2, mod, g_norm1, w_in, q_gain, ks_gain, kw_gain):
    o = np.cumsum((0, 512, 128, 128, 128, 128, 128, 128, 24, 512, 1024, 1024))
    parts = [w_in[:, o[i]:o[i + 1]] for i in range(11)]
    wq, wkc, wvc, wks, wvs, wkw, wvw, wgn, wu, wga, wgs = parts
    w = jnp.concatenate([wq, wkc, wvc, _dup_cols(wks), _dup_cols(wkw),
                         jnp.pad(wgn, ((0, 0), (0, LANES - 24))), wu, wga, wgs], axis=1).astype(BF16)
    wvt = jnp.concatenate([wvs, wvw], axis=1).T.astype(BF16)
    tm = PROJ_TM
    row = lambda i: (i, 0)
    col = lambda i: (0, i)
    fix = lambda i: (0, 0)
    outs = [(512, BF16, row), (128, BF16, row), (128, BF16, row), (256, BF16, row), (256, BF16, row),
            (LANES, BF16, col), (LANES, BF16, col),
            (128, F32, row), (512, F32, row), (1024, BF16, row), (1024, BF16, row)]
    ospec = lambda wd, m: pl.BlockSpec((tm, wd), m) if m is row else pl.BlockSpec((wd, tm), m)
    oshape = lambda wd, dt, m: jax.ShapeDtypeStruct((N_TOK, wd) if m is row else (wd, N_TOK), dt)
    return pl.pallas_call(
        _proj_kernel,
        grid=(N_TOK // tm,),
        in_specs=[pl.BlockSpec((tm, D_MODEL), row),
                  pl.BlockSpec((1, 6, D_MODEL), lambda i: (i // (SEQ // tm), 0, 0)),
                  pl.BlockSpec((1, D_MODEL), fix),
                  pl.BlockSpec((D_MODEL, _C_END), fix),
                  pl.BlockSpec((2 * LANES, D_MODEL), fix),
                  pl.BlockSpec((1, 512), fix), pl.BlockSpec((1, 256), fix), pl.BlockSpec((1, 256), fix),
                  pl.BlockSpec((512, 512), fix), pl.BlockSpec((256, 256), fix)],
        out_specs=[ospec(wd, m) for wd, _, m in outs],
        out_shape=[oshape(wd, dt, m) for wd, dt, m in outs],
        compiler_params=_cparams("arbitrary"),
        name="proj",
    )(x2, mod, g_norm1.reshape(1, -1), w, wvt,
      jnp.tile(q_gain, N_HEADS).reshape(1, -1), jnp.tile(ks_gain, 4).reshape(1, -1),
      jnp.tile(kw_gain, 4).reshape(1, -1), _block_ones(512), _block_ones(256))


def _compress_kernel(r_ref, pe_ref, w1_ref, w2_ref, bd_ref, gain_ref, o_ref, *, do_norm):
    r = r_ref[0].astype(F32)
    p0 = _dot((r + pe_ref[0]).astype(BF16), w1_ref[0])
    p1 = _dot((r + pe_ref[1]).astype(BF16), w1_ref[1])
    hid = p0 + pltpu.roll(p1, N_CMP - 1, 0)
    c = _dot(_gelu(hid).astype(BF16), w2_ref[...])
    if do_norm:
        c = _seg_rms(c, bd_ref[...], gain_ref[...])
    o_ref[0] = c.astype(BF16)


def _compress(raw, pe, w1, w2, gain, do_norm):
    r = raw.reshape(BATCH, SEQ // CMP_STRIDE, CMP_STRIDE * LANES)
    eye = jnp.eye(N_KV, dtype=F32)
    w1r = w1.reshape(2, CMP_STRIDE, HEAD_DIM, CMP_HIDDEN)
    w1big = jnp.einsum('hldc,gk->hlgdkc', w1r, eye).reshape(2, CMP_STRIDE * LANES, N_KV * CMP_HIDDEN).astype(BF16)
    w2big = jnp.einsum('cd,gk->gckd', w2, eye)
    w2big = jnp.concatenate([w2big, w2big], axis=-1).reshape(N_KV * CMP_HIDDEN, 4 * HEAD_DIM).astype(BF16)
    pe_big = jnp.broadcast_to(pe.reshape(2, CMP_STRIDE, 1, HEAD_DIM), (2, CMP_STRIDE, N_KV, HEAD_DIM))
    pe_big = pe_big.reshape(2, 1, CMP_STRIDE * LANES)
    fix2 = lambda b: (0, 0)
    fix3 = lambda b: (0, 0, 0)
    return pl.pallas_call(
        functools.partial(_compress_kernel, do_norm=do_norm),
        grid=(BATCH,),
        in_specs=[pl.BlockSpec((1, N_CMP, CMP_STRIDE * LANES), lambda b: (b, 0, 0)),
                  pl.BlockSpec((2, 1, CMP_STRIDE * LANES), fix3),
                  pl.BlockSpec((2, CMP_STRIDE * LANES, N_KV * CMP_HIDDEN), fix3),
                  pl.BlockSpec((N_KV * CMP_HIDDEN, 256), fix2),
                  pl.BlockSpec((256, 256), fix2), pl.BlockSpec((1, 256), fix2)],
        out_specs=pl.BlockSpec((1, N_CMP, 256), lambda b: (b, 0, 0)),
        out_shape=jax.ShapeDtypeStruct((BATCH, N_CMP, 256), BF16),
        compiler_params=_cparams("arbitrary"),
        name="compress_k" if do_norm else "compress_v",
    )(r, pe_big, w1big, w2big, _block_ones(256), jnp.tile(gain, 4).reshape(1, -1))


ATT_TQ = 256
RANK_CHUNK = 16


def _head_variants(qb):
    lane = lax.broadcasted_iota(I32, qb.shape, 1)
    z = jnp.zeros_like(qb)
    return jnp.where(lane < HEAD_DIM, qb, z), jnp.where(lane < HEAD_DIM, z, qb)


def _cmp_kernel(q_ref, kc_ref, vc_ref, ov_ref, o_ref, sel_ref, vrank_ref):
    tq = ATT_TQ
    qi = pl.program_id(1)
    tpos = qi * tq + lax.broadcasted_iota(I32, (tq, N_CMP), 0)
    nidx = lax.broadcasted_iota(I32, (tq, N_CMP), 1)
    mask = (CMP_STRIDE * nidx + (CMP_BLOCK - 1)) <= tpos
    lane_lo = lax.broadcasted_iota(I32, (tq, LANES), 1) < HEAD_DIM
    for g in range(N_KV):
        kd = kc_ref[0, :, g * LANES:(g + 1) * LANES]
        vd = vc_ref[0, :, g * LANES:(g + 1) * LANES]
        psum = jnp.zeros((tq, N_CMP), F32)
        for jb in range(2):
            blk = 2 * g + jb
            pv = []
            for qv in _head_variants(q_ref[:, blk * LANES:(blk + 1) * LANES]):
                s = jnp.where(mask, _dot_nt(qv, kd), NEG)
                m = jnp.max(s, axis=-1, keepdims=True)
                e = jnp.where(mask, jnp.exp2(s - m), 0.0)
                l = jnp.sum(e, axis=-1, keepdims=True)
                p = e / jnp.where(l > 0.0, l, 1.0)
                psum = psum + p
                pv.append(_dot(p.astype(BF16), vd))
            o_ref[:, blk * LANES:(blk + 1) * LANES] = jnp.where(lane_lo, pv[0], pv[1]).astype(BF16)
        imp = _split_dot(psum, ov_ref[...])
        imp_t = imp.T[:N_SEL_BLOCKS]
        j = lax.broadcasted_iota(I32, (N_SEL_BLOCKS, tq), 0)
        cur = jnp.right_shift(qi * tq + lax.broadcasted_iota(I32, (N_SEL_BLOCKS, tq), 1), 6)
        forced = (j == 0) | (j == cur) | (j == cur - 1)
        v = jnp.where(forced, jnp.inf, jnp.where(j <= cur, imp_t, -jnp.inf))
        vrank_ref[...] = jnp.zeros((N_SEL_BLOCKS, tq), F32)
        n_live = (qi + 1) * (tq // SEL_BLOCK)
        for c0 in range(0, N_SEL_BLOCKS, RANK_CHUNK):
            @pl.when(c0 < n_live)
            def _():
                rank = vrank_ref[...]
                for jp in range(c0, c0 + RANK_CHUNK):
                    row = v[jp:jp + 1, :]
                    tie = jnp.where(j > jp, 1.0, 0.0)
                    rank = rank + jnp.where(row > v, 1.0, jnp.where(row == v, tie, 0.0))
                vrank_ref[...] = rank
        rank = vrank_ref[...]
        sel_ref[g * N_SEL_BLOCKS:(g + 1) * N_SEL_BLOCKS, :] = jnp.where(rank < float(N_SELECT), 0.0, NEG)


def _cmp_attn(q, kcn, vcn):
    nc = np.arange(N_CMP)
    sb = np.arange(LANES)
    ov = ((CMP_STRIDE * nc[:, None] < SEL_BLOCK * sb[None, :] + SEL_BLOCK)
          & (CMP_STRIDE * nc[:, None] + CMP_BLOCK > SEL_BLOCK * sb[None, :])
          & (nc[:, None] < N_CMP - 1) & (sb[None, :] < N_SEL_BLOCKS))
    ov = jnp.asarray(ov, BF16)
    tq = ATT_TQ
    nq = SEQ // tq
    row = lambda b, i: (b * nq + i, 0)
    return pl.pallas_call(
        _cmp_kernel,
        grid=(BATCH, nq),
        in_specs=[pl.BlockSpec((tq, NSA_WIDTH), row),
                  pl.BlockSpec((1, N_CMP, 256), lambda b, i: (b, 0, 0)),
                  pl.BlockSpec((1, N_CMP, 256), lambda b, i: (b, 0, 0)),
                  pl.BlockSpec((N_CMP, LANES), lambda b, i: (0, 0))],
        out_specs=[pl.BlockSpec((tq, NSA_WIDTH), row),
                   pl.BlockSpec((N_KV * N_SEL_BLOCKS, tq), lambda b, i: (0, b * nq + i))],
        out_shape=[jax.ShapeDtypeStruct((N_TOK, NSA_WIDTH), BF16),
                   jax.ShapeDtypeStruct((N_KV * N_SEL_BLOCKS, N_TOK), F32)],
        scratch_shapes=[pltpu.VMEM((N_SEL_BLOCKS, tq), F32)],
        compiler_params=_cparams("arbitrary", "arbitrary"),
        name="cmp_attn",
    )(q, kcn, vcn, ov)


ATT_TK = 256


M_INIT = -1e29


SUM_ROWS = 16


def _selwin_kernel(q_ref, ks_ref, kw_ref, vst_ref, vwt_ref, selb_ref, osel_ref, owin_ref, m_ref, acc_ref):
    tq, tk = ATT_TQ, ATT_TK
    qi = pl.program_id(1)
    krow = lax.broadcasted_iota(I32, (tk, tq), 0)
    qcol = lax.broadcasted_iota(I32, (tk, tq), 1)
    causal_bias = jnp.where(krow <= qcol, 0.0, NEG)
    far_bias = jnp.where(qcol < krow, 0.0, NEG)

    ones_rows = jnp.ones((SUM_ROWS, tk), BF16)

    def reset():
        m_ref[...] = jnp.full(m_ref.shape, M_INIT, F32)
        acc_ref[...] = jnp.zeros(acc_ref.shape, F32)

    def update(g, k_ref, vt_ref, kt, bias):
        k0 = pl.multiple_of(kt * tk, tk)
        kd = k_ref[0, pl.ds(k0, tk), g * LANES:(g + 1) * LANES]
        vt = vt_ref[g * HEAD_DIM:(g + 1) * HEAD_DIM, pl.ds(k0, tk)]
        s = _dot_nt(kd, qvars[g])
        if bias is not None:
            s = s + jnp.concatenate([bias] * 4, axis=1)
        m_old = m_ref[g]
        m_new = jnp.maximum(m_old, jnp.max(s, axis=0, keepdims=True))
        alpha = jnp.exp2(m_old - m_new)
        p = jnp.exp2(s - m_new)
        m_ref[g] = m_new
        vte = jnp.concatenate([vt, ones_rows], axis=0)
        acc_ref[g] = alpha * acc_ref[g] + _dot(vte, p.astype(BF16))

    def finish(out_ref, g):
        o = acc_ref[g, :HEAD_DIM, :] / acc_ref[g, HEAD_DIM:HEAD_DIM + 1, :]
        for jb in range(2):
            blk = 2 * g + jb
            pair = jnp.concatenate([o[:, 2 * jb * tq:(2 * jb + 1) * tq], o[:, (2 * jb + 1) * tq:(2 * jb + 2) * tq]],
                                   axis=0)
            out_ref[:, blk * LANES:(blk + 1) * LANES] = pair.T.astype(BF16)

    def sel_bias(g, kt):
        rows = [jnp.broadcast_to(selb_ref[pl.ds(g * N_SEL_BLOCKS + kt * (tk // SEL_BLOCK) + r, 1), :],
                                 (SEL_BLOCK, tq)) for r in range(tk // SEL_BLOCK)]
        return jnp.concatenate(rows, axis=0)

    qvars = []
    for g in range(N_KV):
        heads = []
        for jb in range(2):
            heads.extend(_head_variants(q_ref[:, (2 * g + jb) * LANES:(2 * g + jb + 1) * LANES]))
        qvars.append(jnp.concatenate(heads, axis=0))
    groups = range(N_KV)

    reset()

    def sel_step(kt, carry):
        for g in groups:
            update(g, ks_ref, vst_ref, kt, sel_bias(g, kt))
        return carry

    lax.fori_loop(0, qi, sel_step, 0)
    for g in groups:
        update(g, ks_ref, vst_ref, qi, sel_bias(g, qi) + causal_bias)
    for g in groups:
        finish(osel_ref, g)

    reset()

    @pl.when(qi >= 2)
    def _():
        for g in groups:
            update(g, kw_ref, vwt_ref, qi - 2, far_bias)

    @pl.when(qi >= 1)
    def _():
        for g in groups:
            update(g, kw_ref, vwt_ref, qi - 1, None)

    for g in groups:
        update(g, kw_ref, vwt_ref, qi, causal_bias)
    for g in groups:
        finish(owin_ref, g)


def _selwin(q, ks, kw, vst, vwt, selb):
    tq = ATT_TQ
    nq = SEQ // tq
    assert WINDOW == 2 * ATT_TK and ATT_TQ == ATT_TK
    row = lambda b, i: (b * nq + i, 0)
    keys = pl.BlockSpec((1, SEQ, 256), lambda b, i: (b, 0, 0))
    vals = pl.BlockSpec((LANES, SEQ), lambda b, i: (0, b))
    r3 = lambda a: a.reshape(BATCH, SEQ, 256)
    return pl.pallas_call(
        _selwin_kernel,
        grid=(BATCH, nq),
        in_specs=[pl.BlockSpec((tq, NSA_WIDTH), row), keys, keys, vals, vals,
                  pl.BlockSpec((N_KV * N_SEL_BLOCKS, tq), lambda b, i: (0, b * nq + i))],
        out_specs=[pl.BlockSpec((tq, NSA_WIDTH), row)] * 2,
        out_shape=[jax.ShapeDtypeStruct((N_TOK, NSA_WIDTH), BF16)] * 2,
        scratch_shapes=[pltpu.VMEM((N_KV, 1, 4 * tq), F32),
                        pltpu.VMEM((N_KV, HEAD_DIM + SUM_ROWS, 4 * tq), F32)],
        compiler_params=_cparams("arbitrary", "arbitrary"),
        name="selwin",
    )(q, r3(ks), r3(kw), vst, vwt, selb)


def _s5_param_kernel(are_ref, aim_ref, ldt_ref, cre_ref, cim_ref, bre_ref, bim_ref,
                     clre_ref, clim_ref, wbre_ref, wbim_ref, bbre_ref, bbim_ref, ltre_ref, ltim_ref):
    are, aim = are_ref[...], aim_ref[...]
    dt = jnp.exp(ldt_ref[...])
    cre, cim = cre_ref[...], cim_ref[...]

    def lam_pow(tau):
        mag = jnp.exp(are * dt * float(tau))
        ang = aim * dt * float(tau)
        return mag * jnp.cos(ang), mag * jnp.sin(ang)

    lre, lim = lam_pow(1)
    den = are * are + aim * aim
    qre = ((lre - 1.0) * are + lim * aim) / den
    qim = (lim * are - (lre - 1.0) * aim) / den
    bre, bim = bre_ref[...], bim_ref[...]
    bbre = qre * bre - qim * bim
    bbim = qre * bim + qim * bre
    bbre_ref[...] = bbre
    bbim_ref[...] = bbim
    for tau in range(S5_T + 1):
        pr, pi = lam_pow(tau)
        clre_ref[tau] = cre * pr - cim * pi
        clim_ref[tau] = cre * pi + cim * pr
        if tau < S5_T:
            k = S5_T - 1 - tau
            wbre_ref[k] = pr * bbre - pi * bbim
            wbim_ref[k] = pr * bbim + pi * bbre
        else:
            ltre_ref[...] = pr
            ltim_ref[...] = pi


def _s5_kmat_kernel(l_ref, r_ref, o_ref):
    o_ref[0] = jnp.dot(l_ref[0], r_ref[0], preferred_element_type=F32, precision=HIGHEST)


def _s5_params(a_re, a_im, log_dt, b_re, b_im, c_re, c_im):
    T = S5_T
    pn = GROUP * STATE
    tile_p = lambda a: jnp.tile(a, (1, GROUP))
    args = (tile_p(a_re), tile_p(a_im), jnp.broadcast_to(log_dt[:, None], (N_GROUPS, pn)),
            c_re.reshape(N_GROUPS, pn), c_im.reshape(N_GROUPS, pn),
            jnp.swapaxes(b_re, 1, 2).reshape(N_GROUPS, pn), jnp.swapaxes(b_im, 1, 2).reshape(N_GROUPS, pn))
    full2 = pl.BlockSpec((N_GROUPS, pn), lambda: (0, 0))
    clre, clim, wbre, wbim, bbre, bbim, ltre, ltim = pl.pallas_call(
        _s5_param_kernel,
        in_specs=[full2] * 7,
        out_specs=[pl.BlockSpec((T + 1, N_GROUPS, pn), lambda: (0, 0, 0))] * 2
                  + [pl.BlockSpec((T, N_GROUPS, pn), lambda: (0, 0, 0))] * 2 + [full2] * 4,
        out_shape=[jax.ShapeDtypeStruct((T + 1, N_GROUPS, pn), F32)] * 2
                  + [jax.ShapeDtypeStruct((T, N_GROUPS, pn), F32)] * 2
                  + [jax.ShapeDtypeStruct((N_GROUPS, pn), F32)] * 4,
        name="s5_params",
    )(*args)

    r5 = lambda a, t: a[:t].reshape(t, N_GROUPS, GROUP, STATE)
    lhs = jnp.concatenate([r5(clre, T), -r5(clim, T)], axis=-1)
    lhs = jnp.transpose(lhs, (1, 0, 2, 3)).reshape(N_GROUPS, T * GROUP, 2 * STATE)
    bb = lambda a: jnp.swapaxes(a.reshape(N_GROUPS, GROUP, STATE), 1, 2)
    rhs = jnp.concatenate([bb(bbre), bb(bbim)], axis=1)
    kmat = pl.pallas_call(
        _s5_kmat_kernel,
        grid=(N_GROUPS,),
        in_specs=[pl.BlockSpec((1, T * GROUP, 2 * STATE), lambda g: (g, 0, 0)),
                  pl.BlockSpec((1, 2 * STATE, GROUP), lambda g: (g, 0, 0))],
        out_specs=pl.BlockSpec((1, T * GROUP, GROUP), lambda g: (g, 0, 0)),
        out_shape=jax.ShapeDtypeStruct((N_GROUPS, T * GROUP, GROUP), F32),
        compiler_params=_cparams("arbitrary"),
        name="s5_kmat",
    )(lhs, rhs)

    eye = jnp.eye(S5_GL, dtype=F32)
    kt = kmat.reshape(S5_SG, S5_GL, T, GROUP, GROUP)
    kbd = jnp.einsum('sgtpq,gh->stgqhp', kt, eye).reshape(S5_SG, T, LANES, LANES)
    krev = kbd[:, ::-1].reshape(S5_SG, T * LANES, LANES).astype(BF16)
    r6 = lambda a: a.reshape(T, S5_SG, S5_GL, GROUP, STATE)
    wb = jnp.stack([r6(wbre), r6(wbim)], axis=-2)
    wb = jnp.einsum('ksgpin,gh->skgpihn', wb, eye).reshape(S5_SG, T * LANES, S5_NSTATE).astype(BF16)
    wc = jnp.stack([r6(clre[1:]), -r6(clim[1:])], axis=-2)
    wc = jnp.einsum('tsgpin,gh->signthp', wc, eye).reshape(S5_SG, S5_NSTATE, T * LANES).astype(BF16)
    lt = lambda a: a.reshape(N_GROUPS, GROUP, STATE)[:, 0].reshape(S5_SG, 1, S5_GL * STATE)
    return krev, wb, wc, lt(ltre), lt(ltim)


S5_TC = 512


def _s5_state_kernel(x_ref, wb_ref, e_ref):
    e_ref[0] = _dot(x_ref[0], wb_ref[0])


def _s5_scan_kernel(e_ref, ltre_ref, ltim_ref, xs_ref):
    lr, li = ltre_ref[0], ltim_ref[0]
    half = S5_NSTATE // 2

    def step(c, carry):
        new = []
        for b in range(BATCH):
            xr, xi = carry[b]
            row = b * S5_CH_PER_BATCH + c
            xs_ref[0, pl.ds(row, 1), :half] = xr
            xs_ref[0, pl.ds(row, 1), half:] = xi
            e = e_ref[0, pl.ds(row, 1), :]
            new.append((lr * xr - li * xi + e[:, :half], lr * xi + li * xr + e[:, half:]))
        return tuple(new)

    zero = jnp.zeros((1, half), F32)
    lax.fori_loop(0, S5_CH_PER_BATCH, step, tuple((zero, zero) for _ in range(BATCH)))


def _s5_out_kernel(x_ref, xs_ref, krev_ref, wc_ref, y_ref):
    xsb = xs_ref[0].astype(BF16)
    for t in range(S5_T):
        cols = slice(t * LANES, (t + 1) * LANES)
        y_ref[0, :, cols] = (_dot(x_ref[0, :, :(t + 1) * LANES], krev_ref[0, (S5_T - 1 - t) * LANES:, :])
                             + _dot(xsb, wc_ref[0, :, cols]))


def _s5(u, krev, wb, wc, ltre, ltim):
    T, tc = S5_T, S5_TC
    xs = u.astype(BF16).reshape(S5_CH, T, S5_SG, LANES).transpose(2, 0, 1, 3).reshape(S5_SG, S5_CH, T * LANES)
    grid = (S5_SG, S5_CH // tc)
    rows = lambda s, i: (s, i, 0)
    per_sg = lambda s, i: (s, 0, 0)
    e = pl.pallas_call(
        _s5_state_kernel, grid=grid,
        in_specs=[pl.BlockSpec((1, tc, T * LANES), rows), pl.BlockSpec((1, T * LANES, S5_NSTATE), per_sg)],
        out_specs=pl.BlockSpec((1, tc, S5_NSTATE), rows),
        out_shape=jax.ShapeDtypeStruct((S5_SG, S5_CH, S5_NSTATE), F32),
        compiler_params=_cparams("arbitrary", "arbitrary"), name="s5_state",
    )(xs, wb)
    sg1 = lambda s: (s, 0, 0)
    xstart = pl.pallas_call(
        _s5_scan_kernel, grid=(S5_SG,),
        in_specs=[pl.BlockSpec((1, S5_CH, S5_NSTATE), sg1),
                  pl.BlockSpec((1, 1, S5_NSTATE // 2), sg1), pl.BlockSpec((1, 1, S5_NSTATE // 2), sg1)],
        out_specs=pl.BlockSpec((1, S5_CH, S5_NSTATE), sg1),
        out_shape=jax.ShapeDtypeStruct((S5_SG, S5_CH, S5_NSTATE), F32),
        compiler_params=_cparams("arbitrary"), name="s5_scan",
    )(e, ltre, ltim)
    y = pl.pallas_call(
        _s5_out_kernel, grid=grid,
        in_specs=[pl.BlockSpec((1, tc, T * LANES), rows), pl.BlockSpec((1, tc, S5_NSTATE), rows),
                  pl.BlockSpec((1, T * LANES, LANES), per_sg), pl.BlockSpec((1, S5_NSTATE, T * LANES), per_sg)],
        out_specs=pl.BlockSpec((1, tc, T * LANES), rows),
        out_shape=jax.ShapeDtypeStruct((S5_SG, S5_CH, T * LANES), F32),
        compiler_params=_cparams("arbitrary", "arbitrary"), name="s5_out",
    )(xs, xstart, krev, wc)
    return y.reshape(S5_SG, S5_CH, T, LANES).transpose(1, 2, 0, 3).reshape(N_TOK, SSM_WIDTH)


MERGE_TM = 256


def _merge_kernel(ocmp_ref, osel_ref, owin_ref, gn_ref, yssm_ref, u_ref, ga_ref, gs_ref, x_ref, mod_ref,
                  eg_ref, dskip_ref, wglu_ref, bglu_ref, wua_ref, wus_ref, wout_ref, g2_ref,
                  wrhi_ref, wrlo_ref, wsgu_ref, wsd_ref,
                  xpart_ref, h2_ref, logit_ref):
    mod = mod_ref[0]
    gnb = gn_ref[...].astype(BF16)
    o_nsa = (_dot(gnb, eg_ref[0]) * ocmp_ref[...].astype(F32)
             + _dot(gnb, eg_ref[1]) * osel_ref[...].astype(F32)
             + _dot(gnb, eg_ref[2]) * owin_ref[...].astype(F32))
    attn = _dot(o_nsa.astype(BF16), wua_ref[...])
    z = _gelu(yssm_ref[...] + dskip_ref[...] * u_ref[...])
    y_ssm = z * jax.nn.sigmoid(_dot(z.astype(BF16), wglu_ref[...]) + bglu_ref[...])
    ssm = _dot(y_ssm.astype(BF16), wus_ref[...])
    merged = ga_ref[...].astype(F32) * attn + gs_ref[...].astype(F32) * ssm
    x1 = x_ref[...] + mod[2:3] * _dot(merged.astype(BF16), wout_ref[...])

    ms = jnp.mean(x1 * x1, axis=-1, keepdims=True)
    h2 = (x1 * lax.rsqrt(ms + EPS) * g2_ref[...]) * (1.0 + mod[4:5]) + mod[3:4]
    hi = h2.astype(BF16)
    lo = (h2 - hi.astype(F32)).astype(BF16)
    h2_ref[...] = _pack_bf16_pairs(h2)
    logit_ref[...] = _dot_nt(wrhi_ref[...], hi) + _dot_nt(wrhi_ref[...], lo) + _dot_nt(wrlo_ref[...], hi)
    gu = _dot(hi, wsgu_ref[...])
    shared = _dot((_silu(gu[:, :D_EXPERT]) * gu[:, D_EXPERT:]).astype(BF16), wsd_ref[...])
    xpart_ref[...] = x1 + mod[5:6] * shared


def _merge(ocmp, osel, owin, gn, yssm, u, ga, gs, x2, mod, d_skip, w_glu, b_glu, w_up_attn, w_up_ssm, w_out,
           g_norm2, w_router, ws_gate, ws_up, ws_down):
    tm = MERGE_TM
    eg = np.zeros((3, LANES, NSA_WIDTH), np.float32)
    for j in range(3):
        for h in range(N_HEADS):
            eg[j, 3 * h + j, h * HEAD_DIM:(h + 1) * HEAD_DIM] = 1.0
    wr_t = w_router.T
    wr_hi = wr_t.astype(BF16)
    wr_lo = (wr_t - wr_hi.astype(F32)).astype(BF16)
    row = lambda i: (i, 0)
    fix2 = lambda i: (0, 0)
    wspec = lambda a: pl.BlockSpec(a.shape, (lambda i: (0,) * a.ndim))
    weights = [jnp.asarray(eg, BF16), d_skip.reshape(1, -1), w_glu.astype(BF16), b_glu.reshape(1, -1),
               w_up_attn.astype(BF16), w_up_ssm.astype(BF16), w_out.astype(BF16), g_norm2.reshape(1, -1),
               wr_hi, wr_lo, jnp.concatenate([ws_gate, ws_up], axis=1).astype(BF16), ws_down.astype(BF16)]
    acts = [(ocmp, 512), (osel, 512), (owin, 512), (gn, 128), (yssm, 512), (u, 512), (ga, 1024), (gs, 1024),
            (x2, 1024)]
    return pl.pallas_call(
        _merge_kernel,
        grid=(N_TOK // tm,),
        in_specs=[pl.BlockSpec((tm, wd), row) for _, wd in acts]
                 + [pl.BlockSpec((1, 6, D_MODEL), lambda i: (i // (SEQ // tm), 0, 0))]
                 + [wspec(w) for w in weights],
        out_specs=[pl.BlockSpec((tm, D_MODEL), row), pl.BlockSpec((tm, HALF), row),
                   pl.BlockSpec((N_EXPERTS, tm), lambda i: (0, i))],
        out_shape=[jax.ShapeDtypeStruct((N_TOK, D_MODEL), F32), jax.ShapeDtypeStruct((N_TOK, HALF), jnp.uint32),
                   jax.ShapeDtypeStruct((N_EXPERTS, N_TOK), F32)],
        compiler_params=_cparams("arbitrary"),
        name="merge",
    )(*[a for a, _ in acts], mod, *weights)


ROUTE_TN = 512


def _route_kernel(logit_ref, bias_ref, eidx_ref, w_ref, count_ref, gscore_ref, masked_ref):
    tn = ROUTE_TN

    @pl.when(pl.program_id(0) == 0)
    def _():
        count_ref[...] = jnp.zeros(count_ref.shape, F32)

    sc = jax.nn.sigmoid(logit_ref[...])
    biased = sc + bias_ref[...]
    gi = lax.broadcasted_iota(I32, (EXPERTS_PER_GROUP, tn), 0).astype(F32)
    for g in range(N_EXPERT_GROUPS):
        blk = biased[g * EXPERTS_PER_GROUP:(g + 1) * EXPERTS_PER_GROUP]
        m1 = jnp.max(blk, axis=0, keepdims=True)
        i1 = jnp.min(jnp.where(blk == m1, gi, float(EXPERTS_PER_GROUP)), axis=0, keepdims=True)
        m2 = jnp.max(jnp.where(gi == i1, -jnp.inf, blk), axis=0, keepdims=True)
        gscore_ref[g:g + 1, :] = m1 + m2
    gs = gscore_ref[...]
    gidx = lax.broadcasted_iota(I32, (N_EXPERT_GROUPS, tn), 0)
    grank = jnp.zeros((N_EXPERT_GROUPS, tn), F32)
    for gp in range(N_EXPERT_GROUPS):
        row = gs[gp:gp + 1, :]
        tie = jnp.where(gidx > gp, 1.0, 0.0)
        grank = grank + jnp.where(row > gs, 1.0, jnp.where(row == gs, tie, 0.0))
    for g in range(N_EXPERT_GROUPS):
        keep = grank[g:g + 1, :] < float(TOPK_GROUPS)
        sl = slice(g * EXPERTS_PER_GROUP, (g + 1) * EXPERTS_PER_GROUP)
        masked_ref[sl, :] = jnp.where(keep, biased[sl], -jnp.inf)
    cur = masked_ref[...]
    eidx = lax.broadcasted_iota(I32, (N_EXPERTS, tn), 0).astype(F32)
    wsum = jnp.zeros((1, tn), F32)
    hits = jnp.zeros((N_EXPERTS, tn), F32)
    for k in range(TOP_K):
        m = jnp.max(cur, axis=0, keepdims=True)
        idx = jnp.min(jnp.where(cur == m, eidx, float(N_EXPERTS)), axis=0, keepdims=True)
        hit = eidx == idx
        wk = jnp.sum(jnp.where(hit, sc, 0.0), axis=0, keepdims=True)
        cur = jnp.where(hit, -jnp.inf, cur)
        hits = hits + jnp.where(hit, 1.0, 0.0)
        eidx_ref[k:k + 1, :] = idx.astype(I32)
        w_ref[k:k + 1, :] = wk
        wsum = wsum + wk
    w_ref[...] = w_ref[...] / wsum * ROUTE_SCALE
    count_ref[...] = count_ref[...] + jnp.sum(hits, axis=1, keepdims=True)


def _route(logits_t, router_bias):
    tn = ROUTE_TN
    return pl.pallas_call(
        _route_kernel,
        grid=(N_TOK // tn,),
        in_specs=[pl.BlockSpec((N_EXPERTS, tn), lambda i: (0, i)), pl.BlockSpec((N_EXPERTS, 1), lambda i: (0, 0))],
        out_specs=[pl.BlockSpec((TOP_K, tn), lambda i: (0, i))] * 2 + [pl.BlockSpec((N_EXPERTS, 1), lambda i: (0, 0))],
        out_shape=[jax.ShapeDtypeStruct((TOP_K, N_TOK), I32), jax.ShapeDtypeStruct((TOP_K, N_TOK), F32),
                   jax.ShapeDtypeStruct((N_EXPERTS, 1), F32)],
        scratch_shapes=[pltpu.VMEM((N_EXPERT_GROUPS, tn), F32), pltpu.VMEM((N_EXPERTS, tn), F32)],
        compiler_params=_cparams("arbitrary"),
        name="route",
    )(logits_t, router_bias.reshape(-1, 1))


N_MOE_BLK = NK // DISPATCH_BLOCK
N_ITEMS = N_MOE_BLK + N_EXPERTS
ASSIGN_BITS = 17


def _dispatch_plan(eidx, counts):
    e_flat = eidx.reshape(-1)
    key = jnp.sort(e_flat * NK + jnp.arange(NK, dtype=I32))
    order = key & (NK - 1)
    counts = counts.reshape(-1).astype(I32)
    start = jnp.cumsum(counts) - counts
    cuts = jnp.sort(jnp.concatenate([jnp.arange(N_MOE_BLK, dtype=I32) * DISPATCH_BLOCK, start]))
    lo = cuts
    hi = jnp.concatenate([cuts[1:], jnp.full((1,), NK, I32)])
    blk = jnp.minimum(lo // DISPATCH_BLOCK, N_MOE_BLK - 1)
    expert = jnp.clip(jnp.sum((start[None, :] <= lo[:, None]).astype(I32), axis=1) - 1, 0, N_EXPERTS - 1)
    one = jnp.ones((1,), I32)
    first = jnp.concatenate([one, (blk[1:] != blk[:-1]).astype(I32)])
    last = jnp.concatenate([(blk[1:] != blk[:-1]).astype(I32), one])
    new_expert = jnp.concatenate([one, (expert[1:] != expert[:-1]).astype(I32)])
    run_id = jnp.cumsum(new_expert) - 1
    n_runs = run_id[-1] + 1
    item = jnp.arange(N_ITEMS, dtype=I32)
    run_first_item = jnp.sort(jnp.where(new_expert == 1, item, N_ITEMS))
    run_expert = expert[jnp.minimum(run_first_item, N_ITEMS - 1)]
    ahead = run_id + (WEIGHT_RING - 1)
    ahead_expert = run_expert[jnp.minimum(ahead, N_ITEMS - 1)]
    ahead_valid = (ahead < n_runs).astype(I32)
    second_expert = run_expert[1:2]
    prologue = jnp.concatenate([second_expert, (n_runs > 1).astype(I32).reshape(1)])
    tok = jnp.right_shift(order, 3)
    home = (order & (TOP_K - 1)) * N_TOK + tok
    return tok, home, (blk, expert, lo - blk * DISPATCH_BLOCK, hi - blk * DISPATCH_BLOCK, first, last, new_expert,
                      run_id % WEIGHT_RING, ahead_expert, ahead_valid, prologue)


SC_CORES = 2
SC_SUBCORES = 16
SC_CHUNK = 128


def _sc_move_rows(table, idx, scatter):
    n = idx.shape[0]
    workers = SC_CORES * SC_SUBCORES
    per_worker = n // workers
    n_chunks = per_worker // SC_CHUNK
    assert per_worker * workers == n and n_chunks * SC_CHUNK == per_worker
    mesh = plsc.VectorSubcoreMesh(core_axis_name="c", subcore_axis_name="s",
                                  num_cores=SC_CORES, num_subcores=SC_SUBCORES)

    def body(table_hbm, idx_hbm, out_hbm, idx_v, rows_v, sem):
        wid = lax.axis_index("s") * SC_CORES + lax.axis_index("c")
        base = wid * per_worker

        @pl.loop(0, n_chunks)
        def _(j):
            off = base + j * SC_CHUNK
            pltpu.sync_copy(idx_hbm.at[pl.ds(off, SC_CHUNK)], idx_v)
            if scatter:
                pltpu.sync_copy(table_hbm.at[pl.ds(off, SC_CHUNK)], rows_v)
                pltpu.async_copy(rows_v, out_hbm.at[idx_v], sem).wait()
            else:
                pltpu.async_copy(table_hbm.at[idx_v], rows_v, sem).wait()
                pltpu.sync_copy(rows_v, out_hbm.at[pl.ds(off, SC_CHUNK)])

    return pl.kernel(
        body,
        out_type=jax.ShapeDtypeStruct((n, table.shape[1]), table.dtype),
        mesh=mesh,
        scratch_types=[pltpu.VMEM((SC_CHUNK,), I32), pltpu.VMEM((SC_CHUNK, table.shape[1]), table.dtype),
                       pltpu.SemaphoreType.DMA],
        name="sc_scatter_rows" if scatter else "sc_gather_rows",
    )(table, idx)


WEIGHT_RING = 3
WEIGHT_CHUNKS = 4


def _expert_weight_copies(w_hbm, wbuf, sem, expert, slot):
    rows = w_hbm.shape[1] // WEIGHT_CHUNKS
    return [pltpu.make_async_copy(w_hbm.at[expert, pl.ds(c * rows, rows)],
                                  wbuf.at[slot, pl.ds(c * rows, rows)], sem.at[slot])
            for c in range(WEIGHT_CHUNKS)]


def _moe_kernel(blk_ref, exp_ref, lo_ref, hi_ref, first_ref, last_ref, newexp_ref,
                slot_ref, ahead_exp_ref, ahead_ok_ref, prologue_ref,
                x_ref, wg_hbm, wu_hbm, wd_hbm, y_ref,
                acc_ref, wgf_ref, wuf_ref, wdf_ref, wgb_ref, wub_ref, wdb_ref, wsem):
    it = pl.program_id(0)
    lo, hi = lo_ref[it], hi_ref[it]
    streams = ((wg_hbm, wgf_ref), (wu_hbm, wuf_ref), (wd_hbm, wdf_ref))

    def request(expert, slot):
        for w_hbm, wbuf in streams:
            for cp in _expert_weight_copies(w_hbm, wbuf, wsem, expert, slot):
                cp.start()

    @pl.when(it == 0)
    def _():
        request(exp_ref[0], 0)

        @pl.when(prologue_ref[1] == 1)
        def _():
            request(prologue_ref[0], 1)

    @pl.when(newexp_ref[it] == 1)
    def _():
        slot = slot_ref[it]
        for w_hbm, wbuf in streams:
            for cp in _expert_weight_copies(w_hbm, wbuf, wsem, 0, slot):
                cp.wait()
        wgb_ref[...] = wgf_ref[slot].astype(BF16)
        wub_ref[...] = wuf_ref[slot].astype(BF16)
        wdb_ref[...] = wdf_ref[slot].astype(BF16)

        @pl.when(ahead_ok_ref[it] == 1)
        def _():
            ahead_slot = slot + (WEIGHT_RING - 1)
            request(ahead_exp_ref[it], jnp.where(ahead_slot >= WEIGHT_RING, ahead_slot - WEIGHT_RING, ahead_slot))

    @pl.when(first_ref[it] == 1)
    def _():
        acc_ref[...] = jnp.zeros(acc_ref.shape, F32)

    def expert_pass(r0, nrows):
        rows = slice(r0, r0 + nrows)
        ridx = r0 + lax.broadcasted_iota(I32, (nrows, HALF), 0)
        mine = (ridx >= lo) & (ridx < hi)
        xlo, xhi = _unpack_bf16_pairs(jnp.where(mine, x_ref[rows, :], jnp.uint32(0)))
        xlo, xhi = xlo.astype(BF16), xhi.astype(BF16)
        gate = _dot(xlo, wgb_ref[:HALF]) + _dot(xhi, wgb_ref[HALF:])
        up = _dot(xlo, wub_ref[:HALF]) + _dot(xhi, wub_ref[HALF:])
        acc_ref[rows, :] = acc_ref[rows, :] + _dot((_silu(gate) * up).astype(BF16), wdb_ref[...])

    mid = DISPATCH_BLOCK // 2
    pl.when((lo < mid) & (hi > mid))(lambda: expert_pass(0, DISPATCH_BLOCK))
    pl.when((hi > lo) & (hi <= mid))(lambda: expert_pass(0, mid))
    pl.when((hi > lo) & (lo >= mid))(lambda: expert_pass(mid, mid))

    @pl.when(last_ref[it] == 1)
    def _():
        y_ref[...] = _pack_bf16_pairs(acc_ref[...])


def _moe(xs, items, w_gate, w_up, w_down):
    by_blk = lambda it, blk, *_: (blk[it], 0)
    any_space = pl.BlockSpec(memory_space=pl.ANY)
    grid_spec = pltpu.PrefetchScalarGridSpec(
        num_scalar_prefetch=len(items),
        grid=(N_ITEMS,),
        in_specs=[pl.BlockSpec((DISPATCH_BLOCK, HALF), by_blk), any_space, any_space, any_space],
        out_specs=pl.BlockSpec((DISPATCH_BLOCK, HALF), by_blk),
        scratch_shapes=[pltpu.VMEM((DISPATCH_BLOCK, D_MODEL), F32),
                        pltpu.VMEM((WEIGHT_RING, D_MODEL, D_EXPERT), F32),
                        pltpu.VMEM((WEIGHT_RING, D_MODEL, D_EXPERT), F32),
                        pltpu.VMEM((WEIGHT_RING, D_EXPERT, D_MODEL), F32),
                        pltpu.VMEM((D_MODEL, D_EXPERT), BF16), pltpu.VMEM((D_MODEL, D_EXPERT), BF16),
                        pltpu.VMEM((D_EXPERT, D_MODEL), BF16),
                        pltpu.SemaphoreType.DMA((WEIGHT_RING,))],
    )
    return pl.pallas_call(
        _moe_kernel,
        grid_spec=grid_spec,
        out_shape=jax.ShapeDtypeStruct((NK, HALF), jnp.uint32),
        compiler_params=_cparams("arbitrary"),
        name="moe",
    )(*items, xs, w_gate, w_up, w_down)


COMB_TC = 64


def _combine_kernel(slots_ref, w_ref, xpart_ref, mod_ref, out_ref):
    w = w_ref[...]
    lo = jnp.zeros((w.shape[0], HALF), F32)
    hi = jnp.zeros((w.shape[0], HALF), F32)
    for k in range(TOP_K):
        klo, khi = _unpack_bf16_pairs(slots_ref[k])
        lo = lo + w[:, k:k + 1] * klo
        hi = hi + w[:, k:k + 1] * khi
    gate2 = mod_ref[0][5:6]
    out_ref[:, :HALF] = xpart_ref[:, :HALF] + gate2[:, :HALF] * lo
    out_ref[:, HALF:] = xpart_ref[:, HALF:] + gate2[:, HALF:] * hi


def _combine(xpart, mod, slots, w):
    tc = COMB_TC
    row = lambda i: (i, 0)
    return pl.pallas_call(
        _combine_kernel,
        grid=(N_TOK // tc,),
        in_specs=[pl.BlockSpec((TOP_K, tc, HALF), lambda i: (0, i, 0)),
                  pl.BlockSpec((tc, TOP_K), row),
                  pl.BlockSpec((tc, D_MODEL), row),
                  pl.BlockSpec((1, 6, D_MODEL), lambda i: (i // (SEQ // tc), 0, 0))],
        out_specs=pl.BlockSpec((tc, D_MODEL), row),
        out_shape=jax.ShapeDtypeStruct((N_TOK, D_MODEL), F32),
        compiler_params=_cparams("arbitrary"),
        name="combine",
    )(slots.reshape(TOP_K, N_TOK, HALF), w, xpart, mod)


def _layer(x, c, w_ada, b_ada, g_norm1, g_norm2, w_in, q_gain, kc_gain, ks_gain, kw_gain,
           pe_k, pe_v, w_cmp_k1, w_cmp_k2, w_cmp_v1, w_cmp_v2,
           a_re, a_im, log_dt, b_re, b_im, c_re, c_im, d_skip, w_glu, b_glu,
           w_up_attn, w_up_ssm, w_out, w_router, router_bias,
           w_gate, w_up, w_down, ws_gate, ws_up, ws_down):
    x2 = x.reshape(N_TOK, D_MODEL)
    mod = _ada(c, w_ada, b_ada)
    q, kc_raw, vc_raw, ks, kw, vst, vwt, gn, u, ga, gs = _proj(x2, mod, g_norm1, w_in, q_gain, ks_gain, kw_gain)
    kcn = _compress(kc_raw, pe_k, w_cmp_k1, w_cmp_k2, kc_gain, True)
    vcn = _compress(vc_raw, pe_v, w_cmp_v1, w_cmp_v2, kc_gain, False)
    ocmp, selb = _cmp_attn(q, kcn, vcn)
    osel, owin = _selwin(q, ks, kw, vst, vwt, selb)
    yssm = _s5(u, *_s5_params(a_re, a_im, log_dt, b_re, b_im, c_re, c_im))
    xpart, h2, logits_t = _merge(ocmp, osel, owin, gn, yssm, u, ga, gs, x2, mod, d_skip, w_glu, b_glu,
                                  w_up_attn, w_up_ssm, w_out, g_norm2, w_router, ws_gate, ws_up, ws_down)
    eidx_t, w_t, counts = _route(logits_t, router_bias)
    tok, home, items = _dispatch_plan(eidx_t.T, counts)
    y = _moe(_sc_move_rows(h2, tok, scatter=False), items, w_gate, w_up, w_down)
    slots = _sc_move_rows(y, home, scatter=True)
    return _combine(xpart, mod, slots, w_t.T).reshape(BATCH, SEQ, D_MODEL)


def kernel(x, c, w_ada, b_ada, g_norm1, g_norm2, w_in, q_gain, kc_gain, ks_gain, kw_gain, pe_k, pe_v, w_cmp_k1,
           w_cmp_k2, w_cmp_v1, w_cmp_v2, a_re, a_im, log_dt, b_re, b_im, c_re, c_im, d_skip, w_glu, b_glu,
           w_up_attn, w_up_ssm, w_out, w_router, router_bias, w_gate, w_up, w_down, ws_gate, ws_up, ws_down):
    params = (w_ada, b_ada, g_norm1, g_norm2, w_in, q_gain, kc_gain, ks_gain, kw_gain, pe_k, pe_v, w_cmp_k1,
              w_cmp_k2, w_cmp_v1, w_cmp_v2, a_re, a_im, log_dt, b_re, b_im, c_re, c_im, d_skip, w_glu, b_glu,
              w_up_attn, w_up_ssm, w_out, w_router, router_bias, w_gate, w_up, w_down, ws_gate, ws_up, ws_down)
    depth = w_ada.shape[0]
    for layer in range(depth):
        x = _layer(x, c, *[p[layer] for p in params])
    return x
```

```python
import functools
import math

import jax
import jax.numpy as jnp
import numpy as np
from jax import lax
from jax.experimental import pallas as pl
from jax.experimental.pallas import tpu as pltpu
from jax.experimental.pallas import tpu_sc as plsc

F32 = jnp.float32
BF16 = jnp.bfloat16
I32 = jnp.int32
HIGHEST = lax.Precision.HIGHEST

D_MODEL = 1024
BATCH = 4
SEQ = 4096
N_TOK = BATCH * SEQ
N_HEADS = 8
HEAD_DIM = 64
N_KV = 2
CMP_BLOCK = 32
CMP_STRIDE = 16
CMP_HIDDEN = 256
N_CMP = 256
SEL_BLOCK = 64
N_SEL_BLOCKS = SEQ // SEL_BLOCK
N_SELECT = 16
WINDOW = 512
ATTN_SCALE = HEAD_DIM ** -0.5
LOG2E = 1.4426950408889634
NSA_WIDTH = N_HEADS * HEAD_DIM
SSM_WIDTH = 512
GROUP = 16
N_GROUPS = SSM_WIDTH // GROUP
STATE = 64
N_EXPERTS = 256
TOP_K = 8
D_EXPERT = 256
N_EXPERT_GROUPS = 8
EXPERTS_PER_GROUP = N_EXPERTS // N_EXPERT_GROUPS
TOPK_GROUPS = 4
ROUTE_SCALE = 2.5
DISPATCH_BLOCK = 256
EPS = 1e-6
NEG = -1e30

LANES = 128
S5_T = 16
S5_SG = 4
S5_GL = N_GROUPS // S5_SG
S5_CH = N_TOK // S5_T
S5_CH_PER_BATCH = SEQ // S5_T
S5_NSTATE = S5_GL * STATE * 2

NK = N_TOK * TOP_K
HALF = D_MODEL // 2

VMEM_LIMIT = 48 * 1024 * 1024


def _cparams(*sem):
    return pltpu.CompilerParams(dimension_semantics=tuple(sem), vmem_limit_bytes=VMEM_LIMIT)


def _dot(a, b):
    return jnp.dot(a, b, preferred_element_type=F32)


def _dot_nt(a, b):
    return lax.dot_general(a, b, (((1,), (1,)), ((), ())), preferred_element_type=F32)


def _split_dot(v, w):
    hi = v.astype(BF16)
    lo = (v - hi.astype(F32)).astype(BF16)
    return _dot(hi, w) + _dot(lo, w)


def _seg_rms(v, bd, gain):
    ss = _split_dot(v * v, bd)
    return v * lax.rsqrt(ss * (1.0 / HEAD_DIM) + EPS) * gain


def _gelu(x):
    return 0.5 * x * (1.0 + jnp.tanh(0.7978845608028654 * (x + 0.044715 * (x * x * x))))


def _silu(x):
    return x * jax.nn.sigmoid(x)


def _pack_bf16_pairs(v):
    bits = lax.bitcast_convert_type(v.astype(BF16).astype(F32), jnp.uint32)
    h = v.shape[1] // 2
    return bits[:, h:] | lax.shift_right_logical(bits[:, :h], jnp.uint32(16))


def _unpack_bf16_pairs(word):
    lo = lax.bitcast_convert_type(lax.shift_left(word, jnp.uint32(16)), F32)
    hi = lax.bitcast_convert_type(word & jnp.uint32(0xFFFF0000), F32)
    return lo, hi


def _ada_kernel(c_ref, w_ref, b_ref, o_ref):
    c = c_ref[...]
    o_ref[...] = jnp.dot(_silu(c), w_ref[...], preferred_element_type=F32, precision=HIGHEST) + b_ref[...]


def _ada(c, w_ada, b_ada):
    cp = jnp.pad(c, ((0, 8 - BATCH), (0, 0)))
    tn = 1536
    out = pl.pallas_call(
        _ada_kernel,
        grid=(6 * D_MODEL // tn,),
        in_specs=[pl.BlockSpec((8, D_MODEL), lambda j: (0, 0)),
                  pl.BlockSpec((D_MODEL, tn), lambda j: (0, j)),
                  pl.BlockSpec((1, tn), lambda j: (0, j))],
        out_specs=pl.BlockSpec((8, tn), lambda j: (0, j)),
        out_shape=jax.ShapeDtypeStruct((8, 6 * D_MODEL), F32),
        compiler_params=_cparams("arbitrary"),
        name="ada",
    )(cp, w_ada, b_ada.reshape(1, -1))
    return out.reshape(8, 6, D_MODEL)


_C_Q = 0
_C_KC = 512
_C_VC = 640
_C_KS = 768
_C_KW = 1024
_C_GN = 1280
_C_U = 1408
_C_GA = 1920
_C_GS = 2944
_C_END = 3968
PROJ_TM = 512


def _proj_kernel(x_ref, mod_ref, g1_ref, w_ref, wvt_ref, qg_ref, ksg_ref, kwg_ref, bd512_ref, bd256_ref,
                 q_ref, kc_ref, vc_ref, ks_ref, kw_ref, vst_ref, vwt_ref, gn_ref, u_ref, ga_ref, gs_ref):
    x = x_ref[...]
    ms = jnp.mean(x * x, axis=-1, keepdims=True)
    mod = mod_ref[0]
    h = (x * lax.rsqrt(ms + EPS) * g1_ref[...]) * (1.0 + mod[1:2]) + mod[0:1]
    hb = h.astype(BF16)

    def p(lo, hi):
        return _dot(hb, w_ref[:, lo:hi])

    q_ref[...] = _seg_rms(p(_C_Q, _C_KC), bd512_ref[...], qg_ref[...] * (ATTN_SCALE * LOG2E)).astype(BF16)
    kc_ref[...] = p(_C_KC, _C_VC).astype(BF16)
    vc_ref[...] = p(_C_VC, _C_KS).astype(BF16)
    ks_ref[...] = _seg_rms(p(_C_KS, _C_KW), bd256_ref[...], ksg_ref[...]).astype(BF16)
    kw_ref[...] = _seg_rms(p(_C_KW, _C_GN), bd256_ref[...], kwg_ref[...]).astype(BF16)
    vt = _dot_nt(wvt_ref[...], hb)
    vst_ref[...] = vt[:LANES].astype(BF16)
    vwt_ref[...] = vt[LANES:].astype(BF16)
    gn_ref[...] = jax.nn.sigmoid(p(_C_GN, _C_U))
    u_ref[...] = p(_C_U, _C_GA)
    ga_ref[...] = jax.nn.sigmoid(p(_C_GA, _C_GS)).astype(BF16)
    gs_ref[...] = jax.nn.sigmoid(p(_C_GS, _C_END)).astype(BF16)


def _dup_cols(w):
    return jnp.concatenate([w[:, :64], w[:, :64], w[:, 64:], w[:, 64:]], axis=1)


def _block_ones(n):
    return jnp.kron(jnp.eye(n // HEAD_DIM, dtype=F32), jnp.ones((HEAD_DIM, HEAD_DIM), F32)).astype(BF16)


def _proj(x2, mod, g_norm1, w_in, q_gain, ks_gain, kw_gain):
    o = np.cumsum((0, 512, 128, 128, 128, 128, 128, 128, 24, 512, 1024, 1024))
    parts = [w_in[:, o[i]:o[i + 1]] for i in range(11)]
    wq, wkc, wvc, wks, wvs, wkw, wvw, wgn, wu, wga, wgs = parts
    w = jnp.concatenate([wq, wkc, wvc, _dup_cols(wks), _dup_cols(wkw),
                         jnp.pad(wgn, ((0, 0), (0, LANES - 24))), wu, wga, wgs], axis=1).astype(BF16)
    wvt = jnp.concatenate([wvs, wvw], axis=1).T.astype(BF16)
    tm = PROJ_TM
    row = lambda i: (i, 0)
    col = lambda i: (0, i)
    fix = lambda i: (0, 0)
    outs = [(512, BF16, row), (128, BF16, row), (128, BF16, row), (256, BF16, row), (256, BF16, row),
            (LANES, BF16, col), (LANES, BF16, col),
            (128, F32, row), (512, F32, row), (1024, BF16, row), (1024, BF16, row)]
    ospec = lambda wd, m: pl.BlockSpec((tm, wd), m) if m is row else pl.BlockSpec((wd, tm), m)
    oshape = lambda wd, dt, m: jax.ShapeDtypeStruct((N_TOK, wd) if m is row else (wd, N_TOK), dt)
    return pl.pallas_call(
        _proj_kernel,
        grid=(N_TOK // tm,),
        in_specs=[pl.BlockSpec((tm, D_MODEL), row),
                  pl.BlockSpec((1, 6, D_MODEL), lambda i: (i // (SEQ // tm), 0, 0)),
                  pl.BlockSpec((1, D_MODEL), fix),
                  pl.BlockSpec((D_MODEL, _C_END), fix),
                  pl.BlockSpec((2 * LANES, D_MODEL), fix),
                  pl.BlockSpec((1, 512), fix), pl.BlockSpec((1, 256), fix), pl.BlockSpec((1, 256), fix),
                  pl.BlockSpec((512, 512), fix), pl.BlockSpec((256, 256), fix)],
        out_specs=[ospec(wd, m) for wd, _, m in outs],
        out_shape=[oshape(wd, dt, m) for wd, dt, m in outs],
        compiler_params=_cparams("arbitrary"),
        name="proj",
    )(x2, mod, g_norm1.reshape(1, -1), w, wvt,
      jnp.tile(q_gain, N_HEADS).reshape(1, -1), jnp.tile(ks_gain, 4).reshape(1, -1),
      jnp.tile(kw_gain, 4).reshape(1, -1), _block_ones(512), _block_ones(256))


def _compress_kernel(r_ref, pe_ref, w1_ref, w2_ref, bd_ref, gain_ref, o_ref, *, do_norm):
    r = r_ref[0].astype(F32)
    p0 = _dot((r + pe_ref[0]).astype(BF16), w1_ref[0])
    p1 = _dot((r + pe_ref[1]).astype(BF16), w1_ref[1])
    hid = p0 + pltpu.roll(p1, N_CMP - 1, 0)
    c = _dot(_gelu(hid).astype(BF16), w2_ref[...])
    if do_norm:
        c = _seg_rms(c, bd_ref[...], gain_ref[...])
    o_ref[0] = c.astype(BF16)


def _compress(raw, pe, w1, w2, gain, do_norm):
    r = raw.reshape(BATCH, SEQ // CMP_STRIDE, CMP_STRIDE * LANES)
    eye = jnp.eye(N_KV, dtype=F32)
    w1r = w1.reshape(2, CMP_STRIDE, HEAD_DIM, CMP_HIDDEN)
    w1big = jnp.einsum('hldc,gk->hlgdkc', w1r, eye).reshape(2, CMP_STRIDE * LANES, N_KV * CMP_HIDDEN).astype(BF16)
    w2big = jnp.einsum('cd,gk->gckd', w2, eye)
    w2big = jnp.concatenate([w2big, w2big], axis=-1).reshape(N_KV * CMP_HIDDEN, 4 * HEAD_DIM).astype(BF16)
    pe_big = jnp.broadcast_to(pe.reshape(2, CMP_STRIDE, 1, HEAD_DIM), (2, CMP_STRIDE, N_KV, HEAD_DIM))
    pe_big = pe_big.reshape(2, 1, CMP_STRIDE * LANES)
    fix2 = lambda b: (0, 0)
    fix3 = lambda b: (0, 0, 0)
    return pl.pallas_call(
        functools.partial(_compress_kernel, do_norm=do_norm),
        grid=(BATCH,),
        in_specs=[pl.BlockSpec((1, N_CMP, CMP_STRIDE * LANES), lambda b: (b, 0, 0)),
                  pl.BlockSpec((2, 1, CMP_STRIDE * LANES), fix3),
                  pl.BlockSpec((2, CMP_STRIDE * LANES, N_KV * CMP_HIDDEN), fix3),
                  pl.BlockSpec((N_KV * CMP_HIDDEN, 256), fix2),
                  pl.BlockSpec((256, 256), fix2), pl.BlockSpec((1, 256), fix2)],
        out_specs=pl.BlockSpec((1, N_CMP, 256), lambda b: (b, 0, 0)),
        out_shape=jax.ShapeDtypeStruct((BATCH, N_CMP, 256), BF16),
        compiler_params=_cparams("arbitrary"),
        name="compress_k" if do_norm else "compress_v",
    )(r, pe_big, w1big, w2big, _block_ones(256), jnp.tile(gain, 4).reshape(1, -1))


ATT_TQ = 256
RANK_CHUNK = 16


def _head_variants(qb):
    lane = lax.broadcasted_iota(I32, qb.shape, 1)
    z = jnp.zeros_like(qb)
    return jnp.where(lane < HEAD_DIM, qb, z), jnp.where(lane < HEAD_DIM, z, qb)


def _cmp_kernel(q_ref, kc_ref, vc_ref, ov_ref, o_ref, sel_ref, vrank_ref):
    tq = ATT_TQ
    qi = pl.program_id(1)
    tpos = qi * tq + lax.broadcasted_iota(I32, (tq, N_CMP), 0)
    nidx = lax.broadcasted_iota(I32, (tq, N_CMP), 1)
    mask = (CMP_STRIDE * nidx + (CMP_BLOCK - 1)) <= tpos
    lane_lo = lax.broadcasted_iota(I32, (tq, LANES), 1) < HEAD_DIM
    for g in range(N_KV):
        kd = kc_ref[0, :, g * LANES:(g + 1) * LANES]
        vd = vc_ref[0, :, g * LANES:(g + 1) * LANES]
        psum = jnp.zeros((tq, N_CMP), F32)
        for jb in range(2):
            blk = 2 * g + jb
            pv = []
            for qv in _head_variants(q_ref[:, blk * LANES:(blk + 1) * LANES]):
                s = jnp.where(mask, _dot_nt(qv, kd), NEG)
                m = jnp.max(s, axis=-1, keepdims=True)
                e = jnp.where(mask, jnp.exp2(s - m), 0.0)
                l = jnp.sum(e, axis=-1, keepdims=True)
                p = e / jnp.where(l > 0.0, l, 1.0)
                psum = psum + p
                pv.append(_dot(p.astype(BF16), vd))
            o_ref[:, blk * LANES:(blk + 1) * LANES] = jnp.where(lane_lo, pv[0], pv[1]).astype(BF16)
        imp = _split_dot(psum, ov_ref[...])
        imp_t = imp.T[:N_SEL_BLOCKS]
        j = lax.broadcasted_iota(I32, (N_SEL_BLOCKS, tq), 0)
        cur = jnp.right_shift(qi * tq + lax.broadcasted_iota(I32, (N_SEL_BLOCKS, tq), 1), 6)
        forced = (j == 0) | (j == cur) | (j == cur - 1)
        v = jnp.where(forced, jnp.inf, jnp.where(j <= cur, imp_t, -jnp.inf))
        vrank_ref[...] = jnp.zeros((N_SEL_BLOCKS, tq), F32)
        n_live = (qi + 1) * (tq // SEL_BLOCK)
        for c0 in range(0, N_SEL_BLOCKS, RANK_CHUNK):
            @pl.when(c0 < n_live)
            def _():
                rank = vrank_ref[...]
                for jp in range(c0, c0 + RANK_CHUNK):
                    row = v[jp:jp + 1, :]
                    tie = jnp.where(j > jp, 1.0, 0.0)
                    rank = rank + jnp.where(row > v, 1.0, jnp.where(row == v, tie, 0.0))
                vrank_ref[...] = rank
        rank = vrank_ref[...]
        sel_ref[g * N_SEL_BLOCKS:(g + 1) * N_SEL_BLOCKS, :] = jnp.where(rank < float(N_SELECT), 0.0, NEG)


def _cmp_attn(q, kcn, vcn):
    nc = np.arange(N_CMP)
    sb = np.arange(LANES)
    ov = ((CMP_STRIDE * nc[:, None] < SEL_BLOCK * sb[None, :] + SEL_BLOCK)
          & (CMP_STRIDE * nc[:, None] + CMP_BLOCK > SEL_BLOCK * sb[None, :])
          & (nc[:, None] < N_CMP - 1) & (sb[None, :] < N_SEL_BLOCKS))
    ov = jnp.asarray(ov, BF16)
    tq = ATT_TQ
    nq = SEQ // tq
    row = lambda b, i: (b * nq + i, 0)
    return pl.pallas_call(
        _cmp_kernel,
        grid=(BATCH, nq),
        in_specs=[pl.BlockSpec((tq, NSA_WIDTH), row),
                  pl.BlockSpec((1, N_CMP, 256), lambda b, i: (b, 0, 0)),
                  pl.BlockSpec((1, N_CMP, 256), lambda b, i: (b, 0, 0)),
                  pl.BlockSpec((N_CMP, LANES), lambda b, i: (0, 0))],
        out_specs=[pl.BlockSpec((tq, NSA_WIDTH), row),
                   pl.BlockSpec((N_KV * N_SEL_BLOCKS, tq), lambda b, i: (0, b * nq + i))],
        out_shape=[jax.ShapeDtypeStruct((N_TOK, NSA_WIDTH), BF16),
                   jax.ShapeDtypeStruct((N_KV * N_SEL_BLOCKS, N_TOK), F32)],
        scratch_shapes=[pltpu.VMEM((N_SEL_BLOCKS, tq), F32)],
        compiler_params=_cparams("arbitrary", "arbitrary"),
        name="cmp_attn",
    )(q, kcn, vcn, ov)


ATT_TK = 256


M_INIT = -1e29


SUM_ROWS = 16


def _selwin_kernel(q_ref, ks_ref, kw_ref, vst_ref, vwt_ref, selb_ref, osel_ref, owin_ref, m_ref, acc_ref):
    tq, tk = ATT_TQ, ATT_TK
    qi = pl.program_id(1)
    krow = lax.broadcasted_iota(I32, (tk, tq), 0)
    qcol = lax.broadcasted_iota(I32, (tk, tq), 1)
    causal_bias = jnp.where(krow <= qcol, 0.0, NEG)
    far_bias = jnp.where(qcol < krow, 0.0, NEG)

    ones_rows = jnp.ones((SUM_ROWS, tk), BF16)

    def reset():
        m_ref[...] = jnp.full(m_ref.shape, M_INIT, F32)
        acc_ref[...] = jnp.zeros(acc_ref.shape, F32)

    def update(g, k_ref, vt_ref, kt, bias):
        k0 = pl.multiple_of(kt * tk, tk)
        kd = k_ref[0, pl.ds(k0, tk), g * LANES:(g + 1) * LANES]
        vt = vt_ref[g * HEAD_DIM:(g + 1) * HEAD_DIM, pl.ds(k0, tk)]
        s = _dot_nt(kd, qvars[g])
        if bias is not None:
            s = s + jnp.concatenate([bias] * 4, axis=1)
        m_old = m_ref[g]
        m_new = jnp.maximum(m_old, jnp.max(s, axis=0, keepdims=True))
        alpha = jnp.exp2(m_old - m_new)
        p = jnp.exp2(s - m_new)
        m_ref[g] = m_new
        vte = jnp.concatenate([vt, ones_rows], axis=0)
        acc_ref[g] = alpha * acc_ref[g] + _dot(vte, p.astype(BF16))

    def finish(out_ref, g):
        o = acc_ref[g, :HEAD_DIM, :] / acc_ref[g, HEAD_DIM:HEAD_DIM + 1, :]
        for jb in range(2):
            blk = 2 * g + jb
            pair = jnp.concatenate([o[:, 2 * jb * tq:(2 * jb + 1) * tq], o[:, (2 * jb + 1) * tq:(2 * jb + 2) * tq]],
                                   axis=0)
            out_ref[:, blk * LANES:(blk + 1) * LANES] = pair.T.astype(BF16)

    def sel_bias(g, kt):
        rows = [jnp.broadcast_to(selb_ref[pl.ds(g * N_SEL_BLOCKS + kt * (tk // SEL_BLOCK) + r, 1), :],
                                 (SEL_BLOCK, tq)) for r in range(tk // SEL_BLOCK)]
        return jnp.concatenate(rows, axis=0)

    qvars = []
    for g in range(N_KV):
        heads = []
        for jb in range(2):
            heads.extend(_head_variants(q_ref[:, (2 * g + jb) * LANES:(2 * g + jb + 1) * LANES]))
        qvars.append(jnp.concatenate(heads, axis=0))
    groups = range(N_KV)

    reset()

    def sel_step(kt, carry):
        for g in groups:
            update(g, ks_ref, vst_ref, kt, sel_bias(g, kt))
        return carry

    lax.fori_loop(0, qi, sel_step, 0)
    for g in groups:
        update(g, ks_ref, vst_ref, qi, sel_bias(g, qi) + causal_bias)
    for g in groups:
        finish(osel_ref, g)

    reset()

    @pl.when(qi >= 2)
    def _():
        for g in groups:
            update(g, kw_ref, vwt_ref, qi - 2, far_bias)

    @pl.when(qi >= 1)
    def _():
        for g in groups:
            update(g, kw_ref, vwt_ref, qi - 1, None)

    for g in groups:
        update(g, kw_ref, vwt_ref, qi, causal_bias)
    for g in groups:
        finish(owin_ref, g)


def _selwin(q, ks, kw, vst, vwt, selb):
    tq = ATT_TQ
    nq = SEQ // tq
    assert WINDOW == 2 * ATT_TK and ATT_TQ == ATT_TK
    row = lambda b, i: (b * nq + i, 0)
    keys = pl.BlockSpec((1, SEQ, 256), lambda b, i: (b, 0, 0))
    vals = pl.BlockSpec((LANES, SEQ), lambda b, i: (0, b))
    r3 = lambda a: a.reshape(BATCH, SEQ, 256)
    return pl.pallas_call(
        _selwin_kernel,
        grid=(BATCH, nq),
        in_specs=[pl.BlockSpec((tq, NSA_WIDTH), row), keys, keys, vals, vals,
                  pl.BlockSpec((N_KV * N_SEL_BLOCKS, tq), lambda b, i: (0, b * nq + i))],
        out_specs=[pl.BlockSpec((tq, NSA_WIDTH), row)] * 2,
        out_shape=[jax.ShapeDtypeStruct((N_TOK, NSA_WIDTH), BF16)] * 2,
        scratch_shapes=[pltpu.VMEM((N_KV, 1, 4 * tq), F32),
                        pltpu.VMEM((N_KV, HEAD_DIM + SUM_ROWS, 4 * tq), F32)],
        compiler_params=_cparams("arbitrary", "arbitrary"),
        name="selwin",
    )(q, r3(ks), r3(kw), vst, vwt, selb)


def _s5_param_kernel(are_ref, aim_ref, ldt_ref, cre_ref, cim_ref, bre_ref, bim_ref,
                     clre_ref, clim_ref, wbre_ref, wbim_ref, bbre_ref, bbim_ref, ltre_ref, ltim_ref):
    are, aim = are_ref[...], aim_ref[...]
    dt = jnp.exp(ldt_ref[...])
    cre, cim = cre_ref[...], cim_ref[...]

    def lam_pow(tau):
        mag = jnp.exp(are * dt * float(tau))
        ang = aim * dt * float(tau)
        return mag * jnp.cos(ang), mag * jnp.sin(ang)

    lre, lim = lam_pow(1)
    den = are * are + aim * aim
    qre = ((lre - 1.0) * are + lim * aim) / den
    qim = (lim * are - (lre - 1.0) * aim) / den
    bre, bim = bre_ref[...], bim_ref[...]
    bbre = qre * bre - qim * bim
    bbim = qre * bim + qim * bre
    bbre_ref[...] = bbre
    bbim_ref[...] = bbim
    for tau in range(S5_T + 1):
        pr, pi = lam_pow(tau)
        clre_ref[tau] = cre * pr - cim * pi
        clim_ref[tau] = cre * pi + cim * pr
        if tau < S5_T:
            k = S5_T - 1 - tau
            wbre_ref[k] = pr * bbre - pi * bbim
            wbim_ref[k] = pr * bbim + pi * bbre
        else:
            ltre_ref[...] = pr
            ltim_ref[...] = pi


def _s5_kmat_kernel(l_ref, r_ref, o_ref):
    o_ref[0] = jnp.dot(l_ref[0], r_ref[0], preferred_element_type=F32, precision=HIGHEST)


def _s5_params(a_re, a_im, log_dt, b_re, b_im, c_re, c_im):
    T = S5_T
    pn = GROUP * STATE
    tile_p = lambda a: jnp.tile(a, (1, GROUP))
    args = (tile_p(a_re), tile_p(a_im), jnp.broadcast_to(log_dt[:, None], (N_GROUPS, pn)),
            c_re.reshape(N_GROUPS, pn), c_im.reshape(N_GROUPS, pn),
            jnp.swapaxes(b_re, 1, 2).reshape(N_GROUPS, pn), jnp.swapaxes(b_im, 1, 2).reshape(N_GROUPS, pn))
    full2 = pl.BlockSpec((N_GROUPS, pn), lambda: (0, 0))
    clre, clim, wbre, wbim, bbre, bbim, ltre, ltim = pl.pallas_call(
        _s5_param_kernel,
        in_specs=[full2] * 7,
        out_specs=[pl.BlockSpec((T + 1, N_GROUPS, pn), lambda: (0, 0, 0))] * 2
                  + [pl.BlockSpec((T, N_GROUPS, pn), lambda: (0, 0, 0))] * 2 + [full2] * 4,
        out_shape=[jax.ShapeDtypeStruct((T + 1, N_GROUPS, pn), F32)] * 2
                  + [jax.ShapeDtypeStruct((T, N_GROUPS, pn), F32)] * 2
                  + [jax.ShapeDtypeStruct((N_GROUPS, pn), F32)] * 4,
        name="s5_params",
    )(*args)

    r5 = lambda a, t: a[:t].reshape(t, N_GROUPS, GROUP, STATE)
    lhs = jnp.concatenate([r5(clre, T), -r5(clim, T)], axis=-1)
    lhs = jnp.transpose(lhs, (1, 0, 2, 3)).reshape(N_GROUPS, T * GROUP, 2 * STATE)
    bb = lambda a: jnp.swapaxes(a.reshape(N_GROUPS, GROUP, STATE), 1, 2)
    rhs = jnp.concatenate([bb(bbre), bb(bbim)], axis=1)
    kmat = pl.pallas_call(
        _s5_kmat_kernel,
        grid=(N_GROUPS,),
        in_specs=[pl.BlockSpec((1, T * GROUP, 2 * STATE), lambda g: (g, 0, 0)),
                  pl.BlockSpec((1, 2 * STATE, GROUP), lambda g: (g, 0, 0))],
        out_specs=pl.BlockSpec((1, T * GROUP, GROUP), lambda g: (g, 0, 0)),
        out_shape=jax.ShapeDtypeStruct((N_GROUPS, T * GROUP, GROUP), F32),
        compiler_params=_cparams("arbitrary"),
        name="s5_kmat",
    )(lhs, rhs)

    eye = jnp.eye(S5_GL, dtype=F32)
    kt = kmat.reshape(S5_SG, S5_GL, T, GROUP, GROUP)
    kbd = jnp.einsum('sgtpq,gh->stgqhp', kt, eye).reshape(S5_SG, T, LANES, LANES)
    krev = kbd[:, ::-1].reshape(S5_SG, T * LANES, LANES).astype(BF16)
    r6 = lambda a: a.reshape(T, S5_SG, S5_GL, GROUP, STATE)
    wb = jnp.stack([r6(wbre), r6(wbim)], axis=-2)
    wb = jnp.einsum('ksgpin,gh->skgpihn', wb, eye).reshape(S5_SG, T * LANES, S5_NSTATE).astype(BF16)
    wc = jnp.stack([r6(clre[1:]), -r6(clim[1:])], axis=-2)
    wc = jnp.einsum('tsgpin,gh->signthp', wc, eye).reshape(S5_SG, S5_NSTATE, T * LANES).astype(BF16)
    lt = lambda a: a.reshape(N_GROUPS, GROUP, STATE)[:, 0].reshape(S5_SG, 1, S5_GL * STATE)
    return krev, wb, wc, lt(ltre), lt(ltim)


S5_TC = 256


def _s5_lane_block(sg, t):
    return pl.ds(pl.multiple_of(t * SSM_WIDTH + sg * LANES, LANES), LANES)


def _s5_chunk_inputs(x_ref, sg):
    return jnp.concatenate([x_ref[:, _s5_lane_block(sg, t)] for t in range(S5_T)], axis=1).astype(BF16)


def _s5_state_kernel(x_ref, wb_ref, e_ref):
    e_ref[0] = _dot(_s5_chunk_inputs(x_ref, pl.program_id(1)), wb_ref[0])


def _s5_scan_kernel(e_ref, ltre_ref, ltim_ref, xs_ref):
    lr, li = ltre_ref[0], ltim_ref[0]
    half = S5_NSTATE // 2

    def step(c, carry):
        new = []
        for b in range(BATCH):
            xr, xi = carry[b]
            row = b * S5_CH_PER_BATCH + c
            xs_ref[0, pl.ds(row, 1), :half] = xr
            xs_ref[0, pl.ds(row, 1), half:] = xi
            e = e_ref[0, pl.ds(row, 1), :]
            new.append((lr * xr - li * xi + e[:, :half], lr * xi + li * xr + e[:, half:]))
        return tuple(new)

    zero = jnp.zeros((1, half), F32)
    lax.fori_loop(0, S5_CH_PER_BATCH, step, tuple((zero, zero) for _ in range(BATCH)))


def _s5_out_kernel(x_ref, xs_ref, krev_ref, wc_ref, y_ref):
    sg = pl.program_id(1)
    x = _s5_chunk_inputs(x_ref, sg)
    xsb = xs_ref[0].astype(BF16)
    for t in range(S5_T):
        cols = slice(t * LANES, (t + 1) * LANES)
        y_ref[:, _s5_lane_block(sg, t)] = (_dot(x[:, :(t + 1) * LANES], krev_ref[0, (S5_T - 1 - t) * LANES:, :])
                                           + _dot(xsb, wc_ref[0, :, cols])).astype(BF16)


def _s5(u, krev, wb, wc, ltre, ltim):
    T, tc = S5_T, S5_TC
    xn = u.reshape(S5_CH, T * SSM_WIDTH)
    grid = (S5_CH // tc, S5_SG)
    natural = pl.BlockSpec((tc, T * SSM_WIDTH), lambda i, s: (i, 0))
    rows = lambda i, s: (s, i, 0)
    per_sg = lambda i, s: (s, 0, 0)
    e = pl.pallas_call(
        _s5_state_kernel, grid=grid,
        in_specs=[natural, pl.BlockSpec((1, T * LANES, S5_NSTATE), per_sg)],
        out_specs=pl.BlockSpec((1, tc, S5_NSTATE), rows),
        out_shape=jax.ShapeDtypeStruct((S5_SG, S5_CH, S5_NSTATE), F32),
        compiler_params=_cparams("arbitrary", "arbitrary"), name="s5_state",
    )(xn, wb)
    sg1 = lambda s: (s, 0, 0)
    xstart = pl.pallas_call(
        _s5_scan_kernel, grid=(S5_SG,),
        in_specs=[pl.BlockSpec((1, S5_CH, S5_NSTATE), sg1),
                  pl.BlockSpec((1, 1, S5_NSTATE // 2), sg1), pl.BlockSpec((1, 1, S5_NSTATE // 2), sg1)],
        out_specs=pl.BlockSpec((1, S5_CH, S5_NSTATE), sg1),
        out_shape=jax.ShapeDtypeStruct((S5_SG, S5_CH, S5_NSTATE), F32),
        compiler_params=_cparams("arbitrary"), name="s5_scan",
    )(e, ltre, ltim)
    y = pl.pallas_call(
        _s5_out_kernel, grid=grid,
        in_specs=[natural, pl.BlockSpec((1, tc, S5_NSTATE), rows),
                  pl.BlockSpec((1, T * LANES, LANES), per_sg), pl.BlockSpec((1, S5_NSTATE, T * LANES), per_sg)],
        out_specs=natural,
        out_shape=jax.ShapeDtypeStruct((S5_CH, T * SSM_WIDTH), BF16),
        compiler_params=_cparams("arbitrary", "arbitrary"), name="s5_out",
    )(xn, xstart, krev, wc)
    return y.reshape(N_TOK, SSM_WIDTH)


MERGE_TM = 256


def _merge_kernel(ocmp_ref, osel_ref, owin_ref, gn_ref, yssm_ref, u_ref, ga_ref, gs_ref, x_ref, mod_ref,
                  eg_ref, dskip_ref, wglu_ref, bglu_ref, wua_ref, wus_ref, wout_ref, g2_ref,
                  wrhi_ref, wrlo_ref, wsgu_ref, wsd_ref,
                  xpart_ref, h2_ref, logit_ref):
    mod = mod_ref[0]
    gnb = gn_ref[...].astype(BF16)
    o_nsa = (_dot(gnb, eg_ref[0]) * ocmp_ref[...].astype(F32)
             + _dot(gnb, eg_ref[1]) * osel_ref[...].astype(F32)
             + _dot(gnb, eg_ref[2]) * owin_ref[...].astype(F32))
    attn = _dot(o_nsa.astype(BF16), wua_ref[...])
    z = _gelu(yssm_ref[...] + dskip_ref[...] * u_ref[...])
    y_ssm = z * jax.nn.sigmoid(_dot(z.astype(BF16), wglu_ref[...]) + bglu_ref[...])
    ssm = _dot(y_ssm.astype(BF16), wus_ref[...])
    merged = ga_ref[...].astype(F32) * attn + gs_ref[...].astype(F32) * ssm
    x1 = x_ref[...] + mod[2:3] * _dot(merged.astype(BF16), wout_ref[...])

    ms = jnp.mean(x1 * x1, axis=-1, keepdims=True)
    h2 = (x1 * lax.rsqrt(ms + EPS) * g2_ref[...]) * (1.0 + mod[4:5]) + mod[3:4]
    hi = h2.astype(BF16)
    lo = (h2 - hi.astype(F32)).astype(BF16)
    h2_ref[...] = _pack_bf16_pairs(h2)
    logit_ref[...] = _dot_nt(wrhi_ref[...], hi) + _dot_nt(wrhi_ref[...], lo) + _dot_nt(wrlo_ref[...], hi)
    gu = _dot(hi, wsgu_ref[...])
    shared = _dot((_silu(gu[:, :D_EXPERT]) * gu[:, D_EXPERT:]).astype(BF16), wsd_ref[...])
    xpart_ref[...] = x1 + mod[5:6] * shared


def _merge(ocmp, osel, owin, gn, yssm, u, ga, gs, x2, mod, d_skip, w_glu, b_glu, w_up_attn, w_up_ssm, w_out,
           g_norm2, w_router, ws_gate, ws_up, ws_down):
    tm = MERGE_TM
    eg = np.zeros((3, LANES, NSA_WIDTH), np.float32)
    for j in range(3):
        for h in range(N_HEADS):
            eg[j, 3 * h + j, h * HEAD_DIM:(h + 1) * HEAD_DIM] = 1.0
    wr_t = w_router.T
    wr_hi = wr_t.astype(BF16)
    wr_lo = (wr_t - wr_hi.astype(F32)).astype(BF16)
    row = lambda i: (i, 0)
    fix2 = lambda i: (0, 0)
    wspec = lambda a: pl.BlockSpec(a.shape, (lambda i: (0,) * a.ndim))
    weights = [jnp.asarray(eg, BF16), d_skip.reshape(1, -1), w_glu.astype(BF16), b_glu.reshape(1, -1),
               w_up_attn.astype(BF16), w_up_ssm.astype(BF16), w_out.astype(BF16), g_norm2.reshape(1, -1),
               wr_hi, wr_lo, jnp.concatenate([ws_gate, ws_up], axis=1).astype(BF16), ws_down.astype(BF16)]
    acts = [(ocmp, 512), (osel, 512), (owin, 512), (gn, 128), (yssm, 512), (u, 512), (ga, 1024), (gs, 1024),
            (x2, 1024)]
    return pl.pallas_call(
        _merge_kernel,
        grid=(N_TOK // tm,),
        in_specs=[pl.BlockSpec((tm, wd), row) for _, wd in acts]
                 + [pl.BlockSpec((1, 6, D_MODEL), lambda i: (i // (SEQ // tm), 0, 0))]
                 + [wspec(w) for w in weights],
        out_specs=[pl.BlockSpec((tm, D_MODEL), row), pl.BlockSpec((tm, HALF), row),
                   pl.BlockSpec((N_EXPERTS, tm), lambda i: (0, i))],
        out_shape=[jax.ShapeDtypeStruct((N_TOK, D_MODEL), F32), jax.ShapeDtypeStruct((N_TOK, HALF), jnp.uint32),
                   jax.ShapeDtypeStruct((N_EXPERTS, N_TOK), F32)],
        compiler_params=_cparams("arbitrary"),
        name="merge",
    )(*[a for a, _ in acts], mod, *weights)


ROUTE_TN = 512


def _route_kernel(logit_ref, bias_ref, eidx_ref, w_ref, count_ref, gscore_ref, masked_ref):
    tn = ROUTE_TN

    @pl.when(pl.program_id(0) == 0)
    def _():
        count_ref[...] = jnp.zeros(count_ref.shape, F32)

    sc = jax.nn.sigmoid(logit_ref[...])
    biased = sc + bias_ref[...]
    gi = lax.broadcasted_iota(I32, (EXPERTS_PER_GROUP, tn), 0).astype(F32)
    for g in range(N_EXPERT_GROUPS):
        blk = biased[g * EXPERTS_PER_GROUP:(g + 1) * EXPERTS_PER_GROUP]
        m1 = jnp.max(blk, axis=0, keepdims=True)
        i1 = jnp.min(jnp.where(blk == m1, gi, float(EXPERTS_PER_GROUP)), axis=0, keepdims=True)
        m2 = jnp.max(jnp.where(gi == i1, -jnp.inf, blk), axis=0, keepdims=True)
        gscore_ref[g:g + 1, :] = m1 + m2
    gs = gscore_ref[...]
    gidx = lax.broadcasted_iota(I32, (N_EXPERT_GROUPS, tn), 0)
    grank = jnp.zeros((N_EXPERT_GROUPS, tn), F32)
    for gp in range(N_EXPERT_GROUPS):
        row = gs[gp:gp + 1, :]
        tie = jnp.where(gidx > gp, 1.0, 0.0)
        grank = grank + jnp.where(row > gs, 1.0, jnp.where(row == gs, tie, 0.0))
    for g in range(N_EXPERT_GROUPS):
        keep = grank[g:g + 1, :] < float(TOPK_GROUPS)
        sl = slice(g * EXPERTS_PER_GROUP, (g + 1) * EXPERTS_PER_GROUP)
        masked_ref[sl, :] = jnp.where(keep, biased[sl], -jnp.inf)
    cur = masked_ref[...]
    eidx = lax.broadcasted_iota(I32, (N_EXPERTS, tn), 0).astype(F32)
    wsum = jnp.zeros((1, tn), F32)
    hits = jnp.zeros((N_EXPERTS, tn), F32)
    for k in range(TOP_K):
        m = jnp.max(cur, axis=0, keepdims=True)
        idx = jnp.min(jnp.where(cur == m, eidx, float(N_EXPERTS)), axis=0, keepdims=True)
        hit = eidx == idx
        wk = jnp.sum(jnp.where(hit, sc, 0.0), axis=0, keepdims=True)
        cur = jnp.where(hit, -jnp.inf, cur)
        hits = hits + jnp.where(hit, 1.0, 0.0)
        eidx_ref[k:k + 1, :] = idx.astype(I32)
        w_ref[k:k + 1, :] = wk
        wsum = wsum + wk
    w_ref[...] = w_ref[...] / wsum * ROUTE_SCALE
    count_ref[...] = count_ref[...] + jnp.sum(hits, axis=1, keepdims=True)


def _route(logits_t, router_bias):
    tn = ROUTE_TN
    return pl.pallas_call(
        _route_kernel,
        grid=(N_TOK // tn,),
        in_specs=[pl.BlockSpec((N_EXPERTS, tn), lambda i: (0, i)), pl.BlockSpec((N_EXPERTS, 1), lambda i: (0, 0))],
        out_specs=[pl.BlockSpec((TOP_K, tn), lambda i: (0, i))] * 2 + [pl.BlockSpec((N_EXPERTS, 1), lambda i: (0, 0))],
        out_shape=[jax.ShapeDtypeStruct((TOP_K, N_TOK), I32), jax.ShapeDtypeStruct((TOP_K, N_TOK), F32),
                   jax.ShapeDtypeStruct((N_EXPERTS, 1), F32)],
        scratch_shapes=[pltpu.VMEM((N_EXPERT_GROUPS, tn), F32), pltpu.VMEM((N_EXPERTS, tn), F32)],
        compiler_params=_cparams("arbitrary"),
        name="route",
    )(logits_t, router_bias.reshape(-1, 1))


N_MOE_BLK = NK // DISPATCH_BLOCK
N_ITEMS = N_MOE_BLK + N_EXPERTS
ASSIGN_BITS = 17


def _dispatch_plan(eidx, counts):
    e_flat = eidx.reshape(-1)
    key = jnp.sort(e_flat * NK + jnp.arange(NK, dtype=I32))
    order = key & (NK - 1)
    counts = counts.reshape(-1).astype(I32)
    start = jnp.cumsum(counts) - counts
    cuts = jnp.sort(jnp.concatenate([jnp.arange(N_MOE_BLK, dtype=I32) * DISPATCH_BLOCK, start]))
    lo = cuts
    hi = jnp.concatenate([cuts[1:], jnp.full((1,), NK, I32)])
    blk = jnp.minimum(lo // DISPATCH_BLOCK, N_MOE_BLK - 1)
    expert = jnp.clip(jnp.sum((start[None, :] <= lo[:, None]).astype(I32), axis=1) - 1, 0, N_EXPERTS - 1)
    one = jnp.ones((1,), I32)
    first = jnp.concatenate([one, (blk[1:] != blk[:-1]).astype(I32)])
    last = jnp.concatenate([(blk[1:] != blk[:-1]).astype(I32), one])
    new_expert = jnp.concatenate([one, (expert[1:] != expert[:-1]).astype(I32)])
    run_id = jnp.cumsum(new_expert) - 1
    n_runs = run_id[-1] + 1
    item = jnp.arange(N_ITEMS, dtype=I32)
    run_first_item = jnp.sort(jnp.where(new_expert == 1, item, N_ITEMS))
    run_expert = expert[jnp.minimum(run_first_item, N_ITEMS - 1)]
    ahead = run_id + (WEIGHT_RING - 1)
    ahead_expert = run_expert[jnp.minimum(ahead, N_ITEMS - 1)]
    ahead_valid = (ahead < n_runs).astype(I32)
    second_expert = run_expert[1:2]
    prologue = jnp.concatenate([second_expert, (n_runs > 1).astype(I32).reshape(1)])
    tok = jnp.right_shift(order, 3)
    home = (order & (TOP_K - 1)) * N_TOK + tok
    return tok, home, (blk, expert, lo - blk * DISPATCH_BLOCK, hi - blk * DISPATCH_BLOCK, first, last, new_expert,
                      run_id % WEIGHT_RING, ahead_expert, ahead_valid, prologue)


SC_CORES = 2
SC_SUBCORES = 16
SC_CHUNK = 128


def _sc_move_rows(table, idx, scatter):
    n = idx.shape[0]
    workers = SC_CORES * SC_SUBCORES
    per_worker = n // workers
    n_chunks = per_worker // SC_CHUNK
    assert per_worker * workers == n and n_chunks * SC_CHUNK == per_worker
    mesh = plsc.VectorSubcoreMesh(core_axis_name="c", subcore_axis_name="s",
                                  num_cores=SC_CORES, num_subcores=SC_SUBCORES)

    def body(table_hbm, idx_hbm, out_hbm, idx_v, rows_v, sem):
        wid = lax.axis_index("s") * SC_CORES + lax.axis_index("c")
        base = wid * per_worker

        @pl.loop(0, n_chunks)
        def _(j):
            off = base + j * SC_CHUNK
            pltpu.sync_copy(idx_hbm.at[pl.ds(off, SC_CHUNK)], idx_v)
            if scatter:
                pltpu.sync_copy(table_hbm.at[pl.ds(off, SC_CHUNK)], rows_v)
                pltpu.async_copy(rows_v, out_hbm.at[idx_v], sem).wait()
            else:
                pltpu.async_copy(table_hbm.at[idx_v], rows_v, sem).wait()
                pltpu.sync_copy(rows_v, out_hbm.at[pl.ds(off, SC_CHUNK)])

    return pl.kernel(
        body,
        out_type=jax.ShapeDtypeStruct((n, table.shape[1]), table.dtype),
        mesh=mesh,
        scratch_types=[pltpu.VMEM((SC_CHUNK,), I32), pltpu.VMEM((SC_CHUNK, table.shape[1]), table.dtype),
                       pltpu.SemaphoreType.DMA],
        name="sc_scatter_rows" if scatter else "sc_gather_rows",
    )(table, idx)


WEIGHT_RING = 3
WEIGHT_CHUNKS = 4


def _expert_weight_copies(w_hbm, wbuf, sem, expert, slot):
    rows = w_hbm.shape[1] // WEIGHT_CHUNKS
    return [pltpu.make_async_copy(w_hbm.at[expert, pl.ds(c * rows, rows)],
                                  wbuf.at[slot, pl.ds(c * rows, rows)], sem.at[slot])
            for c in range(WEIGHT_CHUNKS)]


def _moe_kernel(blk_ref, exp_ref, lo_ref, hi_ref, first_ref, last_ref, newexp_ref,
                slot_ref, ahead_exp_ref, ahead_ok_ref, prologue_ref,
                x_ref, wg_hbm, wu_hbm, wd_hbm, y_ref,
                acc_ref, wgf_ref, wuf_ref, wdf_ref, wgb_ref, wub_ref, wdb_ref, wsem):
    it = pl.program_id(0)
    lo, hi = lo_ref[it], hi_ref[it]
    streams = ((wg_hbm, wgf_ref), (wu_hbm, wuf_ref), (wd_hbm, wdf_ref))

    def request(expert, slot):
        for w_hbm, wbuf in streams:
            for cp in _expert_weight_copies(w_hbm, wbuf, wsem, expert, slot):
                cp.start()

    @pl.when(it == 0)
    def _():
        request(exp_ref[0], 0)

        @pl.when(prologue_ref[1] == 1)
        def _():
            request(prologue_ref[0], 1)

    @pl.when(newexp_ref[it] == 1)
    def _():
        slot = slot_ref[it]
        for w_hbm, wbuf in streams:
            for cp in _expert_weight_copies(w_hbm, wbuf, wsem, 0, slot):
                cp.wait()
        wgb_ref[...] = wgf_ref[slot].astype(BF16)
        wub_ref[...] = wuf_ref[slot].astype(BF16)
        wdb_ref[...] = wdf_ref[slot].astype(BF16)

        @pl.when(ahead_ok_ref[it] == 1)
        def _():
            ahead_slot = slot + (WEIGHT_RING - 1)
            request(ahead_exp_ref[it], jnp.where(ahead_slot >= WEIGHT_RING, ahead_slot - WEIGHT_RING, ahead_slot))

    @pl.when(first_ref[it] == 1)
    def _():
        acc_ref[...] = jnp.zeros(acc_ref.shape, F32)

    def expert_pass(r0, nrows):
        rows = slice(r0, r0 + nrows)
        ridx = r0 + lax.broadcasted_iota(I32, (nrows, HALF), 0)
        mine = (ridx >= lo) & (ridx < hi)
        xlo, xhi = _unpack_bf16_pairs(jnp.where(mine, x_ref[rows, :], jnp.uint32(0)))
        xlo, xhi = xlo.astype(BF16), xhi.astype(BF16)
        gate = _dot(xlo, wgb_ref[:HALF]) + _dot(xhi, wgb_ref[HALF:])
        up = _dot(xlo, wub_ref[:HALF]) + _dot(xhi, wub_ref[HALF:])
        acc_ref[rows, :] = acc_ref[rows, :] + _dot((_silu(gate) * up).astype(BF16), wdb_ref[...])

    mid = DISPATCH_BLOCK // 2
    pl.when((lo < mid) & (hi > mid))(lambda: expert_pass(0, DISPATCH_BLOCK))
    pl.when((hi > lo) & (hi <= mid))(lambda: expert_pass(0, mid))
    pl.when((hi > lo) & (lo >= mid))(lambda: expert_pass(mid, mid))

    @pl.when(last_ref[it] == 1)
    def _():
        y_ref[...] = _pack_bf16_pairs(acc_ref[...])


def _moe(xs, items, w_gate, w_up, w_down):
    by_blk = lambda it, blk, *_: (blk[it], 0)
    any_space = pl.BlockSpec(memory_space=pl.ANY)
    grid_spec = pltpu.PrefetchScalarGridSpec(
        num_scalar_prefetch=len(items),
        grid=(N_ITEMS,),
        in_specs=[pl.BlockSpec((DISPATCH_BLOCK, HALF), by_blk), any_space, any_space, any_space],
        out_specs=pl.BlockSpec((DISPATCH_BLOCK, HALF), by_blk),
        scratch_shapes=[pltpu.VMEM((DISPATCH_BLOCK, D_MODEL), F32),
                        pltpu.VMEM((WEIGHT_RING, D_MODEL, D_EXPERT), F32),
                        pltpu.VMEM((WEIGHT_RING, D_MODEL, D_EXPERT), F32),
                        pltpu.VMEM((WEIGHT_RING, D_EXPERT, D_MODEL), F32),
                        pltpu.VMEM((D_MODEL, D_EXPERT), BF16), pltpu.VMEM((D_MODEL, D_EXPERT), BF16),
                        pltpu.VMEM((D_EXPERT, D_MODEL), BF16),
                        pltpu.SemaphoreType.DMA((WEIGHT_RING,))],
    )
    return pl.pallas_call(
        _moe_kernel,
        grid_spec=grid_spec,
        out_shape=jax.ShapeDtypeStruct((NK, HALF), jnp.uint32),
        compiler_params=_cparams("arbitrary"),
        name="moe",
    )(*items, xs, w_gate, w_up, w_down)


COMB_TC = 256


def _combine_kernel(slots_ref, w_ref, xpart_ref, mod_ref, out_ref):
    w = w_ref[...]
    lo = jnp.zeros((w.shape[0], HALF), F32)
    hi = jnp.zeros((w.shape[0], HALF), F32)
    for k in range(TOP_K):
        klo, khi = _unpack_bf16_pairs(slots_ref[k])
        lo = lo + w[:, k:k + 1] * klo
        hi = hi + w[:, k:k + 1] * khi
    gate2 = mod_ref[0][5:6]
    out_ref[:, :HALF] = xpart_ref[:, :HALF] + gate2[:, :HALF] * lo
    out_ref[:, HALF:] = xpart_ref[:, HALF:] + gate2[:, HALF:] * hi


def _combine(xpart, mod, slots, w):
    tc = COMB_TC
    row = lambda i: (i, 0)
    return pl.pallas_call(
        _combine_kernel,
        grid=(N_TOK // tc,),
        in_specs=[pl.BlockSpec((TOP_K, tc, HALF), lambda i: (0, i, 0)),
                  pl.BlockSpec((tc, TOP_K), row),
                  pl.BlockSpec((tc, D_MODEL), row),
                  pl.BlockSpec((1, 6, D_MODEL), lambda i: (i // (SEQ // tc), 0, 0))],
        out_specs=pl.BlockSpec((tc, D_MODEL), row),
        out_shape=jax.ShapeDtypeStruct((N_TOK, D_MODEL), F32),
        compiler_params=_cparams("arbitrary"),
        name="combine",
    )(slots.reshape(TOP_K, N_TOK, HALF), w, xpart, mod)


def _layer(x, c, w_ada, b_ada, g_norm1, g_norm2, w_in, q_gain, kc_gain, ks_gain, kw_gain,
           pe_k, pe_v, w_cmp_k1, w_cmp_k2, w_cmp_v1, w_cmp_v2,
           a_re, a_im, log_dt, b_re, b_im, c_re, c_im, d_skip, w_glu, b_glu,
           w_up_attn, w_up_ssm, w_out, w_router, router_bias,
           w_gate, w_up, w_down, ws_gate, ws_up, ws_down):
    x2 = x.reshape(N_TOK, D_MODEL)
    mod = _ada(c, w_ada, b_ada)
    q, kc_raw, vc_raw, ks, kw, vst, vwt, gn, u, ga, gs = _proj(x2, mod, g_norm1, w_in, q_gain, ks_gain, kw_gain)
    kcn = _compress(kc_raw, pe_k, w_cmp_k1, w_cmp_k2, kc_gain, True)
    vcn = _compress(vc_raw, pe_v, w_cmp_v1, w_cmp_v2, kc_gain, False)
    ocmp, selb = _cmp_attn(q, kcn, vcn)
    osel, owin = _selwin(q, ks, kw, vst, vwt, selb)
    yssm = _s5(u, *_s5_params(a_re, a_im, log_dt, b_re, b_im, c_re, c_im))
    xpart, h2, logits_t = _merge(ocmp, osel, owin, gn, yssm, u, ga, gs, x2, mod, d_skip, w_glu, b_glu,
                                  w_up_attn, w_up_ssm, w_out, g_norm2, w_router, ws_gate, ws_up, ws_down)
    eidx_t, w_t, counts = _route(logits_t, router_bias)
    tok, home, items = _dispatch_plan(eidx_t.T, counts)
    y = _moe(_sc_move_rows(h2, tok, scatter=False), items, w_gate, w_up, w_down)
    slots = _sc_move_rows(y, home, scatter=True)
    return _combine(xpart, mod, slots, w_t.T).reshape(BATCH, SEQ, D_MODEL)


def kernel(x, c, w_ada, b_ada, g_norm1, g_norm2, w_in, q_gain, kc_gain, ks_gain, kw_gain, pe_k, pe_v, w_cmp_k1,
           w_cmp_k2, w_cmp_v1, w_cmp_v2, a_re, a_im, log_dt, b_re, b_im, c_re, c_im, d_skip, w_glu, b_glu,
           w_up_attn, w_up_ssm, w_out, w_router, router_bias, w_gate, w_up, w_down, ws_gate, ws_up, ws_down):
    params = (w_ada, b_ada, g_norm1, g_norm2, w_in, q_gain, kc_gain, ks_gain, kw_gain, pe_k, pe_v, w_cmp_k1,
              w_cmp_k2, w_cmp_v1, w_cmp_v2, a_re, a_im, log_dt, b_re, b_im, c_re, c_im, d_skip, w_glu, b_glu,
              w_up_attn, w_up_ssm, w_out, w_router, router_bias, w_gate, w_up, w_down, ws_gate, ws_up, ws_down)
    depth = w_ada.shape[0]
    for layer in range(depth):
        x = _layer(x, c, *[p[layer] for p in params])
    return x
```

```python
import functools
import math

import jax
import jax.numpy as jnp
import numpy as np
from jax import lax
from jax.experimental import pallas as pl
from jax.experimental.pallas import tpu as pltpu
from jax.experimental.pallas import tpu_sc as plsc

F32 = jnp.float32
BF16 = jnp.bfloat16
I32 = jnp.int32
HIGHEST = lax.Precision.HIGHEST

D_MODEL = 1024
BATCH = 4
SEQ = 4096
N_TOK = BATCH * SEQ
N_HEADS = 8
HEAD_DIM = 64
N_KV = 2
CMP_BLOCK = 32
CMP_STRIDE = 16
CMP_HIDDEN = 256
N_CMP = 256
SEL_BLOCK = 64
N_SEL_BLOCKS = SEQ // SEL_BLOCK
N_SELECT = 16
WINDOW = 512
ATTN_SCALE = HEAD_DIM ** -0.5
LOG2E = 1.4426950408889634
NSA_WIDTH = N_HEADS * HEAD_DIM
SSM_WIDTH = 512
GROUP = 16
N_GROUPS = SSM_WIDTH // GROUP
STATE = 64
N_EXPERTS = 256
TOP_K = 8
D_EXPERT = 256
N_EXPERT_GROUPS = 8
EXPERTS_PER_GROUP = N_EXPERTS // N_EXPERT_GROUPS
TOPK_GROUPS = 4
ROUTE_SCALE = 2.5
DISPATCH_BLOCK = 256
EPS = 1e-6
NEG = -1e30

LANES = 128
S5_T = 16
S5_SG = 4
S5_GL = N_GROUPS // S5_SG
S5_CH = N_TOK // S5_T
S5_CH_PER_BATCH = SEQ // S5_T
S5_NSTATE = S5_GL * STATE * 2

NK = N_TOK * TOP_K
HALF = D_MODEL // 2

VMEM_LIMIT = 48 * 1024 * 1024


def _cparams(*sem, vmem=VMEM_LIMIT):
    return pltpu.CompilerParams(dimension_semantics=tuple(sem), vmem_limit_bytes=vmem)


def _dot(a, b):
    return jnp.dot(a, b, preferred_element_type=F32)


def _dot_nt(a, b):
    return lax.dot_general(a, b, (((1,), (1,)), ((), ())), preferred_element_type=F32)


def _split_dot(v, w):
    hi = v.astype(BF16)
    lo = (v - hi.astype(F32)).astype(BF16)
    return _dot(hi, w) + _dot(lo, w)


def _seg_rms(v, bd, gain):
    ss = _split_dot(v * v, bd)
    return v * lax.rsqrt(ss * (1.0 / HEAD_DIM) + EPS) * gain


def _gelu(x):
    return 0.5 * x * (1.0 + jnp.tanh(0.7978845608028654 * (x + 0.044715 * (x * x * x))))


def _silu(x):
    return x * jax.nn.sigmoid(x)


def _pack_bf16_pairs(v):
    bits = lax.bitcast_convert_type(v.astype(BF16).astype(F32), jnp.uint32)
    h = v.shape[1] // 2
    return bits[:, h:] | lax.shift_right_logical(bits[:, :h], jnp.uint32(16))


def _unpack_bf16_pairs(word):
    lo = lax.bitcast_convert_type(lax.shift_left(word, jnp.uint32(16)), F32)
    hi = lax.bitcast_convert_type(word & jnp.uint32(0xFFFF0000), F32)
    return lo, hi


def _ada_kernel(c_ref, w_ref, b_ref, o_ref):
    c = c_ref[...]
    o_ref[...] = jnp.dot(_silu(c), w_ref[...], preferred_element_type=F32, precision=HIGHEST) + b_ref[...]


def _ada(c, w_ada, b_ada):
    cp = jnp.pad(c, ((0, 8 - BATCH), (0, 0)))
    tn = 1536
    out = pl.pallas_call(
        _ada_kernel,
        grid=(6 * D_MODEL // tn,),
        in_specs=[pl.BlockSpec((8, D_MODEL), lambda j: (0, 0)),
                  pl.BlockSpec((D_MODEL, tn), lambda j: (0, j)),
                  pl.BlockSpec((1, tn), lambda j: (0, j))],
        out_specs=pl.BlockSpec((8, tn), lambda j: (0, j)),
        out_shape=jax.ShapeDtypeStruct((8, 6 * D_MODEL), F32),
        compiler_params=_cparams("arbitrary"),
        name="ada",
    )(cp, w_ada, b_ada.reshape(1, -1))
    return out.reshape(8, 6, D_MODEL)


_C_Q = 0
_C_KC = 512
_C_VC = 640
_C_KS = 768
_C_KW = 1024
_C_GN = 1280
_C_U = 1408
_C_GA = 1920
_C_GS = 2944
_C_END = 3968
PROJ_TM = 512


def _proj_kernel(x_ref, mod_ref, g1_ref, w_ref, wvt_ref, qg_ref, ksg_ref, kwg_ref, bd512_ref, bd256_ref,
                 q_ref, kc_ref, vc_ref, ks_ref, kw_ref, vst_ref, vwt_ref, gn_ref, u_ref, ga_ref, gs_ref):
    x = x_ref[...]
    ms = jnp.mean(x * x, axis=-1, keepdims=True)
    mod = mod_ref[0]
    h = (x * lax.rsqrt(ms + EPS) * g1_ref[...]) * (1.0 + mod[1:2]) + mod[0:1]
    hb = h.astype(BF16)

    def p(lo, hi):
        return _dot(hb, w_ref[:, lo:hi])

    q_ref[...] = _seg_rms(p(_C_Q, _C_KC), bd512_ref[...], qg_ref[...] * (ATTN_SCALE * LOG2E)).astype(BF16)
    kc_ref[...] = p(_C_KC, _C_VC).astype(BF16)
    vc_ref[...] = p(_C_VC, _C_KS).astype(BF16)
    ks_ref[...] = _seg_rms(p(_C_KS, _C_KW), bd256_ref[...], ksg_ref[...]).astype(BF16)
    kw_ref[...] = _seg_rms(p(_C_KW, _C_GN), bd256_ref[...], kwg_ref[...]).astype(BF16)
    vt = _dot_nt(wvt_ref[...], hb)
    vst_ref[...] = vt[:LANES].astype(BF16)
    vwt_ref[...] = vt[LANES:].astype(BF16)
    gn_ref[...] = jax.nn.sigmoid(p(_C_GN, _C_U))
    u_ref[...] = p(_C_U, _C_GA)
    ga_ref[...] = jax.nn.sigmoid(p(_C_GA, _C_GS)).astype(BF16)
    gs_ref[...] = jax.nn.sigmoid(p(_C_GS, _C_END)).astype(BF16)


def _dup_cols(w):
    return jnp.concatenate([w[:, :64], w[:, :64], w[:, 64:], w[:, 64:]], axis=1)


def _block_ones(n):
    return jnp.kron(jnp.eye(n // HEAD_DIM, dtype=F32), jnp.ones((HEAD_DIM, HEAD_DIM), F32)).astype(BF16)


def _proj(x2, mod, g_norm1, w_in, q_gain, ks_gain, kw_gain):
    o = np.cumsum((0, 512, 128, 128, 128, 128, 128, 128, 24, 512, 1024, 1024))
    parts = [w_in[:, o[i]:o[i + 1]] for i in range(11)]
    wq, wkc, wvc, wks, wvs, wkw, wvw, wgn, wu, wga, wgs = parts
    w = jnp.concatenate([wq, wkc, wvc, _dup_cols(wks), _dup_cols(wkw),
                         jnp.pad(wgn, ((0, 0), (0, LANES - 24))), wu, wga, wgs], axis=1).astype(BF16)
    wvt = jnp.concatenate([wvs, wvw], axis=1).T.astype(BF16)
    tm = PROJ_TM
    row = lambda i: (i, 0)
    col = lambda i: (0, i)
    fix = lambda i: (0, 0)
    outs = [(512, BF16, row), (128, BF16, row), (128, BF16, row), (256, BF16, row), (256, BF16, row),
            (LANES, BF16, col), (LANES, BF16, col),
            (128, F32, row), (512, F32, row), (1024, BF16, row), (1024, BF16, row)]
    ospec = lambda wd, m: pl.BlockSpec((tm, wd), m) if m is row else pl.BlockSpec((wd, tm), m)
    oshape = lambda wd, dt, m: jax.ShapeDtypeStruct((N_TOK, wd) if m is row else (wd, N_TOK), dt)
    return pl.pallas_call(
        _proj_kernel,
        grid=(N_TOK // tm,),
        in_specs=[pl.BlockSpec((tm, D_MODEL), row),
                  pl.BlockSpec((1, 6, D_MODEL), lambda i: (i // (SEQ // tm), 0, 0)),
                  pl.BlockSpec((1, D_MODEL), fix),
                  pl.BlockSpec((D_MODEL, _C_END), fix),
                  pl.BlockSpec((2 * LANES, D_MODEL), fix),
                  pl.BlockSpec((1, 512), fix), pl.BlockSpec((1, 256), fix), pl.BlockSpec((1, 256), fix),
                  pl.BlockSpec((512, 512), fix), pl.BlockSpec((256, 256), fix)],
        out_specs=[ospec(wd, m) for wd, _, m in outs],
        out_shape=[oshape(wd, dt, m) for wd, dt, m in outs],
        compiler_params=_cparams("arbitrary"),
        name="proj",
    )(x2, mod, g_norm1.reshape(1, -1), w, wvt,
      jnp.tile(q_gain, N_HEADS).reshape(1, -1), jnp.tile(ks_gain, 4).reshape(1, -1),
      jnp.tile(kw_gain, 4).reshape(1, -1), _block_ones(512), _block_ones(256))


def _compress_kernel(r_ref, pe_ref, w1_ref, w2_ref, bd_ref, gain_ref, o_ref, *, do_norm):
    r = r_ref[0].astype(F32)
    p0 = _dot((r + pe_ref[0]).astype(BF16), w1_ref[0])
    p1 = _dot((r + pe_ref[1]).astype(BF16), w1_ref[1])
    hid = p0 + pltpu.roll(p1, N_CMP - 1, 0)
    c = _dot(_gelu(hid).astype(BF16), w2_ref[...])
    if do_norm:
        c = _seg_rms(c, bd_ref[...], gain_ref[...])
    o_ref[0] = c.astype(BF16)


def _compress(raw, pe, w1, w2, gain, do_norm):
    r = raw.reshape(BATCH, SEQ // CMP_STRIDE, CMP_STRIDE * LANES)
    eye = jnp.eye(N_KV, dtype=F32)
    w1r = w1.reshape(2, CMP_STRIDE, HEAD_DIM, CMP_HIDDEN)
    w1big = jnp.einsum('hldc,gk->hlgdkc', w1r, eye).reshape(2, CMP_STRIDE * LANES, N_KV * CMP_HIDDEN).astype(BF16)
    w2big = jnp.einsum('cd,gk->gckd', w2, eye)
    w2big = jnp.concatenate([w2big, w2big], axis=-1).reshape(N_KV * CMP_HIDDEN, 4 * HEAD_DIM).astype(BF16)
    pe_big = jnp.broadcast_to(pe.reshape(2, CMP_STRIDE, 1, HEAD_DIM), (2, CMP_STRIDE, N_KV, HEAD_DIM))
    pe_big = pe_big.reshape(2, 1, CMP_STRIDE * LANES)
    fix2 = lambda b: (0, 0)
    fix3 = lambda b: (0, 0, 0)
    return pl.pallas_call(
        functools.partial(_compress_kernel, do_norm=do_norm),
        grid=(BATCH,),
        in_specs=[pl.BlockSpec((1, N_CMP, CMP_STRIDE * LANES), lambda b: (b, 0, 0)),
                  pl.BlockSpec((2, 1, CMP_STRIDE * LANES), fix3),
                  pl.BlockSpec((2, CMP_STRIDE * LANES, N_KV * CMP_HIDDEN), fix3),
                  pl.BlockSpec((N_KV * CMP_HIDDEN, 256), fix2),
                  pl.BlockSpec((256, 256), fix2), pl.BlockSpec((1, 256), fix2)],
        out_specs=pl.BlockSpec((1, N_CMP, 256), lambda b: (b, 0, 0)),
        out_shape=jax.ShapeDtypeStruct((BATCH, N_CMP, 256), BF16),
        compiler_params=_cparams("arbitrary"),
        name="compress_k" if do_norm else "compress_v",
    )(r, pe_big, w1big, w2big, _block_ones(256), jnp.tile(gain, 4).reshape(1, -1))


ATT_TQ = 256
RANK_CHUNK = 16


def _head_variants(qb):
    lane = lax.broadcasted_iota(I32, qb.shape, 1)
    z = jnp.zeros_like(qb)
    return jnp.where(lane < HEAD_DIM, qb, z), jnp.where(lane < HEAD_DIM, z, qb)


def _cmp_kernel(q_ref, kc_ref, vc_ref, ov_ref, o_ref, sel_ref, vrank_ref):
    tq = ATT_TQ
    qi = pl.program_id(1)
    tpos = qi * tq + lax.broadcasted_iota(I32, (tq, N_CMP), 0)
    nidx = lax.broadcasted_iota(I32, (tq, N_CMP), 1)
    mask = (CMP_STRIDE * nidx + (CMP_BLOCK - 1)) <= tpos
    lane_lo = lax.broadcasted_iota(I32, (tq, LANES), 1) < HEAD_DIM
    for g in range(N_KV):
        kd = kc_ref[0, :, g * LANES:(g + 1) * LANES]
        vd = vc_ref[0, :, g * LANES:(g + 1) * LANES]
        psum = jnp.zeros((tq, N_CMP), F32)
        for jb in range(2):
            blk = 2 * g + jb
            pv = []
            for qv in _head_variants(q_ref[:, blk * LANES:(blk + 1) * LANES]):
                s = jnp.where(mask, _dot_nt(qv, kd), NEG)
                m = jnp.max(s, axis=-1, keepdims=True)
                e = jnp.where(mask, jnp.exp2(s - m), 0.0)
                l = jnp.sum(e, axis=-1, keepdims=True)
                p = e / jnp.where(l > 0.0, l, 1.0)
                psum = psum + p
                pv.append(_dot(p.astype(BF16), vd))
            o_ref[:, blk * LANES:(blk + 1) * LANES] = jnp.where(lane_lo, pv[0], pv[1]).astype(BF16)
        imp = _split_dot(psum, ov_ref[...])
        imp_t = imp.T[:N_SEL_BLOCKS]
        j = lax.broadcasted_iota(I32, (N_SEL_BLOCKS, tq), 0)
        cur = jnp.right_shift(qi * tq + lax.broadcasted_iota(I32, (N_SEL_BLOCKS, tq), 1), 6)
        forced = (j == 0) | (j == cur) | (j == cur - 1)
        v = jnp.where(forced, jnp.inf, jnp.where(j <= cur, imp_t, -jnp.inf))
        vrank_ref[...] = jnp.zeros((N_SEL_BLOCKS, tq), F32)
        n_live = (qi + 1) * (tq // SEL_BLOCK)
        for c0 in range(0, N_SEL_BLOCKS, RANK_CHUNK):
            @pl.when(c0 < n_live)
            def _():
                rank = vrank_ref[...]
                for jp in range(c0, c0 + RANK_CHUNK):
                    row = v[jp:jp + 1, :]
                    tie = jnp.where(j > jp, 1.0, 0.0)
                    rank = rank + jnp.where(row > v, 1.0, jnp.where(row == v, tie, 0.0))
                vrank_ref[...] = rank
        rank = vrank_ref[...]
        sel_ref[g * N_SEL_BLOCKS:(g + 1) * N_SEL_BLOCKS, :] = jnp.where(rank < float(N_SELECT), 0.0, NEG)


def _cmp_attn(q, kcn, vcn):
    nc = np.arange(N_CMP)
    sb = np.arange(LANES)
    ov = ((CMP_STRIDE * nc[:, None] < SEL_BLOCK * sb[None, :] + SEL_BLOCK)
          & (CMP_STRIDE * nc[:, None] + CMP_BLOCK > SEL_BLOCK * sb[None, :])
          & (nc[:, None] < N_CMP - 1) & (sb[None, :] < N_SEL_BLOCKS))
    ov = jnp.asarray(ov, BF16)
    tq = ATT_TQ
    nq = SEQ // tq
    row = lambda b, i: (b * nq + i, 0)
    return pl.pallas_call(
        _cmp_kernel,
        grid=(BATCH, nq),
        in_specs=[pl.BlockSpec((tq, NSA_WIDTH), row),
                  pl.BlockSpec((1, N_CMP, 256), lambda b, i: (b, 0, 0)),
                  pl.BlockSpec((1, N_CMP, 256), lambda b, i: (b, 0, 0)),
                  pl.BlockSpec((N_CMP, LANES), lambda b, i: (0, 0))],
        out_specs=[pl.BlockSpec((tq, NSA_WIDTH), row),
                   pl.BlockSpec((N_KV * N_SEL_BLOCKS, tq), lambda b, i: (0, b * nq + i))],
        out_shape=[jax.ShapeDtypeStruct((N_TOK, NSA_WIDTH), BF16),
                   jax.ShapeDtypeStruct((N_KV * N_SEL_BLOCKS, N_TOK), F32)],
        scratch_shapes=[pltpu.VMEM((N_SEL_BLOCKS, tq), F32)],
        compiler_params=_cparams("arbitrary", "arbitrary"),
        name="cmp_attn",
    )(q, kcn, vcn, ov)


ATT_TK = 256


M_INIT = -1e29


SUM_ROWS = 16


def _selwin_kernel(q_ref, ks_ref, kw_ref, vst_ref, vwt_ref, selb_ref, osel_ref, owin_ref, m_ref, acc_ref):
    tq, tk = ATT_TQ, ATT_TK
    qi = pl.program_id(1)
    krow = lax.broadcasted_iota(I32, (tk, tq), 0)
    qcol = lax.broadcasted_iota(I32, (tk, tq), 1)
    causal_bias = jnp.where(krow <= qcol, 0.0, NEG)
    far_bias = jnp.where(qcol < krow, 0.0, NEG)

    ones_rows = jnp.ones((SUM_ROWS, tk), BF16)

    def reset():
        m_ref[...] = jnp.full(m_ref.shape, M_INIT, F32)
        acc_ref[...] = jnp.zeros(acc_ref.shape, F32)

    def update(g, k_ref, vt_ref, kt, bias):
        k0 = pl.multiple_of(kt * tk, tk)
        kd = k_ref[0, pl.ds(k0, tk), g * LANES:(g + 1) * LANES]
        vt = vt_ref[g * HEAD_DIM:(g + 1) * HEAD_DIM, pl.ds(k0, tk)]
        s = _dot_nt(kd, qvars[g])
        if bias is not None:
            s = s + jnp.concatenate([bias] * 4, axis=1)
        m_old = m_ref[g]
        m_new = jnp.maximum(m_old, jnp.max(s, axis=0, keepdims=True))
        alpha = jnp.exp2(m_old - m_new)
        p = jnp.exp2(s - m_new)
        m_ref[g] = m_new
        vte = jnp.concatenate([vt, ones_rows], axis=0)
        acc_ref[g] = alpha * acc_ref[g] + _dot(vte, p.astype(BF16))

    def finish(out_ref, g):
        o = acc_ref[g, :HEAD_DIM, :] / acc_ref[g, HEAD_DIM:HEAD_DIM + 1, :]
        for jb in range(2):
            blk = 2 * g + jb
            pair = jnp.concatenate([o[:, 2 * jb * tq:(2 * jb + 1) * tq], o[:, (2 * jb + 1) * tq:(2 * jb + 2) * tq]],
                                   axis=0)
            out_ref[:, blk * LANES:(blk + 1) * LANES] = pair.T.astype(BF16)

    def sel_bias(g, kt):
        rows = [jnp.broadcast_to(selb_ref[pl.ds(g * N_SEL_BLOCKS + kt * (tk // SEL_BLOCK) + r, 1), :],
                                 (SEL_BLOCK, tq)) for r in range(tk // SEL_BLOCK)]
        return jnp.concatenate(rows, axis=0)

    qvars = []
    for g in range(N_KV):
        heads = []
        for jb in range(2):
            heads.extend(_head_variants(q_ref[:, (2 * g + jb) * LANES:(2 * g + jb + 1) * LANES]))
        qvars.append(jnp.concatenate(heads, axis=0))
    groups = range(N_KV)

    reset()

    def sel_step(kt, carry):
        for g in groups:
            update(g, ks_ref, vst_ref, kt, sel_bias(g, kt))
        return carry

    lax.fori_loop(0, qi, sel_step, 0)
    for g in groups:
        update(g, ks_ref, vst_ref, qi, sel_bias(g, qi) + causal_bias)
    for g in groups:
        finish(osel_ref, g)

    reset()

    @pl.when(qi >= 2)
    def _():
        for g in groups:
            update(g, kw_ref, vwt_ref, qi - 2, far_bias)

    @pl.when(qi >= 1)
    def _():
        for g in groups:
            update(g, kw_ref, vwt_ref, qi - 1, None)

    for g in groups:
        update(g, kw_ref, vwt_ref, qi, causal_bias)
    for g in groups:
        finish(owin_ref, g)


def _selwin(q, ks, kw, vst, vwt, selb):
    tq = ATT_TQ
    nq = SEQ // tq
    assert WINDOW == 2 * ATT_TK and ATT_TQ == ATT_TK
    row = lambda b, i: (b * nq + i, 0)
    keys = pl.BlockSpec((1, SEQ, 256), lambda b, i: (b, 0, 0))
    vals = pl.BlockSpec((LANES, SEQ), lambda b, i: (0, b))
    r3 = lambda a: a.reshape(BATCH, SEQ, 256)
    return pl.pallas_call(
        _selwin_kernel,
        grid=(BATCH, nq),
        in_specs=[pl.BlockSpec((tq, NSA_WIDTH), row), keys, keys, vals, vals,
                  pl.BlockSpec((N_KV * N_SEL_BLOCKS, tq), lambda b, i: (0, b * nq + i))],
        out_specs=[pl.BlockSpec((tq, NSA_WIDTH), row)] * 2,
        out_shape=[jax.ShapeDtypeStruct((N_TOK, NSA_WIDTH), BF16)] * 2,
        scratch_shapes=[pltpu.VMEM((N_KV, 1, 4 * tq), F32),
                        pltpu.VMEM((N_KV, HEAD_DIM + SUM_ROWS, 4 * tq), F32)],
        compiler_params=_cparams("arbitrary", "arbitrary"),
        name="selwin",
    )(q, r3(ks), r3(kw), vst, vwt, selb)


def _s5_param_kernel(are_ref, aim_ref, ldt_ref, cre_ref, cim_ref, bre_ref, bim_ref,
                     clre_ref, clim_ref, wbre_ref, wbim_ref, bbre_ref, bbim_ref, ltre_ref, ltim_ref):
    are, aim = are_ref[...], aim_ref[...]
    dt = jnp.exp(ldt_ref[...])
    cre, cim = cre_ref[...], cim_ref[...]

    def lam_pow(tau):
        mag = jnp.exp(are * dt * float(tau))
        ang = aim * dt * float(tau)
        return mag * jnp.cos(ang), mag * jnp.sin(ang)

    lre, lim = lam_pow(1)
    den = are * are + aim * aim
    qre = ((lre - 1.0) * are + lim * aim) / den
    qim = (lim * are - (lre - 1.0) * aim) / den
    bre, bim = bre_ref[...], bim_ref[...]
    bbre = qre * bre - qim * bim
    bbim = qre * bim + qim * bre
    bbre_ref[...] = bbre
    bbim_ref[...] = bbim
    for tau in range(S5_T + 1):
        pr, pi = lam_pow(tau)
        clre_ref[tau] = cre * pr - cim * pi
        clim_ref[tau] = cre * pi + cim * pr
        if tau < S5_T:
            k = S5_T - 1 - tau
            wbre_ref[k] = pr * bbre - pi * bbim
            wbim_ref[k] = pr * bbim + pi * bbre
        else:
            ltre_ref[...] = pr
            ltim_ref[...] = pi


def _s5_kmat_kernel(l_ref, r_ref, o_ref):
    o_ref[0] = jnp.dot(l_ref[0], r_ref[0], preferred_element_type=F32, precision=HIGHEST)


def _s5_params(a_re, a_im, log_dt, b_re, b_im, c_re, c_im):
    T = S5_T
    pn = GROUP * STATE
    tile_p = lambda a: jnp.tile(a, (1, GROUP))
    args = (tile_p(a_re), tile_p(a_im), jnp.broadcast_to(log_dt[:, None], (N_GROUPS, pn)),
            c_re.reshape(N_GROUPS, pn), c_im.reshape(N_GROUPS, pn),
            jnp.swapaxes(b_re, 1, 2).reshape(N_GROUPS, pn), jnp.swapaxes(b_im, 1, 2).reshape(N_GROUPS, pn))
    full2 = pl.BlockSpec((N_GROUPS, pn), lambda: (0, 0))
    clre, clim, wbre, wbim, bbre, bbim, ltre, ltim = pl.pallas_call(
        _s5_param_kernel,
        in_specs=[full2] * 7,
        out_specs=[pl.BlockSpec((T + 1, N_GROUPS, pn), lambda: (0, 0, 0))] * 2
                  + [pl.BlockSpec((T, N_GROUPS, pn), lambda: (0, 0, 0))] * 2 + [full2] * 4,
        out_shape=[jax.ShapeDtypeStruct((T + 1, N_GROUPS, pn), F32)] * 2
                  + [jax.ShapeDtypeStruct((T, N_GROUPS, pn), F32)] * 2
                  + [jax.ShapeDtypeStruct((N_GROUPS, pn), F32)] * 4,
        name="s5_params",
    )(*args)

    r5 = lambda a, t: a[:t].reshape(t, N_GROUPS, GROUP, STATE)
    lhs = jnp.concatenate([r5(clre, T), -r5(clim, T)], axis=-1)
    lhs = jnp.transpose(lhs, (1, 0, 2, 3)).reshape(N_GROUPS, T * GROUP, 2 * STATE)
    bb = lambda a: jnp.swapaxes(a.reshape(N_GROUPS, GROUP, STATE), 1, 2)
    rhs = jnp.concatenate([bb(bbre), bb(bbim)], axis=1)
    kmat = pl.pallas_call(
        _s5_kmat_kernel,
        grid=(N_GROUPS,),
        in_specs=[pl.BlockSpec((1, T * GROUP, 2 * STATE), lambda g: (g, 0, 0)),
                  pl.BlockSpec((1, 2 * STATE, GROUP), lambda g: (g, 0, 0))],
        out_specs=pl.BlockSpec((1, T * GROUP, GROUP), lambda g: (g, 0, 0)),
        out_shape=jax.ShapeDtypeStruct((N_GROUPS, T * GROUP, GROUP), F32),
        compiler_params=_cparams("arbitrary"),
        name="s5_kmat",
    )(lhs, rhs)

    eye = jnp.eye(S5_GL, dtype=F32)
    kt = kmat.reshape(S5_SG, S5_GL, T, GROUP, GROUP)
    kbd = jnp.einsum('sgtpq,gh->stgqhp', kt, eye).reshape(S5_SG, T, LANES, LANES)
    krev = kbd[:, ::-1].reshape(S5_SG, T * LANES, LANES).astype(BF16)
    r6 = lambda a: a.reshape(T, S5_SG, S5_GL, GROUP, STATE)
    wb = jnp.stack([r6(wbre), r6(wbim)], axis=-2)
    wb = jnp.einsum('ksgpin,gh->skgpihn', wb, eye).reshape(S5_SG, T * LANES, S5_NSTATE).astype(BF16)
    wc = jnp.stack([r6(clre[1:]), -r6(clim[1:])], axis=-2)
    wc = jnp.einsum('tsgpin,gh->signthp', wc, eye).reshape(S5_SG, S5_NSTATE, T * LANES).astype(BF16)
    lt = lambda a: a.reshape(N_GROUPS, GROUP, STATE)[:, 0].reshape(S5_SG, 1, S5_GL * STATE)
    return krev, wb, wc, lt(ltre), lt(ltim)


S5_TC = 256


def _s5_lane_block(sg):
    return pl.ds(pl.multiple_of(sg * LANES, LANES), LANES)


def _s5_chunk_inputs(x_ref, sg):
    return jnp.concatenate([x_ref[:, t, _s5_lane_block(sg)] for t in range(S5_T)], axis=1).astype(BF16)


def _s5_state_kernel(x_ref, wb_ref, e_ref):
    e_ref[0] = _dot(_s5_chunk_inputs(x_ref, pl.program_id(1)), wb_ref[0])


def _s5_scan_kernel(e_ref, ltre_ref, ltim_ref, xs_ref):
    lr, li = ltre_ref[0], ltim_ref[0]
    half = S5_NSTATE // 2

    def step(c, carry):
        new = []
        for b in range(BATCH):
            xr, xi = carry[b]
            row = b * S5_CH_PER_BATCH + c
            xs_ref[0, pl.ds(row, 1), :half] = xr
            xs_ref[0, pl.ds(row, 1), half:] = xi
            e = e_ref[0, pl.ds(row, 1), :]
            new.append((lr * xr - li * xi + e[:, :half], lr * xi + li * xr + e[:, half:]))
        return tuple(new)

    zero = jnp.zeros((1, half), F32)
    lax.fori_loop(0, S5_CH_PER_BATCH, step, tuple((zero, zero) for _ in range(BATCH)))


def _s5_out_kernel(x_ref, xs_ref, krev_ref, wc_ref, y_ref):
    sg = pl.program_id(1)
    x = _s5_chunk_inputs(x_ref, sg)
    xsb = xs_ref[0].astype(BF16)
    for t in range(S5_T):
        cols = slice(t * LANES, (t + 1) * LANES)
        y_ref[:, t, _s5_lane_block(sg)] = (_dot(x[:, :(t + 1) * LANES], krev_ref[0, (S5_T - 1 - t) * LANES:, :])
                                           + _dot(xsb, wc_ref[0, :, cols]))


def _s5(u, krev, wb, wc, ltre, ltim):
    T, tc = S5_T, S5_TC
    xn = u.reshape(S5_CH, T, SSM_WIDTH)
    grid = (S5_CH // tc, S5_SG)
    natural = pl.BlockSpec((tc, T, SSM_WIDTH), lambda i, s: (i, 0, 0))
    rows = lambda i, s: (s, i, 0)
    per_sg = lambda i, s: (s, 0, 0)
    e = pl.pallas_call(
        _s5_state_kernel, grid=grid,
        in_specs=[natural, pl.BlockSpec((1, T * LANES, S5_NSTATE), per_sg)],
        out_specs=pl.BlockSpec((1, tc, S5_NSTATE), rows),
        out_shape=jax.ShapeDtypeStruct((S5_SG, S5_CH, S5_NSTATE), F32),
        compiler_params=_cparams("arbitrary", "arbitrary"), name="s5_state",
    )(xn, wb)
    sg1 = lambda s: (s, 0, 0)
    xstart = pl.pallas_call(
        _s5_scan_kernel, grid=(S5_SG,),
        in_specs=[pl.BlockSpec((1, S5_CH, S5_NSTATE), sg1),
                  pl.BlockSpec((1, 1, S5_NSTATE // 2), sg1), pl.BlockSpec((1, 1, S5_NSTATE // 2), sg1)],
        out_specs=pl.BlockSpec((1, S5_CH, S5_NSTATE), sg1),
        out_shape=jax.ShapeDtypeStruct((S5_SG, S5_CH, S5_NSTATE), F32),
        compiler_params=_cparams("arbitrary"), name="s5_scan",
    )(e, ltre, ltim)
    y = pl.pallas_call(
        _s5_out_kernel, grid=grid,
        in_specs=[natural, pl.BlockSpec((1, tc, S5_NSTATE), rows),
                  pl.BlockSpec((1, T * LANES, LANES), per_sg), pl.BlockSpec((1, S5_NSTATE, T * LANES), per_sg)],
        out_specs=natural,
        out_shape=jax.ShapeDtypeStruct((S5_CH, T, SSM_WIDTH), F32),
        compiler_params=_cparams("arbitrary", "arbitrary", vmem=56 * 1024 * 1024), name="s5_out",
    )(xn, xstart, krev, wc)
    return y.reshape(N_TOK, SSM_WIDTH)


MERGE_TM = 256


def _merge_kernel(ocmp_ref, osel_ref, owin_ref, gn_ref, yssm_ref, u_ref, ga_ref, gs_ref, x_ref, mod_ref,
                  eg_ref, dskip_ref, wglu_ref, bglu_ref, wua_ref, wus_ref, wout_ref, g2_ref,
                  wrhi_ref, wrlo_ref, wsgu_ref, wsd_ref,
                  xpart_ref, h2_ref, logit_ref):
    mod = mod_ref[0]
    gnb = gn_ref[...].astype(BF16)
    o_nsa = (_dot(gnb, eg_ref[0]) * ocmp_ref[...].astype(F32)
             + _dot(gnb, eg_ref[1]) * osel_ref[...].astype(F32)
             + _dot(gnb, eg_ref[2]) * owin_ref[...].astype(F32))
    attn = _dot(o_nsa.astype(BF16), wua_ref[...])
    z = _gelu(yssm_ref[...] + dskip_ref[...] * u_ref[...])
    y_ssm = z * jax.nn.sigmoid(_dot(z.astype(BF16), wglu_ref[...]) + bglu_ref[...])
    ssm = _dot(y_ssm.astype(BF16), wus_ref[...])
    merged = ga_ref[...].astype(F32) * attn + gs_ref[...].astype(F32) * ssm
    x1 = x_ref[...] + mod[2:3] * _dot(merged.astype(BF16), wout_ref[...])

    ms = jnp.mean(x1 * x1, axis=-1, keepdims=True)
    h2 = (x1 * lax.rsqrt(ms + EPS) * g2_ref[...]) * (1.0 + mod[4:5]) + mod[3:4]
    hi = h2.astype(BF16)
    lo = (h2 - hi.astype(F32)).astype(BF16)
    h2_ref[...] = _pack_bf16_pairs(h2)
    logit_ref[...] = _dot_nt(wrhi_ref[...], hi) + _dot_nt(wrhi_ref[...], lo) + _dot_nt(wrlo_ref[...], hi)
    gu = _dot(hi, wsgu_ref[...])
    shared = _dot((_silu(gu[:, :D_EXPERT]) * gu[:, D_EXPERT:]).astype(BF16), wsd_ref[...])
    xpart_ref[...] = x1 + mod[5:6] * shared


def _merge(ocmp, osel, owin, gn, yssm, u, ga, gs, x2, mod, d_skip, w_glu, b_glu, w_up_attn, w_up_ssm, w_out,
           g_norm2, w_router, ws_gate, ws_up, ws_down):
    tm = MERGE_TM
    eg = np.zeros((3, LANES, NSA_WIDTH), np.float32)
    for j in range(3):
        for h in range(N_HEADS):
            eg[j, 3 * h + j, h * HEAD_DIM:(h + 1) * HEAD_DIM] = 1.0
    wr_t = w_router.T
    wr_hi = wr_t.astype(BF16)
    wr_lo = (wr_t - wr_hi.astype(F32)).astype(BF16)
    row = lambda i: (i, 0)
    fix2 = lambda i: (0, 0)
    wspec = lambda a: pl.BlockSpec(a.shape, (lambda i: (0,) * a.ndim))
    weights = [jnp.asarray(eg, BF16), d_skip.reshape(1, -1), w_glu.astype(BF16), b_glu.reshape(1, -1),
               w_up_attn.astype(BF16), w_up_ssm.astype(BF16), w_out.astype(BF16), g_norm2.reshape(1, -1),
               wr_hi, wr_lo, jnp.concatenate([ws_gate, ws_up], axis=1).astype(BF16), ws_down.astype(BF16)]
    acts = [(ocmp, 512), (osel, 512), (owin, 512), (gn, 128), (yssm, 512), (u, 512), (ga, 1024), (gs, 1024),
            (x2, 1024)]
    return pl.pallas_call(
        _merge_kernel,
        grid=(N_TOK // tm,),
        in_specs=[pl.BlockSpec((tm, wd), row) for _, wd in acts]
                 + [pl.BlockSpec((1, 6, D_MODEL), lambda i: (i // (SEQ // tm), 0, 0))]
                 + [wspec(w) for w in weights],
        out_specs=[pl.BlockSpec((tm, D_MODEL), row), pl.BlockSpec((tm, HALF), row),
                   pl.BlockSpec((N_EXPERTS, tm), lambda i: (0, i))],
        out_shape=[jax.ShapeDtypeStruct((N_TOK, D_MODEL), F32), jax.ShapeDtypeStruct((N_TOK, HALF), jnp.uint32),
                   jax.ShapeDtypeStruct((N_EXPERTS, N_TOK), F32)],
        compiler_params=_cparams("arbitrary"),
        name="merge",
    )(*[a for a, _ in acts], mod, *weights)


ROUTE_TN = 512


def _route_kernel(logit_ref, bias_ref, eidx_ref, w_ref, count_ref, gscore_ref, masked_ref):
    tn = ROUTE_TN

    @pl.when(pl.program_id(0) == 0)
    def _():
        count_ref[...] = jnp.zeros(count_ref.shape, F32)

    sc = jax.nn.sigmoid(logit_ref[...])
    biased = sc + bias_ref[...]
    gi = lax.broadcasted_iota(I32, (EXPERTS_PER_GROUP, tn), 0).astype(F32)
    for g in range(N_EXPERT_GROUPS):
        blk = biased[g * EXPERTS_PER_GROUP:(g + 1) * EXPERTS_PER_GROUP]
        m1 = jnp.max(blk, axis=0, keepdims=True)
        i1 = jnp.min(jnp.where(blk == m1, gi, float(EXPERTS_PER_GROUP)), axis=0, keepdims=True)
        m2 = jnp.max(jnp.where(gi == i1, -jnp.inf, blk), axis=0, keepdims=True)
        gscore_ref[g:g + 1, :] = m1 + m2
    gs = gscore_ref[...]
    gidx = lax.broadcasted_iota(I32, (N_EXPERT_GROUPS, tn), 0)
    grank = jnp.zeros((N_EXPERT_GROUPS, tn), F32)
    for gp in range(N_EXPERT_GROUPS):
        row = gs[gp:gp + 1, :]
        tie = jnp.where(gidx > gp, 1.0, 0.0)
        grank = grank + jnp.where(row > gs, 1.0, jnp.where(row == gs, tie, 0.0))
    for g in range(N_EXPERT_GROUPS):
        keep = grank[g:g + 1, :] < float(TOPK_GROUPS)
        sl = slice(g * EXPERTS_PER_GROUP, (g + 1) * EXPERTS_PER_GROUP)
        masked_ref[sl, :] = jnp.where(keep, biased[sl], -jnp.inf)
    cur = masked_ref[...]
    eidx = lax.broadcasted_iota(I32, (N_EXPERTS, tn), 0).astype(F32)
    wsum = jnp.zeros((1, tn), F32)
    hits = jnp.zeros((N_EXPERTS, tn), F32)
    for k in range(TOP_K):
        m = jnp.max(cur, axis=0, keepdims=True)
        idx = jnp.min(jnp.where(cur == m, eidx, float(N_EXPERTS)), axis=0, keepdims=True)
        hit = eidx == idx
        wk = jnp.sum(jnp.where(hit, sc, 0.0), axis=0, keepdims=True)
        cur = jnp.where(hit, -jnp.inf, cur)
        hits = hits + jnp.where(hit, 1.0, 0.0)
        eidx_ref[k:k + 1, :] = idx.astype(I32)
        w_ref[k:k + 1, :] = wk
        wsum = wsum + wk
    w_ref[...] = w_ref[...] / wsum * ROUTE_SCALE
    count_ref[...] = count_ref[...] + jnp.sum(hits, axis=1, keepdims=True)


def _route(logits_t, router_bias):
    tn = ROUTE_TN
    return pl.pallas_call(
        _route_kernel,
        grid=(N_TOK // tn,),
        in_specs=[pl.BlockSpec((N_EXPERTS, tn), lambda i: (0, i)), pl.BlockSpec((N_EXPERTS, 1), lambda i: (0, 0))],
        out_specs=[pl.BlockSpec((TOP_K, tn), lambda i: (0, i))] * 2 + [pl.BlockSpec((N_EXPERTS, 1), lambda i: (0, 0))],
        out_shape=[jax.ShapeDtypeStruct((TOP_K, N_TOK), I32), jax.ShapeDtypeStruct((TOP_K, N_TOK), F32),
                   jax.ShapeDtypeStruct((N_EXPERTS, 1), F32)],
        scratch_shapes=[pltpu.VMEM((N_EXPERT_GROUPS, tn), F32), pltpu.VMEM((N_EXPERTS, tn), F32)],
        compiler_params=_cparams("arbitrary"),
        name="route",
    )(logits_t, router_bias.reshape(-1, 1))


N_MOE_BLK = NK // DISPATCH_BLOCK
N_ITEMS = N_MOE_BLK + N_EXPERTS
ASSIGN_BITS = 17


def _dispatch_plan(eidx, counts):
    e_flat = eidx.reshape(-1)
    key = jnp.sort(e_flat * NK + jnp.arange(NK, dtype=I32))
    order = key & (NK - 1)
    counts = counts.reshape(-1).astype(I32)
    start = jnp.cumsum(counts) - counts
    cuts = jnp.sort(jnp.concatenate([jnp.arange(N_MOE_BLK, dtype=I32) * DISPATCH_BLOCK, start]))
    lo = cuts
    hi = jnp.concatenate([cuts[1:], jnp.full((1,), NK, I32)])
    blk = jnp.minimum(lo // DISPATCH_BLOCK, N_MOE_BLK - 1)
    expert = jnp.clip(jnp.sum((start[None, :] <= lo[:, None]).astype(I32), axis=1) - 1, 0, N_EXPERTS - 1)
    one = jnp.ones((1,), I32)
    first = jnp.concatenate([one, (blk[1:] != blk[:-1]).astype(I32)])
    last = jnp.concatenate([(blk[1:] != blk[:-1]).astype(I32), one])
    new_expert = jnp.concatenate([one, (expert[1:] != expert[:-1]).astype(I32)])
    run_id = jnp.cumsum(new_expert) - 1
    n_runs = run_id[-1] + 1
    item = jnp.arange(N_ITEMS, dtype=I32)
    run_first_item = jnp.sort(jnp.where(new_expert == 1, item, N_ITEMS))
    run_expert = expert[jnp.minimum(run_first_item, N_ITEMS - 1)]
    ahead = run_id + (WEIGHT_RING - 1)
    ahead_expert = run_expert[jnp.minimum(ahead, N_ITEMS - 1)]
    ahead_valid = (ahead < n_runs).astype(I32)
    second_expert = run_expert[1:2]
    prologue = jnp.concatenate([second_expert, (n_runs > 1).astype(I32).reshape(1)])
    tok = jnp.right_shift(order, 3)
    home = (order & (TOP_K - 1)) * N_TOK + tok
    return tok, home, (blk, expert, lo - blk * DISPATCH_BLOCK, hi - blk * DISPATCH_BLOCK, first, last, new_expert,
                      run_id % WEIGHT_RING, ahead_expert, ahead_valid, prologue)


SC_CORES = 2
SC_SUBCORES = 16
SC_CHUNK = 128


def _sc_move_rows(table, idx, scatter):
    n = idx.shape[0]
    workers = SC_CORES * SC_SUBCORES
    per_worker = n // workers
    n_chunks = per_worker // SC_CHUNK
    assert per_worker * workers == n and n_chunks * SC_CHUNK == per_worker
    mesh = plsc.VectorSubcoreMesh(core_axis_name="c", subcore_axis_name="s",
                                  num_cores=SC_CORES, num_subcores=SC_SUBCORES)

    def body(table_hbm, idx_hbm, out_hbm, idx_v, rows_v, sem):
        wid = lax.axis_index("s") * SC_CORES + lax.axis_index("c")
        base = wid * per_worker

        @pl.loop(0, n_chunks)
        def _(j):
            off = base + j * SC_CHUNK
            pltpu.sync_copy(idx_hbm.at[pl.ds(off, SC_CHUNK)], idx_v)
            if scatter:
                pltpu.sync_copy(table_hbm.at[pl.ds(off, SC_CHUNK)], rows_v)
                pltpu.async_copy(rows_v, out_hbm.at[idx_v], sem).wait()
            else:
                pltpu.async_copy(table_hbm.at[idx_v], rows_v, sem).wait()
                pltpu.sync_copy(rows_v, out_hbm.at[pl.ds(off, SC_CHUNK)])

    return pl.kernel(
        body,
        out_type=jax.ShapeDtypeStruct((n, table.shape[1]), table.dtype),
        mesh=mesh,
        scratch_types=[pltpu.VMEM((SC_CHUNK,), I32), pltpu.VMEM((SC_CHUNK, table.shape[1]), table.dtype),
                       pltpu.SemaphoreType.DMA],
        name="sc_scatter_rows" if scatter else "sc_gather_rows",
    )(table, idx)


WEIGHT_RING = 3
WEIGHT_CHUNKS = 4


def _expert_weight_copies(w_hbm, wbuf, sem, expert, slot):
    rows = w_hbm.shape[1] // WEIGHT_CHUNKS
    return [pltpu.make_async_copy(w_hbm.at[expert, pl.ds(c * rows, rows)],
                                  wbuf.at[slot, pl.ds(c * rows, rows)], sem.at[slot])
            for c in range(WEIGHT_CHUNKS)]


def _moe_kernel(blk_ref, exp_ref, lo_ref, hi_ref, first_ref, last_ref, newexp_ref,
                slot_ref, ahead_exp_ref, ahead_ok_ref, prologue_ref,
                x_ref, wg_hbm, wu_hbm, wd_hbm, y_ref,
                acc_ref, wgf_ref, wuf_ref, wdf_ref, wgb_ref, wub_ref, wdb_ref, wsem):
    it = pl.program_id(0)
    lo, hi = lo_ref[it], hi_ref[it]
    streams = ((wg_hbm, wgf_ref), (wu_hbm, wuf_ref), (wd_hbm, wdf_ref))

    def request(expert, slot):
        for w_hbm, wbuf in streams:
            for cp in _expert_weight_copies(w_hbm, wbuf, wsem, expert, slot):
                cp.start()

    @pl.when(it == 0)
    def _():
        request(exp_ref[0], 0)

        @pl.when(prologue_ref[1] == 1)
        def _():
            request(prologue_ref[0], 1)

    @pl.when(newexp_ref[it] == 1)
    def _():
        slot = slot_ref[it]
        for w_hbm, wbuf in streams:
            for cp in _expert_weight_copies(w_hbm, wbuf, wsem, 0, slot):
                cp.wait()
        wgb_ref[...] = wgf_ref[slot].astype(BF16)
        wub_ref[...] = wuf_ref[slot].astype(BF16)
        wdb_ref[...] = wdf_ref[slot].astype(BF16)

        @pl.when(ahead_ok_ref[it] == 1)
        def _():
            ahead_slot = slot + (WEIGHT_RING - 1)
            request(ahead_exp_ref[it], jnp.where(ahead_slot >= WEIGHT_RING, ahead_slot - WEIGHT_RING, ahead_slot))

    @pl.when(first_ref[it] == 1)
    def _():
        acc_ref[...] = jnp.zeros(acc_ref.shape, F32)

    def expert_pass(r0, nrows):
        rows = slice(r0, r0 + nrows)
        ridx = r0 + lax.broadcasted_iota(I32, (nrows, HALF), 0)
        mine = (ridx >= lo) & (ridx < hi)
        xlo, xhi = _unpack_bf16_pairs(jnp.where(mine, x_ref[rows, :], jnp.uint32(0)))
        xlo, xhi = xlo.astype(BF16), xhi.astype(BF16)
        gate = _dot(xlo, wgb_ref[:HALF]) + _dot(xhi, wgb_ref[HALF:])
        up = _dot(xlo, wub_ref[:HALF]) + _dot(xhi, wub_ref[HALF:])
        acc_ref[rows, :] = acc_ref[rows, :] + _dot((_silu(gate) * up).astype(BF16), wdb_ref[...])

    mid = DISPATCH_BLOCK // 2
    pl.when((lo < mid) & (hi > mid))(lambda: expert_pass(0, DISPATCH_BLOCK))
    pl.when((hi > lo) & (hi <= mid))(lambda: expert_pass(0, mid))
    pl.when((hi > lo) & (lo >= mid))(lambda: expert_pass(mid, mid))

    @pl.when(last_ref[it] == 1)
    def _():
        y_ref[...] = _pack_bf16_pairs(acc_ref[...])


def _moe(xs, items, w_gate, w_up, w_down):
    by_blk = lambda it, blk, *_: (blk[it], 0)
    any_space = pl.BlockSpec(memory_space=pl.ANY)
    grid_spec = pltpu.PrefetchScalarGridSpec(
        num_scalar_prefetch=len(items),
        grid=(N_ITEMS,),
        in_specs=[pl.BlockSpec((DISPATCH_BLOCK, HALF), by_blk), any_space, any_space, any_space],
        out_specs=pl.BlockSpec((DISPATCH_BLOCK, HALF), by_blk),
        scratch_shapes=[pltpu.VMEM((DISPATCH_BLOCK, D_MODEL), F32),
                        pltpu.VMEM((WEIGHT_RING, D_MODEL, D_EXPERT), F32),
                        pltpu.VMEM((WEIGHT_RING, D_MODEL, D_EXPERT), F32),
                        pltpu.VMEM((WEIGHT_RING, D_EXPERT, D_MODEL), F32),
                        pltpu.VMEM((D_MODEL, D_EXPERT), BF16), pltpu.VMEM((D_MODEL, D_EXPERT), BF16),
                        pltpu.VMEM((D_EXPERT, D_MODEL), BF16),
                        pltpu.SemaphoreType.DMA((WEIGHT_RING,))],
    )
    return pl.pallas_call(
        _moe_kernel,
        grid_spec=grid_spec,
        out_shape=jax.ShapeDtypeStruct((NK, HALF), jnp.uint32),
        compiler_params=_cparams("arbitrary"),
        name="moe",
    )(*items, xs, w_gate, w_up, w_down)


COMB_TC = 256


def _combine_kernel(slots_ref, w_ref, xpart_ref, mod_ref, out_ref):
    w = w_ref[...]
    lo = jnp.zeros((w.shape[0], HALF), F32)
    hi = jnp.zeros((w.shape[0], HALF), F32)
    for k in range(TOP_K):
        klo, khi = _unpack_bf16_pairs(slots_ref[k])
        lo = lo + w[:, k:k + 1] * klo
        hi = hi + w[:, k:k + 1] * khi
    gate2 = mod_ref[0][5:6]
    out_ref[:, :HALF] = xpart_ref[:, :HALF] + gate2[:, :HALF] * lo
    out_ref[:, HALF:] = xpart_ref[:, HALF:] + gate2[:, HALF:] * hi


def _combine(xpart, mod, slots, w):
    tc = COMB_TC
    row = lambda i: (i, 0)
    return pl.pallas_call(
        _combine_kernel,
        grid=(N_TOK // tc,),
        in_specs=[pl.BlockSpec((TOP_K, tc, HALF), lambda i: (0, i, 0)),
                  pl.BlockSpec((tc, TOP_K), row),
                  pl.BlockSpec((tc, D_MODEL), row),
                  pl.BlockSpec((1, 6, D_MODEL), lambda i: (i // (SEQ // tc), 0, 0))],
        out_specs=pl.BlockSpec((tc, D_MODEL), row),
        out_shape=jax.ShapeDtypeStruct((N_TOK, D_MODEL), F32),
        compiler_params=_cparams("arbitrary"),
        name="combine",
    )(slots.reshape(TOP_K, N_TOK, HALF), w, xpart, mod)


def _layer(x, c, w_ada, b_ada, g_norm1, g_norm2, w_in, q_gain, kc_gain, ks_gain, kw_gain,
           pe_k, pe_v, w_cmp_k1, w_cmp_k2, w_cmp_v1, w_cmp_v2,
           a_re, a_im, log_dt, b_re, b_im, c_re, c_im, d_skip, w_glu, b_glu,
           w_up_attn, w_up_ssm, w_out, w_router, router_bias,
           w_gate, w_up, w_down, ws_gate, ws_up, ws_down):
    x2 = x.reshape(N_TOK, D_MODEL)
    mod = _ada(c, w_ada, b_ada)
    q, kc_raw, vc_raw, ks, kw, vst, vwt, gn, u, ga, gs = _proj(x2, mod, g_norm1, w_in, q_gain, ks_gain, kw_gain)
    kcn = _compress(kc_raw, pe_k, w_cmp_k1, w_cmp_k2, kc_gain, True)
    vcn = _compress(vc_raw, pe_v, w_cmp_v1, w_cmp_v2, kc_gain, False)
    ocmp, selb = _cmp_attn(q, kcn, vcn)
    osel, owin = _selwin(q, ks, kw, vst, vwt, selb)
    yssm = _s5(u, *_s5_params(a_re, a_im, log_dt, b_re, b_im, c_re, c_im))
    xpart, h2, logits_t = _merge(ocmp, osel, owin, gn, yssm, u, ga, gs, x2, mod, d_skip, w_glu, b_glu,
                                  w_up_attn, w_up_ssm, w_out, g_norm2, w_router, ws_gate, ws_up, ws_down)
    eidx_t, w_t, counts = _route(logits_t, router_bias)
    tok, home, items = _dispatch_plan(eidx_t.T, counts)
    y = _moe(_sc_move_rows(h2, tok, scatter=False), items, w_gate, w_up, w_down)
    slots = _sc_move_rows(y, home, scatter=True)
    return _combine(xpart, mod, slots, w_t.T).reshape(BATCH, SEQ, D_MODEL)


def kernel(x, c, w_ada, b_ada, g_norm1, g_norm2, w_in, q_gain, kc_gain, ks_gain, kw_gain, pe_k, pe_v, w_cmp_k1,
           w_cmp_k2, w_cmp_v1, w_cmp_v2, a_re, a_im, log_dt, b_re, b_im, c_re, c_im, d_skip, w_glu, b_glu,
           w_up_attn, w_up_ssm, w_out, w_router, router_bias, w_gate, w_up, w_down, ws_gate, ws_up, ws_down):
    params = (w_ada, b_ada, g_norm1, g_norm2, w_in, q_gain, kc_gain, ks_gain, kw_gain, pe_k, pe_v, w_cmp_k1,
              w_cmp_k2, w_cmp_v1, w_cmp_v2, a_re, a_im, log_dt, b_re, b_im, c_re, c_im, d_skip, w_glu, b_glu,
              w_up_attn, w_up_ssm, w_out, w_router, router_bias, w_gate, w_up, w_down, ws_gate, ws_up, ws_down)
    depth = w_ada.shape[0]
    for layer in range(depth):
        x = _layer(x, c, *[p[layer] for p in params])
    return x
```

```python
import functools
import math

import jax
import jax.numpy as jnp
import numpy as np
from jax import lax
from jax.experimental import pallas as pl
from jax.experimental.pallas import tpu as pltpu
from jax.experimental.pallas import tpu_sc as plsc

F32 = jnp.float32
BF16 = jnp.bfloat16
I32 = jnp.int32
HIGHEST = lax.Precision.HIGHEST

D_MODEL = 1024
BATCH = 4
SEQ = 4096
N_TOK = BATCH * SEQ
N_HEADS = 8
HEAD_DIM = 64
N_KV = 2
CMP_BLOCK = 32
CMP_STRIDE = 16
CMP_HIDDEN = 256
N_CMP = 256
SEL_BLOCK = 64
N_SEL_BLOCKS = SEQ // SEL_BLOCK
N_SELECT = 16
WINDOW = 512
ATTN_SCALE = HEAD_DIM ** -0.5
LOG2E = 1.4426950408889634
NSA_WIDTH = N_HEADS * HEAD_DIM
SSM_WIDTH = 512
GROUP = 16
N_GROUPS = SSM_WIDTH // GROUP
STATE = 64
N_EXPERTS = 256
TOP_K = 8
D_EXPERT = 256
N_EXPERT_GROUPS = 8
EXPERTS_PER_GROUP = N_EXPERTS // N_EXPERT_GROUPS
TOPK_GROUPS = 4
ROUTE_SCALE = 2.5
DISPATCH_BLOCK = 256
EPS = 1e-6
NEG = -1e30

LANES = 128
S5_T = 16
S5_SG = 4
S5_GL = N_GROUPS // S5_SG
S5_CH = N_TOK // S5_T
S5_CH_PER_BATCH = SEQ // S5_T
S5_NSTATE = S5_GL * STATE * 2

NK = N_TOK * TOP_K
HALF = D_MODEL // 2

VMEM_LIMIT = 48 * 1024 * 1024


def _cparams(*sem, vmem=VMEM_LIMIT):
    return pltpu.CompilerParams(dimension_semantics=tuple(sem), vmem_limit_bytes=vmem)


def _dot(a, b):
    return jnp.dot(a, b, preferred_element_type=F32)


def _dot_nt(a, b):
    return lax.dot_general(a, b, (((1,), (1,)), ((), ())), preferred_element_type=F32)


def _split_dot(v, w):
    hi = v.astype(BF16)
    lo = (v - hi.astype(F32)).astype(BF16)
    return _dot(hi, w) + _dot(lo, w)


def _seg_rms(v, bd, gain):
    ss = _split_dot(v * v, bd)
    return v * lax.rsqrt(ss * (1.0 / HEAD_DIM) + EPS) * gain


def _gelu(x):
    return 0.5 * x * (1.0 + jnp.tanh(0.7978845608028654 * (x + 0.044715 * (x * x * x))))


def _silu(x):
    return x * jax.nn.sigmoid(x)


def _pack_bf16_pairs(v):
    bits = lax.bitcast_convert_type(v.astype(BF16).astype(F32), jnp.uint32)
    h = v.shape[1] // 2
    return bits[:, h:] | lax.shift_right_logical(bits[:, :h], jnp.uint32(16))


def _unpack_bf16_pairs(word):
    lo = lax.bitcast_convert_type(lax.shift_left(word, jnp.uint32(16)), F32)
    hi = lax.bitcast_convert_type(word & jnp.uint32(0xFFFF0000), F32)
    return lo, hi


def _ada_kernel(c_ref, w_ref, b_ref, o_ref):
    c = c_ref[...]
    o_ref[...] = jnp.dot(_silu(c), w_ref[...], preferred_element_type=F32, precision=HIGHEST) + b_ref[...]


def _ada(c, w_ada, b_ada):
    cp = jnp.pad(c, ((0, 8 - BATCH), (0, 0)))
    tn = 1536
    out = pl.pallas_call(
        _ada_kernel,
        grid=(6 * D_MODEL // tn,),
        in_specs=[pl.BlockSpec((8, D_MODEL), lambda j: (0, 0)),
                  pl.BlockSpec((D_MODEL, tn), lambda j: (0, j)),
                  pl.BlockSpec((1, tn), lambda j: (0, j))],
        out_specs=pl.BlockSpec((8, tn), lambda j: (0, j)),
        out_shape=jax.ShapeDtypeStruct((8, 6 * D_MODEL), F32),
        compiler_params=_cparams("arbitrary"),
        name="ada",
    )(cp, w_ada, b_ada.reshape(1, -1))
    return out.reshape(8, 6, D_MODEL)


_C_Q = 0
_C_KC = 512
_C_VC = 640
_C_KS = 768
_C_KW = 1024
_C_GN = 1280
_C_U = 1408
_C_GA = 1920
_C_GS = 2944
_C_END = 3968
PROJ_TM = 512


def _proj_kernel(x_ref, mod_ref, g1_ref, w_ref, wvt_ref, qg_ref, ksg_ref, kwg_ref, bd512_ref, bd256_ref,
                 q_ref, kc_ref, vc_ref, ks_ref, kw_ref, vst_ref, vwt_ref, gn_ref, u_ref, ga_ref, gs_ref):
    x = x_ref[...]
    ms = jnp.mean(x * x, axis=-1, keepdims=True)
    mod = mod_ref[0]
    h = (x * lax.rsqrt(ms + EPS) * g1_ref[...]) * (1.0 + mod[1:2]) + mod[0:1]
    hb = h.astype(BF16)

    def p(lo, hi):
        return _dot(hb, w_ref[:, lo:hi])

    q_ref[...] = _seg_rms(p(_C_Q, _C_KC), bd512_ref[...], qg_ref[...] * (ATTN_SCALE * LOG2E)).astype(BF16)
    kc_ref[...] = p(_C_KC, _C_VC)
    vc_ref[...] = p(_C_VC, _C_KS)
    ks_ref[...] = _seg_rms(p(_C_KS, _C_KW), bd256_ref[...], ksg_ref[...]).astype(BF16)
    kw_ref[...] = _seg_rms(p(_C_KW, _C_GN), bd256_ref[...], kwg_ref[...]).astype(BF16)
    vt = _dot_nt(wvt_ref[...], hb)
    vst_ref[...] = vt[:LANES].astype(BF16)
    vwt_ref[...] = vt[LANES:].astype(BF16)
    gn_ref[...] = jax.nn.sigmoid(p(_C_GN, _C_U))
    u_ref[...] = p(_C_U, _C_GA)
    ga_ref[...] = jax.nn.sigmoid(p(_C_GA, _C_GS)).astype(BF16)
    gs_ref[...] = jax.nn.sigmoid(p(_C_GS, _C_END)).astype(BF16)


def _dup_cols(w):
    return jnp.concatenate([w[:, :64], w[:, :64], w[:, 64:], w[:, 64:]], axis=1)


def _block_ones(n):
    return jnp.kron(jnp.eye(n // HEAD_DIM, dtype=F32), jnp.ones((HEAD_DIM, HEAD_DIM), F32)).astype(BF16)


def _proj(x2, mod, g_norm1, w_in, q_gain, ks_gain, kw_gain):
    o = np.cumsum((0, 512, 128, 128, 128, 128, 128, 128, 24, 512, 1024, 1024))
    parts = [w_in[:, o[i]:o[i + 1]] for i in range(11)]
    wq, wkc, wvc, wks, wvs, wkw, wvw, wgn, wu, wga, wgs = parts
    w = jnp.concatenate([wq, wkc, wvc, _dup_cols(wks), _dup_cols(wkw),
                         jnp.pad(wgn, ((0, 0), (0, LANES - 24))), wu, wga, wgs], axis=1).astype(BF16)
    wvt = jnp.concatenate([wvs, wvw], axis=1).T.astype(BF16)
    tm = PROJ_TM
    row = lambda i: (i, 0)
    col = lambda i: (0, i)
    fix = lambda i: (0, 0)
    outs = [(512, BF16, row), (128, F32, row), (128, F32, row), (256, BF16, row), (256, BF16, row),
            (LANES, BF16, col), (LANES, BF16, col),
            (128, F32, row), (512, F32, row), (1024, BF16, row), (1024, BF16, row)]
    ospec = lambda wd, m: pl.BlockSpec((tm, wd), m) if m is row else pl.BlockSpec((wd, tm), m)
    oshape = lambda wd, dt, m: jax.ShapeDtypeStruct((N_TOK, wd) if m is row else (wd, N_TOK), dt)
    return pl.pallas_call(
        _proj_kernel,
        grid=(N_TOK // tm,),
        in_specs=[pl.BlockSpec((tm, D_MODEL), row),
                  pl.BlockSpec((1, 6, D_MODEL), lambda i: (i // (SEQ // tm), 0, 0)),
                  pl.BlockSpec((1, D_MODEL), fix),
                  pl.BlockSpec((D_MODEL, _C_END), fix),
                  pl.BlockSpec((2 * LANES, D_MODEL), fix),
                  pl.BlockSpec((1, 512), fix), pl.BlockSpec((1, 256), fix), pl.BlockSpec((1, 256), fix),
                  pl.BlockSpec((512, 512), fix), pl.BlockSpec((256, 256), fix)],
        out_specs=[ospec(wd, m) for wd, _, m in outs],
        out_shape=[oshape(wd, dt, m) for wd, dt, m in outs],
        compiler_params=_cparams("arbitrary"),
        name="proj",
    )(x2, mod, g_norm1.reshape(1, -1), w, wvt,
      jnp.tile(q_gain, N_HEADS).reshape(1, -1), jnp.tile(ks_gain, 4).reshape(1, -1),
      jnp.tile(kw_gain, 4).reshape(1, -1), _block_ones(512), _block_ones(256))


def _compress_kernel(r_ref, pe_ref, w1_ref, w2_ref, bd_ref, gain_ref, o_ref, *, do_norm):
    r = jnp.concatenate([r_ref[0, :, l, :] for l in range(CMP_STRIDE)], axis=1)
    p0 = _dot((r + pe_ref[0]).astype(BF16), w1_ref[0])
    p1 = _dot((r + pe_ref[1]).astype(BF16), w1_ref[1])
    hid = p0 + pltpu.roll(p1, N_CMP - 1, 0)
    c = _dot(_gelu(hid).astype(BF16), w2_ref[...])
    if do_norm:
        c = _seg_rms(c, bd_ref[...], gain_ref[...])
    o_ref[0] = c.astype(BF16)


def _compress(raw, pe, w1, w2, gain, do_norm):
    r = raw.reshape(BATCH, SEQ // CMP_STRIDE, CMP_STRIDE, LANES)
    eye = jnp.eye(N_KV, dtype=F32)
    w1r = w1.reshape(2, CMP_STRIDE, HEAD_DIM, CMP_HIDDEN)
    w1big = jnp.einsum('hldc,gk->hlgdkc', w1r, eye).reshape(2, CMP_STRIDE * LANES, N_KV * CMP_HIDDEN).astype(BF16)
    w2big = jnp.einsum('cd,gk->gckd', w2, eye)
    w2big = jnp.concatenate([w2big, w2big], axis=-1).reshape(N_KV * CMP_HIDDEN, 4 * HEAD_DIM).astype(BF16)
    pe_big = jnp.broadcast_to(pe.reshape(2, CMP_STRIDE, 1, HEAD_DIM), (2, CMP_STRIDE, N_KV, HEAD_DIM))
    pe_big = pe_big.reshape(2, 1, CMP_STRIDE * LANES)
    fix2 = lambda b: (0, 0)
    fix3 = lambda b: (0, 0, 0)
    return pl.pallas_call(
        functools.partial(_compress_kernel, do_norm=do_norm),
        grid=(BATCH,),
        in_specs=[pl.BlockSpec((1, N_CMP, CMP_STRIDE, LANES), lambda b: (b, 0, 0, 0)),
                  pl.BlockSpec((2, 1, CMP_STRIDE * LANES), fix3),
                  pl.BlockSpec((2, CMP_STRIDE * LANES, N_KV * CMP_HIDDEN), fix3),
                  pl.BlockSpec((N_KV * CMP_HIDDEN, 256), fix2),
                  pl.BlockSpec((256, 256), fix2), pl.BlockSpec((1, 256), fix2)],
        out_specs=pl.BlockSpec((1, N_CMP, 256), lambda b: (b, 0, 0)),
        out_shape=jax.ShapeDtypeStruct((BATCH, N_CMP, 256), BF16),
        compiler_params=_cparams("arbitrary"),
        name="compress_k" if do_norm else "compress_v",
    )(r, pe_big, w1big, w2big, _block_ones(256), jnp.tile(gain, 4).reshape(1, -1))


ATT_TQ = 256
RANK_CHUNK = 16


def _head_variants(qb):
    lane = lax.broadcasted_iota(I32, qb.shape, 1)
    z = jnp.zeros_like(qb)
    return jnp.where(lane < HEAD_DIM, qb, z), jnp.where(lane < HEAD_DIM, z, qb)


def _cmp_kernel(q_ref, kc_ref, vc_ref, ov_ref, o_ref, sel_ref, vrank_ref):
    tq = ATT_TQ
    qi = pl.program_id(1)
    tpos = qi * tq + lax.broadcasted_iota(I32, (tq, N_CMP), 0)
    nidx = lax.broadcasted_iota(I32, (tq, N_CMP), 1)
    mask = (CMP_STRIDE * nidx + (CMP_BLOCK - 1)) <= tpos
    lane_lo = lax.broadcasted_iota(I32, (tq, LANES), 1) < HEAD_DIM
    for g in range(N_KV):
        kd = kc_ref[0, :, g * LANES:(g + 1) * LANES]
        vd = vc_ref[0, :, g * LANES:(g + 1) * LANES]
        psum = jnp.zeros((tq, N_CMP), F32)
        for jb in range(2):
            blk = 2 * g + jb
            pv = []
            for qv in _head_variants(q_ref[:, blk * LANES:(blk + 1) * LANES]):
                s = jnp.where(mask, _dot_nt(qv, kd), NEG)
                m = jnp.max(s, axis=-1, keepdims=True)
                e = jnp.where(mask, jnp.exp2(s - m), 0.0)
                l = jnp.sum(e, axis=-1, keepdims=True)
                p = e / jnp.where(l > 0.0, l, 1.0)
                psum = psum + p
                pv.append(_dot(p.astype(BF16), vd))
            o_ref[:, blk * LANES:(blk + 1) * LANES] = jnp.where(lane_lo, pv[0], pv[1]).astype(BF16)
        imp = _split_dot(psum, ov_ref[...])
        imp_t = imp.T[:N_SEL_BLOCKS]
        j = lax.broadcasted_iota(I32, (N_SEL_BLOCKS, tq), 0)
        cur = jnp.right_shift(qi * tq + lax.broadcasted_iota(I32, (N_SEL_BLOCKS, tq), 1), 6)
        forced = (j == 0) | (j == cur) | (j == cur - 1)
        v = jnp.where(forced, jnp.inf, jnp.where(j <= cur, imp_t, -jnp.inf))
        vrank_ref[...] = jnp.zeros((N_SEL_BLOCKS, tq), F32)
        n_live = (qi + 1) * (tq // SEL_BLOCK)
        for c0 in range(0, N_SEL_BLOCKS, RANK_CHUNK):
            @pl.when(c0 < n_live)
            def _():
                rank = vrank_ref[...]
                for jp in range(c0, c0 + RANK_CHUNK):
                    row = v[jp:jp + 1, :]
                    tie = jnp.where(j > jp, 1.0, 0.0)
                    rank = rank + jnp.where(row > v, 1.0, jnp.where(row == v, tie, 0.0))
                vrank_ref[...] = rank
        rank = vrank_ref[...]
        sel_ref[g * N_SEL_BLOCKS:(g + 1) * N_SEL_BLOCKS, :] = jnp.where(rank < float(N_SELECT), 0.0, NEG)


def _cmp_attn(q, kcn, vcn):
    nc = np.arange(N_CMP)
    sb = np.arange(LANES)
    ov = ((CMP_STRIDE * nc[:, None] < SEL_BLOCK * sb[None, :] + SEL_BLOCK)
          & (CMP_STRIDE * nc[:, None] + CMP_BLOCK > SEL_BLOCK * sb[None, :])
          & (nc[:, None] < N_CMP - 1) & (sb[None, :] < N_SEL_BLOCKS))
    ov = jnp.asarray(ov, BF16)
    tq = ATT_TQ
    nq = SEQ // tq
    row = lambda b, i: (b * nq + i, 0)
    return pl.pallas_call(
        _cmp_kernel,
        grid=(BATCH, nq),
        in_specs=[pl.BlockSpec((tq, NSA_WIDTH), row),
                  pl.BlockSpec((1, N_CMP, 256), lambda b, i: (b, 0, 0)),
                  pl.BlockSpec((1, N_CMP, 256), lambda b, i: (b, 0, 0)),
                  pl.BlockSpec((N_CMP, LANES), lambda b, i: (0, 0))],
        out_specs=[pl.BlockSpec((tq, NSA_WIDTH), row),
                   pl.BlockSpec((N_KV * N_SEL_BLOCKS, tq), lambda b, i: (0, b * nq + i))],
        out_shape=[jax.ShapeDtypeStruct((N_TOK, NSA_WIDTH), BF16),
                   jax.ShapeDtypeStruct((N_KV * N_SEL_BLOCKS, N_TOK), F32)],
        scratch_shapes=[pltpu.VMEM((N_SEL_BLOCKS, tq), F32)],
        compiler_params=_cparams("arbitrary", "arbitrary"),
        name="cmp_attn",
    )(q, kcn, vcn, ov)


ATT_TK = 256


M_INIT = -1e29


SUM_ROWS = 16


def _selwin_kernel(q_ref, ks_ref, kw_ref, vst_ref, vwt_ref, selb_ref, osel_ref, owin_ref, m_ref, acc_ref):
    tq, tk = ATT_TQ, ATT_TK
    qi = pl.program_id(1)
    krow = lax.broadcasted_iota(I32, (tk, tq), 0)
    qcol = lax.broadcasted_iota(I32, (tk, tq), 1)
    causal_bias = jnp.where(krow <= qcol, 0.0, NEG)
    far_bias = jnp.where(qcol < krow, 0.0, NEG)

    ones_rows = jnp.ones((SUM_ROWS, tk), BF16)

    def reset():
        m_ref[...] = jnp.full(m_ref.shape, M_INIT, F32)
        acc_ref[...] = jnp.zeros(acc_ref.shape, F32)

    def update(g, k_ref, vt_ref, kt, bias):
        k0 = pl.multiple_of(kt * tk, tk)
        kd = k_ref[0, pl.ds(k0, tk), g * LANES:(g + 1) * LANES]
        vt = vt_ref[g * HEAD_DIM:(g + 1) * HEAD_DIM, pl.ds(k0, tk)]
        s = _dot_nt(kd, qvars[g])
        if bias is not None:
            s = s + jnp.concatenate([bias] * 4, axis=1)
        m_old = m_ref[g]
        m_new = jnp.maximum(m_old, jnp.max(s, axis=0, keepdims=True))
        alpha = jnp.exp2(m_old - m_new)
        p = jnp.exp2(s - m_new)
        m_ref[g] = m_new
        vte = jnp.concatenate([vt, ones_rows], axis=0)
        acc_ref[g] = alpha * acc_ref[g] + _dot(vte, p.astype(BF16))

    def finish(out_ref, g):
        o = acc_ref[g, :HEAD_DIM, :] / acc_ref[g, HEAD_DIM:HEAD_DIM + 1, :]
        for jb in range(2):
            blk = 2 * g + jb
            pair = jnp.concatenate([o[:, 2 * jb * tq:(2 * jb + 1) * tq], o[:, (2 * jb + 1) * tq:(2 * jb + 2) * tq]],
                                   axis=0)
            out_ref[:, blk * LANES:(blk + 1) * LANES] = pair.T.astype(BF16)

    def sel_bias(g, kt):
        rows = [jnp.broadcast_to(selb_ref[pl.ds(g * N_SEL_BLOCKS + kt * (tk // SEL_BLOCK) + r, 1), :],
                                 (SEL_BLOCK, tq)) for r in range(tk // SEL_BLOCK)]
        return jnp.concatenate(rows, axis=0)

    qvars = []
    for g in range(N_KV):
        heads = []
        for jb in range(2):
            heads.extend(_head_variants(q_ref[:, (2 * g + jb) * LANES:(2 * g + jb + 1) * LANES]))
        qvars.append(jnp.concatenate(heads, axis=0))
    groups = range(N_KV)

    reset()

    def sel_step(kt, carry):
        for g in groups:
            update(g, ks_ref, vst_ref, kt, sel_bias(g, kt))
        return carry

    lax.fori_loop(0, qi, sel_step, 0)
    for g in groups:
        update(g, ks_ref, vst_ref, qi, sel_bias(g, qi) + causal_bias)
    for g in groups:
        finish(osel_ref, g)

    reset()

    @pl.when(qi >= 2)
    def _():
        for g in groups:
            update(g, kw_ref, vwt_ref, qi - 2, far_bias)

    @pl.when(qi >= 1)
    def _():
        for g in groups:
            update(g, kw_ref, vwt_ref, qi - 1, None)

    for g in groups:
        update(g, kw_ref, vwt_ref, qi, causal_bias)
    for g in groups:
        finish(owin_ref, g)


def _selwin(q, ks, kw, vst, vwt, selb):
    tq = ATT_TQ
    nq = SEQ // tq
    assert WINDOW == 2 * ATT_TK and ATT_TQ == ATT_TK
    row = lambda b, i: (b * nq + i, 0)
    keys = pl.BlockSpec((1, SEQ, 256), lambda b, i: (b, 0, 0))
    vals = pl.BlockSpec((LANES, SEQ), lambda b, i: (0, b))
    r3 = lambda a: a.reshape(BATCH, SEQ, 256)
    return pl.pallas_call(
        _selwin_kernel,
        grid=(BATCH, nq),
        in_specs=[pl.BlockSpec((tq, NSA_WIDTH), row), keys, keys, vals, vals,
                  pl.BlockSpec((N_KV * N_SEL_BLOCKS, tq), lambda b, i: (0, b * nq + i))],
        out_specs=[pl.BlockSpec((tq, NSA_WIDTH), row)] * 2,
        out_shape=[jax.ShapeDtypeStruct((N_TOK, NSA_WIDTH), BF16)] * 2,
        scratch_shapes=[pltpu.VMEM((N_KV, 1, 4 * tq), F32),
                        pltpu.VMEM((N_KV, HEAD_DIM + SUM_ROWS, 4 * tq), F32)],
        compiler_params=_cparams("arbitrary", "arbitrary"),
        name="selwin",
    )(q, r3(ks), r3(kw), vst, vwt, selb)


def _s5_param_kernel(are_ref, aim_ref, ldt_ref, cre_ref, cim_ref, bre_ref, bim_ref,
                     clre_ref, clim_ref, wbre_ref, wbim_ref, bbre_ref, bbim_ref, ltre_ref, ltim_ref):
    are, aim = are_ref[...], aim_ref[...]
    dt = jnp.exp(ldt_ref[...])
    cre, cim = cre_ref[...], cim_ref[...]

    def lam_pow(tau):
        mag = jnp.exp(are * dt * float(tau))
        ang = aim * dt * float(tau)
        return mag * jnp.cos(ang), mag * jnp.sin(ang)

    lre, lim = lam_pow(1)
    den = are * are + aim * aim
    qre = ((lre - 1.0) * are + lim * aim) / den
    qim = (lim * are - (lre - 1.0) * aim) / den
    bre, bim = bre_ref[...], bim_ref[...]
    bbre = qre * bre - qim * bim
    bbim = qre * bim + qim * bre
    bbre_ref[...] = bbre
    bbim_ref[...] = bbim
    for tau in range(S5_T + 1):
        pr, pi = lam_pow(tau)
        clre_ref[tau] = cre * pr - cim * pi
        clim_ref[tau] = cre * pi + cim * pr
        if tau < S5_T:
            k = S5_T - 1 - tau
            wbre_ref[k] = pr * bbre - pi * bbim
            wbim_ref[k] = pr * bbim + pi * bbre
        else:
            ltre_ref[...] = pr
            ltim_ref[...] = pi


def _s5_kmat_kernel(l_ref, r_ref, o_ref):
    o_ref[0] = jnp.dot(l_ref[0], r_ref[0], preferred_element_type=F32, precision=HIGHEST)


def _s5_params(a_re, a_im, log_dt, b_re, b_im, c_re, c_im):
    T = S5_T
    pn = GROUP * STATE
    tile_p = lambda a: jnp.tile(a, (1, GROUP))
    args = (tile_p(a_re), tile_p(a_im), jnp.broadcast_to(log_dt[:, None], (N_GROUPS, pn)),
            c_re.reshape(N_GROUPS, pn), c_im.reshape(N_GROUPS, pn),
            jnp.swapaxes(b_re, 1, 2).reshape(N_GROUPS, pn), jnp.swapaxes(b_im, 1, 2).reshape(N_GROUPS, pn))
    full2 = pl.BlockSpec((N_GROUPS, pn), lambda: (0, 0))
    clre, clim, wbre, wbim, bbre, bbim, ltre, ltim = pl.pallas_call(
        _s5_param_kernel,
        in_specs=[full2] * 7,
        out_specs=[pl.BlockSpec((T + 1, N_GROUPS, pn), lambda: (0, 0, 0))] * 2
                  + [pl.BlockSpec((T, N_GROUPS, pn), lambda: (0, 0, 0))] * 2 + [full2] * 4,
        out_shape=[jax.ShapeDtypeStruct((T + 1, N_GROUPS, pn), F32)] * 2
                  + [jax.ShapeDtypeStruct((T, N_GROUPS, pn), F32)] * 2
                  + [jax.ShapeDtypeStruct((N_GROUPS, pn), F32)] * 4,
        name="s5_params",
    )(*args)

    r5 = lambda a, t: a[:t].reshape(t, N_GROUPS, GROUP, STATE)
    lhs = jnp.concatenate([r5(clre, T), -r5(clim, T)], axis=-1)
    lhs = jnp.transpose(lhs, (1, 0, 2, 3)).reshape(N_GROUPS, T * GROUP, 2 * STATE)
    bb = lambda a: jnp.swapaxes(a.reshape(N_GROUPS, GROUP, STATE), 1, 2)
    rhs = jnp.concatenate([bb(bbre), bb(bbim)], axis=1)
    kmat = pl.pallas_call(
        _s5_kmat_kernel,
        grid=(N_GROUPS,),
        in_specs=[pl.BlockSpec((1, T * GROUP, 2 * STATE), lambda g: (g, 0, 0)),
                  pl.BlockSpec((1, 2 * STATE, GROUP), lambda g: (g, 0, 0))],
        out_specs=pl.BlockSpec((1, T * GROUP, GROUP), lambda g: (g, 0, 0)),
        out_shape=jax.ShapeDtypeStruct((N_GROUPS, T * GROUP, GROUP), F32),
        compiler_params=_cparams("arbitrary"),
        name="s5_kmat",
    )(lhs, rhs)

    eye = jnp.eye(S5_GL, dtype=F32)
    kt = kmat.reshape(S5_SG, S5_GL, T, GROUP, GROUP)
    kbd = jnp.einsum('sgtpq,gh->stgqhp', kt, eye).reshape(S5_SG, T, LANES, LANES)
    krev = kbd[:, ::-1].reshape(S5_SG, T * LANES, LANES).astype(BF16)
    r6 = lambda a: a.reshape(T, S5_SG, S5_GL, GROUP, STATE)
    wb = jnp.stack([r6(wbre), r6(wbim)], axis=-2)
    wb = jnp.einsum('ksgpin,gh->skgpihn', wb, eye).reshape(S5_SG, T * LANES, S5_NSTATE).astype(BF16)
    wc = jnp.stack([r6(clre[1:]), -r6(clim[1:])], axis=-2)
    wc = jnp.einsum('tsgpin,gh->signthp', wc, eye).reshape(S5_SG, S5_NSTATE, T * LANES).astype(BF16)
    lt = lambda a: a.reshape(N_GROUPS, GROUP, STATE)[:, 0].reshape(S5_SG, 1, S5_GL * STATE)
    return krev, wb, wc, lt(ltre), lt(ltim)


S5_TC = 256


def _s5_lane_block(sg):
    return pl.ds(pl.multiple_of(sg * LANES, LANES), LANES)


def _s5_chunk_inputs(x_ref, sg):
    return jnp.concatenate([x_ref[:, t, _s5_lane_block(sg)] for t in range(S5_T)], axis=1).astype(BF16)


def _s5_state_kernel(x_ref, wb_ref, e_ref):
    e_ref[0] = _dot(_s5_chunk_inputs(x_ref, pl.program_id(1)), wb_ref[0])


def _s5_scan_kernel(e_ref, ltre_ref, ltim_ref, xs_ref):
    lr, li = ltre_ref[0], ltim_ref[0]
    half = S5_NSTATE // 2

    def step(c, carry):
        new = []
        for b in range(BATCH):
            xr, xi = carry[b]
            row = b * S5_CH_PER_BATCH + c
            xs_ref[0, pl.ds(row, 1), :half] = xr
            xs_ref[0, pl.ds(row, 1), half:] = xi
            e = e_ref[0, pl.ds(row, 1), :]
            new.append((lr * xr - li * xi + e[:, :half], lr * xi + li * xr + e[:, half:]))
        return tuple(new)

    zero = jnp.zeros((1, half), F32)
    lax.fori_loop(0, S5_CH_PER_BATCH, step, tuple((zero, zero) for _ in range(BATCH)))


def _s5_out_kernel(x_ref, xs_ref, krev_ref, wc_ref, y_ref):
    sg = pl.program_id(1)
    x = _s5_chunk_inputs(x_ref, sg)
    xsb = xs_ref[0].astype(BF16)
    for t in range(S5_T):
        cols = slice(t * LANES, (t + 1) * LANES)
        y_ref[:, t, _s5_lane_block(sg)] = (_dot(x[:, :(t + 1) * LANES], krev_ref[0, (S5_T - 1 - t) * LANES:, :])
                                           + _dot(xsb, wc_ref[0, :, cols]))


def _s5(u, krev, wb, wc, ltre, ltim):
    T, tc = S5_T, S5_TC
    xn = u.reshape(S5_CH, T, SSM_WIDTH)
    grid = (S5_CH // tc, S5_SG)
    natural = pl.BlockSpec((tc, T, SSM_WIDTH), lambda i, s: (i, 0, 0))
    rows = lambda i, s: (s, i, 0)
    per_sg = lambda i, s: (s, 0, 0)
    e = pl.pallas_call(
        _s5_state_kernel, grid=grid,
        in_specs=[natural, pl.BlockSpec((1, T * LANES, S5_NSTATE), per_sg)],
        out_specs=pl.BlockSpec((1, tc, S5_NSTATE), rows),
        out_shape=jax.ShapeDtypeStruct((S5_SG, S5_CH, S5_NSTATE), F32),
        compiler_params=_cparams("arbitrary", "arbitrary"), name="s5_state",
    )(xn, wb)
    sg1 = lambda s: (s, 0, 0)
    xstart = pl.pallas_call(
        _s5_scan_kernel, grid=(S5_SG,),
        in_specs=[pl.BlockSpec((1, S5_CH, S5_NSTATE), sg1),
                  pl.BlockSpec((1, 1, S5_NSTATE // 2), sg1), pl.BlockSpec((1, 1, S5_NSTATE // 2), sg1)],
        out_specs=pl.BlockSpec((1, S5_CH, S5_NSTATE), sg1),
        out_shape=jax.ShapeDtypeStruct((S5_SG, S5_CH, S5_NSTATE), F32),
        compiler_params=_cparams("arbitrary"), name="s5_scan",
    )(e, ltre, ltim)
    y = pl.pallas_call(
        _s5_out_kernel, grid=grid,
        in_specs=[natural, pl.BlockSpec((1, tc, S5_NSTATE), rows),
                  pl.BlockSpec((1, T * LANES, LANES), per_sg), pl.BlockSpec((1, S5_NSTATE, T * LANES), per_sg)],
        out_specs=natural,
        out_shape=jax.ShapeDtypeStruct((S5_CH, T, SSM_WIDTH), F32),
        compiler_params=_cparams("arbitrary", "arbitrary", vmem=56 * 1024 * 1024), name="s5_out",
    )(xn, xstart, krev, wc)
    return y.reshape(N_TOK, SSM_WIDTH)


MERGE_TM = 256


def _merge_kernel(ocmp_ref, osel_ref, owin_ref, gn_ref, yssm_ref, u_ref, ga_ref, gs_ref, x_ref, mod_ref,
                  eg_ref, dskip_ref, wglu_ref, bglu_ref, wua_ref, wus_ref, wout_ref, g2_ref,
                  wrhi_ref, wrlo_ref, wsgu_ref, wsd_ref,
                  xpart_ref, h2_ref, logit_ref):
    mod = mod_ref[0]
    gnb = gn_ref[...].astype(BF16)
    o_nsa = (_dot(gnb, eg_ref[0]) * ocmp_ref[...].astype(F32)
             + _dot(gnb, eg_ref[1]) * osel_ref[...].astype(F32)
             + _dot(gnb, eg_ref[2]) * owin_ref[...].astype(F32))
    attn = _dot(o_nsa.astype(BF16), wua_ref[...])
    z = _gelu(yssm_ref[...] + dskip_ref[...] * u_ref[...])
    y_ssm = z * jax.nn.sigmoid(_dot(z.astype(BF16), wglu_ref[...]) + bglu_ref[...])
    ssm = _dot(y_ssm.astype(BF16), wus_ref[...])
    merged = ga_ref[...].astype(F32) * attn + gs_ref[...].astype(F32) * ssm
    x1 = x_ref[...] + mod[2:3] * _dot(merged.astype(BF16), wout_ref[...])

    ms = jnp.mean(x1 * x1, axis=-1, keepdims=True)
    h2 = (x1 * lax.rsqrt(ms + EPS) * g2_ref[...]) * (1.0 + mod[4:5]) + mod[3:4]
    hi = h2.astype(BF16)
    lo = (h2 - hi.astype(F32)).astype(BF16)
    h2_ref[...] = _pack_bf16_pairs(h2)
    logit_ref[...] = _dot_nt(wrhi_ref[...], hi) + _dot_nt(wrhi_ref[...], lo) + _dot_nt(wrlo_ref[...], hi)
    gu = _dot(hi, wsgu_ref[...])
    shared = _dot((_silu(gu[:, :D_EXPERT]) * gu[:, D_EXPERT:]).astype(BF16), wsd_ref[...])
    xpart_ref[...] = x1 + mod[5:6] * shared


def _merge(ocmp, osel, owin, gn, yssm, u, ga, gs, x2, mod, d_skip, w_glu, b_glu, w_up_attn, w_up_ssm, w_out,
           g_norm2, w_router, ws_gate, ws_up, ws_down):
    tm = MERGE_TM
    eg = np.zeros((3, LANES, NSA_WIDTH), np.float32)
    for j in range(3):
        for h in range(N_HEADS):
            eg[j, 3 * h + j, h * HEAD_DIM:(h + 1) * HEAD_DIM] = 1.0
    wr_t = w_router.T
    wr_hi = wr_t.astype(BF16)
    wr_lo = (wr_t - wr_hi.astype(F32)).astype(BF16)
    row = lambda i: (i, 0)
    fix2 = lambda i: (0, 0)
    wspec = lambda a: pl.BlockSpec(a.shape, (lambda i: (0,) * a.ndim))
    weights = [jnp.asarray(eg, BF16), d_skip.reshape(1, -1), w_glu.astype(BF16), b_glu.reshape(1, -1),
               w_up_attn.astype(BF16), w_up_ssm.astype(BF16), w_out.astype(BF16), g_norm2.reshape(1, -1),
               wr_hi, wr_lo, jnp.concatenate([ws_gate, ws_up], axis=1).astype(BF16), ws_down.astype(BF16)]
    acts = [(ocmp, 512), (osel, 512), (owin, 512), (gn, 128), (yssm, 512), (u, 512), (ga, 1024), (gs, 1024),
            (x2, 1024)]
    return pl.pallas_call(
        _merge_kernel,
        grid=(N_TOK // tm,),
        in_specs=[pl.BlockSpec((tm, wd), row) for _, wd in acts]
                 + [pl.BlockSpec((1, 6, D_MODEL), lambda i: (i // (SEQ // tm), 0, 0))]
                 + [wspec(w) for w in weights],
        out_specs=[pl.BlockSpec((tm, D_MODEL), row), pl.BlockSpec((tm, HALF), row),
                   pl.BlockSpec((N_EXPERTS, tm), lambda i: (0, i))],
        out_shape=[jax.ShapeDtypeStruct((N_TOK, D_MODEL), F32), jax.ShapeDtypeStruct((N_TOK, HALF), jnp.uint32),
                   jax.ShapeDtypeStruct((N_EXPERTS, N_TOK), F32)],
        compiler_params=_cparams("arbitrary"),
        name="merge",
    )(*[a for a, _ in acts], mod, *weights)


ROUTE_TN = 512


def _route_kernel(logit_ref, bias_ref, eidx_ref, w_ref, count_ref, gscore_ref, masked_ref):
    tn = ROUTE_TN

    @pl.when(pl.program_id(0) == 0)
    def _():
        count_ref[...] = jnp.zeros(count_ref.shape, F32)

    sc = jax.nn.sigmoid(logit_ref[...])
    biased = sc + bias_ref[...]
    gi = lax.broadcasted_iota(I32, (EXPERTS_PER_GROUP, tn), 0).astype(F32)
    for g in range(N_EXPERT_GROUPS):
        blk = biased[g * EXPERTS_PER_GROUP:(g + 1) * EXPERTS_PER_GROUP]
        m1 = jnp.max(blk, axis=0, keepdims=True)
        i1 = jnp.min(jnp.where(blk == m1, gi, float(EXPERTS_PER_GROUP)), axis=0, keepdims=True)
        m2 = jnp.max(jnp.where(gi == i1, -jnp.inf, blk), axis=0, keepdims=True)
        gscore_ref[g:g + 1, :] = m1 + m2
    gs = gscore_ref[...]
    gidx = lax.broadcasted_iota(I32, (N_EXPERT_GROUPS, tn), 0)
    grank = jnp.zeros((N_EXPERT_GROUPS, tn), F32)
    for gp in range(N_EXPERT_GROUPS):
        row = gs[gp:gp + 1, :]
        tie = jnp.where(gidx > gp, 1.0, 0.0)
        grank = grank + jnp.where(row > gs, 1.0, jnp.where(row == gs, tie, 0.0))
    for g in range(N_EXPERT_GROUPS):
        keep = grank[g:g + 1, :] < float(TOPK_GROUPS)
        sl = slice(g * EXPERTS_PER_GROUP, (g + 1) * EXPERTS_PER_GROUP)
        masked_ref[sl, :] = jnp.where(keep, biased[sl], -jnp.inf)
    cur = masked_ref[...]
    eidx = lax.broadcasted_iota(I32, (N_EXPERTS, tn), 0).astype(F32)
    wsum = jnp.zeros((1, tn), F32)
    hits = jnp.zeros((N_EXPERTS, tn), F32)
    for k in range(TOP_K):
        m = jnp.max(cur, axis=0, keepdims=True)
        idx = jnp.min(jnp.where(cur == m, eidx, float(N_EXPERTS)), axis=0, keepdims=True)
        hit = eidx == idx
        wk = jnp.sum(jnp.where(hit, sc, 0.0), axis=0, keepdims=True)
        cur = jnp.where(hit, -jnp.inf, cur)
        hits = hits + jnp.where(hit, 1.0, 0.0)
        eidx_ref[k:k + 1, :] = idx.astype(I32)
        w_ref[k:k + 1, :] = wk
        wsum = wsum + wk
    w_ref[...] = w_ref[...] / wsum * ROUTE_SCALE
    count_ref[...] = count_ref[...] + jnp.sum(hits, axis=1, keepdims=True)


def _route(logits_t, router_bias):
    tn = ROUTE_TN
    return pl.pallas_call(
        _route_kernel,
        grid=(N_TOK // tn,),
        in_specs=[pl.BlockSpec((N_EXPERTS, tn), lambda i: (0, i)), pl.BlockSpec((N_EXPERTS, 1), lambda i: (0, 0))],
        out_specs=[pl.BlockSpec((TOP_K, tn), lambda i: (0, i))] * 2 + [pl.BlockSpec((N_EXPERTS, 1), lambda i: (0, 0))],
        out_shape=[jax.ShapeDtypeStruct((TOP_K, N_TOK), I32), jax.ShapeDtypeStruct((TOP_K, N_TOK), F32),
                   jax.ShapeDtypeStruct((N_EXPERTS, 1), F32)],
        scratch_shapes=[pltpu.VMEM((N_EXPERT_GROUPS, tn), F32), pltpu.VMEM((N_EXPERTS, tn), F32)],
        compiler_params=_cparams("arbitrary"),
        name="route",
    )(logits_t, router_bias.reshape(-1, 1))


N_MOE_BLK = NK // DISPATCH_BLOCK
N_ITEMS = N_MOE_BLK + N_EXPERTS
ASSIGN_BITS = 17


def _dispatch_plan(eidx, counts):
    e_flat = eidx.reshape(-1)
    key = jnp.sort(e_flat * NK + jnp.arange(NK, dtype=I32))
    order = key & (NK - 1)
    counts = counts.reshape(-1).astype(I32)
    start = jnp.cumsum(counts) - counts
    cuts = jnp.sort(jnp.concatenate([jnp.arange(N_MOE_BLK, dtype=I32) * DISPATCH_BLOCK, start]))
    lo = cuts
    hi = jnp.concatenate([cuts[1:], jnp.full((1,), NK, I32)])
    blk = jnp.minimum(lo // DISPATCH_BLOCK, N_MOE_BLK - 1)
    expert = jnp.clip(jnp.sum((start[None, :] <= lo[:, None]).astype(I32), axis=1) - 1, 0, N_EXPERTS - 1)
    one = jnp.ones((1,), I32)
    first = jnp.concatenate([one, (blk[1:] != blk[:-1]).astype(I32)])
    last = jnp.concatenate([(blk[1:] != blk[:-1]).astype(I32), one])
    new_expert = jnp.concatenate([one, (expert[1:] != expert[:-1]).astype(I32)])
    run_id = jnp.cumsum(new_expert) - 1
    n_runs = run_id[-1] + 1
    item = jnp.arange(N_ITEMS, dtype=I32)
    run_first_item = jnp.sort(jnp.where(new_expert == 1, item, N_ITEMS))
    run_expert = expert[jnp.minimum(run_first_item, N_ITEMS - 1)]
    ahead = run_id + (WEIGHT_RING - 1)
    ahead_expert = run_expert[jnp.minimum(ahead, N_ITEMS - 1)]
    ahead_valid = (ahead < n_runs).astype(I32)
    second_expert = run_expert[1:2]
    prologue = jnp.concatenate([second_expert, (n_runs > 1).astype(I32).reshape(1)])
    tok = jnp.right_shift(order, 3)
    home = (order & (TOP_K - 1)) * N_TOK + tok
    return tok, home, (blk, expert, lo - blk * DISPATCH_BLOCK, hi - blk * DISPATCH_BLOCK, first, last, new_expert,
                      run_id % WEIGHT_RING, ahead_expert, ahead_valid, prologue)


SC_CORES = 2
SC_SUBCORES = 16
SC_CHUNK = 128


def _sc_move_rows(table, idx, scatter):
    n = idx.shape[0]
    workers = SC_CORES * SC_SUBCORES
    per_worker = n // workers
    n_chunks = per_worker // SC_CHUNK
    assert per_worker * workers == n and n_chunks * SC_CHUNK == per_worker
    mesh = plsc.VectorSubcoreMesh(core_axis_name="c", subcore_axis_name="s",
                                  num_cores=SC_CORES, num_subcores=SC_SUBCORES)

    def body(table_hbm, idx_hbm, out_hbm, idx_v, rows_v, sem):
        wid = lax.axis_index("s") * SC_CORES + lax.axis_index("c")
        base = wid * per_worker

        @pl.loop(0, n_chunks)
        def _(j):
            off = base + j * SC_CHUNK
            pltpu.sync_copy(idx_hbm.at[pl.ds(off, SC_CHUNK)], idx_v)
            if scatter:
                pltpu.sync_copy(table_hbm.at[pl.ds(off, SC_CHUNK)], rows_v)
                pltpu.async_copy(rows_v, out_hbm.at[idx_v], sem).wait()
            else:
                pltpu.async_copy(table_hbm.at[idx_v], rows_v, sem).wait()
                pltpu.sync_copy(rows_v, out_hbm.at[pl.ds(off, SC_CHUNK)])

    return pl.kernel(
        body,
        out_type=jax.ShapeDtypeStruct((n, table.shape[1]), table.dtype),
        mesh=mesh,
        scratch_types=[pltpu.VMEM((SC_CHUNK,), I32), pltpu.VMEM((SC_CHUNK, table.shape[1]), table.dtype),
                       pltpu.SemaphoreType.DMA],
        name="sc_scatter_rows" if scatter else "sc_gather_rows",
    )(table, idx)


WEIGHT_RING = 3
WEIGHT_CHUNKS = 4


def _expert_weight_copies(w_hbm, wbuf, sem, expert, slot):
    rows = w_hbm.shape[1] // WEIGHT_CHUNKS
    return [pltpu.make_async_copy(w_hbm.at[expert, pl.ds(c * rows, rows)],
                                  wbuf.at[slot, pl.ds(c * rows, rows)], sem.at[slot])
            for c in range(WEIGHT_CHUNKS)]


def _moe_kernel(blk_ref, exp_ref, lo_ref, hi_ref, first_ref, last_ref, newexp_ref,
                slot_ref, ahead_exp_ref, ahead_ok_ref, prologue_ref,
                x_ref, wg_hbm, wu_hbm, wd_hbm, y_ref,
                acc_ref, wgf_ref, wuf_ref, wdf_ref, wgb_ref, wub_ref, wdb_ref, wsem):
    it = pl.program_id(0)
    lo, hi = lo_ref[it], hi_ref[it]
    streams = ((wg_hbm, wgf_ref), (wu_hbm, wuf_ref), (wd_hbm, wdf_ref))

    def request(expert, slot):
        for w_hbm, wbuf in streams:
            for cp in _expert_weight_copies(w_hbm, wbuf, wsem, expert, slot):
                cp.start()

    @pl.when(it == 0)
    def _():
        request(exp_ref[0], 0)

        @pl.when(prologue_ref[1] == 1)
        def _():
            request(prologue_ref[0], 1)

    @pl.when(newexp_ref[it] == 1)
    def _():
        slot = slot_ref[it]
        for w_hbm, wbuf in streams:
            for cp in _expert_weight_copies(w_hbm, wbuf, wsem, 0, slot):
                cp.wait()
        wgb_ref[...] = wgf_ref[slot].astype(BF16)
        wub_ref[...] = wuf_ref[slot].astype(BF16)
        wdb_ref[...] = wdf_ref[slot].astype(BF16)

        @pl.when(ahead_ok_ref[it] == 1)
        def _():
            ahead_slot = slot + (WEIGHT_RING - 1)
            request(ahead_exp_ref[it], jnp.where(ahead_slot >= WEIGHT_RING, ahead_slot - WEIGHT_RING, ahead_slot))

    @pl.when(first_ref[it] == 1)
    def _():
        acc_ref[...] = jnp.zeros(acc_ref.shape, F32)

    def expert_pass(r0, nrows):
        rows = slice(r0, r0 + nrows)
        ridx = r0 + lax.broadcasted_iota(I32, (nrows, HALF), 0)
        mine = (ridx >= lo) & (ridx < hi)
        xlo, xhi = _unpack_bf16_pairs(jnp.where(mine, x_ref[rows, :], jnp.uint32(0)))
        xlo, xhi = xlo.astype(BF16), xhi.astype(BF16)
        gate = _dot(xlo, wgb_ref[:HALF]) + _dot(xhi, wgb_ref[HALF:])
        up = _dot(xlo, wub_ref[:HALF]) + _dot(xhi, wub_ref[HALF:])
        acc_ref[rows, :] = acc_ref[rows, :] + _dot((_silu(gate) * up).astype(BF16), wdb_ref[...])

    mid = DISPATCH_BLOCK // 2
    pl.when((lo < mid) & (hi > mid))(lambda: expert_pass(0, DISPATCH_BLOCK))
    pl.when((hi > lo) & (hi <= mid))(lambda: expert_pass(0, mid))
    pl.when((hi > lo) & (lo >= mid))(lambda: expert_pass(mid, mid))

    @pl.when(last_ref[it] == 1)
    def _():
        y_ref[...] = _pack_bf16_pairs(acc_ref[...])


def _moe(xs, items, w_gate, w_up, w_down):
    by_blk = lambda it, blk, *_: (blk[it], 0)
    any_space = pl.BlockSpec(memory_space=pl.ANY)
    grid_spec = pltpu.PrefetchScalarGridSpec(
        num_scalar_prefetch=len(items),
        grid=(N_ITEMS,),
        in_specs=[pl.BlockSpec((DISPATCH_BLOCK, HALF), by_blk), any_space, any_space, any_space],
        out_specs=pl.BlockSpec((DISPATCH_BLOCK, HALF), by_blk),
        scratch_shapes=[pltpu.VMEM((DISPATCH_BLOCK, D_MODEL), F32),
                        pltpu.VMEM((WEIGHT_RING, D_MODEL, D_EXPERT), F32),
                        pltpu.VMEM((WEIGHT_RING, D_MODEL, D_EXPERT), F32),
                        pltpu.VMEM((WEIGHT_RING, D_EXPERT, D_MODEL), F32),
                        pltpu.VMEM((D_MODEL, D_EXPERT), BF16), pltpu.VMEM((D_MODEL, D_EXPERT), BF16),
                        pltpu.VMEM((D_EXPERT, D_MODEL), BF16),
                        pltpu.SemaphoreType.DMA((WEIGHT_RING,))],
    )
    return pl.pallas_call(
        _moe_kernel,
        grid_spec=grid_spec,
        out_shape=jax.ShapeDtypeStruct((NK, HALF), jnp.uint32),
        compiler_params=_cparams("arbitrary"),
        name="moe",
    )(*items, xs, w_gate, w_up, w_down)


COMB_TC = 256


def _combine_kernel(slots_ref, w_ref, xpart_ref, mod_ref, out_ref):
    w = w_ref[...]
    lo = jnp.zeros((w.shape[0], HALF), F32)
    hi = jnp.zeros((w.shape[0], HALF), F32)
    for k in range(TOP_K):
        klo, khi = _unpack_bf16_pairs(slots_ref[k])
        lo = lo + w[:, k:k + 1] * klo
        hi = hi + w[:, k:k + 1] * khi
    gate2 = mod_ref[0][5:6]
    out_ref[:, :HALF] = xpart_ref[:, :HALF] + gate2[:, :HALF] * lo
    out_ref[:, HALF:] = xpart_ref[:, HALF:] + gate2[:, HALF:] * hi


def _combine(xpart, mod, slots, w):
    tc = COMB_TC
    row = lambda i: (i, 0)
    return pl.pallas_call(
        _combine_kernel,
        grid=(N_TOK // tc,),
        in_specs=[pl.BlockSpec((TOP_K, tc, HALF), lambda i: (0, i, 0)),
                  pl.BlockSpec((tc, TOP_K), row),
                  pl.BlockSpec((tc, D_MODEL), row),
                  pl.BlockSpec((1, 6, D_MODEL), lambda i: (i // (SEQ // tc), 0, 0))],
        out_specs=pl.BlockSpec((tc, D_MODEL), row),
        out_shape=jax.ShapeDtypeStruct((N_TOK, D_MODEL), F32),
        compiler_params=_cparams("arbitrary"),
        name="combine",
    )(slots.reshape(TOP_K, N_TOK, HALF), w, xpart, mod)


def _layer(x, c, w_ada, b_ada, g_norm1, g_norm2, w_in, q_gain, kc_gain, ks_gain, kw_gain,
           pe_k, pe_v, w_cmp_k1, w_cmp_k2, w_cmp_v1, w_cmp_v2,
           a_re, a_im, log_dt, b_re, b_im, c_re, c_im, d_skip, w_glu, b_glu,
           w_up_attn, w_up_ssm, w_out, w_router, router_bias,
           w_gate, w_up, w_down, ws_gate, ws_up, ws_down):
    x2 = x.reshape(N_TOK, D_MODEL)
    mod = _ada(c, w_ada, b_ada)
    q, kc_raw, vc_raw, ks, kw, vst, vwt, gn, u, ga, gs = _proj(x2, mod, g_norm1, w_in, q_gain, ks_gain, kw_gain)
    kcn = _compress(kc_raw, pe_k, w_cmp_k1, w_cmp_k2, kc_gain, True)
    vcn = _compress(vc_raw, pe_v, w_cmp_v1, w_cmp_v2, kc_gain, False)
    ocmp, selb = _cmp_attn(q, kcn, vcn)
    osel, owin = _selwin(q, ks, kw, vst, vwt, selb)
    yssm = _s5(u, *_s5_params(a_re, a_im, log_dt, b_re, b_im, c_re, c_im))
    xpart, h2, logits_t = _merge(ocmp, osel, owin, gn, yssm, u, ga, gs, x2, mod, d_skip, w_glu, b_glu,
                                  w_up_attn, w_up_ssm, w_out, g_norm2, w_router, ws_gate, ws_up, ws_down)
    eidx_t, w_t, counts = _route(logits_t, router_bias)
    tok, home, items = _dispatch_plan(eidx_t.T, counts)
    y = _moe(_sc_move_rows(h2, tok, scatter=False), items, w_gate, w_up, w_down)
    slots = _sc_move_rows(y, home, scatter=True)
    return _combine(xpart, mod, slots, w_t.T).reshape(BATCH, SEQ, D_MODEL)


def kernel(x, c, w_ada, b_ada, g_norm1, g_norm2, w_in, q_gain, kc_gain, ks_gain, kw_gain, pe_k, pe_v, w_cmp_k1,
           w_cmp_k2, w_cmp_v1, w_cmp_v2, a_re, a_im, log_dt, b_re, b_im, c_re, c_im, d_skip, w_glu, b_glu,
           w_up_attn, w_up_ssm, w_out, w_router, router_bias, w_gate, w_up, w_down, ws_gate, ws_up, ws_down):
    params = (w_ada, b_ada, g_norm1, g_norm2, w_in, q_gain, kc_gain, ks_gain, kw_gain, pe_k, pe_v, w_cmp_k1,
              w_cmp_k2, w_cmp_v1, w_cmp_v2, a_re, a_im, log_dt, b_re, b_im, c_re, c_im, d_skip, w_glu, b_glu,
              w_up_attn, w_up_ssm, w_out, w_router, router_bias, w_gate, w_up, w_down, ws_gate, ws_up, ws_down)
    depth = w_ada.shape[0]
    for layer in range(depth):
        x = _layer(x, c, *[p[layer] for p in params])
    return x
```

```python
import functools
import math

import jax
import jax.numpy as jnp
import numpy as np
from jax import lax
from jax.experimental import pallas as pl
from jax.experimental.pallas import tpu as pltpu
from jax.experimental.pallas import tpu_sc as plsc

F32 = jnp.float32
BF16 = jnp.bfloat16
I32 = jnp.int32
HIGHEST = lax.Precision.HIGHEST

D_MODEL = 1024
BATCH = 4
SEQ = 4096
N_TOK = BATCH * SEQ
N_HEADS = 8
HEAD_DIM = 64
N_KV = 2
CMP_BLOCK = 32
CMP_STRIDE = 16
CMP_HIDDEN = 256
N_CMP = 256
SEL_BLOCK = 64
N_SEL_BLOCKS = SEQ // SEL_BLOCK
N_SELECT = 16
WINDOW = 512
ATTN_SCALE = HEAD_DIM ** -0.5
LOG2E = 1.4426950408889634
NSA_WIDTH = N_HEADS * HEAD_DIM
SSM_WIDTH = 512
GROUP = 16
N_GROUPS = SSM_WIDTH // GROUP
STATE = 64
N_EXPERTS = 256
TOP_K = 8
D_EXPERT = 256
N_EXPERT_GROUPS = 8
EXPERTS_PER_GROUP = N_EXPERTS // N_EXPERT_GROUPS
TOPK_GROUPS = 4
ROUTE_SCALE = 2.5
DISPATCH_BLOCK = 256
EPS = 1e-6
NEG = -1e30

LANES = 128
S5_T = 16
S5_SG = 4
S5_GL = N_GROUPS // S5_SG
S5_CH = N_TOK // S5_T
S5_CH_PER_BATCH = SEQ // S5_T
S5_NSTATE = S5_GL * STATE * 2

NK = N_TOK * TOP_K
HALF = D_MODEL // 2

VMEM_LIMIT = 48 * 1024 * 1024


def _cparams(*sem, vmem=VMEM_LIMIT):
    return pltpu.CompilerParams(dimension_semantics=tuple(sem), vmem_limit_bytes=vmem)


def _dot(a, b):
    return jnp.dot(a, b, preferred_element_type=F32)


def _dot_nt(a, b):
    return lax.dot_general(a, b, (((1,), (1,)), ((), ())), preferred_element_type=F32)


def _split_dot(v, w):
    hi = v.astype(BF16)
    lo = (v - hi.astype(F32)).astype(BF16)
    return _dot(hi, w) + _dot(lo, w)


def _seg_rms(v, bd, gain):
    ss = _split_dot(v * v, bd)
    return v * lax.rsqrt(ss * (1.0 / HEAD_DIM) + EPS) * gain


def _gelu(x):
    return 0.5 * x * (1.0 + jnp.tanh(0.7978845608028654 * (x + 0.044715 * (x * x * x))))


def _silu(x):
    return x * jax.nn.sigmoid(x)


def _pack_bf16_pairs(v):
    h = v.shape[1] // 2
    return pltpu.pack_elementwise([v[:, :h], v[:, h:]], packed_dtype=BF16)


def _unpack_bf16_pairs(word):
    return (pltpu.unpack_elementwise(word, index=0, packed_dtype=BF16, unpacked_dtype=F32),
            pltpu.unpack_elementwise(word, index=1, packed_dtype=BF16, unpacked_dtype=F32))


def _ada_kernel(c_ref, w_ref, b_ref, o_ref):
    c = c_ref[...]
    o_ref[...] = jnp.dot(_silu(c), w_ref[...], preferred_element_type=F32, precision=HIGHEST) + b_ref[...]


def _ada(c, w_ada, b_ada):
    cp = jnp.pad(c, ((0, 8 - BATCH), (0, 0)))
    tn = 1536
    out = pl.pallas_call(
        _ada_kernel,
        grid=(6 * D_MODEL // tn,),
        in_specs=[pl.BlockSpec((8, D_MODEL), lambda j: (0, 0)),
                  pl.BlockSpec((D_MODEL, tn), lambda j: (0, j)),
                  pl.BlockSpec((1, tn), lambda j: (0, j))],
        out_specs=pl.BlockSpec((8, tn), lambda j: (0, j)),
        out_shape=jax.ShapeDtypeStruct((8, 6 * D_MODEL), F32),
        compiler_params=_cparams("arbitrary"),
        name="ada",
    )(cp, w_ada, b_ada.reshape(1, -1))
    return out.reshape(8, 6, D_MODEL)


_C_Q = 0
_C_KC = 512
_C_VC = 640
_C_KS = 768
_C_KW = 1024
_C_GN = 1280
_C_U = 1408
_C_GA = 1920
_C_GS = 2944
_C_END = 3968
PROJ_TM = 512


def _proj_kernel(x_ref, mod_ref, g1_ref, w_ref, wvt_ref, qg_ref, ksg_ref, kwg_ref, bd512_ref, bd256_ref,
                 q_ref, kc_ref, vc_ref, ks_ref, kw_ref, vst_ref, vwt_ref, gn_ref, u_ref, ga_ref, gs_ref):
    x = x_ref[...]
    ms = jnp.mean(x * x, axis=-1, keepdims=True)
    mod = mod_ref[0]
    h = (x * lax.rsqrt(ms + EPS) * g1_ref[...]) * (1.0 + mod[1:2]) + mod[0:1]
    hb = h.astype(BF16)

    def p(lo, hi):
        return _dot(hb, w_ref[:, lo:hi])

    q_ref[...] = _seg_rms(p(_C_Q, _C_KC), bd512_ref[...], qg_ref[...] * (ATTN_SCALE * LOG2E)).astype(BF16)
    kc_ref[...] = p(_C_KC, _C_VC)
    vc_ref[...] = p(_C_VC, _C_KS)
    ks_ref[...] = _seg_rms(p(_C_KS, _C_KW), bd256_ref[...], ksg_ref[...]).astype(BF16)
    kw_ref[...] = _seg_rms(p(_C_KW, _C_GN), bd256_ref[...], kwg_ref[...]).astype(BF16)
    vt = _dot_nt(wvt_ref[...], hb)
    vst_ref[...] = vt[:LANES].astype(BF16)
    vwt_ref[...] = vt[LANES:].astype(BF16)
    gn_ref[...] = jax.nn.sigmoid(p(_C_GN, _C_U))
    u_ref[...] = p(_C_U, _C_GA)
    ga_ref[...] = jax.nn.sigmoid(p(_C_GA, _C_GS)).astype(BF16)
    gs_ref[...] = jax.nn.sigmoid(p(_C_GS, _C_END)).astype(BF16)


def _dup_cols(w):
    return jnp.concatenate([w[:, :64], w[:, :64], w[:, 64:], w[:, 64:]], axis=1)


def _block_ones(n):
    return jnp.kron(jnp.eye(n // HEAD_DIM, dtype=F32), jnp.ones((HEAD_DIM, HEAD_DIM), F32)).astype(BF16)


def _proj(x2, mod, g_norm1, w_in, q_gain, ks_gain, kw_gain):
    o = np.cumsum((0, 512, 128, 128, 128, 128, 128, 128, 24, 512, 1024, 1024))
    parts = [w_in[:, o[i]:o[i + 1]] for i in range(11)]
    wq, wkc, wvc, wks, wvs, wkw, wvw, wgn, wu, wga, wgs = parts
    w = jnp.concatenate([wq, wkc, wvc, _dup_cols(wks), _dup_cols(wkw),
                         jnp.pad(wgn, ((0, 0), (0, LANES - 24))), wu, wga, wgs], axis=1).astype(BF16)
    wvt = jnp.concatenate([wvs, wvw], axis=1).T.astype(BF16)
    tm = PROJ_TM
    row = lambda i: (i, 0)
    col = lambda i: (0, i)
    fix = lambda i: (0, 0)
    outs = [(512, BF16, row), (128, F32, row), (128, F32, row), (256, BF16, row), (256, BF16, row),
            (LANES, BF16, col), (LANES, BF16, col),
            (128, F32, row), (512, F32, row), (1024, BF16, row), (1024, BF16, row)]
    ospec = lambda wd, m: pl.BlockSpec((tm, wd), m) if m is row else pl.BlockSpec((wd, tm), m)
    oshape = lambda wd, dt, m: jax.ShapeDtypeStruct((N_TOK, wd) if m is row else (wd, N_TOK), dt)
    return pl.pallas_call(
        _proj_kernel,
        grid=(N_TOK // tm,),
        in_specs=[pl.BlockSpec((tm, D_MODEL), row),
                  pl.BlockSpec((1, 6, D_MODEL), lambda i: (i // (SEQ // tm), 0, 0)),
                  pl.BlockSpec((1, D_MODEL), fix),
                  pl.BlockSpec((D_MODEL, _C_END), fix),
                  pl.BlockSpec((2 * LANES, D_MODEL), fix),
                  pl.BlockSpec((1, 512), fix), pl.BlockSpec((1, 256), fix), pl.BlockSpec((1, 256), fix),
                  pl.BlockSpec((512, 512), fix), pl.BlockSpec((256, 256), fix)],
        out_specs=[ospec(wd, m) for wd, _, m in outs],
        out_shape=[oshape(wd, dt, m) for wd, dt, m in outs],
        compiler_params=_cparams("arbitrary"),
        name="proj",
    )(x2, mod, g_norm1.reshape(1, -1), w, wvt,
      jnp.tile(q_gain, N_HEADS).reshape(1, -1), jnp.tile(ks_gain, 4).reshape(1, -1),
      jnp.tile(kw_gain, 4).reshape(1, -1), _block_ones(512), _block_ones(256))


def _compress_kernel(r_ref, pe_ref, w1_ref, w2_ref, bd_ref, gain_ref, o_ref, *, do_norm):
    r = jnp.concatenate([r_ref[0, :, l, :] for l in range(CMP_STRIDE)], axis=1)
    p0 = _dot((r + pe_ref[0]).astype(BF16), w1_ref[0])
    p1 = _dot((r + pe_ref[1]).astype(BF16), w1_ref[1])
    hid = p0 + pltpu.roll(p1, N_CMP - 1, 0)
    c = _dot(_gelu(hid).astype(BF16), w2_ref[...])
    if do_norm:
        c = _seg_rms(c, bd_ref[...], gain_ref[...])
    o_ref[0] = c.astype(BF16)


def _compress(raw, pe, w1, w2, gain, do_norm):
    r = raw.reshape(BATCH, SEQ // CMP_STRIDE, CMP_STRIDE, LANES)
    eye = jnp.eye(N_KV, dtype=F32)
    w1r = w1.reshape(2, CMP_STRIDE, HEAD_DIM, CMP_HIDDEN)
    w1big = jnp.einsum('hldc,gk->hlgdkc', w1r, eye).reshape(2, CMP_STRIDE * LANES, N_KV * CMP_HIDDEN).astype(BF16)
    w2big = jnp.einsum('cd,gk->gckd', w2, eye)
    w2big = jnp.concatenate([w2big, w2big], axis=-1).reshape(N_KV * CMP_HIDDEN, 4 * HEAD_DIM).astype(BF16)
    pe_big = jnp.broadcast_to(pe.reshape(2, CMP_STRIDE, 1, HEAD_DIM), (2, CMP_STRIDE, N_KV, HEAD_DIM))
    pe_big = pe_big.reshape(2, 1, CMP_STRIDE * LANES)
    fix2 = lambda b: (0, 0)
    fix3 = lambda b: (0, 0, 0)
    return pl.pallas_call(
        functools.partial(_compress_kernel, do_norm=do_norm),
        grid=(BATCH,),
        in_specs=[pl.BlockSpec((1, N_CMP, CMP_STRIDE, LANES), lambda b: (b, 0, 0, 0)),
                  pl.BlockSpec((2, 1, CMP_STRIDE * LANES), fix3),
                  pl.BlockSpec((2, CMP_STRIDE * LANES, N_KV * CMP_HIDDEN), fix3),
                  pl.BlockSpec((N_KV * CMP_HIDDEN, 256), fix2),
                  pl.BlockSpec((256, 256), fix2), pl.BlockSpec((1, 256), fix2)],
        out_specs=pl.BlockSpec((1, N_CMP, 256), lambda b: (b, 0, 0)),
        out_shape=jax.ShapeDtypeStruct((BATCH, N_CMP, 256), BF16),
        compiler_params=_cparams("arbitrary"),
        name="compress_k" if do_norm else "compress_v",
    )(r, pe_big, w1big, w2big, _block_ones(256), jnp.tile(gain, 4).reshape(1, -1))


ATT_TQ = 256
RANK_CHUNK = 16


def _head_variants(qb):
    lane = lax.broadcasted_iota(I32, qb.shape, 1)
    z = jnp.zeros_like(qb)
    return jnp.where(lane < HEAD_DIM, qb, z), jnp.where(lane < HEAD_DIM, z, qb)


def _cmp_kernel(q_ref, kc_ref, vc_ref, ov_ref, o_ref, sel_ref, vrank_ref):
    tq = ATT_TQ
    qi = pl.program_id(1)
    tpos = qi * tq + lax.broadcasted_iota(I32, (tq, N_CMP), 0)
    nidx = lax.broadcasted_iota(I32, (tq, N_CMP), 1)
    mask = (CMP_STRIDE * nidx + (CMP_BLOCK - 1)) <= tpos
    lane_lo = lax.broadcasted_iota(I32, (tq, LANES), 1) < HEAD_DIM
    for g in range(N_KV):
        kd = kc_ref[0, :, g * LANES:(g + 1) * LANES]
        vd = vc_ref[0, :, g * LANES:(g + 1) * LANES]
        psum = jnp.zeros((tq, N_CMP), F32)
        for jb in range(2):
            blk = 2 * g + jb
            pv = []
            for qv in _head_variants(q_ref[:, blk * LANES:(blk + 1) * LANES]):
                s = jnp.where(mask, _dot_nt(qv, kd), NEG)
                m = jnp.max(s, axis=-1, keepdims=True)
                e = jnp.where(mask, jnp.exp2(s - m), 0.0)
                l = jnp.sum(e, axis=-1, keepdims=True)
                p = e / jnp.where(l > 0.0, l, 1.0)
                psum = psum + p
                pv.append(_dot(p.astype(BF16), vd))
            o_ref[:, blk * LANES:(blk + 1) * LANES] = jnp.where(lane_lo, pv[0], pv[1]).astype(BF16)
        imp = _split_dot(psum, ov_ref[...])
        imp_t = imp.T[:N_SEL_BLOCKS]
        j = lax.broadcasted_iota(I32, (N_SEL_BLOCKS, tq), 0)
        cur = jnp.right_shift(qi * tq + lax.broadcasted_iota(I32, (N_SEL_BLOCKS, tq), 1), 6)
        forced = (j == 0) | (j == cur) | (j == cur - 1)
        v = jnp.where(forced, jnp.inf, jnp.where(j <= cur, imp_t, -jnp.inf))
        vrank_ref[...] = jnp.zeros((N_SEL_BLOCKS, tq), F32)
        n_live = (qi + 1) * (tq // SEL_BLOCK)
        for c0 in range(0, N_SEL_BLOCKS, RANK_CHUNK):
            @pl.when(c0 < n_live)
            def _():
                rank = vrank_ref[...]
                for jp in range(c0, c0 + RANK_CHUNK):
                    row = v[jp:jp + 1, :]
                    tie = jnp.where(j > jp, 1.0, 0.0)
                    rank = rank + jnp.where(row > v, 1.0, jnp.where(row == v, tie, 0.0))
                vrank_ref[...] = rank
        rank = vrank_ref[...]
        sel_ref[g * N_SEL_BLOCKS:(g + 1) * N_SEL_BLOCKS, :] = jnp.where(rank < float(N_SELECT), 0.0, NEG)


def _cmp_attn(q, kcn, vcn):
    nc = np.arange(N_CMP)
    sb = np.arange(LANES)
    ov = ((CMP_STRIDE * nc[:, None] < SEL_BLOCK * sb[None, :] + SEL_BLOCK)
          & (CMP_STRIDE * nc[:, None] + CMP_BLOCK > SEL_BLOCK * sb[None, :])
          & (nc[:, None] < N_CMP - 1) & (sb[None, :] < N_SEL_BLOCKS))
    ov = jnp.asarray(ov, BF16)
    tq = ATT_TQ
    nq = SEQ // tq
    row = lambda b, i: (b * nq + i, 0)
    return pl.pallas_call(
        _cmp_kernel,
        grid=(BATCH, nq),
        in_specs=[pl.BlockSpec((tq, NSA_WIDTH), row),
                  pl.BlockSpec((1, N_CMP, 256), lambda b, i: (b, 0, 0)),
                  pl.BlockSpec((1, N_CMP, 256), lambda b, i: (b, 0, 0)),
                  pl.BlockSpec((N_CMP, LANES), lambda b, i: (0, 0))],
        out_specs=[pl.BlockSpec((tq, NSA_WIDTH), row),
                   pl.BlockSpec((N_KV * N_SEL_BLOCKS, tq), lambda b, i: (0, b * nq + i))],
        out_shape=[jax.ShapeDtypeStruct((N_TOK, NSA_WIDTH), BF16),
                   jax.ShapeDtypeStruct((N_KV * N_SEL_BLOCKS, N_TOK), F32)],
        scratch_shapes=[pltpu.VMEM((N_SEL_BLOCKS, tq), F32)],
        compiler_params=_cparams("arbitrary", "arbitrary"),
        name="cmp_attn",
    )(q, kcn, vcn, ov)


ATT_TK = 256


M_INIT = -1e29


SUM_ROWS = 16


def _selwin_kernel(q_ref, ks_ref, kw_ref, vst_ref, vwt_ref, selb_ref, osel_ref, owin_ref, m_ref, acc_ref):
    tq, tk = ATT_TQ, ATT_TK
    qi = pl.program_id(1)
    krow = lax.broadcasted_iota(I32, (tk, tq), 0)
    qcol = lax.broadcasted_iota(I32, (tk, tq), 1)
    causal_bias = jnp.where(krow <= qcol, 0.0, NEG)
    far_bias = jnp.where(qcol < krow, 0.0, NEG)

    ones_rows = jnp.ones((SUM_ROWS, tk), BF16)

    def reset():
        m_ref[...] = jnp.full(m_ref.shape, M_INIT, F32)
        acc_ref[...] = jnp.zeros(acc_ref.shape, F32)

    def update(g, k_ref, vt_ref, kt, bias):
        k0 = pl.multiple_of(kt * tk, tk)
        kd = k_ref[0, pl.ds(k0, tk), g * LANES:(g + 1) * LANES]
        vt = vt_ref[g * HEAD_DIM:(g + 1) * HEAD_DIM, pl.ds(k0, tk)]
        s = _dot_nt(kd, qvars[g])
        if bias is not None:
            s = s + jnp.concatenate([bias] * 4, axis=1)
        m_old = m_ref[g]
        m_new = jnp.maximum(m_old, jnp.max(s, axis=0, keepdims=True))
        alpha = jnp.exp2(m_old - m_new)
        p = jnp.exp2(s - m_new)
        m_ref[g] = m_new
        vte = jnp.concatenate([vt, ones_rows], axis=0)
        acc_ref[g] = alpha * acc_ref[g] + _dot(vte, p.astype(BF16))

    def finish(out_ref, g):
        o = acc_ref[g, :HEAD_DIM, :] / acc_ref[g, HEAD_DIM:HEAD_DIM + 1, :]
        for jb in range(2):
            blk = 2 * g + jb
            pair = jnp.concatenate([o[:, 2 * jb * tq:(2 * jb + 1) * tq], o[:, (2 * jb + 1) * tq:(2 * jb + 2) * tq]],
                                   axis=0)
            out_ref[:, blk * LANES:(blk + 1) * LANES] = pair.T.astype(BF16)

    def sel_bias(g, kt):
        rows = [jnp.broadcast_to(selb_ref[pl.ds(g * N_SEL_BLOCKS + kt * (tk // SEL_BLOCK) + r, 1), :],
                                 (SEL_BLOCK, tq)) for r in range(tk // SEL_BLOCK)]
        return jnp.concatenate(rows, axis=0)

    qvars = []
    for g in range(N_KV):
        heads = []
        for jb in range(2):
            heads.extend(_head_variants(q_ref[:, (2 * g + jb) * LANES:(2 * g + jb + 1) * LANES]))
        qvars.append(jnp.concatenate(heads, axis=0))
    groups = range(N_KV)

    reset()

    def sel_step(kt, carry):
        for g in groups:
            update(g, ks_ref, vst_ref, kt, sel_bias(g, kt))
        return carry

    lax.fori_loop(0, qi, sel_step, 0)
    for g in groups:
        update(g, ks_ref, vst_ref, qi, sel_bias(g, qi) + causal_bias)
    for g in groups:
        finish(osel_ref, g)

    reset()

    @pl.when(qi >= 2)
    def _():
        for g in groups:
            update(g, kw_ref, vwt_ref, qi - 2, far_bias)

    @pl.when(qi >= 1)
    def _():
        for g in groups:
            update(g, kw_ref, vwt_ref, qi - 1, None)

    for g in groups:
        update(g, kw_ref, vwt_ref, qi, causal_bias)
    for g in groups:
        finish(owin_ref, g)


def _selwin(q, ks, kw, vst, vwt, selb):
    tq = ATT_TQ
    nq = SEQ // tq
    assert WINDOW == 2 * ATT_TK and ATT_TQ == ATT_TK
    row = lambda b, i: (b * nq + i, 0)
    keys = pl.BlockSpec((1, SEQ, 256), lambda b, i: (b, 0, 0))
    vals = pl.BlockSpec((LANES, SEQ), lambda b, i: (0, b))
    r3 = lambda a: a.reshape(BATCH, SEQ, 256)
    return pl.pallas_call(
        _selwin_kernel,
        grid=(BATCH, nq),
        in_specs=[pl.BlockSpec((tq, NSA_WIDTH), row), keys, keys, vals, vals,
                  pl.BlockSpec((N_KV * N_SEL_BLOCKS, tq), lambda b, i: (0, b * nq + i))],
        out_specs=[pl.BlockSpec((tq, NSA_WIDTH), row)] * 2,
        out_shape=[jax.ShapeDtypeStruct((N_TOK, NSA_WIDTH), BF16)] * 2,
        scratch_shapes=[pltpu.VMEM((N_KV, 1, 4 * tq), F32),
                        pltpu.VMEM((N_KV, HEAD_DIM + SUM_ROWS, 4 * tq), F32)],
        compiler_params=_cparams("arbitrary", "arbitrary"),
        name="selwin",
    )(q, r3(ks), r3(kw), vst, vwt, selb)


def _s5_param_kernel(are_ref, aim_ref, ldt_ref, cre_ref, cim_ref, bre_ref, bim_ref,
                     clre_ref, clim_ref, wbre_ref, wbim_ref, bbre_ref, bbim_ref, ltre_ref, ltim_ref):
    are, aim = are_ref[...], aim_ref[...]
    dt = jnp.exp(ldt_ref[...])
    cre, cim = cre_ref[...], cim_ref[...]

    def lam_pow(tau):
        mag = jnp.exp(are * dt * float(tau))
        ang = aim * dt * float(tau)
        return mag * jnp.cos(ang), mag * jnp.sin(ang)

    lre, lim = lam_pow(1)
    den = are * are + aim * aim
    qre = ((lre - 1.0) * are + lim * aim) / den
    qim = (lim * are - (lre - 1.0) * aim) / den
    bre, bim = bre_ref[...], bim_ref[...]
    bbre = qre * bre - qim * bim
    bbim = qre * bim + qim * bre
    bbre_ref[...] = bbre
    bbim_ref[...] = bbim
    for tau in range(S5_T + 1):
        pr, pi = lam_pow(tau)
        clre_ref[tau] = cre * pr - cim * pi
        clim_ref[tau] = cre * pi + cim * pr
        if tau < S5_T:
            k = S5_T - 1 - tau
            wbre_ref[k] = pr * bbre - pi * bbim
            wbim_ref[k] = pr * bbim + pi * bbre
        else:
            ltre_ref[...] = pr
            ltim_ref[...] = pi


def _s5_kmat_kernel(l_ref, r_ref, o_ref):
    o_ref[0] = jnp.dot(l_ref[0], r_ref[0], preferred_element_type=F32, precision=HIGHEST)


def _s5_params(a_re, a_im, log_dt, b_re, b_im, c_re, c_im):
    T = S5_T
    pn = GROUP * STATE
    tile_p = lambda a: jnp.tile(a, (1, GROUP))
    args = (tile_p(a_re), tile_p(a_im), jnp.broadcast_to(log_dt[:, None], (N_GROUPS, pn)),
            c_re.reshape(N_GROUPS, pn), c_im.reshape(N_GROUPS, pn),
            jnp.swapaxes(b_re, 1, 2).reshape(N_GROUPS, pn), jnp.swapaxes(b_im, 1, 2).reshape(N_GROUPS, pn))
    full2 = pl.BlockSpec((N_GROUPS, pn), lambda: (0, 0))
    clre, clim, wbre, wbim, bbre, bbim, ltre, ltim = pl.pallas_call(
        _s5_param_kernel,
        in_specs=[full2] * 7,
        out_specs=[pl.BlockSpec((T + 1, N_GROUPS, pn), lambda: (0, 0, 0))] * 2
                  + [pl.BlockSpec((T, N_GROUPS, pn), lambda: (0, 0, 0))] * 2 + [full2] * 4,
        out_shape=[jax.ShapeDtypeStruct((T + 1, N_GROUPS, pn), F32)] * 2
                  + [jax.ShapeDtypeStruct((T, N_GROUPS, pn), F32)] * 2
                  + [jax.ShapeDtypeStruct((N_GROUPS, pn), F32)] * 4,
        name="s5_params",
    )(*args)

    r5 = lambda a, t: a[:t].reshape(t, N_GROUPS, GROUP, STATE)
    lhs = jnp.concatenate([r5(clre, T), -r5(clim, T)], axis=-1)
    lhs = jnp.transpose(lhs, (1, 0, 2, 3)).reshape(N_GROUPS, T * GROUP, 2 * STATE)
    bb = lambda a: jnp.swapaxes(a.reshape(N_GROUPS, GROUP, STATE), 1, 2)
    rhs = jnp.concatenate([bb(bbre), bb(bbim)], axis=1)
    kmat = pl.pallas_call(
        _s5_kmat_kernel,
        grid=(N_GROUPS,),
        in_specs=[pl.BlockSpec((1, T * GROUP, 2 * STATE), lambda g: (g, 0, 0)),
                  pl.BlockSpec((1, 2 * STATE, GROUP), lambda g: (g, 0, 0))],
        out_specs=pl.BlockSpec((1, T * GROUP, GROUP), lambda g: (g, 0, 0)),
        out_shape=jax.ShapeDtypeStruct((N_GROUPS, T * GROUP, GROUP), F32),
        compiler_params=_cparams("arbitrary"),
        name="s5_kmat",
    )(lhs, rhs)

    eye = jnp.eye(S5_GL, dtype=F32)
    kt = kmat.reshape(S5_SG, S5_GL, T, GROUP, GROUP)
    kbd = jnp.einsum('sgtpq,gh->stgqhp', kt, eye).reshape(S5_SG, T, LANES, LANES)
    krev = kbd[:, ::-1].reshape(S5_SG, T * LANES, LANES).astype(BF16)
    r6 = lambda a: a.reshape(T, S5_SG, S5_GL, GROUP, STATE)
    wb = jnp.stack([r6(wbre), r6(wbim)], axis=-2)
    wb = jnp.einsum('ksgpin,gh->skgpihn', wb, eye).reshape(S5_SG, T * LANES, S5_NSTATE).astype(BF16)
    wc = jnp.stack([r6(clre[1:]), -r6(clim[1:])], axis=-2)
    wc = jnp.einsum('tsgpin,gh->signthp', wc, eye).reshape(S5_SG, S5_NSTATE, T * LANES).astype(BF16)
    lt = lambda a: a.reshape(N_GROUPS, GROUP, STATE)[:, 0].reshape(S5_SG, 1, S5_GL * STATE)
    return krev, wb, wc, lt(ltre), lt(ltim)


S5_TC = 256


def _s5_lane_block(sg):
    return pl.ds(pl.multiple_of(sg * LANES, LANES), LANES)


def _s5_chunk_inputs(x_ref, sg):
    return jnp.concatenate([x_ref[:, t, _s5_lane_block(sg)] for t in range(S5_T)], axis=1).astype(BF16)


def _s5_state_kernel(x_ref, wb_ref, e_ref):
    e_ref[0] = _dot(_s5_chunk_inputs(x_ref, pl.program_id(1)), wb_ref[0])


def _s5_scan_kernel(e_ref, ltre_ref, ltim_ref, xs_ref):
    lr, li = ltre_ref[0], ltim_ref[0]
    half = S5_NSTATE // 2

    def step(c, carry):
        new = []
        for b in range(BATCH):
            xr, xi = carry[b]
            row = b * S5_CH_PER_BATCH + c
            xs_ref[0, pl.ds(row, 1), :half] = xr
            xs_ref[0, pl.ds(row, 1), half:] = xi
            e = e_ref[0, pl.ds(row, 1), :]
            new.append((lr * xr - li * xi + e[:, :half], lr * xi + li * xr + e[:, half:]))
        return tuple(new)

    zero = jnp.zeros((1, half), F32)
    lax.fori_loop(0, S5_CH_PER_BATCH, step, tuple((zero, zero) for _ in range(BATCH)))


def _s5_out_kernel(x_ref, xs_ref, krev_ref, wc_ref, y_ref):
    sg = pl.program_id(1)
    x = _s5_chunk_inputs(x_ref, sg)
    xsb = xs_ref[0].astype(BF16)
    for t in range(S5_T):
        cols = slice(t * LANES, (t + 1) * LANES)
        y_ref[:, t, _s5_lane_block(sg)] = (_dot(x[:, :(t + 1) * LANES], krev_ref[0, (S5_T - 1 - t) * LANES:, :])
                                           + _dot(xsb, wc_ref[0, :, cols]))


def _s5(u, krev, wb, wc, ltre, ltim):
    T, tc = S5_T, S5_TC
    xn = u.reshape(S5_CH, T, SSM_WIDTH)
    grid = (S5_CH // tc, S5_SG)
    natural = pl.BlockSpec((tc, T, SSM_WIDTH), lambda i, s: (i, 0, 0))
    rows = lambda i, s: (s, i, 0)
    per_sg = lambda i, s: (s, 0, 0)
    e = pl.pallas_call(
        _s5_state_kernel, grid=grid,
        in_specs=[natural, pl.BlockSpec((1, T * LANES, S5_NSTATE), per_sg)],
        out_specs=pl.BlockSpec((1, tc, S5_NSTATE), rows),
        out_shape=jax.ShapeDtypeStruct((S5_SG, S5_CH, S5_NSTATE), F32),
        compiler_params=_cparams("arbitrary", "arbitrary"), name="s5_state",
    )(xn, wb)
    sg1 = lambda s: (s, 0, 0)
    xstart = pl.pallas_call(
        _s5_scan_kernel, grid=(S5_SG,),
        in_specs=[pl.BlockSpec((1, S5_CH, S5_NSTATE), sg1),
                  pl.BlockSpec((1, 1, S5_NSTATE // 2), sg1), pl.BlockSpec((1, 1, S5_NSTATE // 2), sg1)],
        out_specs=pl.BlockSpec((1, S5_CH, S5_NSTATE), sg1),
        out_shape=jax.ShapeDtypeStruct((S5_SG, S5_CH, S5_NSTATE), F32),
        compiler_params=_cparams("arbitrary"), name="s5_scan",
    )(e, ltre, ltim)
    y = pl.pallas_call(
        _s5_out_kernel, grid=grid,
        in_specs=[natural, pl.BlockSpec((1, tc, S5_NSTATE), rows),
                  pl.BlockSpec((1, T * LANES, LANES), per_sg), pl.BlockSpec((1, S5_NSTATE, T * LANES), per_sg)],
        out_specs=natural,
        out_shape=jax.ShapeDtypeStruct((S5_CH, T, SSM_WIDTH), F32),
        compiler_params=_cparams("arbitrary", "arbitrary", vmem=56 * 1024 * 1024), name="s5_out",
    )(xn, xstart, krev, wc)
    return y.reshape(N_TOK, SSM_WIDTH)


MERGE_TM = 256


def _merge_kernel(ocmp_ref, osel_ref, owin_ref, gn_ref, yssm_ref, u_ref, ga_ref, gs_ref, x_ref, mod_ref,
                  eg_ref, dskip_ref, wglu_ref, bglu_ref, wua_ref, wus_ref, wout_ref, g2_ref,
                  wrhi_ref, wrlo_ref, wsgu_ref, wsd_ref,
                  xpart_ref, h2_ref, logit_ref):
    mod = mod_ref[0]
    gnb = gn_ref[...].astype(BF16)
    o_nsa = (_dot(gnb, eg_ref[0]) * ocmp_ref[...].astype(F32)
             + _dot(gnb, eg_ref[1]) * osel_ref[...].astype(F32)
             + _dot(gnb, eg_ref[2]) * owin_ref[...].astype(F32))
    attn = _dot(o_nsa.astype(BF16), wua_ref[...])
    z = _gelu(yssm_ref[...] + dskip_ref[...] * u_ref[...])
    y_ssm = z * jax.nn.sigmoid(_dot(z.astype(BF16), wglu_ref[...]) + bglu_ref[...])
    ssm = _dot(y_ssm.astype(BF16), wus_ref[...])
    merged = ga_ref[...].astype(F32) * attn + gs_ref[...].astype(F32) * ssm
    x1 = x_ref[...] + mod[2:3] * _dot(merged.astype(BF16), wout_ref[...])

    ms = jnp.mean(x1 * x1, axis=-1, keepdims=True)
    h2 = (x1 * lax.rsqrt(ms + EPS) * g2_ref[...]) * (1.0 + mod[4:5]) + mod[3:4]
    hi = h2.astype(BF16)
    lo = (h2 - hi.astype(F32)).astype(BF16)
    h2_ref[...] = _pack_bf16_pairs(h2)
    logit_ref[...] = _dot_nt(wrhi_ref[...], hi) + _dot_nt(wrhi_ref[...], lo) + _dot_nt(wrlo_ref[...], hi)
    gu = _dot(hi, wsgu_ref[...])
    shared = _dot((_silu(gu[:, :D_EXPERT]) * gu[:, D_EXPERT:]).astype(BF16), wsd_ref[...])
    xpart_ref[...] = x1 + mod[5:6] * shared


def _merge(ocmp, osel, owin, gn, yssm, u, ga, gs, x2, mod, d_skip, w_glu, b_glu, w_up_attn, w_up_ssm, w_out,
           g_norm2, w_router, ws_gate, ws_up, ws_down):
    tm = MERGE_TM
    eg = np.zeros((3, LANES, NSA_WIDTH), np.float32)
    for j in range(3):
        for h in range(N_HEADS):
            eg[j, 3 * h + j, h * HEAD_DIM:(h + 1) * HEAD_DIM] = 1.0
    wr_t = w_router.T
    wr_hi = wr_t.astype(BF16)
    wr_lo = (wr_t - wr_hi.astype(F32)).astype(BF16)
    row = lambda i: (i, 0)
    fix2 = lambda i: (0, 0)
    wspec = lambda a: pl.BlockSpec(a.shape, (lambda i: (0,) * a.ndim))
    weights = [jnp.asarray(eg, BF16), d_skip.reshape(1, -1), w_glu.astype(BF16), b_glu.reshape(1, -1),
               w_up_attn.astype(BF16), w_up_ssm.astype(BF16), w_out.astype(BF16), g_norm2.reshape(1, -1),
               wr_hi, wr_lo, jnp.concatenate([ws_gate, ws_up], axis=1).astype(BF16), ws_down.astype(BF16)]
    acts = [(ocmp, 512), (osel, 512), (owin, 512), (gn, 128), (yssm, 512), (u, 512), (ga, 1024), (gs, 1024),
            (x2, 1024)]
    return pl.pallas_call(
        _merge_kernel,
        grid=(N_TOK // tm,),
        in_specs=[pl.BlockSpec((tm, wd), row) for _, wd in acts]
                 + [pl.BlockSpec((1, 6, D_MODEL), lambda i: (i // (SEQ // tm), 0, 0))]
                 + [wspec(w) for w in weights],
        out_specs=[pl.BlockSpec((tm, D_MODEL), row), pl.BlockSpec((tm, HALF), row),
                   pl.BlockSpec((N_EXPERTS, tm), lambda i: (0, i))],
        out_shape=[jax.ShapeDtypeStruct((N_TOK, D_MODEL), F32), jax.ShapeDtypeStruct((N_TOK, HALF), I32),
                   jax.ShapeDtypeStruct((N_EXPERTS, N_TOK), F32)],
        compiler_params=_cparams("arbitrary"),
        name="merge",
    )(*[a for a, _ in acts], mod, *weights)


ROUTE_TN = 512


def _route_kernel(logit_ref, bias_ref, eidx_ref, w_ref, count_ref, gscore_ref, masked_ref):
    tn = ROUTE_TN

    @pl.when(pl.program_id(0) == 0)
    def _():
        count_ref[...] = jnp.zeros(count_ref.shape, F32)

    sc = jax.nn.sigmoid(logit_ref[...])
    biased = sc + bias_ref[...]
    gi = lax.broadcasted_iota(I32, (EXPERTS_PER_GROUP, tn), 0).astype(F32)
    for g in range(N_EXPERT_GROUPS):
        blk = biased[g * EXPERTS_PER_GROUP:(g + 1) * EXPERTS_PER_GROUP]
        m1 = jnp.max(blk, axis=0, keepdims=True)
        i1 = jnp.min(jnp.where(blk == m1, gi, float(EXPERTS_PER_GROUP)), axis=0, keepdims=True)
        m2 = jnp.max(jnp.where(gi == i1, -jnp.inf, blk), axis=0, keepdims=True)
        gscore_ref[g:g + 1, :] = m1 + m2
    gs = gscore_ref[...]
    gidx = lax.broadcasted_iota(I32, (N_EXPERT_GROUPS, tn), 0)
    grank = jnp.zeros((N_EXPERT_GROUPS, tn), F32)
    for gp in range(N_EXPERT_GROUPS):
        row = gs[gp:gp + 1, :]
        tie = jnp.where(gidx > gp, 1.0, 0.0)
        grank = grank + jnp.where(row > gs, 1.0, jnp.where(row == gs, tie, 0.0))
    for g in range(N_EXPERT_GROUPS):
        keep = grank[g:g + 1, :] < float(TOPK_GROUPS)
        sl = slice(g * EXPERTS_PER_GROUP, (g + 1) * EXPERTS_PER_GROUP)
        masked_ref[sl, :] = jnp.where(keep, biased[sl], -jnp.inf)
    cur = masked_ref[...]
    eidx = lax.broadcasted_iota(I32, (N_EXPERTS, tn), 0).astype(F32)
    wsum = jnp.zeros((1, tn), F32)
    hits = jnp.zeros((N_EXPERTS, tn), F32)
    for k in range(TOP_K):
        m = jnp.max(cur, axis=0, keepdims=True)
        idx = jnp.min(jnp.where(cur == m, eidx, float(N_EXPERTS)), axis=0, keepdims=True)
        hit = eidx == idx
        wk = jnp.sum(jnp.where(hit, sc, 0.0), axis=0, keepdims=True)
        cur = jnp.where(hit, -jnp.inf, cur)
        hits = hits + jnp.where(hit, 1.0, 0.0)
        eidx_ref[k:k + 1, :] = idx.astype(I32)
        w_ref[k:k + 1, :] = wk
        wsum = wsum + wk
    w_ref[...] = w_ref[...] / wsum * ROUTE_SCALE
    count_ref[...] = count_ref[...] + jnp.sum(hits, axis=1, keepdims=True)


def _route(logits_t, router_bias):
    tn = ROUTE_TN
    return pl.pallas_call(
        _route_kernel,
        grid=(N_TOK // tn,),
        in_specs=[pl.BlockSpec((N_EXPERTS, tn), lambda i: (0, i)), pl.BlockSpec((N_EXPERTS, 1), lambda i: (0, 0))],
        out_specs=[pl.BlockSpec((TOP_K, tn), lambda i: (0, i))] * 2 + [pl.BlockSpec((N_EXPERTS, 1), lambda i: (0, 0))],
        out_shape=[jax.ShapeDtypeStruct((TOP_K, N_TOK), I32), jax.ShapeDtypeStruct((TOP_K, N_TOK), F32),
                   jax.ShapeDtypeStruct((N_EXPERTS, 1), F32)],
        scratch_shapes=[pltpu.VMEM((N_EXPERT_GROUPS, tn), F32), pltpu.VMEM((N_EXPERTS, tn), F32)],
        compiler_params=_cparams("arbitrary"),
        name="route",
    )(logits_t, router_bias.reshape(-1, 1))


N_MOE_BLK = NK // DISPATCH_BLOCK
N_ITEMS = N_MOE_BLK + N_EXPERTS
ASSIGN_BITS = 17


def _dispatch_plan(eidx, counts):
    e_flat = eidx.reshape(-1)
    key = jnp.sort(e_flat * NK + jnp.arange(NK, dtype=I32))
    order = key & (NK - 1)
    counts = counts.reshape(-1).astype(I32)
    start = jnp.cumsum(counts) - counts
    cuts = jnp.sort(jnp.concatenate([jnp.arange(N_MOE_BLK, dtype=I32) * DISPATCH_BLOCK, start]))
    lo = cuts
    hi = jnp.concatenate([cuts[1:], jnp.full((1,), NK, I32)])
    blk = jnp.minimum(lo // DISPATCH_BLOCK, N_MOE_BLK - 1)
    expert = jnp.clip(jnp.sum((start[None, :] <= lo[:, None]).astype(I32), axis=1) - 1, 0, N_EXPERTS - 1)
    one = jnp.ones((1,), I32)
    first = jnp.concatenate([one, (blk[1:] != blk[:-1]).astype(I32)])
    last = jnp.concatenate([(blk[1:] != blk[:-1]).astype(I32), one])
    new_expert = jnp.concatenate([one, (expert[1:] != expert[:-1]).astype(I32)])
    run_id = jnp.cumsum(new_expert) - 1
    n_runs = run_id[-1] + 1
    item = jnp.arange(N_ITEMS, dtype=I32)
    run_first_item = jnp.sort(jnp.where(new_expert == 1, item, N_ITEMS))
    run_expert = expert[jnp.minimum(run_first_item, N_ITEMS - 1)]
    ahead = run_id + (WEIGHT_RING - 1)
    ahead_expert = run_expert[jnp.minimum(ahead, N_ITEMS - 1)]
    ahead_valid = (ahead < n_runs).astype(I32)
    second_expert = run_expert[1:2]
    prologue = jnp.concatenate([second_expert, (n_runs > 1).astype(I32).reshape(1)])
    tok = jnp.right_shift(order, 3)
    home = (order & (TOP_K - 1)) * N_TOK + tok
    return tok, home, (blk, expert, lo - blk * DISPATCH_BLOCK, hi - blk * DISPATCH_BLOCK, first, last, new_expert,
                      run_id % WEIGHT_RING, ahead_expert, ahead_valid, prologue)


SC_CORES = 2
SC_SUBCORES = 16
SC_CHUNK = 128


def _sc_move_rows(table, idx, scatter):
    n = idx.shape[0]
    workers = SC_CORES * SC_SUBCORES
    per_worker = n // workers
    n_chunks = per_worker // SC_CHUNK
    assert per_worker * workers == n and n_chunks * SC_CHUNK == per_worker
    mesh = plsc.VectorSubcoreMesh(core_axis_name="c", subcore_axis_name="s",
                                  num_cores=SC_CORES, num_subcores=SC_SUBCORES)

    def body(table_hbm, idx_hbm, out_hbm, idx_v, rows_v, sem):
        wid = lax.axis_index("s") * SC_CORES + lax.axis_index("c")
        base = wid * per_worker

        @pl.loop(0, n_chunks)
        def _(j):
            off = base + j * SC_CHUNK
            pltpu.sync_copy(idx_hbm.at[pl.ds(off, SC_CHUNK)], idx_v)
            if scatter:
                pltpu.sync_copy(table_hbm.at[pl.ds(off, SC_CHUNK)], rows_v)
                pltpu.async_copy(rows_v, out_hbm.at[idx_v], sem).wait()
            else:
                pltpu.async_copy(table_hbm.at[idx_v], rows_v, sem).wait()
                pltpu.sync_copy(rows_v, out_hbm.at[pl.ds(off, SC_CHUNK)])

    return pl.kernel(
        body,
        out_type=jax.ShapeDtypeStruct((n, table.shape[1]), table.dtype),
        mesh=mesh,
        scratch_types=[pltpu.VMEM((SC_CHUNK,), I32), pltpu.VMEM((SC_CHUNK, table.shape[1]), table.dtype),
                       pltpu.SemaphoreType.DMA],
        name="sc_scatter_rows" if scatter else "sc_gather_rows",
    )(table, idx)


WEIGHT_RING = 3
WEIGHT_CHUNKS = 4


def _expert_weight_copies(w_hbm, wbuf, sem, expert, slot):
    rows = w_hbm.shape[1] // WEIGHT_CHUNKS
    return [pltpu.make_async_copy(w_hbm.at[expert, pl.ds(c * rows, rows)],
                                  wbuf.at[slot, pl.ds(c * rows, rows)], sem.at[slot])
            for c in range(WEIGHT_CHUNKS)]


def _moe_kernel(blk_ref, exp_ref, lo_ref, hi_ref, first_ref, last_ref, newexp_ref,
                slot_ref, ahead_exp_ref, ahead_ok_ref, prologue_ref,
                x_ref, wg_hbm, wu_hbm, wd_hbm, y_ref,
                acc_ref, wgf_ref, wuf_ref, wdf_ref, wgb_ref, wub_ref, wdb_ref, wsem):
    it = pl.program_id(0)
    lo, hi = lo_ref[it], hi_ref[it]
    streams = ((wg_hbm, wgf_ref), (wu_hbm, wuf_ref), (wd_hbm, wdf_ref))

    def request(expert, slot):
        for w_hbm, wbuf in streams:
            for cp in _expert_weight_copies(w_hbm, wbuf, wsem, expert, slot):
                cp.start()

    @pl.when(it == 0)
    def _():
        request(exp_ref[0], 0)

        @pl.when(prologue_ref[1] == 1)
        def _():
            request(prologue_ref[0], 1)

    @pl.when(newexp_ref[it] == 1)
    def _():
        slot = slot_ref[it]
        for w_hbm, wbuf in streams:
            for cp in _expert_weight_copies(w_hbm, wbuf, wsem, 0, slot):
                cp.wait()
        wgb_ref[...] = wgf_ref[slot].astype(BF16)
        wub_ref[...] = wuf_ref[slot].astype(BF16)
        wdb_ref[...] = wdf_ref[slot].astype(BF16)

        @pl.when(ahead_ok_ref[it] == 1)
        def _():
            ahead_slot = slot + (WEIGHT_RING - 1)
            request(ahead_exp_ref[it], jnp.where(ahead_slot >= WEIGHT_RING, ahead_slot - WEIGHT_RING, ahead_slot))

    @pl.when(first_ref[it] == 1)
    def _():
        acc_ref[...] = jnp.zeros(acc_ref.shape, F32)

    def expert_pass(r0, nrows):
        rows = slice(r0, r0 + nrows)
        ridx = r0 + lax.broadcasted_iota(I32, (nrows, HALF), 0)
        mine = (ridx >= lo) & (ridx < hi)
        xlo, xhi = _unpack_bf16_pairs(jnp.where(mine, x_ref[rows, :], 0))
        xlo, xhi = xlo.astype(BF16), xhi.astype(BF16)
        gate = _dot(xlo, wgb_ref[:HALF]) + _dot(xhi, wgb_ref[HALF:])
        up = _dot(xlo, wub_ref[:HALF]) + _dot(xhi, wub_ref[HALF:])
        acc_ref[rows, :] = acc_ref[rows, :] + _dot((_silu(gate) * up).astype(BF16), wdb_ref[...])

    mid = DISPATCH_BLOCK // 2
    pl.when((lo < mid) & (hi > mid))(lambda: expert_pass(0, DISPATCH_BLOCK))
    pl.when((hi > lo) & (hi <= mid))(lambda: expert_pass(0, mid))
    pl.when((hi > lo) & (lo >= mid))(lambda: expert_pass(mid, mid))

    @pl.when(last_ref[it] == 1)
    def _():
        y_ref[...] = _pack_bf16_pairs(acc_ref[...])


def _moe(xs, items, w_gate, w_up, w_down):
    by_blk = lambda it, blk, *_: (blk[it], 0)
    any_space = pl.BlockSpec(memory_space=pl.ANY)
    grid_spec = pltpu.PrefetchScalarGridSpec(
        num_scalar_prefetch=len(items),
        grid=(N_ITEMS,),
        in_specs=[pl.BlockSpec((DISPATCH_BLOCK, HALF), by_blk), any_space, any_space, any_space],
        out_specs=pl.BlockSpec((DISPATCH_BLOCK, HALF), by_blk),
        scratch_shapes=[pltpu.VMEM((DISPATCH_BLOCK, D_MODEL), F32),
                        pltpu.VMEM((WEIGHT_RING, D_MODEL, D_EXPERT), F32),
                        pltpu.VMEM((WEIGHT_RING, D_MODEL, D_EXPERT), F32),
                        pltpu.VMEM((WEIGHT_RING, D_EXPERT, D_MODEL), F32),
                        pltpu.VMEM((D_MODEL, D_EXPERT), BF16), pltpu.VMEM((D_MODEL, D_EXPERT), BF16),
                        pltpu.VMEM((D_EXPERT, D_MODEL), BF16),
                        pltpu.SemaphoreType.DMA((WEIGHT_RING,))],
    )
    return pl.pallas_call(
        _moe_kernel,
        grid_spec=grid_spec,
        out_shape=jax.ShapeDtypeStruct((NK, HALF), I32),
        compiler_params=_cparams("arbitrary"),
        name="moe",
    )(*items, xs, w_gate, w_up, w_down)


COMB_TC = 256


def _combine_kernel(slots_ref, w_ref, xpart_ref, mod_ref, out_ref):
    w = w_ref[...]
    lo = jnp.zeros((w.shape[0], HALF), F32)
    hi = jnp.zeros((w.shape[0], HALF), F32)
    for k in range(TOP_K):
        klo, khi = _unpack_bf16_pairs(slots_ref[k])
        lo = lo + w[:, k:k + 1] * klo
        hi = hi + w[:, k:k + 1] * khi
    gate2 = mod_ref[0][5:6]
    out_ref[:, :HALF] = xpart_ref[:, :HALF] + gate2[:, :HALF] * lo
    out_ref[:, HALF:] = xpart_ref[:, HALF:] + gate2[:, HALF:] * hi


def _combine(xpart, mod, slots, w):
    tc = COMB_TC
    row = lambda i: (i, 0)
    return pl.pallas_call(
        _combine_kernel,
        grid=(N_TOK // tc,),
        in_specs=[pl.BlockSpec((TOP_K, tc, HALF), lambda i: (0, i, 0)),
                  pl.BlockSpec((tc, TOP_K), row),
                  pl.BlockSpec((tc, D_MODEL), row),
                  pl.BlockSpec((1, 6, D_MODEL), lambda i: (i // (SEQ // tc), 0, 0))],
        out_specs=pl.BlockSpec((tc, D_MODEL), row),
        out_shape=jax.ShapeDtypeStruct((N_TOK, D_MODEL), F32),
        compiler_params=_cparams("arbitrary"),
        name="combine",
    )(slots.reshape(TOP_K, N_TOK, HALF), w, xpart, mod)


def _layer(x, c, w_ada, b_ada, g_norm1, g_norm2, w_in, q_gain, kc_gain, ks_gain, kw_gain,
           pe_k, pe_v, w_cmp_k1, w_cmp_k2, w_cmp_v1, w_cmp_v2,
           a_re, a_im, log_dt, b_re, b_im, c_re, c_im, d_skip, w_glu, b_glu,
           w_up_attn, w_up_ssm, w_out, w_router, router_bias,
           w_gate, w_up, w_down, ws_gate, ws_up, ws_down):
    x2 = x.reshape(N_TOK, D_MODEL)
    mod = _ada(c, w_ada, b_ada)
    q, kc_raw, vc_raw, ks, kw, vst, vwt, gn, u, ga, gs = _proj(x2, mod, g_norm1, w_in, q_gain, ks_gain, kw_gain)
    kcn = _compress(kc_raw, pe_k, w_cmp_k1, w_cmp_k2, kc_gain, True)
    vcn = _compress(vc_raw, pe_v, w_cmp_v1, w_cmp_v2, kc_gain, False)
    ocmp, selb = _cmp_attn(q, kcn, vcn)
    osel, owin = _selwin(q, ks, kw, vst, vwt, selb)
    yssm = _s5(u, *_s5_params(a_re, a_im, log_dt, b_re, b_im, c_re, c_im))
    xpart, h2, logits_t = _merge(ocmp, osel, owin, gn, yssm, u, ga, gs, x2, mod, d_skip, w_glu, b_glu,
                                  w_up_attn, w_up_ssm, w_out, g_norm2, w_router, ws_gate, ws_up, ws_down)
    eidx_t, w_t, counts = _route(logits_t, router_bias)
    tok, home, items = _dispatch_plan(eidx_t.T, counts)
    y = _moe(_sc_move_rows(h2, tok, scatter=False), items, w_gate, w_up, w_down)
    slots = _sc_move_rows(y, home, scatter=True)
    return _combine(xpart, mod, slots, w_t.T).reshape(BATCH, SEQ, D_MODEL)


def kernel(x, c, w_ada, b_ada, g_norm1, g_norm2, w_in, q_gain, kc_gain, ks_gain, kw_gain, pe_k, pe_v, w_cmp_k1,
           w_cmp_k2, w_cmp_v1, w_cmp_v2, a_re, a_im, log_dt, b_re, b_im, c_re, c_im, d_skip, w_glu, b_glu,
           w_up_attn, w_up_ssm, w_out, w_router, router_bias, w_gate, w_up, w_down, ws_gate, ws_up, ws_down):
    params = (w_ada, b_ada, g_norm1, g_norm2, w_in, q_gain, kc_gain, ks_gain, kw_gain, pe_k, pe_v, w_cmp_k1,
              w_cmp_k2, w_cmp_v1, w_cmp_v2, a_re, a_im, log_dt, b_re, b_im, c_re, c_im, d_skip, w_glu, b_glu,
              w_up_attn, w_up_ssm, w_out, w_router, router_bias, w_gate, w_up, w_down, ws_gate, ws_up, ws_down)
    depth = w_ada.shape[0]
    for layer in range(depth):
        x = _layer(x, c, *[p[layer] for p in params])
    return x
```

```python
import functools
import math

import jax
import jax.numpy as jnp
import numpy as np
from jax import lax
from jax.experimental import pallas as pl
from jax.experimental.pallas import tpu as pltpu
from jax.experimental.pallas import tpu_sc as plsc

F32 = jnp.float32
BF16 = jnp.bfloat16
I32 = jnp.int32
HIGHEST = lax.Precision.HIGHEST

D_MODEL = 1024
BATCH = 4
SEQ = 4096
N_TOK = BATCH * SEQ
N_HEADS = 8
HEAD_DIM = 64
N_KV = 2
CMP_BLOCK = 32
CMP_STRIDE = 16
CMP_HIDDEN = 256
N_CMP = 256
SEL_BLOCK = 64
N_SEL_BLOCKS = SEQ // SEL_BLOCK
N_SELECT = 16
WINDOW = 512
ATTN_SCALE = HEAD_DIM ** -0.5
LOG2E = 1.4426950408889634
NSA_WIDTH = N_HEADS * HEAD_DIM
SSM_WIDTH = 512
GROUP = 16
N_GROUPS = SSM_WIDTH // GROUP
STATE = 64
N_EXPERTS = 256
TOP_K = 8
D_EXPERT = 256
N_EXPERT_GROUPS = 8
EXPERTS_PER_GROUP = N_EXPERTS // N_EXPERT_GROUPS
TOPK_GROUPS = 4
ROUTE_SCALE = 2.5
DISPATCH_BLOCK = 512
EPS = 1e-6
NEG = -1e30

LANES = 128
S5_T = 16
S5_SG = 4
S5_GL = N_GROUPS // S5_SG
S5_CH = N_TOK // S5_T
S5_CH_PER_BATCH = SEQ // S5_T
S5_NSTATE = S5_GL * STATE * 2

NK = N_TOK * TOP_K
HALF = D_MODEL // 2

VMEM_LIMIT = 48 * 1024 * 1024


def _cparams(*sem, vmem=VMEM_LIMIT):
    return pltpu.CompilerParams(dimension_semantics=tuple(sem), vmem_limit_bytes=vmem)


def _dot(a, b):
    return jnp.dot(a, b, preferred_element_type=F32)


def _dot_nt(a, b):
    return lax.dot_general(a, b, (((1,), (1,)), ((), ())), preferred_element_type=F32)


def _split_dot(v, w):
    hi = v.astype(BF16)
    lo = (v - hi.astype(F32)).astype(BF16)
    return _dot(hi, w) + _dot(lo, w)


def _seg_rms(v, bd, gain):
    ss = _split_dot(v * v, bd)
    return v * lax.rsqrt(ss * (1.0 / HEAD_DIM) + EPS) * gain


def _gelu(x):
    return 0.5 * x * (1.0 + jnp.tanh(0.7978845608028654 * (x + 0.044715 * (x * x * x))))


def _silu(x):
    return x * jax.nn.sigmoid(x)


def _pack_bf16_pairs(v):
    h = v.shape[1] // 2
    return pltpu.pack_elementwise([v[:, :h], v[:, h:]], packed_dtype=BF16)


def _unpack_bf16_pairs(word):
    return (pltpu.unpack_elementwise(word, index=0, packed_dtype=BF16, unpacked_dtype=F32),
            pltpu.unpack_elementwise(word, index=1, packed_dtype=BF16, unpacked_dtype=F32))


def _ada_kernel(c_ref, w_ref, b_ref, o_ref):
    c = c_ref[...]
    o_ref[...] = jnp.dot(_silu(c), w_ref[...], preferred_element_type=F32, precision=HIGHEST) + b_ref[...]


def _ada(c, w_ada, b_ada):
    cp = jnp.pad(c, ((0, 8 - BATCH), (0, 0)))
    tn = 1536
    out = pl.pallas_call(
        _ada_kernel,
        grid=(6 * D_MODEL // tn,),
        in_specs=[pl.BlockSpec((8, D_MODEL), lambda j: (0, 0)),
                  pl.BlockSpec((D_MODEL, tn), lambda j: (0, j)),
                  pl.BlockSpec((1, tn), lambda j: (0, j))],
        out_specs=pl.BlockSpec((8, tn), lambda j: (0, j)),
        out_shape=jax.ShapeDtypeStruct((8, 6 * D_MODEL), F32),
        compiler_params=_cparams("arbitrary"),
        name="ada",
    )(cp, w_ada, b_ada.reshape(1, -1))
    return out.reshape(8, 6, D_MODEL)


_C_Q = 0
_C_KC = 512
_C_VC = 640
_C_KS = 768
_C_KW = 1024
_C_GN = 1280
_C_U = 1408
_C_GA = 1920
_C_GS = 2944
_C_END = 3968
PROJ_TM = 512


def _proj_kernel(x_ref, mod_ref, g1_ref, w_ref, wvt_ref, qg_ref, ksg_ref, kwg_ref, bd512_ref, bd256_ref,
                 q_ref, kc_ref, vc_ref, ks_ref, kw_ref, vst_ref, vwt_ref, gn_ref, u_ref, ga_ref, gs_ref):
    x = x_ref[...]
    ms = jnp.mean(x * x, axis=-1, keepdims=True)
    mod = mod_ref[0]
    h = (x * lax.rsqrt(ms + EPS) * g1_ref[...]) * (1.0 + mod[1:2]) + mod[0:1]
    hb = h.astype(BF16)

    def p(lo, hi):
        return _dot(hb, w_ref[:, lo:hi])

    q_ref[...] = _seg_rms(p(_C_Q, _C_KC), bd512_ref[...], qg_ref[...] * (ATTN_SCALE * LOG2E)).astype(BF16)
    kc_ref[...] = p(_C_KC, _C_VC)
    vc_ref[...] = p(_C_VC, _C_KS)
    ks_ref[...] = _seg_rms(p(_C_KS, _C_KW), bd256_ref[...], ksg_ref[...]).astype(BF16)
    kw_ref[...] = _seg_rms(p(_C_KW, _C_GN), bd256_ref[...], kwg_ref[...]).astype(BF16)
    vt = _dot_nt(wvt_ref[...], hb)
    vst_ref[...] = vt[:LANES].astype(BF16)
    vwt_ref[...] = vt[LANES:].astype(BF16)
    gn_ref[...] = jax.nn.sigmoid(p(_C_GN, _C_U))
    u_ref[...] = p(_C_U, _C_GA)
    ga_ref[...] = jax.nn.sigmoid(p(_C_GA, _C_GS)).astype(BF16)
    gs_ref[...] = jax.nn.sigmoid(p(_C_GS, _C_END)).astype(BF16)


def _dup_cols(w):
    return jnp.concatenate([w[:, :64], w[:, :64], w[:, 64:], w[:, 64:]], axis=1)


def _block_ones(n):
    return jnp.kron(jnp.eye(n // HEAD_DIM, dtype=F32), jnp.ones((HEAD_DIM, HEAD_DIM), F32)).astype(BF16)


def _proj(x2, mod, g_norm1, w_in, q_gain, ks_gain, kw_gain):
    o = np.cumsum((0, 512, 128, 128, 128, 128, 128, 128, 24, 512, 1024, 1024))
    parts = [w_in[:, o[i]:o[i + 1]] for i in range(11)]
    wq, wkc, wvc, wks, wvs, wkw, wvw, wgn, wu, wga, wgs = parts
    w = jnp.concatenate([wq, wkc, wvc, _dup_cols(wks), _dup_cols(wkw),
                         jnp.pad(wgn, ((0, 0), (0, LANES - 24))), wu, wga, wgs], axis=1).astype(BF16)
    wvt = jnp.concatenate([wvs, wvw], axis=1).T.astype(BF16)
    tm = PROJ_TM
    row = lambda i: (i, 0)
    col = lambda i: (0, i)
    fix = lambda i: (0, 0)
    outs = [(512, BF16, row), (128, F32, row), (128, F32, row), (256, BF16, row), (256, BF16, row),
            (LANES, BF16, col), (LANES, BF16, col),
            (128, F32, row), (512, F32, row), (1024, BF16, row), (1024, BF16, row)]
    ospec = lambda wd, m: pl.BlockSpec((tm, wd), m) if m is row else pl.BlockSpec((wd, tm), m)
    oshape = lambda wd, dt, m: jax.ShapeDtypeStruct((N_TOK, wd) if m is row else (wd, N_TOK), dt)
    return pl.pallas_call(
        _proj_kernel,
        grid=(N_TOK // tm,),
        in_specs=[pl.BlockSpec((tm, D_MODEL), row),
                  pl.BlockSpec((1, 6, D_MODEL), lambda i: (i // (SEQ // tm), 0, 0)),
                  pl.BlockSpec((1, D_MODEL), fix),
                  pl.BlockSpec((D_MODEL, _C_END), fix),
                  pl.BlockSpec((2 * LANES, D_MODEL), fix),
                  pl.BlockSpec((1, 512), fix), pl.BlockSpec((1, 256), fix), pl.BlockSpec((1, 256), fix),
                  pl.BlockSpec((512, 512), fix), pl.BlockSpec((256, 256), fix)],
        out_specs=[ospec(wd, m) for wd, _, m in outs],
        out_shape=[oshape(wd, dt, m) for wd, dt, m in outs],
        compiler_params=_cparams("arbitrary"),
        name="proj",
    )(x2, mod, g_norm1.reshape(1, -1), w, wvt,
      jnp.tile(q_gain, N_HEADS).reshape(1, -1), jnp.tile(ks_gain, 4).reshape(1, -1),
      jnp.tile(kw_gain, 4).reshape(1, -1), _block_ones(512), _block_ones(256))


def _compress_kernel(r_ref, pe_ref, w1_ref, w2_ref, bd_ref, gain_ref, o_ref, *, do_norm):
    r = jnp.concatenate([r_ref[0, :, l, :] for l in range(CMP_STRIDE)], axis=1)
    p0 = _dot((r + pe_ref[0]).astype(BF16), w1_ref[0])
    p1 = _dot((r + pe_ref[1]).astype(BF16), w1_ref[1])
    hid = p0 + pltpu.roll(p1, N_CMP - 1, 0)
    c = _dot(_gelu(hid).astype(BF16), w2_ref[...])
    if do_norm:
        c = _seg_rms(c, bd_ref[...], gain_ref[...])
    o_ref[0] = c.astype(BF16)


def _compress(raw, pe, w1, w2, gain, do_norm):
    r = raw.reshape(BATCH, SEQ // CMP_STRIDE, CMP_STRIDE, LANES)
    eye = jnp.eye(N_KV, dtype=F32)
    w1r = w1.reshape(2, CMP_STRIDE, HEAD_DIM, CMP_HIDDEN)
    w1big = jnp.einsum('hldc,gk->hlgdkc', w1r, eye).reshape(2, CMP_STRIDE * LANES, N_KV * CMP_HIDDEN).astype(BF16)
    w2big = jnp.einsum('cd,gk->gckd', w2, eye)
    w2big = jnp.concatenate([w2big, w2big], axis=-1).reshape(N_KV * CMP_HIDDEN, 4 * HEAD_DIM).astype(BF16)
    pe_big = jnp.broadcast_to(pe.reshape(2, CMP_STRIDE, 1, HEAD_DIM), (2, CMP_STRIDE, N_KV, HEAD_DIM))
    pe_big = pe_big.reshape(2, 1, CMP_STRIDE * LANES)
    fix2 = lambda b: (0, 0)
    fix3 = lambda b: (0, 0, 0)
    return pl.pallas_call(
        functools.partial(_compress_kernel, do_norm=do_norm),
        grid=(BATCH,),
        in_specs=[pl.BlockSpec((1, N_CMP, CMP_STRIDE, LANES), lambda b: (b, 0, 0, 0)),
                  pl.BlockSpec((2, 1, CMP_STRIDE * LANES), fix3),
                  pl.BlockSpec((2, CMP_STRIDE * LANES, N_KV * CMP_HIDDEN), fix3),
                  pl.BlockSpec((N_KV * CMP_HIDDEN, 256), fix2),
                  pl.BlockSpec((256, 256), fix2), pl.BlockSpec((1, 256), fix2)],
        out_specs=pl.BlockSpec((1, N_CMP, 256), lambda b: (b, 0, 0)),
        out_shape=jax.ShapeDtypeStruct((BATCH, N_CMP, 256), BF16),
        compiler_params=_cparams("arbitrary"),
        name="compress_k" if do_norm else "compress_v",
    )(r, pe_big, w1big, w2big, _block_ones(256), jnp.tile(gain, 4).reshape(1, -1))


ATT_TQ = 256
RANK_CHUNK = 16


def _head_variants(qb):
    lane = lax.broadcasted_iota(I32, qb.shape, 1)
    z = jnp.zeros_like(qb)
    return jnp.where(lane < HEAD_DIM, qb, z), jnp.where(lane < HEAD_DIM, z, qb)


def _cmp_kernel(q_ref, kc_ref, vc_ref, ov_ref, o_ref, sel_ref, vrank_ref):
    tq = ATT_TQ
    qi = pl.program_id(1)
    tpos = qi * tq + lax.broadcasted_iota(I32, (tq, N_CMP), 0)
    nidx = lax.broadcasted_iota(I32, (tq, N_CMP), 1)
    mask = (CMP_STRIDE * nidx + (CMP_BLOCK - 1)) <= tpos
    lane_lo = lax.broadcasted_iota(I32, (tq, LANES), 1) < HEAD_DIM
    for g in range(N_KV):
        kd = kc_ref[0, :, g * LANES:(g + 1) * LANES]
        vd = vc_ref[0, :, g * LANES:(g + 1) * LANES]
        psum = jnp.zeros((tq, N_CMP), F32)
        for jb in range(2):
            blk = 2 * g + jb
            pv = []
            for qv in _head_variants(q_ref[:, blk * LANES:(blk + 1) * LANES]):
                s = jnp.where(mask, _dot_nt(qv, kd), NEG)
                m = jnp.max(s, axis=-1, keepdims=True)
                e = jnp.where(mask, jnp.exp2(s - m), 0.0)
                l = jnp.sum(e, axis=-1, keepdims=True)
                p = e / jnp.where(l > 0.0, l, 1.0)
                psum = psum + p
                pv.append(_dot(p.astype(BF16), vd))
            o_ref[:, blk * LANES:(blk + 1) * LANES] = jnp.where(lane_lo, pv[0], pv[1]).astype(BF16)
        imp = _split_dot(psum, ov_ref[...])
        imp_t = imp.T[:N_SEL_BLOCKS]
        j = lax.broadcasted_iota(I32, (N_SEL_BLOCKS, tq), 0)
        cur = jnp.right_shift(qi * tq + lax.broadcasted_iota(I32, (N_SEL_BLOCKS, tq), 1), 6)
        forced = (j == 0) | (j == cur) | (j == cur - 1)
        v = jnp.where(forced, jnp.inf, jnp.where(j <= cur, imp_t, -jnp.inf))
        vrank_ref[...] = jnp.zeros((N_SEL_BLOCKS, tq), F32)
        n_live = (qi + 1) * (tq // SEL_BLOCK)
        for c0 in range(0, N_SEL_BLOCKS, RANK_CHUNK):
            @pl.when(c0 < n_live)
            def _():
                rank = vrank_ref[...]
                for jp in range(c0, c0 + RANK_CHUNK):
                    row = v[jp:jp + 1, :]
                    tie = jnp.where(j > jp, 1.0, 0.0)
                    rank = rank + jnp.where(row > v, 1.0, jnp.where(row == v, tie, 0.0))
                vrank_ref[...] = rank
        rank = vrank_ref[...]
        sel_ref[g * N_SEL_BLOCKS:(g + 1) * N_SEL_BLOCKS, :] = jnp.where(rank < float(N_SELECT), 0.0, NEG)


def _cmp_attn(q, kcn, vcn):
    nc = np.arange(N_CMP)
    sb = np.arange(LANES)
    ov = ((CMP_STRIDE * nc[:, None] < SEL_BLOCK * sb[None, :] + SEL_BLOCK)
          & (CMP_STRIDE * nc[:, None] + CMP_BLOCK > SEL_BLOCK * sb[None, :])
          & (nc[:, None] < N_CMP - 1) & (sb[None, :] < N_SEL_BLOCKS))
    ov = jnp.asarray(ov, BF16)
    tq = ATT_TQ
    nq = SEQ // tq
    row = lambda b, i: (b * nq + i, 0)
    return pl.pallas_call(
        _cmp_kernel,
        grid=(BATCH, nq),
        in_specs=[pl.BlockSpec((tq, NSA_WIDTH), row),
                  pl.BlockSpec((1, N_CMP, 256), lambda b, i: (b, 0, 0)),
                  pl.BlockSpec((1, N_CMP, 256), lambda b, i: (b, 0, 0)),
                  pl.BlockSpec((N_CMP, LANES), lambda b, i: (0, 0))],
        out_specs=[pl.BlockSpec((tq, NSA_WIDTH), row),
                   pl.BlockSpec((N_KV * N_SEL_BLOCKS, tq), lambda b, i: (0, b * nq + i))],
        out_shape=[jax.ShapeDtypeStruct((N_TOK, NSA_WIDTH), BF16),
                   jax.ShapeDtypeStruct((N_KV * N_SEL_BLOCKS, N_TOK), F32)],
        scratch_shapes=[pltpu.VMEM((N_SEL_BLOCKS, tq), F32)],
        compiler_params=_cparams("arbitrary", "arbitrary"),
        name="cmp_attn",
    )(q, kcn, vcn, ov)


ATT_TK = 256


M_INIT = -1e29


SUM_ROWS = 16


def _selwin_kernel(q_ref, ks_ref, kw_ref, vst_ref, vwt_ref, selb_ref, osel_ref, owin_ref, m_ref, acc_ref):
    tq, tk = ATT_TQ, ATT_TK
    qi = pl.program_id(1)
    krow = lax.broadcasted_iota(I32, (tk, tq), 0)
    qcol = lax.broadcasted_iota(I32, (tk, tq), 1)
    causal_bias = jnp.where(krow <= qcol, 0.0, NEG)
    far_bias = jnp.where(qcol < krow, 0.0, NEG)

    ones_rows = jnp.ones((SUM_ROWS, tk), BF16)

    def reset():
        m_ref[...] = jnp.full(m_ref.shape, M_INIT, F32)
        acc_ref[...] = jnp.zeros(acc_ref.shape, F32)

    def update(g, k_ref, vt_ref, kt, bias):
        k0 = pl.multiple_of(kt * tk, tk)
        kd = k_ref[0, pl.ds(k0, tk), g * LANES:(g + 1) * LANES]
        vt = vt_ref[g * HEAD_DIM:(g + 1) * HEAD_DIM, pl.ds(k0, tk)]
        s = _dot_nt(kd, qvars[g])
        if bias is not None:
            s = s + jnp.concatenate([bias] * 4, axis=1)
        m_old = m_ref[g]
        m_new = jnp.maximum(m_old, jnp.max(s, axis=0, keepdims=True))
        alpha = jnp.exp2(m_old - m_new)
        p = jnp.exp2(s - m_new)
        m_ref[g] = m_new
        vte = jnp.concatenate([vt, ones_rows], axis=0)
        acc_ref[g] = alpha * acc_ref[g] + _dot(vte, p.astype(BF16))

    def finish(out_ref, g):
        o = acc_ref[g, :HEAD_DIM, :] / acc_ref[g, HEAD_DIM:HEAD_DIM + 1, :]
        for jb in range(2):
            blk = 2 * g + jb
            pair = jnp.concatenate([o[:, 2 * jb * tq:(2 * jb + 1) * tq], o[:, (2 * jb + 1) * tq:(2 * jb + 2) * tq]],
                                   axis=0)
            out_ref[:, blk * LANES:(blk + 1) * LANES] = pair.T.astype(BF16)

    def sel_bias(g, kt):
        rows = [jnp.broadcast_to(selb_ref[pl.ds(g * N_SEL_BLOCKS + kt * (tk // SEL_BLOCK) + r, 1), :],
                                 (SEL_BLOCK, tq)) for r in range(tk // SEL_BLOCK)]
        return jnp.concatenate(rows, axis=0)

    qvars = []
    for g in range(N_KV):
        heads = []
        for jb in range(2):
            heads.extend(_head_variants(q_ref[:, (2 * g + jb) * LANES:(2 * g + jb + 1) * LANES]))
        qvars.append(jnp.concatenate(heads, axis=0))
    groups = range(N_KV)

    reset()

    def sel_step(kt, carry):
        for g in groups:
            update(g, ks_ref, vst_ref, kt, sel_bias(g, kt))
        return carry

    lax.fori_loop(0, qi, sel_step, 0)
    for g in groups:
        update(g, ks_ref, vst_ref, qi, sel_bias(g, qi) + causal_bias)
    for g in groups:
        finish(osel_ref, g)

    reset()

    @pl.when(qi >= 2)
    def _():
        for g in groups:
            update(g, kw_ref, vwt_ref, qi - 2, far_bias)

    @pl.when(qi >= 1)
    def _():
        for g in groups:
            update(g, kw_ref, vwt_ref, qi - 1, None)

    for g in groups:
        update(g, kw_ref, vwt_ref, qi, causal_bias)
    for g in groups:
        finish(owin_ref, g)


def _selwin(q, ks, kw, vst, vwt, selb):
    tq = ATT_TQ
    nq = SEQ // tq
    assert WINDOW == 2 * ATT_TK and ATT_TQ == ATT_TK
    row = lambda b, i: (b * nq + i, 0)
    keys = pl.BlockSpec((1, SEQ, 256), lambda b, i: (b, 0, 0))
    vals = pl.BlockSpec((LANES, SEQ), lambda b, i: (0, b))
    r3 = lambda a: a.reshape(BATCH, SEQ, 256)
    return pl.pallas_call(
        _selwin_kernel,
        grid=(BATCH, nq),
        in_specs=[pl.BlockSpec((tq, NSA_WIDTH), row), keys, keys, vals, vals,
                  pl.BlockSpec((N_KV * N_SEL_BLOCKS, tq), lambda b, i: (0, b * nq + i))],
        out_specs=[pl.BlockSpec((tq, NSA_WIDTH), row)] * 2,
        out_shape=[jax.ShapeDtypeStruct((N_TOK, NSA_WIDTH), BF16)] * 2,
        scratch_shapes=[pltpu.VMEM((N_KV, 1, 4 * tq), F32),
                        pltpu.VMEM((N_KV, HEAD_DIM + SUM_ROWS, 4 * tq), F32)],
        compiler_params=_cparams("arbitrary", "arbitrary"),
        name="selwin",
    )(q, r3(ks), r3(kw), vst, vwt, selb)


def _s5_param_kernel(are_ref, aim_ref, ldt_ref, cre_ref, cim_ref, bre_ref, bim_ref,
                     clre_ref, clim_ref, wbre_ref, wbim_ref, bbre_ref, bbim_ref, ltre_ref, ltim_ref):
    are, aim = are_ref[...], aim_ref[...]
    dt = jnp.exp(ldt_ref[...])
    cre, cim = cre_ref[...], cim_ref[...]

    def lam_pow(tau):
        mag = jnp.exp(are * dt * float(tau))
        ang = aim * dt * float(tau)
        return mag * jnp.cos(ang), mag * jnp.sin(ang)

    lre, lim = lam_pow(1)
    den = are * are + aim * aim
    qre = ((lre - 1.0) * are + lim * aim) / den
    qim = (lim * are - (lre - 1.0) * aim) / den
    bre, bim = bre_ref[...], bim_ref[...]
    bbre = qre * bre - qim * bim
    bbim = qre * bim + qim * bre
    bbre_ref[...] = bbre
    bbim_ref[...] = bbim
    for tau in range(S5_T + 1):
        pr, pi = lam_pow(tau)
        clre_ref[tau] = cre * pr - cim * pi
        clim_ref[tau] = cre * pi + cim * pr
        if tau < S5_T:
            k = S5_T - 1 - tau
            wbre_ref[k] = pr * bbre - pi * bbim
            wbim_ref[k] = pr * bbim + pi * bbre
        else:
            ltre_ref[...] = pr
            ltim_ref[...] = pi


def _s5_kmat_kernel(l_ref, r_ref, o_ref):
    o_ref[0] = jnp.dot(l_ref[0], r_ref[0], preferred_element_type=F32, precision=HIGHEST)


def _s5_params(a_re, a_im, log_dt, b_re, b_im, c_re, c_im):
    T = S5_T
    pn = GROUP * STATE
    tile_p = lambda a: jnp.tile(a, (1, GROUP))
    args = (tile_p(a_re), tile_p(a_im), jnp.broadcast_to(log_dt[:, None], (N_GROUPS, pn)),
            c_re.reshape(N_GROUPS, pn), c_im.reshape(N_GROUPS, pn),
            jnp.swapaxes(b_re, 1, 2).reshape(N_GROUPS, pn), jnp.swapaxes(b_im, 1, 2).reshape(N_GROUPS, pn))
    full2 = pl.BlockSpec((N_GROUPS, pn), lambda: (0, 0))
    clre, clim, wbre, wbim, bbre, bbim, ltre, ltim = pl.pallas_call(
        _s5_param_kernel,
        in_specs=[full2] * 7,
        out_specs=[pl.BlockSpec((T + 1, N_GROUPS, pn), lambda: (0, 0, 0))] * 2
                  + [pl.BlockSpec((T, N_GROUPS, pn), lambda: (0, 0, 0))] * 2 + [full2] * 4,
        out_shape=[jax.ShapeDtypeStruct((T + 1, N_GROUPS, pn), F32)] * 2
                  + [jax.ShapeDtypeStruct((T, N_GROUPS, pn), F32)] * 2
                  + [jax.ShapeDtypeStruct((N_GROUPS, pn), F32)] * 4,
        name="s5_params",
    )(*args)

    r5 = lambda a, t: a[:t].reshape(t, N_GROUPS, GROUP, STATE)
    lhs = jnp.concatenate([r5(clre, T), -r5(clim, T)], axis=-1)
    lhs = jnp.transpose(lhs, (1, 0, 2, 3)).reshape(N_GROUPS, T * GROUP, 2 * STATE)
    bb = lambda a: jnp.swapaxes(a.reshape(N_GROUPS, GROUP, STATE), 1, 2)
    rhs = jnp.concatenate([bb(bbre), bb(bbim)], axis=1)
    kmat = pl.pallas_call(
        _s5_kmat_kernel,
        grid=(N_GROUPS,),
        in_specs=[pl.BlockSpec((1, T * GROUP, 2 * STATE), lambda g: (g, 0, 0)),
                  pl.BlockSpec((1, 2 * STATE, GROUP), lambda g: (g, 0, 0))],
        out_specs=pl.BlockSpec((1, T * GROUP, GROUP), lambda g: (g, 0, 0)),
        out_shape=jax.ShapeDtypeStruct((N_GROUPS, T * GROUP, GROUP), F32),
        compiler_params=_cparams("arbitrary"),
        name="s5_kmat",
    )(lhs, rhs)

    eye = jnp.eye(S5_GL, dtype=F32)
    kt = kmat.reshape(S5_SG, S5_GL, T, GROUP, GROUP)
    kbd = jnp.einsum('sgtpq,gh->stgqhp', kt, eye).reshape(S5_SG, T, LANES, LANES)
    krev = kbd[:, ::-1].reshape(S5_SG, T * LANES, LANES).astype(BF16)
    r6 = lambda a: a.reshape(T, S5_SG, S5_GL, GROUP, STATE)
    wb = jnp.stack([r6(wbre), r6(wbim)], axis=-2)
    wb = jnp.einsum('ksgpin,gh->skgpihn', wb, eye).reshape(S5_SG, T * LANES, S5_NSTATE).astype(BF16)
    wc = jnp.stack([r6(clre[1:]), -r6(clim[1:])], axis=-2)
    wc = jnp.einsum('tsgpin,gh->signthp', wc, eye).reshape(S5_SG, S5_NSTATE, T * LANES).astype(BF16)
    lt = lambda a: a.reshape(N_GROUPS, GROUP, STATE)[:, 0].reshape(S5_SG, 1, S5_GL * STATE)
    return krev, wb, wc, lt(ltre), lt(ltim)


S5_TC = 256


def _s5_lane_block(sg):
    return pl.ds(pl.multiple_of(sg * LANES, LANES), LANES)


def _s5_chunk_inputs(x_ref, sg):
    return jnp.concatenate([x_ref[:, t, _s5_lane_block(sg)] for t in range(S5_T)], axis=1).astype(BF16)


def _s5_state_kernel(x_ref, wb_ref, e_ref):
    e_ref[0] = _dot(_s5_chunk_inputs(x_ref, pl.program_id(1)), wb_ref[0])


def _s5_scan_kernel(e_ref, ltre_ref, ltim_ref, xs_ref):
    lr, li = ltre_ref[0], ltim_ref[0]
    half = S5_NSTATE // 2

    def step(c, carry):
        new = []
        for b in range(BATCH):
            xr, xi = carry[b]
            row = b * S5_CH_PER_BATCH + c
            xs_ref[0, pl.ds(row, 1), :half] = xr
            xs_ref[0, pl.ds(row, 1), half:] = xi
            e = e_ref[0, pl.ds(row, 1), :]
            new.append((lr * xr - li * xi + e[:, :half], lr * xi + li * xr + e[:, half:]))
        return tuple(new)

    zero = jnp.zeros((1, half), F32)
    lax.fori_loop(0, S5_CH_PER_BATCH, step, tuple((zero, zero) for _ in range(BATCH)))


def _s5_out_kernel(x_ref, xs_ref, krev_ref, wc_ref, y_ref):
    sg = pl.program_id(1)
    x = _s5_chunk_inputs(x_ref, sg)
    xsb = xs_ref[0].astype(BF16)
    for t in range(S5_T):
        cols = slice(t * LANES, (t + 1) * LANES)
        y_ref[:, t, _s5_lane_block(sg)] = (_dot(x[:, :(t + 1) * LANES], krev_ref[0, (S5_T - 1 - t) * LANES:, :])
                                           + _dot(xsb, wc_ref[0, :, cols]))


def _s5(u, krev, wb, wc, ltre, ltim):
    T, tc = S5_T, S5_TC
    xn = u.reshape(S5_CH, T, SSM_WIDTH)
    grid = (S5_CH // tc, S5_SG)
    natural = pl.BlockSpec((tc, T, SSM_WIDTH), lambda i, s: (i, 0, 0))
    rows = lambda i, s: (s, i, 0)
    per_sg = lambda i, s: (s, 0, 0)
    e = pl.pallas_call(
        _s5_state_kernel, grid=grid,
        in_specs=[natural, pl.BlockSpec((1, T * LANES, S5_NSTATE), per_sg)],
        out_specs=pl.BlockSpec((1, tc, S5_NSTATE), rows),
        out_shape=jax.ShapeDtypeStruct((S5_SG, S5_CH, S5_NSTATE), F32),
        compiler_params=_cparams("arbitrary", "arbitrary"), name="s5_state",
    )(xn, wb)
    sg1 = lambda s: (s, 0, 0)
    xstart = pl.pallas_call(
        _s5_scan_kernel, grid=(S5_SG,),
        in_specs=[pl.BlockSpec((1, S5_CH, S5_NSTATE), sg1),
                  pl.BlockSpec((1, 1, S5_NSTATE // 2), sg1), pl.BlockSpec((1, 1, S5_NSTATE // 2), sg1)],
        out_specs=pl.BlockSpec((1, S5_CH, S5_NSTATE), sg1),
        out_shape=jax.ShapeDtypeStruct((S5_SG, S5_CH, S5_NSTATE), F32),
        compiler_params=_cparams("arbitrary"), name="s5_scan",
    )(e, ltre, ltim)
    y = pl.pallas_call(
        _s5_out_kernel, grid=grid,
        in_specs=[natural, pl.BlockSpec((1, tc, S5_NSTATE), rows),
                  pl.BlockSpec((1, T * LANES, LANES), per_sg), pl.BlockSpec((1, S5_NSTATE, T * LANES), per_sg)],
        out_specs=natural,
        out_shape=jax.ShapeDtypeStruct((S5_CH, T, SSM_WIDTH), F32),
        compiler_params=_cparams("arbitrary", "arbitrary", vmem=56 * 1024 * 1024), name="s5_out",
    )(xn, xstart, krev, wc)
    return y.reshape(N_TOK, SSM_WIDTH)


MERGE_TM = 256


def _merge_kernel(ocmp_ref, osel_ref, owin_ref, gn_ref, yssm_ref, u_ref, ga_ref, gs_ref, x_ref, mod_ref,
                  eg_ref, dskip_ref, wglu_ref, bglu_ref, wua_ref, wus_ref, wout_ref, g2_ref,
                  wrhi_ref, wrlo_ref, wsgu_ref, wsd_ref,
                  xpart_ref, h2_ref, logit_ref):
    mod = mod_ref[0]
    gnb = gn_ref[...].astype(BF16)
    o_nsa = (_dot(gnb, eg_ref[0]) * ocmp_ref[...].astype(F32)
             + _dot(gnb, eg_ref[1]) * osel_ref[...].astype(F32)
             + _dot(gnb, eg_ref[2]) * owin_ref[...].astype(F32))
    attn = _dot(o_nsa.astype(BF16), wua_ref[...])
    z = _gelu(yssm_ref[...] + dskip_ref[...] * u_ref[...])
    y_ssm = z * jax.nn.sigmoid(_dot(z.astype(BF16), wglu_ref[...]) + bglu_ref[...])
    ssm = _dot(y_ssm.astype(BF16), wus_ref[...])
    merged = ga_ref[...].astype(F32) * attn + gs_ref[...].astype(F32) * ssm
    x1 = x_ref[...] + mod[2:3] * _dot(merged.astype(BF16), wout_ref[...])

    ms = jnp.mean(x1 * x1, axis=-1, keepdims=True)
    h2 = (x1 * lax.rsqrt(ms + EPS) * g2_ref[...]) * (1.0 + mod[4:5]) + mod[3:4]
    hi = h2.astype(BF16)
    lo = (h2 - hi.astype(F32)).astype(BF16)
    h2_ref[...] = _pack_bf16_pairs(h2)
    logit_ref[...] = _dot_nt(wrhi_ref[...], hi) + _dot_nt(wrhi_ref[...], lo) + _dot_nt(wrlo_ref[...], hi)
    gu = _dot(hi, wsgu_ref[...])
    shared = _dot((_silu(gu[:, :D_EXPERT]) * gu[:, D_EXPERT:]).astype(BF16), wsd_ref[...])
    xpart_ref[...] = x1 + mod[5:6] * shared


def _merge(ocmp, osel, owin, gn, yssm, u, ga, gs, x2, mod, d_skip, w_glu, b_glu, w_up_attn, w_up_ssm, w_out,
           g_norm2, w_router, ws_gate, ws_up, ws_down):
    tm = MERGE_TM
    eg = np.zeros((3, LANES, NSA_WIDTH), np.float32)
    for j in range(3):
        for h in range(N_HEADS):
            eg[j, 3 * h + j, h * HEAD_DIM:(h + 1) * HEAD_DIM] = 1.0
    wr_t = w_router.T
    wr_hi = wr_t.astype(BF16)
    wr_lo = (wr_t - wr_hi.astype(F32)).astype(BF16)
    row = lambda i: (i, 0)
    fix2 = lambda i: (0, 0)
    wspec = lambda a: pl.BlockSpec(a.shape, (lambda i: (0,) * a.ndim))
    weights = [jnp.asarray(eg, BF16), d_skip.reshape(1, -1), w_glu.astype(BF16), b_glu.reshape(1, -1),
               w_up_attn.astype(BF16), w_up_ssm.astype(BF16), w_out.astype(BF16), g_norm2.reshape(1, -1),
               wr_hi, wr_lo, jnp.concatenate([ws_gate, ws_up], axis=1).astype(BF16), ws_down.astype(BF16)]
    acts = [(ocmp, 512), (osel, 512), (owin, 512), (gn, 128), (yssm, 512), (u, 512), (ga, 1024), (gs, 1024),
            (x2, 1024)]
    return pl.pallas_call(
        _merge_kernel,
        grid=(N_TOK // tm,),
        in_specs=[pl.BlockSpec((tm, wd), row) for _, wd in acts]
                 + [pl.BlockSpec((1, 6, D_MODEL), lambda i: (i // (SEQ // tm), 0, 0))]
                 + [wspec(w) for w in weights],
        out_specs=[pl.BlockSpec((tm, D_MODEL), row), pl.BlockSpec((tm, HALF), row),
                   pl.BlockSpec((N_EXPERTS, tm), lambda i: (0, i))],
        out_shape=[jax.ShapeDtypeStruct((N_TOK, D_MODEL), F32), jax.ShapeDtypeStruct((N_TOK, HALF), I32),
                   jax.ShapeDtypeStruct((N_EXPERTS, N_TOK), F32)],
        compiler_params=_cparams("arbitrary"),
        name="merge",
    )(*[a for a, _ in acts], mod, *weights)


ROUTE_TN = 512


def _route_kernel(logit_ref, bias_ref, eidx_ref, w_ref, count_ref, gscore_ref, masked_ref):
    tn = ROUTE_TN

    @pl.when(pl.program_id(0) == 0)
    def _():
        count_ref[...] = jnp.zeros(count_ref.shape, F32)

    sc = jax.nn.sigmoid(logit_ref[...])
    biased = sc + bias_ref[...]
    gi = lax.broadcasted_iota(I32, (EXPERTS_PER_GROUP, tn), 0).astype(F32)
    for g in range(N_EXPERT_GROUPS):
        blk = biased[g * EXPERTS_PER_GROUP:(g + 1) * EXPERTS_PER_GROUP]
        m1 = jnp.max(blk, axis=0, keepdims=True)
        i1 = jnp.min(jnp.where(blk == m1, gi, float(EXPERTS_PER_GROUP)), axis=0, keepdims=True)
        m2 = jnp.max(jnp.where(gi == i1, -jnp.inf, blk), axis=0, keepdims=True)
        gscore_ref[g:g + 1, :] = m1 + m2
    gs = gscore_ref[...]
    gidx = lax.broadcasted_iota(I32, (N_EXPERT_GROUPS, tn), 0)
    grank = jnp.zeros((N_EXPERT_GROUPS, tn), F32)
    for gp in range(N_EXPERT_GROUPS):
        row = gs[gp:gp + 1, :]
        tie = jnp.where(gidx > gp, 1.0, 0.0)
        grank = grank + jnp.where(row > gs, 1.0, jnp.where(row == gs, tie, 0.0))
    for g in range(N_EXPERT_GROUPS):
        keep = grank[g:g + 1, :] < float(TOPK_GROUPS)
        sl = slice(g * EXPERTS_PER_GROUP, (g + 1) * EXPERTS_PER_GROUP)
        masked_ref[sl, :] = jnp.where(keep, biased[sl], -jnp.inf)
    cur = masked_ref[...]
    eidx = lax.broadcasted_iota(I32, (N_EXPERTS, tn), 0).astype(F32)
    wsum = jnp.zeros((1, tn), F32)
    hits = jnp.zeros((N_EXPERTS, tn), F32)
    for k in range(TOP_K):
        m = jnp.max(cur, axis=0, keepdims=True)
        idx = jnp.min(jnp.where(cur == m, eidx, float(N_EXPERTS)), axis=0, keepdims=True)
        hit = eidx == idx
        wk = jnp.sum(jnp.where(hit, sc, 0.0), axis=0, keepdims=True)
        cur = jnp.where(hit, -jnp.inf, cur)
        hits = hits + jnp.where(hit, 1.0, 0.0)
        eidx_ref[k:k + 1, :] = idx.astype(I32)
        w_ref[k:k + 1, :] = wk
        wsum = wsum + wk
    w_ref[...] = w_ref[...] / wsum * ROUTE_SCALE
    count_ref[...] = count_ref[...] + jnp.sum(hits, axis=1, keepdims=True)


def _route(logits_t, router_bias):
    tn = ROUTE_TN
    return pl.pallas_call(
        _route_kernel,
        grid=(N_TOK // tn,),
        in_specs=[pl.BlockSpec((N_EXPERTS, tn), lambda i: (0, i)), pl.BlockSpec((N_EXPERTS, 1), lambda i: (0, 0))],
        out_specs=[pl.BlockSpec((TOP_K, tn), lambda i: (0, i))] * 2 + [pl.BlockSpec((N_EXPERTS, 1), lambda i: (0, 0))],
        out_shape=[jax.ShapeDtypeStruct((TOP_K, N_TOK), I32), jax.ShapeDtypeStruct((TOP_K, N_TOK), F32),
                   jax.ShapeDtypeStruct((N_EXPERTS, 1), F32)],
        scratch_shapes=[pltpu.VMEM((N_EXPERT_GROUPS, tn), F32), pltpu.VMEM((N_EXPERTS, tn), F32)],
        compiler_params=_cparams("arbitrary"),
        name="route",
    )(logits_t, router_bias.reshape(-1, 1))


N_MOE_BLK = NK // DISPATCH_BLOCK
N_ITEMS = N_MOE_BLK + N_EXPERTS
ASSIGN_BITS = 17


def _dispatch_plan(eidx, counts):
    e_flat = eidx.reshape(-1)
    key = jnp.sort(e_flat * NK + jnp.arange(NK, dtype=I32))
    order = key & (NK - 1)
    counts = counts.reshape(-1).astype(I32)
    start = jnp.cumsum(counts) - counts
    cuts = jnp.sort(jnp.concatenate([jnp.arange(N_MOE_BLK, dtype=I32) * DISPATCH_BLOCK, start]))
    lo = cuts
    hi = jnp.concatenate([cuts[1:], jnp.full((1,), NK, I32)])
    blk = jnp.minimum(lo // DISPATCH_BLOCK, N_MOE_BLK - 1)
    expert = jnp.clip(jnp.sum((start[None, :] <= lo[:, None]).astype(I32), axis=1) - 1, 0, N_EXPERTS - 1)
    one = jnp.ones((1,), I32)
    first = jnp.concatenate([one, (blk[1:] != blk[:-1]).astype(I32)])
    last = jnp.concatenate([(blk[1:] != blk[:-1]).astype(I32), one])
    new_expert = jnp.concatenate([one, (expert[1:] != expert[:-1]).astype(I32)])
    run_id = jnp.cumsum(new_expert) - 1
    n_runs = run_id[-1] + 1
    item = jnp.arange(N_ITEMS, dtype=I32)
    run_first_item = jnp.sort(jnp.where(new_expert == 1, item, N_ITEMS))
    run_expert = expert[jnp.minimum(run_first_item, N_ITEMS - 1)]
    ahead = run_id + (WEIGHT_RING - 1)
    ahead_expert = run_expert[jnp.minimum(ahead, N_ITEMS - 1)]
    ahead_valid = (ahead < n_runs).astype(I32)
    second_expert = run_expert[1:2]
    prologue = jnp.concatenate([second_expert, (n_runs > 1).astype(I32).reshape(1)])
    tok = jnp.right_shift(order, 3)
    home = (order & (TOP_K - 1)) * N_TOK + tok
    return tok, home, (blk, expert, lo - blk * DISPATCH_BLOCK, hi - blk * DISPATCH_BLOCK, first, last, new_expert,
                      run_id % WEIGHT_RING, ahead_expert, ahead_valid, prologue)


SC_CORES = 2
SC_SUBCORES = 16
SC_CHUNK = 128


def _sc_move_rows(table, idx, scatter):
    n = idx.shape[0]
    workers = SC_CORES * SC_SUBCORES
    per_worker = n // workers
    n_chunks = per_worker // SC_CHUNK
    assert per_worker * workers == n and n_chunks * SC_CHUNK == per_worker
    mesh = plsc.VectorSubcoreMesh(core_axis_name="c", subcore_axis_name="s",
                                  num_cores=SC_CORES, num_subcores=SC_SUBCORES)

    def body(table_hbm, idx_hbm, out_hbm, idx_v, rows_v, sem):
        wid = lax.axis_index("s") * SC_CORES + lax.axis_index("c")
        base = wid * per_worker

        @pl.loop(0, n_chunks)
        def _(j):
            off = base + j * SC_CHUNK
            pltpu.sync_copy(idx_hbm.at[pl.ds(off, SC_CHUNK)], idx_v)
            if scatter:
                pltpu.sync_copy(table_hbm.at[pl.ds(off, SC_CHUNK)], rows_v)
                pltpu.async_copy(rows_v, out_hbm.at[idx_v], sem).wait()
            else:
                pltpu.async_copy(table_hbm.at[idx_v], rows_v, sem).wait()
                pltpu.sync_copy(rows_v, out_hbm.at[pl.ds(off, SC_CHUNK)])

    return pl.kernel(
        body,
        out_type=jax.ShapeDtypeStruct((n, table.shape[1]), table.dtype),
        mesh=mesh,
        scratch_types=[pltpu.VMEM((SC_CHUNK,), I32), pltpu.VMEM((SC_CHUNK, table.shape[1]), table.dtype),
                       pltpu.SemaphoreType.DMA],
        name="sc_scatter_rows" if scatter else "sc_gather_rows",
    )(table, idx)


WEIGHT_RING = 3
WEIGHT_CHUNKS = 4


def _expert_weight_copies(w_hbm, wbuf, sem, expert, slot):
    rows = w_hbm.shape[1] // WEIGHT_CHUNKS
    return [pltpu.make_async_copy(w_hbm.at[expert, pl.ds(c * rows, rows)],
                                  wbuf.at[slot, pl.ds(c * rows, rows)], sem.at[slot])
            for c in range(WEIGHT_CHUNKS)]


def _moe_kernel(blk_ref, exp_ref, lo_ref, hi_ref, first_ref, last_ref, newexp_ref,
                slot_ref, ahead_exp_ref, ahead_ok_ref, prologue_ref,
                x_ref, wg_hbm, wu_hbm, wd_hbm, y_ref,
                acc_ref, wgf_ref, wuf_ref, wdf_ref, wgb_ref, wub_ref, wdb_ref, wsem):
    it = pl.program_id(0)
    lo, hi = lo_ref[it], hi_ref[it]
    streams = ((wg_hbm, wgf_ref), (wu_hbm, wuf_ref), (wd_hbm, wdf_ref))

    def request(expert, slot):
        for w_hbm, wbuf in streams:
            for cp in _expert_weight_copies(w_hbm, wbuf, wsem, expert, slot):
                cp.start()

    @pl.when(it == 0)
    def _():
        request(exp_ref[0], 0)

        @pl.when(prologue_ref[1] == 1)
        def _():
            request(prologue_ref[0], 1)

    @pl.when(newexp_ref[it] == 1)
    def _():
        slot = slot_ref[it]
        for w_hbm, wbuf in streams:
            for cp in _expert_weight_copies(w_hbm, wbuf, wsem, 0, slot):
                cp.wait()
        wgb_ref[...] = wgf_ref[slot].astype(BF16)
        wub_ref[...] = wuf_ref[slot].astype(BF16)
        wdb_ref[...] = wdf_ref[slot].astype(BF16)

        @pl.when(ahead_ok_ref[it] == 1)
        def _():
            ahead_slot = slot + (WEIGHT_RING - 1)
            request(ahead_exp_ref[it], jnp.where(ahead_slot >= WEIGHT_RING, ahead_slot - WEIGHT_RING, ahead_slot))

    @pl.when(first_ref[it] == 1)
    def _():
        acc_ref[...] = jnp.zeros(acc_ref.shape, F32)

    def expert_pass(r0, nrows):
        rows = slice(r0, r0 + nrows)
        ridx = r0 + lax.broadcasted_iota(I32, (nrows, HALF), 0)
        mine = (ridx >= lo) & (ridx < hi)
        xlo, xhi = _unpack_bf16_pairs(jnp.where(mine, x_ref[rows, :], 0))
        xlo, xhi = xlo.astype(BF16), xhi.astype(BF16)
        gate = _dot(xlo, wgb_ref[:HALF]) + _dot(xhi, wgb_ref[HALF:])
        up = _dot(xlo, wub_ref[:HALF]) + _dot(xhi, wub_ref[HALF:])
        acc_ref[rows, :] = acc_ref[rows, :] + _dot((_silu(gate) * up).astype(BF16), wdb_ref[...])

    mid = DISPATCH_BLOCK // 2
    pl.when((lo < mid) & (hi > mid))(lambda: expert_pass(0, DISPATCH_BLOCK))
    pl.when((hi > lo) & (hi <= mid))(lambda: expert_pass(0, mid))
    pl.when((hi > lo) & (lo >= mid))(lambda: expert_pass(mid, mid))

    @pl.when(last_ref[it] == 1)
    def _():
        y_ref[...] = _pack_bf16_pairs(acc_ref[...])


def _moe(xs, items, w_gate, w_up, w_down):
    by_blk = lambda it, blk, *_: (blk[it], 0)
    any_space = pl.BlockSpec(memory_space=pl.ANY)
    grid_spec = pltpu.PrefetchScalarGridSpec(
        num_scalar_prefetch=len(items),
        grid=(N_ITEMS,),
        in_specs=[pl.BlockSpec((DISPATCH_BLOCK, HALF), by_blk), any_space, any_space, any_space],
        out_specs=pl.BlockSpec((DISPATCH_BLOCK, HALF), by_blk),
        scratch_shapes=[pltpu.VMEM((DISPATCH_BLOCK, D_MODEL), F32),
                        pltpu.VMEM((WEIGHT_RING, D_MODEL, D_EXPERT), F32),
                        pltpu.VMEM((WEIGHT_RING, D_MODEL, D_EXPERT), F32),
                        pltpu.VMEM((WEIGHT_RING, D_EXPERT, D_MODEL), F32),
                        pltpu.VMEM((D_MODEL, D_EXPERT), BF16), pltpu.VMEM((D_MODEL, D_EXPERT), BF16),
                        pltpu.VMEM((D_EXPERT, D_MODEL), BF16),
                        pltpu.SemaphoreType.DMA((WEIGHT_RING,))],
    )
    return pl.pallas_call(
        _moe_kernel,
        grid_spec=grid_spec,
        out_shape=jax.ShapeDtypeStruct((NK, HALF), I32),
        compiler_params=_cparams("arbitrary"),
        name="moe",
    )(*items, xs, w_gate, w_up, w_down)


COMB_TC = 256


def _combine_kernel(slots_ref, w_ref, xpart_ref, mod_ref, out_ref):
    w = w_ref[...]
    lo = jnp.zeros((w.shape[0], HALF), F32)
    hi = jnp.zeros((w.shape[0], HALF), F32)
    for k in range(TOP_K):
        klo, khi = _unpack_bf16_pairs(slots_ref[k])
        lo = lo + w[:, k:k + 1] * klo
        hi = hi + w[:, k:k + 1] * khi
    gate2 = mod_ref[0][5:6]
    out_ref[:, :HALF] = xpart_ref[:, :HALF] + gate2[:, :HALF] * lo
    out_ref[:, HALF:] = xpart_ref[:, HALF:] + gate2[:, HALF:] * hi


def _combine(xpart, mod, slots, w):
    tc = COMB_TC
    row = lambda i: (i, 0)
    return pl.pallas_call(
        _combine_kernel,
        grid=(N_TOK // tc,),
        in_specs=[pl.BlockSpec((TOP_K, tc, HALF), lambda i: (0, i, 0)),
                  pl.BlockSpec((tc, TOP_K), row),
                  pl.BlockSpec((tc, D_MODEL), row),
                  pl.BlockSpec((1, 6, D_MODEL), lambda i: (i // (SEQ // tc), 0, 0))],
        out_specs=pl.BlockSpec((tc, D_MODEL), row),
        out_shape=jax.ShapeDtypeStruct((N_TOK, D_MODEL), F32),
        compiler_params=_cparams("arbitrary"),
        name="combine",
    )(slots.reshape(TOP_K, N_TOK, HALF), w, xpart, mod)


def _layer(x, c, w_ada, b_ada, g_norm1, g_norm2, w_in, q_gain, kc_gain, ks_gain, kw_gain,
           pe_k, pe_v, w_cmp_k1, w_cmp_k2, w_cmp_v1, w_cmp_v2,
           a_re, a_im, log_dt, b_re, b_im, c_re, c_im, d_skip, w_glu, b_glu,
           w_up_attn, w_up_ssm, w_out, w_router, router_bias,
           w_gate, w_up, w_down, ws_gate, ws_up, ws_down):
    x2 = x.reshape(N_TOK, D_MODEL)
    mod = _ada(c, w_ada, b_ada)
    q, kc_raw, vc_raw, ks, kw, vst, vwt, gn, u, ga, gs = _proj(x2, mod, g_norm1, w_in, q_gain, ks_gain, kw_gain)
    kcn = _compress(kc_raw, pe_k, w_cmp_k1, w_cmp_k2, kc_gain, True)
    vcn = _compress(vc_raw, pe_v, w_cmp_v1, w_cmp_v2, kc_gain, False)
    ocmp, selb = _cmp_attn(q, kcn, vcn)
    osel, owin = _selwin(q, ks, kw, vst, vwt, selb)
    yssm = _s5(u, *_s5_params(a_re, a_im, log_dt, b_re, b_im, c_re, c_im))
    xpart, h2, logits_t = _merge(ocmp, osel, owin, gn, yssm, u, ga, gs, x2, mod, d_skip, w_glu, b_glu,
                                  w_up_attn, w_up_ssm, w_out, g_norm2, w_router, ws_gate, ws_up, ws_down)
    eidx_t, w_t, counts = _route(logits_t, router_bias)
    tok, home, items = _dispatch_plan(eidx_t.T, counts)
    y = _moe(_sc_move_rows(h2, tok, scatter=False), items, w_gate, w_up, w_down)
    slots = _sc_move_rows(y, home, scatter=True)
    return _combine(xpart, mod, slots, w_t.T).reshape(BATCH, SEQ, D_MODEL)


def kernel(x, c, w_ada, b_ada, g_norm1, g_norm2, w_in, q_gain, kc_gain, ks_gain, kw_gain, pe_k, pe_v, w_cmp_k1,
           w_cmp_k2, w_cmp_v1, w_cmp_v2, a_re, a_im, log_dt, b_re, b_im, c_re, c_im, d_skip, w_glu, b_glu,
           w_up_attn, w_up_ssm, w_out, w_router, router_bias, w_gate, w_up, w_down, ws_gate, ws_up, ws_down):
    params = (w_ada, b_ada, g_norm1, g_norm2, w_in, q_gain, kc_gain, ks_gain, kw_gain, pe_k, pe_v, w_cmp_k1,
              w_cmp_k2, w_cmp_v1, w_cmp_v2, a_re, a_im, log_dt, b_re, b_im, c_re, c_im, d_skip, w_glu, b_glu,
              w_up_attn, w_up_ssm, w_out, w_router, router_bias, w_gate, w_up, w_down, ws_gate, ws_up, ws_down)
    depth = w_ada.shape[0]
    for layer in range(depth):
        x = _layer(x, c, *[p[layer] for p in params])
    return x
```

```python
import functools
import math

import jax
import jax.numpy as jnp
import numpy as np
from jax import lax
from jax.experimental import pallas as pl
from jax.experimental.pallas import tpu as pltpu
from jax.experimental.pallas import tpu_sc as plsc

F32 = jnp.float32
BF16 = jnp.bfloat16
I32 = jnp.int32
HIGHEST = lax.Precision.HIGHEST

D_MODEL = 1024
BATCH = 4
SEQ = 4096
N_TOK = BATCH * SEQ
N_HEADS = 8
HEAD_DIM = 64
N_KV = 2
CMP_BLOCK = 32
CMP_STRIDE = 16
CMP_HIDDEN = 256
N_CMP = 256
SEL_BLOCK = 64
N_SEL_BLOCKS = SEQ // SEL_BLOCK
N_SELECT = 16
WINDOW = 512
ATTN_SCALE = HEAD_DIM ** -0.5
LOG2E = 1.4426950408889634
NSA_WIDTH = N_HEADS * HEAD_DIM
SSM_WIDTH = 512
GROUP = 16
N_GROUPS = SSM_WIDTH // GROUP
STATE = 64
N_EXPERTS = 256
TOP_K = 8
D_EXPERT = 256
N_EXPERT_GROUPS = 8
EXPERTS_PER_GROUP = N_EXPERTS // N_EXPERT_GROUPS
TOPK_GROUPS = 4
ROUTE_SCALE = 2.5
DISPATCH_BLOCK = 512
EPS = 1e-6
NEG = -1e30

LANES = 128
S5_T = 16
S5_SG = 4
S5_GL = N_GROUPS // S5_SG
S5_CH = N_TOK // S5_T
S5_CH_PER_BATCH = SEQ // S5_T
S5_NSTATE = S5_GL * STATE * 2

NK = N_TOK * TOP_K
HALF = D_MODEL // 2

VMEM_LIMIT = 48 * 1024 * 1024


def _cparams(*sem, vmem=VMEM_LIMIT):
    return pltpu.CompilerParams(dimension_semantics=tuple(sem), vmem_limit_bytes=vmem)


def _dot(a, b):
    return jnp.dot(a, b, preferred_element_type=F32)


def _dot_nt(a, b):
    return lax.dot_general(a, b, (((1,), (1,)), ((), ())), preferred_element_type=F32)


def _split_dot(v, w):
    hi = v.astype(BF16)
    lo = (v - hi.astype(F32)).astype(BF16)
    return _dot(hi, w) + _dot(lo, w)


def _seg_rms(v, bd, gain):
    ss = _split_dot(v * v, bd)
    return v * lax.rsqrt(ss * (1.0 / HEAD_DIM) + EPS) * gain


def _gelu(x):
    return 0.5 * x * (1.0 + jnp.tanh(0.7978845608028654 * (x + 0.044715 * (x * x * x))))


def _silu(x):
    return x * jax.nn.sigmoid(x)


def _pack_bf16_pairs(v):
    h = v.shape[1] // 2
    return pltpu.pack_elementwise([v[:, :h], v[:, h:]], packed_dtype=BF16)


def _unpack_bf16_pairs(word):
    return (pltpu.unpack_elementwise(word, index=0, packed_dtype=BF16, unpacked_dtype=F32),
            pltpu.unpack_elementwise(word, index=1, packed_dtype=BF16, unpacked_dtype=F32))


def _ada_kernel(c_ref, w_ref, b_ref, o_ref):
    c = c_ref[...]
    o_ref[...] = jnp.dot(_silu(c), w_ref[...], preferred_element_type=F32, precision=HIGHEST) + b_ref[...]


def _ada(c, w_ada, b_ada):
    cp = jnp.pad(c, ((0, 8 - BATCH), (0, 0)))
    tn = 1536
    out = pl.pallas_call(
        _ada_kernel,
        grid=(6 * D_MODEL // tn,),
        in_specs=[pl.BlockSpec((8, D_MODEL), lambda j: (0, 0)),
                  pl.BlockSpec((D_MODEL, tn), lambda j: (0, j)),
                  pl.BlockSpec((1, tn), lambda j: (0, j))],
        out_specs=pl.BlockSpec((8, tn), lambda j: (0, j)),
        out_shape=jax.ShapeDtypeStruct((8, 6 * D_MODEL), F32),
        compiler_params=_cparams("arbitrary"),
        name="ada",
    )(cp, w_ada, b_ada.reshape(1, -1))
    return out.reshape(8, 6, D_MODEL)


_C_Q = 0
_C_KC = 512
_C_VC = 640
_C_KS = 768
_C_KW = 1024
_C_GN = 1280
_C_U = 1408
_C_GA = 1920
_C_GS = 2944
_C_END = 3968
PROJ_TM = 512


def _proj_kernel(x_ref, mod_ref, g1_ref, w_ref, wvt_ref, qg_ref, ksg_ref, kwg_ref, bd512_ref, bd256_ref,
                 q_ref, kc_ref, vc_ref, ks_ref, kw_ref, vst_ref, vwt_ref, gn_ref, u_ref, ga_ref, gs_ref):
    x = x_ref[...]
    ms = jnp.mean(x * x, axis=-1, keepdims=True)
    mod = mod_ref[0]
    h = (x * lax.rsqrt(ms + EPS) * g1_ref[...]) * (1.0 + mod[1:2]) + mod[0:1]
    hb = h.astype(BF16)

    def p(lo, hi):
        return _dot(hb, w_ref[:, lo:hi])

    q_ref[...] = _seg_rms(p(_C_Q, _C_KC), bd512_ref[...], qg_ref[...] * (ATTN_SCALE * LOG2E)).astype(BF16)
    kc_ref[...] = p(_C_KC, _C_VC)
    vc_ref[...] = p(_C_VC, _C_KS)
    ks_ref[...] = _seg_rms(p(_C_KS, _C_KW), bd256_ref[...], ksg_ref[...]).astype(BF16)
    kw_ref[...] = _seg_rms(p(_C_KW, _C_GN), bd256_ref[...], kwg_ref[...]).astype(BF16)
    vt = _dot_nt(wvt_ref[...], hb)
    vst_ref[...] = vt[:LANES].astype(BF16)
    vwt_ref[...] = vt[LANES:].astype(BF16)
    gn_ref[...] = jax.nn.sigmoid(p(_C_GN, _C_U))
    u_ref[...] = p(_C_U, _C_GA)
    ga_ref[...] = jax.nn.sigmoid(p(_C_GA, _C_GS)).astype(BF16)
    gs_ref[...] = jax.nn.sigmoid(p(_C_GS, _C_END)).astype(BF16)


def _dup_cols(w):
    return jnp.concatenate([w[:, :64], w[:, :64], w[:, 64:], w[:, 64:]], axis=1)


def _block_ones(n):
    return jnp.kron(jnp.eye(n // HEAD_DIM, dtype=F32), jnp.ones((HEAD_DIM, HEAD_DIM), F32)).astype(BF16)


def _proj(x2, mod, g_norm1, w_in, q_gain, ks_gain, kw_gain):
    o = np.cumsum((0, 512, 128, 128, 128, 128, 128, 128, 24, 512, 1024, 1024))
    parts = [w_in[:, o[i]:o[i + 1]] for i in range(11)]
    wq, wkc, wvc, wks, wvs, wkw, wvw, wgn, wu, wga, wgs = parts
    w = jnp.concatenate([wq, wkc, wvc, _dup_cols(wks), _dup_cols(wkw),
                         jnp.pad(wgn, ((0, 0), (0, LANES - 24))), wu, wga, wgs], axis=1).astype(BF16)
    wvt = jnp.concatenate([wvs, wvw], axis=1).T.astype(BF16)
    tm = PROJ_TM
    row = lambda i: (i, 0)
    col = lambda i: (0, i)
    fix = lambda i: (0, 0)
    outs = [(512, BF16, row), (128, F32, row), (128, F32, row), (256, BF16, row), (256, BF16, row),
            (LANES, BF16, col), (LANES, BF16, col),
            (128, F32, row), (512, F32, row), (1024, BF16, row), (1024, BF16, row)]
    ospec = lambda wd, m: pl.BlockSpec((tm, wd), m) if m is row else pl.BlockSpec((wd, tm), m)
    oshape = lambda wd, dt, m: jax.ShapeDtypeStruct((N_TOK, wd) if m is row else (wd, N_TOK), dt)
    return pl.pallas_call(
        _proj_kernel,
        grid=(N_TOK // tm,),
        in_specs=[pl.BlockSpec((tm, D_MODEL), row),
                  pl.BlockSpec((1, 6, D_MODEL), lambda i: (i // (SEQ // tm), 0, 0)),
                  pl.BlockSpec((1, D_MODEL), fix),
                  pl.BlockSpec((D_MODEL, _C_END), fix),
                  pl.BlockSpec((2 * LANES, D_MODEL), fix),
                  pl.BlockSpec((1, 512), fix), pl.BlockSpec((1, 256), fix), pl.BlockSpec((1, 256), fix),
                  pl.BlockSpec((512, 512), fix), pl.BlockSpec((256, 256), fix)],
        out_specs=[ospec(wd, m) for wd, _, m in outs],
        out_shape=[oshape(wd, dt, m) for wd, dt, m in outs],
        compiler_params=_cparams("arbitrary"),
        name="proj",
    )(x2, mod, g_norm1.reshape(1, -1), w, wvt,
      jnp.tile(q_gain, N_HEADS).reshape(1, -1), jnp.tile(ks_gain, 4).reshape(1, -1),
      jnp.tile(kw_gain, 4).reshape(1, -1), _block_ones(512), _block_ones(256))


def _compress_kernel(r_ref, pe_ref, w1_ref, w2_ref, bd_ref, gain_ref, o_ref, *, do_norm):
    r = jnp.concatenate([r_ref[0, :, l, :] for l in range(CMP_STRIDE)], axis=1)
    p0 = _dot((r + pe_ref[0]).astype(BF16), w1_ref[0])
    p1 = _dot((r + pe_ref[1]).astype(BF16), w1_ref[1])
    hid = p0 + pltpu.roll(p1, N_CMP - 1, 0)
    c = _dot(_gelu(hid).astype(BF16), w2_ref[...])
    if do_norm:
        c = _seg_rms(c, bd_ref[...], gain_ref[...])
    o_ref[0] = c.astype(BF16)


def _compress(raw, pe, w1, w2, gain, do_norm):
    r = raw.reshape(BATCH, SEQ // CMP_STRIDE, CMP_STRIDE, LANES)
    eye = jnp.eye(N_KV, dtype=F32)
    w1r = w1.reshape(2, CMP_STRIDE, HEAD_DIM, CMP_HIDDEN)
    w1big = jnp.einsum('hldc,gk->hlgdkc', w1r, eye).reshape(2, CMP_STRIDE * LANES, N_KV * CMP_HIDDEN).astype(BF16)
    w2big = jnp.einsum('cd,gk->gckd', w2, eye)
    w2big = jnp.concatenate([w2big, w2big], axis=-1).reshape(N_KV * CMP_HIDDEN, 4 * HEAD_DIM).astype(BF16)
    pe_big = jnp.broadcast_to(pe.reshape(2, CMP_STRIDE, 1, HEAD_DIM), (2, CMP_STRIDE, N_KV, HEAD_DIM))
    pe_big = pe_big.reshape(2, 1, CMP_STRIDE * LANES)
    fix2 = lambda b: (0, 0)
    fix3 = lambda b: (0, 0, 0)
    return pl.pallas_call(
        functools.partial(_compress_kernel, do_norm=do_norm),
        grid=(BATCH,),
        in_specs=[pl.BlockSpec((1, N_CMP, CMP_STRIDE, LANES), lambda b: (b, 0, 0, 0)),
                  pl.BlockSpec((2, 1, CMP_STRIDE * LANES), fix3),
                  pl.BlockSpec((2, CMP_STRIDE * LANES, N_KV * CMP_HIDDEN), fix3),
                  pl.BlockSpec((N_KV * CMP_HIDDEN, 256), fix2),
                  pl.BlockSpec((256, 256), fix2), pl.BlockSpec((1, 256), fix2)],
        out_specs=pl.BlockSpec((1, N_CMP, 256), lambda b: (b, 0, 0)),
        out_shape=jax.ShapeDtypeStruct((BATCH, N_CMP, 256), BF16),
        compiler_params=_cparams("arbitrary"),
        name="compress_k" if do_norm else "compress_v",
    )(r, pe_big, w1big, w2big, _block_ones(256), jnp.tile(gain, 4).reshape(1, -1))


ATT_TQ = 256
RANK_CHUNK = 16


def _head_variants(qb):
    lane = lax.broadcasted_iota(I32, qb.shape, 1)
    z = jnp.zeros_like(qb)
    return jnp.where(lane < HEAD_DIM, qb, z), jnp.where(lane < HEAD_DIM, z, qb)


def _cmp_kernel(q_ref, kc_ref, vc_ref, ov_ref, o_ref, sel_ref, vrank_ref):
    tq = ATT_TQ
    qi = pl.program_id(1)
    tpos = qi * tq + lax.broadcasted_iota(I32, (tq, N_CMP), 0)
    nidx = lax.broadcasted_iota(I32, (tq, N_CMP), 1)
    mask = (CMP_STRIDE * nidx + (CMP_BLOCK - 1)) <= tpos
    lane_lo = lax.broadcasted_iota(I32, (tq, LANES), 1) < HEAD_DIM
    for g in range(N_KV):
        kd = kc_ref[0, :, g * LANES:(g + 1) * LANES]
        vd = vc_ref[0, :, g * LANES:(g + 1) * LANES]
        psum = jnp.zeros((tq, N_CMP), F32)
        for jb in range(2):
            blk = 2 * g + jb
            pv = []
            for qv in _head_variants(q_ref[:, blk * LANES:(blk + 1) * LANES]):
                s = jnp.where(mask, _dot_nt(qv, kd), NEG)
                m = jnp.max(s, axis=-1, keepdims=True)
                e = jnp.where(mask, jnp.exp2(s - m), 0.0)
                l = jnp.sum(e, axis=-1, keepdims=True)
                p = e / jnp.where(l > 0.0, l, 1.0)
                psum = psum + p
                pv.append(_dot(p.astype(BF16), vd))
            o_ref[:, blk * LANES:(blk + 1) * LANES] = jnp.where(lane_lo, pv[0], pv[1]).astype(BF16)
        imp = _split_dot(psum, ov_ref[...])
        imp_t = imp.T[:N_SEL_BLOCKS]
        j = lax.broadcasted_iota(I32, (N_SEL_BLOCKS, tq), 0)
        cur = jnp.right_shift(qi * tq + lax.broadcasted_iota(I32, (N_SEL_BLOCKS, tq), 1), 6)
        forced = (j == 0) | (j == cur) | (j == cur - 1)
        v = jnp.where(forced, jnp.inf, jnp.where(j <= cur, imp_t, -jnp.inf))
        vrank_ref[...] = jnp.zeros((N_SEL_BLOCKS, tq), F32)
        n_live = (qi + 1) * (tq // SEL_BLOCK)
        for c0 in range(0, N_SEL_BLOCKS, RANK_CHUNK):
            @pl.when(c0 < n_live)
            def _():
                rank = vrank_ref[...]
                for jp in range(c0, c0 + RANK_CHUNK):
                    row = v[jp:jp + 1, :]
                    tie = jnp.where(j > jp, 1.0, 0.0)
                    rank = rank + jnp.where(row > v, 1.0, jnp.where(row == v, tie, 0.0))
                vrank_ref[...] = rank
        rank = vrank_ref[...]
        sel_ref[g * N_SEL_BLOCKS:(g + 1) * N_SEL_BLOCKS, :] = jnp.where(rank < float(N_SELECT), 0.0, NEG)


def _cmp_attn(q, kcn, vcn):
    nc = np.arange(N_CMP)
    sb = np.arange(LANES)
    ov = ((CMP_STRIDE * nc[:, None] < SEL_BLOCK * sb[None, :] + SEL_BLOCK)
          & (CMP_STRIDE * nc[:, None] + CMP_BLOCK > SEL_BLOCK * sb[None, :])
          & (nc[:, None] < N_CMP - 1) & (sb[None, :] < N_SEL_BLOCKS))
    ov = jnp.asarray(ov, BF16)
    tq = ATT_TQ
    nq = SEQ // tq
    row = lambda b, i: (b * nq + i, 0)
    return pl.pallas_call(
        _cmp_kernel,
        grid=(BATCH, nq),
        in_specs=[pl.BlockSpec((tq, NSA_WIDTH), row),
                  pl.BlockSpec((1, N_CMP, 256), lambda b, i: (b, 0, 0)),
                  pl.BlockSpec((1, N_CMP, 256), lambda b, i: (b, 0, 0)),
                  pl.BlockSpec((N_CMP, LANES), lambda b, i: (0, 0))],
        out_specs=[pl.BlockSpec((tq, NSA_WIDTH), row),
                   pl.BlockSpec((N_KV * N_SEL_BLOCKS, tq), lambda b, i: (0, b * nq + i))],
        out_shape=[jax.ShapeDtypeStruct((N_TOK, NSA_WIDTH), BF16),
                   jax.ShapeDtypeStruct((N_KV * N_SEL_BLOCKS, N_TOK), F32)],
        scratch_shapes=[pltpu.VMEM((N_SEL_BLOCKS, tq), F32)],
        compiler_params=_cparams("arbitrary", "arbitrary"),
        name="cmp_attn",
    )(q, kcn, vcn, ov)


ATT_TK = 256


M_INIT = -1e29


SUM_ROWS = 16


def _selwin_kernel(q_ref, ks_ref, kw_ref, vst_ref, vwt_ref, selb_ref, osel_ref, owin_ref, m_ref, acc_ref):
    tq, tk = ATT_TQ, ATT_TK
    qi = pl.program_id(1)
    krow = lax.broadcasted_iota(I32, (tk, tq), 0)
    qcol = lax.broadcasted_iota(I32, (tk, tq), 1)
    causal_bias = jnp.where(krow <= qcol, 0.0, NEG)
    far_bias = jnp.where(qcol < krow, 0.0, NEG)

    ones_rows = jnp.ones((SUM_ROWS, tk), BF16)

    def reset():
        m_ref[...] = jnp.full(m_ref.shape, M_INIT, F32)
        acc_ref[...] = jnp.zeros(acc_ref.shape, F32)

    def update(g, k_ref, vt_ref, kt, bias):
        k0 = pl.multiple_of(kt * tk, tk)
        kd = k_ref[0, pl.ds(k0, tk), g * LANES:(g + 1) * LANES]
        vt = vt_ref[g * HEAD_DIM:(g + 1) * HEAD_DIM, pl.ds(k0, tk)]
        s = _dot_nt(kd, qvars[g])
        if bias is not None:
            s = s + jnp.concatenate([bias] * 4, axis=1)
        m_old = m_ref[g]
        m_new = jnp.maximum(m_old, jnp.max(s, axis=0, keepdims=True))
        alpha = jnp.exp2(m_old - m_new)
        p = jnp.exp2(s - m_new)
        m_ref[g] = m_new
        vte = jnp.concatenate([vt, ones_rows], axis=0)
        acc_ref[g] = alpha * acc_ref[g] + _dot(vte, p.astype(BF16))

    def finish(out_ref, g):
        o = acc_ref[g, :HEAD_DIM, :] / acc_ref[g, HEAD_DIM:HEAD_DIM + 1, :]
        for jb in range(2):
            blk = 2 * g + jb
            pair = jnp.concatenate([o[:, 2 * jb * tq:(2 * jb + 1) * tq], o[:, (2 * jb + 1) * tq:(2 * jb + 2) * tq]],
                                   axis=0)
            out_ref[:, blk * LANES:(blk + 1) * LANES] = pair.T.astype(BF16)

    def sel_bias(g, kt):
        rows = [jnp.broadcast_to(selb_ref[pl.ds(g * N_SEL_BLOCKS + kt * (tk // SEL_BLOCK) + r, 1), :],
                                 (SEL_BLOCK, tq)) for r in range(tk // SEL_BLOCK)]
        return jnp.concatenate(rows, axis=0)

    qvars = []
    for g in range(N_KV):
        heads = []
        for jb in range(2):
            heads.extend(_head_variants(q_ref[:, (2 * g + jb) * LANES:(2 * g + jb + 1) * LANES]))
        qvars.append(jnp.concatenate(heads, axis=0))
    groups = range(N_KV)

    reset()

    def sel_step(kt, carry):
        for g in groups:
            update(g, ks_ref, vst_ref, kt, sel_bias(g, kt))
        return carry

    lax.fori_loop(0, qi, sel_step, 0)
    for g in groups:
        update(g, ks_ref, vst_ref, qi, sel_bias(g, qi) + causal_bias)
    for g in groups:
        finish(osel_ref, g)

    reset()

    @pl.when(qi >= 2)
    def _():
        for g in groups:
            update(g, kw_ref, vwt_ref, qi - 2, far_bias)

    @pl.when(qi >= 1)
    def _():
        for g in groups:
            update(g, kw_ref, vwt_ref, qi - 1, None)

    for g in groups:
        update(g, kw_ref, vwt_ref, qi, causal_bias)
    for g in groups:
        finish(owin_ref, g)


def _selwin(q, ks, kw, vst, vwt, selb):
    tq = ATT_TQ
    nq = SEQ // tq
    assert WINDOW == 2 * ATT_TK and ATT_TQ == ATT_TK
    row = lambda b, i: (b * nq + i, 0)
    keys = pl.BlockSpec((1, SEQ, 256), lambda b, i: (b, 0, 0))
    vals = pl.BlockSpec((LANES, SEQ), lambda b, i: (0, b))
    r3 = lambda a: a.reshape(BATCH, SEQ, 256)
    return pl.pallas_call(
        _selwin_kernel,
        grid=(BATCH, nq),
        in_specs=[pl.BlockSpec((tq, NSA_WIDTH), row), keys, keys, vals, vals,
                  pl.BlockSpec((N_KV * N_SEL_BLOCKS, tq), lambda b, i: (0, b * nq + i))],
        out_specs=[pl.BlockSpec((tq, NSA_WIDTH), row)] * 2,
        out_shape=[jax.ShapeDtypeStruct((N_TOK, NSA_WIDTH), BF16)] * 2,
        scratch_shapes=[pltpu.VMEM((N_KV, 1, 4 * tq), F32),
                        pltpu.VMEM((N_KV, HEAD_DIM + SUM_ROWS, 4 * tq), F32)],
        compiler_params=_cparams("arbitrary", "arbitrary"),
        name="selwin",
    )(q, r3(ks), r3(kw), vst, vwt, selb)


def _s5_param_kernel(are_ref, aim_ref, ldt_ref, cre_ref, cim_ref, bre_ref, bim_ref,
                     clre_ref, clim_ref, wbre_ref, wbim_ref, bbre_ref, bbim_ref, ltre_ref, ltim_ref):
    are, aim = are_ref[...], aim_ref[...]
    dt = jnp.exp(ldt_ref[...])
    cre, cim = cre_ref[...], cim_ref[...]

    def lam_pow(tau):
        mag = jnp.exp(are * dt * float(tau))
        ang = aim * dt * float(tau)
        return mag * jnp.cos(ang), mag * jnp.sin(ang)

    lre, lim = lam_pow(1)
    den = are * are + aim * aim
    qre = ((lre - 1.0) * are + lim * aim) / den
    qim = (lim * are - (lre - 1.0) * aim) / den
    bre, bim = bre_ref[...], bim_ref[...]
    bbre = qre * bre - qim * bim
    bbim = qre * bim + qim * bre
    bbre_ref[...] = bbre
    bbim_ref[...] = bbim
    for tau in range(S5_T + 1):
        pr, pi = lam_pow(tau)
        clre_ref[tau] = cre * pr - cim * pi
        clim_ref[tau] = cre * pi + cim * pr
        if tau < S5_T:
            k = S5_T - 1 - tau
            wbre_ref[k] = pr * bbre - pi * bbim
            wbim_ref[k] = pr * bbim + pi * bbre
        else:
            ltre_ref[...] = pr
            ltim_ref[...] = pi


def _s5_kmat_kernel(l_ref, r_ref, o_ref):
    o_ref[0] = jnp.dot(l_ref[0], r_ref[0], preferred_element_type=F32, precision=HIGHEST)


def _s5_params(a_re, a_im, log_dt, b_re, b_im, c_re, c_im):
    T = S5_T
    pn = GROUP * STATE
    tile_p = lambda a: jnp.tile(a, (1, GROUP))
    args = (tile_p(a_re), tile_p(a_im), jnp.broadcast_to(log_dt[:, None], (N_GROUPS, pn)),
            c_re.reshape(N_GROUPS, pn), c_im.reshape(N_GROUPS, pn),
            jnp.swapaxes(b_re, 1, 2).reshape(N_GROUPS, pn), jnp.swapaxes(b_im, 1, 2).reshape(N_GROUPS, pn))
    full2 = pl.BlockSpec((N_GROUPS, pn), lambda: (0, 0))
    clre, clim, wbre, wbim, bbre, bbim, ltre, ltim = pl.pallas_call(
        _s5_param_kernel,
        in_specs=[full2] * 7,
        out_specs=[pl.BlockSpec((T + 1, N_GROUPS, pn), lambda: (0, 0, 0))] * 2
                  + [pl.BlockSpec((T, N_GROUPS, pn), lambda: (0, 0, 0))] * 2 + [full2] * 4,
        out_shape=[jax.ShapeDtypeStruct((T + 1, N_GROUPS, pn), F32)] * 2
                  + [jax.ShapeDtypeStruct((T, N_GROUPS, pn), F32)] * 2
                  + [jax.ShapeDtypeStruct((N_GROUPS, pn), F32)] * 4,
        name="s5_params",
    )(*args)

    r5 = lambda a, t: a[:t].reshape(t, N_GROUPS, GROUP, STATE)
    lhs = jnp.concatenate([r5(clre, T), -r5(clim, T)], axis=-1)
    lhs = jnp.transpose(lhs, (1, 0, 2, 3)).reshape(N_GROUPS, T * GROUP, 2 * STATE)
    bb = lambda a: jnp.swapaxes(a.reshape(N_GROUPS, GROUP, STATE), 1, 2)
    rhs = jnp.concatenate([bb(bbre), bb(bbim)], axis=1)
    kmat = pl.pallas_call(
        _s5_kmat_kernel,
        grid=(N_GROUPS,),
        in_specs=[pl.BlockSpec((1, T * GROUP, 2 * STATE), lambda g: (g, 0, 0)),
                  pl.BlockSpec((1, 2 * STATE, GROUP), lambda g: (g, 0, 0))],
        out_specs=pl.BlockSpec((1, T * GROUP, GROUP), lambda g: (g, 0, 0)),
        out_shape=jax.ShapeDtypeStruct((N_GROUPS, T * GROUP, GROUP), F32),
        compiler_params=_cparams("arbitrary"),
        name="s5_kmat",
    )(lhs, rhs)

    eye = jnp.eye(S5_GL, dtype=F32)
    kt = kmat.reshape(S5_SG, S5_GL, T, GROUP, GROUP)
    kbd = jnp.einsum('sgtpq,gh->stgqhp', kt, eye).reshape(S5_SG, T, LANES, LANES)
    krev = kbd[:, ::-1].reshape(S5_SG, T * LANES, LANES).astype(BF16)
    r6 = lambda a: a.reshape(T, S5_SG, S5_GL, GROUP, STATE)
    wb = jnp.stack([r6(wbre), r6(wbim)], axis=-2)
    wb = jnp.einsum('ksgpin,gh->skgpihn', wb, eye).reshape(S5_SG, T * LANES, S5_NSTATE).astype(BF16)
    wc = jnp.stack([r6(clre[1:]), -r6(clim[1:])], axis=-2)
    wc = jnp.einsum('tsgpin,gh->signthp', wc, eye).reshape(S5_SG, S5_NSTATE, T * LANES).astype(BF16)
    lt = lambda a: a.reshape(N_GROUPS, GROUP, STATE)[:, 0].reshape(S5_SG, 1, S5_GL * STATE)
    return krev, wb, wc, lt(ltre), lt(ltim)


S5_TC = 256


def _s5_lane_block(sg):
    return pl.ds(pl.multiple_of(sg * LANES, LANES), LANES)


def _s5_chunk_inputs(x_ref, sg):
    return jnp.concatenate([x_ref[:, t, _s5_lane_block(sg)] for t in range(S5_T)], axis=1).astype(BF16)


def _s5_state_kernel(x_ref, wb_ref, e_ref):
    e_ref[0] = _dot(_s5_chunk_inputs(x_ref, pl.program_id(1)), wb_ref[0])


def _s5_scan_kernel(e_ref, ltre_ref, ltim_ref, xs_ref):
    lr, li = ltre_ref[0], ltim_ref[0]
    half = S5_NSTATE // 2

    def step(c, carry):
        new = []
        for b in range(BATCH):
            xr, xi = carry[b]
            row = b * S5_CH_PER_BATCH + c
            xs_ref[0, pl.ds(row, 1), :half] = xr
            xs_ref[0, pl.ds(row, 1), half:] = xi
            e = e_ref[0, pl.ds(row, 1), :]
            new.append((lr * xr - li * xi + e[:, :half], lr * xi + li * xr + e[:, half:]))
        return tuple(new)

    zero = jnp.zeros((1, half), F32)
    lax.fori_loop(0, S5_CH_PER_BATCH, step, tuple((zero, zero) for _ in range(BATCH)))


def _s5_out_kernel(x_ref, xs_ref, krev_ref, wc_ref, y_ref):
    sg = pl.program_id(1)
    x = _s5_chunk_inputs(x_ref, sg)
    xsb = xs_ref[0].astype(BF16)
    for t in range(S5_T):
        cols = slice(t * LANES, (t + 1) * LANES)
        y_ref[:, t, _s5_lane_block(sg)] = (_dot(x[:, :(t + 1) * LANES], krev_ref[0, (S5_T - 1 - t) * LANES:, :])
                                           + _dot(xsb, wc_ref[0, :, cols]))


def _s5(u, krev, wb, wc, ltre, ltim):
    T, tc = S5_T, S5_TC
    xn = u.reshape(S5_CH, T, SSM_WIDTH)
    grid = (S5_CH // tc, S5_SG)
    natural = pl.BlockSpec((tc, T, SSM_WIDTH), lambda i, s: (i, 0, 0))
    rows = lambda i, s: (s, i, 0)
    per_sg = lambda i, s: (s, 0, 0)
    e = pl.pallas_call(
        _s5_state_kernel, grid=grid,
        in_specs=[natural, pl.BlockSpec((1, T * LANES, S5_NSTATE), per_sg)],
        out_specs=pl.BlockSpec((1, tc, S5_NSTATE), rows),
        out_shape=jax.ShapeDtypeStruct((S5_SG, S5_CH, S5_NSTATE), F32),
        compiler_params=_cparams("arbitrary", "arbitrary"), name="s5_state",
    )(xn, wb)
    sg1 = lambda s: (s, 0, 0)
    xstart = pl.pallas_call(
        _s5_scan_kernel, grid=(S5_SG,),
        in_specs=[pl.BlockSpec((1, S5_CH, S5_NSTATE), sg1),
                  pl.BlockSpec((1, 1, S5_NSTATE // 2), sg1), pl.BlockSpec((1, 1, S5_NSTATE // 2), sg1)],
        out_specs=pl.BlockSpec((1, S5_CH, S5_NSTATE), sg1),
        out_shape=jax.ShapeDtypeStruct((S5_SG, S5_CH, S5_NSTATE), F32),
        compiler_params=_cparams("arbitrary"), name="s5_scan",
    )(e, ltre, ltim)
    y = pl.pallas_call(
        _s5_out_kernel, grid=grid,
        in_specs=[natural, pl.BlockSpec((1, tc, S5_NSTATE), rows),
                  pl.BlockSpec((1, T * LANES, LANES), per_sg), pl.BlockSpec((1, S5_NSTATE, T * LANES), per_sg)],
        out_specs=natural,
        out_shape=jax.ShapeDtypeStruct((S5_CH, T, SSM_WIDTH), F32),
        compiler_params=_cparams("arbitrary", "arbitrary", vmem=56 * 1024 * 1024), name="s5_out",
    )(xn, xstart, krev, wc)
    return y.reshape(N_TOK, SSM_WIDTH)


MERGE_TM = 512


def _merge_kernel(ocmp_ref, osel_ref, owin_ref, gn_ref, yssm_ref, u_ref, ga_ref, gs_ref, x_ref, mod_ref,
                  eg_ref, dskip_ref, wglu_ref, bglu_ref, wua_ref, wus_ref, wout_ref, g2_ref,
                  wrhi_ref, wrlo_ref, wsgu_ref, wsd_ref,
                  xpart_ref, h2_ref, logit_ref):
    mod = mod_ref[0]
    gnb = gn_ref[...].astype(BF16)
    o_nsa = (_dot(gnb, eg_ref[0]) * ocmp_ref[...].astype(F32)
             + _dot(gnb, eg_ref[1]) * osel_ref[...].astype(F32)
             + _dot(gnb, eg_ref[2]) * owin_ref[...].astype(F32))
    attn = _dot(o_nsa.astype(BF16), wua_ref[...])
    z = _gelu(yssm_ref[...] + dskip_ref[...] * u_ref[...])
    y_ssm = z * jax.nn.sigmoid(_dot(z.astype(BF16), wglu_ref[...]) + bglu_ref[...])
    ssm = _dot(y_ssm.astype(BF16), wus_ref[...])
    merged = ga_ref[...].astype(F32) * attn + gs_ref[...].astype(F32) * ssm
    x1 = x_ref[...] + mod[2:3] * _dot(merged.astype(BF16), wout_ref[...])

    ms = jnp.mean(x1 * x1, axis=-1, keepdims=True)
    h2 = (x1 * lax.rsqrt(ms + EPS) * g2_ref[...]) * (1.0 + mod[4:5]) + mod[3:4]
    hi = h2.astype(BF16)
    lo = (h2 - hi.astype(F32)).astype(BF16)
    h2_ref[...] = _pack_bf16_pairs(h2)
    logit_ref[...] = _dot_nt(wrhi_ref[...], hi) + _dot_nt(wrhi_ref[...], lo) + _dot_nt(wrlo_ref[...], hi)
    gu = _dot(hi, wsgu_ref[...])
    shared = _dot((_silu(gu[:, :D_EXPERT]) * gu[:, D_EXPERT:]).astype(BF16), wsd_ref[...])
    xpart_ref[...] = x1 + mod[5:6] * shared


def _merge(ocmp, osel, owin, gn, yssm, u, ga, gs, x2, mod, d_skip, w_glu, b_glu, w_up_attn, w_up_ssm, w_out,
           g_norm2, w_router, ws_gate, ws_up, ws_down):
    tm = MERGE_TM
    eg = np.zeros((3, LANES, NSA_WIDTH), np.float32)
    for j in range(3):
        for h in range(N_HEADS):
            eg[j, 3 * h + j, h * HEAD_DIM:(h + 1) * HEAD_DIM] = 1.0
    wr_t = w_router.T
    wr_hi = wr_t.astype(BF16)
    wr_lo = (wr_t - wr_hi.astype(F32)).astype(BF16)
    row = lambda i: (i, 0)
    fix2 = lambda i: (0, 0)
    wspec = lambda a: pl.BlockSpec(a.shape, (lambda i: (0,) * a.ndim))
    weights = [jnp.asarray(eg, BF16), d_skip.reshape(1, -1), w_glu.astype(BF16), b_glu.reshape(1, -1),
               w_up_attn.astype(BF16), w_up_ssm.astype(BF16), w_out.astype(BF16), g_norm2.reshape(1, -1),
               wr_hi, wr_lo, jnp.concatenate([ws_gate, ws_up], axis=1).astype(BF16), ws_down.astype(BF16)]
    acts = [(ocmp, 512), (osel, 512), (owin, 512), (gn, 128), (yssm, 512), (u, 512), (ga, 1024), (gs, 1024),
            (x2, 1024)]
    return pl.pallas_call(
        _merge_kernel,
        grid=(N_TOK // tm,),
        in_specs=[pl.BlockSpec((tm, wd), row) for _, wd in acts]
                 + [pl.BlockSpec((1, 6, D_MODEL), lambda i: (i // (SEQ // tm), 0, 0))]
                 + [wspec(w) for w in weights],
        out_specs=[pl.BlockSpec((tm, D_MODEL), row), pl.BlockSpec((tm, HALF), row),
                   pl.BlockSpec((N_EXPERTS, tm), lambda i: (0, i))],
        out_shape=[jax.ShapeDtypeStruct((N_TOK, D_MODEL), F32), jax.ShapeDtypeStruct((N_TOK, HALF), I32),
                   jax.ShapeDtypeStruct((N_EXPERTS, N_TOK), F32)],
        compiler_params=_cparams("arbitrary", vmem=56 * 1024 * 1024),
        name="merge",
    )(*[a for a, _ in acts], mod, *weights)


ROUTE_TN = 512


def _route_kernel(logit_ref, bias_ref, eidx_ref, w_ref, count_ref, gscore_ref, masked_ref):
    tn = ROUTE_TN

    @pl.when(pl.program_id(0) == 0)
    def _():
        count_ref[...] = jnp.zeros(count_ref.shape, F32)

    sc = jax.nn.sigmoid(logit_ref[...])
    biased = sc + bias_ref[...]
    gi = lax.broadcasted_iota(I32, (EXPERTS_PER_GROUP, tn), 0).astype(F32)
    for g in range(N_EXPERT_GROUPS):
        blk = biased[g * EXPERTS_PER_GROUP:(g + 1) * EXPERTS_PER_GROUP]
        m1 = jnp.max(blk, axis=0, keepdims=True)
        i1 = jnp.min(jnp.where(blk == m1, gi, float(EXPERTS_PER_GROUP)), axis=0, keepdims=True)
        m2 = jnp.max(jnp.where(gi == i1, -jnp.inf, blk), axis=0, keepdims=True)
        gscore_ref[g:g + 1, :] = m1 + m2
    gs = gscore_ref[...]
    gidx = lax.broadcasted_iota(I32, (N_EXPERT_GROUPS, tn), 0)
    grank = jnp.zeros((N_EXPERT_GROUPS, tn), F32)
    for gp in range(N_EXPERT_GROUPS):
        row = gs[gp:gp + 1, :]
        tie = jnp.where(gidx > gp, 1.0, 0.0)
        grank = grank + jnp.where(row > gs, 1.0, jnp.where(row == gs, tie, 0.0))
    for g in range(N_EXPERT_GROUPS):
        keep = grank[g:g + 1, :] < float(TOPK_GROUPS)
        sl = slice(g * EXPERTS_PER_GROUP, (g + 1) * EXPERTS_PER_GROUP)
        masked_ref[sl, :] = jnp.where(keep, biased[sl], -jnp.inf)
    cur = masked_ref[...]
    eidx = lax.broadcasted_iota(I32, (N_EXPERTS, tn), 0).astype(F32)
    wsum = jnp.zeros((1, tn), F32)
    hits = jnp.zeros((N_EXPERTS, tn), F32)
    for k in range(TOP_K):
        m = jnp.max(cur, axis=0, keepdims=True)
        idx = jnp.min(jnp.where(cur == m, eidx, float(N_EXPERTS)), axis=0, keepdims=True)
        hit = eidx == idx
        wk = jnp.sum(jnp.where(hit, sc, 0.0), axis=0, keepdims=True)
        cur = jnp.where(hit, -jnp.inf, cur)
        hits = hits + jnp.where(hit, 1.0, 0.0)
        eidx_ref[k:k + 1, :] = idx.astype(I32)
        w_ref[k:k + 1, :] = wk
        wsum = wsum + wk
    w_ref[...] = w_ref[...] / wsum * ROUTE_SCALE
    count_ref[...] = count_ref[...] + jnp.sum(hits, axis=1, keepdims=True)


def _route(logits_t, router_bias):
    tn = ROUTE_TN
    return pl.pallas_call(
        _route_kernel,
        grid=(N_TOK // tn,),
        in_specs=[pl.BlockSpec((N_EXPERTS, tn), lambda i: (0, i)), pl.BlockSpec((N_EXPERTS, 1), lambda i: (0, 0))],
        out_specs=[pl.BlockSpec((TOP_K, tn), lambda i: (0, i))] * 2 + [pl.BlockSpec((N_EXPERTS, 1), lambda i: (0, 0))],
        out_shape=[jax.ShapeDtypeStruct((TOP_K, N_TOK), I32), jax.ShapeDtypeStruct((TOP_K, N_TOK), F32),
                   jax.ShapeDtypeStruct((N_EXPERTS, 1), F32)],
        scratch_shapes=[pltpu.VMEM((N_EXPERT_GROUPS, tn), F32), pltpu.VMEM((N_EXPERTS, tn), F32)],
        compiler_params=_cparams("arbitrary"),
        name="route",
    )(logits_t, router_bias.reshape(-1, 1))


N_MOE_BLK = NK // DISPATCH_BLOCK
N_ITEMS = N_MOE_BLK + N_EXPERTS
ASSIGN_BITS = 17


def _dispatch_plan(eidx, counts):
    e_flat = eidx.reshape(-1)
    key = jnp.sort(e_flat * NK + jnp.arange(NK, dtype=I32))
    order = key & (NK - 1)
    counts = counts.reshape(-1).astype(I32)
    start = jnp.cumsum(counts) - counts
    cuts = jnp.sort(jnp.concatenate([jnp.arange(N_MOE_BLK, dtype=I32) * DISPATCH_BLOCK, start]))
    lo = cuts
    hi = jnp.concatenate([cuts[1:], jnp.full((1,), NK, I32)])
    blk = jnp.minimum(lo // DISPATCH_BLOCK, N_MOE_BLK - 1)
    expert = jnp.clip(jnp.sum((start[None, :] <= lo[:, None]).astype(I32), axis=1) - 1, 0, N_EXPERTS - 1)
    one = jnp.ones((1,), I32)
    first = jnp.concatenate([one, (blk[1:] != blk[:-1]).astype(I32)])
    last = jnp.concatenate([(blk[1:] != blk[:-1]).astype(I32), one])
    new_expert = jnp.concatenate([one, (expert[1:] != expert[:-1]).astype(I32)])
    run_id = jnp.cumsum(new_expert) - 1
    n_runs = run_id[-1] + 1
    item = jnp.arange(N_ITEMS, dtype=I32)
    run_first_item = jnp.sort(jnp.where(new_expert == 1, item, N_ITEMS))
    run_expert = expert[jnp.minimum(run_first_item, N_ITEMS - 1)]
    ahead = run_id + (WEIGHT_RING - 1)
    ahead_expert = run_expert[jnp.minimum(ahead, N_ITEMS - 1)]
    ahead_valid = (ahead < n_runs).astype(I32)
    second_expert = run_expert[1:2]
    prologue = jnp.concatenate([second_expert, (n_runs > 1).astype(I32).reshape(1)])
    tok = jnp.right_shift(order, 3)
    home = (order & (TOP_K - 1)) * N_TOK + tok
    return tok, home, (blk, expert, lo - blk * DISPATCH_BLOCK, hi - blk * DISPATCH_BLOCK, first, last, new_expert,
                      run_id % WEIGHT_RING, ahead_expert, ahead_valid, prologue)


SC_CORES = 2
SC_SUBCORES = 16
SC_CHUNK = 128


def _sc_move_rows(table, idx, scatter):
    n = idx.shape[0]
    workers = SC_CORES * SC_SUBCORES
    per_worker = n // workers
    n_chunks = per_worker // SC_CHUNK
    assert per_worker * workers == n and n_chunks * SC_CHUNK == per_worker
    mesh = plsc.VectorSubcoreMesh(core_axis_name="c", subcore_axis_name="s",
                                  num_cores=SC_CORES, num_subcores=SC_SUBCORES)

    def body(table_hbm, idx_hbm, out_hbm, idx_v, rows_v, sem):
        wid = lax.axis_index("s") * SC_CORES + lax.axis_index("c")
        base = wid * per_worker

        @pl.loop(0, n_chunks)
        def _(j):
            off = base + j * SC_CHUNK
            pltpu.sync_copy(idx_hbm.at[pl.ds(off, SC_CHUNK)], idx_v)
            if scatter:
                pltpu.sync_copy(table_hbm.at[pl.ds(off, SC_CHUNK)], rows_v)
                pltpu.async_copy(rows_v, out_hbm.at[idx_v], sem).wait()
            else:
                pltpu.async_copy(table_hbm.at[idx_v], rows_v, sem).wait()
                pltpu.sync_copy(rows_v, out_hbm.at[pl.ds(off, SC_CHUNK)])

    return pl.kernel(
        body,
        out_type=jax.ShapeDtypeStruct((n, table.shape[1]), table.dtype),
        mesh=mesh,
        scratch_types=[pltpu.VMEM((SC_CHUNK,), I32), pltpu.VMEM((SC_CHUNK, table.shape[1]), table.dtype),
                       pltpu.SemaphoreType.DMA],
        name="sc_scatter_rows" if scatter else "sc_gather_rows",
    )(table, idx)


WEIGHT_RING = 3
WEIGHT_CHUNKS = 4


def _expert_weight_copies(w_hbm, wbuf, sem, expert, slot):
    rows = w_hbm.shape[1] // WEIGHT_CHUNKS
    return [pltpu.make_async_copy(w_hbm.at[expert, pl.ds(c * rows, rows)],
                                  wbuf.at[slot, pl.ds(c * rows, rows)], sem.at[slot])
            for c in range(WEIGHT_CHUNKS)]


def _moe_kernel(blk_ref, exp_ref, lo_ref, hi_ref, first_ref, last_ref, newexp_ref,
                slot_ref, ahead_exp_ref, ahead_ok_ref, prologue_ref,
                x_ref, wg_hbm, wu_hbm, wd_hbm, y_ref,
                acc_ref, wgf_ref, wuf_ref, wdf_ref, wgb_ref, wub_ref, wdb_ref, wsem):
    it = pl.program_id(0)
    lo, hi = lo_ref[it], hi_ref[it]
    streams = ((wg_hbm, wgf_ref), (wu_hbm, wuf_ref), (wd_hbm, wdf_ref))

    def request(expert, slot):
        for w_hbm, wbuf in streams:
            for cp in _expert_weight_copies(w_hbm, wbuf, wsem, expert, slot):
                cp.start()

    @pl.when(it == 0)
    def _():
        request(exp_ref[0], 0)

        @pl.when(prologue_ref[1] == 1)
        def _():
            request(prologue_ref[0], 1)

    @pl.when(newexp_ref[it] == 1)
    def _():
        slot = slot_ref[it]
        for w_hbm, wbuf in streams:
            for cp in _expert_weight_copies(w_hbm, wbuf, wsem, 0, slot):
                cp.wait()
        wgb_ref[...] = wgf_ref[slot].astype(BF16)
        wub_ref[...] = wuf_ref[slot].astype(BF16)
        wdb_ref[...] = wdf_ref[slot].astype(BF16)

        @pl.when(ahead_ok_ref[it] == 1)
        def _():
            ahead_slot = slot + (WEIGHT_RING - 1)
            request(ahead_exp_ref[it], jnp.where(ahead_slot >= WEIGHT_RING, ahead_slot - WEIGHT_RING, ahead_slot))

    @pl.when(first_ref[it] == 1)
    def _():
        acc_ref[...] = jnp.zeros(acc_ref.shape, F32)

    def expert_pass(r0, nrows):
        rows = slice(r0, r0 + nrows)
        ridx = r0 + lax.broadcasted_iota(I32, (nrows, HALF), 0)
        mine = (ridx >= lo) & (ridx < hi)
        xlo, xhi = _unpack_bf16_pairs(jnp.where(mine, x_ref[rows, :], 0))
        xlo, xhi = xlo.astype(BF16), xhi.astype(BF16)
        gate = _dot(xlo, wgb_ref[:HALF]) + _dot(xhi, wgb_ref[HALF:])
        up = _dot(xlo, wub_ref[:HALF]) + _dot(xhi, wub_ref[HALF:])
        acc_ref[rows, :] = acc_ref[rows, :] + _dot((_silu(gate) * up).astype(BF16), wdb_ref[...])

    mid = DISPATCH_BLOCK // 2
    pl.when((lo < mid) & (hi > mid))(lambda: expert_pass(0, DISPATCH_BLOCK))
    pl.when((hi > lo) & (hi <= mid))(lambda: expert_pass(0, mid))
    pl.when((hi > lo) & (lo >= mid))(lambda: expert_pass(mid, mid))

    @pl.when(last_ref[it] == 1)
    def _():
        y_ref[...] = _pack_bf16_pairs(acc_ref[...])


def _moe(xs, items, w_gate, w_up, w_down):
    by_blk = lambda it, blk, *_: (blk[it], 0)
    any_space = pl.BlockSpec(memory_space=pl.ANY)
    grid_spec = pltpu.PrefetchScalarGridSpec(
        num_scalar_prefetch=len(items),
        grid=(N_ITEMS,),
        in_specs=[pl.BlockSpec((DISPATCH_BLOCK, HALF), by_blk), any_space, any_space, any_space],
        out_specs=pl.BlockSpec((DISPATCH_BLOCK, HALF), by_blk),
        scratch_shapes=[pltpu.VMEM((DISPATCH_BLOCK, D_MODEL), F32),
                        pltpu.VMEM((WEIGHT_RING, D_MODEL, D_EXPERT), F32),
                        pltpu.VMEM((WEIGHT_RING, D_MODEL, D_EXPERT), F32),
                        pltpu.VMEM((WEIGHT_RING, D_EXPERT, D_MODEL), F32),
                        pltpu.VMEM((D_MODEL, D_EXPERT), BF16), pltpu.VMEM((D_MODEL, D_EXPERT), BF16),
                        pltpu.VMEM((D_EXPERT, D_MODEL), BF16),
                        pltpu.SemaphoreType.DMA((WEIGHT_RING,))],
    )
    return pl.pallas_call(
        _moe_kernel,
        grid_spec=grid_spec,
        out_shape=jax.ShapeDtypeStruct((NK, HALF), I32),
        compiler_params=_cparams("arbitrary"),
        name="moe",
    )(*items, xs, w_gate, w_up, w_down)


COMB_TC = 512


def _combine_kernel(slots_ref, w_ref, xpart_ref, mod_ref, out_ref):
    w = w_ref[...]
    lo = jnp.zeros((w.shape[0], HALF), F32)
    hi = jnp.zeros((w.shape[0], HALF), F32)
    for k in range(TOP_K):
        klo, khi = _unpack_bf16_pairs(slots_ref[k])
        lo = lo + w[:, k:k + 1] * klo
        hi = hi + w[:, k:k + 1] * khi
    gate2 = mod_ref[0][5:6]
    out_ref[:, :HALF] = xpart_ref[:, :HALF] + gate2[:, :HALF] * lo
    out_ref[:, HALF:] = xpart_ref[:, HALF:] + gate2[:, HALF:] * hi


def _combine(xpart, mod, slots, w):
    tc = COMB_TC
    row = lambda i: (i, 0)
    return pl.pallas_call(
        _combine_kernel,
        grid=(N_TOK // tc,),
        in_specs=[pl.BlockSpec((TOP_K, tc, HALF), lambda i: (0, i, 0)),
                  pl.BlockSpec((tc, TOP_K), row),
                  pl.BlockSpec((tc, D_MODEL), row),
                  pl.BlockSpec((1, 6, D_MODEL), lambda i: (i // (SEQ // tc), 0, 0))],
        out_specs=pl.BlockSpec((tc, D_MODEL), row),
        out_shape=jax.ShapeDtypeStruct((N_TOK, D_MODEL), F32),
        compiler_params=_cparams("arbitrary"),
        name="combine",
    )(slots.reshape(TOP_K, N_TOK, HALF), w, xpart, mod)


def _layer(x, c, w_ada, b_ada, g_norm1, g_norm2, w_in, q_gain, kc_gain, ks_gain, kw_gain,
           pe_k, pe_v, w_cmp_k1, w_cmp_k2, w_cmp_v1, w_cmp_v2,
           a_re, a_im, log_dt, b_re, b_im, c_re, c_im, d_skip, w_glu, b_glu,
           w_up_attn, w_up_ssm, w_out, w_router, router_bias,
           w_gate, w_up, w_down, ws_gate, ws_up, ws_down):
    x2 = x.reshape(N_TOK, D_MODEL)
    mod = _ada(c, w_ada, b_ada)
    q, kc_raw, vc_raw, ks, kw, vst, vwt, gn, u, ga, gs = _proj(x2, mod, g_norm1, w_in, q_gain, ks_gain, kw_gain)
    kcn = _compress(kc_raw, pe_k, w_cmp_k1, w_cmp_k2, kc_gain, True)
    vcn = _compress(vc_raw, pe_v, w_cmp_v1, w_cmp_v2, kc_gain, False)
    ocmp, selb = _cmp_attn(q, kcn, vcn)
    osel, owin = _selwin(q, ks, kw, vst, vwt, selb)
    yssm = _s5(u, *_s5_params(a_re, a_im, log_dt, b_re, b_im, c_re, c_im))
    xpart, h2, logits_t = _merge(ocmp, osel, owin, gn, yssm, u, ga, gs, x2, mod, d_skip, w_glu, b_glu,
                                  w_up_attn, w_up_ssm, w_out, g_norm2, w_router, ws_gate, ws_up, ws_down)
    eidx_t, w_t, counts = _route(logits_t, router_bias)
    tok, home, items = _dispatch_plan(eidx_t.T, counts)
    y = _moe(_sc_move_rows(h2, tok, scatter=False), items, w_gate, w_up, w_down)
    slots = _sc_move_rows(y, home, scatter=True)
    return _combine(xpart, mod, slots, w_t.T).reshape(BATCH, SEQ, D_MODEL)


def kernel(x, c, w_ada, b_ada, g_norm1, g_norm2, w_in, q_gain, kc_gain, ks_gain, kw_gain, pe_k, pe_v, w_cmp_k1,
           w_cmp_k2, w_cmp_v1, w_cmp_v2, a_re, a_im, log_dt, b_re, b_im, c_re, c_im, d_skip, w_glu, b_glu,
           w_up_attn, w_up_ssm, w_out, w_router, router_bias, w_gate, w_up, w_down, ws_gate, ws_up, ws_down):
    params = (w_ada, b_ada, g_norm1, g_norm2, w_in, q_gain, kc_gain, ks_gain, kw_gain, pe_k, pe_v, w_cmp_k1,
              w_cmp_k2, w_cmp_v1, w_cmp_v2, a_re, a_im, log_dt, b_re, b_im, c_re, c_im, d_skip, w_glu, b_glu,
              w_up_attn, w_up_ssm, w_out, w_router, router_bias, w_gate, w_up, w_down, ws_gate, ws_up, ws_down)
    depth = w_ada.shape[0]
    for layer in range(depth):
        x = _layer(x, c, *[p[layer] for p in params])
    return x
```

```python
import functools
import math

import jax
import jax.numpy as jnp
import numpy as np
from jax import lax
from jax.experimental import pallas as pl
from jax.experimental.pallas import tpu as pltpu
from jax.experimental.pallas import tpu_sc as plsc

F32 = jnp.float32
BF16 = jnp.bfloat16
I32 = jnp.int32
HIGHEST = lax.Precision.HIGHEST

D_MODEL = 1024
BATCH = 4
SEQ = 4096
N_TOK = BATCH * SEQ
N_HEADS = 8
HEAD_DIM = 64
N_KV = 2
CMP_BLOCK = 32
CMP_STRIDE = 16
CMP_HIDDEN = 256
N_CMP = 256
SEL_BLOCK = 64
N_SEL_BLOCKS = SEQ // SEL_BLOCK
N_SELECT = 16
WINDOW = 512
ATTN_SCALE = HEAD_DIM ** -0.5
LOG2E = 1.4426950408889634
NSA_WIDTH = N_HEADS * HEAD_DIM
SSM_WIDTH = 512
GROUP = 16
N_GROUPS = SSM_WIDTH // GROUP
STATE = 64
N_EXPERTS = 256
TOP_K = 8
D_EXPERT = 256
N_EXPERT_GROUPS = 8
EXPERTS_PER_GROUP = N_EXPERTS // N_EXPERT_GROUPS
TOPK_GROUPS = 4
ROUTE_SCALE = 2.5
DISPATCH_BLOCK = 512
EPS = 1e-6
NEG = -1e30

LANES = 128
S5_T = 16
S5_SG = 4
S5_GL = N_GROUPS // S5_SG
S5_CH = N_TOK // S5_T
S5_CH_PER_BATCH = SEQ // S5_T
S5_NSTATE = S5_GL * STATE * 2

NK = N_TOK * TOP_K
HALF = D_MODEL // 2

VMEM_LIMIT = 48 * 1024 * 1024


def _cparams(*sem, vmem=VMEM_LIMIT):
    return pltpu.CompilerParams(dimension_semantics=tuple(sem), vmem_limit_bytes=vmem)


def _dot(a, b):
    return jnp.dot(a, b, preferred_element_type=F32)


def _dot_nt(a, b):
    return lax.dot_general(a, b, (((1,), (1,)), ((), ())), preferred_element_type=F32)


def _split_dot(v, w):
    hi = v.astype(BF16)
    lo = (v - hi.astype(F32)).astype(BF16)
    return _dot(hi, w) + _dot(lo, w)


def _seg_rms(v, bd, gain):
    ss = _split_dot(v * v, bd)
    return v * lax.rsqrt(ss * (1.0 / HEAD_DIM) + EPS) * gain


def _gelu(x):
    return 0.5 * x * (1.0 + jnp.tanh(0.7978845608028654 * (x + 0.044715 * (x * x * x))))


def _silu(x):
    return x * jax.nn.sigmoid(x)


def _pack_bf16_pairs(v):
    h = v.shape[1] // 2
    return pltpu.pack_elementwise([v[:, :h], v[:, h:]], packed_dtype=BF16)


def _unpack_bf16_pairs(word):
    return (pltpu.unpack_elementwise(word, index=0, packed_dtype=BF16, unpacked_dtype=F32),
            pltpu.unpack_elementwise(word, index=1, packed_dtype=BF16, unpacked_dtype=F32))


def _ada_kernel(c_ref, w_ref, b_ref, o_ref):
    c = c_ref[...]
    o_ref[...] = jnp.dot(_silu(c), w_ref[...], preferred_element_type=F32, precision=HIGHEST) + b_ref[...]


def _ada(c, w_ada, b_ada):
    cp = jnp.pad(c, ((0, 8 - BATCH), (0, 0)))
    tn = 1536
    out = pl.pallas_call(
        _ada_kernel,
        grid=(6 * D_MODEL // tn,),
        in_specs=[pl.BlockSpec((8, D_MODEL), lambda j: (0, 0)),
                  pl.BlockSpec((D_MODEL, tn), lambda j: (0, j)),
                  pl.BlockSpec((1, tn), lambda j: (0, j))],
        out_specs=pl.BlockSpec((8, tn), lambda j: (0, j)),
        out_shape=jax.ShapeDtypeStruct((8, 6 * D_MODEL), F32),
        compiler_params=_cparams("arbitrary"),
        name="ada",
    )(cp, w_ada, b_ada.reshape(1, -1))
    return out.reshape(8, 6, D_MODEL)


_C_Q = 0
_C_KC = 512
_C_VC = 640
_C_KS = 768
_C_KW = 1024
_C_GN = 1280
_C_U = 1408
_C_GA = 1920
_C_GS = 2944
_C_END = 3968
PROJ_TM = 512


def _proj_kernel(x_ref, mod_ref, g1_ref, w_ref, wvt_ref, qg_ref, ksg_ref, kwg_ref, bd512_ref, bd256_ref,
                 q_ref, kc_ref, vc_ref, ks_ref, kw_ref, vst_ref, vwt_ref, gn_ref, u_ref, ga_ref, gs_ref):
    x = x_ref[...]
    ms = jnp.mean(x * x, axis=-1, keepdims=True)
    mod = mod_ref[0]
    h = (x * lax.rsqrt(ms + EPS) * g1_ref[...]) * (1.0 + mod[1:2]) + mod[0:1]
    hb = h.astype(BF16)

    def p(lo, hi):
        return _dot(hb, w_ref[:, lo:hi])

    q_ref[...] = _seg_rms(p(_C_Q, _C_KC), bd512_ref[...], qg_ref[...] * (ATTN_SCALE * LOG2E)).astype(BF16)
    kc_ref[...] = p(_C_KC, _C_VC)
    vc_ref[...] = p(_C_VC, _C_KS)
    ks_ref[...] = _seg_rms(p(_C_KS, _C_KW), bd256_ref[...], ksg_ref[...]).astype(BF16)
    kw_ref[...] = _seg_rms(p(_C_KW, _C_GN), bd256_ref[...], kwg_ref[...]).astype(BF16)
    vt = _dot_nt(wvt_ref[...], hb)
    vst_ref[...] = vt[:LANES].astype(BF16)
    vwt_ref[...] = vt[LANES:].astype(BF16)
    gn_ref[...] = jax.nn.sigmoid(p(_C_GN, _C_U))
    u_ref[...] = p(_C_U, _C_GA)
    ga_ref[...] = jax.nn.sigmoid(p(_C_GA, _C_GS)).astype(BF16)
    gs_ref[...] = jax.nn.sigmoid(p(_C_GS, _C_END)).astype(BF16)


def _dup_cols(w):
    return jnp.concatenate([w[:, :64], w[:, :64], w[:, 64:], w[:, 64:]], axis=1)


def _block_ones(n):
    return jnp.kron(jnp.eye(n // HEAD_DIM, dtype=F32), jnp.ones((HEAD_DIM, HEAD_DIM), F32)).astype(BF16)


def _proj(x2, mod, g_norm1, w_in, q_gain, ks_gain, kw_gain):
    o = np.cumsum((0, 512, 128, 128, 128, 128, 128, 128, 24, 512, 1024, 1024))
    parts = [w_in[:, o[i]:o[i + 1]] for i in range(11)]
    wq, wkc, wvc, wks, wvs, wkw, wvw, wgn, wu, wga, wgs = parts
    w = jnp.concatenate([wq, wkc, wvc, _dup_cols(wks), _dup_cols(wkw),
                         jnp.pad(wgn, ((0, 0), (0, LANES - 24))), wu, wga, wgs], axis=1).astype(BF16)
    wvt = jnp.concatenate([wvs, wvw], axis=1).T.astype(BF16)
    tm = PROJ_TM
    row = lambda i: (i, 0)
    col = lambda i: (0, i)
    fix = lambda i: (0, 0)
    outs = [(512, BF16, row), (128, F32, row), (128, F32, row), (256, BF16, row), (256, BF16, row),
            (LANES, BF16, col), (LANES, BF16, col),
            (128, F32, row), (512, F32, row), (1024, BF16, row), (1024, BF16, row)]
    ospec = lambda wd, m: pl.BlockSpec((tm, wd), m) if m is row else pl.BlockSpec((wd, tm), m)
    oshape = lambda wd, dt, m: jax.ShapeDtypeStruct((N_TOK, wd) if m is row else (wd, N_TOK), dt)
    return pl.pallas_call(
        _proj_kernel,
        grid=(N_TOK // tm,),
        in_specs=[pl.BlockSpec((tm, D_MODEL), row),
                  pl.BlockSpec((1, 6, D_MODEL), lambda i: (i // (SEQ // tm), 0, 0)),
                  pl.BlockSpec((1, D_MODEL), fix),
                  pl.BlockSpec((D_MODEL, _C_END), fix),
                  pl.BlockSpec((2 * LANES, D_MODEL), fix),
                  pl.BlockSpec((1, 512), fix), pl.BlockSpec((1, 256), fix), pl.BlockSpec((1, 256), fix),
                  pl.BlockSpec((512, 512), fix), pl.BlockSpec((256, 256), fix)],
        out_specs=[ospec(wd, m) for wd, _, m in outs],
        out_shape=[oshape(wd, dt, m) for wd, dt, m in outs],
        compiler_params=_cparams("arbitrary"),
        name="proj",
    )(x2, mod, g_norm1.reshape(1, -1), w, wvt,
      jnp.tile(q_gain, N_HEADS).reshape(1, -1), jnp.tile(ks_gain, 4).reshape(1, -1),
      jnp.tile(kw_gain, 4).reshape(1, -1), _block_ones(512), _block_ones(256))


def _compress_kernel(r_ref, pe_ref, w1_ref, w2_ref, bd_ref, gain_ref, o_ref, *, do_norm):
    r = jnp.concatenate([r_ref[0, :, l, :] for l in range(CMP_STRIDE)], axis=1)
    p0 = _dot((r + pe_ref[0]).astype(BF16), w1_ref[0])
    p1 = _dot((r + pe_ref[1]).astype(BF16), w1_ref[1])
    hid = p0 + pltpu.roll(p1, N_CMP - 1, 0)
    c = _dot(_gelu(hid).astype(BF16), w2_ref[...])
    if do_norm:
        c = _seg_rms(c, bd_ref[...], gain_ref[...])
    o_ref[0] = c.astype(BF16)


def _compress(raw, pe, w1, w2, gain, do_norm):
    r = raw.reshape(BATCH, SEQ // CMP_STRIDE, CMP_STRIDE, LANES)
    eye = jnp.eye(N_KV, dtype=F32)
    w1r = w1.reshape(2, CMP_STRIDE, HEAD_DIM, CMP_HIDDEN)
    w1big = jnp.einsum('hldc,gk->hlgdkc', w1r, eye).reshape(2, CMP_STRIDE * LANES, N_KV * CMP_HIDDEN).astype(BF16)
    w2big = jnp.einsum('cd,gk->gckd', w2, eye)
    w2big = jnp.concatenate([w2big, w2big], axis=-1).reshape(N_KV * CMP_HIDDEN, 4 * HEAD_DIM).astype(BF16)
    pe_big = jnp.broadcast_to(pe.reshape(2, CMP_STRIDE, 1, HEAD_DIM), (2, CMP_STRIDE, N_KV, HEAD_DIM))
    pe_big = pe_big.reshape(2, 1, CMP_STRIDE * LANES)
    fix2 = lambda b: (0, 0)
    fix3 = lambda b: (0, 0, 0)
    return pl.pallas_call(
        functools.partial(_compress_kernel, do_norm=do_norm),
        grid=(BATCH,),
        in_specs=[pl.BlockSpec((1, N_CMP, CMP_STRIDE, LANES), lambda b: (b, 0, 0, 0)),
                  pl.BlockSpec((2, 1, CMP_STRIDE * LANES), fix3),
                  pl.BlockSpec((2, CMP_STRIDE * LANES, N_KV * CMP_HIDDEN), fix3),
                  pl.BlockSpec((N_KV * CMP_HIDDEN, 256), fix2),
                  pl.BlockSpec((256, 256), fix2), pl.BlockSpec((1, 256), fix2)],
        out_specs=pl.BlockSpec((1, N_CMP, 256), lambda b: (b, 0, 0)),
        out_shape=jax.ShapeDtypeStruct((BATCH, N_CMP, 256), BF16),
        compiler_params=_cparams("arbitrary"),
        name="compress_k" if do_norm else "compress_v",
    )(r, pe_big, w1big, w2big, _block_ones(256), jnp.tile(gain, 4).reshape(1, -1))


ATT_TQ = 256
RANK_CHUNK = 16


def _head_variants(qb):
    lane = lax.broadcasted_iota(I32, qb.shape, 1)
    z = jnp.zeros_like(qb)
    return jnp.where(lane < HEAD_DIM, qb, z), jnp.where(lane < HEAD_DIM, z, qb)


def _cmp_kernel(q_ref, kc_ref, vc_ref, ov_ref, o_ref, sel_ref, vrank_ref):
    tq = ATT_TQ
    qi = pl.program_id(1)
    tpos = qi * tq + lax.broadcasted_iota(I32, (tq, N_CMP), 0)
    nidx = lax.broadcasted_iota(I32, (tq, N_CMP), 1)
    mask = (CMP_STRIDE * nidx + (CMP_BLOCK - 1)) <= tpos
    lane_lo = lax.broadcasted_iota(I32, (tq, LANES), 1) < HEAD_DIM
    for g in range(N_KV):
        kd = kc_ref[0, :, g * LANES:(g + 1) * LANES]
        vd = vc_ref[0, :, g * LANES:(g + 1) * LANES]
        psum = jnp.zeros((tq, N_CMP), F32)
        for jb in range(2):
            blk = 2 * g + jb
            pv = []
            for qv in _head_variants(q_ref[:, blk * LANES:(blk + 1) * LANES]):
                s = jnp.where(mask, _dot_nt(qv, kd), NEG)
                m = jnp.max(s, axis=-1, keepdims=True)
                e = jnp.where(mask, jnp.exp2(s - m), 0.0)
                l = jnp.sum(e, axis=-1, keepdims=True)
                p = e / jnp.where(l > 0.0, l, 1.0)
                psum = psum + p
                pv.append(_dot(p.astype(BF16), vd))
            o_ref[:, blk * LANES:(blk + 1) * LANES] = jnp.where(lane_lo, pv[0], pv[1]).astype(BF16)
        imp = _split_dot(psum, ov_ref[...])
        imp_t = imp.T[:N_SEL_BLOCKS]
        j = lax.broadcasted_iota(I32, (N_SEL_BLOCKS, tq), 0)
        cur = jnp.right_shift(qi * tq + lax.broadcasted_iota(I32, (N_SEL_BLOCKS, tq), 1), 6)
        forced = (j == 0) | (j == cur) | (j == cur - 1)
        v = jnp.where(forced, jnp.inf, jnp.where(j <= cur, imp_t, -jnp.inf))
        vrank_ref[...] = jnp.zeros((N_SEL_BLOCKS, tq), F32)
        n_live = (qi + 1) * (tq // SEL_BLOCK)
        for c0 in range(0, N_SEL_BLOCKS, RANK_CHUNK):
            @pl.when(c0 < n_live)
            def _():
                rank = vrank_ref[...]
                for jp in range(c0, c0 + RANK_CHUNK):
                    row = v[jp:jp + 1, :]
                    tie = jnp.where(j > jp, 1.0, 0.0)
                    rank = rank + jnp.where(row > v, 1.0, jnp.where(row == v, tie, 0.0))
                vrank_ref[...] = rank
        rank = vrank_ref[...]
        sel_ref[g * N_SEL_BLOCKS:(g + 1) * N_SEL_BLOCKS, :] = jnp.where(rank < float(N_SELECT), 0.0, NEG)


def _cmp_attn(q, kcn, vcn):
    nc = np.arange(N_CMP)
    sb = np.arange(LANES)
    ov = ((CMP_STRIDE * nc[:, None] < SEL_BLOCK * sb[None, :] + SEL_BLOCK)
          & (CMP_STRIDE * nc[:, None] + CMP_BLOCK > SEL_BLOCK * sb[None, :])
          & (nc[:, None] < N_CMP - 1) & (sb[None, :] < N_SEL_BLOCKS))
    ov = jnp.asarray(ov, BF16)
    tq = ATT_TQ
    nq = SEQ // tq
    row = lambda b, i: (b * nq + i, 0)
    return pl.pallas_call(
        _cmp_kernel,
        grid=(BATCH, nq),
        in_specs=[pl.BlockSpec((tq, NSA_WIDTH), row),
                  pl.BlockSpec((1, N_CMP, 256), lambda b, i: (b, 0, 0)),
                  pl.BlockSpec((1, N_CMP, 256), lambda b, i: (b, 0, 0)),
                  pl.BlockSpec((N_CMP, LANES), lambda b, i: (0, 0))],
        out_specs=[pl.BlockSpec((tq, NSA_WIDTH), row),
                   pl.BlockSpec((N_KV * N_SEL_BLOCKS, tq), lambda b, i: (0, b * nq + i))],
        out_shape=[jax.ShapeDtypeStruct((N_TOK, NSA_WIDTH), BF16),
                   jax.ShapeDtypeStruct((N_KV * N_SEL_BLOCKS, N_TOK), F32)],
        scratch_shapes=[pltpu.VMEM((N_SEL_BLOCKS, tq), F32)],
        compiler_params=_cparams("arbitrary", "arbitrary"),
        name="cmp_attn",
    )(q, kcn, vcn, ov)


ATT_TK = 256


M_INIT = -1e29


SUM_ROWS = 16


def _selwin_kernel(q_ref, ks_ref, kw_ref, vst_ref, vwt_ref, selb_ref, osel_ref, owin_ref, m_ref, acc_ref):
    tq, tk = ATT_TQ, ATT_TK
    qi = pl.program_id(1)
    krow = lax.broadcasted_iota(I32, (tk, tq), 0)
    qcol = lax.broadcasted_iota(I32, (tk, tq), 1)
    causal_bias = jnp.where(krow <= qcol, 0.0, NEG)
    far_bias = jnp.where(qcol < krow, 0.0, NEG)

    ones_rows = jnp.ones((SUM_ROWS, tk), BF16)

    def reset():
        m_ref[...] = jnp.full(m_ref.shape, M_INIT, F32)
        acc_ref[...] = jnp.zeros(acc_ref.shape, F32)

    def update(g, k_ref, vt_ref, kt, bias):
        k0 = pl.multiple_of(kt * tk, tk)
        kd = k_ref[0, pl.ds(k0, tk), g * LANES:(g + 1) * LANES]
        vt = vt_ref[g * HEAD_DIM:(g + 1) * HEAD_DIM, pl.ds(k0, tk)]
        s = _dot_nt(kd, qvars[g])
        if bias is not None:
            s = s + jnp.concatenate([bias] * 4, axis=1)
        m_old = m_ref[g]
        m_new = jnp.maximum(m_old, jnp.max(s, axis=0, keepdims=True))
        alpha = jnp.exp2(m_old - m_new)
        p = jnp.exp2(s - m_new)
        m_ref[g] = m_new
        vte = jnp.concatenate([vt, ones_rows], axis=0)
        acc_ref[g] = alpha * acc_ref[g] + _dot(vte, p.astype(BF16))

    def finish(out_ref, g):
        o = acc_ref[g, :HEAD_DIM, :] / acc_ref[g, HEAD_DIM:HEAD_DIM + 1, :]
        for jb in range(2):
            blk = 2 * g + jb
            pair = jnp.concatenate([o[:, 2 * jb * tq:(2 * jb + 1) * tq], o[:, (2 * jb + 1) * tq:(2 * jb + 2) * tq]],
                                   axis=0)
            out_ref[:, blk * LANES:(blk + 1) * LANES] = pair.T.astype(BF16)

    def sel_bias(g, kt):
        rows = [jnp.broadcast_to(selb_ref[pl.ds(g * N_SEL_BLOCKS + kt * (tk // SEL_BLOCK) + r, 1), :],
                                 (SEL_BLOCK, tq)) for r in range(tk // SEL_BLOCK)]
        return jnp.concatenate(rows, axis=0)

    qvars = []
    for g in range(N_KV):
        heads = []
        for jb in range(2):
            heads.extend(_head_variants(q_ref[:, (2 * g + jb) * LANES:(2 * g + jb + 1) * LANES]))
        qvars.append(jnp.concatenate(heads, axis=0))
    groups = range(N_KV)

    reset()

    def sel_step(kt, carry):
        for g in groups:
            update(g, ks_ref, vst_ref, kt, sel_bias(g, kt))
        return carry

    lax.fori_loop(0, qi, sel_step, 0)
    for g in groups:
        update(g, ks_ref, vst_ref, qi, sel_bias(g, qi) + causal_bias)
    for g in groups:
        finish(osel_ref, g)

    reset()

    @pl.when(qi >= 2)
    def _():
        for g in groups:
            update(g, kw_ref, vwt_ref, qi - 2, far_bias)

    @pl.when(qi >= 1)
    def _():
        for g in groups:
            update(g, kw_ref, vwt_ref, qi - 1, None)

    for g in groups:
        update(g, kw_ref, vwt_ref, qi, causal_bias)
    for g in groups:
        finish(owin_ref, g)


def _selwin(q, ks, kw, vst, vwt, selb):
    tq = ATT_TQ
    nq = SEQ // tq
    assert WINDOW == 2 * ATT_TK and ATT_TQ == ATT_TK
    row = lambda b, i: (b * nq + i, 0)
    keys = pl.BlockSpec((1, SEQ, 256), lambda b, i: (b, 0, 0))
    vals = pl.BlockSpec((LANES, SEQ), lambda b, i: (0, b))
    r3 = lambda a: a.reshape(BATCH, SEQ, 256)
    return pl.pallas_call(
        _selwin_kernel,
        grid=(BATCH, nq),
        in_specs=[pl.BlockSpec((tq, NSA_WIDTH), row), keys, keys, vals, vals,
                  pl.BlockSpec((N_KV * N_SEL_BLOCKS, tq), lambda b, i: (0, b * nq + i))],
        out_specs=[pl.BlockSpec((tq, NSA_WIDTH), row)] * 2,
        out_shape=[jax.ShapeDtypeStruct((N_TOK, NSA_WIDTH), BF16)] * 2,
        scratch_shapes=[pltpu.VMEM((N_KV, 1, 4 * tq), F32),
                        pltpu.VMEM((N_KV, HEAD_DIM + SUM_ROWS, 4 * tq), F32)],
        compiler_params=_cparams("arbitrary", "arbitrary"),
        name="selwin",
    )(q, r3(ks), r3(kw), vst, vwt, selb)


def _s5_param_kernel(are_ref, aim_ref, ldt_ref, cre_ref, cim_ref, bre_ref, bim_ref,
                     clre_ref, clim_ref, wbre_ref, wbim_ref, bbre_ref, bbim_ref, ltre_ref, ltim_ref):
    are, aim = are_ref[...], aim_ref[...]
    dt = jnp.exp(ldt_ref[...])
    cre, cim = cre_ref[...], cim_ref[...]

    def lam_pow(tau):
        mag = jnp.exp(are * dt * float(tau))
        ang = aim * dt * float(tau)
        return mag * jnp.cos(ang), mag * jnp.sin(ang)

    lre, lim = lam_pow(1)
    den = are * are + aim * aim
    qre = ((lre - 1.0) * are + lim * aim) / den
    qim = (lim * are - (lre - 1.0) * aim) / den
    bre, bim = bre_ref[...], bim_ref[...]
    bbre = qre * bre - qim * bim
    bbim = qre * bim + qim * bre
    bbre_ref[...] = bbre
    bbim_ref[...] = bbim
    for tau in range(S5_T + 1):
        pr, pi = lam_pow(tau)
        clre_ref[tau] = cre * pr - cim * pi
        clim_ref[tau] = cre * pi + cim * pr
        if tau < S5_T:
            k = S5_T - 1 - tau
            wbre_ref[k] = pr * bbre - pi * bbim
            wbim_ref[k] = pr * bbim + pi * bbre
        else:
            ltre_ref[...] = pr
            ltim_ref[...] = pi


def _s5_kmat_kernel(l_ref, r_ref, o_ref):
    o_ref[0] = jnp.dot(l_ref[0], r_ref[0], preferred_element_type=F32, precision=HIGHEST)


def _s5_params(a_re, a_im, log_dt, b_re, b_im, c_re, c_im):
    T = S5_T
    pn = GROUP * STATE
    tile_p = lambda a: jnp.tile(a, (1, GROUP))
    args = (tile_p(a_re), tile_p(a_im), jnp.broadcast_to(log_dt[:, None], (N_GROUPS, pn)),
            c_re.reshape(N_GROUPS, pn), c_im.reshape(N_GROUPS, pn),
            jnp.swapaxes(b_re, 1, 2).reshape(N_GROUPS, pn), jnp.swapaxes(b_im, 1, 2).reshape(N_GROUPS, pn))
    full2 = pl.BlockSpec((N_GROUPS, pn), lambda: (0, 0))
    clre, clim, wbre, wbim, bbre, bbim, ltre, ltim = pl.pallas_call(
        _s5_param_kernel,
        in_specs=[full2] * 7,
        out_specs=[pl.BlockSpec((T + 1, N_GROUPS, pn), lambda: (0, 0, 0))] * 2
                  + [pl.BlockSpec((T, N_GROUPS, pn), lambda: (0, 0, 0))] * 2 + [full2] * 4,
        out_shape=[jax.ShapeDtypeStruct((T + 1, N_GROUPS, pn), F32)] * 2
                  + [jax.ShapeDtypeStruct((T, N_GROUPS, pn), F32)] * 2
                  + [jax.ShapeDtypeStruct((N_GROUPS, pn), F32)] * 4,
        name="s5_params",
    )(*args)

    r5 = lambda a, t: a[:t].reshape(t, N_GROUPS, GROUP, STATE)
    lhs = jnp.concatenate([r5(clre, T), -r5(clim, T)], axis=-1)
    lhs = jnp.transpose(lhs, (1, 0, 2, 3)).reshape(N_GROUPS, T * GROUP, 2 * STATE)
    bb = lambda a: jnp.swapaxes(a.reshape(N_GROUPS, GROUP, STATE), 1, 2)
    rhs = jnp.concatenate([bb(bbre), bb(bbim)], axis=1)
    kmat = pl.pallas_call(
        _s5_kmat_kernel,
        grid=(N_GROUPS,),
        in_specs=[pl.BlockSpec((1, T * GROUP, 2 * STATE), lambda g: (g, 0, 0)),
                  pl.BlockSpec((1, 2 * STATE, GROUP), lambda g: (g, 0, 0))],
        out_specs=pl.BlockSpec((1, T * GROUP, GROUP), lambda g: (g, 0, 0)),
        out_shape=jax.ShapeDtypeStruct((N_GROUPS, T * GROUP, GROUP), F32),
        compiler_params=_cparams("arbitrary"),
        name="s5_kmat",
    )(lhs, rhs)

    eye = jnp.eye(S5_GL, dtype=F32)
    kt = kmat.reshape(S5_SG, S5_GL, T, GROUP, GROUP)
    kbd = jnp.einsum('sgtpq,gh->stgqhp', kt, eye).reshape(S5_SG, T, LANES, LANES)
    krev = kbd[:, ::-1].reshape(S5_SG, T * LANES, LANES).astype(BF16)
    krev = jnp.pad(krev, ((0, 0), (0, LANES), (0, 0)))
    r6 = lambda a: a.reshape(T, S5_SG, S5_GL, GROUP, STATE)
    wb = jnp.stack([r6(wbre), r6(wbim)], axis=-2)
    wb = jnp.einsum('ksgpin,gh->skgpihn', wb, eye).reshape(S5_SG, T * LANES, S5_NSTATE).astype(BF16)
    wc = jnp.stack([r6(clre[1:]), -r6(clim[1:])], axis=-2)
    wc = jnp.einsum('tsgpin,gh->signthp', wc, eye).reshape(S5_SG, S5_NSTATE, T * LANES).astype(BF16)
    lt = lambda a: a.reshape(N_GROUPS, GROUP, STATE)[:, 0].reshape(S5_SG, 1, S5_GL * STATE)
    return krev, wb, wc, lt(ltre), lt(ltim)


S5_TC = 256


def _s5_lane_block(sg):
    return pl.ds(pl.multiple_of(sg * LANES, LANES), LANES)


def _s5_chunk_inputs(x_ref, sg):
    return jnp.concatenate([x_ref[:, t, _s5_lane_block(sg)] for t in range(S5_T)], axis=1).astype(BF16)


def _s5_state_kernel(x_ref, wb_ref, e_ref):
    e_ref[0] = _dot(_s5_chunk_inputs(x_ref, pl.program_id(1)), wb_ref[0])


def _s5_scan_kernel(e_ref, ltre_ref, ltim_ref, xs_ref):
    lr, li = ltre_ref[0], ltim_ref[0]
    half = S5_NSTATE // 2

    def step(c, carry):
        new = []
        for b in range(BATCH):
            xr, xi = carry[b]
            row = b * S5_CH_PER_BATCH + c
            xs_ref[0, pl.ds(row, 1), :half] = xr
            xs_ref[0, pl.ds(row, 1), half:] = xi
            e = e_ref[0, pl.ds(row, 1), :]
            new.append((lr * xr - li * xi + e[:, :half], lr * xi + li * xr + e[:, half:]))
        return tuple(new)

    zero = jnp.zeros((1, half), F32)
    lax.fori_loop(0, S5_CH_PER_BATCH, step, tuple((zero, zero) for _ in range(BATCH)))


def _s5_out_kernel(x_ref, xs_ref, krev_ref, wc_ref, y_ref):
    sg = pl.program_id(1)
    x = _s5_chunk_inputs(x_ref, sg)
    xsb = xs_ref[0].astype(BF16)
    for t in range(0, S5_T, 2):
        n_in = (t + 2) * LANES
        first = (S5_T - 1 - t) * LANES
        taps = jnp.concatenate([krev_ref[0, first:first + n_in, :], krev_ref[0, first - LANES:first - LANES + n_in, :]],
                               axis=1)
        pair = _dot(x[:, :n_in], taps) + _dot(xsb, wc_ref[0, :, t * LANES:(t + 2) * LANES])
        y_ref[:, t, _s5_lane_block(sg)] = pair[:, :LANES]
        y_ref[:, t + 1, _s5_lane_block(sg)] = pair[:, LANES:]


def _s5(u, krev, wb, wc, ltre, ltim):
    T, tc = S5_T, S5_TC
    xn = u.reshape(S5_CH, T, SSM_WIDTH)
    grid = (S5_CH // tc, S5_SG)
    natural = pl.BlockSpec((tc, T, SSM_WIDTH), lambda i, s: (i, 0, 0))
    rows = lambda i, s: (s, i, 0)
    per_sg = lambda i, s: (s, 0, 0)
    e = pl.pallas_call(
        _s5_state_kernel, grid=grid,
        in_specs=[natural, pl.BlockSpec((1, T * LANES, S5_NSTATE), per_sg)],
        out_specs=pl.BlockSpec((1, tc, S5_NSTATE), rows),
        out_shape=jax.ShapeDtypeStruct((S5_SG, S5_CH, S5_NSTATE), F32),
        compiler_params=_cparams("arbitrary", "arbitrary"), name="s5_state",
    )(xn, wb)
    sg1 = lambda s: (s, 0, 0)
    xstart = pl.pallas_call(
        _s5_scan_kernel, grid=(S5_SG,),
        in_specs=[pl.BlockSpec((1, S5_CH, S5_NSTATE), sg1),
                  pl.BlockSpec((1, 1, S5_NSTATE // 2), sg1), pl.BlockSpec((1, 1, S5_NSTATE // 2), sg1)],
        out_specs=pl.BlockSpec((1, S5_CH, S5_NSTATE), sg1),
        out_shape=jax.ShapeDtypeStruct((S5_SG, S5_CH, S5_NSTATE), F32),
        compiler_params=_cparams("arbitrary"), name="s5_scan",
    )(e, ltre, ltim)
    y = pl.pallas_call(
        _s5_out_kernel, grid=grid,
        in_specs=[natural, pl.BlockSpec((1, tc, S5_NSTATE), rows),
                  pl.BlockSpec((1, (T + 1) * LANES, LANES), per_sg), pl.BlockSpec((1, S5_NSTATE, T * LANES), per_sg)],
        out_specs=natural,
        out_shape=jax.ShapeDtypeStruct((S5_CH, T, SSM_WIDTH), F32),
        compiler_params=_cparams("arbitrary", "arbitrary", vmem=56 * 1024 * 1024), name="s5_out",
    )(xn, xstart, krev, wc)
    return y.reshape(N_TOK, SSM_WIDTH)


MERGE_TM = 512


def _merge_kernel(ocmp_ref, osel_ref, owin_ref, gn_ref, yssm_ref, u_ref, ga_ref, gs_ref, x_ref, mod_ref,
                  eg_ref, dskip_ref, wglu_ref, bglu_ref, wua_ref, wus_ref, wout_ref, g2_ref,
                  wrhi_ref, wrlo_ref, wsgu_ref, wsd_ref,
                  xpart_ref, h2_ref, logit_ref):
    mod = mod_ref[0]
    gnb = gn_ref[...].astype(BF16)
    o_nsa = (_dot(gnb, eg_ref[0]) * ocmp_ref[...].astype(F32)
             + _dot(gnb, eg_ref[1]) * osel_ref[...].astype(F32)
             + _dot(gnb, eg_ref[2]) * owin_ref[...].astype(F32))
    attn = _dot(o_nsa.astype(BF16), wua_ref[...])
    z = _gelu(yssm_ref[...] + dskip_ref[...] * u_ref[...])
    y_ssm = z * jax.nn.sigmoid(_dot(z.astype(BF16), wglu_ref[...]) + bglu_ref[...])
    ssm = _dot(y_ssm.astype(BF16), wus_ref[...])
    merged = ga_ref[...].astype(F32) * attn + gs_ref[...].astype(F32) * ssm
    x1 = x_ref[...] + mod[2:3] * _dot(merged.astype(BF16), wout_ref[...])

    ms = jnp.mean(x1 * x1, axis=-1, keepdims=True)
    h2 = (x1 * lax.rsqrt(ms + EPS) * g2_ref[...]) * (1.0 + mod[4:5]) + mod[3:4]
    hi = h2.astype(BF16)
    lo = (h2 - hi.astype(F32)).astype(BF16)
    h2_ref[...] = _pack_bf16_pairs(h2)
    logit_ref[...] = _dot_nt(wrhi_ref[...], hi) + _dot_nt(wrhi_ref[...], lo) + _dot_nt(wrlo_ref[...], hi)
    gu = _dot(hi, wsgu_ref[...])
    shared = _dot((_silu(gu[:, :D_EXPERT]) * gu[:, D_EXPERT:]).astype(BF16), wsd_ref[...])
    xpart_ref[...] = x1 + mod[5:6] * shared


def _merge(ocmp, osel, owin, gn, yssm, u, ga, gs, x2, mod, d_skip, w_glu, b_glu, w_up_attn, w_up_ssm, w_out,
           g_norm2, w_router, ws_gate, ws_up, ws_down):
    tm = MERGE_TM
    eg = np.zeros((3, LANES, NSA_WIDTH), np.float32)
    for j in range(3):
        for h in range(N_HEADS):
            eg[j, 3 * h + j, h * HEAD_DIM:(h + 1) * HEAD_DIM] = 1.0
    wr_t = w_router.T
    wr_hi = wr_t.astype(BF16)
    wr_lo = (wr_t - wr_hi.astype(F32)).astype(BF16)
    row = lambda i: (i, 0)
    fix2 = lambda i: (0, 0)
    wspec = lambda a: pl.BlockSpec(a.shape, (lambda i: (0,) * a.ndim))
    weights = [jnp.asarray(eg, BF16), d_skip.reshape(1, -1), w_glu.astype(BF16), b_glu.reshape(1, -1),
               w_up_attn.astype(BF16), w_up_ssm.astype(BF16), w_out.astype(BF16), g_norm2.reshape(1, -1),
               wr_hi, wr_lo, jnp.concatenate([ws_gate, ws_up], axis=1).astype(BF16), ws_down.astype(BF16)]
    acts = [(ocmp, 512), (osel, 512), (owin, 512), (gn, 128), (yssm, 512), (u, 512), (ga, 1024), (gs, 1024),
            (x2, 1024)]
    return pl.pallas_call(
        _merge_kernel,
        grid=(N_TOK // tm,),
        in_specs=[pl.BlockSpec((tm, wd), row) for _, wd in acts]
                 + [pl.BlockSpec((1, 6, D_MODEL), lambda i: (i // (SEQ // tm), 0, 0))]
                 + [wspec(w) for w in weights],
        out_specs=[pl.BlockSpec((tm, D_MODEL), row), pl.BlockSpec((tm, HALF), row),
                   pl.BlockSpec((N_EXPERTS, tm), lambda i: (0, i))],
        out_shape=[jax.ShapeDtypeStruct((N_TOK, D_MODEL), F32), jax.ShapeDtypeStruct((N_TOK, HALF), I32),
                   jax.ShapeDtypeStruct((N_EXPERTS, N_TOK), F32)],
        compiler_params=_cparams("arbitrary", vmem=56 * 1024 * 1024),
        name="merge",
    )(*[a for a, _ in acts], mod, *weights)


ROUTE_TN = 512


def _route_kernel(logit_ref, bias_ref, eidx_ref, w_ref, count_ref, gscore_ref, masked_ref):
    tn = ROUTE_TN

    @pl.when(pl.program_id(0) == 0)
    def _():
        count_ref[...] = jnp.zeros(count_ref.shape, F32)

    sc = jax.nn.sigmoid(logit_ref[...])
    biased = sc + bias_ref[...]
    gi = lax.broadcasted_iota(I32, (EXPERTS_PER_GROUP, tn), 0).astype(F32)
    for g in range(N_EXPERT_GROUPS):
        blk = biased[g * EXPERTS_PER_GROUP:(g + 1) * EXPERTS_PER_GROUP]
        m1 = jnp.max(blk, axis=0, keepdims=True)
        i1 = jnp.min(jnp.where(blk == m1, gi, float(EXPERTS_PER_GROUP)), axis=0, keepdims=True)
        m2 = jnp.max(jnp.where(gi == i1, -jnp.inf, blk), axis=0, keepdims=True)
        gscore_ref[g:g + 1, :] = m1 + m2
    gs = gscore_ref[...]
    gidx = lax.broadcasted_iota(I32, (N_EXPERT_GROUPS, tn), 0)
    grank = jnp.zeros((N_EXPERT_GROUPS, tn), F32)
    for gp in range(N_EXPERT_GROUPS):
        row = gs[gp:gp + 1, :]
        tie = jnp.where(gidx > gp, 1.0, 0.0)
        grank = grank + jnp.where(row > gs, 1.0, jnp.where(row == gs, tie, 0.0))
    for g in range(N_EXPERT_GROUPS):
        keep = grank[g:g + 1, :] < float(TOPK_GROUPS)
        sl = slice(g * EXPERTS_PER_GROUP, (g + 1) * EXPERTS_PER_GROUP)
        masked_ref[sl, :] = jnp.where(keep, biased[sl], -jnp.inf)
    cur = masked_ref[...]
    eidx = lax.broadcasted_iota(I32, (N_EXPERTS, tn), 0).astype(F32)
    wsum = jnp.zeros((1, tn), F32)
    hits = jnp.zeros((N_EXPERTS, tn), F32)
    for k in range(TOP_K):
        m = jnp.max(cur, axis=0, keepdims=True)
        idx = jnp.min(jnp.where(cur == m, eidx, float(N_EXPERTS)), axis=0, keepdims=True)
        hit = eidx == idx
        wk = jnp.sum(jnp.where(hit, sc, 0.0), axis=0, keepdims=True)
        cur = jnp.where(hit, -jnp.inf, cur)
        hits = hits + jnp.where(hit, 1.0, 0.0)
        eidx_ref[k:k + 1, :] = idx.astype(I32)
        w_ref[k:k + 1, :] = wk
        wsum = wsum + wk
    w_ref[...] = w_ref[...] / wsum * ROUTE_SCALE
    count_ref[...] = count_ref[...] + jnp.sum(hits, axis=1, keepdims=True)


def _route(logits_t, router_bias):
    tn = ROUTE_TN
    return pl.pallas_call(
        _route_kernel,
        grid=(N_TOK // tn,),
        in_specs=[pl.BlockSpec((N_EXPERTS, tn), lambda i: (0, i)), pl.BlockSpec((N_EXPERTS, 1), lambda i: (0, 0))],
        out_specs=[pl.BlockSpec((TOP_K, tn), lambda i: (0, i))] * 2 + [pl.BlockSpec((N_EXPERTS, 1), lambda i: (0, 0))],
        out_shape=[jax.ShapeDtypeStruct((TOP_K, N_TOK), I32), jax.ShapeDtypeStruct((TOP_K, N_TOK), F32),
                   jax.ShapeDtypeStruct((N_EXPERTS, 1), F32)],
        scratch_shapes=[pltpu.VMEM((N_EXPERT_GROUPS, tn), F32), pltpu.VMEM((N_EXPERTS, tn), F32)],
        compiler_params=_cparams("arbitrary"),
        name="route",
    )(logits_t, router_bias.reshape(-1, 1))


N_MOE_BLK = NK // DISPATCH_BLOCK
N_ITEMS = N_MOE_BLK + N_EXPERTS
ASSIGN_BITS = 17


def _dispatch_plan(eidx, counts):
    e_flat = eidx.reshape(-1)
    key = jnp.sort(e_flat * NK + jnp.arange(NK, dtype=I32))
    order = key & (NK - 1)
    counts = counts.reshape(-1).astype(I32)
    start = jnp.cumsum(counts) - counts
    cuts = jnp.sort(jnp.concatenate([jnp.arange(N_MOE_BLK, dtype=I32) * DISPATCH_BLOCK, start]))
    lo = cuts
    hi = jnp.concatenate([cuts[1:], jnp.full((1,), NK, I32)])
    blk = jnp.minimum(lo // DISPATCH_BLOCK, N_MOE_BLK - 1)
    expert = jnp.clip(jnp.sum((start[None, :] <= lo[:, None]).astype(I32), axis=1) - 1, 0, N_EXPERTS - 1)
    one = jnp.ones((1,), I32)
    first = jnp.concatenate([one, (blk[1:] != blk[:-1]).astype(I32)])
    last = jnp.concatenate([(blk[1:] != blk[:-1]).astype(I32), one])
    new_expert = jnp.concatenate([one, (expert[1:] != expert[:-1]).astype(I32)])
    run_id = jnp.cumsum(new_expert) - 1
    n_runs = run_id[-1] + 1
    item = jnp.arange(N_ITEMS, dtype=I32)
    run_first_item = jnp.sort(jnp.where(new_expert == 1, item, N_ITEMS))
    run_expert = expert[jnp.minimum(run_first_item, N_ITEMS - 1)]
    ahead = run_id + (WEIGHT_RING - 1)
    ahead_expert = run_expert[jnp.minimum(ahead, N_ITEMS - 1)]
    ahead_valid = (ahead < n_runs).astype(I32)
    second_expert = run_expert[1:2]
    prologue = jnp.concatenate([second_expert, (n_runs > 1).astype(I32).reshape(1)])
    tok = jnp.right_shift(order, 3)
    home = (order & (TOP_K - 1)) * N_TOK + tok
    return tok, home, (blk, expert, lo - blk * DISPATCH_BLOCK, hi - blk * DISPATCH_BLOCK, first, last, new_expert,
                      run_id % WEIGHT_RING, ahead_expert, ahead_valid, prologue)


SC_CORES = 2
SC_SUBCORES = 16
SC_CHUNK = 128


def _sc_move_rows(table, idx, scatter):
    n = idx.shape[0]
    workers = SC_CORES * SC_SUBCORES
    per_worker = n // workers
    n_chunks = per_worker // SC_CHUNK
    assert per_worker * workers == n and n_chunks * SC_CHUNK == per_worker
    mesh = plsc.VectorSubcoreMesh(core_axis_name="c", subcore_axis_name="s",
                                  num_cores=SC_CORES, num_subcores=SC_SUBCORES)

    def body(table_hbm, idx_hbm, out_hbm, idx_v, rows_v, sem):
        wid = lax.axis_index("s") * SC_CORES + lax.axis_index("c")
        base = wid * per_worker

        @pl.loop(0, n_chunks)
        def _(j):
            off = base + j * SC_CHUNK
            pltpu.sync_copy(idx_hbm.at[pl.ds(off, SC_CHUNK)], idx_v)
            if scatter:
                pltpu.sync_copy(table_hbm.at[pl.ds(off, SC_CHUNK)], rows_v)
                pltpu.async_copy(rows_v, out_hbm.at[idx_v], sem).wait()
            else:
                pltpu.async_copy(table_hbm.at[idx_v], rows_v, sem).wait()
                pltpu.sync_copy(rows_v, out_hbm.at[pl.ds(off, SC_CHUNK)])

    return pl.kernel(
        body,
        out_type=jax.ShapeDtypeStruct((n, table.shape[1]), table.dtype),
        mesh=mesh,
        scratch_types=[pltpu.VMEM((SC_CHUNK,), I32), pltpu.VMEM((SC_CHUNK, table.shape[1]), table.dtype),
                       pltpu.SemaphoreType.DMA],
        name="sc_scatter_rows" if scatter else "sc_gather_rows",
    )(table, idx)


WEIGHT_RING = 3
WEIGHT_CHUNKS = 4


def _expert_weight_copies(w_hbm, wbuf, sem, expert, slot):
    rows = w_hbm.shape[1] // WEIGHT_CHUNKS
    return [pltpu.make_async_copy(w_hbm.at[expert, pl.ds(c * rows, rows)],
                                  wbuf.at[slot, pl.ds(c * rows, rows)], sem.at[slot])
            for c in range(WEIGHT_CHUNKS)]


def _moe_kernel(blk_ref, exp_ref, lo_ref, hi_ref, first_ref, last_ref, newexp_ref,
                slot_ref, ahead_exp_ref, ahead_ok_ref, prologue_ref,
                x_ref, wg_hbm, wu_hbm, wd_hbm, y_ref,
                acc_ref, wgf_ref, wuf_ref, wdf_ref, wgb_ref, wub_ref, wdb_ref, wsem):
    it = pl.program_id(0)
    lo, hi = lo_ref[it], hi_ref[it]
    streams = ((wg_hbm, wgf_ref), (wu_hbm, wuf_ref), (wd_hbm, wdf_ref))

    def request(expert, slot):
        for w_hbm, wbuf in streams:
            for cp in _expert_weight_copies(w_hbm, wbuf, wsem, expert, slot):
                cp.start()

    @pl.when(it == 0)
    def _():
        request(exp_ref[0], 0)

        @pl.when(prologue_ref[1] == 1)
        def _():
            request(prologue_ref[0], 1)

    @pl.when(newexp_ref[it] == 1)
    def _():
        slot = slot_ref[it]
        for w_hbm, wbuf in streams:
            for cp in _expert_weight_copies(w_hbm, wbuf, wsem, 0, slot):
                cp.wait()
        wgb_ref[...] = wgf_ref[slot].astype(BF16)
        wub_ref[...] = wuf_ref[slot].astype(BF16)
        wdb_ref[...] = wdf_ref[slot].astype(BF16)

        @pl.when(ahead_ok_ref[it] == 1)
        def _():
            ahead_slot = slot + (WEIGHT_RING - 1)
            request(ahead_exp_ref[it], jnp.where(ahead_slot >= WEIGHT_RING, ahead_slot - WEIGHT_RING, ahead_slot))

    @pl.when(first_ref[it] == 1)
    def _():
        acc_ref[...] = jnp.zeros(acc_ref.shape, F32)

    def expert_pass(r0, nrows):
        rows = slice(r0, r0 + nrows)
        ridx = r0 + lax.broadcasted_iota(I32, (nrows, HALF), 0)
        mine = (ridx >= lo) & (ridx < hi)
        xlo, xhi = _unpack_bf16_pairs(jnp.where(mine, x_ref[rows, :], 0))
        xlo, xhi = xlo.astype(BF16), xhi.astype(BF16)
        gate = _dot(xlo, wgb_ref[:HALF]) + _dot(xhi, wgb_ref[HALF:])
        up = _dot(xlo, wub_ref[:HALF]) + _dot(xhi, wub_ref[HALF:])
        acc_ref[rows, :] = acc_ref[rows, :] + _dot((_silu(gate) * up).astype(BF16), wdb_ref[...])

    mid = DISPATCH_BLOCK // 2
    pl.when((lo < mid) & (hi > mid))(lambda: expert_pass(0, DISPATCH_BLOCK))
    pl.when((hi > lo) & (hi <= mid))(lambda: expert_pass(0, mid))
    pl.when((hi > lo) & (lo >= mid))(lambda: expert_pass(mid, mid))

    @pl.when(last_ref[it] == 1)
    def _():
        y_ref[...] = _pack_bf16_pairs(acc_ref[...])


def _moe(xs, items, w_gate, w_up, w_down):
    by_blk = lambda it, blk, *_: (blk[it], 0)
    any_space = pl.BlockSpec(memory_space=pl.ANY)
    grid_spec = pltpu.PrefetchScalarGridSpec(
        num_scalar_prefetch=len(items),
        grid=(N_ITEMS,),
        in_specs=[pl.BlockSpec((DISPATCH_BLOCK, HALF), by_blk), any_space, any_space, any_space],
        out_specs=pl.BlockSpec((DISPATCH_BLOCK, HALF), by_blk),
        scratch_shapes=[pltpu.VMEM((DISPATCH_BLOCK, D_MODEL), F32),
                        pltpu.VMEM((WEIGHT_RING, D_MODEL, D_EXPERT), F32),
                        pltpu.VMEM((WEIGHT_RING, D_MODEL, D_EXPERT), F32),
                        pltpu.VMEM((WEIGHT_RING, D_EXPERT, D_MODEL), F32),
                        pltpu.VMEM((D_MODEL, D_EXPERT), BF16), pltpu.VMEM((D_MODEL, D_EXPERT), BF16),
                        pltpu.VMEM((D_EXPERT, D_MODEL), BF16),
                        pltpu.SemaphoreType.DMA((WEIGHT_RING,))],
    )
    return pl.pallas_call(
        _moe_kernel,
        grid_spec=grid_spec,
        out_shape=jax.ShapeDtypeStruct((NK, HALF), I32),
        compiler_params=_cparams("arbitrary"),
        name="moe",
    )(*items, xs, w_gate, w_up, w_down)


COMB_TC = 512


def _combine_kernel(slots_ref, w_ref, xpart_ref, mod_ref, out_ref):
    w = w_ref[...]
    lo = jnp.zeros((w.shape[0], HALF), F32)
    hi = jnp.zeros((w.shape[0], HALF), F32)
    for k in range(TOP_K):
        klo, khi = _unpack_bf16_pairs(slots_ref[k])
        lo = lo + w[:, k:k + 1] * klo
        hi = hi + w[:, k:k + 1] * khi
    gate2 = mod_ref[0][5:6]
    out_ref[:, :HALF] = xpart_ref[:, :HALF] + gate2[:, :HALF] * lo
    out_ref[:, HALF:] = xpart_ref[:, HALF:] + gate2[:, HALF:] * hi


def _combine(xpart, mod, slots, w):
    tc = COMB_TC
    row = lambda i: (i, 0)
    return pl.pallas_call(
        _combine_kernel,
        grid=(N_TOK // tc,),
        in_specs=[pl.BlockSpec((TOP_K, tc, HALF), lambda i: (0, i, 0)),
                  pl.BlockSpec((tc, TOP_K), row),
                  pl.BlockSpec((tc, D_MODEL), row),
                  pl.BlockSpec((1, 6, D_MODEL), lambda i: (i // (SEQ // tc), 0, 0))],
        out_specs=pl.BlockSpec((tc, D_MODEL), row),
        out_shape=jax.ShapeDtypeStruct((N_TOK, D_MODEL), F32),
        compiler_params=_cparams("arbitrary"),
        name="combine",
    )(slots.reshape(TOP_K, N_TOK, HALF), w, xpart, mod)


def _layer(x, c, w_ada, b_ada, g_norm1, g_norm2, w_in, q_gain, kc_gain, ks_gain, kw_gain,
           pe_k, pe_v, w_cmp_k1, w_cmp_k2, w_cmp_v1, w_cmp_v2,
           a_re, a_im, log_dt, b_re, b_im, c_re, c_im, d_skip, w_glu, b_glu,
           w_up_attn, w_up_ssm, w_out, w_router, router_bias,
           w_gate, w_up, w_down, ws_gate, ws_up, ws_down):
    x2 = x.reshape(N_TOK, D_MODEL)
    mod = _ada(c, w_ada, b_ada)
    q, kc_raw, vc_raw, ks, kw, vst, vwt, gn, u, ga, gs = _proj(x2, mod, g_norm1, w_in, q_gain, ks_gain, kw_gain)
    kcn = _compress(kc_raw, pe_k, w_cmp_k1, w_cmp_k2, kc_gain, True)
    vcn = _compress(vc_raw, pe_v, w_cmp_v1, w_cmp_v2, kc_gain, False)
    ocmp, selb = _cmp_attn(q, kcn, vcn)
    osel, owin = _selwin(q, ks, kw, vst, vwt, selb)
    yssm = _s5(u, *_s5_params(a_re, a_im, log_dt, b_re, b_im, c_re, c_im))
    xpart, h2, logits_t = _merge(ocmp, osel, owin, gn, yssm, u, ga, gs, x2, mod, d_skip, w_glu, b_glu,
                                  w_up_attn, w_up_ssm, w_out, g_norm2, w_router, ws_gate, ws_up, ws_down)
    eidx_t, w_t, counts = _route(logits_t, router_bias)
    tok, home, items = _dispatch_plan(eidx_t.T, counts)
    y = _moe(_sc_move_rows(h2, tok, scatter=False), items, w_gate, w_up, w_down)
    slots = _sc_move_rows(y, home, scatter=True)
    return _combine(xpart, mod, slots, w_t.T).reshape(BATCH, SEQ, D_MODEL)


def kernel(x, c, w_ada, b_ada, g_norm1, g_norm2, w_in, q_gain, kc_gain, ks_gain, kw_gain, pe_k, pe_v, w_cmp_k1,
           w_cmp_k2, w_cmp_v1, w_cmp_v2, a_re, a_im, log_dt, b_re, b_im, c_re, c_im, d_skip, w_glu, b_glu,
           w_up_attn, w_up_ssm, w_out, w_router, router_bias, w_gate, w_up, w_down, ws_gate, ws_up, ws_down):
    params = (w_ada, b_ada, g_norm1, g_norm2, w_in, q_gain, kc_gain, ks_gain, kw_gain, pe_k, pe_v, w_cmp_k1,
              w_cmp_k2, w_cmp_v1, w_cmp_v2, a_re, a_im, log_dt, b_re, b_im, c_re, c_im, d_skip, w_glu, b_glu,
              w_up_attn, w_up_ssm, w_out, w_router, router_bias, w_gate, w_up, w_down, ws_gate, ws_up, ws_down)
    depth = w_ada.shape[0]
    for layer in range(depth):
        x = _layer(x, c, *[p[layer] for p in params])
    return x
```

```python
import functools
import math

import jax
import jax.numpy as jnp
import numpy as np
from jax import lax
from jax.experimental import pallas as pl
from jax.experimental.pallas import tpu as pltpu
from jax.experimental.pallas import tpu_sc as plsc

F32 = jnp.float32
BF16 = jnp.bfloat16
I32 = jnp.int32
HIGHEST = lax.Precision.HIGHEST

D_MODEL = 1024
BATCH = 4
SEQ = 4096
N_TOK = BATCH * SEQ
N_HEADS = 8
HEAD_DIM = 64
N_KV = 2
CMP_BLOCK = 32
CMP_STRIDE = 16
CMP_HIDDEN = 256
N_CMP = 256
SEL_BLOCK = 64
N_SEL_BLOCKS = SEQ // SEL_BLOCK
N_SELECT = 16
WINDOW = 512
ATTN_SCALE = HEAD_DIM ** -0.5
LOG2E = 1.4426950408889634
NSA_WIDTH = N_HEADS * HEAD_DIM
SSM_WIDTH = 512
GROUP = 16
N_GROUPS = SSM_WIDTH // GROUP
STATE = 64
N_EXPERTS = 256
TOP_K = 8
D_EXPERT = 256
N_EXPERT_GROUPS = 8
EXPERTS_PER_GROUP = N_EXPERTS // N_EXPERT_GROUPS
TOPK_GROUPS = 4
ROUTE_SCALE = 2.5
DISPATCH_BLOCK = 512
EPS = 1e-6
NEG = -1e30

LANES = 128
S5_T = 16
S5_SG = 4
S5_GL = N_GROUPS // S5_SG
S5_CH = N_TOK // S5_T
S5_CH_PER_BATCH = SEQ // S5_T
S5_NSTATE = S5_GL * STATE * 2

NK = N_TOK * TOP_K
HALF = D_MODEL // 2

VMEM_LIMIT = 48 * 1024 * 1024


def _cparams(*sem, vmem=VMEM_LIMIT):
    return pltpu.CompilerParams(dimension_semantics=tuple(sem), vmem_limit_bytes=vmem)


def _dot(a, b):
    return jnp.dot(a, b, preferred_element_type=F32)


def _dot_nt(a, b):
    return lax.dot_general(a, b, (((1,), (1,)), ((), ())), preferred_element_type=F32)


def _split_dot(v, w):
    hi = v.astype(BF16)
    lo = (v - hi.astype(F32)).astype(BF16)
    return _dot(hi, w) + _dot(lo, w)


def _seg_rms(v, bd, gain):
    ss = _split_dot(v * v, bd)
    return v * lax.rsqrt(ss * (1.0 / HEAD_DIM) + EPS) * gain


def _gelu(x):
    return 0.5 * x * (1.0 + jnp.tanh(0.7978845608028654 * (x + 0.044715 * (x * x * x))))


def _silu(x):
    return x * jax.nn.sigmoid(x)


def _pack_bf16_pairs(v):
    h = v.shape[1] // 2
    return pltpu.pack_elementwise([v[:, :h], v[:, h:]], packed_dtype=BF16)


def _unpack_bf16_pairs(word):
    return (pltpu.unpack_elementwise(word, index=0, packed_dtype=BF16, unpacked_dtype=F32),
            pltpu.unpack_elementwise(word, index=1, packed_dtype=BF16, unpacked_dtype=F32))


def _ada_kernel(c_ref, w_ref, b_ref, o_ref):
    c = c_ref[...]
    o_ref[...] = jnp.dot(_silu(c), w_ref[...], preferred_element_type=F32, precision=HIGHEST) + b_ref[...]


def _ada(c, w_ada, b_ada):
    cp = jnp.pad(c, ((0, 8 - BATCH), (0, 0)))
    tn = 1536
    out = pl.pallas_call(
        _ada_kernel,
        grid=(6 * D_MODEL // tn,),
        in_specs=[pl.BlockSpec((8, D_MODEL), lambda j: (0, 0)),
                  pl.BlockSpec((D_MODEL, tn), lambda j: (0, j)),
                  pl.BlockSpec((1, tn), lambda j: (0, j))],
        out_specs=pl.BlockSpec((8, tn), lambda j: (0, j)),
        out_shape=jax.ShapeDtypeStruct((8, 6 * D_MODEL), F32),
        compiler_params=_cparams("arbitrary"),
        name="ada",
    )(cp, w_ada, b_ada.reshape(1, -1))
    return out.reshape(8, 6, D_MODEL)


_C_Q = 0
_C_KC = 512
_C_VC = 640
_C_KS = 768
_C_KW = 1024
_C_GN = 1280
_C_U = 1408
_C_GA = 1920
_C_GS = 2944
_C_END = 3968
PROJ_TM = 512


def _proj_kernel(x_ref, mod_ref, g1_ref, w_ref, wvt_ref, qg_ref, ksg_ref, kwg_ref, bd512_ref, bd256_ref,
                 q_ref, kc_ref, vc_ref, ks_ref, kw_ref, vst_ref, vwt_ref, gn_ref, u_ref, ga_ref, gs_ref):
    x = x_ref[...]
    ms = jnp.mean(x * x, axis=-1, keepdims=True)
    mod = mod_ref[0]
    h = (x * lax.rsqrt(ms + EPS) * g1_ref[...]) * (1.0 + mod[1:2]) + mod[0:1]
    hb = h.astype(BF16)

    def p(lo, hi):
        return _dot(hb, w_ref[:, lo:hi])

    q_ref[...] = _seg_rms(p(_C_Q, _C_KC), bd512_ref[...], qg_ref[...] * (ATTN_SCALE * LOG2E)).astype(BF16)
    kc_ref[...] = p(_C_KC, _C_VC)
    vc_ref[...] = p(_C_VC, _C_KS)
    ks_ref[...] = _seg_rms(p(_C_KS, _C_KW), bd256_ref[...], ksg_ref[...]).astype(BF16)
    kw_ref[...] = _seg_rms(p(_C_KW, _C_GN), bd256_ref[...], kwg_ref[...]).astype(BF16)
    vt = _dot_nt(wvt_ref[...], hb)
    vst_ref[...] = vt[:LANES].astype(BF16)
    vwt_ref[...] = vt[LANES:].astype(BF16)
    gn_ref[...] = jax.nn.sigmoid(p(_C_GN, _C_U))
    u_ref[...] = p(_C_U, _C_GA)
    ga_ref[...] = jax.nn.sigmoid(p(_C_GA, _C_GS)).astype(BF16)
    gs_ref[...] = jax.nn.sigmoid(p(_C_GS, _C_END)).astype(BF16)


def _dup_cols(w):
    return jnp.concatenate([w[:, :64], w[:, :64], w[:, 64:], w[:, 64:]], axis=1)


def _block_ones(n):
    return jnp.kron(jnp.eye(n // HEAD_DIM, dtype=F32), jnp.ones((HEAD_DIM, HEAD_DIM), F32)).astype(BF16)


def _proj(x2, mod, g_norm1, w_in, q_gain, ks_gain, kw_gain):
    o = np.cumsum((0, 512, 128, 128, 128, 128, 128, 128, 24, 512, 1024, 1024))
    parts = [w_in[:, o[i]:o[i + 1]] for i in range(11)]
    wq, wkc, wvc, wks, wvs, wkw, wvw, wgn, wu, wga, wgs = parts
    w = jnp.concatenate([wq, wkc, wvc, _dup_cols(wks), _dup_cols(wkw),
                         jnp.pad(wgn, ((0, 0), (0, LANES - 24))), wu, wga, wgs], axis=1).astype(BF16)
    wvt = jnp.concatenate([wvs, wvw], axis=1).T.astype(BF16)
    tm = PROJ_TM
    row = lambda i: (i, 0)
    col = lambda i: (0, i)
    fix = lambda i: (0, 0)
    outs = [(512, BF16, row), (128, F32, row), (128, F32, row), (256, BF16, row), (256, BF16, row),
            (LANES, BF16, col), (LANES, BF16, col),
            (128, F32, row), (512, F32, row), (1024, BF16, row), (1024, BF16, row)]
    ospec = lambda wd, m: pl.BlockSpec((tm, wd), m) if m is row else pl.BlockSpec((wd, tm), m)
    oshape = lambda wd, dt, m: jax.ShapeDtypeStruct((N_TOK, wd) if m is row else (wd, N_TOK), dt)
    return pl.pallas_call(
        _proj_kernel,
        grid=(N_TOK // tm,),
        in_specs=[pl.BlockSpec((tm, D_MODEL), row),
                  pl.BlockSpec((1, 6, D_MODEL), lambda i: (i // (SEQ // tm), 0, 0)),
                  pl.BlockSpec((1, D_MODEL), fix),
                  pl.BlockSpec((D_MODEL, _C_END), fix),
                  pl.BlockSpec((2 * LANES, D_MODEL), fix),
                  pl.BlockSpec((1, 512), fix), pl.BlockSpec((1, 256), fix), pl.BlockSpec((1, 256), fix),
                  pl.BlockSpec((512, 512), fix), pl.BlockSpec((256, 256), fix)],
        out_specs=[ospec(wd, m) for wd, _, m in outs],
        out_shape=[oshape(wd, dt, m) for wd, dt, m in outs],
        compiler_params=_cparams("arbitrary"),
        name="proj",
    )(x2, mod, g_norm1.reshape(1, -1), w, wvt,
      jnp.tile(q_gain, N_HEADS).reshape(1, -1), jnp.tile(ks_gain, 4).reshape(1, -1),
      jnp.tile(kw_gain, 4).reshape(1, -1), _block_ones(512), _block_ones(256))


def _compress_kernel(r_ref, pe_ref, w1_ref, w2_ref, bd_ref, gain_ref, o_ref, *, do_norm):
    r = jnp.concatenate([r_ref[0, :, l, :] for l in range(CMP_STRIDE)], axis=1)
    p0 = _dot((r + pe_ref[0]).astype(BF16), w1_ref[0])
    p1 = _dot((r + pe_ref[1]).astype(BF16), w1_ref[1])
    hid = p0 + pltpu.roll(p1, N_CMP - 1, 0)
    c = _dot(_gelu(hid).astype(BF16), w2_ref[...])
    if do_norm:
        c = _seg_rms(c, bd_ref[...], gain_ref[...])
    o_ref[0] = c.astype(BF16)


def _compress(raw, pe, w1, w2, gain, do_norm):
    r = raw.reshape(BATCH, SEQ // CMP_STRIDE, CMP_STRIDE, LANES)
    eye = jnp.eye(N_KV, dtype=F32)
    w1r = w1.reshape(2, CMP_STRIDE, HEAD_DIM, CMP_HIDDEN)
    w1big = jnp.einsum('hldc,gk->hlgdkc', w1r, eye).reshape(2, CMP_STRIDE * LANES, N_KV * CMP_HIDDEN).astype(BF16)
    w2big = jnp.einsum('cd,gk->gckd', w2, eye)
    w2big = jnp.concatenate([w2big, w2big], axis=-1).reshape(N_KV * CMP_HIDDEN, 4 * HEAD_DIM).astype(BF16)
    pe_big = jnp.broadcast_to(pe.reshape(2, CMP_STRIDE, 1, HEAD_DIM), (2, CMP_STRIDE, N_KV, HEAD_DIM))
    pe_big = pe_big.reshape(2, 1, CMP_STRIDE * LANES)
    fix2 = lambda b: (0, 0)
    fix3 = lambda b: (0, 0, 0)
    return pl.pallas_call(
        functools.partial(_compress_kernel, do_norm=do_norm),
        grid=(BATCH,),
        in_specs=[pl.BlockSpec((1, N_CMP, CMP_STRIDE, LANES), lambda b: (b, 0, 0, 0)),
                  pl.BlockSpec((2, 1, CMP_STRIDE * LANES), fix3),
                  pl.BlockSpec((2, CMP_STRIDE * LANES, N_KV * CMP_HIDDEN), fix3),
                  pl.BlockSpec((N_KV * CMP_HIDDEN, 256), fix2),
                  pl.BlockSpec((256, 256), fix2), pl.BlockSpec((1, 256), fix2)],
        out_specs=pl.BlockSpec((1, N_CMP, 256), lambda b: (b, 0, 0)),
        out_shape=jax.ShapeDtypeStruct((BATCH, N_CMP, 256), BF16),
        compiler_params=_cparams("arbitrary"),
        name="compress_k" if do_norm else "compress_v",
    )(r, pe_big, w1big, w2big, _block_ones(256), jnp.tile(gain, 4).reshape(1, -1))


ATT_TQ = 256
RANK_CHUNK = 16


def _head_variants(qb):
    lane = lax.broadcasted_iota(I32, qb.shape, 1)
    z = jnp.zeros_like(qb)
    return jnp.where(lane < HEAD_DIM, qb, z), jnp.where(lane < HEAD_DIM, z, qb)


def _cmp_kernel(q_ref, kc_ref, vc_ref, ov_ref, o_ref, sel_ref, vrank_ref):
    tq = ATT_TQ
    qi = pl.program_id(1)
    tpos = qi * tq + lax.broadcasted_iota(I32, (tq, N_CMP), 0)
    nidx = lax.broadcasted_iota(I32, (tq, N_CMP), 1)
    mask = (CMP_STRIDE * nidx + (CMP_BLOCK - 1)) <= tpos
    lane_lo = lax.broadcasted_iota(I32, (tq, LANES), 1) < HEAD_DIM
    for g in range(N_KV):
        kd = kc_ref[0, :, g * LANES:(g + 1) * LANES]
        vd = vc_ref[0, :, g * LANES:(g + 1) * LANES]
        psum = jnp.zeros((tq, N_CMP), F32)
        for jb in range(2):
            blk = 2 * g + jb
            pv = []
            for qv in _head_variants(q_ref[:, blk * LANES:(blk + 1) * LANES]):
                s = jnp.where(mask, _dot_nt(qv, kd), NEG)
                m = jnp.max(s, axis=-1, keepdims=True)
                e = jnp.where(mask, jnp.exp2(s - m), 0.0)
                l = jnp.sum(e, axis=-1, keepdims=True)
                p = e / jnp.where(l > 0.0, l, 1.0)
                psum = psum + p
                pv.append(_dot(p.astype(BF16), vd))
            o_ref[:, blk * LANES:(blk + 1) * LANES] = jnp.where(lane_lo, pv[0], pv[1]).astype(BF16)
        imp = _split_dot(psum, ov_ref[...])
        imp_t = imp.T[:N_SEL_BLOCKS]
        j = lax.broadcasted_iota(I32, (N_SEL_BLOCKS, tq), 0)
        cur = jnp.right_shift(qi * tq + lax.broadcasted_iota(I32, (N_SEL_BLOCKS, tq), 1), 6)
        forced = (j == 0) | (j == cur) | (j == cur - 1)
        v = jnp.where(forced, jnp.inf, jnp.where(j <= cur, imp_t, -jnp.inf))
        vrank_ref[...] = jnp.zeros((N_SEL_BLOCKS, tq), F32)
        n_live = (qi + 1) * (tq // SEL_BLOCK)
        for c0 in range(0, N_SEL_BLOCKS, RANK_CHUNK):
            @pl.when(c0 < n_live)
            def _():
                rank = vrank_ref[...]
                for jp in range(c0, c0 + RANK_CHUNK):
                    row = v[jp:jp + 1, :]
                    tie = jnp.where(j > jp, 1.0, 0.0)
                    rank = rank + jnp.where(row > v, 1.0, jnp.where(row == v, tie, 0.0))
                vrank_ref[...] = rank
        rank = vrank_ref[...]
        sel_ref[g * N_SEL_BLOCKS:(g + 1) * N_SEL_BLOCKS, :] = jnp.where(rank < float(N_SELECT), 0.0, NEG)


def _cmp_attn(q, kcn, vcn):
    nc = np.arange(N_CMP)
    sb = np.arange(LANES)
    ov = ((CMP_STRIDE * nc[:, None] < SEL_BLOCK * sb[None, :] + SEL_BLOCK)
          & (CMP_STRIDE * nc[:, None] + CMP_BLOCK > SEL_BLOCK * sb[None, :])
          & (nc[:, None] < N_CMP - 1) & (sb[None, :] < N_SEL_BLOCKS))
    ov = jnp.asarray(ov, BF16)
    tq = ATT_TQ
    nq = SEQ // tq
    row = lambda b, i: (b * nq + i, 0)
    return pl.pallas_call(
        _cmp_kernel,
        grid=(BATCH, nq),
        in_specs=[pl.BlockSpec((tq, NSA_WIDTH), row),
                  pl.BlockSpec((1, N_CMP, 256), lambda b, i: (b, 0, 0)),
                  pl.BlockSpec((1, N_CMP, 256), lambda b, i: (b, 0, 0)),
                  pl.BlockSpec((N_CMP, LANES), lambda b, i: (0, 0))],
        out_specs=[pl.BlockSpec((tq, NSA_WIDTH), row),
                   pl.BlockSpec((N_KV * N_SEL_BLOCKS, tq), lambda b, i: (0, b * nq + i))],
        out_shape=[jax.ShapeDtypeStruct((N_TOK, NSA_WIDTH), BF16),
                   jax.ShapeDtypeStruct((N_KV * N_SEL_BLOCKS, N_TOK), F32)],
        scratch_shapes=[pltpu.VMEM((N_SEL_BLOCKS, tq), F32)],
        compiler_params=_cparams("arbitrary", "arbitrary"),
        name="cmp_attn",
    )(q, kcn, vcn, ov)


ATT_TK = 256


M_INIT = -1e29


SUM_ROWS = 16
SEL_TK = 512


def _selwin_kernel(q_ref, ks_ref, kw_ref, vst_ref, vwt_ref, selb_ref, osel_ref, owin_ref, m_ref, acc_ref):
    tq, tk = ATT_TQ, ATT_TK
    qi = pl.program_id(1)
    krow = lax.broadcasted_iota(I32, (tk, tq), 0)
    qcol = lax.broadcasted_iota(I32, (tk, tq), 1)
    causal_bias = jnp.where(krow <= qcol, 0.0, NEG)
    far_bias = jnp.where(qcol < krow, 0.0, NEG)

    def reset():
        m_ref[...] = jnp.full(m_ref.shape, M_INIT, F32)
        acc_ref[...] = jnp.zeros(acc_ref.shape, F32)

    def update(g, k_ref, vt_ref, kt, bias, nk=tk):
        k0 = pl.multiple_of(kt * nk, nk)
        kd = k_ref[0, pl.ds(k0, nk), g * LANES:(g + 1) * LANES]
        vt = vt_ref[g * HEAD_DIM:(g + 1) * HEAD_DIM, pl.ds(k0, nk)]
        s = _dot_nt(kd, qvars[g])
        if bias is not None:
            s = s + jnp.concatenate([bias] * 4, axis=1)
        m_old = m_ref[g]
        m_new = jnp.maximum(m_old, jnp.max(s, axis=0, keepdims=True))
        alpha = jnp.exp2(m_old - m_new)
        p = jnp.exp2(s - m_new)
        m_ref[g] = m_new
        vte = jnp.concatenate([vt, jnp.ones((SUM_ROWS, nk), BF16)], axis=0)
        acc_ref[g] = alpha * acc_ref[g] + _dot(vte, p.astype(BF16))

    def finish(out_ref, g):
        o = acc_ref[g, :HEAD_DIM, :] / acc_ref[g, HEAD_DIM:HEAD_DIM + 1, :]
        for jb in range(2):
            blk = 2 * g + jb
            pair = jnp.concatenate([o[:, 2 * jb * tq:(2 * jb + 1) * tq], o[:, (2 * jb + 1) * tq:(2 * jb + 2) * tq]],
                                   axis=0)
            out_ref[:, blk * LANES:(blk + 1) * LANES] = pair.T.astype(BF16)

    def sel_bias(g, kt):
        rows = [jnp.broadcast_to(selb_ref[pl.ds(g * N_SEL_BLOCKS + kt * (SEL_TK // SEL_BLOCK) + r, 1), :],
                                 (SEL_BLOCK, tq)) for r in range(SEL_TK // SEL_BLOCK)]
        return jnp.concatenate(rows, axis=0)

    qvars = []
    for g in range(N_KV):
        heads = []
        for jb in range(2):
            heads.extend(_head_variants(q_ref[:, (2 * g + jb) * LANES:(2 * g + jb + 1) * LANES]))
        qvars.append(jnp.concatenate(heads, axis=0))
    groups = range(N_KV)

    reset()
    last_tile = qi // (SEL_TK // tq)

    def sel_step(kt, carry):
        for g in groups:
            update(g, ks_ref, vst_ref, kt, sel_bias(g, kt), SEL_TK)
        return carry

    lax.fori_loop(0, last_tile, sel_step, 0)
    q_first = qi * tq - last_tile * SEL_TK
    visible = (lax.broadcasted_iota(I32, (SEL_TK, tq), 0)
               <= lax.broadcasted_iota(I32, (SEL_TK, tq), 1) + q_first)
    diag_bias = jnp.where(visible, 0.0, NEG)
    for g in groups:
        update(g, ks_ref, vst_ref, last_tile, sel_bias(g, last_tile) + diag_bias, SEL_TK)
    for g in groups:
        finish(osel_ref, g)

    reset()

    @pl.when(qi >= 2)
    def _():
        for g in groups:
            update(g, kw_ref, vwt_ref, qi - 2, far_bias)

    @pl.when(qi >= 1)
    def _():
        for g in groups:
            update(g, kw_ref, vwt_ref, qi - 1, None)

    for g in groups:
        update(g, kw_ref, vwt_ref, qi, causal_bias)
    for g in groups:
        finish(owin_ref, g)


def _selwin(q, ks, kw, vst, vwt, selb):
    tq = ATT_TQ
    nq = SEQ // tq
    assert WINDOW == 2 * ATT_TK and ATT_TQ == ATT_TK
    row = lambda b, i: (b * nq + i, 0)
    keys = pl.BlockSpec((1, SEQ, 256), lambda b, i: (b, 0, 0))
    vals = pl.BlockSpec((LANES, SEQ), lambda b, i: (0, b))
    r3 = lambda a: a.reshape(BATCH, SEQ, 256)
    return pl.pallas_call(
        _selwin_kernel,
        grid=(BATCH, nq),
        in_specs=[pl.BlockSpec((tq, NSA_WIDTH), row), keys, keys, vals, vals,
                  pl.BlockSpec((N_KV * N_SEL_BLOCKS, tq), lambda b, i: (0, b * nq + i))],
        out_specs=[pl.BlockSpec((tq, NSA_WIDTH), row)] * 2,
        out_shape=[jax.ShapeDtypeStruct((N_TOK, NSA_WIDTH), BF16)] * 2,
        scratch_shapes=[pltpu.VMEM((N_KV, 1, 4 * tq), F32),
                        pltpu.VMEM((N_KV, HEAD_DIM + SUM_ROWS, 4 * tq), F32)],
        compiler_params=_cparams("arbitrary", "arbitrary"),
        name="selwin",
    )(q, r3(ks), r3(kw), vst, vwt, selb)


def _s5_param_kernel(are_ref, aim_ref, ldt_ref, cre_ref, cim_ref, bre_ref, bim_ref,
                     clre_ref, clim_ref, wbre_ref, wbim_ref, bbre_ref, bbim_ref, ltre_ref, ltim_ref):
    are, aim = are_ref[...], aim_ref[...]
    dt = jnp.exp(ldt_ref[...])
    cre, cim = cre_ref[...], cim_ref[...]

    def lam_pow(tau):
        mag = jnp.exp(are * dt * float(tau))
        ang = aim * dt * float(tau)
        return mag * jnp.cos(ang), mag * jnp.sin(ang)

    lre, lim = lam_pow(1)
    den = are * are + aim * aim
    qre = ((lre - 1.0) * are + lim * aim) / den
    qim = (lim * are - (lre - 1.0) * aim) / den
    bre, bim = bre_ref[...], bim_ref[...]
    bbre = qre * bre - qim * bim
    bbim = qre * bim + qim * bre
    bbre_ref[...] = bbre
    bbim_ref[...] = bbim
    for tau in range(S5_T + 1):
        pr, pi = lam_pow(tau)
        clre_ref[tau] = cre * pr - cim * pi
        clim_ref[tau] = cre * pi + cim * pr
        if tau < S5_T:
            k = S5_T - 1 - tau
            wbre_ref[k] = pr * bbre - pi * bbim
            wbim_ref[k] = pr * bbim + pi * bbre
        else:
            ltre_ref[...] = pr
            ltim_ref[...] = pi


def _s5_kmat_kernel(l_ref, r_ref, o_ref):
    o_ref[0] = jnp.dot(l_ref[0], r_ref[0], preferred_element_type=F32, precision=HIGHEST)


def _s5_params(a_re, a_im, log_dt, b_re, b_im, c_re, c_im):
    T = S5_T
    pn = GROUP * STATE
    tile_p = lambda a: jnp.tile(a, (1, GROUP))
    args = (tile_p(a_re), tile_p(a_im), jnp.broadcast_to(log_dt[:, None], (N_GROUPS, pn)),
            c_re.reshape(N_GROUPS, pn), c_im.reshape(N_GROUPS, pn),
            jnp.swapaxes(b_re, 1, 2).reshape(N_GROUPS, pn), jnp.swapaxes(b_im, 1, 2).reshape(N_GROUPS, pn))
    full2 = pl.BlockSpec((N_GROUPS, pn), lambda: (0, 0))
    clre, clim, wbre, wbim, bbre, bbim, ltre, ltim = pl.pallas_call(
        _s5_param_kernel,
        in_specs=[full2] * 7,
        out_specs=[pl.BlockSpec((T + 1, N_GROUPS, pn), lambda: (0, 0, 0))] * 2
                  + [pl.BlockSpec((T, N_GROUPS, pn), lambda: (0, 0, 0))] * 2 + [full2] * 4,
        out_shape=[jax.ShapeDtypeStruct((T + 1, N_GROUPS, pn), F32)] * 2
                  + [jax.ShapeDtypeStruct((T, N_GROUPS, pn), F32)] * 2
                  + [jax.ShapeDtypeStruct((N_GROUPS, pn), F32)] * 4,
        name="s5_params",
    )(*args)

    r5 = lambda a, t: a[:t].reshape(t, N_GROUPS, GROUP, STATE)
    lhs = jnp.concatenate([r5(clre, T), -r5(clim, T)], axis=-1)
    lhs = jnp.transpose(lhs, (1, 0, 2, 3)).reshape(N_GROUPS, T * GROUP, 2 * STATE)
    bb = lambda a: jnp.swapaxes(a.reshape(N_GROUPS, GROUP, STATE), 1, 2)
    rhs = jnp.concatenate([bb(bbre), bb(bbim)], axis=1)
    kmat = pl.pallas_call(
        _s5_kmat_kernel,
        grid=(N_GROUPS,),
        in_specs=[pl.BlockSpec((1, T * GROUP, 2 * STATE), lambda g: (g, 0, 0)),
                  pl.BlockSpec((1, 2 * STATE, GROUP), lambda g: (g, 0, 0))],
        out_specs=pl.BlockSpec((1, T * GROUP, GROUP), lambda g: (g, 0, 0)),
        out_shape=jax.ShapeDtypeStruct((N_GROUPS, T * GROUP, GROUP), F32),
        compiler_params=_cparams("arbitrary"),
        name="s5_kmat",
    )(lhs, rhs)

    eye = jnp.eye(S5_GL, dtype=F32)
    kt = kmat.reshape(S5_SG, S5_GL, T, GROUP, GROUP)
    kbd = jnp.einsum('sgtpq,gh->stgqhp', kt, eye).reshape(S5_SG, T, LANES, LANES)
    krev = kbd[:, ::-1].reshape(S5_SG, T * LANES, LANES).astype(BF16)
    krev = jnp.pad(krev, ((0, 0), (0, LANES), (0, 0)))
    r6 = lambda a: a.reshape(T, S5_SG, S5_GL, GROUP, STATE)
    wb = jnp.stack([r6(wbre), r6(wbim)], axis=-2)
    wb = jnp.einsum('ksgpin,gh->skgpihn', wb, eye).reshape(S5_SG, T * LANES, S5_NSTATE).astype(BF16)
    wc = jnp.stack([r6(clre[1:]), -r6(clim[1:])], axis=-2)
    wc = jnp.einsum('tsgpin,gh->signthp', wc, eye).reshape(S5_SG, S5_NSTATE, T * LANES).astype(BF16)
    lt = lambda a: a.reshape(N_GROUPS, GROUP, STATE)[:, 0].reshape(S5_SG, 1, S5_GL * STATE)
    return krev, wb, wc, lt(ltre), lt(ltim)


S5_TC = 256


def _s5_lane_block(sg):
    return pl.ds(pl.multiple_of(sg * LANES, LANES), LANES)


def _s5_chunk_inputs(x_ref, sg):
    return jnp.concatenate([x_ref[:, t, _s5_lane_block(sg)] for t in range(S5_T)], axis=1).astype(BF16)


def _s5_state_kernel(x_ref, wb_ref, e_ref):
    e_ref[0] = _dot(_s5_chunk_inputs(x_ref, pl.program_id(1)), wb_ref[0])


def _s5_scan_kernel(e_ref, ltre_ref, ltim_ref, xs_ref):
    lr, li = ltre_ref[0], ltim_ref[0]
    half = S5_NSTATE // 2

    def step(c, carry):
        new = []
        for b in range(BATCH):
            xr, xi = carry[b]
            row = b * S5_CH_PER_BATCH + c
            xs_ref[0, pl.ds(row, 1), :half] = xr
            xs_ref[0, pl.ds(row, 1), half:] = xi
            e = e_ref[0, pl.ds(row, 1), :]
            new.append((lr * xr - li * xi + e[:, :half], lr * xi + li * xr + e[:, half:]))
        return tuple(new)

    zero = jnp.zeros((1, half), F32)
    lax.fori_loop(0, S5_CH_PER_BATCH, step, tuple((zero, zero) for _ in range(BATCH)))


def _s5_out_kernel(x_ref, xs_ref, krev_ref, wc_ref, y_ref):
    sg = pl.program_id(1)
    x = _s5_chunk_inputs(x_ref, sg)
    xsb = xs_ref[0].astype(BF16)
    for t in range(0, S5_T, 2):
        n_in = (t + 2) * LANES
        first = (S5_T - 1 - t) * LANES
        taps = jnp.concatenate([krev_ref[0, first:first + n_in, :], krev_ref[0, first - LANES:first - LANES + n_in, :]],
                               axis=1)
        pair = _dot(x[:, :n_in], taps) + _dot(xsb, wc_ref[0, :, t * LANES:(t + 2) * LANES])
        y_ref[:, t, _s5_lane_block(sg)] = pair[:, :LANES]
        y_ref[:, t + 1, _s5_lane_block(sg)] = pair[:, LANES:]


def _s5(u, krev, wb, wc, ltre, ltim):
    T, tc = S5_T, S5_TC
    xn = u.reshape(S5_CH, T, SSM_WIDTH)
    grid = (S5_CH // tc, S5_SG)
    natural = pl.BlockSpec((tc, T, SSM_WIDTH), lambda i, s: (i, 0, 0))
    rows = lambda i, s: (s, i, 0)
    per_sg = lambda i, s: (s, 0, 0)
    e = pl.pallas_call(
        _s5_state_kernel, grid=grid,
        in_specs=[natural, pl.BlockSpec((1, T * LANES, S5_NSTATE), per_sg)],
        out_specs=pl.BlockSpec((1, tc, S5_NSTATE), rows),
        out_shape=jax.ShapeDtypeStruct((S5_SG, S5_CH, S5_NSTATE), F32),
        compiler_params=_cparams("arbitrary", "arbitrary"), name="s5_state",
    )(xn, wb)
    sg1 = lambda s: (s, 0, 0)
    xstart = pl.pallas_call(
        _s5_scan_kernel, grid=(S5_SG,),
        in_specs=[pl.BlockSpec((1, S5_CH, S5_NSTATE), sg1),
                  pl.BlockSpec((1, 1, S5_NSTATE // 2), sg1), pl.BlockSpec((1, 1, S5_NSTATE // 2), sg1)],
        out_specs=pl.BlockSpec((1, S5_CH, S5_NSTATE), sg1),
        out_shape=jax.ShapeDtypeStruct((S5_SG, S5_CH, S5_NSTATE), F32),
        compiler_params=_cparams("arbitrary"), name="s5_scan",
    )(e, ltre, ltim)
    y = pl.pallas_call(
        _s5_out_kernel, grid=grid,
        in_specs=[natural, pl.BlockSpec((1, tc, S5_NSTATE), rows),
                  pl.BlockSpec((1, (T + 1) * LANES, LANES), per_sg), pl.BlockSpec((1, S5_NSTATE, T * LANES), per_sg)],
        out_specs=natural,
        out_shape=jax.ShapeDtypeStruct((S5_CH, T, SSM_WIDTH), F32),
        compiler_params=_cparams("arbitrary", "arbitrary", vmem=56 * 1024 * 1024), name="s5_out",
    )(xn, xstart, krev, wc)
    return y.reshape(N_TOK, SSM_WIDTH)


MERGE_TM = 512


def _merge_kernel(ocmp_ref, osel_ref, owin_ref, gn_ref, yssm_ref, u_ref, ga_ref, gs_ref, x_ref, mod_ref,
                  eg_ref, dskip_ref, wglu_ref, bglu_ref, wua_ref, wus_ref, wout_ref, g2_ref,
                  wrhi_ref, wrlo_ref, wsgu_ref, wsd_ref,
                  xpart_ref, h2_ref, logit_ref):
    mod = mod_ref[0]
    gnb = gn_ref[...].astype(BF16)
    o_nsa = (_dot(gnb, eg_ref[0]) * ocmp_ref[...].astype(F32)
             + _dot(gnb, eg_ref[1]) * osel_ref[...].astype(F32)
             + _dot(gnb, eg_ref[2]) * owin_ref[...].astype(F32))
    attn = _dot(o_nsa.astype(BF16), wua_ref[...])
    z = _gelu(yssm_ref[...] + dskip_ref[...] * u_ref[...])
    y_ssm = z * jax.nn.sigmoid(_dot(z.astype(BF16), wglu_ref[...]) + bglu_ref[...])
    ssm = _dot(y_ssm.astype(BF16), wus_ref[...])
    merged = ga_ref[...].astype(F32) * attn + gs_ref[...].astype(F32) * ssm
    x1 = x_ref[...] + mod[2:3] * _dot(merged.astype(BF16), wout_ref[...])

    ms = jnp.mean(x1 * x1, axis=-1, keepdims=True)
    h2 = (x1 * lax.rsqrt(ms + EPS) * g2_ref[...]) * (1.0 + mod[4:5]) + mod[3:4]
    hi = h2.astype(BF16)
    lo = (h2 - hi.astype(F32)).astype(BF16)
    h2_ref[...] = _pack_bf16_pairs(h2)
    logit_ref[...] = _dot_nt(wrhi_ref[...], hi) + _dot_nt(wrhi_ref[...], lo) + _dot_nt(wrlo_ref[...], hi)
    gu = _dot(hi, wsgu_ref[...])
    shared = _dot((_silu(gu[:, :D_EXPERT]) * gu[:, D_EXPERT:]).astype(BF16), wsd_ref[...])
    xpart_ref[...] = x1 + mod[5:6] * shared


def _merge(ocmp, osel, owin, gn, yssm, u, ga, gs, x2, mod, d_skip, w_glu, b_glu, w_up_attn, w_up_ssm, w_out,
           g_norm2, w_router, ws_gate, ws_up, ws_down):
    tm = MERGE_TM
    eg = np.zeros((3, LANES, NSA_WIDTH), np.float32)
    for j in range(3):
        for h in range(N_HEADS):
            eg[j, 3 * h + j, h * HEAD_DIM:(h + 1) * HEAD_DIM] = 1.0
    wr_t = w_router.T
    wr_hi = wr_t.astype(BF16)
    wr_lo = (wr_t - wr_hi.astype(F32)).astype(BF16)
    row = lambda i: (i, 0)
    fix2 = lambda i: (0, 0)
    wspec = lambda a: pl.BlockSpec(a.shape, (lambda i: (0,) * a.ndim))
    weights = [jnp.asarray(eg, BF16), d_skip.reshape(1, -1), w_glu.astype(BF16), b_glu.reshape(1, -1),
               w_up_attn.astype(BF16), w_up_ssm.astype(BF16), w_out.astype(BF16), g_norm2.reshape(1, -1),
               wr_hi, wr_lo, jnp.concatenate([ws_gate, ws_up], axis=1).astype(BF16), ws_down.astype(BF16)]
    acts = [(ocmp, 512), (osel, 512), (owin, 512), (gn, 128), (yssm, 512), (u, 512), (ga, 1024), (gs, 1024),
            (x2, 1024)]
    return pl.pallas_call(
        _merge_kernel,
        grid=(N_TOK // tm,),
        in_specs=[pl.BlockSpec((tm, wd), row) for _, wd in acts]
                 + [pl.BlockSpec((1, 6, D_MODEL), lambda i: (i // (SEQ // tm), 0, 0))]
                 + [wspec(w) for w in weights],
        out_specs=[pl.BlockSpec((tm, D_MODEL), row), pl.BlockSpec((tm, HALF), row),
                   pl.BlockSpec((N_EXPERTS, tm), lambda i: (0, i))],
        out_shape=[jax.ShapeDtypeStruct((N_TOK, D_MODEL), F32), jax.ShapeDtypeStruct((N_TOK, HALF), I32),
                   jax.ShapeDtypeStruct((N_EXPERTS, N_TOK), F32)],
        compiler_params=_cparams("arbitrary", vmem=56 * 1024 * 1024),
        name="merge",
    )(*[a for a, _ in acts], mod, *weights)


ROUTE_TN = 512


def _route_kernel(logit_ref, bias_ref, eidx_ref, w_ref, count_ref, gscore_ref, masked_ref):
    tn = ROUTE_TN

    @pl.when(pl.program_id(0) == 0)
    def _():
        count_ref[...] = jnp.zeros(count_ref.shape, F32)

    sc = jax.nn.sigmoid(logit_ref[...])
    biased = sc + bias_ref[...]
    gi = lax.broadcasted_iota(I32, (EXPERTS_PER_GROUP, tn), 0).astype(F32)
    for g in range(N_EXPERT_GROUPS):
        blk = biased[g * EXPERTS_PER_GROUP:(g + 1) * EXPERTS_PER_GROUP]
        m1 = jnp.max(blk, axis=0, keepdims=True)
        i1 = jnp.min(jnp.where(blk == m1, gi, float(EXPERTS_PER_GROUP)), axis=0, keepdims=True)
        m2 = jnp.max(jnp.where(gi == i1, -jnp.inf, blk), axis=0, keepdims=True)
        gscore_ref[g:g + 1, :] = m1 + m2
    gs = gscore_ref[...]
    gidx = lax.broadcasted_iota(I32, (N_EXPERT_GROUPS, tn), 0)
    grank = jnp.zeros((N_EXPERT_GROUPS, tn), F32)
    for gp in range(N_EXPERT_GROUPS):
        row = gs[gp:gp + 1, :]
        tie = jnp.where(gidx > gp, 1.0, 0.0)
        grank = grank + jnp.where(row > gs, 1.0, jnp.where(row == gs, tie, 0.0))
    for g in range(N_EXPERT_GROUPS):
        keep = grank[g:g + 1, :] < float(TOPK_GROUPS)
        sl = slice(g * EXPERTS_PER_GROUP, (g + 1) * EXPERTS_PER_GROUP)
        masked_ref[sl, :] = jnp.where(keep, biased[sl], -jnp.inf)
    cur = masked_ref[...]
    eidx = lax.broadcasted_iota(I32, (N_EXPERTS, tn), 0).astype(F32)
    wsum = jnp.zeros((1, tn), F32)
    hits = jnp.zeros((N_EXPERTS, tn), F32)
    for k in range(TOP_K):
        m = jnp.max(cur, axis=0, keepdims=True)
        idx = jnp.min(jnp.where(cur == m, eidx, float(N_EXPERTS)), axis=0, keepdims=True)
        hit = eidx == idx
        wk = jnp.sum(jnp.where(hit, sc, 0.0), axis=0, keepdims=True)
        cur = jnp.where(hit, -jnp.inf, cur)
        hits = hits + jnp.where(hit, 1.0, 0.0)
        eidx_ref[k:k + 1, :] = idx.astype(I32)
        w_ref[k:k + 1, :] = wk
        wsum = wsum + wk
    w_ref[...] = w_ref[...] / wsum * ROUTE_SCALE
    count_ref[...] = count_ref[...] + jnp.sum(hits, axis=1, keepdims=True)


def _route(logits_t, router_bias):
    tn = ROUTE_TN
    return pl.pallas_call(
        _route_kernel,
        grid=(N_TOK // tn,),
        in_specs=[pl.BlockSpec((N_EXPERTS, tn), lambda i: (0, i)), pl.BlockSpec((N_EXPERTS, 1), lambda i: (0, 0))],
        out_specs=[pl.BlockSpec((TOP_K, tn), lambda i: (0, i))] * 2 + [pl.BlockSpec((N_EXPERTS, 1), lambda i: (0, 0))],
        out_shape=[jax.ShapeDtypeStruct((TOP_K, N_TOK), I32), jax.ShapeDtypeStruct((TOP_K, N_TOK), F32),
                   jax.ShapeDtypeStruct((N_EXPERTS, 1), F32)],
        scratch_shapes=[pltpu.VMEM((N_EXPERT_GROUPS, tn), F32), pltpu.VMEM((N_EXPERTS, tn), F32)],
        compiler_params=_cparams("arbitrary"),
        name="route",
    )(logits_t, router_bias.reshape(-1, 1))


N_MOE_BLK = NK // DISPATCH_BLOCK
N_ITEMS = N_MOE_BLK + N_EXPERTS
ASSIGN_BITS = 17


def _dispatch_plan(eidx, counts):
    e_flat = eidx.reshape(-1)
    key = jnp.sort(e_flat * NK + jnp.arange(NK, dtype=I32))
    order = key & (NK - 1)
    counts = counts.reshape(-1).astype(I32)
    start = jnp.cumsum(counts) - counts
    cuts = jnp.sort(jnp.concatenate([jnp.arange(N_MOE_BLK, dtype=I32) * DISPATCH_BLOCK, start]))
    lo = cuts
    hi = jnp.concatenate([cuts[1:], jnp.full((1,), NK, I32)])
    blk = jnp.minimum(lo // DISPATCH_BLOCK, N_MOE_BLK - 1)
    expert = jnp.clip(jnp.sum((start[None, :] <= lo[:, None]).astype(I32), axis=1) - 1, 0, N_EXPERTS - 1)
    one = jnp.ones((1,), I32)
    first = jnp.concatenate([one, (blk[1:] != blk[:-1]).astype(I32)])
    last = jnp.concatenate([(blk[1:] != blk[:-1]).astype(I32), one])
    new_expert = jnp.concatenate([one, (expert[1:] != expert[:-1]).astype(I32)])
    run_id = jnp.cumsum(new_expert) - 1
    n_runs = run_id[-1] + 1
    item = jnp.arange(N_ITEMS, dtype=I32)
    run_first_item = jnp.sort(jnp.where(new_expert == 1, item, N_ITEMS))
    run_expert = expert[jnp.minimum(run_first_item, N_ITEMS - 1)]
    ahead = run_id + (WEIGHT_RING - 1)
    ahead_expert = run_expert[jnp.minimum(ahead, N_ITEMS - 1)]
    ahead_valid = (ahead < n_runs).astype(I32)
    second_expert = run_expert[1:2]
    prologue = jnp.concatenate([second_expert, (n_runs > 1).astype(I32).reshape(1)])
    tok = jnp.right_shift(order, 3)
    home = (order & (TOP_K - 1)) * N_TOK + tok
    return tok, home, (blk, expert, lo - blk * DISPATCH_BLOCK, hi - blk * DISPATCH_BLOCK, first, last, new_expert,
                      run_id % WEIGHT_RING, ahead_expert, ahead_valid, prologue)


SC_CORES = 2
SC_SUBCORES = 16
SC_CHUNK = 128


def _sc_move_rows(table, idx, scatter):
    n = idx.shape[0]
    workers = SC_CORES * SC_SUBCORES
    per_worker = n // workers
    n_chunks = per_worker // SC_CHUNK
    assert per_worker * workers == n and n_chunks * SC_CHUNK == per_worker
    mesh = plsc.VectorSubcoreMesh(core_axis_name="c", subcore_axis_name="s",
                                  num_cores=SC_CORES, num_subcores=SC_SUBCORES)

    def body(table_hbm, idx_hbm, out_hbm, idx_v, rows_v, sem):
        wid = lax.axis_index("s") * SC_CORES + lax.axis_index("c")
        base = wid * per_worker

        @pl.loop(0, n_chunks)
        def _(j):
            off = base + j * SC_CHUNK
            pltpu.sync_copy(idx_hbm.at[pl.ds(off, SC_CHUNK)], idx_v)
            if scatter:
                pltpu.sync_copy(table_hbm.at[pl.ds(off, SC_CHUNK)], rows_v)
                pltpu.async_copy(rows_v, out_hbm.at[idx_v], sem).wait()
            else:
                pltpu.async_copy(table_hbm.at[idx_v], rows_v, sem).wait()
                pltpu.sync_copy(rows_v, out_hbm.at[pl.ds(off, SC_CHUNK)])

    return pl.kernel(
        body,
        out_type=jax.ShapeDtypeStruct((n, table.shape[1]), table.dtype),
        mesh=mesh,
        scratch_types=[pltpu.VMEM((SC_CHUNK,), I32), pltpu.VMEM((SC_CHUNK, table.shape[1]), table.dtype),
                       pltpu.SemaphoreType.DMA],
        name="sc_scatter_rows" if scatter else "sc_gather_rows",
    )(table, idx)


WEIGHT_RING = 3
WEIGHT_CHUNKS = 4


def _expert_weight_copies(w_hbm, wbuf, sem, expert, slot):
    rows = w_hbm.shape[1] // WEIGHT_CHUNKS
    return [pltpu.make_async_copy(w_hbm.at[expert, pl.ds(c * rows, rows)],
                                  wbuf.at[slot, pl.ds(c * rows, rows)], sem.at[slot])
            for c in range(WEIGHT_CHUNKS)]


def _moe_kernel(blk_ref, exp_ref, lo_ref, hi_ref, first_ref, last_ref, newexp_ref,
                slot_ref, ahead_exp_ref, ahead_ok_ref, prologue_ref,
                x_ref, wg_hbm, wu_hbm, wd_hbm, y_ref,
                acc_ref, wgf_ref, wuf_ref, wdf_ref, wgb_ref, wub_ref, wdb_ref, wsem):
    it = pl.program_id(0)
    lo, hi = lo_ref[it], hi_ref[it]
    streams = ((wg_hbm, wgf_ref), (wu_hbm, wuf_ref), (wd_hbm, wdf_ref))

    def request(expert, slot):
        for w_hbm, wbuf in streams:
            for cp in _expert_weight_copies(w_hbm, wbuf, wsem, expert, slot):
                cp.start()

    @pl.when(it == 0)
    def _():
        request(exp_ref[0], 0)

        @pl.when(prologue_ref[1] == 1)
        def _():
            request(prologue_ref[0], 1)

    @pl.when(newexp_ref[it] == 1)
    def _():
        slot = slot_ref[it]
        for w_hbm, wbuf in streams:
            for cp in _expert_weight_copies(w_hbm, wbuf, wsem, 0, slot):
                cp.wait()
        wgb_ref[...] = wgf_ref[slot].astype(BF16)
        wub_ref[...] = wuf_ref[slot].astype(BF16)
        wdb_ref[...] = wdf_ref[slot].astype(BF16)

        @pl.when(ahead_ok_ref[it] == 1)
        def _():
            ahead_slot = slot + (WEIGHT_RING - 1)
            request(ahead_exp_ref[it], jnp.where(ahead_slot >= WEIGHT_RING, ahead_slot - WEIGHT_RING, ahead_slot))

    @pl.when(first_ref[it] == 1)
    def _():
        acc_ref[...] = jnp.zeros(acc_ref.shape, F32)

    def expert_pass(r0, nrows):
        rows = slice(r0, r0 + nrows)
        ridx = r0 + lax.broadcasted_iota(I32, (nrows, HALF), 0)
        mine = (ridx >= lo) & (ridx < hi)
        xlo, xhi = _unpack_bf16_pairs(jnp.where(mine, x_ref[rows, :], 0))
        xlo, xhi = xlo.astype(BF16), xhi.astype(BF16)
        gate = _dot(xlo, wgb_ref[:HALF]) + _dot(xhi, wgb_ref[HALF:])
        up = _dot(xlo, wub_ref[:HALF]) + _dot(xhi, wub_ref[HALF:])
        acc_ref[rows, :] = acc_ref[rows, :] + _dot((_silu(gate) * up).astype(BF16), wdb_ref[...])

    mid = DISPATCH_BLOCK // 2
    pl.when((lo < mid) & (hi > mid))(lambda: expert_pass(0, DISPATCH_BLOCK))
    pl.when((hi > lo) & (hi <= mid))(lambda: expert_pass(0, mid))
    pl.when((hi > lo) & (lo >= mid))(lambda: expert_pass(mid, mid))

    @pl.when(last_ref[it] == 1)
    def _():
        y_ref[...] = _pack_bf16_pairs(acc_ref[...])


def _moe(xs, items, w_gate, w_up, w_down):
    by_blk = lambda it, blk, *_: (blk[it], 0)
    any_space = pl.BlockSpec(memory_space=pl.ANY)
    grid_spec = pltpu.PrefetchScalarGridSpec(
        num_scalar_prefetch=len(items),
        grid=(N_ITEMS,),
        in_specs=[pl.BlockSpec((DISPATCH_BLOCK, HALF), by_blk), any_space, any_space, any_space],
        out_specs=pl.BlockSpec((DISPATCH_BLOCK, HALF), by_blk),
        scratch_shapes=[pltpu.VMEM((DISPATCH_BLOCK, D_MODEL), F32),
                        pltpu.VMEM((WEIGHT_RING, D_MODEL, D_EXPERT), F32),
                        pltpu.VMEM((WEIGHT_RING, D_MODEL, D_EXPERT), F32),
                        pltpu.VMEM((WEIGHT_RING, D_EXPERT, D_MODEL), F32),
                        pltpu.VMEM((D_MODEL, D_EXPERT), BF16), pltpu.VMEM((D_MODEL, D_EXPERT), BF16),
                        pltpu.VMEM((D_EXPERT, D_MODEL), BF16),
                        pltpu.SemaphoreType.DMA((WEIGHT_RING,))],
    )
    return pl.pallas_call(
        _moe_kernel,
        grid_spec=grid_spec,
        out_shape=jax.ShapeDtypeStruct((NK, HALF), I32),
        compiler_params=_cparams("arbitrary"),
        name="moe",
    )(*items, xs, w_gate, w_up, w_down)


COMB_TC = 512


def _combine_kernel(slots_ref, w_ref, xpart_ref, mod_ref, out_ref):
    w = w_ref[...]
    lo = jnp.zeros((w.shape[0], HALF), F32)
    hi = jnp.zeros((w.shape[0], HALF), F32)
    for k in range(TOP_K):
        klo, khi = _unpack_bf16_pairs(slots_ref[k])
        lo = lo + w[:, k:k + 1] * klo
        hi = hi + w[:, k:k + 1] * khi
    gate2 = mod_ref[0][5:6]
    out_ref[:, :HALF] = xpart_ref[:, :HALF] + gate2[:, :HALF] * lo
    out_ref[:, HALF:] = xpart_ref[:, HALF:] + gate2[:, HALF:] * hi


def _combine(xpart, mod, slots, w):
    tc = COMB_TC
    row = lambda i: (i, 0)
    return pl.pallas_call(
        _combine_kernel,
        grid=(N_TOK // tc,),
        in_specs=[pl.BlockSpec((TOP_K, tc, HALF), lambda i: (0, i, 0)),
                  pl.BlockSpec((tc, TOP_K), row),
                  pl.BlockSpec((tc, D_MODEL), row),
                  pl.BlockSpec((1, 6, D_MODEL), lambda i: (i // (SEQ // tc), 0, 0))],
        out_specs=pl.BlockSpec((tc, D_MODEL), row),
        out_shape=jax.ShapeDtypeStruct((N_TOK, D_MODEL), F32),
        compiler_params=_cparams("arbitrary"),
        name="combine",
    )(slots.reshape(TOP_K, N_TOK, HALF), w, xpart, mod)


def _layer(x, c, w_ada, b_ada, g_norm1, g_norm2, w_in, q_gain, kc_gain, ks_gain, kw_gain,
           pe_k, pe_v, w_cmp_k1, w_cmp_k2, w_cmp_v1, w_cmp_v2,
           a_re, a_im, log_dt, b_re, b_im, c_re, c_im, d_skip, w_glu, b_glu,
           w_up_attn, w_up_ssm, w_out, w_router, router_bias,
           w_gate, w_up, w_down, ws_gate, ws_up, ws_down):
    x2 = x.reshape(N_TOK, D_MODEL)
    mod = _ada(c, w_ada, b_ada)
    q, kc_raw, vc_raw, ks, kw, vst, vwt, gn, u, ga, gs = _proj(x2, mod, g_norm1, w_in, q_gain, ks_gain, kw_gain)
    kcn = _compress(kc_raw, pe_k, w_cmp_k1, w_cmp_k2, kc_gain, True)
    vcn = _compress(vc_raw, pe_v, w_cmp_v1, w_cmp_v2, kc_gain, False)
    ocmp, selb = _cmp_attn(q, kcn, vcn)
    osel, owin = _selwin(q, ks, kw, vst, vwt, selb)
    yssm = _s5(u, *_s5_params(a_re, a_im, log_dt, b_re, b_im, c_re, c_im))
    xpart, h2, logits_t = _merge(ocmp, osel, owin, gn, yssm, u, ga, gs, x2, mod, d_skip, w_glu, b_glu,
                                  w_up_attn, w_up_ssm, w_out, g_norm2, w_router, ws_gate, ws_up, ws_down)
    eidx_t, w_t, counts = _route(logits_t, router_bias)
    tok, home, items = _dispatch_plan(eidx_t.T, counts)
    y = _moe(_sc_move_rows(h2, tok, scatter=False), items, w_gate, w_up, w_down)
    slots = _sc_move_rows(y, home, scatter=True)
    return _combine(xpart, mod, slots, w_t.T).reshape(BATCH, SEQ, D_MODEL)


def kernel(x, c, w_ada, b_ada, g_norm1, g_norm2, w_in, q_gain, kc_gain, ks_gain, kw_gain, pe_k, pe_v, w_cmp_k1,
           w_cmp_k2, w_cmp_v1, w_cmp_v2, a_re, a_im, log_dt, b_re, b_im, c_re, c_im, d_skip, w_glu, b_glu,
           w_up_attn, w_up_ssm, w_out, w_router, router_bias, w_gate, w_up, w_down, ws_gate, ws_up, ws_down):
    params = (w_ada, b_ada, g_norm1, g_norm2, w_in, q_gain, kc_gain, ks_gain, kw_gain, pe_k, pe_v, w_cmp_k1,
              w_cmp_k2, w_cmp_v1, w_cmp_v2, a_re, a_im, log_dt, b_re, b_im, c_re, c_im, d_skip, w_glu, b_glu,
              w_up_attn, w_up_ssm, w_out, w_router, router_bias, w_gate, w_up, w_down, ws_gate, ws_up, ws_down)
    depth = w_ada.shape[0]
    for layer in range(depth):
        x = _layer(x, c, *[p[layer] for p in params])
    return x
```

```python
import functools
import math

import jax
import jax.numpy as jnp
import numpy as np
from jax import lax
from jax.experimental import pallas as pl
from jax.experimental.pallas import tpu as pltpu
from jax.experimental.pallas import tpu_sc as plsc

F32 = jnp.float32
BF16 = jnp.bfloat16
I32 = jnp.int32
HIGHEST = lax.Precision.HIGHEST

D_MODEL = 1024
BATCH = 4
SEQ = 4096
N_TOK = BATCH * SEQ
N_HEADS = 8
HEAD_DIM = 64
N_KV = 2
CMP_BLOCK = 32
CMP_STRIDE = 16
CMP_HIDDEN = 256
N_CMP = 256
SEL_BLOCK = 64
N_SEL_BLOCKS = SEQ // SEL_BLOCK
N_SELECT = 16
WINDOW = 512
ATTN_SCALE = HEAD_DIM ** -0.5
LOG2E = 1.4426950408889634
NSA_WIDTH = N_HEADS * HEAD_DIM
SSM_WIDTH = 512
GROUP = 16
N_GROUPS = SSM_WIDTH // GROUP
STATE = 64
N_EXPERTS = 256
TOP_K = 8
D_EXPERT = 256
N_EXPERT_GROUPS = 8
EXPERTS_PER_GROUP = N_EXPERTS // N_EXPERT_GROUPS
TOPK_GROUPS = 4
ROUTE_SCALE = 2.5
DISPATCH_BLOCK = 512
EPS = 1e-6
NEG = -1e30

LANES = 128
S5_T = 16
S5_SG = 4
S5_GL = N_GROUPS // S5_SG
S5_CH = N_TOK // S5_T
S5_CH_PER_BATCH = SEQ // S5_T
S5_NSTATE = S5_GL * STATE * 2

NK = N_TOK * TOP_K
HALF = D_MODEL // 2

VMEM_LIMIT = 48 * 1024 * 1024


def _cparams(*sem, vmem=VMEM_LIMIT):
    return pltpu.CompilerParams(dimension_semantics=tuple(sem), vmem_limit_bytes=vmem)


def _dot(a, b):
    return jnp.dot(a, b, preferred_element_type=F32)


def _dot_nt(a, b):
    return lax.dot_general(a, b, (((1,), (1,)), ((), ())), preferred_element_type=F32)


def _split_dot(v, w):
    hi = v.astype(BF16)
    lo = (v - hi.astype(F32)).astype(BF16)
    return _dot(hi, w) + _dot(lo, w)


def _seg_rms(v, bd, gain):
    ss = _split_dot(v * v, bd)
    return v * lax.rsqrt(ss * (1.0 / HEAD_DIM) + EPS) * gain


def _gelu(x):
    return 0.5 * x * (1.0 + jnp.tanh(0.7978845608028654 * (x + 0.044715 * (x * x * x))))


def _silu(x):
    return x * jax.nn.sigmoid(x)


def _pack_bf16_pairs(v):
    h = v.shape[1] // 2
    return pltpu.pack_elementwise([v[:, :h], v[:, h:]], packed_dtype=BF16)


def _unpack_bf16_pairs(word):
    return (pltpu.unpack_elementwise(word, index=0, packed_dtype=BF16, unpacked_dtype=F32),
            pltpu.unpack_elementwise(word, index=1, packed_dtype=BF16, unpacked_dtype=F32))


def _ada_kernel(c_ref, w_ref, b_ref, o_ref):
    c = c_ref[...]
    o_ref[...] = jnp.dot(_silu(c), w_ref[...], preferred_element_type=F32, precision=HIGHEST) + b_ref[...]


def _ada(c, w_ada, b_ada):
    cp = jnp.pad(c, ((0, 8 - BATCH), (0, 0)))
    tn = 1536
    out = pl.pallas_call(
        _ada_kernel,
        grid=(6 * D_MODEL // tn,),
        in_specs=[pl.BlockSpec((8, D_MODEL), lambda j: (0, 0)),
                  pl.BlockSpec((D_MODEL, tn), lambda j: (0, j)),
                  pl.BlockSpec((1, tn), lambda j: (0, j))],
        out_specs=pl.BlockSpec((8, tn), lambda j: (0, j)),
        out_shape=jax.ShapeDtypeStruct((8, 6 * D_MODEL), F32),
        compiler_params=_cparams("arbitrary"),
        name="ada",
    )(cp, w_ada, b_ada.reshape(1, -1))
    return out.reshape(8, 6, D_MODEL)


_C_Q = 0
_C_KC = 512
_C_VC = 640
_C_KS = 768
_C_KW = 1024
_C_GN = 1280
_C_U = 1408
_C_GA = 1920
_C_GS = 2944
_C_END = 3968
PROJ_TM = 512


def _proj_kernel(x_ref, mod_ref, g1_ref, w_ref, wvt_ref, qg_ref, ksg_ref, kwg_ref, bd512_ref, bd256_ref,
                 q_ref, kc_ref, vc_ref, ks_ref, kw_ref, vst_ref, vwt_ref, gn_ref, u_ref, ga_ref, gs_ref):
    x = x_ref[...]
    ms = jnp.mean(x * x, axis=-1, keepdims=True)
    mod = mod_ref[0]
    h = (x * lax.rsqrt(ms + EPS) * g1_ref[...]) * (1.0 + mod[1:2]) + mod[0:1]
    hb = h.astype(BF16)

    def p(lo, hi):
        return _dot(hb, w_ref[:, lo:hi])

    q_ref[...] = _seg_rms(p(_C_Q, _C_KC), bd512_ref[...], qg_ref[...] * (ATTN_SCALE * LOG2E)).astype(BF16)
    kc_ref[...] = p(_C_KC, _C_VC)
    vc_ref[...] = p(_C_VC, _C_KS)
    ks_ref[...] = _seg_rms(p(_C_KS, _C_KW), bd256_ref[...], ksg_ref[...]).astype(BF16)
    kw_ref[...] = _seg_rms(p(_C_KW, _C_GN), bd256_ref[...], kwg_ref[...]).astype(BF16)
    vt = _dot_nt(wvt_ref[...], hb)
    vst_ref[...] = vt[:LANES].astype(BF16)
    vwt_ref[...] = vt[LANES:].astype(BF16)
    gn_ref[...] = jax.nn.sigmoid(p(_C_GN, _C_U))
    u_ref[...] = p(_C_U, _C_GA)
    ga_ref[...] = jax.nn.sigmoid(p(_C_GA, _C_GS)).astype(BF16)
    gs_ref[...] = jax.nn.sigmoid(p(_C_GS, _C_END)).astype(BF16)


def _dup_cols(w):
    return jnp.concatenate([w[:, :64], w[:, :64], w[:, 64:], w[:, 64:]], axis=1)


def _block_ones(n):
    return jnp.kron(jnp.eye(n // HEAD_DIM, dtype=F32), jnp.ones((HEAD_DIM, HEAD_DIM), F32)).astype(BF16)


def _proj(x2, mod, g_norm1, w_in, q_gain, ks_gain, kw_gain):
    o = np.cumsum((0, 512, 128, 128, 128, 128, 128, 128, 24, 512, 1024, 1024))
    parts = [w_in[:, o[i]:o[i + 1]] for i in range(11)]
    wq, wkc, wvc, wks, wvs, wkw, wvw, wgn, wu, wga, wgs = parts
    w = jnp.concatenate([wq, wkc, wvc, _dup_cols(wks), _dup_cols(wkw),
                         jnp.pad(wgn, ((0, 0), (0, LANES - 24))), wu, wga, wgs], axis=1).astype(BF16)
    wvt = jnp.concatenate([wvs, wvw], axis=1).T.astype(BF16)
    tm = PROJ_TM
    row = lambda i: (i, 0)
    col = lambda i: (0, i)
    fix = lambda i: (0, 0)
    outs = [(512, BF16, row), (128, F32, row), (128, F32, row), (256, BF16, row), (256, BF16, row),
            (LANES, BF16, col), (LANES, BF16, col),
            (128, F32, row), (512, F32, row), (1024, BF16, row), (1024, BF16, row)]
    ospec = lambda wd, m: pl.BlockSpec((tm, wd), m) if m is row else pl.BlockSpec((wd, tm), m)
    oshape = lambda wd, dt, m: jax.ShapeDtypeStruct((N_TOK, wd) if m is row else (wd, N_TOK), dt)
    return pl.pallas_call(
        _proj_kernel,
        grid=(N_TOK // tm,),
        in_specs=[pl.BlockSpec((tm, D_MODEL), row),
                  pl.BlockSpec((1, 6, D_MODEL), lambda i: (i // (SEQ // tm), 0, 0)),
                  pl.BlockSpec((1, D_MODEL), fix),
                  pl.BlockSpec((D_MODEL, _C_END), fix),
                  pl.BlockSpec((2 * LANES, D_MODEL), fix),
                  pl.BlockSpec((1, 512), fix), pl.BlockSpec((1, 256), fix), pl.BlockSpec((1, 256), fix),
                  pl.BlockSpec((512, 512), fix), pl.BlockSpec((256, 256), fix)],
        out_specs=[ospec(wd, m) for wd, _, m in outs],
        out_shape=[oshape(wd, dt, m) for wd, dt, m in outs],
        compiler_params=_cparams("arbitrary"),
        name="proj",
    )(x2, mod, g_norm1.reshape(1, -1), w, wvt,
      jnp.tile(q_gain, N_HEADS).reshape(1, -1), jnp.tile(ks_gain, 4).reshape(1, -1),
      jnp.tile(kw_gain, 4).reshape(1, -1), _block_ones(512), _block_ones(256))


def _compress_kernel(r_ref, pe_ref, w1_ref, w2_ref, bd_ref, gain_ref, o_ref, *, do_norm):
    r = jnp.concatenate([r_ref[0, :, l, :] for l in range(CMP_STRIDE)], axis=1)
    p0 = _dot((r + pe_ref[0]).astype(BF16), w1_ref[0])
    p1 = _dot((r + pe_ref[1]).astype(BF16), w1_ref[1])
    hid = p0 + pltpu.roll(p1, N_CMP - 1, 0)
    c = _dot(_gelu(hid).astype(BF16), w2_ref[...])
    if do_norm:
        c = _seg_rms(c, bd_ref[...], gain_ref[...])
    o_ref[0] = c.astype(BF16)


def _compress(raw, pe, w1, w2, gain, do_norm):
    r = raw.reshape(BATCH, SEQ // CMP_STRIDE, CMP_STRIDE, LANES)
    eye = jnp.eye(N_KV, dtype=F32)
    w1r = w1.reshape(2, CMP_STRIDE, HEAD_DIM, CMP_HIDDEN)
    w1big = jnp.einsum('hldc,gk->hlgdkc', w1r, eye).reshape(2, CMP_STRIDE * LANES, N_KV * CMP_HIDDEN).astype(BF16)
    w2big = jnp.einsum('cd,gk->gckd', w2, eye)
    w2big = jnp.concatenate([w2big, w2big], axis=-1).reshape(N_KV * CMP_HIDDEN, 4 * HEAD_DIM).astype(BF16)
    pe_big = jnp.broadcast_to(pe.reshape(2, CMP_STRIDE, 1, HEAD_DIM), (2, CMP_STRIDE, N_KV, HEAD_DIM))
    pe_big = pe_big.reshape(2, 1, CMP_STRIDE * LANES)
    fix2 = lambda b: (0, 0)
    fix3 = lambda b: (0, 0, 0)
    return pl.pallas_call(
        functools.partial(_compress_kernel, do_norm=do_norm),
        grid=(BATCH,),
        in_specs=[pl.BlockSpec((1, N_CMP, CMP_STRIDE, LANES), lambda b: (b, 0, 0, 0)),
                  pl.BlockSpec((2, 1, CMP_STRIDE * LANES), fix3),
                  pl.BlockSpec((2, CMP_STRIDE * LANES, N_KV * CMP_HIDDEN), fix3),
                  pl.BlockSpec((N_KV * CMP_HIDDEN, 256), fix2),
                  pl.BlockSpec((256, 256), fix2), pl.BlockSpec((1, 256), fix2)],
        out_specs=pl.BlockSpec((1, N_CMP, 256), lambda b: (b, 0, 0)),
        out_shape=jax.ShapeDtypeStruct((BATCH, N_CMP, 256), BF16),
        compiler_params=_cparams("arbitrary"),
        name="compress_k" if do_norm else "compress_v",
    )(r, pe_big, w1big, w2big, _block_ones(256), jnp.tile(gain, 4).reshape(1, -1))


ATT_TQ = 256
RANK_CHUNK = 16


def _head_variants(qb):
    lane = lax.broadcasted_iota(I32, qb.shape, 1)
    z = jnp.zeros_like(qb)
    return jnp.where(lane < HEAD_DIM, qb, z), jnp.where(lane < HEAD_DIM, z, qb)


def _cmp_kernel(q_ref, kc_ref, vc_ref, ov_ref, o_ref, sel_ref, vrank_ref):
    tq = ATT_TQ
    qi = pl.program_id(1)
    tpos = qi * tq + lax.broadcasted_iota(I32, (tq, N_CMP), 0)
    nidx = lax.broadcasted_iota(I32, (tq, N_CMP), 1)
    mask = (CMP_STRIDE * nidx + (CMP_BLOCK - 1)) <= tpos
    lane_lo = lax.broadcasted_iota(I32, (tq, LANES), 1) < HEAD_DIM
    for g in range(N_KV):
        kd = kc_ref[0, :, g * LANES:(g + 1) * LANES]
        vd = vc_ref[0, :, g * LANES:(g + 1) * LANES]
        psum = jnp.zeros((tq, N_CMP), F32)
        for jb in range(2):
            blk = 2 * g + jb
            pv = []
            for qv in _head_variants(q_ref[:, blk * LANES:(blk + 1) * LANES]):
                s = jnp.where(mask, _dot_nt(qv, kd), NEG)
                m = jnp.max(s, axis=-1, keepdims=True)
                e = jnp.where(mask, jnp.exp2(s - m), 0.0)
                l = jnp.sum(e, axis=-1, keepdims=True)
                p = e / jnp.where(l > 0.0, l, 1.0)
                psum = psum + p
                pv.append(_dot(p.astype(BF16), vd))
            o_ref[:, blk * LANES:(blk + 1) * LANES] = jnp.where(lane_lo, pv[0], pv[1]).astype(BF16)
        imp = _split_dot(psum, ov_ref[...])
        imp_t = imp.T[:N_SEL_BLOCKS]
        j = lax.broadcasted_iota(I32, (N_SEL_BLOCKS, tq), 0)
        cur = jnp.right_shift(qi * tq + lax.broadcasted_iota(I32, (N_SEL_BLOCKS, tq), 1), 6)
        forced = (j == 0) | (j == cur) | (j == cur - 1)
        v = jnp.where(forced, jnp.inf, jnp.where(j <= cur, imp_t, -jnp.inf))
        vrank_ref[...] = jnp.zeros((N_SEL_BLOCKS, tq), F32)
        n_live = (qi + 1) * (tq // SEL_BLOCK)
        for c0 in range(0, N_SEL_BLOCKS, RANK_CHUNK):
            @pl.when(c0 < n_live)
            def _():
                rank = vrank_ref[...]
                for jp in range(c0, c0 + RANK_CHUNK):
                    row = v[jp:jp + 1, :]
                    tie = jnp.where(j > jp, 1.0, 0.0)
                    rank = rank + jnp.where(row > v, 1.0, jnp.where(row == v, tie, 0.0))
                vrank_ref[...] = rank
        rank = vrank_ref[...]
        sel_ref[g * N_SEL_BLOCKS:(g + 1) * N_SEL_BLOCKS, :] = jnp.where(rank < float(N_SELECT), 0.0, NEG)


def _cmp_attn(q, kcn, vcn):
    nc = np.arange(N_CMP)
    sb = np.arange(LANES)
    ov = ((CMP_STRIDE * nc[:, None] < SEL_BLOCK * sb[None, :] + SEL_BLOCK)
          & (CMP_STRIDE * nc[:, None] + CMP_BLOCK > SEL_BLOCK * sb[None, :])
          & (nc[:, None] < N_CMP - 1) & (sb[None, :] < N_SEL_BLOCKS))
    ov = jnp.asarray(ov, BF16)
    tq = ATT_TQ
    nq = SEQ // tq
    row = lambda b, i: (b * nq + i, 0)
    return pl.pallas_call(
        _cmp_kernel,
        grid=(BATCH, nq),
        in_specs=[pl.BlockSpec((tq, NSA_WIDTH), row),
                  pl.BlockSpec((1, N_CMP, 256), lambda b, i: (b, 0, 0)),
                  pl.BlockSpec((1, N_CMP, 256), lambda b, i: (b, 0, 0)),
                  pl.BlockSpec((N_CMP, LANES), lambda b, i: (0, 0))],
        out_specs=[pl.BlockSpec((tq, NSA_WIDTH), row),
                   pl.BlockSpec((N_KV * N_SEL_BLOCKS, tq), lambda b, i: (0, b * nq + i))],
        out_shape=[jax.ShapeDtypeStruct((N_TOK, NSA_WIDTH), BF16),
                   jax.ShapeDtypeStruct((N_KV * N_SEL_BLOCKS, N_TOK), F32)],
        scratch_shapes=[pltpu.VMEM((N_SEL_BLOCKS, tq), F32)],
        compiler_params=_cparams("arbitrary", "arbitrary"),
        name="cmp_attn",
    )(q, kcn, vcn, ov)


ATT_TK = 256


M_INIT = -1e29


SUM_ROWS = 16
SW_TQ = 512
SEL_TK = 512


def _selwin_kernel(q_ref, ks_ref, kw_ref, vst_ref, vwt_ref, selb_ref, osel_ref, owin_ref, m_ref, acc_ref):
    tq = tk = SW_TQ
    qi = pl.program_id(1)
    krow = lax.broadcasted_iota(I32, (tk, tq), 0)
    qcol = lax.broadcasted_iota(I32, (tk, tq), 1)
    causal_bias = jnp.where(krow <= qcol, 0.0, NEG)
    far_bias = jnp.where(qcol < krow, 0.0, NEG)

    def reset():
        m_ref[...] = jnp.full(m_ref.shape, M_INIT, F32)
        acc_ref[...] = jnp.zeros(acc_ref.shape, F32)

    def update(g, k_ref, vt_ref, kt, bias, nk=tk):
        k0 = pl.multiple_of(kt * nk, nk)
        kd = k_ref[0, pl.ds(k0, nk), g * LANES:(g + 1) * LANES]
        vt = vt_ref[g * HEAD_DIM:(g + 1) * HEAD_DIM, pl.ds(k0, nk)]
        s = _dot_nt(kd, qvars[g])
        if bias is not None:
            s = s + jnp.concatenate([bias] * 4, axis=1)
        m_old = m_ref[g]
        m_new = jnp.maximum(m_old, jnp.max(s, axis=0, keepdims=True))
        alpha = jnp.exp2(m_old - m_new)
        p = jnp.exp2(s - m_new)
        m_ref[g] = m_new
        vte = jnp.concatenate([vt, jnp.ones((SUM_ROWS, nk), BF16)], axis=0)
        acc_ref[g] = alpha * acc_ref[g] + _dot(vte, p.astype(BF16))

    def finish(out_ref, g):
        o = acc_ref[g, :HEAD_DIM, :] / acc_ref[g, HEAD_DIM:HEAD_DIM + 1, :]
        for jb in range(2):
            blk = 2 * g + jb
            pair = jnp.concatenate([o[:, 2 * jb * tq:(2 * jb + 1) * tq], o[:, (2 * jb + 1) * tq:(2 * jb + 2) * tq]],
                                   axis=0)
            out_ref[:, blk * LANES:(blk + 1) * LANES] = pair.T.astype(BF16)

    def sel_bias(g, kt):
        rows = [jnp.broadcast_to(selb_ref[pl.ds(g * N_SEL_BLOCKS + kt * (SEL_TK // SEL_BLOCK) + r, 1), :],
                                 (SEL_BLOCK, tq)) for r in range(SEL_TK // SEL_BLOCK)]
        return jnp.concatenate(rows, axis=0)

    qvars = []
    for g in range(N_KV):
        heads = []
        for jb in range(2):
            heads.extend(_head_variants(q_ref[:, (2 * g + jb) * LANES:(2 * g + jb + 1) * LANES]))
        qvars.append(jnp.concatenate(heads, axis=0))
    groups = range(N_KV)

    reset()
    last_tile = qi // (SEL_TK // tq)

    def sel_step(kt, carry):
        for g in groups:
            update(g, ks_ref, vst_ref, kt, sel_bias(g, kt), SEL_TK)
        return carry

    lax.fori_loop(0, last_tile, sel_step, 0)
    q_first = qi * tq - last_tile * SEL_TK
    visible = (lax.broadcasted_iota(I32, (SEL_TK, tq), 0)
               <= lax.broadcasted_iota(I32, (SEL_TK, tq), 1) + q_first)
    diag_bias = jnp.where(visible, 0.0, NEG)
    for g in groups:
        update(g, ks_ref, vst_ref, last_tile, sel_bias(g, last_tile) + diag_bias, SEL_TK)
    for g in groups:
        finish(osel_ref, g)

    reset()
    back = WINDOW // tk

    @pl.when(qi >= back)
    def _():
        for g in groups:
            update(g, kw_ref, vwt_ref, qi - back, far_bias)

    for d in range(back - 1, 0, -1):
        @pl.when(qi >= d)
        def _():
            for g in groups:
                update(g, kw_ref, vwt_ref, qi - d, None)

    for g in groups:
        update(g, kw_ref, vwt_ref, qi, causal_bias)
    for g in groups:
        finish(owin_ref, g)


def _selwin(q, ks, kw, vst, vwt, selb):
    tq = SW_TQ
    nq = SEQ // tq
    assert WINDOW % SW_TQ == 0 and SEL_TK % SW_TQ == 0
    row = lambda b, i: (b * nq + i, 0)
    keys = pl.BlockSpec((1, SEQ, 256), lambda b, i: (b, 0, 0))
    vals = pl.BlockSpec((LANES, SEQ), lambda b, i: (0, b))
    r3 = lambda a: a.reshape(BATCH, SEQ, 256)
    return pl.pallas_call(
        _selwin_kernel,
        grid=(BATCH, nq),
        in_specs=[pl.BlockSpec((tq, NSA_WIDTH), row), keys, keys, vals, vals,
                  pl.BlockSpec((N_KV * N_SEL_BLOCKS, tq), lambda b, i: (0, b * nq + i))],
        out_specs=[pl.BlockSpec((tq, NSA_WIDTH), row)] * 2,
        out_shape=[jax.ShapeDtypeStruct((N_TOK, NSA_WIDTH), BF16)] * 2,
        scratch_shapes=[pltpu.VMEM((N_KV, 1, 4 * tq), F32),
                        pltpu.VMEM((N_KV, HEAD_DIM + SUM_ROWS, 4 * tq), F32)],
        compiler_params=_cparams("arbitrary", "arbitrary"),
        name="selwin",
    )(q, r3(ks), r3(kw), vst, vwt, selb)


def _s5_param_kernel(are_ref, aim_ref, ldt_ref, cre_ref, cim_ref, bre_ref, bim_ref,
                     clre_ref, clim_ref, wbre_ref, wbim_ref, bbre_ref, bbim_ref, ltre_ref, ltim_ref):
    are, aim = are_ref[...], aim_ref[...]
    dt = jnp.exp(ldt_ref[...])
    cre, cim = cre_ref[...], cim_ref[...]

    def lam_pow(tau):
        mag = jnp.exp(are * dt * float(tau))
        ang = aim * dt * float(tau)
        return mag * jnp.cos(ang), mag * jnp.sin(ang)

    lre, lim = lam_pow(1)
    den = are * are + aim * aim
    qre = ((lre - 1.0) * are + lim * aim) / den
    qim = (lim * are - (lre - 1.0) * aim) / den
    bre, bim = bre_ref[...], bim_ref[...]
    bbre = qre * bre - qim * bim
    bbim = qre * bim + qim * bre
    bbre_ref[...] = bbre
    bbim_ref[...] = bbim
    for tau in range(S5_T + 1):
        pr, pi = lam_pow(tau)
        clre_ref[tau] = cre * pr - cim * pi
        clim_ref[tau] = cre * pi + cim * pr
        if tau < S5_T:
            k = S5_T - 1 - tau
            wbre_ref[k] = pr * bbre - pi * bbim
            wbim_ref[k] = pr * bbim + pi * bbre
        else:
            ltre_ref[...] = pr
            ltim_ref[...] = pi


def _s5_kmat_kernel(l_ref, r_ref, o_ref):
    o_ref[0] = jnp.dot(l_ref[0], r_ref[0], preferred_element_type=F32, precision=HIGHEST)


def _s5_params(a_re, a_im, log_dt, b_re, b_im, c_re, c_im):
    T = S5_T
    pn = GROUP * STATE
    tile_p = lambda a: jnp.tile(a, (1, GROUP))
    args = (tile_p(a_re), tile_p(a_im), jnp.broadcast_to(log_dt[:, None], (N_GROUPS, pn)),
            c_re.reshape(N_GROUPS, pn), c_im.reshape(N_GROUPS, pn),
            jnp.swapaxes(b_re, 1, 2).reshape(N_GROUPS, pn), jnp.swapaxes(b_im, 1, 2).reshape(N_GROUPS, pn))
    full2 = pl.BlockSpec((N_GROUPS, pn), lambda: (0, 0))
    clre, clim, wbre, wbim, bbre, bbim, ltre, ltim = pl.pallas_call(
        _s5_param_kernel,
        in_specs=[full2] * 7,
        out_specs=[pl.BlockSpec((T + 1, N_GROUPS, pn), lambda: (0, 0, 0))] * 2
                  + [pl.BlockSpec((T, N_GROUPS, pn), lambda: (0, 0, 0))] * 2 + [full2] * 4,
        out_shape=[jax.ShapeDtypeStruct((T + 1, N_GROUPS, pn), F32)] * 2
                  + [jax.ShapeDtypeStruct((T, N_GROUPS, pn), F32)] * 2
                  + [jax.ShapeDtypeStruct((N_GROUPS, pn), F32)] * 4,
        name="s5_params",
    )(*args)

    r5 = lambda a, t: a[:t].reshape(t, N_GROUPS, GROUP, STATE)
    lhs = jnp.concatenate([r5(clre, T), -r5(clim, T)], axis=-1)
    lhs = jnp.transpose(lhs, (1, 0, 2, 3)).reshape(N_GROUPS, T * GROUP, 2 * STATE)
    bb = lambda a: jnp.swapaxes(a.reshape(N_GROUPS, GROUP, STATE), 1, 2)
    rhs = jnp.concatenate([bb(bbre), bb(bbim)], axis=1)
    kmat = pl.pallas_call(
        _s5_kmat_kernel,
        grid=(N_GROUPS,),
        in_specs=[pl.BlockSpec((1, T * GROUP, 2 * STATE), lambda g: (g, 0, 0)),
                  pl.BlockSpec((1, 2 * STATE, GROUP), lambda g: (g, 0, 0))],
        out_specs=pl.BlockSpec((1, T * GROUP, GROUP), lambda g: (g, 0, 0)),
        out_shape=jax.ShapeDtypeStruct((N_GROUPS, T * GROUP, GROUP), F32),
        compiler_params=_cparams("arbitrary"),
        name="s5_kmat",
    )(lhs, rhs)

    eye = jnp.eye(S5_GL, dtype=F32)
    kt = kmat.reshape(S5_SG, S5_GL, T, GROUP, GROUP)
    kbd = jnp.einsum('sgtpq,gh->stgqhp', kt, eye).reshape(S5_SG, T, LANES, LANES)
    krev = kbd[:, ::-1].reshape(S5_SG, T * LANES, LANES).astype(BF16)
    krev = jnp.pad(krev, ((0, 0), (0, LANES), (0, 0)))
    r6 = lambda a: a.reshape(T, S5_SG, S5_GL, GROUP, STATE)
    wb = jnp.stack([r6(wbre), r6(wbim)], axis=-2)
    wb = jnp.einsum('ksgpin,gh->skgpihn', wb, eye).reshape(S5_SG, T * LANES, S5_NSTATE).astype(BF16)
    wc = jnp.stack([r6(clre[1:]), -r6(clim[1:])], axis=-2)
    wc = jnp.einsum('tsgpin,gh->signthp', wc, eye).reshape(S5_SG, S5_NSTATE, T * LANES).astype(BF16)
    lt = lambda a: a.reshape(N_GROUPS, GROUP, STATE)[:, 0].reshape(S5_SG, 1, S5_GL * STATE)
    return krev, wb, wc, lt(ltre), lt(ltim)


S5_TC = 256


def _s5_lane_block(sg):
    return pl.ds(pl.multiple_of(sg * LANES, LANES), LANES)


def _s5_chunk_inputs(x_ref, sg):
    return jnp.concatenate([x_ref[:, t, _s5_lane_block(sg)] for t in range(S5_T)], axis=1).astype(BF16)


def _s5_state_kernel(x_ref, wb_ref, e_ref):
    e_ref[0] = _dot(_s5_chunk_inputs(x_ref, pl.program_id(1)), wb_ref[0])


def _s5_scan_kernel(e_ref, ltre_ref, ltim_ref, xs_ref):
    lr, li = ltre_ref[0], ltim_ref[0]
    half = S5_NSTATE // 2

    def step(c, carry):
        new = []
        for b in range(BATCH):
            xr, xi = carry[b]
            row = b * S5_CH_PER_BATCH + c
            xs_ref[0, pl.ds(row, 1), :half] = xr
            xs_ref[0, pl.ds(row, 1), half:] = xi
            e = e_ref[0, pl.ds(row, 1), :]
            new.append((lr * xr - li * xi + e[:, :half], lr * xi + li * xr + e[:, half:]))
        return tuple(new)

    zero = jnp.zeros((1, half), F32)
    lax.fori_loop(0, S5_CH_PER_BATCH, step, tuple((zero, zero) for _ in range(BATCH)))


def _s5_out_kernel(x_ref, xs_ref, krev_ref, wc_ref, y_ref):
    sg = pl.program_id(1)
    x = _s5_chunk_inputs(x_ref, sg)
    xsb = xs_ref[0].astype(BF16)
    for t in range(0, S5_T, 2):
        n_in = (t + 2) * LANES
        first = (S5_T - 1 - t) * LANES
        taps = jnp.concatenate([krev_ref[0, first:first + n_in, :], krev_ref[0, first - LANES:first - LANES + n_in, :]],
                               axis=1)
        pair = _dot(x[:, :n_in], taps) + _dot(xsb, wc_ref[0, :, t * LANES:(t + 2) * LANES])
        y_ref[:, t, _s5_lane_block(sg)] = pair[:, :LANES]
        y_ref[:, t + 1, _s5_lane_block(sg)] = pair[:, LANES:]


def _s5(u, krev, wb, wc, ltre, ltim):
    T, tc = S5_T, S5_TC
    xn = u.reshape(S5_CH, T, SSM_WIDTH)
    grid = (S5_CH // tc, S5_SG)
    natural = pl.BlockSpec((tc, T, SSM_WIDTH), lambda i, s: (i, 0, 0))
    rows = lambda i, s: (s, i, 0)
    per_sg = lambda i, s: (s, 0, 0)
    e = pl.pallas_call(
        _s5_state_kernel, grid=grid,
        in_specs=[natural, pl.BlockSpec((1, T * LANES, S5_NSTATE), per_sg)],
        out_specs=pl.BlockSpec((1, tc, S5_NSTATE), rows),
        out_shape=jax.ShapeDtypeStruct((S5_SG, S5_CH, S5_NSTATE), F32),
        compiler_params=_cparams("arbitrary", "arbitrary"), name="s5_state",
    )(xn, wb)
    sg1 = lambda s: (s, 0, 0)
    xstart = pl.pallas_call(
        _s5_scan_kernel, grid=(S5_SG,),
        in_specs=[pl.BlockSpec((1, S5_CH, S5_NSTATE), sg1),
                  pl.BlockSpec((1, 1, S5_NSTATE // 2), sg1), pl.BlockSpec((1, 1, S5_NSTATE // 2), sg1)],
        out_specs=pl.BlockSpec((1, S5_CH, S5_NSTATE), sg1),
        out_shape=jax.ShapeDtypeStruct((S5_SG, S5_CH, S5_NSTATE), F32),
        compiler_params=_cparams("arbitrary"), name="s5_scan",
    )(e, ltre, ltim)
    y = pl.pallas_call(
        _s5_out_kernel, grid=grid,
        in_specs=[natural, pl.BlockSpec((1, tc, S5_NSTATE), rows),
                  pl.BlockSpec((1, (T + 1) * LANES, LANES), per_sg), pl.BlockSpec((1, S5_NSTATE, T * LANES), per_sg)],
        out_specs=natural,
        out_shape=jax.ShapeDtypeStruct((S5_CH, T, SSM_WIDTH), F32),
        compiler_params=_cparams("arbitrary", "arbitrary", vmem=56 * 1024 * 1024), name="s5_out",
    )(xn, xstart, krev, wc)
    return y.reshape(N_TOK, SSM_WIDTH)


MERGE_TM = 512


def _merge_kernel(ocmp_ref, osel_ref, owin_ref, gn_ref, yssm_ref, u_ref, ga_ref, gs_ref, x_ref, mod_ref,
                  eg_ref, dskip_ref, wglu_ref, bglu_ref, wua_ref, wus_ref, wout_ref, g2_ref,
                  wrhi_ref, wrlo_ref, wsgu_ref, wsd_ref,
                  xpart_ref, h2_ref, logit_ref):
    mod = mod_ref[0]
    gnb = gn_ref[...].astype(BF16)
    o_nsa = (_dot(gnb, eg_ref[0]) * ocmp_ref[...].astype(F32)
             + _dot(gnb, eg_ref[1]) * osel_ref[...].astype(F32)
             + _dot(gnb, eg_ref[2]) * owin_ref[...].astype(F32))
    attn = _dot(o_nsa.astype(BF16), wua_ref[...])
    z = _gelu(yssm_ref[...] + dskip_ref[...] * u_ref[...])
    y_ssm = z * jax.nn.sigmoid(_dot(z.astype(BF16), wglu_ref[...]) + bglu_ref[...])
    ssm = _dot(y_ssm.astype(BF16), wus_ref[...])
    merged = ga_ref[...].astype(F32) * attn + gs_ref[...].astype(F32) * ssm
    x1 = x_ref[...] + mod[2:3] * _dot(merged.astype(BF16), wout_ref[...])

    ms = jnp.mean(x1 * x1, axis=-1, keepdims=True)
    h2 = (x1 * lax.rsqrt(ms + EPS) * g2_ref[...]) * (1.0 + mod[4:5]) + mod[3:4]
    hi = h2.astype(BF16)
    lo = (h2 - hi.astype(F32)).astype(BF16)
    h2_ref[...] = _pack_bf16_pairs(h2)
    logit_ref[...] = _dot_nt(wrhi_ref[...], hi) + _dot_nt(wrhi_ref[...], lo) + _dot_nt(wrlo_ref[...], hi)
    gu = _dot(hi, wsgu_ref[...])
    shared = _dot((_silu(gu[:, :D_EXPERT]) * gu[:, D_EXPERT:]).astype(BF16), wsd_ref[...])
    xpart_ref[...] = x1 + mod[5:6] * shared


def _merge(ocmp, osel, owin, gn, yssm, u, ga, gs, x2, mod, d_skip, w_glu, b_glu, w_up_attn, w_up_ssm, w_out,
           g_norm2, w_router, ws_gate, ws_up, ws_down):
    tm = MERGE_TM
    eg = np.zeros((3, LANES, NSA_WIDTH), np.float32)
    for j in range(3):
        for h in range(N_HEADS):
            eg[j, 3 * h + j, h * HEAD_DIM:(h + 1) * HEAD_DIM] = 1.0
    wr_t = w_router.T
    wr_hi = wr_t.astype(BF16)
    wr_lo = (wr_t - wr_hi.astype(F32)).astype(BF16)
    row = lambda i: (i, 0)
    fix2 = lambda i: (0, 0)
    wspec = lambda a: pl.BlockSpec(a.shape, (lambda i: (0,) * a.ndim))
    weights = [jnp.asarray(eg, BF16), d_skip.reshape(1, -1), w_glu.astype(BF16), b_glu.reshape(1, -1),
               w_up_attn.astype(BF16), w_up_ssm.astype(BF16), w_out.astype(BF16), g_norm2.reshape(1, -1),
               wr_hi, wr_lo, jnp.concatenate([ws_gate, ws_up], axis=1).astype(BF16), ws_down.astype(BF16)]
    acts = [(ocmp, 512), (osel, 512), (owin, 512), (gn, 128), (yssm, 512), (u, 512), (ga, 1024), (gs, 1024),
            (x2, 1024)]
    return pl.pallas_call(
        _merge_kernel,
        grid=(N_TOK // tm,),
        in_specs=[pl.BlockSpec((tm, wd), row) for _, wd in acts]
                 + [pl.BlockSpec((1, 6, D_MODEL), lambda i: (i // (SEQ // tm), 0, 0))]
                 + [wspec(w) for w in weights],
        out_specs=[pl.BlockSpec((tm, D_MODEL), row), pl.BlockSpec((tm, HALF), row),
                   pl.BlockSpec((N_EXPERTS, tm), lambda i: (0, i))],
        out_shape=[jax.ShapeDtypeStruct((N_TOK, D_MODEL), F32), jax.ShapeDtypeStruct((N_TOK, HALF), I32),
                   jax.ShapeDtypeStruct((N_EXPERTS, N_TOK), F32)],
        compiler_params=_cparams("arbitrary", vmem=56 * 1024 * 1024),
        name="merge",
    )(*[a for a, _ in acts], mod, *weights)


ROUTE_TN = 512


def _route_kernel(logit_ref, bias_ref, eidx_ref, w_ref, count_ref, gscore_ref, masked_ref):
    tn = ROUTE_TN

    @pl.when(pl.program_id(0) == 0)
    def _():
        count_ref[...] = jnp.zeros(count_ref.shape, F32)

    sc = jax.nn.sigmoid(logit_ref[...])
    biased = sc + bias_ref[...]
    gi = lax.broadcasted_iota(I32, (EXPERTS_PER_GROUP, tn), 0).astype(F32)
    for g in range(N_EXPERT_GROUPS):
        blk = biased[g * EXPERTS_PER_GROUP:(g + 1) * EXPERTS_PER_GROUP]
        m1 = jnp.max(blk, axis=0, keepdims=True)
        i1 = jnp.min(jnp.where(blk == m1, gi, float(EXPERTS_PER_GROUP)), axis=0, keepdims=True)
        m2 = jnp.max(jnp.where(gi == i1, -jnp.inf, blk), axis=0, keepdims=True)
        gscore_ref[g:g + 1, :] = m1 + m2
    gs = gscore_ref[...]
    gidx = lax.broadcasted_iota(I32, (N_EXPERT_GROUPS, tn), 0)
    grank = jnp.zeros((N_EXPERT_GROUPS, tn), F32)
    for gp in range(N_EXPERT_GROUPS):
        row = gs[gp:gp + 1, :]
        tie = jnp.where(gidx > gp, 1.0, 0.0)
        grank = grank + jnp.where(row > gs, 1.0, jnp.where(row == gs, tie, 0.0))
    for g in range(N_EXPERT_GROUPS):
        keep = grank[g:g + 1, :] < float(TOPK_GROUPS)
        sl = slice(g * EXPERTS_PER_GROUP, (g + 1) * EXPERTS_PER_GROUP)
        masked_ref[sl, :] = jnp.where(keep, biased[sl], -jnp.inf)
    cur = masked_ref[...]
    eidx = lax.broadcasted_iota(I32, (N_EXPERTS, tn), 0).astype(F32)
    wsum = jnp.zeros((1, tn), F32)
    hits = jnp.zeros((N_EXPERTS, tn), F32)
    for k in range(TOP_K):
        m = jnp.max(cur, axis=0, keepdims=True)
        idx = jnp.min(jnp.where(cur == m, eidx, float(N_EXPERTS)), axis=0, keepdims=True)
        hit = eidx == idx
        wk = jnp.sum(jnp.where(hit, sc, 0.0), axis=0, keepdims=True)
        cur = jnp.where(hit, -jnp.inf, cur)
        hits = hits + jnp.where(hit, 1.0, 0.0)
        eidx_ref[k:k + 1, :] = idx.astype(I32)
        w_ref[k:k + 1, :] = wk
        wsum = wsum + wk
    w_ref[...] = w_ref[...] / wsum * ROUTE_SCALE
    count_ref[...] = count_ref[...] + jnp.sum(hits, axis=1, keepdims=True)


def _route(logits_t, router_bias):
    tn = ROUTE_TN
    return pl.pallas_call(
        _route_kernel,
        grid=(N_TOK // tn,),
        in_specs=[pl.BlockSpec((N_EXPERTS, tn), lambda i: (0, i)), pl.BlockSpec((N_EXPERTS, 1), lambda i: (0, 0))],
        out_specs=[pl.BlockSpec((TOP_K, tn), lambda i: (0, i))] * 2 + [pl.BlockSpec((N_EXPERTS, 1), lambda i: (0, 0))],
        out_shape=[jax.ShapeDtypeStruct((TOP_K, N_TOK), I32), jax.ShapeDtypeStruct((TOP_K, N_TOK), F32),
                   jax.ShapeDtypeStruct((N_EXPERTS, 1), F32)],
        scratch_shapes=[pltpu.VMEM((N_EXPERT_GROUPS, tn), F32), pltpu.VMEM((N_EXPERTS, tn), F32)],
        compiler_params=_cparams("arbitrary"),
        name="route",
    )(logits_t, router_bias.reshape(-1, 1))


N_MOE_BLK = NK // DISPATCH_BLOCK
N_ITEMS = N_MOE_BLK + N_EXPERTS
ASSIGN_BITS = 17


def _dispatch_plan(eidx, counts):
    e_flat = eidx.reshape(-1)
    key = jnp.sort(e_flat * NK + jnp.arange(NK, dtype=I32))
    order = key & (NK - 1)
    counts = counts.reshape(-1).astype(I32)
    start = jnp.cumsum(counts) - counts
    cuts = jnp.sort(jnp.concatenate([jnp.arange(N_MOE_BLK, dtype=I32) * DISPATCH_BLOCK, start]))
    lo = cuts
    hi = jnp.concatenate([cuts[1:], jnp.full((1,), NK, I32)])
    blk = jnp.minimum(lo // DISPATCH_BLOCK, N_MOE_BLK - 1)
    expert = jnp.clip(jnp.sum((start[None, :] <= lo[:, None]).astype(I32), axis=1) - 1, 0, N_EXPERTS - 1)
    one = jnp.ones((1,), I32)
    first = jnp.concatenate([one, (blk[1:] != blk[:-1]).astype(I32)])
    last = jnp.concatenate([(blk[1:] != blk[:-1]).astype(I32), one])
    new_expert = jnp.concatenate([one, (expert[1:] != expert[:-1]).astype(I32)])
    run_id = jnp.cumsum(new_expert) - 1
    n_runs = run_id[-1] + 1
    item = jnp.arange(N_ITEMS, dtype=I32)
    run_first_item = jnp.sort(jnp.where(new_expert == 1, item, N_ITEMS))
    run_expert = expert[jnp.minimum(run_first_item, N_ITEMS - 1)]
    ahead = run_id + (WEIGHT_RING - 1)
    ahead_expert = run_expert[jnp.minimum(ahead, N_ITEMS - 1)]
    ahead_valid = (ahead < n_runs).astype(I32)
    second_expert = run_expert[1:2]
    prologue = jnp.concatenate([second_expert, (n_runs > 1).astype(I32).reshape(1)])
    tok = jnp.right_shift(order, 3)
    home = (order & (TOP_K - 1)) * N_TOK + tok
    return tok, home, (blk, expert, lo - blk * DISPATCH_BLOCK, hi - blk * DISPATCH_BLOCK, first, last, new_expert,
                      run_id % WEIGHT_RING, ahead_expert, ahead_valid, prologue)


SC_CORES = 2
SC_SUBCORES = 16
SC_CHUNK = 128


def _sc_move_rows(table, idx, scatter):
    n = idx.shape[0]
    workers = SC_CORES * SC_SUBCORES
    per_worker = n // workers
    n_chunks = per_worker // SC_CHUNK
    assert per_worker * workers == n and n_chunks * SC_CHUNK == per_worker
    mesh = plsc.VectorSubcoreMesh(core_axis_name="c", subcore_axis_name="s",
                                  num_cores=SC_CORES, num_subcores=SC_SUBCORES)

    def body(table_hbm, idx_hbm, out_hbm, idx_v, rows_v, sem):
        wid = lax.axis_index("s") * SC_CORES + lax.axis_index("c")
        base = wid * per_worker

        @pl.loop(0, n_chunks)
        def _(j):
            off = base + j * SC_CHUNK
            pltpu.sync_copy(idx_hbm.at[pl.ds(off, SC_CHUNK)], idx_v)
            if scatter:
                pltpu.sync_copy(table_hbm.at[pl.ds(off, SC_CHUNK)], rows_v)
                pltpu.async_copy(rows_v, out_hbm.at[idx_v], sem).wait()
            else:
                pltpu.async_copy(table_hbm.at[idx_v], rows_v, sem).wait()
                pltpu.sync_copy(rows_v, out_hbm.at[pl.ds(off, SC_CHUNK)])

    return pl.kernel(
        body,
        out_type=jax.ShapeDtypeStruct((n, table.shape[1]), table.dtype),
        mesh=mesh,
        scratch_types=[pltpu.VMEM((SC_CHUNK,), I32), pltpu.VMEM((SC_CHUNK, table.shape[1]), table.dtype),
                       pltpu.SemaphoreType.DMA],
        name="sc_scatter_rows" if scatter else "sc_gather_rows",
    )(table, idx)


WEIGHT_RING = 3
WEIGHT_CHUNKS = 4


def _expert_weight_copies(w_hbm, wbuf, sem, expert, slot):
    rows = w_hbm.shape[1] // WEIGHT_CHUNKS
    return [pltpu.make_async_copy(w_hbm.at[expert, pl.ds(c * rows, rows)],
                                  wbuf.at[slot, pl.ds(c * rows, rows)], sem.at[slot])
            for c in range(WEIGHT_CHUNKS)]


def _moe_kernel(blk_ref, exp_ref, lo_ref, hi_ref, first_ref, last_ref, newexp_ref,
                slot_ref, ahead_exp_ref, ahead_ok_ref, prologue_ref,
                x_ref, wg_hbm, wu_hbm, wd_hbm, y_ref,
                acc_ref, wgf_ref, wuf_ref, wdf_ref, wgb_ref, wub_ref, wdb_ref, wsem):
    it = pl.program_id(0)
    lo, hi = lo_ref[it], hi_ref[it]
    streams = ((wg_hbm, wgf_ref), (wu_hbm, wuf_ref), (wd_hbm, wdf_ref))

    def request(expert, slot):
        for w_hbm, wbuf in streams:
            for cp in _expert_weight_copies(w_hbm, wbuf, wsem, expert, slot):
                cp.start()

    @pl.when(it == 0)
    def _():
        request(exp_ref[0], 0)

        @pl.when(prologue_ref[1] == 1)
        def _():
            request(prologue_ref[0], 1)

    @pl.when(newexp_ref[it] == 1)
    def _():
        slot = slot_ref[it]
        for w_hbm, wbuf in streams:
            for cp in _expert_weight_copies(w_hbm, wbuf, wsem, 0, slot):
                cp.wait()
        wgb_ref[...] = wgf_ref[slot].astype(BF16)
        wub_ref[...] = wuf_ref[slot].astype(BF16)
        wdb_ref[...] = wdf_ref[slot].astype(BF16)

        @pl.when(ahead_ok_ref[it] == 1)
        def _():
            ahead_slot = slot + (WEIGHT_RING - 1)
            request(ahead_exp_ref[it], jnp.where(ahead_slot >= WEIGHT_RING, ahead_slot - WEIGHT_RING, ahead_slot))

    @pl.when(first_ref[it] == 1)
    def _():
        acc_ref[...] = jnp.zeros(acc_ref.shape, F32)

    def expert_pass(r0, nrows):
        rows = slice(r0, r0 + nrows)
        ridx = r0 + lax.broadcasted_iota(I32, (nrows, HALF), 0)
        mine = (ridx >= lo) & (ridx < hi)
        xlo, xhi = _unpack_bf16_pairs(jnp.where(mine, x_ref[rows, :], 0))
        xlo, xhi = xlo.astype(BF16), xhi.astype(BF16)
        gate = _dot(xlo, wgb_ref[:HALF]) + _dot(xhi, wgb_ref[HALF:])
        up = _dot(xlo, wub_ref[:HALF]) + _dot(xhi, wub_ref[HALF:])
        acc_ref[rows, :] = acc_ref[rows, :] + _dot((_silu(gate) * up).astype(BF16), wdb_ref[...])

    mid = DISPATCH_BLOCK // 2
    pl.when((lo < mid) & (hi > mid))(lambda: expert_pass(0, DISPATCH_BLOCK))
    pl.when((hi > lo) & (hi <= mid))(lambda: expert_pass(0, mid))
    pl.when((hi > lo) & (lo >= mid))(lambda: expert_pass(mid, mid))

    @pl.when(last_ref[it] == 1)
    def _():
        y_ref[...] = _pack_bf16_pairs(acc_ref[...])


def _moe(xs, items, w_gate, w_up, w_down):
    by_blk = lambda it, blk, *_: (blk[it], 0)
    any_space = pl.BlockSpec(memory_space=pl.ANY)
    grid_spec = pltpu.PrefetchScalarGridSpec(
        num_scalar_prefetch=len(items),
        grid=(N_ITEMS,),
        in_specs=[pl.BlockSpec((DISPATCH_BLOCK, HALF), by_blk), any_space, any_space, any_space],
        out_specs=pl.BlockSpec((DISPATCH_BLOCK, HALF), by_blk),
        scratch_shapes=[pltpu.VMEM((DISPATCH_BLOCK, D_MODEL), F32),
                        pltpu.VMEM((WEIGHT_RING, D_MODEL, D_EXPERT), F32),
                        pltpu.VMEM((WEIGHT_RING, D_MODEL, D_EXPERT), F32),
                        pltpu.VMEM((WEIGHT_RING, D_EXPERT, D_MODEL), F32),
                        pltpu.VMEM((D_MODEL, D_EXPERT), BF16), pltpu.VMEM((D_MODEL, D_EXPERT), BF16),
                        pltpu.VMEM((D_EXPERT, D_MODEL), BF16),
                        pltpu.SemaphoreType.DMA((WEIGHT_RING,))],
    )
    return pl.pallas_call(
        _moe_kernel,
        grid_spec=grid_spec,
        out_shape=jax.ShapeDtypeStruct((NK, HALF), I32),
        compiler_params=_cparams("arbitrary"),
        name="moe",
    )(*items, xs, w_gate, w_up, w_down)


COMB_TC = 512


def _combine_kernel(slots_ref, w_ref, xpart_ref, mod_ref, out_ref):
    w = w_ref[...]
    lo = jnp.zeros((w.shape[0], HALF), F32)
    hi = jnp.zeros((w.shape[0], HALF), F32)
    for k in range(TOP_K):
        klo, khi = _unpack_bf16_pairs(slots_ref[k])
        lo = lo + w[:, k:k + 1] * klo
        hi = hi + w[:, k:k + 1] * khi
    gate2 = mod_ref[0][5:6]
    out_ref[:, :HALF] = xpart_ref[:, :HALF] + gate2[:, :HALF] * lo
    out_ref[:, HALF:] = xpart_ref[:, HALF:] + gate2[:, HALF:] * hi


def _combine(xpart, mod, slots, w):
    tc = COMB_TC
    row = lambda i: (i, 0)
    return pl.pallas_call(
        _combine_kernel,
        grid=(N_TOK // tc,),
        in_specs=[pl.BlockSpec((TOP_K, tc, HALF), lambda i: (0, i, 0)),
                  pl.BlockSpec((tc, TOP_K), row),
                  pl.BlockSpec((tc, D_MODEL), row),
                  pl.BlockSpec((1, 6, D_MODEL), lambda i: (i // (SEQ // tc), 0, 0))],
        out_specs=pl.BlockSpec((tc, D_MODEL), row),
        out_shape=jax.ShapeDtypeStruct((N_TOK, D_MODEL), F32),
        compiler_params=_cparams("arbitrary"),
        name="combine",
    )(slots.reshape(TOP_K, N_TOK, HALF), w, xpart, mod)


def _layer(x, c, w_ada, b_ada, g_norm1, g_norm2, w_in, q_gain, kc_gain, ks_gain, kw_gain,
           pe_k, pe_v, w_cmp_k1, w_cmp_k2, w_cmp_v1, w_cmp_v2,
           a_re, a_im, log_dt, b_re, b_im, c_re, c_im, d_skip, w_glu, b_glu,
           w_up_attn, w_up_ssm, w_out, w_router, router_bias,
           w_gate, w_up, w_down, ws_gate, ws_up, ws_down):
    x2 = x.reshape(N_TOK, D_MODEL)
    mod = _ada(c, w_ada, b_ada)
    q, kc_raw, vc_raw, ks, kw, vst, vwt, gn, u, ga, gs = _proj(x2, mod, g_norm1, w_in, q_gain, ks_gain, kw_gain)
    kcn = _compress(kc_raw, pe_k, w_cmp_k1, w_cmp_k2, kc_gain, True)
    vcn = _compress(vc_raw, pe_v, w_cmp_v1, w_cmp_v2, kc_gain, False)
    ocmp, selb = _cmp_attn(q, kcn, vcn)
    osel, owin = _selwin(q, ks, kw, vst, vwt, selb)
    yssm = _s5(u, *_s5_params(a_re, a_im, log_dt, b_re, b_im, c_re, c_im))
    xpart, h2, logits_t = _merge(ocmp, osel, owin, gn, yssm, u, ga, gs, x2, mod, d_skip, w_glu, b_glu,
                                  w_up_attn, w_up_ssm, w_out, g_norm2, w_router, ws_gate, ws_up, ws_down)
    eidx_t, w_t, counts = _route(logits_t, router_bias)
    tok, home, items = _dispatch_plan(eidx_t.T, counts)
    y = _moe(_sc_move_rows(h2, tok, scatter=False), items, w_gate, w_up, w_down)
    slots = _sc_move_rows(y, home, scatter=True)
    return _combine(xpart, mod, slots, w_t.T).reshape(BATCH, SEQ, D_MODEL)


def kernel(x, c, w_ada, b_ada, g_norm1, g_norm2, w_in, q_gain, kc_gain, ks_gain, kw_gain, pe_k, pe_v, w_cmp_k1,
           w_cmp_k2, w_cmp_v1, w_cmp_v2, a_re, a_im, log_dt, b_re, b_im, c_re, c_im, d_skip, w_glu, b_glu,
           w_up_attn, w_up_ssm, w_out, w_router, router_bias, w_gate, w_up, w_down, ws_gate, ws_up, ws_down):
    params = (w_ada, b_ada, g_norm1, g_norm2, w_in, q_gain, kc_gain, ks_gain, kw_gain, pe_k, pe_v, w_cmp_k1,
              w_cmp_k2, w_cmp_v1, w_cmp_v2, a_re, a_im, log_dt, b_re, b_im, c_re, c_im, d_skip, w_glu, b_glu,
              w_up_attn, w_up_ssm, w_out, w_router, router_bias, w_gate, w_up, w_down, ws_gate, ws_up, ws_down)
    depth = w_ada.shape[0]
    for layer in range(depth):
        x = _layer(x, c, *[p[layer] for p in params])
    return x
```

```python
import functools
import math

import jax
import jax.numpy as jnp
import numpy as np
from jax import lax
from jax.experimental import pallas as pl
from jax.experimental.pallas import tpu as pltpu
from jax.experimental.pallas import tpu_sc as plsc

F32 = jnp.float32
BF16 = jnp.bfloat16
I32 = jnp.int32
HIGHEST = lax.Precision.HIGHEST

D_MODEL = 1024
BATCH = 4
SEQ = 4096
N_TOK = BATCH * SEQ
N_HEADS = 8
HEAD_DIM = 64
N_KV = 2
CMP_BLOCK = 32
CMP_STRIDE = 16
CMP_HIDDEN = 256
N_CMP = 256
SEL_BLOCK = 64
N_SEL_BLOCKS = SEQ // SEL_BLOCK
N_SELECT = 16
WINDOW = 512
ATTN_SCALE = HEAD_DIM ** -0.5
LOG2E = 1.4426950408889634
NSA_WIDTH = N_HEADS * HEAD_DIM
SSM_WIDTH = 512
GROUP = 16
N_GROUPS = SSM_WIDTH // GROUP
STATE = 64
N_EXPERTS = 256
TOP_K = 8
D_EXPERT = 256
N_EXPERT_GROUPS = 8
EXPERTS_PER_GROUP = N_EXPERTS // N_EXPERT_GROUPS
TOPK_GROUPS = 4
ROUTE_SCALE = 2.5
DISPATCH_BLOCK = 512
EPS = 1e-6
NEG = -1e30

LANES = 128
S5_T = 16
S5_SG = 4
S5_GL = N_GROUPS // S5_SG
S5_CH = N_TOK // S5_T
S5_CH_PER_BATCH = SEQ // S5_T
S5_NSTATE = S5_GL * STATE * 2

NK = N_TOK * TOP_K
HALF = D_MODEL // 2

VMEM_LIMIT = 48 * 1024 * 1024


def _cparams(*sem, vmem=VMEM_LIMIT):
    return pltpu.CompilerParams(dimension_semantics=tuple(sem), vmem_limit_bytes=vmem)


def _dot(a, b):
    return jnp.dot(a, b, preferred_element_type=F32)


def _dot_nt(a, b):
    return lax.dot_general(a, b, (((1,), (1,)), ((), ())), preferred_element_type=F32)


def _split_dot(v, w):
    hi = v.astype(BF16)
    lo = (v - hi.astype(F32)).astype(BF16)
    return _dot(hi, w) + _dot(lo, w)


def _seg_rms(v, bd, gain):
    ss = _split_dot(v * v, bd)
    return v * lax.rsqrt(ss * (1.0 / HEAD_DIM) + EPS) * gain


def _gelu(x):
    return 0.5 * x * (1.0 + jnp.tanh(0.7978845608028654 * (x + 0.044715 * (x * x * x))))


def _silu(x):
    return x * jax.nn.sigmoid(x)


def _pack_bf16_pairs(v):
    h = v.shape[1] // 2
    return pltpu.pack_elementwise([v[:, :h], v[:, h:]], packed_dtype=BF16)


def _unpack_bf16_pairs(word):
    return (pltpu.unpack_elementwise(word, index=0, packed_dtype=BF16, unpacked_dtype=F32),
            pltpu.unpack_elementwise(word, index=1, packed_dtype=BF16, unpacked_dtype=F32))


def _ada_kernel(c_ref, w_ref, b_ref, o_ref):
    c = c_ref[...]
    o_ref[...] = jnp.dot(_silu(c), w_ref[...], preferred_element_type=F32, precision=HIGHEST) + b_ref[...]


def _ada(c, w_ada, b_ada):
    cp = jnp.pad(c, ((0, 8 - BATCH), (0, 0)))
    tn = 1536
    out = pl.pallas_call(
        _ada_kernel,
        grid=(6 * D_MODEL // tn,),
        in_specs=[pl.BlockSpec((8, D_MODEL), lambda j: (0, 0)),
                  pl.BlockSpec((D_MODEL, tn), lambda j: (0, j)),
                  pl.BlockSpec((1, tn), lambda j: (0, j))],
        out_specs=pl.BlockSpec((8, tn), lambda j: (0, j)),
        out_shape=jax.ShapeDtypeStruct((8, 6 * D_MODEL), F32),
        compiler_params=_cparams("arbitrary"),
        name="ada",
    )(cp, w_ada, b_ada.reshape(1, -1))
    return out.reshape(8, 6, D_MODEL)


_C_Q = 0
_C_KC = 512
_C_VC = 640
_C_KS = 768
_C_KW = 1024
_C_GN = 1280
_C_U = 1408
_C_GA = 1920
_C_GS = 2944
_C_END = 3968
PROJ_TM = 512


def _proj_kernel(x_ref, mod_ref, g1_ref, w_ref, wvt_ref, qg_ref, ksg_ref, kwg_ref, bd512_ref, bd256_ref,
                 q_ref, kc_ref, vc_ref, ks_ref, kw_ref, vst_ref, vwt_ref, gn_ref, u_ref, ga_ref, gs_ref):
    x = x_ref[...]
    ms = jnp.mean(x * x, axis=-1, keepdims=True)
    mod = mod_ref[0]
    h = (x * lax.rsqrt(ms + EPS) * g1_ref[...]) * (1.0 + mod[1:2]) + mod[0:1]
    hb = h.astype(BF16)

    def p(lo, hi):
        return _dot(hb, w_ref[:, lo:hi])

    q_ref[...] = _seg_rms(p(_C_Q, _C_KC), bd512_ref[...], qg_ref[...] * (ATTN_SCALE * LOG2E)).astype(BF16)
    kc_ref[...] = p(_C_KC, _C_VC)
    vc_ref[...] = p(_C_VC, _C_KS)
    ks_ref[...] = _seg_rms(p(_C_KS, _C_KW), bd256_ref[...], ksg_ref[...]).astype(BF16)
    kw_ref[...] = _seg_rms(p(_C_KW, _C_GN), bd256_ref[...], kwg_ref[...]).astype(BF16)
    vt = _dot_nt(wvt_ref[...], hb)
    vst_ref[...] = vt[:LANES].astype(BF16)
    vwt_ref[...] = vt[LANES:].astype(BF16)
    gn_ref[...] = jax.nn.sigmoid(p(_C_GN, _C_U))
    u_ref[...] = p(_C_U, _C_GA)
    ga_ref[...] = jax.nn.sigmoid(p(_C_GA, _C_GS)).astype(BF16)
    gs_ref[...] = jax.nn.sigmoid(p(_C_GS, _C_END)).astype(BF16)


def _dup_cols(w):
    return jnp.concatenate([w[:, :64], w[:, :64], w[:, 64:], w[:, 64:]], axis=1)


def _block_ones(n):
    return jnp.kron(jnp.eye(n // HEAD_DIM, dtype=F32), jnp.ones((HEAD_DIM, HEAD_DIM), F32)).astype(BF16)


def _proj(x2, mod, g_norm1, w_in, q_gain, ks_gain, kw_gain):
    o = np.cumsum((0, 512, 128, 128, 128, 128, 128, 128, 24, 512, 1024, 1024))
    parts = [w_in[:, o[i]:o[i + 1]] for i in range(11)]
    wq, wkc, wvc, wks, wvs, wkw, wvw, wgn, wu, wga, wgs = parts
    w = jnp.concatenate([wq, wkc, wvc, _dup_cols(wks), _dup_cols(wkw),
                         jnp.pad(wgn, ((0, 0), (0, LANES - 24))), wu, wga, wgs], axis=1).astype(BF16)
    wvt = jnp.concatenate([wvs, wvw], axis=1).T.astype(BF16)
    tm = PROJ_TM
    row = lambda i: (i, 0)
    col = lambda i: (0, i)
    fix = lambda i: (0, 0)
    outs = [(512, BF16, row), (128, F32, row), (128, F32, row), (256, BF16, row), (256, BF16, row),
            (LANES, BF16, col), (LANES, BF16, col),
            (128, F32, row), (512, F32, row), (1024, BF16, row), (1024, BF16, row)]
    ospec = lambda wd, m: pl.BlockSpec((tm, wd), m) if m is row else pl.BlockSpec((wd, tm), m)
    oshape = lambda wd, dt, m: jax.ShapeDtypeStruct((N_TOK, wd) if m is row else (wd, N_TOK), dt)
    return pl.pallas_call(
        _proj_kernel,
        grid=(N_TOK // tm,),
        in_specs=[pl.BlockSpec((tm, D_MODEL), row),
                  pl.BlockSpec((1, 6, D_MODEL), lambda i: (i // (SEQ // tm), 0, 0)),
                  pl.BlockSpec((1, D_MODEL), fix),
                  pl.BlockSpec((D_MODEL, _C_END), fix),
                  pl.BlockSpec((2 * LANES, D_MODEL), fix),
                  pl.BlockSpec((1, 512), fix), pl.BlockSpec((1, 256), fix), pl.BlockSpec((1, 256), fix),
                  pl.BlockSpec((512, 512), fix), pl.BlockSpec((256, 256), fix)],
        out_specs=[ospec(wd, m) for wd, _, m in outs],
        out_shape=[oshape(wd, dt, m) for wd, dt, m in outs],
        compiler_params=_cparams("arbitrary"),
        name="proj",
    )(x2, mod, g_norm1.reshape(1, -1), w, wvt,
      jnp.tile(q_gain, N_HEADS).reshape(1, -1), jnp.tile(ks_gain, 4).reshape(1, -1),
      jnp.tile(kw_gain, 4).reshape(1, -1), _block_ones(512), _block_ones(256))


def _compress_kernel(r_ref, pe_ref, w1_ref, w2_ref, bd_ref, gain_ref, o_ref, *, do_norm):
    r = jnp.concatenate([r_ref[0, :, l, :] for l in range(CMP_STRIDE)], axis=1)
    p0 = _dot((r + pe_ref[0]).astype(BF16), w1_ref[0])
    p1 = _dot((r + pe_ref[1]).astype(BF16), w1_ref[1])
    hid = p0 + pltpu.roll(p1, N_CMP - 1, 0)
    c = _dot(_gelu(hid).astype(BF16), w2_ref[...])
    if do_norm:
        c = _seg_rms(c, bd_ref[...], gain_ref[...])
    o_ref[0] = c.astype(BF16)


def _compress(raw, pe, w1, w2, gain, do_norm):
    r = raw.reshape(BATCH, SEQ // CMP_STRIDE, CMP_STRIDE, LANES)
    eye = jnp.eye(N_KV, dtype=F32)
    w1r = w1.reshape(2, CMP_STRIDE, HEAD_DIM, CMP_HIDDEN)
    w1big = jnp.einsum('hldc,gk->hlgdkc', w1r, eye).reshape(2, CMP_STRIDE * LANES, N_KV * CMP_HIDDEN).astype(BF16)
    w2big = jnp.einsum('cd,gk->gckd', w2, eye)
    w2big = jnp.concatenate([w2big, w2big], axis=-1).reshape(N_KV * CMP_HIDDEN, 4 * HEAD_DIM).astype(BF16)
    pe_big = jnp.broadcast_to(pe.reshape(2, CMP_STRIDE, 1, HEAD_DIM), (2, CMP_STRIDE, N_KV, HEAD_DIM))
    pe_big = pe_big.reshape(2, 1, CMP_STRIDE * LANES)
    fix2 = lambda b: (0, 0)
    fix3 = lambda b: (0, 0, 0)
    return pl.pallas_call(
        functools.partial(_compress_kernel, do_norm=do_norm),
        grid=(BATCH,),
        in_specs=[pl.BlockSpec((1, N_CMP, CMP_STRIDE, LANES), lambda b: (b, 0, 0, 0)),
                  pl.BlockSpec((2, 1, CMP_STRIDE * LANES), fix3),
                  pl.BlockSpec((2, CMP_STRIDE * LANES, N_KV * CMP_HIDDEN), fix3),
                  pl.BlockSpec((N_KV * CMP_HIDDEN, 256), fix2),
                  pl.BlockSpec((256, 256), fix2), pl.BlockSpec((1, 256), fix2)],
        out_specs=pl.BlockSpec((1, N_CMP, 256), lambda b: (b, 0, 0)),
        out_shape=jax.ShapeDtypeStruct((BATCH, N_CMP, 256), BF16),
        compiler_params=_cparams("arbitrary"),
        name="compress_k" if do_norm else "compress_v",
    )(r, pe_big, w1big, w2big, _block_ones(256), jnp.tile(gain, 4).reshape(1, -1))


ATT_TQ = 512
RANK_CHUNK = 16


def _head_variants(qb):
    lane = lax.broadcasted_iota(I32, qb.shape, 1)
    z = jnp.zeros_like(qb)
    return jnp.where(lane < HEAD_DIM, qb, z), jnp.where(lane < HEAD_DIM, z, qb)


def _cmp_kernel(q_ref, kc_ref, vc_ref, ov_ref, o_ref, sel_ref, vrank_ref):
    tq = ATT_TQ
    qi = pl.program_id(1)
    tpos = qi * tq + lax.broadcasted_iota(I32, (tq, N_CMP), 0)
    nidx = lax.broadcasted_iota(I32, (tq, N_CMP), 1)
    mask = (CMP_STRIDE * nidx + (CMP_BLOCK - 1)) <= tpos
    lane_lo = lax.broadcasted_iota(I32, (tq, LANES), 1) < HEAD_DIM
    for g in range(N_KV):
        kd = kc_ref[0, :, g * LANES:(g + 1) * LANES]
        vd = vc_ref[0, :, g * LANES:(g + 1) * LANES]
        psum = jnp.zeros((tq, N_CMP), F32)
        for jb in range(2):
            blk = 2 * g + jb
            pv = []
            for qv in _head_variants(q_ref[:, blk * LANES:(blk + 1) * LANES]):
                s = jnp.where(mask, _dot_nt(qv, kd), NEG)
                m = jnp.max(s, axis=-1, keepdims=True)
                e = jnp.where(mask, jnp.exp2(s - m), 0.0)
                l = jnp.sum(e, axis=-1, keepdims=True)
                p = e / jnp.where(l > 0.0, l, 1.0)
                psum = psum + p
                pv.append(_dot(p.astype(BF16), vd))
            o_ref[:, blk * LANES:(blk + 1) * LANES] = jnp.where(lane_lo, pv[0], pv[1]).astype(BF16)
        imp = _split_dot(psum, ov_ref[...])
        imp_t = imp.T[:N_SEL_BLOCKS]
        j = lax.broadcasted_iota(I32, (N_SEL_BLOCKS, tq), 0)
        cur = jnp.right_shift(qi * tq + lax.broadcasted_iota(I32, (N_SEL_BLOCKS, tq), 1), 6)
        forced = (j == 0) | (j == cur) | (j == cur - 1)
        v = jnp.where(forced, jnp.inf, jnp.where(j <= cur, imp_t, -jnp.inf))
        vrank_ref[...] = jnp.zeros((N_SEL_BLOCKS, tq), F32)
        n_live = (qi + 1) * (tq // SEL_BLOCK)
        for c0 in range(0, N_SEL_BLOCKS, RANK_CHUNK):
            @pl.when(c0 < n_live)
            def _():
                rank = vrank_ref[...]
                for jp in range(c0, c0 + RANK_CHUNK):
                    row = v[jp:jp + 1, :]
                    tie = jnp.where(j > jp, 1.0, 0.0)
                    rank = rank + jnp.where(row > v, 1.0, jnp.where(row == v, tie, 0.0))
                vrank_ref[...] = rank
        rank = vrank_ref[...]
        sel_ref[g * N_SEL_BLOCKS:(g + 1) * N_SEL_BLOCKS, :] = jnp.where(rank < float(N_SELECT), 0.0, NEG)


def _cmp_attn(q, kcn, vcn):
    nc = np.arange(N_CMP)
    sb = np.arange(LANES)
    ov = ((CMP_STRIDE * nc[:, None] < SEL_BLOCK * sb[None, :] + SEL_BLOCK)
          & (CMP_STRIDE * nc[:, None] + CMP_BLOCK > SEL_BLOCK * sb[None, :])
          & (nc[:, None] < N_CMP - 1) & (sb[None, :] < N_SEL_BLOCKS))
    ov = jnp.asarray(ov, BF16)
    tq = ATT_TQ
    nq = SEQ // tq
    row = lambda b, i: (b * nq + i, 0)
    return pl.pallas_call(
        _cmp_kernel,
        grid=(BATCH, nq),
        in_specs=[pl.BlockSpec((tq, NSA_WIDTH), row),
                  pl.BlockSpec((1, N_CMP, 256), lambda b, i: (b, 0, 0)),
                  pl.BlockSpec((1, N_CMP, 256), lambda b, i: (b, 0, 0)),
                  pl.BlockSpec((N_CMP, LANES), lambda b, i: (0, 0))],
        out_specs=[pl.BlockSpec((tq, NSA_WIDTH), row),
                   pl.BlockSpec((N_KV * N_SEL_BLOCKS, tq), lambda b, i: (0, b * nq + i))],
        out_shape=[jax.ShapeDtypeStruct((N_TOK, NSA_WIDTH), BF16),
                   jax.ShapeDtypeStruct((N_KV * N_SEL_BLOCKS, N_TOK), F32)],
        scratch_shapes=[pltpu.VMEM((N_SEL_BLOCKS, tq), F32)],
        compiler_params=_cparams("arbitrary", "arbitrary"),
        name="cmp_attn",
    )(q, kcn, vcn, ov)


ATT_TK = 256


M_INIT = -1e29


SUM_ROWS = 16
SW_TQ = 512
SEL_TK = 512


def _selwin_kernel(q_ref, ks_ref, kw_ref, vst_ref, vwt_ref, selb_ref, osel_ref, owin_ref, m_ref, acc_ref):
    tq = tk = SW_TQ
    qi = pl.program_id(1)
    krow = lax.broadcasted_iota(I32, (tk, tq), 0)
    qcol = lax.broadcasted_iota(I32, (tk, tq), 1)
    causal_bias = jnp.where(krow <= qcol, 0.0, NEG)
    far_bias = jnp.where(qcol < krow, 0.0, NEG)

    def reset():
        m_ref[...] = jnp.full(m_ref.shape, M_INIT, F32)
        acc_ref[...] = jnp.zeros(acc_ref.shape, F32)

    def update(g, k_ref, vt_ref, kt, bias, nk=tk):
        k0 = pl.multiple_of(kt * nk, nk)
        kd = k_ref[0, pl.ds(k0, nk), g * LANES:(g + 1) * LANES]
        vt = vt_ref[g * HEAD_DIM:(g + 1) * HEAD_DIM, pl.ds(k0, nk)]
        s = _dot_nt(kd, qvars[g])
        if bias is not None:
            s = s + jnp.concatenate([bias] * 4, axis=1)
        m_old = m_ref[g]
        m_new = jnp.maximum(m_old, jnp.max(s, axis=0, keepdims=True))
        alpha = jnp.exp2(m_old - m_new)
        p = jnp.exp2(s - m_new)
        m_ref[g] = m_new
        vte = jnp.concatenate([vt, jnp.ones((SUM_ROWS, nk), BF16)], axis=0)
        acc_ref[g] = alpha * acc_ref[g] + _dot(vte, p.astype(BF16))

    def finish(out_ref, g):
        o = acc_ref[g, :HEAD_DIM, :] / acc_ref[g, HEAD_DIM:HEAD_DIM + 1, :]
        for jb in range(2):
            blk = 2 * g + jb
            pair = jnp.concatenate([o[:, 2 * jb * tq:(2 * jb + 1) * tq], o[:, (2 * jb + 1) * tq:(2 * jb + 2) * tq]],
                                   axis=0)
            out_ref[:, blk * LANES:(blk + 1) * LANES] = pair.T.astype(BF16)

    def sel_bias(g, kt):
        rows = [jnp.broadcast_to(selb_ref[pl.ds(g * N_SEL_BLOCKS + kt * (SEL_TK // SEL_BLOCK) + r, 1), :],
                                 (SEL_BLOCK, tq)) for r in range(SEL_TK // SEL_BLOCK)]
        return jnp.concatenate(rows, axis=0)

    qvars = []
    for g in range(N_KV):
        heads = []
        for jb in range(2):
            heads.extend(_head_variants(q_ref[:, (2 * g + jb) * LANES:(2 * g + jb + 1) * LANES]))
        qvars.append(jnp.concatenate(heads, axis=0))
    groups = range(N_KV)

    reset()
    last_tile = qi // (SEL_TK // tq)

    def sel_step(kt, carry):
        for g in groups:
            update(g, ks_ref, vst_ref, kt, sel_bias(g, kt), SEL_TK)
        return carry

    lax.fori_loop(0, last_tile, sel_step, 0)
    q_first = qi * tq - last_tile * SEL_TK
    visible = (lax.broadcasted_iota(I32, (SEL_TK, tq), 0)
               <= lax.broadcasted_iota(I32, (SEL_TK, tq), 1) + q_first)
    diag_bias = jnp.where(visible, 0.0, NEG)
    for g in groups:
        update(g, ks_ref, vst_ref, last_tile, sel_bias(g, last_tile) + diag_bias, SEL_TK)
    for g in groups:
        finish(osel_ref, g)

    reset()
    back = WINDOW // tk

    @pl.when(qi >= back)
    def _():
        for g in groups:
            update(g, kw_ref, vwt_ref, qi - back, far_bias)

    for d in range(back - 1, 0, -1):
        @pl.when(qi >= d)
        def _():
            for g in groups:
                update(g, kw_ref, vwt_ref, qi - d, None)

    for g in groups:
        update(g, kw_ref, vwt_ref, qi, causal_bias)
    for g in groups:
        finish(owin_ref, g)


def _selwin(q, ks, kw, vst, vwt, selb):
    tq = SW_TQ
    nq = SEQ // tq
    assert WINDOW % SW_TQ == 0 and SEL_TK % SW_TQ == 0
    row = lambda b, i: (b * nq + i, 0)
    keys = pl.BlockSpec((1, SEQ, 256), lambda b, i: (b, 0, 0))
    vals = pl.BlockSpec((LANES, SEQ), lambda b, i: (0, b))
    r3 = lambda a: a.reshape(BATCH, SEQ, 256)
    return pl.pallas_call(
        _selwin_kernel,
        grid=(BATCH, nq),
        in_specs=[pl.BlockSpec((tq, NSA_WIDTH), row), keys, keys, vals, vals,
                  pl.BlockSpec((N_KV * N_SEL_BLOCKS, tq), lambda b, i: (0, b * nq + i))],
        out_specs=[pl.BlockSpec((tq, NSA_WIDTH), row)] * 2,
        out_shape=[jax.ShapeDtypeStruct((N_TOK, NSA_WIDTH), BF16)] * 2,
        scratch_shapes=[pltpu.VMEM((N_KV, 1, 4 * tq), F32),
                        pltpu.VMEM((N_KV, HEAD_DIM + SUM_ROWS, 4 * tq), F32)],
        compiler_params=_cparams("arbitrary", "arbitrary"),
        name="selwin",
    )(q, r3(ks), r3(kw), vst, vwt, selb)


def _s5_param_kernel(are_ref, aim_ref, ldt_ref, cre_ref, cim_ref, bre_ref, bim_ref,
                     clre_ref, clim_ref, wbre_ref, wbim_ref, bbre_ref, bbim_ref, ltre_ref, ltim_ref):
    are, aim = are_ref[...], aim_ref[...]
    dt = jnp.exp(ldt_ref[...])
    cre, cim = cre_ref[...], cim_ref[...]

    def lam_pow(tau):
        mag = jnp.exp(are * dt * float(tau))
        ang = aim * dt * float(tau)
        return mag * jnp.cos(ang), mag * jnp.sin(ang)

    lre, lim = lam_pow(1)
    den = are * are + aim * aim
    qre = ((lre - 1.0) * are + lim * aim) / den
    qim = (lim * are - (lre - 1.0) * aim) / den
    bre, bim = bre_ref[...], bim_ref[...]
    bbre = qre * bre - qim * bim
    bbim = qre * bim + qim * bre
    bbre_ref[...] = bbre
    bbim_ref[...] = bbim
    for tau in range(S5_T + 1):
        pr, pi = lam_pow(tau)
        clre_ref[tau] = cre * pr - cim * pi
        clim_ref[tau] = cre * pi + cim * pr
        if tau < S5_T:
            k = S5_T - 1 - tau
            wbre_ref[k] = pr * bbre - pi * bbim
            wbim_ref[k] = pr * bbim + pi * bbre
        else:
            ltre_ref[...] = pr
            ltim_ref[...] = pi


def _s5_kmat_kernel(l_ref, r_ref, o_ref):
    o_ref[0] = jnp.dot(l_ref[0], r_ref[0], preferred_element_type=F32, precision=HIGHEST)


def _s5_params(a_re, a_im, log_dt, b_re, b_im, c_re, c_im):
    T = S5_T
    pn = GROUP * STATE
    tile_p = lambda a: jnp.tile(a, (1, GROUP))
    args = (tile_p(a_re), tile_p(a_im), jnp.broadcast_to(log_dt[:, None], (N_GROUPS, pn)),
            c_re.reshape(N_GROUPS, pn), c_im.reshape(N_GROUPS, pn),
            jnp.swapaxes(b_re, 1, 2).reshape(N_GROUPS, pn), jnp.swapaxes(b_im, 1, 2).reshape(N_GROUPS, pn))
    full2 = pl.BlockSpec((N_GROUPS, pn), lambda: (0, 0))
    clre, clim, wbre, wbim, bbre, bbim, ltre, ltim = pl.pallas_call(
        _s5_param_kernel,
        in_specs=[full2] * 7,
        out_specs=[pl.BlockSpec((T + 1, N_GROUPS, pn), lambda: (0, 0, 0))] * 2
                  + [pl.BlockSpec((T, N_GROUPS, pn), lambda: (0, 0, 0))] * 2 + [full2] * 4,
        out_shape=[jax.ShapeDtypeStruct((T + 1, N_GROUPS, pn), F32)] * 2
                  + [jax.ShapeDtypeStruct((T, N_GROUPS, pn), F32)] * 2
                  + [jax.ShapeDtypeStruct((N_GROUPS, pn), F32)] * 4,
        name="s5_params",
    )(*args)

    r5 = lambda a, t: a[:t].reshape(t, N_GROUPS, GROUP, STATE)
    lhs = jnp.concatenate([r5(clre, T), -r5(clim, T)], axis=-1)
    lhs = jnp.transpose(lhs, (1, 0, 2, 3)).reshape(N_GROUPS, T * GROUP, 2 * STATE)
    bb = lambda a: jnp.swapaxes(a.reshape(N_GROUPS, GROUP, STATE), 1, 2)
    rhs = jnp.concatenate([bb(bbre), bb(bbim)], axis=1)
    kmat = pl.pallas_call(
        _s5_kmat_kernel,
        grid=(N_GROUPS,),
        in_specs=[pl.BlockSpec((1, T * GROUP, 2 * STATE), lambda g: (g, 0, 0)),
                  pl.BlockSpec((1, 2 * STATE, GROUP), lambda g: (g, 0, 0))],
        out_specs=pl.BlockSpec((1, T * GROUP, GROUP), lambda g: (g, 0, 0)),
        out_shape=jax.ShapeDtypeStruct((N_GROUPS, T * GROUP, GROUP), F32),
        compiler_params=_cparams("arbitrary"),
        name="s5_kmat",
    )(lhs, rhs)

    eye = jnp.eye(S5_GL, dtype=F32)
    kt = kmat.reshape(S5_SG, S5_GL, T, GROUP, GROUP)
    kbd = jnp.einsum('sgtpq,gh->stgqhp', kt, eye).reshape(S5_SG, T, LANES, LANES)
    krev = kbd[:, ::-1].reshape(S5_SG, T * LANES, LANES).astype(BF16)
    krev = jnp.pad(krev, ((0, 0), (0, LANES), (0, 0)))
    r6 = lambda a: a.reshape(T, S5_SG, S5_GL, GROUP, STATE)
    wb = jnp.stack([r6(wbre), r6(wbim)], axis=-2)
    wb = jnp.einsum('ksgpin,gh->skgpihn', wb, eye).reshape(S5_SG, T * LANES, S5_NSTATE).astype(BF16)
    wc = jnp.stack([r6(clre[1:]), -r6(clim[1:])], axis=-2)
    wc = jnp.einsum('tsgpin,gh->signthp', wc, eye).reshape(S5_SG, S5_NSTATE, T * LANES).astype(BF16)
    lt = lambda a: a.reshape(N_GROUPS, GROUP, STATE)[:, 0].reshape(S5_SG, 1, S5_GL * STATE)
    return krev, wb, wc, lt(ltre), lt(ltim)


S5_TC = 256


def _s5_lane_block(sg):
    return pl.ds(pl.multiple_of(sg * LANES, LANES), LANES)


def _s5_chunk_inputs(x_ref, sg):
    return jnp.concatenate([x_ref[:, t, _s5_lane_block(sg)] for t in range(S5_T)], axis=1).astype(BF16)


def _s5_state_kernel(x_ref, wb_ref, e_ref):
    e_ref[0] = _dot(_s5_chunk_inputs(x_ref, pl.program_id(1)), wb_ref[0])


def _s5_scan_kernel(e_ref, ltre_ref, ltim_ref, xs_ref):
    lr, li = ltre_ref[0], ltim_ref[0]
    half = S5_NSTATE // 2

    def step(c, carry):
        new = []
        for b in range(BATCH):
            xr, xi = carry[b]
            row = b * S5_CH_PER_BATCH + c
            xs_ref[0, pl.ds(row, 1), :half] = xr
            xs_ref[0, pl.ds(row, 1), half:] = xi
            e = e_ref[0, pl.ds(row, 1), :]
            new.append((lr * xr - li * xi + e[:, :half], lr * xi + li * xr + e[:, half:]))
        return tuple(new)

    zero = jnp.zeros((1, half), F32)
    lax.fori_loop(0, S5_CH_PER_BATCH, step, tuple((zero, zero) for _ in range(BATCH)))


def _s5_out_kernel(x_ref, xs_ref, krev_ref, wc_ref, y_ref):
    sg = pl.program_id(1)
    x = _s5_chunk_inputs(x_ref, sg)
    xsb = xs_ref[0].astype(BF16)
    for t in range(0, S5_T, 2):
        n_in = (t + 2) * LANES
        first = (S5_T - 1 - t) * LANES
        taps = jnp.concatenate([krev_ref[0, first:first + n_in, :], krev_ref[0, first - LANES:first - LANES + n_in, :]],
                               axis=1)
        pair = _dot(x[:, :n_in], taps) + _dot(xsb, wc_ref[0, :, t * LANES:(t + 2) * LANES])
        y_ref[:, t, _s5_lane_block(sg)] = pair[:, :LANES]
        y_ref[:, t + 1, _s5_lane_block(sg)] = pair[:, LANES:]


def _s5(u, krev, wb, wc, ltre, ltim):
    T, tc = S5_T, S5_TC
    xn = u.reshape(S5_CH, T, SSM_WIDTH)
    grid = (S5_CH // tc, S5_SG)
    natural = pl.BlockSpec((tc, T, SSM_WIDTH), lambda i, s: (i, 0, 0))
    rows = lambda i, s: (s, i, 0)
    per_sg = lambda i, s: (s, 0, 0)
    e = pl.pallas_call(
        _s5_state_kernel, grid=grid,
        in_specs=[natural, pl.BlockSpec((1, T * LANES, S5_NSTATE), per_sg)],
        out_specs=pl.BlockSpec((1, tc, S5_NSTATE), rows),
        out_shape=jax.ShapeDtypeStruct((S5_SG, S5_CH, S5_NSTATE), F32),
        compiler_params=_cparams("arbitrary", "arbitrary"), name="s5_state",
    )(xn, wb)
    sg1 = lambda s: (s, 0, 0)
    xstart = pl.pallas_call(
        _s5_scan_kernel, grid=(S5_SG,),
        in_specs=[pl.BlockSpec((1, S5_CH, S5_NSTATE), sg1),
                  pl.BlockSpec((1, 1, S5_NSTATE // 2), sg1), pl.BlockSpec((1, 1, S5_NSTATE // 2), sg1)],
        out_specs=pl.BlockSpec((1, S5_CH, S5_NSTATE), sg1),
        out_shape=jax.ShapeDtypeStruct((S5_SG, S5_CH, S5_NSTATE), F32),
        compiler_params=_cparams("arbitrary"), name="s5_scan",
    )(e, ltre, ltim)
    y = pl.pallas_call(
        _s5_out_kernel, grid=grid,
        in_specs=[natural, pl.BlockSpec((1, tc, S5_NSTATE), rows),
                  pl.BlockSpec((1, (T + 1) * LANES, LANES), per_sg), pl.BlockSpec((1, S5_NSTATE, T * LANES), per_sg)],
        out_specs=natural,
        out_shape=jax.ShapeDtypeStruct((S5_CH, T, SSM_WIDTH), F32),
        compiler_params=_cparams("arbitrary", "arbitrary", vmem=56 * 1024 * 1024), name="s5_out",
    )(xn, xstart, krev, wc)
    return y.reshape(N_TOK, SSM_WIDTH)


MERGE_TM = 512


def _merge_kernel(ocmp_ref, osel_ref, owin_ref, gn_ref, yssm_ref, u_ref, ga_ref, gs_ref, x_ref, mod_ref,
                  eg_ref, dskip_ref, wglu_ref, bglu_ref, wua_ref, wus_ref, wout_ref, g2_ref,
                  wrhi_ref, wrlo_ref, wsgu_ref, wsd_ref,
                  xpart_ref, h2_ref, logit_ref):
    mod = mod_ref[0]
    gnb = gn_ref[...].astype(BF16)
    o_nsa = (_dot(gnb, eg_ref[0]) * ocmp_ref[...].astype(F32)
             + _dot(gnb, eg_ref[1]) * osel_ref[...].astype(F32)
             + _dot(gnb, eg_ref[2]) * owin_ref[...].astype(F32))
    attn = _dot(o_nsa.astype(BF16), wua_ref[...])
    z = _gelu(yssm_ref[...] + dskip_ref[...] * u_ref[...])
    y_ssm = z * jax.nn.sigmoid(_dot(z.astype(BF16), wglu_ref[...]) + bglu_ref[...])
    ssm = _dot(y_ssm.astype(BF16), wus_ref[...])
    merged = ga_ref[...].astype(F32) * attn + gs_ref[...].astype(F32) * ssm
    x1 = x_ref[...] + mod[2:3] * _dot(merged.astype(BF16), wout_ref[...])

    ms = jnp.mean(x1 * x1, axis=-1, keepdims=True)
    h2 = (x1 * lax.rsqrt(ms + EPS) * g2_ref[...]) * (1.0 + mod[4:5]) + mod[3:4]
    hi = h2.astype(BF16)
    lo = (h2 - hi.astype(F32)).astype(BF16)
    h2_ref[...] = _pack_bf16_pairs(h2)
    logit_ref[...] = _dot_nt(wrhi_ref[...], hi) + _dot_nt(wrhi_ref[...], lo) + _dot_nt(wrlo_ref[...], hi)
    gu = _dot(hi, wsgu_ref[...])
    shared = _dot((_silu(gu[:, :D_EXPERT]) * gu[:, D_EXPERT:]).astype(BF16), wsd_ref[...])
    xpart_ref[...] = x1 + mod[5:6] * shared


def _merge(ocmp, osel, owin, gn, yssm, u, ga, gs, x2, mod, d_skip, w_glu, b_glu, w_up_attn, w_up_ssm, w_out,
           g_norm2, w_router, ws_gate, ws_up, ws_down):
    tm = MERGE_TM
    eg = np.zeros((3, LANES, NSA_WIDTH), np.float32)
    for j in range(3):
        for h in range(N_HEADS):
            eg[j, 3 * h + j, h * HEAD_DIM:(h + 1) * HEAD_DIM] = 1.0
    wr_t = w_router.T
    wr_hi = wr_t.astype(BF16)
    wr_lo = (wr_t - wr_hi.astype(F32)).astype(BF16)
    row = lambda i: (i, 0)
    fix2 = lambda i: (0, 0)
    wspec = lambda a: pl.BlockSpec(a.shape, (lambda i: (0,) * a.ndim))
    weights = [jnp.asarray(eg, BF16), d_skip.reshape(1, -1), w_glu.astype(BF16), b_glu.reshape(1, -1),
               w_up_attn.astype(BF16), w_up_ssm.astype(BF16), w_out.astype(BF16), g_norm2.reshape(1, -1),
               wr_hi, wr_lo, jnp.concatenate([ws_gate, ws_up], axis=1).astype(BF16), ws_down.astype(BF16)]
    acts = [(ocmp, 512), (osel, 512), (owin, 512), (gn, 128), (yssm, 512), (u, 512), (ga, 1024), (gs, 1024),
            (x2, 1024)]
    return pl.pallas_call(
        _merge_kernel,
        grid=(N_TOK // tm,),
        in_specs=[pl.BlockSpec((tm, wd), row) for _, wd in acts]
                 + [pl.BlockSpec((1, 6, D_MODEL), lambda i: (i // (SEQ // tm), 0, 0))]
                 + [wspec(w) for w in weights],
        out_specs=[pl.BlockSpec((tm, D_MODEL), row), pl.BlockSpec((tm, HALF), row),
                   pl.BlockSpec((N_EXPERTS, tm), lambda i: (0, i))],
        out_shape=[jax.ShapeDtypeStruct((N_TOK, D_MODEL), F32), jax.ShapeDtypeStruct((N_TOK, HALF), I32),
                   jax.ShapeDtypeStruct((N_EXPERTS, N_TOK), F32)],
        compiler_params=_cparams("arbitrary", vmem=56 * 1024 * 1024),
        name="merge",
    )(*[a for a, _ in acts], mod, *weights)


ROUTE_TN = 1024


def _route_kernel(logit_ref, bias_ref, eidx_ref, w_ref, count_ref, gscore_ref, masked_ref):
    tn = ROUTE_TN

    @pl.when(pl.program_id(0) == 0)
    def _():
        count_ref[...] = jnp.zeros(count_ref.shape, F32)

    sc = jax.nn.sigmoid(logit_ref[...])
    biased = sc + bias_ref[...]
    gi = lax.broadcasted_iota(I32, (EXPERTS_PER_GROUP, tn), 0).astype(F32)
    for g in range(N_EXPERT_GROUPS):
        blk = biased[g * EXPERTS_PER_GROUP:(g + 1) * EXPERTS_PER_GROUP]
        m1 = jnp.max(blk, axis=0, keepdims=True)
        i1 = jnp.min(jnp.where(blk == m1, gi, float(EXPERTS_PER_GROUP)), axis=0, keepdims=True)
        m2 = jnp.max(jnp.where(gi == i1, -jnp.inf, blk), axis=0, keepdims=True)
        gscore_ref[g:g + 1, :] = m1 + m2
    gs = gscore_ref[...]
    gidx = lax.broadcasted_iota(I32, (N_EXPERT_GROUPS, tn), 0)
    grank = jnp.zeros((N_EXPERT_GROUPS, tn), F32)
    for gp in range(N_EXPERT_GROUPS):
        row = gs[gp:gp + 1, :]
        tie = jnp.where(gidx > gp, 1.0, 0.0)
        grank = grank + jnp.where(row > gs, 1.0, jnp.where(row == gs, tie, 0.0))
    for g in range(N_EXPERT_GROUPS):
        keep = grank[g:g + 1, :] < float(TOPK_GROUPS)
        sl = slice(g * EXPERTS_PER_GROUP, (g + 1) * EXPERTS_PER_GROUP)
        masked_ref[sl, :] = jnp.where(keep, biased[sl], -jnp.inf)
    cur = masked_ref[...]
    eidx = lax.broadcasted_iota(I32, (N_EXPERTS, tn), 0).astype(F32)
    wsum = jnp.zeros((1, tn), F32)
    hits = jnp.zeros((N_EXPERTS, tn), F32)
    for k in range(TOP_K):
        m = jnp.max(cur, axis=0, keepdims=True)
        idx = jnp.min(jnp.where(cur == m, eidx, float(N_EXPERTS)), axis=0, keepdims=True)
        hit = eidx == idx
        wk = jnp.sum(jnp.where(hit, sc, 0.0), axis=0, keepdims=True)
        cur = jnp.where(hit, -jnp.inf, cur)
        hits = hits + jnp.where(hit, 1.0, 0.0)
        eidx_ref[k:k + 1, :] = idx.astype(I32)
        w_ref[k:k + 1, :] = wk
        wsum = wsum + wk
    w_ref[...] = w_ref[...] / wsum * ROUTE_SCALE
    count_ref[...] = count_ref[...] + jnp.sum(hits, axis=1, keepdims=True)


def _route(logits_t, router_bias):
    tn = ROUTE_TN
    return pl.pallas_call(
        _route_kernel,
        grid=(N_TOK // tn,),
        in_specs=[pl.BlockSpec((N_EXPERTS, tn), lambda i: (0, i)), pl.BlockSpec((N_EXPERTS, 1), lambda i: (0, 0))],
        out_specs=[pl.BlockSpec((TOP_K, tn), lambda i: (0, i))] * 2 + [pl.BlockSpec((N_EXPERTS, 1), lambda i: (0, 0))],
        out_shape=[jax.ShapeDtypeStruct((TOP_K, N_TOK), I32), jax.ShapeDtypeStruct((TOP_K, N_TOK), F32),
                   jax.ShapeDtypeStruct((N_EXPERTS, 1), F32)],
        scratch_shapes=[pltpu.VMEM((N_EXPERT_GROUPS, tn), F32), pltpu.VMEM((N_EXPERTS, tn), F32)],
        compiler_params=_cparams("arbitrary"),
        name="route",
    )(logits_t, router_bias.reshape(-1, 1))


N_MOE_BLK = NK // DISPATCH_BLOCK
N_ITEMS = N_MOE_BLK + N_EXPERTS
ASSIGN_BITS = 17


def _dispatch_plan(eidx, counts):
    e_flat = eidx.reshape(-1)
    key = jnp.sort(e_flat * NK + jnp.arange(NK, dtype=I32))
    order = key & (NK - 1)
    counts = counts.reshape(-1).astype(I32)
    start = jnp.cumsum(counts) - counts
    cuts = jnp.sort(jnp.concatenate([jnp.arange(N_MOE_BLK, dtype=I32) * DISPATCH_BLOCK, start]))
    lo = cuts
    hi = jnp.concatenate([cuts[1:], jnp.full((1,), NK, I32)])
    blk = jnp.minimum(lo // DISPATCH_BLOCK, N_MOE_BLK - 1)
    expert = jnp.clip(jnp.sum((start[None, :] <= lo[:, None]).astype(I32), axis=1) - 1, 0, N_EXPERTS - 1)
    one = jnp.ones((1,), I32)
    first = jnp.concatenate([one, (blk[1:] != blk[:-1]).astype(I32)])
    last = jnp.concatenate([(blk[1:] != blk[:-1]).astype(I32), one])
    new_expert = jnp.concatenate([one, (expert[1:] != expert[:-1]).astype(I32)])
    run_id = jnp.cumsum(new_expert) - 1
    n_runs = run_id[-1] + 1
    item = jnp.arange(N_ITEMS, dtype=I32)
    run_first_item = jnp.sort(jnp.where(new_expert == 1, item, N_ITEMS))
    run_expert = expert[jnp.minimum(run_first_item, N_ITEMS - 1)]
    ahead = run_id + (WEIGHT_RING - 1)
    ahead_expert = run_expert[jnp.minimum(ahead, N_ITEMS - 1)]
    ahead_valid = (ahead < n_runs).astype(I32)
    second_expert = run_expert[1:2]
    prologue = jnp.concatenate([second_expert, (n_runs > 1).astype(I32).reshape(1)])
    tok = jnp.right_shift(order, 3)
    home = (order & (TOP_K - 1)) * N_TOK + tok
    return tok, home, (blk, expert, lo - blk * DISPATCH_BLOCK, hi - blk * DISPATCH_BLOCK, first, last, new_expert,
                      run_id % WEIGHT_RING, ahead_expert, ahead_valid, prologue)


SC_CORES = 2
SC_SUBCORES = 16
SC_CHUNK = 128


def _sc_move_rows(table, idx, scatter):
    n = idx.shape[0]
    workers = SC_CORES * SC_SUBCORES
    per_worker = n // workers
    n_chunks = per_worker // SC_CHUNK
    assert per_worker * workers == n and n_chunks * SC_CHUNK == per_worker
    mesh = plsc.VectorSubcoreMesh(core_axis_name="c", subcore_axis_name="s",
                                  num_cores=SC_CORES, num_subcores=SC_SUBCORES)

    def body(table_hbm, idx_hbm, out_hbm, idx_v, rows_v, sem):
        wid = lax.axis_index("s") * SC_CORES + lax.axis_index("c")
        base = wid * per_worker

        @pl.loop(0, n_chunks)
        def _(j):
            off = base + j * SC_CHUNK
            pltpu.sync_copy(idx_hbm.at[pl.ds(off, SC_CHUNK)], idx_v)
            if scatter:
                pltpu.sync_copy(table_hbm.at[pl.ds(off, SC_CHUNK)], rows_v)
                pltpu.async_copy(rows_v, out_hbm.at[idx_v], sem).wait()
            else:
                pltpu.async_copy(table_hbm.at[idx_v], rows_v, sem).wait()
                pltpu.sync_copy(rows_v, out_hbm.at[pl.ds(off, SC_CHUNK)])

    return pl.kernel(
        body,
        out_type=jax.ShapeDtypeStruct((n, table.shape[1]), table.dtype),
        mesh=mesh,
        scratch_types=[pltpu.VMEM((SC_CHUNK,), I32), pltpu.VMEM((SC_CHUNK, table.shape[1]), table.dtype),
                       pltpu.SemaphoreType.DMA],
        name="sc_scatter_rows" if scatter else "sc_gather_rows",
    )(table, idx)


WEIGHT_RING = 3
WEIGHT_CHUNKS = 4


def _expert_weight_copies(w_hbm, wbuf, sem, expert, slot):
    rows = w_hbm.shape[1] // WEIGHT_CHUNKS
    return [pltpu.make_async_copy(w_hbm.at[expert, pl.ds(c * rows, rows)],
                                  wbuf.at[slot, pl.ds(c * rows, rows)], sem.at[slot])
            for c in range(WEIGHT_CHUNKS)]


def _moe_kernel(blk_ref, exp_ref, lo_ref, hi_ref, first_ref, last_ref, newexp_ref,
                slot_ref, ahead_exp_ref, ahead_ok_ref, prologue_ref,
                x_ref, wg_hbm, wu_hbm, wd_hbm, y_ref,
                acc_ref, wgf_ref, wuf_ref, wdf_ref, wgb_ref, wub_ref, wdb_ref, wsem):
    it = pl.program_id(0)
    lo, hi = lo_ref[it], hi_ref[it]
    streams = ((wg_hbm, wgf_ref), (wu_hbm, wuf_ref), (wd_hbm, wdf_ref))

    def request(expert, slot):
        for w_hbm, wbuf in streams:
            for cp in _expert_weight_copies(w_hbm, wbuf, wsem, expert, slot):
                cp.start()

    @pl.when(it == 0)
    def _():
        request(exp_ref[0], 0)

        @pl.when(prologue_ref[1] == 1)
        def _():
            request(prologue_ref[0], 1)

    @pl.when(newexp_ref[it] == 1)
    def _():
        slot = slot_ref[it]
        for w_hbm, wbuf in streams:
            for cp in _expert_weight_copies(w_hbm, wbuf, wsem, 0, slot):
                cp.wait()
        wgb_ref[...] = wgf_ref[slot].astype(BF16)
        wub_ref[...] = wuf_ref[slot].astype(BF16)
        wdb_ref[...] = wdf_ref[slot].astype(BF16)

        @pl.when(ahead_ok_ref[it] == 1)
        def _():
            ahead_slot = slot + (WEIGHT_RING - 1)
            request(ahead_exp_ref[it], jnp.where(ahead_slot >= WEIGHT_RING, ahead_slot - WEIGHT_RING, ahead_slot))

    @pl.when(first_ref[it] == 1)
    def _():
        acc_ref[...] = jnp.zeros(acc_ref.shape, F32)

    def expert_pass(r0, nrows):
        rows = slice(r0, r0 + nrows)
        ridx = r0 + lax.broadcasted_iota(I32, (nrows, HALF), 0)
        mine = (ridx >= lo) & (ridx < hi)
        xlo, xhi = _unpack_bf16_pairs(jnp.where(mine, x_ref[rows, :], 0))
        xlo, xhi = xlo.astype(BF16), xhi.astype(BF16)
        gate = _dot(xlo, wgb_ref[:HALF]) + _dot(xhi, wgb_ref[HALF:])
        up = _dot(xlo, wub_ref[:HALF]) + _dot(xhi, wub_ref[HALF:])
        acc_ref[rows, :] = acc_ref[rows, :] + _dot((_silu(gate) * up).astype(BF16), wdb_ref[...])

    mid = DISPATCH_BLOCK // 2
    pl.when((lo < mid) & (hi > mid))(lambda: expert_pass(0, DISPATCH_BLOCK))
    pl.when((hi > lo) & (hi <= mid))(lambda: expert_pass(0, mid))
    pl.when((hi > lo) & (lo >= mid))(lambda: expert_pass(mid, mid))

    @pl.when(last_ref[it] == 1)
    def _():
        y_ref[...] = _pack_bf16_pairs(acc_ref[...])


def _moe(xs, items, w_gate, w_up, w_down):
    by_blk = lambda it, blk, *_: (blk[it], 0)
    any_space = pl.BlockSpec(memory_space=pl.ANY)
    grid_spec = pltpu.PrefetchScalarGridSpec(
        num_scalar_prefetch=len(items),
        grid=(N_ITEMS,),
        in_specs=[pl.BlockSpec((DISPATCH_BLOCK, HALF), by_blk), any_space, any_space, any_space],
        out_specs=pl.BlockSpec((DISPATCH_BLOCK, HALF), by_blk),
        scratch_shapes=[pltpu.VMEM((DISPATCH_BLOCK, D_MODEL), F32),
                        pltpu.VMEM((WEIGHT_RING, D_MODEL, D_EXPERT), F32),
                        pltpu.VMEM((WEIGHT_RING, D_MODEL, D_EXPERT), F32),
                        pltpu.VMEM((WEIGHT_RING, D_EXPERT, D_MODEL), F32),
                        pltpu.VMEM((D_MODEL, D_EXPERT), BF16), pltpu.VMEM((D_MODEL, D_EXPERT), BF16),
                        pltpu.VMEM((D_EXPERT, D_MODEL), BF16),
                        pltpu.SemaphoreType.DMA((WEIGHT_RING,))],
    )
    return pl.pallas_call(
        _moe_kernel,
        grid_spec=grid_spec,
        out_shape=jax.ShapeDtypeStruct((NK, HALF), I32),
        compiler_params=_cparams("arbitrary"),
        name="moe",
    )(*items, xs, w_gate, w_up, w_down)


COMB_TC = 512


def _combine_kernel(slots_ref, w_ref, xpart_ref, mod_ref, out_ref):
    w = w_ref[...]
    lo = jnp.zeros((w.shape[0], HALF), F32)
    hi = jnp.zeros((w.shape[0], HALF), F32)
    for k in range(TOP_K):
        klo, khi = _unpack_bf16_pairs(slots_ref[k])
        lo = lo + w[:, k:k + 1] * klo
        hi = hi + w[:, k:k + 1] * khi
    gate2 = mod_ref[0][5:6]
    out_ref[:, :HALF] = xpart_ref[:, :HALF] + gate2[:, :HALF] * lo
    out_ref[:, HALF:] = xpart_ref[:, HALF:] + gate2[:, HALF:] * hi


def _combine(xpart, mod, slots, w):
    tc = COMB_TC
    row = lambda i: (i, 0)
    return pl.pallas_call(
        _combine_kernel,
        grid=(N_TOK // tc,),
        in_specs=[pl.BlockSpec((TOP_K, tc, HALF), lambda i: (0, i, 0)),
                  pl.BlockSpec((tc, TOP_K), row),
                  pl.BlockSpec((tc, D_MODEL), row),
                  pl.BlockSpec((1, 6, D_MODEL), lambda i: (i // (SEQ // tc), 0, 0))],
        out_specs=pl.BlockSpec((tc, D_MODEL), row),
        out_shape=jax.ShapeDtypeStruct((N_TOK, D_MODEL), F32),
        compiler_params=_cparams("arbitrary"),
        name="combine",
    )(slots.reshape(TOP_K, N_TOK, HALF), w, xpart, mod)


def _layer(x, c, w_ada, b_ada, g_norm1, g_norm2, w_in, q_gain, kc_gain, ks_gain, kw_gain,
           pe_k, pe_v, w_cmp_k1, w_cmp_k2, w_cmp_v1, w_cmp_v2,
           a_re, a_im, log_dt, b_re, b_im, c_re, c_im, d_skip, w_glu, b_glu,
           w_up_attn, w_up_ssm, w_out, w_router, router_bias,
           w_gate, w_up, w_down, ws_gate, ws_up, ws_down):
    x2 = x.reshape(N_TOK, D_MODEL)
    mod = _ada(c, w_ada, b_ada)
    q, kc_raw, vc_raw, ks, kw, vst, vwt, gn, u, ga, gs = _proj(x2, mod, g_norm1, w_in, q_gain, ks_gain, kw_gain)
    kcn = _compress(kc_raw, pe_k, w_cmp_k1, w_cmp_k2, kc_gain, True)
    vcn = _compress(vc_raw, pe_v, w_cmp_v1, w_cmp_v2, kc_gain, False)
    ocmp, selb = _cmp_attn(q, kcn, vcn)
    osel, owin = _selwin(q, ks, kw, vst, vwt, selb)
    yssm = _s5(u, *_s5_params(a_re, a_im, log_dt, b_re, b_im, c_re, c_im))
    xpart, h2, logits_t = _merge(ocmp, osel, owin, gn, yssm, u, ga, gs, x2, mod, d_skip, w_glu, b_glu,
                                  w_up_attn, w_up_ssm, w_out, g_norm2, w_router, ws_gate, ws_up, ws_down)
    eidx_t, w_t, counts = _route(logits_t, router_bias)
    tok, home, items = _dispatch_plan(eidx_t.T, counts)
    y = _moe(_sc_move_rows(h2, tok, scatter=False), items, w_gate, w_up, w_down)
    slots = _sc_move_rows(y, home, scatter=True)
    return _combine(xpart, mod, slots, w_t.T).reshape(BATCH, SEQ, D_MODEL)


def kernel(x, c, w_ada, b_ada, g_norm1, g_norm2, w_in, q_gain, kc_gain, ks_gain, kw_gain, pe_k, pe_v, w_cmp_k1,
           w_cmp_k2, w_cmp_v1, w_cmp_v2, a_re, a_im, log_dt, b_re, b_im, c_re, c_im, d_skip, w_glu, b_glu,
           w_up_attn, w_up_ssm, w_out, w_router, router_bias, w_gate, w_up, w_down, ws_gate, ws_up, ws_down):
    params = (w_ada, b_ada, g_norm1, g_norm2, w_in, q_gain, kc_gain, ks_gain, kw_gain, pe_k, pe_v, w_cmp_k1,
              w_cmp_k2, w_cmp_v1, w_cmp_v2, a_re, a_im, log_dt, b_re, b_im, c_re, c_im, d_skip, w_glu, b_glu,
              w_up_attn, w_up_ssm, w_out, w_router, router_bias, w_gate, w_up, w_down, ws_gate, ws_up, ws_down)
    depth = w_ada.shape[0]
    for layer in range(depth):
        x = _layer(x, c, *[p[layer] for p in params])
    return x
```

```python
import functools
import math

import jax
import jax.numpy as jnp
import numpy as np
from jax import lax
from jax.experimental import pallas as pl
from jax.experimental.pallas import tpu as pltpu
from jax.experimental.pallas import tpu_sc as plsc

F32 = jnp.float32
BF16 = jnp.bfloat16
I32 = jnp.int32
HIGHEST = lax.Precision.HIGHEST

D_MODEL = 1024
BATCH = 4
SEQ = 4096
N_TOK = BATCH * SEQ
N_HEADS = 8
HEAD_DIM = 64
N_KV = 2
CMP_BLOCK = 32
CMP_STRIDE = 16
CMP_HIDDEN = 256
N_CMP = 256
SEL_BLOCK = 64
N_SEL_BLOCKS = SEQ // SEL_BLOCK
N_SELECT = 16
WINDOW = 512
ATTN_SCALE = HEAD_DIM ** -0.5
LOG2E = 1.4426950408889634
NSA_WIDTH = N_HEADS * HEAD_DIM
SSM_WIDTH = 512
GROUP = 16
N_GROUPS = SSM_WIDTH // GROUP
STATE = 64
N_EXPERTS = 256
TOP_K = 8
D_EXPERT = 256
N_EXPERT_GROUPS = 8
EXPERTS_PER_GROUP = N_EXPERTS // N_EXPERT_GROUPS
TOPK_GROUPS = 4
ROUTE_SCALE = 2.5
DISPATCH_BLOCK = 1024
EPS = 1e-6
NEG = -1e30

LANES = 128
S5_T = 16
S5_SG = 4
S5_GL = N_GROUPS // S5_SG
S5_CH = N_TOK // S5_T
S5_CH_PER_BATCH = SEQ // S5_T
S5_NSTATE = S5_GL * STATE * 2

NK = N_TOK * TOP_K
HALF = D_MODEL // 2

VMEM_LIMIT = 48 * 1024 * 1024


def _cparams(*sem, vmem=VMEM_LIMIT):
    return pltpu.CompilerParams(dimension_semantics=tuple(sem), vmem_limit_bytes=vmem)


def _dot(a, b):
    return jnp.dot(a, b, preferred_element_type=F32)


def _dot_nt(a, b):
    return lax.dot_general(a, b, (((1,), (1,)), ((), ())), preferred_element_type=F32)


def _split_dot(v, w):
    hi = v.astype(BF16)
    lo = (v - hi.astype(F32)).astype(BF16)
    return _dot(hi, w) + _dot(lo, w)


def _seg_rms(v, bd, gain):
    ss = _split_dot(v * v, bd)
    return v * lax.rsqrt(ss * (1.0 / HEAD_DIM) + EPS) * gain


def _gelu(x):
    return 0.5 * x * (1.0 + jnp.tanh(0.7978845608028654 * (x + 0.044715 * (x * x * x))))


def _silu(x):
    return x * jax.nn.sigmoid(x)


def _pack_bf16_pairs(v):
    h = v.shape[1] // 2
    return pltpu.pack_elementwise([v[:, :h], v[:, h:]], packed_dtype=BF16)


def _unpack_bf16_pairs(word):
    return (pltpu.unpack_elementwise(word, index=0, packed_dtype=BF16, unpacked_dtype=F32),
            pltpu.unpack_elementwise(word, index=1, packed_dtype=BF16, unpacked_dtype=F32))


def _ada_kernel(c_ref, w_ref, b_ref, o_ref):
    c = c_ref[...]
    o_ref[...] = jnp.dot(_silu(c), w_ref[...], preferred_element_type=F32, precision=HIGHEST) + b_ref[...]


def _ada(c, w_ada, b_ada):
    cp = jnp.pad(c, ((0, 8 - BATCH), (0, 0)))
    tn = 1536
    out = pl.pallas_call(
        _ada_kernel,
        grid=(6 * D_MODEL // tn,),
        in_specs=[pl.BlockSpec((8, D_MODEL), lambda j: (0, 0)),
                  pl.BlockSpec((D_MODEL, tn), lambda j: (0, j)),
                  pl.BlockSpec((1, tn), lambda j: (0, j))],
        out_specs=pl.BlockSpec((8, tn), lambda j: (0, j)),
        out_shape=jax.ShapeDtypeStruct((8, 6 * D_MODEL), F32),
        compiler_params=_cparams("arbitrary"),
        name="ada",
    )(cp, w_ada, b_ada.reshape(1, -1))
    return out.reshape(8, 6, D_MODEL)


_C_Q = 0
_C_KC = 512
_C_VC = 640
_C_KS = 768
_C_KW = 1024
_C_GN = 1280
_C_U = 1408
_C_GA = 1920
_C_GS = 2944
_C_END = 3968
PROJ_TM = 512


def _proj_kernel(x_ref, mod_ref, g1_ref, w_ref, wvt_ref, qg_ref, ksg_ref, kwg_ref, bd512_ref, bd256_ref,
                 q_ref, kc_ref, vc_ref, ks_ref, kw_ref, vst_ref, vwt_ref, gn_ref, u_ref, ga_ref, gs_ref):
    x = x_ref[...]
    ms = jnp.mean(x * x, axis=-1, keepdims=True)
    mod = mod_ref[0]
    h = (x * lax.rsqrt(ms + EPS) * g1_ref[...]) * (1.0 + mod[1:2]) + mod[0:1]
    hb = h.astype(BF16)

    def p(lo, hi):
        return _dot(hb, w_ref[:, lo:hi])

    q_ref[...] = _seg_rms(p(_C_Q, _C_KC), bd512_ref[...], qg_ref[...] * (ATTN_SCALE * LOG2E)).astype(BF16)
    kc_ref[...] = p(_C_KC, _C_VC)
    vc_ref[...] = p(_C_VC, _C_KS)
    ks_ref[...] = _seg_rms(p(_C_KS, _C_KW), bd256_ref[...], ksg_ref[...]).astype(BF16)
    kw_ref[...] = _seg_rms(p(_C_KW, _C_GN), bd256_ref[...], kwg_ref[...]).astype(BF16)
    vt = _dot_nt(wvt_ref[...], hb)
    vst_ref[...] = vt[:LANES].astype(BF16)
    vwt_ref[...] = vt[LANES:].astype(BF16)
    gn_ref[...] = jax.nn.sigmoid(p(_C_GN, _C_U))
    u_ref[...] = p(_C_U, _C_GA)
    ga_ref[...] = jax.nn.sigmoid(p(_C_GA, _C_GS)).astype(BF16)
    gs_ref[...] = jax.nn.sigmoid(p(_C_GS, _C_END)).astype(BF16)


def _dup_cols(w):
    return jnp.concatenate([w[:, :64], w[:, :64], w[:, 64:], w[:, 64:]], axis=1)


def _block_ones(n):
    return jnp.kron(jnp.eye(n // HEAD_DIM, dtype=F32), jnp.ones((HEAD_DIM, HEAD_DIM), F32)).astype(BF16)


def _proj(x2, mod, g_norm1, w_in, q_gain, ks_gain, kw_gain):
    o = np.cumsum((0, 512, 128, 128, 128, 128, 128, 128, 24, 512, 1024, 1024))
    parts = [w_in[:, o[i]:o[i + 1]] for i in range(11)]
    wq, wkc, wvc, wks, wvs, wkw, wvw, wgn, wu, wga, wgs = parts
    w = jnp.concatenate([wq, wkc, wvc, _dup_cols(wks), _dup_cols(wkw),
                         jnp.pad(wgn, ((0, 0), (0, LANES - 24))), wu, wga, wgs], axis=1).astype(BF16)
    wvt = jnp.concatenate([wvs, wvw], axis=1).T.astype(BF16)
    tm = PROJ_TM
    row = lambda i: (i, 0)
    col = lambda i: (0, i)
    fix = lambda i: (0, 0)
    outs = [(512, BF16, row), (128, F32, row), (128, F32, row), (256, BF16, row), (256, BF16, row),
            (LANES, BF16, col), (LANES, BF16, col),
            (128, F32, row), (512, F32, row), (1024, BF16, row), (1024, BF16, row)]
    ospec = lambda wd, m: pl.BlockSpec((tm, wd), m) if m is row else pl.BlockSpec((wd, tm), m)
    oshape = lambda wd, dt, m: jax.ShapeDtypeStruct((N_TOK, wd) if m is row else (wd, N_TOK), dt)
    return pl.pallas_call(
        _proj_kernel,
        grid=(N_TOK // tm,),
        in_specs=[pl.BlockSpec((tm, D_MODEL), row),
                  pl.BlockSpec((1, 6, D_MODEL), lambda i: (i // (SEQ // tm), 0, 0)),
                  pl.BlockSpec((1, D_MODEL), fix),
                  pl.BlockSpec((D_MODEL, _C_END), fix),
                  pl.BlockSpec((2 * LANES, D_MODEL), fix),
                  pl.BlockSpec((1, 512), fix), pl.BlockSpec((1, 256), fix), pl.BlockSpec((1, 256), fix),
                  pl.BlockSpec((512, 512), fix), pl.BlockSpec((256, 256), fix)],
        out_specs=[ospec(wd, m) for wd, _, m in outs],
        out_shape=[oshape(wd, dt, m) for wd, dt, m in outs],
        compiler_params=_cparams("arbitrary"),
        name="proj",
    )(x2, mod, g_norm1.reshape(1, -1), w, wvt,
      jnp.tile(q_gain, N_HEADS).reshape(1, -1), jnp.tile(ks_gain, 4).reshape(1, -1),
      jnp.tile(kw_gain, 4).reshape(1, -1), _block_ones(512), _block_ones(256))


def _compress_kernel(r_ref, pe_ref, w1_ref, w2_ref, bd_ref, gain_ref, o_ref, *, do_norm):
    r = jnp.concatenate([r_ref[0, :, l, :] for l in range(CMP_STRIDE)], axis=1)
    p0 = _dot((r + pe_ref[0]).astype(BF16), w1_ref[0])
    p1 = _dot((r + pe_ref[1]).astype(BF16), w1_ref[1])
    hid = p0 + pltpu.roll(p1, N_CMP - 1, 0)
    c = _dot(_gelu(hid).astype(BF16), w2_ref[...])
    if do_norm:
        c = _seg_rms(c, bd_ref[...], gain_ref[...])
    o_ref[0] = c.astype(BF16)


def _compress(raw, pe, w1, w2, gain, do_norm):
    r = raw.reshape(BATCH, SEQ // CMP_STRIDE, CMP_STRIDE, LANES)
    eye = jnp.eye(N_KV, dtype=F32)
    w1r = w1.reshape(2, CMP_STRIDE, HEAD_DIM, CMP_HIDDEN)
    w1big = jnp.einsum('hldc,gk->hlgdkc', w1r, eye).reshape(2, CMP_STRIDE * LANES, N_KV * CMP_HIDDEN).astype(BF16)
    w2big = jnp.einsum('cd,gk->gckd', w2, eye)
    w2big = jnp.concatenate([w2big, w2big], axis=-1).reshape(N_KV * CMP_HIDDEN, 4 * HEAD_DIM).astype(BF16)
    pe_big = jnp.broadcast_to(pe.reshape(2, CMP_STRIDE, 1, HEAD_DIM), (2, CMP_STRIDE, N_KV, HEAD_DIM))
    pe_big = pe_big.reshape(2, 1, CMP_STRIDE * LANES)
    fix2 = lambda b: (0, 0)
    fix3 = lambda b: (0, 0, 0)
    return pl.pallas_call(
        functools.partial(_compress_kernel, do_norm=do_norm),
        grid=(BATCH,),
        in_specs=[pl.BlockSpec((1, N_CMP, CMP_STRIDE, LANES), lambda b: (b, 0, 0, 0)),
                  pl.BlockSpec((2, 1, CMP_STRIDE * LANES), fix3),
                  pl.BlockSpec((2, CMP_STRIDE * LANES, N_KV * CMP_HIDDEN), fix3),
                  pl.BlockSpec((N_KV * CMP_HIDDEN, 256), fix2),
                  pl.BlockSpec((256, 256), fix2), pl.BlockSpec((1, 256), fix2)],
        out_specs=pl.BlockSpec((1, N_CMP, 256), lambda b: (b, 0, 0)),
        out_shape=jax.ShapeDtypeStruct((BATCH, N_CMP, 256), BF16),
        compiler_params=_cparams("arbitrary"),
        name="compress_k" if do_norm else "compress_v",
    )(r, pe_big, w1big, w2big, _block_ones(256), jnp.tile(gain, 4).reshape(1, -1))


ATT_TQ = 1024
RANK_CHUNK = 16


def _head_variants(qb):
    lane = lax.broadcasted_iota(I32, qb.shape, 1)
    z = jnp.zeros_like(qb)
    return jnp.where(lane < HEAD_DIM, qb, z), jnp.where(lane < HEAD_DIM, z, qb)


def _cmp_kernel(q_ref, kc_ref, vc_ref, ov_ref, o_ref, sel_ref, vrank_ref):
    tq = ATT_TQ
    qi = pl.program_id(1)
    tpos = qi * tq + lax.broadcasted_iota(I32, (tq, N_CMP), 0)
    nidx = lax.broadcasted_iota(I32, (tq, N_CMP), 1)
    mask = (CMP_STRIDE * nidx + (CMP_BLOCK - 1)) <= tpos
    lane_lo = lax.broadcasted_iota(I32, (tq, LANES), 1) < HEAD_DIM
    for g in range(N_KV):
        kd = kc_ref[0, :, g * LANES:(g + 1) * LANES]
        vd = vc_ref[0, :, g * LANES:(g + 1) * LANES]
        psum = jnp.zeros((tq, N_CMP), F32)
        for jb in range(2):
            blk = 2 * g + jb
            pv = []
            for qv in _head_variants(q_ref[:, blk * LANES:(blk + 1) * LANES]):
                s = jnp.where(mask, _dot_nt(qv, kd), NEG)
                m = jnp.max(s, axis=-1, keepdims=True)
                e = jnp.where(mask, jnp.exp2(s - m), 0.0)
                l = jnp.sum(e, axis=-1, keepdims=True)
                p = e / jnp.where(l > 0.0, l, 1.0)
                psum = psum + p
                pv.append(_dot(p.astype(BF16), vd))
            o_ref[:, blk * LANES:(blk + 1) * LANES] = jnp.where(lane_lo, pv[0], pv[1]).astype(BF16)
        imp = _split_dot(psum, ov_ref[...])
        imp_t = imp.T[:N_SEL_BLOCKS]
        j = lax.broadcasted_iota(I32, (N_SEL_BLOCKS, tq), 0)
        cur = jnp.right_shift(qi * tq + lax.broadcasted_iota(I32, (N_SEL_BLOCKS, tq), 1), 6)
        forced = (j == 0) | (j == cur) | (j == cur - 1)
        v = jnp.where(forced, jnp.inf, jnp.where(j <= cur, imp_t, -jnp.inf))
        vrank_ref[...] = jnp.zeros((N_SEL_BLOCKS, tq), F32)
        n_live = (qi + 1) * (tq // SEL_BLOCK)
        for c0 in range(0, N_SEL_BLOCKS, RANK_CHUNK):
            @pl.when(c0 < n_live)
            def _():
                rank = vrank_ref[...]
                for jp in range(c0, c0 + RANK_CHUNK):
                    row = v[jp:jp + 1, :]
                    tie = jnp.where(j > jp, 1.0, 0.0)
                    rank = rank + jnp.where(row > v, 1.0, jnp.where(row == v, tie, 0.0))
                vrank_ref[...] = rank
        rank = vrank_ref[...]
        sel_ref[g * N_SEL_BLOCKS:(g + 1) * N_SEL_BLOCKS, :] = jnp.where(rank < float(N_SELECT), 0.0, NEG)


def _cmp_attn(q, kcn, vcn):
    nc = np.arange(N_CMP)
    sb = np.arange(LANES)
    ov = ((CMP_STRIDE * nc[:, None] < SEL_BLOCK * sb[None, :] + SEL_BLOCK)
          & (CMP_STRIDE * nc[:, None] + CMP_BLOCK > SEL_BLOCK * sb[None, :])
          & (nc[:, None] < N_CMP - 1) & (sb[None, :] < N_SEL_BLOCKS))
    ov = jnp.asarray(ov, BF16)
    tq = ATT_TQ
    nq = SEQ // tq
    row = lambda b, i: (b * nq + i, 0)
    return pl.pallas_call(
        _cmp_kernel,
        grid=(BATCH, nq),
        in_specs=[pl.BlockSpec((tq, NSA_WIDTH), row),
                  pl.BlockSpec((1, N_CMP, 256), lambda b, i: (b, 0, 0)),
                  pl.BlockSpec((1, N_CMP, 256), lambda b, i: (b, 0, 0)),
                  pl.BlockSpec((N_CMP, LANES), lambda b, i: (0, 0))],
        out_specs=[pl.BlockSpec((tq, NSA_WIDTH), row),
                   pl.BlockSpec((N_KV * N_SEL_BLOCKS, tq), lambda b, i: (0, b * nq + i))],
        out_shape=[jax.ShapeDtypeStruct((N_TOK, NSA_WIDTH), BF16),
                   jax.ShapeDtypeStruct((N_KV * N_SEL_BLOCKS, N_TOK), F32)],
        scratch_shapes=[pltpu.VMEM((N_SEL_BLOCKS, tq), F32)],
        compiler_params=_cparams("arbitrary", "arbitrary"),
        name="cmp_attn",
    )(q, kcn, vcn, ov)


ATT_TK = 256


M_INIT = -1e29


SUM_ROWS = 16
SW_TQ = 512
SEL_TK = 512


def _selwin_kernel(q_ref, ks_ref, kw_ref, vst_ref, vwt_ref, selb_ref, osel_ref, owin_ref, m_ref, acc_ref):
    tq = tk = SW_TQ
    qi = pl.program_id(1)
    krow = lax.broadcasted_iota(I32, (tk, tq), 0)
    qcol = lax.broadcasted_iota(I32, (tk, tq), 1)
    causal_bias = jnp.where(krow <= qcol, 0.0, NEG)
    far_bias = jnp.where(qcol < krow, 0.0, NEG)

    def reset():
        m_ref[...] = jnp.full(m_ref.shape, M_INIT, F32)
        acc_ref[...] = jnp.zeros(acc_ref.shape, F32)

    def update(g, k_ref, vt_ref, kt, bias, nk=tk):
        k0 = pl.multiple_of(kt * nk, nk)
        kd = k_ref[0, pl.ds(k0, nk), g * LANES:(g + 1) * LANES]
        vt = vt_ref[g * HEAD_DIM:(g + 1) * HEAD_DIM, pl.ds(k0, nk)]
        s = _dot_nt(kd, qvars[g])
        if bias is not None:
            s = s + jnp.concatenate([bias] * 4, axis=1)
        m_old = m_ref[g]
        m_new = jnp.maximum(m_old, jnp.max(s, axis=0, keepdims=True))
        alpha = jnp.exp2(m_old - m_new)
        p = jnp.exp2(s - m_new)
        m_ref[g] = m_new
        vte = jnp.concatenate([vt, jnp.ones((SUM_ROWS, nk), BF16)], axis=0)
        acc_ref[g] = alpha * acc_ref[g] + _dot(vte, p.astype(BF16))

    def finish(out_ref, g):
        o = acc_ref[g, :HEAD_DIM, :] / acc_ref[g, HEAD_DIM:HEAD_DIM + 1, :]
        for jb in range(2):
            blk = 2 * g + jb
            pair = jnp.concatenate([o[:, 2 * jb * tq:(2 * jb + 1) * tq], o[:, (2 * jb + 1) * tq:(2 * jb + 2) * tq]],
                                   axis=0)
            out_ref[:, blk * LANES:(blk + 1) * LANES] = pair.T.astype(BF16)

    def sel_bias(g, kt):
        rows = [jnp.broadcast_to(selb_ref[pl.ds(g * N_SEL_BLOCKS + kt * (SEL_TK // SEL_BLOCK) + r, 1), :],
                                 (SEL_BLOCK, tq)) for r in range(SEL_TK // SEL_BLOCK)]
        return jnp.concatenate(rows, axis=0)

    qvars = []
    for g in range(N_KV):
        heads = []
        for jb in range(2):
            heads.extend(_head_variants(q_ref[:, (2 * g + jb) * LANES:(2 * g + jb + 1) * LANES]))
        qvars.append(jnp.concatenate(heads, axis=0))
    groups = range(N_KV)

    reset()
    last_tile = qi // (SEL_TK // tq)

    def sel_step(kt, carry):
        for g in groups:
            update(g, ks_ref, vst_ref, kt, sel_bias(g, kt), SEL_TK)
        return carry

    lax.fori_loop(0, last_tile, sel_step, 0)
    q_first = qi * tq - last_tile * SEL_TK
    visible = (lax.broadcasted_iota(I32, (SEL_TK, tq), 0)
               <= lax.broadcasted_iota(I32, (SEL_TK, tq), 1) + q_first)
    diag_bias = jnp.where(visible, 0.0, NEG)
    for g in groups:
        update(g, ks_ref, vst_ref, last_tile, sel_bias(g, last_tile) + diag_bias, SEL_TK)
    for g in groups:
        finish(osel_ref, g)

    reset()
    back = WINDOW // tk

    @pl.when(qi >= back)
    def _():
        for g in groups:
            update(g, kw_ref, vwt_ref, qi - back, far_bias)

    for d in range(back - 1, 0, -1):
        @pl.when(qi >= d)
        def _():
            for g in groups:
                update(g, kw_ref, vwt_ref, qi - d, None)

    for g in groups:
        update(g, kw_ref, vwt_ref, qi, causal_bias)
    for g in groups:
        finish(owin_ref, g)


def _selwin(q, ks, kw, vst, vwt, selb):
    tq = SW_TQ
    nq = SEQ // tq
    assert WINDOW % SW_TQ == 0 and SEL_TK % SW_TQ == 0
    row = lambda b, i: (b * nq + i, 0)
    keys = pl.BlockSpec((1, SEQ, 256), lambda b, i: (b, 0, 0))
    vals = pl.BlockSpec((LANES, SEQ), lambda b, i: (0, b))
    r3 = lambda a: a.reshape(BATCH, SEQ, 256)
    return pl.pallas_call(
        _selwin_kernel,
        grid=(BATCH, nq),
        in_specs=[pl.BlockSpec((tq, NSA_WIDTH), row), keys, keys, vals, vals,
                  pl.BlockSpec((N_KV * N_SEL_BLOCKS, tq), lambda b, i: (0, b * nq + i))],
        out_specs=[pl.BlockSpec((tq, NSA_WIDTH), row)] * 2,
        out_shape=[jax.ShapeDtypeStruct((N_TOK, NSA_WIDTH), BF16)] * 2,
        scratch_shapes=[pltpu.VMEM((N_KV, 1, 4 * tq), F32),
                        pltpu.VMEM((N_KV, HEAD_DIM + SUM_ROWS, 4 * tq), F32)],
        compiler_params=_cparams("arbitrary", "arbitrary"),
        name="selwin",
    )(q, r3(ks), r3(kw), vst, vwt, selb)


def _s5_param_kernel(are_ref, aim_ref, ldt_ref, cre_ref, cim_ref, bre_ref, bim_ref,
                     clre_ref, clim_ref, wbre_ref, wbim_ref, bbre_ref, bbim_ref, ltre_ref, ltim_ref):
    are, aim = are_ref[...], aim_ref[...]
    dt = jnp.exp(ldt_ref[...])
    cre, cim = cre_ref[...], cim_ref[...]

    def lam_pow(tau):
        mag = jnp.exp(are * dt * float(tau))
        ang = aim * dt * float(tau)
        return mag * jnp.cos(ang), mag * jnp.sin(ang)

    lre, lim = lam_pow(1)
    den = are * are + aim * aim
    qre = ((lre - 1.0) * are + lim * aim) / den
    qim = (lim * are - (lre - 1.0) * aim) / den
    bre, bim = bre_ref[...], bim_ref[...]
    bbre = qre * bre - qim * bim
    bbim = qre * bim + qim * bre
    bbre_ref[...] = bbre
    bbim_ref[...] = bbim
    for tau in range(S5_T + 1):
        pr, pi = lam_pow(tau)
        clre_ref[tau] = cre * pr - cim * pi
        clim_ref[tau] = cre * pi + cim * pr
        if tau < S5_T:
            k = S5_T - 1 - tau
            wbre_ref[k] = pr * bbre - pi * bbim
            wbim_ref[k] = pr * bbim + pi * bbre
        else:
            ltre_ref[...] = pr
            ltim_ref[...] = pi


def _s5_kmat_kernel(l_ref, r_ref, o_ref):
    o_ref[0] = jnp.dot(l_ref[0], r_ref[0], preferred_element_type=F32, precision=HIGHEST)


def _s5_params(a_re, a_im, log_dt, b_re, b_im, c_re, c_im):
    T = S5_T
    pn = GROUP * STATE
    tile_p = lambda a: jnp.tile(a, (1, GROUP))
    args = (tile_p(a_re), tile_p(a_im), jnp.broadcast_to(log_dt[:, None], (N_GROUPS, pn)),
            c_re.reshape(N_GROUPS, pn), c_im.reshape(N_GROUPS, pn),
            jnp.swapaxes(b_re, 1, 2).reshape(N_GROUPS, pn), jnp.swapaxes(b_im, 1, 2).reshape(N_GROUPS, pn))
    full2 = pl.BlockSpec((N_GROUPS, pn), lambda: (0, 0))
    clre, clim, wbre, wbim, bbre, bbim, ltre, ltim = pl.pallas_call(
        _s5_param_kernel,
        in_specs=[full2] * 7,
        out_specs=[pl.BlockSpec((T + 1, N_GROUPS, pn), lambda: (0, 0, 0))] * 2
                  + [pl.BlockSpec((T, N_GROUPS, pn), lambda: (0, 0, 0))] * 2 + [full2] * 4,
        out_shape=[jax.ShapeDtypeStruct((T + 1, N_GROUPS, pn), F32)] * 2
                  + [jax.ShapeDtypeStruct((T, N_GROUPS, pn), F32)] * 2
                  + [jax.ShapeDtypeStruct((N_GROUPS, pn), F32)] * 4,
        name="s5_params",
    )(*args)

    r5 = lambda a, t: a[:t].reshape(t, N_GROUPS, GROUP, STATE)
    lhs = jnp.concatenate([r5(clre, T), -r5(clim, T)], axis=-1)
    lhs = jnp.transpose(lhs, (1, 0, 2, 3)).reshape(N_GROUPS, T * GROUP, 2 * STATE)
    bb = lambda a: jnp.swapaxes(a.reshape(N_GROUPS, GROUP, STATE), 1, 2)
    rhs = jnp.concatenate([bb(bbre), bb(bbim)], axis=1)
    kmat = pl.pallas_call(
        _s5_kmat_kernel,
        grid=(N_GROUPS,),
        in_specs=[pl.BlockSpec((1, T * GROUP, 2 * STATE), lambda g: (g, 0, 0)),
                  pl.BlockSpec((1, 2 * STATE, GROUP), lambda g: (g, 0, 0))],
        out_specs=pl.BlockSpec((1, T * GROUP, GROUP), lambda g: (g, 0, 0)),
        out_shape=jax.ShapeDtypeStruct((N_GROUPS, T * GROUP, GROUP), F32),
        compiler_params=_cparams("arbitrary"),
        name="s5_kmat",
    )(lhs, rhs)

    eye = jnp.eye(S5_GL, dtype=F32)
    kt = kmat.reshape(S5_SG, S5_GL, T, GROUP, GROUP)
    kbd = jnp.einsum('sgtpq,gh->stgqhp', kt, eye).reshape(S5_SG, T, LANES, LANES)
    krev = kbd[:, ::-1].reshape(S5_SG, T * LANES, LANES).astype(BF16)
    krev = jnp.pad(krev, ((0, 0), (0, LANES), (0, 0)))
    r6 = lambda a: a.reshape(T, S5_SG, S5_GL, GROUP, STATE)
    wb = jnp.stack([r6(wbre), r6(wbim)], axis=-2)
    wb = jnp.einsum('ksgpin,gh->skgpihn', wb, eye).reshape(S5_SG, T * LANES, S5_NSTATE).astype(BF16)
    wc = jnp.stack([r6(clre[1:]), -r6(clim[1:])], axis=-2)
    wc = jnp.einsum('tsgpin,gh->signthp', wc, eye).reshape(S5_SG, S5_NSTATE, T * LANES).astype(BF16)
    lt = lambda a: a.reshape(N_GROUPS, GROUP, STATE)[:, 0].reshape(S5_SG, 1, S5_GL * STATE)
    return krev, wb, wc, lt(ltre), lt(ltim)


S5_TC = 256


def _s5_lane_block(sg):
    return pl.ds(pl.multiple_of(sg * LANES, LANES), LANES)


def _s5_chunk_inputs(x_ref, sg):
    return jnp.concatenate([x_ref[:, t, _s5_lane_block(sg)] for t in range(S5_T)], axis=1).astype(BF16)


def _s5_state_kernel(x_ref, wb_ref, e_ref):
    e_ref[0] = _dot(_s5_chunk_inputs(x_ref, pl.program_id(1)), wb_ref[0])


def _s5_scan_kernel(e_ref, ltre_ref, ltim_ref, xs_ref):
    lr, li = ltre_ref[0], ltim_ref[0]
    half = S5_NSTATE // 2

    def step(c, carry):
        new = []
        for b in range(BATCH):
            xr, xi = carry[b]
            row = b * S5_CH_PER_BATCH + c
            xs_ref[0, pl.ds(row, 1), :half] = xr
            xs_ref[0, pl.ds(row, 1), half:] = xi
            e = e_ref[0, pl.ds(row, 1), :]
            new.append((lr * xr - li * xi + e[:, :half], lr * xi + li * xr + e[:, half:]))
        return tuple(new)

    zero = jnp.zeros((1, half), F32)
    lax.fori_loop(0, S5_CH_PER_BATCH, step, tuple((zero, zero) for _ in range(BATCH)))


def _s5_out_kernel(x_ref, xs_ref, krev_ref, wc_ref, y_ref):
    sg = pl.program_id(1)
    x = _s5_chunk_inputs(x_ref, sg)
    xsb = xs_ref[0].astype(BF16)
    for t in range(0, S5_T, 2):
        n_in = (t + 2) * LANES
        first = (S5_T - 1 - t) * LANES
        taps = jnp.concatenate([krev_ref[0, first:first + n_in, :], krev_ref[0, first - LANES:first - LANES + n_in, :]],
                               axis=1)
        pair = _dot(x[:, :n_in], taps) + _dot(xsb, wc_ref[0, :, t * LANES:(t + 2) * LANES])
        y_ref[:, t, _s5_lane_block(sg)] = pair[:, :LANES]
        y_ref[:, t + 1, _s5_lane_block(sg)] = pair[:, LANES:]


def _s5(u, krev, wb, wc, ltre, ltim):
    T, tc = S5_T, S5_TC
    xn = u.reshape(S5_CH, T, SSM_WIDTH)
    grid = (S5_CH // tc, S5_SG)
    natural = pl.BlockSpec((tc, T, SSM_WIDTH), lambda i, s: (i, 0, 0))
    rows = lambda i, s: (s, i, 0)
    per_sg = lambda i, s: (s, 0, 0)
    e = pl.pallas_call(
        _s5_state_kernel, grid=grid,
        in_specs=[natural, pl.BlockSpec((1, T * LANES, S5_NSTATE), per_sg)],
        out_specs=pl.BlockSpec((1, tc, S5_NSTATE), rows),
        out_shape=jax.ShapeDtypeStruct((S5_SG, S5_CH, S5_NSTATE), F32),
        compiler_params=_cparams("arbitrary", "arbitrary"), name="s5_state",
    )(xn, wb)
    sg1 = lambda s: (s, 0, 0)
    xstart = pl.pallas_call(
        _s5_scan_kernel, grid=(S5_SG,),
        in_specs=[pl.BlockSpec((1, S5_CH, S5_NSTATE), sg1),
                  pl.BlockSpec((1, 1, S5_NSTATE // 2), sg1), pl.BlockSpec((1, 1, S5_NSTATE // 2), sg1)],
        out_specs=pl.BlockSpec((1, S5_CH, S5_NSTATE), sg1),
        out_shape=jax.ShapeDtypeStruct((S5_SG, S5_CH, S5_NSTATE), F32),
        compiler_params=_cparams("arbitrary"), name="s5_scan",
    )(e, ltre, ltim)
    y = pl.pallas_call(
        _s5_out_kernel, grid=grid,
        in_specs=[natural, pl.BlockSpec((1, tc, S5_NSTATE), rows),
                  pl.BlockSpec((1, (T + 1) * LANES, LANES), per_sg), pl.BlockSpec((1, S5_NSTATE, T * LANES), per_sg)],
        out_specs=natural,
        out_shape=jax.ShapeDtypeStruct((S5_CH, T, SSM_WIDTH), F32),
        compiler_params=_cparams("arbitrary", "arbitrary", vmem=56 * 1024 * 1024), name="s5_out",
    )(xn, xstart, krev, wc)
    return y.reshape(N_TOK, SSM_WIDTH)


MERGE_TM = 512


def _merge_kernel(ocmp_ref, osel_ref, owin_ref, gn_ref, yssm_ref, u_ref, ga_ref, gs_ref, x_ref, mod_ref,
                  eg_ref, dskip_ref, wglu_ref, bglu_ref, wua_ref, wus_ref, wout_ref, g2_ref,
                  wrhi_ref, wrlo_ref, wsgu_ref, wsd_ref,
                  xpart_ref, h2_ref, logit_ref):
    mod = mod_ref[0]
    gnb = gn_ref[...].astype(BF16)
    o_nsa = (_dot(gnb, eg_ref[0]) * ocmp_ref[...].astype(F32)
             + _dot(gnb, eg_ref[1]) * osel_ref[...].astype(F32)
             + _dot(gnb, eg_ref[2]) * owin_ref[...].astype(F32))
    attn = _dot(o_nsa.astype(BF16), wua_ref[...])
    z = _gelu(yssm_ref[...] + dskip_ref[...] * u_ref[...])
    y_ssm = z * jax.nn.sigmoid(_dot(z.astype(BF16), wglu_ref[...]) + bglu_ref[...])
    ssm = _dot(y_ssm.astype(BF16), wus_ref[...])
    merged = ga_ref[...].astype(F32) * attn + gs_ref[...].astype(F32) * ssm
    x1 = x_ref[...] + mod[2:3] * _dot(merged.astype(BF16), wout_ref[...])

    ms = jnp.mean(x1 * x1, axis=-1, keepdims=True)
    h2 = (x1 * lax.rsqrt(ms + EPS) * g2_ref[...]) * (1.0 + mod[4:5]) + mod[3:4]
    hi = h2.astype(BF16)
    lo = (h2 - hi.astype(F32)).astype(BF16)
    h2_ref[...] = _pack_bf16_pairs(h2)
    logit_ref[...] = _dot_nt(wrhi_ref[...], hi) + _dot_nt(wrhi_ref[...], lo) + _dot_nt(wrlo_ref[...], hi)
    gu = _dot(hi, wsgu_ref[...])
    shared = _dot((_silu(gu[:, :D_EXPERT]) * gu[:, D_EXPERT:]).astype(BF16), wsd_ref[...])
    xpart_ref[...] = x1 + mod[5:6] * shared


def _merge(ocmp, osel, owin, gn, yssm, u, ga, gs, x2, mod, d_skip, w_glu, b_glu, w_up_attn, w_up_ssm, w_out,
           g_norm2, w_router, ws_gate, ws_up, ws_down):
    tm = MERGE_TM
    eg = np.zeros((3, LANES, NSA_WIDTH), np.float32)
    for j in range(3):
        for h in range(N_HEADS):
            eg[j, 3 * h + j, h * HEAD_DIM:(h + 1) * HEAD_DIM] = 1.0
    wr_t = w_router.T
    wr_hi = wr_t.astype(BF16)
    wr_lo = (wr_t - wr_hi.astype(F32)).astype(BF16)
    row = lambda i: (i, 0)
    fix2 = lambda i: (0, 0)
    wspec = lambda a: pl.BlockSpec(a.shape, (lambda i: (0,) * a.ndim))
    weights = [jnp.asarray(eg, BF16), d_skip.reshape(1, -1), w_glu.astype(BF16), b_glu.reshape(1, -1),
               w_up_attn.astype(BF16), w_up_ssm.astype(BF16), w_out.astype(BF16), g_norm2.reshape(1, -1),
               wr_hi, wr_lo, jnp.concatenate([ws_gate, ws_up], axis=1).astype(BF16), ws_down.astype(BF16)]
    acts = [(ocmp, 512), (osel, 512), (owin, 512), (gn, 128), (yssm, 512), (u, 512), (ga, 1024), (gs, 1024),
            (x2, 1024)]
    return pl.pallas_call(
        _merge_kernel,
        grid=(N_TOK // tm,),
        in_specs=[pl.BlockSpec((tm, wd), row) for _, wd in acts]
                 + [pl.BlockSpec((1, 6, D_MODEL), lambda i: (i // (SEQ // tm), 0, 0))]
                 + [wspec(w) for w in weights],
        out_specs=[pl.BlockSpec((tm, D_MODEL), row), pl.BlockSpec((tm, HALF), row),
                   pl.BlockSpec((N_EXPERTS, tm), lambda i: (0, i))],
        out_shape=[jax.ShapeDtypeStruct((N_TOK, D_MODEL), F32), jax.ShapeDtypeStruct((N_TOK, HALF), I32),
                   jax.ShapeDtypeStruct((N_EXPERTS, N_TOK), F32)],
        compiler_params=_cparams("arbitrary", vmem=56 * 1024 * 1024),
        name="merge",
    )(*[a for a, _ in acts], mod, *weights)


ROUTE_TN = 1024


def _route_kernel(logit_ref, bias_ref, eidx_ref, w_ref, count_ref, gscore_ref, masked_ref):
    tn = ROUTE_TN

    @pl.when(pl.program_id(0) == 0)
    def _():
        count_ref[...] = jnp.zeros(count_ref.shape, F32)

    sc = jax.nn.sigmoid(logit_ref[...])
    biased = sc + bias_ref[...]
    gi = lax.broadcasted_iota(I32, (EXPERTS_PER_GROUP, tn), 0).astype(F32)
    for g in range(N_EXPERT_GROUPS):
        blk = biased[g * EXPERTS_PER_GROUP:(g + 1) * EXPERTS_PER_GROUP]
        m1 = jnp.max(blk, axis=0, keepdims=True)
        i1 = jnp.min(jnp.where(blk == m1, gi, float(EXPERTS_PER_GROUP)), axis=0, keepdims=True)
        m2 = jnp.max(jnp.where(gi == i1, -jnp.inf, blk), axis=0, keepdims=True)
        gscore_ref[g:g + 1, :] = m1 + m2
    gs = gscore_ref[...]
    gidx = lax.broadcasted_iota(I32, (N_EXPERT_GROUPS, tn), 0)
    grank = jnp.zeros((N_EXPERT_GROUPS, tn), F32)
    for gp in range(N_EXPERT_GROUPS):
        row = gs[gp:gp + 1, :]
        tie = jnp.where(gidx > gp, 1.0, 0.0)
        grank = grank + jnp.where(row > gs, 1.0, jnp.where(row == gs, tie, 0.0))
    for g in range(N_EXPERT_GROUPS):
        keep = grank[g:g + 1, :] < float(TOPK_GROUPS)
        sl = slice(g * EXPERTS_PER_GROUP, (g + 1) * EXPERTS_PER_GROUP)
        masked_ref[sl, :] = jnp.where(keep, biased[sl], -jnp.inf)
    cur = masked_ref[...]
    eidx = lax.broadcasted_iota(I32, (N_EXPERTS, tn), 0).astype(F32)
    wsum = jnp.zeros((1, tn), F32)
    hits = jnp.zeros((N_EXPERTS, tn), F32)
    for k in range(TOP_K):
        m = jnp.max(cur, axis=0, keepdims=True)
        idx = jnp.min(jnp.where(cur == m, eidx, float(N_EXPERTS)), axis=0, keepdims=True)
        hit = eidx == idx
        wk = jnp.sum(jnp.where(hit, sc, 0.0), axis=0, keepdims=True)
        cur = jnp.where(hit, -jnp.inf, cur)
        hits = hits + jnp.where(hit, 1.0, 0.0)
        eidx_ref[k:k + 1, :] = idx.astype(I32)
        w_ref[k:k + 1, :] = wk
        wsum = wsum + wk
    w_ref[...] = w_ref[...] / wsum * ROUTE_SCALE
    count_ref[...] = count_ref[...] + jnp.sum(hits, axis=1, keepdims=True)


def _route(logits_t, router_bias):
    tn = ROUTE_TN
    return pl.pallas_call(
        _route_kernel,
        grid=(N_TOK // tn,),
        in_specs=[pl.BlockSpec((N_EXPERTS, tn), lambda i: (0, i)), pl.BlockSpec((N_EXPERTS, 1), lambda i: (0, 0))],
        out_specs=[pl.BlockSpec((TOP_K, tn), lambda i: (0, i))] * 2 + [pl.BlockSpec((N_EXPERTS, 1), lambda i: (0, 0))],
        out_shape=[jax.ShapeDtypeStruct((TOP_K, N_TOK), I32), jax.ShapeDtypeStruct((TOP_K, N_TOK), F32),
                   jax.ShapeDtypeStruct((N_EXPERTS, 1), F32)],
        scratch_shapes=[pltpu.VMEM((N_EXPERT_GROUPS, tn), F32), pltpu.VMEM((N_EXPERTS, tn), F32)],
        compiler_params=_cparams("arbitrary"),
        name="route",
    )(logits_t, router_bias.reshape(-1, 1))


N_MOE_BLK = NK // DISPATCH_BLOCK
N_ITEMS = N_MOE_BLK + N_EXPERTS
ASSIGN_BITS = 17


def _dispatch_plan(eidx, counts):
    e_flat = eidx.reshape(-1)
    key = jnp.sort(e_flat * NK + jnp.arange(NK, dtype=I32))
    order = key & (NK - 1)
    counts = counts.reshape(-1).astype(I32)
    start = jnp.cumsum(counts) - counts
    cuts = jnp.sort(jnp.concatenate([jnp.arange(N_MOE_BLK, dtype=I32) * DISPATCH_BLOCK, start]))
    lo = cuts
    hi = jnp.concatenate([cuts[1:], jnp.full((1,), NK, I32)])
    blk = jnp.minimum(lo // DISPATCH_BLOCK, N_MOE_BLK - 1)
    expert = jnp.clip(jnp.sum((start[None, :] <= lo[:, None]).astype(I32), axis=1) - 1, 0, N_EXPERTS - 1)
    one = jnp.ones((1,), I32)
    first = jnp.concatenate([one, (blk[1:] != blk[:-1]).astype(I32)])
    last = jnp.concatenate([(blk[1:] != blk[:-1]).astype(I32), one])
    new_expert = jnp.concatenate([one, (expert[1:] != expert[:-1]).astype(I32)])
    run_id = jnp.cumsum(new_expert) - 1
    n_runs = run_id[-1] + 1
    item = jnp.arange(N_ITEMS, dtype=I32)
    run_first_item = jnp.sort(jnp.where(new_expert == 1, item, N_ITEMS))
    run_expert = expert[jnp.minimum(run_first_item, N_ITEMS - 1)]
    ahead = run_id + (WEIGHT_RING - 1)
    ahead_expert = run_expert[jnp.minimum(ahead, N_ITEMS - 1)]
    ahead_valid = (ahead < n_runs).astype(I32)
    second_expert = run_expert[1:2]
    prologue = jnp.concatenate([second_expert, (n_runs > 1).astype(I32).reshape(1)])
    tok = jnp.right_shift(order, 3)
    home = (order & (TOP_K - 1)) * N_TOK + tok
    return tok, home, (blk, expert, lo - blk * DISPATCH_BLOCK, hi - blk * DISPATCH_BLOCK, first, last, new_expert,
                      run_id % WEIGHT_RING, ahead_expert, ahead_valid, prologue)


SC_CORES = 2
SC_SUBCORES = 16
SC_CHUNK = 128


def _sc_move_rows(table, idx, scatter):
    n = idx.shape[0]
    workers = SC_CORES * SC_SUBCORES
    per_worker = n // workers
    n_chunks = per_worker // SC_CHUNK
    assert per_worker * workers == n and n_chunks * SC_CHUNK == per_worker
    mesh = plsc.VectorSubcoreMesh(core_axis_name="c", subcore_axis_name="s",
                                  num_cores=SC_CORES, num_subcores=SC_SUBCORES)

    def body(table_hbm, idx_hbm, out_hbm, idx_v, rows_v, sem):
        wid = lax.axis_index("s") * SC_CORES + lax.axis_index("c")
        base = wid * per_worker

        @pl.loop(0, n_chunks)
        def _(j):
            off = base + j * SC_CHUNK
            pltpu.sync_copy(idx_hbm.at[pl.ds(off, SC_CHUNK)], idx_v)
            if scatter:
                pltpu.sync_copy(table_hbm.at[pl.ds(off, SC_CHUNK)], rows_v)
                pltpu.async_copy(rows_v, out_hbm.at[idx_v], sem).wait()
            else:
                pltpu.async_copy(table_hbm.at[idx_v], rows_v, sem).wait()
                pltpu.sync_copy(rows_v, out_hbm.at[pl.ds(off, SC_CHUNK)])

    return pl.kernel(
        body,
        out_type=jax.ShapeDtypeStruct((n, table.shape[1]), table.dtype),
        mesh=mesh,
        scratch_types=[pltpu.VMEM((SC_CHUNK,), I32), pltpu.VMEM((SC_CHUNK, table.shape[1]), table.dtype),
                       pltpu.SemaphoreType.DMA],
        name="sc_scatter_rows" if scatter else "sc_gather_rows",
    )(table, idx)


WEIGHT_RING = 3
WEIGHT_CHUNKS = 4


def _expert_weight_copies(w_hbm, wbuf, sem, expert, slot):
    rows = w_hbm.shape[1] // WEIGHT_CHUNKS
    return [pltpu.make_async_copy(w_hbm.at[expert, pl.ds(c * rows, rows)],
                                  wbuf.at[slot, pl.ds(c * rows, rows)], sem.at[slot])
            for c in range(WEIGHT_CHUNKS)]


def _moe_kernel(blk_ref, exp_ref, lo_ref, hi_ref, first_ref, last_ref, newexp_ref,
                slot_ref, ahead_exp_ref, ahead_ok_ref, prologue_ref,
                x_ref, wg_hbm, wu_hbm, wd_hbm, y_ref,
                acc_ref, wgf_ref, wuf_ref, wdf_ref, wgb_ref, wub_ref, wdb_ref, wsem):
    it = pl.program_id(0)
    lo, hi = lo_ref[it], hi_ref[it]
    streams = ((wg_hbm, wgf_ref), (wu_hbm, wuf_ref), (wd_hbm, wdf_ref))

    def request(expert, slot):
        for w_hbm, wbuf in streams:
            for cp in _expert_weight_copies(w_hbm, wbuf, wsem, expert, slot):
                cp.start()

    @pl.when(it == 0)
    def _():
        request(exp_ref[0], 0)

        @pl.when(prologue_ref[1] == 1)
        def _():
            request(prologue_ref[0], 1)

    @pl.when(newexp_ref[it] == 1)
    def _():
        slot = slot_ref[it]
        for w_hbm, wbuf in streams:
            for cp in _expert_weight_copies(w_hbm, wbuf, wsem, 0, slot):
                cp.wait()
        wgb_ref[...] = wgf_ref[slot].astype(BF16)
        wub_ref[...] = wuf_ref[slot].astype(BF16)
        wdb_ref[...] = wdf_ref[slot].astype(BF16)

        @pl.when(ahead_ok_ref[it] == 1)
        def _():
            ahead_slot = slot + (WEIGHT_RING - 1)
            request(ahead_exp_ref[it], jnp.where(ahead_slot >= WEIGHT_RING, ahead_slot - WEIGHT_RING, ahead_slot))

    @pl.when(first_ref[it] == 1)
    def _():
        acc_ref[...] = jnp.zeros(acc_ref.shape, F32)

    def expert_pass(r0, nrows):
        rows = slice(r0, r0 + nrows)
        ridx = r0 + lax.broadcasted_iota(I32, (nrows, HALF), 0)
        mine = (ridx >= lo) & (ridx < hi)
        xlo, xhi = _unpack_bf16_pairs(jnp.where(mine, x_ref[rows, :], 0))
        xlo, xhi = xlo.astype(BF16), xhi.astype(BF16)
        gate = _dot(xlo, wgb_ref[:HALF]) + _dot(xhi, wgb_ref[HALF:])
        up = _dot(xlo, wub_ref[:HALF]) + _dot(xhi, wub_ref[HALF:])
        acc_ref[rows, :] = acc_ref[rows, :] + _dot((_silu(gate) * up).astype(BF16), wdb_ref[...])

    mid = DISPATCH_BLOCK // 2
    pl.when((lo < mid) & (hi > mid))(lambda: expert_pass(0, DISPATCH_BLOCK))
    pl.when((hi > lo) & (hi <= mid))(lambda: expert_pass(0, mid))
    pl.when((hi > lo) & (lo >= mid))(lambda: expert_pass(mid, mid))

    @pl.when(last_ref[it] == 1)
    def _():
        y_ref[...] = _pack_bf16_pairs(acc_ref[...])


def _moe(xs, items, w_gate, w_up, w_down):
    by_blk = lambda it, blk, *_: (blk[it], 0)
    any_space = pl.BlockSpec(memory_space=pl.ANY)
    grid_spec = pltpu.PrefetchScalarGridSpec(
        num_scalar_prefetch=len(items),
        grid=(N_ITEMS,),
        in_specs=[pl.BlockSpec((DISPATCH_BLOCK, HALF), by_blk), any_space, any_space, any_space],
        out_specs=pl.BlockSpec((DISPATCH_BLOCK, HALF), by_blk),
        scratch_shapes=[pltpu.VMEM((DISPATCH_BLOCK, D_MODEL), F32),
                        pltpu.VMEM((WEIGHT_RING, D_MODEL, D_EXPERT), F32),
                        pltpu.VMEM((WEIGHT_RING, D_MODEL, D_EXPERT), F32),
                        pltpu.VMEM((WEIGHT_RING, D_EXPERT, D_MODEL), F32),
                        pltpu.VMEM((D_MODEL, D_EXPERT), BF16), pltpu.VMEM((D_MODEL, D_EXPERT), BF16),
                        pltpu.VMEM((D_EXPERT, D_MODEL), BF16),
                        pltpu.SemaphoreType.DMA((WEIGHT_RING,))],
    )
    return pl.pallas_call(
        _moe_kernel,
        grid_spec=grid_spec,
        out_shape=jax.ShapeDtypeStruct((NK, HALF), I32),
        compiler_params=_cparams("arbitrary"),
        name="moe",
    )(*items, xs, w_gate, w_up, w_down)


COMB_TC = 512


def _combine_kernel(slots_ref, w_ref, xpart_ref, mod_ref, out_ref):
    w = w_ref[...]
    lo = jnp.zeros((w.shape[0], HALF), F32)
    hi = jnp.zeros((w.shape[0], HALF), F32)
    for k in range(TOP_K):
        klo, khi = _unpack_bf16_pairs(slots_ref[k])
        lo = lo + w[:, k:k + 1] * klo
        hi = hi + w[:, k:k + 1] * khi
    gate2 = mod_ref[0][5:6]
    out_ref[:, :HALF] = xpart_ref[:, :HALF] + gate2[:, :HALF] * lo
    out_ref[:, HALF:] = xpart_ref[:, HALF:] + gate2[:, HALF:] * hi


def _combine(xpart, mod, slots, w):
    tc = COMB_TC
    row = lambda i: (i, 0)
    return pl.pallas_call(
        _combine_kernel,
        grid=(N_TOK // tc,),
        in_specs=[pl.BlockSpec((TOP_K, tc, HALF), lambda i: (0, i, 0)),
                  pl.BlockSpec((tc, TOP_K), row),
                  pl.BlockSpec((tc, D_MODEL), row),
                  pl.BlockSpec((1, 6, D_MODEL), lambda i: (i // (SEQ // tc), 0, 0))],
        out_specs=pl.BlockSpec((tc, D_MODEL), row),
        out_shape=jax.ShapeDtypeStruct((N_TOK, D_MODEL), F32),
        compiler_params=_cparams("arbitrary"),
        name="combine",
    )(slots.reshape(TOP_K, N_TOK, HALF), w, xpart, mod)


def _layer(x, c, w_ada, b_ada, g_norm1, g_norm2, w_in, q_gain, kc_gain, ks_gain, kw_gain,
           pe_k, pe_v, w_cmp_k1, w_cmp_k2, w_cmp_v1, w_cmp_v2,
           a_re, a_im, log_dt, b_re, b_im, c_re, c_im, d_skip, w_glu, b_glu,
           w_up_attn, w_up_ssm, w_out, w_router, router_bias,
           w_gate, w_up, w_down, ws_gate, ws_up, ws_down):
    x2 = x.reshape(N_TOK, D_MODEL)
    mod = _ada(c, w_ada, b_ada)
    q, kc_raw, vc_raw, ks, kw, vst, vwt, gn, u, ga, gs = _proj(x2, mod, g_norm1, w_in, q_gain, ks_gain, kw_gain)
    kcn = _compress(kc_raw, pe_k, w_cmp_k1, w_cmp_k2, kc_gain, True)
    vcn = _compress(vc_raw, pe_v, w_cmp_v1, w_cmp_v2, kc_gain, False)
    ocmp, selb = _cmp_attn(q, kcn, vcn)
    osel, owin = _selwin(q, ks, kw, vst, vwt, selb)
    yssm = _s5(u, *_s5_params(a_re, a_im, log_dt, b_re, b_im, c_re, c_im))
    xpart, h2, logits_t = _merge(ocmp, osel, owin, gn, yssm, u, ga, gs, x2, mod, d_skip, w_glu, b_glu,
                                  w_up_attn, w_up_ssm, w_out, g_norm2, w_router, ws_gate, ws_up, ws_down)
    eidx_t, w_t, counts = _route(logits_t, router_bias)
    tok, home, items = _dispatch_plan(eidx_t.T, counts)
    y = _moe(_sc_move_rows(h2, tok, scatter=False), items, w_gate, w_up, w_down)
    slots = _sc_move_rows(y, home, scatter=True)
    return _combine(xpart, mod, slots, w_t.T).reshape(BATCH, SEQ, D_MODEL)


def kernel(x, c, w_ada, b_ada, g_norm1, g_norm2, w_in, q_gain, kc_gain, ks_gain, kw_gain, pe_k, pe_v, w_cmp_k1,
           w_cmp_k2, w_cmp_v1, w_cmp_v2, a_re, a_im, log_dt, b_re, b_im, c_re, c_im, d_skip, w_glu, b_glu,
           w_up_attn, w_up_ssm, w_out, w_router, router_bias, w_gate, w_up, w_down, ws_gate, ws_up, ws_down):
    params = (w_ada, b_ada, g_norm1, g_norm2, w_in, q_gain, kc_gain, ks_gain, kw_gain, pe_k, pe_v, w_cmp_k1,
              w_cmp_k2, w_cmp_v1, w_cmp_v2, a_re, a_im, log_dt, b_re, b_im, c_re, c_im, d_skip, w_glu, b_glu,
              w_up_attn, w_up_ssm, w_out, w_router, router_bias, w_gate, w_up, w_down, ws_gate, ws_up, ws_down)
    depth = w_ada.shape[0]
    for layer in range(depth):
        x = _layer(x, c, *[p[layer] for p in params])
    return x
```

```python
import functools
import math

import jax
import jax.numpy as jnp
import numpy as np
from jax import lax
from jax.experimental import pallas as pl
from jax.experimental.pallas import tpu as pltpu
from jax.experimental.pallas import tpu_sc as plsc

F32 = jnp.float32
BF16 = jnp.bfloat16
I32 = jnp.int32
HIGHEST = lax.Precision.HIGHEST

D_MODEL = 1024
BATCH = 4
SEQ = 4096
N_TOK = BATCH * SEQ
N_HEADS = 8
HEAD_DIM = 64
N_KV = 2
CMP_BLOCK = 32
CMP_STRIDE = 16
CMP_HIDDEN = 256
N_CMP = 256
SEL_BLOCK = 64
N_SEL_BLOCKS = SEQ // SEL_BLOCK
N_SELECT = 16
WINDOW = 512
ATTN_SCALE = HEAD_DIM ** -0.5
LOG2E = 1.4426950408889634
NSA_WIDTH = N_HEADS * HEAD_DIM
SSM_WIDTH = 512
GROUP = 16
N_GROUPS = SSM_WIDTH // GROUP
STATE = 64
N_EXPERTS = 256
TOP_K = 8
D_EXPERT = 256
N_EXPERT_GROUPS = 8
EXPERTS_PER_GROUP = N_EXPERTS // N_EXPERT_GROUPS
TOPK_GROUPS = 4
ROUTE_SCALE = 2.5
DISPATCH_BLOCK = 512
EPS = 1e-6
NEG = -1e30

LANES = 128
S5_T = 16
S5_SG = 4
S5_GL = N_GROUPS // S5_SG
S5_CH = N_TOK // S5_T
S5_CH_PER_BATCH = SEQ // S5_T
S5_NSTATE = S5_GL * STATE * 2

NK = N_TOK * TOP_K
HALF = D_MODEL // 2

VMEM_LIMIT = 48 * 1024 * 1024


def _cparams(*sem, vmem=VMEM_LIMIT):
    return pltpu.CompilerParams(dimension_semantics=tuple(sem), vmem_limit_bytes=vmem)


def _dot(a, b):
    return jnp.dot(a, b, preferred_element_type=F32)


def _dot_nt(a, b):
    return lax.dot_general(a, b, (((1,), (1,)), ((), ())), preferred_element_type=F32)


def _split_dot(v, w):
    hi = v.astype(BF16)
    lo = (v - hi.astype(F32)).astype(BF16)
    return _dot(hi, w) + _dot(lo, w)


def _seg_rms(v, bd, gain):
    ss = _split_dot(v * v, bd)
    return v * lax.rsqrt(ss * (1.0 / HEAD_DIM) + EPS) * gain


def _gelu(x):
    return 0.5 * x * (1.0 + jnp.tanh(0.7978845608028654 * (x + 0.044715 * (x * x * x))))


def _silu(x):
    return x * jax.nn.sigmoid(x)


def _pack_bf16_pairs(v):
    h = v.shape[1] // 2
    return pltpu.pack_elementwise([v[:, :h], v[:, h:]], packed_dtype=BF16)


def _unpack_bf16_pairs(word):
    return (pltpu.unpack_elementwise(word, index=0, packed_dtype=BF16, unpacked_dtype=F32),
            pltpu.unpack_elementwise(word, index=1, packed_dtype=BF16, unpacked_dtype=F32))


def _ada_kernel(c_ref, w_ref, b_ref, o_ref):
    c = c_ref[...]
    o_ref[...] = jnp.dot(_silu(c), w_ref[...], preferred_element_type=F32, precision=HIGHEST) + b_ref[...]


def _ada(c, w_ada, b_ada):
    cp = jnp.pad(c, ((0, 8 - BATCH), (0, 0)))
    tn = 1536
    out = pl.pallas_call(
        _ada_kernel,
        grid=(6 * D_MODEL // tn,),
        in_specs=[pl.BlockSpec((8, D_MODEL), lambda j: (0, 0)),
                  pl.BlockSpec((D_MODEL, tn), lambda j: (0, j)),
                  pl.BlockSpec((1, tn), lambda j: (0, j))],
        out_specs=pl.BlockSpec((8, tn), lambda j: (0, j)),
        out_shape=jax.ShapeDtypeStruct((8, 6 * D_MODEL), F32),
        compiler_params=_cparams("arbitrary"),
        name="ada",
    )(cp, w_ada, b_ada.reshape(1, -1))
    return out.reshape(8, 6, D_MODEL)


_C_Q = 0
_C_KC = 512
_C_VC = 640
_C_KS = 768
_C_KW = 1024
_C_GN = 1280
_C_U = 1408
_C_GA = 1920
_C_GS = 2944
_C_END = 3968
PROJ_TM = 512


def _proj_kernel(x_ref, mod_ref, g1_ref, w_ref, wvt_ref, qg_ref, ksg_ref, kwg_ref, bd512_ref, bd256_ref,
                 q_ref, kc_ref, vc_ref, ks_ref, kw_ref, vst_ref, vwt_ref, gn_ref, u_ref, ga_ref, gs_ref):
    x = x_ref[...]
    ms = jnp.mean(x * x, axis=-1, keepdims=True)
    mod = mod_ref[0]
    h = (x * lax.rsqrt(ms + EPS) * g1_ref[...]) * (1.0 + mod[1:2]) + mod[0:1]
    hb = h.astype(BF16)

    def p(lo, hi):
        return _dot(hb, w_ref[:, lo:hi])

    q_ref[...] = _seg_rms(p(_C_Q, _C_KC), bd512_ref[...], qg_ref[...] * (ATTN_SCALE * LOG2E)).astype(BF16)
    kvc = p(_C_KC, _C_KS)
    kc_ref[...] = kvc[:, :LANES]
    vc_ref[...] = kvc[:, LANES:]
    ks_ref[...] = _seg_rms(p(_C_KS, _C_KW), bd256_ref[...], ksg_ref[...]).astype(BF16)
    kw_ref[...] = _seg_rms(p(_C_KW, _C_GN), bd256_ref[...], kwg_ref[...]).astype(BF16)
    vt = _dot_nt(wvt_ref[...], hb)
    vst_ref[...] = vt[:LANES].astype(BF16)
    vwt_ref[...] = vt[LANES:].astype(BF16)
    gnu = p(_C_GN, _C_GA)
    gn_ref[...] = jax.nn.sigmoid(gnu[:, :LANES])
    u_ref[...] = gnu[:, LANES:]
    ga_ref[...] = jax.nn.sigmoid(p(_C_GA, _C_GS)).astype(BF16)
    gs_ref[...] = jax.nn.sigmoid(p(_C_GS, _C_END)).astype(BF16)


def _dup_cols(w):
    return jnp.concatenate([w[:, :64], w[:, :64], w[:, 64:], w[:, 64:]], axis=1)


def _block_ones(n):
    return jnp.kron(jnp.eye(n // HEAD_DIM, dtype=F32), jnp.ones((HEAD_DIM, HEAD_DIM), F32)).astype(BF16)


def _proj(x2, mod, g_norm1, w_in, q_gain, ks_gain, kw_gain):
    o = np.cumsum((0, 512, 128, 128, 128, 128, 128, 128, 24, 512, 1024, 1024))
    parts = [w_in[:, o[i]:o[i + 1]] for i in range(11)]
    wq, wkc, wvc, wks, wvs, wkw, wvw, wgn, wu, wga, wgs = parts
    w = jnp.concatenate([wq, wkc, wvc, _dup_cols(wks), _dup_cols(wkw),
                         jnp.pad(wgn, ((0, 0), (0, LANES - 24))), wu, wga, wgs], axis=1).astype(BF16)
    wvt = jnp.concatenate([wvs, wvw], axis=1).T.astype(BF16)
    tm = PROJ_TM
    row = lambda i: (i, 0)
    col = lambda i: (0, i)
    fix = lambda i: (0, 0)
    outs = [(512, BF16, row), (128, F32, row), (128, F32, row), (256, BF16, row), (256, BF16, row),
            (LANES, BF16, col), (LANES, BF16, col),
            (128, F32, row), (512, F32, row), (1024, BF16, row), (1024, BF16, row)]
    ospec = lambda wd, m: pl.BlockSpec((tm, wd), m) if m is row else pl.BlockSpec((wd, tm), m)
    oshape = lambda wd, dt, m: jax.ShapeDtypeStruct((N_TOK, wd) if m is row else (wd, N_TOK), dt)
    return pl.pallas_call(
        _proj_kernel,
        grid=(N_TOK // tm,),
        in_specs=[pl.BlockSpec((tm, D_MODEL), row),
                  pl.BlockSpec((1, 6, D_MODEL), lambda i: (i // (SEQ // tm), 0, 0)),
                  pl.BlockSpec((1, D_MODEL), fix),
                  pl.BlockSpec((D_MODEL, _C_END), fix),
                  pl.BlockSpec((2 * LANES, D_MODEL), fix),
                  pl.BlockSpec((1, 512), fix), pl.BlockSpec((1, 256), fix), pl.BlockSpec((1, 256), fix),
                  pl.BlockSpec((512, 512), fix), pl.BlockSpec((256, 256), fix)],
        out_specs=[ospec(wd, m) for wd, _, m in outs],
        out_shape=[oshape(wd, dt, m) for wd, dt, m in outs],
        compiler_params=_cparams("arbitrary"),
        name="proj",
    )(x2, mod, g_norm1.reshape(1, -1), w, wvt,
      jnp.tile(q_gain, N_HEADS).reshape(1, -1), jnp.tile(ks_gain, 4).reshape(1, -1),
      jnp.tile(kw_gain, 4).reshape(1, -1), _block_ones(512), _block_ones(256))


def _compress_kernel(r_ref, pe_ref, w1_ref, w2_ref, bd_ref, gain_ref, o_ref, *, do_norm):
    r = jnp.concatenate([r_ref[0, :, l, :] for l in range(CMP_STRIDE)], axis=1)
    p0 = _dot((r + pe_ref[0]).astype(BF16), w1_ref[0])
    p1 = _dot((r + pe_ref[1]).astype(BF16), w1_ref[1])
    hid = p0 + pltpu.roll(p1, N_CMP - 1, 0)
    c = _dot(_gelu(hid).astype(BF16), w2_ref[...])
    if do_norm:
        c = _seg_rms(c, bd_ref[...], gain_ref[...])
    o_ref[0] = c.astype(BF16)


def _compress(raw, pe, w1, w2, gain, do_norm):
    r = raw.reshape(BATCH, SEQ // CMP_STRIDE, CMP_STRIDE, LANES)
    eye = jnp.eye(N_KV, dtype=F32)
    w1r = w1.reshape(2, CMP_STRIDE, HEAD_DIM, CMP_HIDDEN)
    w1big = jnp.einsum('hldc,gk->hlgdkc', w1r, eye).reshape(2, CMP_STRIDE * LANES, N_KV * CMP_HIDDEN).astype(BF16)
    w2big = jnp.einsum('cd,gk->gckd', w2, eye)
    w2big = jnp.concatenate([w2big, w2big], axis=-1).reshape(N_KV * CMP_HIDDEN, 4 * HEAD_DIM).astype(BF16)
    pe_big = jnp.broadcast_to(pe.reshape(2, CMP_STRIDE, 1, HEAD_DIM), (2, CMP_STRIDE, N_KV, HEAD_DIM))
    pe_big = pe_big.reshape(2, 1, CMP_STRIDE * LANES)
    fix2 = lambda b: (0, 0)
    fix3 = lambda b: (0, 0, 0)
    return pl.pallas_call(
        functools.partial(_compress_kernel, do_norm=do_norm),
        grid=(BATCH,),
        in_specs=[pl.BlockSpec((1, N_CMP, CMP_STRIDE, LANES), lambda b: (b, 0, 0, 0)),
                  pl.BlockSpec((2, 1, CMP_STRIDE * LANES), fix3),
                  pl.BlockSpec((2, CMP_STRIDE * LANES, N_KV * CMP_HIDDEN), fix3),
                  pl.BlockSpec((N_KV * CMP_HIDDEN, 256), fix2),
                  pl.BlockSpec((256, 256), fix2), pl.BlockSpec((1, 256), fix2)],
        out_specs=pl.BlockSpec((1, N_CMP, 256), lambda b: (b, 0, 0)),
        out_shape=jax.ShapeDtypeStruct((BATCH, N_CMP, 256), BF16),
        compiler_params=_cparams("arbitrary"),
        name="compress_k" if do_norm else "compress_v",
    )(r, pe_big, w1big, w2big, _block_ones(256), jnp.tile(gain, 4).reshape(1, -1))


ATT_TQ = 1024
RANK_CHUNK = 16


def _head_variants(qb):
    lane = lax.broadcasted_iota(I32, qb.shape, 1)
    z = jnp.zeros_like(qb)
    return jnp.where(lane < HEAD_DIM, qb, z), jnp.where(lane < HEAD_DIM, z, qb)


def _cmp_kernel(q_ref, kc_ref, vc_ref, ov_ref, o_ref, sel_ref, vrank_ref):
    tq = ATT_TQ
    qi = pl.program_id(1)
    tpos = qi * tq + lax.broadcasted_iota(I32, (tq, N_CMP), 0)
    nidx = lax.broadcasted_iota(I32, (tq, N_CMP), 1)
    mask = (CMP_STRIDE * nidx + (CMP_BLOCK - 1)) <= tpos
    lane_lo = lax.broadcasted_iota(I32, (tq, LANES), 1) < HEAD_DIM
    for g in range(N_KV):
        kd = kc_ref[0, :, g * LANES:(g + 1) * LANES]
        vd = vc_ref[0, :, g * LANES:(g + 1) * LANES]
        psum = jnp.zeros((tq, N_CMP), F32)
        for jb in range(2):
            blk = 2 * g + jb
            pv = []
            for qv in _head_variants(q_ref[:, blk * LANES:(blk + 1) * LANES]):
                s = jnp.where(mask, _dot_nt(qv, kd), NEG)
                m = jnp.max(s, axis=-1, keepdims=True)
                e = jnp.where(mask, jnp.exp2(s - m), 0.0)
                l = jnp.sum(e, axis=-1, keepdims=True)
                p = e / jnp.where(l > 0.0, l, 1.0)
                psum = psum + p
                pv.append(_dot(p.astype(BF16), vd))
            o_ref[:, blk * LANES:(blk + 1) * LANES] = jnp.where(lane_lo, pv[0], pv[1]).astype(BF16)
        imp = _split_dot(psum, ov_ref[...])
        imp_t = imp.T[:N_SEL_BLOCKS]
        j = lax.broadcasted_iota(I32, (N_SEL_BLOCKS, tq), 0)
        cur = jnp.right_shift(qi * tq + lax.broadcasted_iota(I32, (N_SEL_BLOCKS, tq), 1), 6)
        forced = (j == 0) | (j == cur) | (j == cur - 1)
        v = jnp.where(forced, jnp.inf, jnp.where(j <= cur, imp_t, -jnp.inf))
        vrank_ref[...] = jnp.zeros((N_SEL_BLOCKS, tq), F32)
        n_live = (qi + 1) * (tq // SEL_BLOCK)
        for c0 in range(0, N_SEL_BLOCKS, RANK_CHUNK):
            @pl.when(c0 < n_live)
            def _():
                rank = vrank_ref[...]
                for jp in range(c0, c0 + RANK_CHUNK):
                    row = v[jp:jp + 1, :]
                    tie = jnp.where(j > jp, 1.0, 0.0)
                    rank = rank + jnp.where(row > v, 1.0, jnp.where(row == v, tie, 0.0))
                vrank_ref[...] = rank
        rank = vrank_ref[...]
        sel_ref[g * N_SEL_BLOCKS:(g + 1) * N_SEL_BLOCKS, :] = jnp.where(rank < float(N_SELECT), 0.0, NEG)


def _cmp_attn(q, kcn, vcn):
    nc = np.arange(N_CMP)
    sb = np.arange(LANES)
    ov = ((CMP_STRIDE * nc[:, None] < SEL_BLOCK * sb[None, :] + SEL_BLOCK)
          & (CMP_STRIDE * nc[:, None] + CMP_BLOCK > SEL_BLOCK * sb[None, :])
          & (nc[:, None] < N_CMP - 1) & (sb[None, :] < N_SEL_BLOCKS))
    ov = jnp.asarray(ov, BF16)
    tq = ATT_TQ
    nq = SEQ // tq
    row = lambda b, i: (b * nq + i, 0)
    return pl.pallas_call(
        _cmp_kernel,
        grid=(BATCH, nq),
        in_specs=[pl.BlockSpec((tq, NSA_WIDTH), row),
                  pl.BlockSpec((1, N_CMP, 256), lambda b, i: (b, 0, 0)),
                  pl.BlockSpec((1, N_CMP, 256), lambda b, i: (b, 0, 0)),
                  pl.BlockSpec((N_CMP, LANES), lambda b, i: (0, 0))],
        out_specs=[pl.BlockSpec((tq, NSA_WIDTH), row),
                   pl.BlockSpec((N_KV * N_SEL_BLOCKS, tq), lambda b, i: (0, b * nq + i))],
        out_shape=[jax.ShapeDtypeStruct((N_TOK, NSA_WIDTH), BF16),
                   jax.ShapeDtypeStruct((N_KV * N_SEL_BLOCKS, N_TOK), F32)],
        scratch_shapes=[pltpu.VMEM((N_SEL_BLOCKS, tq), F32)],
        compiler_params=_cparams("arbitrary", "arbitrary"),
        name="cmp_attn",
    )(q, kcn, vcn, ov)


ATT_TK = 256


M_INIT = -1e29


SUM_ROWS = 16
SW_TQ = 512
SEL_TK = 512


def _selwin_kernel(q_ref, ks_ref, kw_ref, vst_ref, vwt_ref, selb_ref, osel_ref, owin_ref, m_ref, acc_ref):
    tq = tk = SW_TQ
    qi = pl.program_id(1)
    krow = lax.broadcasted_iota(I32, (tk, tq), 0)
    qcol = lax.broadcasted_iota(I32, (tk, tq), 1)
    causal_bias = jnp.where(krow <= qcol, 0.0, NEG)
    far_bias = jnp.where(qcol < krow, 0.0, NEG)

    def reset():
        m_ref[...] = jnp.full(m_ref.shape, M_INIT, F32)
        acc_ref[...] = jnp.zeros(acc_ref.shape, F32)

    def update(g, k_ref, vt_ref, kt, bias, nk=tk):
        k0 = pl.multiple_of(kt * nk, nk)
        kd = k_ref[0, pl.ds(k0, nk), g * LANES:(g + 1) * LANES]
        vt = vt_ref[g * HEAD_DIM:(g + 1) * HEAD_DIM, pl.ds(k0, nk)]
        s = _dot_nt(kd, qvars[g])
        if bias is not None:
            s = s + jnp.concatenate([bias] * 4, axis=1)
        m_old = m_ref[g]
        m_new = jnp.maximum(m_old, jnp.max(s, axis=0, keepdims=True))
        alpha = jnp.exp2(m_old - m_new)
        p = jnp.exp2(s - m_new)
        m_ref[g] = m_new
        vte = jnp.concatenate([vt, jnp.ones((SUM_ROWS, nk), BF16)], axis=0)
        acc_ref[g] = alpha * acc_ref[g] + _dot(vte, p.astype(BF16))

    def finish(out_ref, g):
        o = acc_ref[g, :HEAD_DIM, :] / acc_ref[g, HEAD_DIM:HEAD_DIM + 1, :]
        for jb in range(2):
            blk = 2 * g + jb
            pair = jnp.concatenate([o[:, 2 * jb * tq:(2 * jb + 1) * tq], o[:, (2 * jb + 1) * tq:(2 * jb + 2) * tq]],
                                   axis=0)
            out_ref[:, blk * LANES:(blk + 1) * LANES] = pair.T.astype(BF16)

    def sel_bias(g, kt):
        rows = [jnp.broadcast_to(selb_ref[pl.ds(g * N_SEL_BLOCKS + kt * (SEL_TK // SEL_BLOCK) + r, 1), :],
                                 (SEL_BLOCK, tq)) for r in range(SEL_TK // SEL_BLOCK)]
        return jnp.concatenate(rows, axis=0)

    qvars = []
    for g in range(N_KV):
        heads = []
        for jb in range(2):
            heads.extend(_head_variants(q_ref[:, (2 * g + jb) * LANES:(2 * g + jb + 1) * LANES]))
        qvars.append(jnp.concatenate(heads, axis=0))
    groups = range(N_KV)

    reset()
    last_tile = qi // (SEL_TK // tq)

    def sel_step(kt, carry):
        for g in groups:
            update(g, ks_ref, vst_ref, kt, sel_bias(g, kt), SEL_TK)
        return carry

    lax.fori_loop(0, last_tile, sel_step, 0)
    q_first = qi * tq - last_tile * SEL_TK
    visible = (lax.broadcasted_iota(I32, (SEL_TK, tq), 0)
               <= lax.broadcasted_iota(I32, (SEL_TK, tq), 1) + q_first)
    diag_bias = jnp.where(visible, 0.0, NEG)
    for g in groups:
        update(g, ks_ref, vst_ref, last_tile, sel_bias(g, last_tile) + diag_bias, SEL_TK)
    for g in groups:
        finish(osel_ref, g)

    reset()
    back = WINDOW // tk

    @pl.when(qi >= back)
    def _():
        for g in groups:
            update(g, kw_ref, vwt_ref, qi - back, far_bias)

    for d in range(back - 1, 0, -1):
        @pl.when(qi >= d)
        def _():
            for g in groups:
                update(g, kw_ref, vwt_ref, qi - d, None)

    for g in groups:
        update(g, kw_ref, vwt_ref, qi, causal_bias)
    for g in groups:
        finish(owin_ref, g)


def _selwin(q, ks, kw, vst, vwt, selb):
    tq = SW_TQ
    nq = SEQ // tq
    assert WINDOW % SW_TQ == 0 and SEL_TK % SW_TQ == 0
    row = lambda b, i: (b * nq + i, 0)
    keys = pl.BlockSpec((1, SEQ, 256), lambda b, i: (b, 0, 0))
    vals = pl.BlockSpec((LANES, SEQ), lambda b, i: (0, b))
    r3 = lambda a: a.reshape(BATCH, SEQ, 256)
    return pl.pallas_call(
        _selwin_kernel,
        grid=(BATCH, nq),
        in_specs=[pl.BlockSpec((tq, NSA_WIDTH), row), keys, keys, vals, vals,
                  pl.BlockSpec((N_KV * N_SEL_BLOCKS, tq), lambda b, i: (0, b * nq + i))],
        out_specs=[pl.BlockSpec((tq, NSA_WIDTH), row)] * 2,
        out_shape=[jax.ShapeDtypeStruct((N_TOK, NSA_WIDTH), BF16)] * 2,
        scratch_shapes=[pltpu.VMEM((N_KV, 1, 4 * tq), F32),
                        pltpu.VMEM((N_KV, HEAD_DIM + SUM_ROWS, 4 * tq), F32)],
        compiler_params=_cparams("arbitrary", "arbitrary"),
        name="selwin",
    )(q, r3(ks), r3(kw), vst, vwt, selb)


def _s5_param_kernel(are_ref, aim_ref, ldt_ref, cre_ref, cim_ref, bre_ref, bim_ref,
                     clre_ref, clim_ref, wbre_ref, wbim_ref, bbre_ref, bbim_ref, ltre_ref, ltim_ref):
    are, aim = are_ref[...], aim_ref[...]
    dt = jnp.exp(ldt_ref[...])
    cre, cim = cre_ref[...], cim_ref[...]

    def lam_pow(tau):
        mag = jnp.exp(are * dt * float(tau))
        ang = aim * dt * float(tau)
        return mag * jnp.cos(ang), mag * jnp.sin(ang)

    lre, lim = lam_pow(1)
    den = are * are + aim * aim
    qre = ((lre - 1.0) * are + lim * aim) / den
    qim = (lim * are - (lre - 1.0) * aim) / den
    bre, bim = bre_ref[...], bim_ref[...]
    bbre = qre * bre - qim * bim
    bbim = qre * bim + qim * bre
    bbre_ref[...] = bbre
    bbim_ref[...] = bbim
    for tau in range(S5_T + 1):
        pr, pi = lam_pow(tau)
        clre_ref[tau] = cre * pr - cim * pi
        clim_ref[tau] = cre * pi + cim * pr
        if tau < S5_T:
            k = S5_T - 1 - tau
            wbre_ref[k] = pr * bbre - pi * bbim
            wbim_ref[k] = pr * bbim + pi * bbre
        else:
            ltre_ref[...] = pr
            ltim_ref[...] = pi


def _s5_kmat_kernel(l_ref, r_ref, o_ref):
    o_ref[0] = jnp.dot(l_ref[0], r_ref[0], preferred_element_type=F32, precision=HIGHEST)


def _s5_params(a_re, a_im, log_dt, b_re, b_im, c_re, c_im):
    T = S5_T
    pn = GROUP * STATE
    tile_p = lambda a: jnp.tile(a, (1, GROUP))
    args = (tile_p(a_re), tile_p(a_im), jnp.broadcast_to(log_dt[:, None], (N_GROUPS, pn)),
            c_re.reshape(N_GROUPS, pn), c_im.reshape(N_GROUPS, pn),
            jnp.swapaxes(b_re, 1, 2).reshape(N_GROUPS, pn), jnp.swapaxes(b_im, 1, 2).reshape(N_GROUPS, pn))
    full2 = pl.BlockSpec((N_GROUPS, pn), lambda: (0, 0))
    clre, clim, wbre, wbim, bbre, bbim, ltre, ltim = pl.pallas_call(
        _s5_param_kernel,
        in_specs=[full2] * 7,
        out_specs=[pl.BlockSpec((T + 1, N_GROUPS, pn), lambda: (0, 0, 0))] * 2
                  + [pl.BlockSpec((T, N_GROUPS, pn), lambda: (0, 0, 0))] * 2 + [full2] * 4,
        out_shape=[jax.ShapeDtypeStruct((T + 1, N_GROUPS, pn), F32)] * 2
                  + [jax.ShapeDtypeStruct((T, N_GROUPS, pn), F32)] * 2
                  + [jax.ShapeDtypeStruct((N_GROUPS, pn), F32)] * 4,
        name="s5_params",
    )(*args)

    r5 = lambda a, t: a[:t].reshape(t, N_GROUPS, GROUP, STATE)
    lhs = jnp.concatenate([r5(clre, T), -r5(clim, T)], axis=-1)
    lhs = jnp.transpose(lhs, (1, 0, 2, 3)).reshape(N_GROUPS, T * GROUP, 2 * STATE)
    bb = lambda a: jnp.swapaxes(a.reshape(N_GROUPS, GROUP, STATE), 1, 2)
    rhs = jnp.concatenate([bb(bbre), bb(bbim)], axis=1)
    kmat = pl.pallas_call(
        _s5_kmat_kernel,
        grid=(N_GROUPS,),
        in_specs=[pl.BlockSpec((1, T * GROUP, 2 * STATE), lambda g: (g, 0, 0)),
                  pl.BlockSpec((1, 2 * STATE, GROUP), lambda g: (g, 0, 0))],
        out_specs=pl.BlockSpec((1, T * GROUP, GROUP), lambda g: (g, 0, 0)),
        out_shape=jax.ShapeDtypeStruct((N_GROUPS, T * GROUP, GROUP), F32),
        compiler_params=_cparams("arbitrary"),
        name="s5_kmat",
    )(lhs, rhs)

    eye = jnp.eye(S5_GL, dtype=F32)
    kt = kmat.reshape(S5_SG, S5_GL, T, GROUP, GROUP)
    kbd = jnp.einsum('sgtpq,gh->stgqhp', kt, eye).reshape(S5_SG, T, LANES, LANES)
    krev = kbd[:, ::-1].reshape(S5_SG, T * LANES, LANES).astype(BF16)
    krev = jnp.pad(krev, ((0, 0), (0, LANES), (0, 0)))
    r6 = lambda a: a.reshape(T, S5_SG, S5_GL, GROUP, STATE)
    wb = jnp.stack([r6(wbre), r6(wbim)], axis=-2)
    wb = jnp.einsum('ksgpin,gh->skgpihn', wb, eye).reshape(S5_SG, T * LANES, S5_NSTATE).astype(BF16)
    wc = jnp.stack([r6(clre[1:]), -r6(clim[1:])], axis=-2)
    wc = jnp.einsum('tsgpin,gh->signthp', wc, eye).reshape(S5_SG, S5_NSTATE, T * LANES).astype(BF16)
    lt = lambda a: a.reshape(N_GROUPS, GROUP, STATE)[:, 0].reshape(S5_SG, 1, S5_GL * STATE)
    return krev, wb, wc, lt(ltre), lt(ltim)


S5_TC = 256


def _s5_lane_block(sg):
    return pl.ds(pl.multiple_of(sg * LANES, LANES), LANES)


def _s5_chunk_inputs(x_ref, sg):
    return jnp.concatenate([x_ref[:, t, _s5_lane_block(sg)] for t in range(S5_T)], axis=1).astype(BF16)


def _s5_state_kernel(x_ref, wb_ref, e_ref):
    e_ref[0] = _dot(_s5_chunk_inputs(x_ref, pl.program_id(1)), wb_ref[0])


def _s5_scan_kernel(e_ref, ltre_ref, ltim_ref, xs_ref):
    lr, li = ltre_ref[0], ltim_ref[0]
    half = S5_NSTATE // 2

    def step(c, carry):
        new = []
        for b in range(BATCH):
            xr, xi = carry[b]
            row = b * S5_CH_PER_BATCH + c
            xs_ref[0, pl.ds(row, 1), :half] = xr
            xs_ref[0, pl.ds(row, 1), half:] = xi
            e = e_ref[0, pl.ds(row, 1), :]
            new.append((lr * xr - li * xi + e[:, :half], lr * xi + li * xr + e[:, half:]))
        return tuple(new)

    zero = jnp.zeros((1, half), F32)
    lax.fori_loop(0, S5_CH_PER_BATCH, step, tuple((zero, zero) for _ in range(BATCH)))


def _s5_out_kernel(x_ref, xs_ref, krev_ref, wc_ref, y_ref):
    sg = pl.program_id(1)
    x = _s5_chunk_inputs(x_ref, sg)
    xsb = xs_ref[0].astype(BF16)
    for t in range(0, S5_T, 2):
        n_in = (t + 2) * LANES
        first = (S5_T - 1 - t) * LANES
        taps = jnp.concatenate([krev_ref[0, first:first + n_in, :], krev_ref[0, first - LANES:first - LANES + n_in, :]],
                               axis=1)
        pair = _dot(x[:, :n_in], taps) + _dot(xsb, wc_ref[0, :, t * LANES:(t + 2) * LANES])
        y_ref[:, t, _s5_lane_block(sg)] = pair[:, :LANES]
        y_ref[:, t + 1, _s5_lane_block(sg)] = pair[:, LANES:]


def _s5(u, krev, wb, wc, ltre, ltim):
    T, tc = S5_T, S5_TC
    xn = u.reshape(S5_CH, T, SSM_WIDTH)
    grid = (S5_CH // tc, S5_SG)
    natural = pl.BlockSpec((tc, T, SSM_WIDTH), lambda i, s: (i, 0, 0))
    rows = lambda i, s: (s, i, 0)
    per_sg = lambda i, s: (s, 0, 0)
    e = pl.pallas_call(
        _s5_state_kernel, grid=grid,
        in_specs=[natural, pl.BlockSpec((1, T * LANES, S5_NSTATE), per_sg)],
        out_specs=pl.BlockSpec((1, tc, S5_NSTATE), rows),
        out_shape=jax.ShapeDtypeStruct((S5_SG, S5_CH, S5_NSTATE), F32),
        compiler_params=_cparams("arbitrary", "arbitrary"), name="s5_state",
    )(xn, wb)
    sg1 = lambda s: (s, 0, 0)
    xstart = pl.pallas_call(
        _s5_scan_kernel, grid=(S5_SG,),
        in_specs=[pl.BlockSpec((1, S5_CH, S5_NSTATE), sg1),
                  pl.BlockSpec((1, 1, S5_NSTATE // 2), sg1), pl.BlockSpec((1, 1, S5_NSTATE // 2), sg1)],
        out_specs=pl.BlockSpec((1, S5_CH, S5_NSTATE), sg1),
        out_shape=jax.ShapeDtypeStruct((S5_SG, S5_CH, S5_NSTATE), F32),
        compiler_params=_cparams("arbitrary"), name="s5_scan",
    )(e, ltre, ltim)
    y = pl.pallas_call(
        _s5_out_kernel, grid=grid,
        in_specs=[natural, pl.BlockSpec((1, tc, S5_NSTATE), rows),
                  pl.BlockSpec((1, (T + 1) * LANES, LANES), per_sg), pl.BlockSpec((1, S5_NSTATE, T * LANES), per_sg)],
        out_specs=natural,
        out_shape=jax.ShapeDtypeStruct((S5_CH, T, SSM_WIDTH), F32),
        compiler_params=_cparams("arbitrary", "arbitrary", vmem=56 * 1024 * 1024), name="s5_out",
    )(xn, xstart, krev, wc)
    return y.reshape(N_TOK, SSM_WIDTH)


MERGE_TM = 512


def _merge_kernel(ocmp_ref, osel_ref, owin_ref, gn_ref, yssm_ref, u_ref, ga_ref, gs_ref, x_ref, mod_ref,
                  eg_ref, dskip_ref, wglu_ref, bglu_ref, wua_ref, wus_ref, wout_ref, g2_ref,
                  wrhi_ref, wrlo_ref, wsgu_ref, wsd_ref,
                  xpart_ref, h2_ref, logit_ref):
    mod = mod_ref[0]
    gnb = gn_ref[...].astype(BF16)
    o_nsa = (_dot(gnb, eg_ref[0]) * ocmp_ref[...].astype(F32)
             + _dot(gnb, eg_ref[1]) * osel_ref[...].astype(F32)
             + _dot(gnb, eg_ref[2]) * owin_ref[...].astype(F32))
    attn = _dot(o_nsa.astype(BF16), wua_ref[...])
    z = _gelu(yssm_ref[...] + dskip_ref[...] * u_ref[...])
    y_ssm = z * jax.nn.sigmoid(_dot(z.astype(BF16), wglu_ref[...]) + bglu_ref[...])
    ssm = _dot(y_ssm.astype(BF16), wus_ref[...])
    merged = ga_ref[...].astype(F32) * attn + gs_ref[...].astype(F32) * ssm
    x1 = x_ref[...] + mod[2:3] * _dot(merged.astype(BF16), wout_ref[...])

    ms = jnp.mean(x1 * x1, axis=-1, keepdims=True)
    h2 = (x1 * lax.rsqrt(ms + EPS) * g2_ref[...]) * (1.0 + mod[4:5]) + mod[3:4]
    hi = h2.astype(BF16)
    lo = (h2 - hi.astype(F32)).astype(BF16)
    h2_ref[...] = _pack_bf16_pairs(h2)
    logit_ref[...] = _dot_nt(wrhi_ref[...], hi) + _dot_nt(wrhi_ref[...], lo) + _dot_nt(wrlo_ref[...], hi)
    gu = _dot(hi, wsgu_ref[...])
    shared = _dot((_silu(gu[:, :D_EXPERT]) * gu[:, D_EXPERT:]).astype(BF16), wsd_ref[...])
    xpart_ref[...] = x1 + mod[5:6] * shared


def _merge(ocmp, osel, owin, gn, yssm, u, ga, gs, x2, mod, d_skip, w_glu, b_glu, w_up_attn, w_up_ssm, w_out,
           g_norm2, w_router, ws_gate, ws_up, ws_down):
    tm = MERGE_TM
    eg = np.zeros((3, LANES, NSA_WIDTH), np.float32)
    for j in range(3):
        for h in range(N_HEADS):
            eg[j, 3 * h + j, h * HEAD_DIM:(h + 1) * HEAD_DIM] = 1.0
    wr_t = w_router.T
    wr_hi = wr_t.astype(BF16)
    wr_lo = (wr_t - wr_hi.astype(F32)).astype(BF16)
    row = lambda i: (i, 0)
    fix2 = lambda i: (0, 0)
    wspec = lambda a: pl.BlockSpec(a.shape, (lambda i: (0,) * a.ndim))
    weights = [jnp.asarray(eg, BF16), d_skip.reshape(1, -1), w_glu.astype(BF16), b_glu.reshape(1, -1),
               w_up_attn.astype(BF16), w_up_ssm.astype(BF16), w_out.astype(BF16), g_norm2.reshape(1, -1),
               wr_hi, wr_lo, jnp.concatenate([ws_gate, ws_up], axis=1).astype(BF16), ws_down.astype(BF16)]
    acts = [(ocmp, 512), (osel, 512), (owin, 512), (gn, 128), (yssm, 512), (u, 512), (ga, 1024), (gs, 1024),
            (x2, 1024)]
    return pl.pallas_call(
        _merge_kernel,
        grid=(N_TOK // tm,),
        in_specs=[pl.BlockSpec((tm, wd), row) for _, wd in acts]
                 + [pl.BlockSpec((1, 6, D_MODEL), lambda i: (i // (SEQ // tm), 0, 0))]
                 + [wspec(w) for w in weights],
        out_specs=[pl.BlockSpec((tm, D_MODEL), row), pl.BlockSpec((tm, HALF), row),
                   pl.BlockSpec((N_EXPERTS, tm), lambda i: (0, i))],
        out_shape=[jax.ShapeDtypeStruct((N_TOK, D_MODEL), F32), jax.ShapeDtypeStruct((N_TOK, HALF), I32),
                   jax.ShapeDtypeStruct((N_EXPERTS, N_TOK), F32)],
        compiler_params=_cparams("arbitrary", vmem=56 * 1024 * 1024),
        name="merge",
    )(*[a for a, _ in acts], mod, *weights)


ROUTE_TN = 1024


def _route_kernel(logit_ref, bias_ref, eidx_ref, w_ref, count_ref, gscore_ref, masked_ref):
    tn = ROUTE_TN

    @pl.when(pl.program_id(0) == 0)
    def _():
        count_ref[...] = jnp.zeros(count_ref.shape, F32)

    sc = jax.nn.sigmoid(logit_ref[...])
    biased = sc + bias_ref[...]
    gi = lax.broadcasted_iota(I32, (EXPERTS_PER_GROUP, tn), 0).astype(F32)
    for g in range(N_EXPERT_GROUPS):
        blk = biased[g * EXPERTS_PER_GROUP:(g + 1) * EXPERTS_PER_GROUP]
        m1 = jnp.max(blk, axis=0, keepdims=True)
        i1 = jnp.min(jnp.where(blk == m1, gi, float(EXPERTS_PER_GROUP)), axis=0, keepdims=True)
        m2 = jnp.max(jnp.where(gi == i1, -jnp.inf, blk), axis=0, keepdims=True)
        gscore_ref[g:g + 1, :] = m1 + m2
    gs = gscore_ref[...]
    gidx = lax.broadcasted_iota(I32, (N_EXPERT_GROUPS, tn), 0)
    grank = jnp.zeros((N_EXPERT_GROUPS, tn), F32)
    for gp in range(N_EXPERT_GROUPS):
        row = gs[gp:gp + 1, :]
        tie = jnp.where(gidx > gp, 1.0, 0.0)
        grank = grank + jnp.where(row > gs, 1.0, jnp.where(row == gs, tie, 0.0))
    for g in range(N_EXPERT_GROUPS):
        keep = grank[g:g + 1, :] < float(TOPK_GROUPS)
        sl = slice(g * EXPERTS_PER_GROUP, (g + 1) * EXPERTS_PER_GROUP)
        masked_ref[sl, :] = jnp.where(keep, biased[sl], -jnp.inf)
    cur = masked_ref[...]
    eidx = lax.broadcasted_iota(I32, (N_EXPERTS, tn), 0).astype(F32)
    wsum = jnp.zeros((1, tn), F32)
    hits = jnp.zeros((N_EXPERTS, tn), F32)
    for k in range(TOP_K):
        m = jnp.max(cur, axis=0, keepdims=True)
        idx = jnp.min(jnp.where(cur == m, eidx, float(N_EXPERTS)), axis=0, keepdims=True)
        hit = eidx == idx
        wk = jnp.sum(jnp.where(hit, sc, 0.0), axis=0, keepdims=True)
        cur = jnp.where(hit, -jnp.inf, cur)
        hits = hits + jnp.where(hit, 1.0, 0.0)
        eidx_ref[k:k + 1, :] = idx.astype(I32)
        w_ref[k:k + 1, :] = wk
        wsum = wsum + wk
    w_ref[...] = w_ref[...] / wsum * ROUTE_SCALE
    count_ref[...] = count_ref[...] + jnp.sum(hits, axis=1, keepdims=True)


def _route(logits_t, router_bias):
    tn = ROUTE_TN
    return pl.pallas_call(
        _route_kernel,
        grid=(N_TOK // tn,),
        in_specs=[pl.BlockSpec((N_EXPERTS, tn), lambda i: (0, i)), pl.BlockSpec((N_EXPERTS, 1), lambda i: (0, 0))],
        out_specs=[pl.BlockSpec((TOP_K, tn), lambda i: (0, i))] * 2 + [pl.BlockSpec((N_EXPERTS, 1), lambda i: (0, 0))],
        out_shape=[jax.ShapeDtypeStruct((TOP_K, N_TOK), I32), jax.ShapeDtypeStruct((TOP_K, N_TOK), F32),
                   jax.ShapeDtypeStruct((N_EXPERTS, 1), F32)],
        scratch_shapes=[pltpu.VMEM((N_EXPERT_GROUPS, tn), F32), pltpu.VMEM((N_EXPERTS, tn), F32)],
        compiler_params=_cparams("arbitrary"),
        name="route",
    )(logits_t, router_bias.reshape(-1, 1))


N_MOE_BLK = NK // DISPATCH_BLOCK
N_ITEMS = N_MOE_BLK + N_EXPERTS
ASSIGN_BITS = 17


def _dispatch_plan(eidx, counts):
    e_flat = eidx.reshape(-1)
    key = jnp.sort(e_flat * NK + jnp.arange(NK, dtype=I32))
    order = key & (NK - 1)
    counts = counts.reshape(-1).astype(I32)
    start = jnp.cumsum(counts) - counts
    cuts = jnp.sort(jnp.concatenate([jnp.arange(N_MOE_BLK, dtype=I32) * DISPATCH_BLOCK, start]))
    lo = cuts
    hi = jnp.concatenate([cuts[1:], jnp.full((1,), NK, I32)])
    blk = jnp.minimum(lo // DISPATCH_BLOCK, N_MOE_BLK - 1)
    expert = jnp.clip(jnp.sum((start[None, :] <= lo[:, None]).astype(I32), axis=1) - 1, 0, N_EXPERTS - 1)
    one = jnp.ones((1,), I32)
    first = jnp.concatenate([one, (blk[1:] != blk[:-1]).astype(I32)])
    last = jnp.concatenate([(blk[1:] != blk[:-1]).astype(I32), one])
    new_expert = jnp.concatenate([one, (expert[1:] != expert[:-1]).astype(I32)])
    run_id = jnp.cumsum(new_expert) - 1
    n_runs = run_id[-1] + 1
    item = jnp.arange(N_ITEMS, dtype=I32)
    run_first_item = jnp.sort(jnp.where(new_expert == 1, item, N_ITEMS))
    run_expert = expert[jnp.minimum(run_first_item, N_ITEMS - 1)]
    ahead = run_id + (WEIGHT_RING - 1)
    ahead_expert = run_expert[jnp.minimum(ahead, N_ITEMS - 1)]
    ahead_valid = (ahead < n_runs).astype(I32)
    second_expert = run_expert[1:2]
    prologue = jnp.concatenate([second_expert, (n_runs > 1).astype(I32).reshape(1)])
    tok = jnp.right_shift(order, 3)
    home = (order & (TOP_K - 1)) * N_TOK + tok
    return tok, home, (blk, expert, lo - blk * DISPATCH_BLOCK, hi - blk * DISPATCH_BLOCK, first, last, new_expert,
                      run_id % WEIGHT_RING, ahead_expert, ahead_valid, prologue)


SC_CORES = 2
SC_SUBCORES = 16
SC_CHUNK = 128


def _sc_move_rows(table, idx, scatter):
    n = idx.shape[0]
    workers = SC_CORES * SC_SUBCORES
    per_worker = n // workers
    n_chunks = per_worker // SC_CHUNK
    assert per_worker * workers == n and n_chunks * SC_CHUNK == per_worker
    mesh = plsc.VectorSubcoreMesh(core_axis_name="c", subcore_axis_name="s",
                                  num_cores=SC_CORES, num_subcores=SC_SUBCORES)

    def body(table_hbm, idx_hbm, out_hbm, idx_v, rows_v, sem):
        wid = lax.axis_index("s") * SC_CORES + lax.axis_index("c")
        base = wid * per_worker

        @pl.loop(0, n_chunks)
        def _(j):
            off = base + j * SC_CHUNK
            pltpu.sync_copy(idx_hbm.at[pl.ds(off, SC_CHUNK)], idx_v)
            if scatter:
                pltpu.sync_copy(table_hbm.at[pl.ds(off, SC_CHUNK)], rows_v)
                pltpu.async_copy(rows_v, out_hbm.at[idx_v], sem).wait()
            else:
                pltpu.async_copy(table_hbm.at[idx_v], rows_v, sem).wait()
                pltpu.sync_copy(rows_v, out_hbm.at[pl.ds(off, SC_CHUNK)])

    return pl.kernel(
        body,
        out_type=jax.ShapeDtypeStruct((n, table.shape[1]), table.dtype),
        mesh=mesh,
        scratch_types=[pltpu.VMEM((SC_CHUNK,), I32), pltpu.VMEM((SC_CHUNK, table.shape[1]), table.dtype),
                       pltpu.SemaphoreType.DMA],
        name="sc_scatter_rows" if scatter else "sc_gather_rows",
    )(table, idx)


WEIGHT_RING = 3
WEIGHT_CHUNKS = 4


def _expert_weight_copies(w_hbm, wbuf, sem, expert, slot):
    rows = w_hbm.shape[1] // WEIGHT_CHUNKS
    return [pltpu.make_async_copy(w_hbm.at[expert, pl.ds(c * rows, rows)],
                                  wbuf.at[slot, pl.ds(c * rows, rows)], sem.at[slot])
            for c in range(WEIGHT_CHUNKS)]


def _moe_kernel(blk_ref, exp_ref, lo_ref, hi_ref, first_ref, last_ref, newexp_ref,
                slot_ref, ahead_exp_ref, ahead_ok_ref, prologue_ref,
                x_ref, wg_hbm, wu_hbm, wd_hbm, y_ref,
                acc_ref, wgf_ref, wuf_ref, wdf_ref, wgb_ref, wub_ref, wdb_ref, wsem):
    it = pl.program_id(0)
    lo, hi = lo_ref[it], hi_ref[it]
    streams = ((wg_hbm, wgf_ref), (wu_hbm, wuf_ref), (wd_hbm, wdf_ref))

    def request(expert, slot):
        for w_hbm, wbuf in streams:
            for cp in _expert_weight_copies(w_hbm, wbuf, wsem, expert, slot):
                cp.start()

    @pl.when(it == 0)
    def _():
        request(exp_ref[0], 0)

        @pl.when(prologue_ref[1] == 1)
        def _():
            request(prologue_ref[0], 1)

    @pl.when(newexp_ref[it] == 1)
    def _():
        slot = slot_ref[it]
        for w_hbm, wbuf in streams:
            for cp in _expert_weight_copies(w_hbm, wbuf, wsem, 0, slot):
                cp.wait()
        wgb_ref[...] = wgf_ref[slot].astype(BF16)
        wub_ref[...] = wuf_ref[slot].astype(BF16)
        wdb_ref[...] = wdf_ref[slot].astype(BF16)

        @pl.when(ahead_ok_ref[it] == 1)
        def _():
            ahead_slot = slot + (WEIGHT_RING - 1)
            request(ahead_exp_ref[it], jnp.where(ahead_slot >= WEIGHT_RING, ahead_slot - WEIGHT_RING, ahead_slot))

    @pl.when(first_ref[it] == 1)
    def _():
        acc_ref[...] = jnp.zeros(acc_ref.shape, F32)

    def expert_pass(r0, nrows):
        rows = slice(r0, r0 + nrows)
        ridx = r0 + lax.broadcasted_iota(I32, (nrows, HALF), 0)
        mine = (ridx >= lo) & (ridx < hi)
        xlo, xhi = _unpack_bf16_pairs(jnp.where(mine, x_ref[rows, :], 0))
        xlo, xhi = xlo.astype(BF16), xhi.astype(BF16)
        gate = _dot(xlo, wgb_ref[:HALF]) + _dot(xhi, wgb_ref[HALF:])
        up = _dot(xlo, wub_ref[:HALF]) + _dot(xhi, wub_ref[HALF:])
        acc_ref[rows, :] = acc_ref[rows, :] + _dot((_silu(gate) * up).astype(BF16), wdb_ref[...])

    mid = DISPATCH_BLOCK // 2
    pl.when((lo < mid) & (hi > mid))(lambda: expert_pass(0, DISPATCH_BLOCK))
    pl.when((hi > lo) & (hi <= mid))(lambda: expert_pass(0, mid))
    pl.when((hi > lo) & (lo >= mid))(lambda: expert_pass(mid, mid))

    @pl.when(last_ref[it] == 1)
    def _():
        y_ref[...] = _pack_bf16_pairs(acc_ref[...])


def _moe(xs, items, w_gate, w_up, w_down):
    by_blk = lambda it, blk, *_: (blk[it], 0)
    any_space = pl.BlockSpec(memory_space=pl.ANY)
    grid_spec = pltpu.PrefetchScalarGridSpec(
        num_scalar_prefetch=len(items),
        grid=(N_ITEMS,),
        in_specs=[pl.BlockSpec((DISPATCH_BLOCK, HALF), by_blk), any_space, any_space, any_space],
        out_specs=pl.BlockSpec((DISPATCH_BLOCK, HALF), by_blk),
        scratch_shapes=[pltpu.VMEM((DISPATCH_BLOCK, D_MODEL), F32),
                        pltpu.VMEM((WEIGHT_RING, D_MODEL, D_EXPERT), F32),
                        pltpu.VMEM((WEIGHT_RING, D_MODEL, D_EXPERT), F32),
                        pltpu.VMEM((WEIGHT_RING, D_EXPERT, D_MODEL), F32),
                        pltpu.VMEM((D_MODEL, D_EXPERT), BF16), pltpu.VMEM((D_MODEL, D_EXPERT), BF16),
                        pltpu.VMEM((D_EXPERT, D_MODEL), BF16),
                        pltpu.SemaphoreType.DMA((WEIGHT_RING,))],
    )
    return pl.pallas_call(
        _moe_kernel,
        grid_spec=grid_spec,
        out_shape=jax.ShapeDtypeStruct((NK, HALF), I32),
        compiler_params=_cparams("arbitrary"),
        name="moe",
    )(*items, xs, w_gate, w_up, w_down)


COMB_TC = 512


def _combine_kernel(slots_ref, w_ref, xpart_ref, mod_ref, out_ref):
    w = w_ref[...]
    lo = jnp.zeros((w.shape[0], HALF), F32)
    hi = jnp.zeros((w.shape[0], HALF), F32)
    for k in range(TOP_K):
        klo, khi = _unpack_bf16_pairs(slots_ref[k])
        lo = lo + w[:, k:k + 1] * klo
        hi = hi + w[:, k:k + 1] * khi
    gate2 = mod_ref[0][5:6]
    out_ref[:, :HALF] = xpart_ref[:, :HALF] + gate2[:, :HALF] * lo
    out_ref[:, HALF:] = xpart_ref[:, HALF:] + gate2[:, HALF:] * hi


def _combine(xpart, mod, slots, w):
    tc = COMB_TC
    row = lambda i: (i, 0)
    return pl.pallas_call(
        _combine_kernel,
        grid=(N_TOK // tc,),
        in_specs=[pl.BlockSpec((TOP_K, tc, HALF), lambda i: (0, i, 0)),
                  pl.BlockSpec((tc, TOP_K), row),
                  pl.BlockSpec((tc, D_MODEL), row),
                  pl.BlockSpec((1, 6, D_MODEL), lambda i: (i // (SEQ // tc), 0, 0))],
        out_specs=pl.BlockSpec((tc, D_MODEL), row),
        out_shape=jax.ShapeDtypeStruct((N_TOK, D_MODEL), F32),
        compiler_params=_cparams("arbitrary"),
        name="combine",
    )(slots.reshape(TOP_K, N_TOK, HALF), w, xpart, mod)


def _layer(x, c, w_ada, b_ada, g_norm1, g_norm2, w_in, q_gain, kc_gain, ks_gain, kw_gain,
           pe_k, pe_v, w_cmp_k1, w_cmp_k2, w_cmp_v1, w_cmp_v2,
           a_re, a_im, log_dt, b_re, b_im, c_re, c_im, d_skip, w_glu, b_glu,
           w_up_attn, w_up_ssm, w_out, w_router, router_bias,
           w_gate, w_up, w_down, ws_gate, ws_up, ws_down):
    x2 = x.reshape(N_TOK, D_MODEL)
    mod = _ada(c, w_ada, b_ada)
    q, kc_raw, vc_raw, ks, kw, vst, vwt, gn, u, ga, gs = _proj(x2, mod, g_norm1, w_in, q_gain, ks_gain, kw_gain)
    kcn = _compress(kc_raw, pe_k, w_cmp_k1, w_cmp_k2, kc_gain, True)
    vcn = _compress(vc_raw, pe_v, w_cmp_v1, w_cmp_v2, kc_gain, False)
    ocmp, selb = _cmp_attn(q, kcn, vcn)
    osel, owin = _selwin(q, ks, kw, vst, vwt, selb)
    yssm = _s5(u, *_s5_params(a_re, a_im, log_dt, b_re, b_im, c_re, c_im))
    xpart, h2, logits_t = _merge(ocmp, osel, owin, gn, yssm, u, ga, gs, x2, mod, d_skip, w_glu, b_glu,
                                  w_up_attn, w_up_ssm, w_out, g_norm2, w_router, ws_gate, ws_up, ws_down)
    eidx_t, w_t, counts = _route(logits_t, router_bias)
    tok, home, items = _dispatch_plan(eidx_t.T, counts)
    y = _moe(_sc_move_rows(h2, tok, scatter=False), items, w_gate, w_up, w_down)
    slots = _sc_move_rows(y, home, scatter=True)
    return _combine(xpart, mod, slots, w_t.T).reshape(BATCH, SEQ, D_MODEL)


def kernel(x, c, w_ada, b_ada, g_norm1, g_norm2, w_in, q_gain, kc_gain, ks_gain, kw_gain, pe_k, pe_v, w_cmp_k1,
           w_cmp_k2, w_cmp_v1, w_cmp_v2, a_re, a_im, log_dt, b_re, b_im, c_re, c_im, d_skip, w_glu, b_glu,
           w_up_attn, w_up_ssm, w_out, w_router, router_bias, w_gate, w_up, w_down, ws_gate, ws_up, ws_down):
    params = (w_ada, b_ada, g_norm1, g_norm2, w_in, q_gain, kc_gain, ks_gain, kw_gain, pe_k, pe_v, w_cmp_k1,
              w_cmp_k2, w_cmp_v1, w_cmp_v2, a_re, a_im, log_dt, b_re, b_im, c_re, c_im, d_skip, w_glu, b_glu,
              w_up_attn, w_up_ssm, w_out, w_router, router_bias, w_gate, w_up, w_down, ws_gate, ws_up, ws_down)
    depth = w_ada.shape[0]
    for layer in range(depth):
        x = _layer(x, c, *[p[layer] for p in params])
    return x
```

```python
import functools

import jax
import jax.numpy as jnp
import numpy as np
from jax import lax
from jax.experimental import pallas as pl
from jax.experimental.pallas import tpu as pltpu
from jax.experimental.pallas import tpu_sc as plsc

F32 = jnp.float32
BF16 = jnp.bfloat16
I32 = jnp.int32
HIGHEST = lax.Precision.HIGHEST

D_MODEL = 1024
BATCH = 4
SEQ = 4096
N_TOK = BATCH * SEQ
N_HEADS = 8
HEAD_DIM = 64
N_KV = 2
CMP_BLOCK = 32
CMP_STRIDE = 16
CMP_HIDDEN = 256
N_CMP = 256
SEL_BLOCK = 64
N_SEL_BLOCKS = SEQ // SEL_BLOCK
N_SELECT = 16
WINDOW = 512
ATTN_SCALE = HEAD_DIM ** -0.5
LOG2E = 1.4426950408889634
NSA_WIDTH = N_HEADS * HEAD_DIM
SSM_WIDTH = 512
GROUP = 16
N_GROUPS = SSM_WIDTH // GROUP
STATE = 64
N_EXPERTS = 256
TOP_K = 8
D_EXPERT = 256
N_EXPERT_GROUPS = 8
EXPERTS_PER_GROUP = N_EXPERTS // N_EXPERT_GROUPS
TOPK_GROUPS = 4
ROUTE_SCALE = 2.5
DISPATCH_BLOCK = 512
EPS = 1e-6
NEG = -1e30

LANES = 128
S5_T = 16
S5_SG = 4
S5_GL = N_GROUPS // S5_SG
S5_CH = N_TOK // S5_T
S5_CH_PER_BATCH = SEQ // S5_T
S5_NSTATE = S5_GL * STATE * 2

NK = N_TOK * TOP_K
HALF = D_MODEL // 2

VMEM_LIMIT = 48 * 1024 * 1024


def _cparams(*sem, vmem=VMEM_LIMIT):
    return pltpu.CompilerParams(dimension_semantics=tuple(sem), vmem_limit_bytes=vmem)


def _dot(a, b):
    return jnp.dot(a, b, preferred_element_type=F32)


def _dot_nt(a, b):
    return lax.dot_general(a, b, (((1,), (1,)), ((), ())), preferred_element_type=F32)


def _split_dot(v, w):
    hi = v.astype(BF16)
    lo = (v - hi.astype(F32)).astype(BF16)
    return _dot(hi, w) + _dot(lo, w)


def _seg_rms(v, bd, gain):
    ss = _split_dot(v * v, bd)
    return v * lax.rsqrt(ss * (1.0 / HEAD_DIM) + EPS) * gain


def _gelu(x):
    return 0.5 * x * (1.0 + jnp.tanh(0.7978845608028654 * (x + 0.044715 * (x * x * x))))


def _silu(x):
    return x * jax.nn.sigmoid(x)


def _pack_bf16_pairs(v):
    h = v.shape[1] // 2
    return pltpu.pack_elementwise([v[:, :h], v[:, h:]], packed_dtype=BF16)


def _unpack_bf16_pairs(word):
    return (pltpu.unpack_elementwise(word, index=0, packed_dtype=BF16, unpacked_dtype=F32),
            pltpu.unpack_elementwise(word, index=1, packed_dtype=BF16, unpacked_dtype=F32))


def _ada_kernel(c_ref, w_ref, b_ref, o_ref):
    c = c_ref[...]
    o_ref[...] = jnp.dot(_silu(c), w_ref[...], preferred_element_type=F32, precision=HIGHEST) + b_ref[...]


def _ada(c, w_ada, b_ada):
    cp = jnp.pad(c, ((0, 8 - BATCH), (0, 0)))
    tn = 1536
    out = pl.pallas_call(
        _ada_kernel,
        grid=(6 * D_MODEL // tn,),
        in_specs=[pl.BlockSpec((8, D_MODEL), lambda j: (0, 0)),
                  pl.BlockSpec((D_MODEL, tn), lambda j: (0, j)),
                  pl.BlockSpec((1, tn), lambda j: (0, j))],
        out_specs=pl.BlockSpec((8, tn), lambda j: (0, j)),
        out_shape=jax.ShapeDtypeStruct((8, 6 * D_MODEL), F32),
        compiler_params=_cparams("arbitrary"),
        name="ada",
    )(cp, w_ada, b_ada.reshape(1, -1))
    return out.reshape(8, 6, D_MODEL)


_C_Q = 0
_C_KC = 512
_C_VC = 640
_C_KS = 768
_C_KW = 1024
_C_GN = 1280
_C_U = 1408
_C_GA = 1920
_C_GS = 2944
_C_END = 3968
PROJ_TM = 512


def _proj_kernel(x_ref, mod_ref, g1_ref, w_ref, wvt_ref, qg_ref, ksg_ref, kwg_ref, bd512_ref, bd256_ref,
                 q_ref, kc_ref, vc_ref, ks_ref, kw_ref, vst_ref, vwt_ref, gn_ref, u_ref, ga_ref, gs_ref):
    x = x_ref[...]
    ms = jnp.mean(x * x, axis=-1, keepdims=True)
    mod = mod_ref[0]
    h = (x * lax.rsqrt(ms + EPS) * g1_ref[...]) * (1.0 + mod[1:2]) + mod[0:1]
    hb = h.astype(BF16)

    def p(lo, hi):
        return _dot(hb, w_ref[:, lo:hi])

    q_ref[...] = _seg_rms(p(_C_Q, _C_KC), bd512_ref[...], qg_ref[...] * (ATTN_SCALE * LOG2E)).astype(BF16)
    kvc = p(_C_KC, _C_KS)
    kc_ref[...] = kvc[:, :LANES]
    vc_ref[...] = kvc[:, LANES:]
    ks_ref[...] = _seg_rms(p(_C_KS, _C_KW), bd256_ref[...], ksg_ref[...]).astype(BF16)
    kw_ref[...] = _seg_rms(p(_C_KW, _C_GN), bd256_ref[...], kwg_ref[...]).astype(BF16)
    vt = _dot_nt(wvt_ref[...], hb)
    vst_ref[...] = vt[:LANES].astype(BF16)
    vwt_ref[...] = vt[LANES:].astype(BF16)
    gnu = p(_C_GN, _C_GA)
    gn_ref[...] = jax.nn.sigmoid(gnu[:, :LANES])
    u_ref[...] = gnu[:, LANES:]
    ga_ref[...] = jax.nn.sigmoid(p(_C_GA, _C_GS)).astype(BF16)
    gs_ref[...] = jax.nn.sigmoid(p(_C_GS, _C_END)).astype(BF16)


def _dup_cols(w):
    return jnp.concatenate([w[:, :64], w[:, :64], w[:, 64:], w[:, 64:]], axis=1)


def _block_ones(n):
    return jnp.kron(jnp.eye(n // HEAD_DIM, dtype=F32), jnp.ones((HEAD_DIM, HEAD_DIM), F32)).astype(BF16)


def _proj(x2, mod, g_norm1, w_in, q_gain, ks_gain, kw_gain):
    o = np.cumsum((0, 512, 128, 128, 128, 128, 128, 128, 24, 512, 1024, 1024))
    parts = [w_in[:, o[i]:o[i + 1]] for i in range(11)]
    wq, wkc, wvc, wks, wvs, wkw, wvw, wgn, wu, wga, wgs = parts
    w = jnp.concatenate([wq, wkc, wvc, _dup_cols(wks), _dup_cols(wkw),
                         jnp.pad(wgn, ((0, 0), (0, LANES - 24))), wu, wga, wgs], axis=1).astype(BF16)
    wvt = jnp.concatenate([wvs, wvw], axis=1).T.astype(BF16)
    tm = PROJ_TM
    row = lambda i: (i, 0)
    col = lambda i: (0, i)
    fix = lambda i: (0, 0)
    outs = [(512, BF16, row), (128, F32, row), (128, F32, row), (256, BF16, row), (256, BF16, row),
            (LANES, BF16, col), (LANES, BF16, col),
            (128, F32, row), (512, F32, row), (1024, BF16, row), (1024, BF16, row)]
    ospec = lambda wd, m: pl.BlockSpec((tm, wd), m) if m is row else pl.BlockSpec((wd, tm), m)
    oshape = lambda wd, dt, m: jax.ShapeDtypeStruct((N_TOK, wd) if m is row else (wd, N_TOK), dt)
    return pl.pallas_call(
        _proj_kernel,
        grid=(N_TOK // tm,),
        in_specs=[pl.BlockSpec((tm, D_MODEL), row),
                  pl.BlockSpec((1, 6, D_MODEL), lambda i: (i // (SEQ // tm), 0, 0)),
                  pl.BlockSpec((1, D_MODEL), fix),
                  pl.BlockSpec((D_MODEL, _C_END), fix),
                  pl.BlockSpec((2 * LANES, D_MODEL), fix),
                  pl.BlockSpec((1, 512), fix), pl.BlockSpec((1, 256), fix), pl.BlockSpec((1, 256), fix),
                  pl.BlockSpec((512, 512), fix), pl.BlockSpec((256, 256), fix)],
        out_specs=[ospec(wd, m) for wd, _, m in outs],
        out_shape=[oshape(wd, dt, m) for wd, dt, m in outs],
        compiler_params=_cparams("arbitrary"),
        name="proj",
    )(x2, mod, g_norm1.reshape(1, -1), w, wvt,
      jnp.tile(q_gain, N_HEADS).reshape(1, -1), jnp.tile(ks_gain, 4).reshape(1, -1),
      jnp.tile(kw_gain, 4).reshape(1, -1), _block_ones(512), _block_ones(256))


def _compress_kernel(r_ref, pe_ref, w1_ref, w2_ref, bd_ref, gain_ref, o_ref, *, do_norm):
    r = jnp.concatenate([r_ref[0, :, l, :] for l in range(CMP_STRIDE)], axis=1)
    p0 = _dot((r + pe_ref[0]).astype(BF16), w1_ref[0])
    p1 = _dot((r + pe_ref[1]).astype(BF16), w1_ref[1])
    hid = p0 + pltpu.roll(p1, N_CMP - 1, 0)
    c = _dot(_gelu(hid).astype(BF16), w2_ref[...])
    if do_norm:
        c = _seg_rms(c, bd_ref[...], gain_ref[...])
    o_ref[0] = c.astype(BF16)


def _compress(raw, pe, w1, w2, gain, do_norm):
    r = raw.reshape(BATCH, SEQ // CMP_STRIDE, CMP_STRIDE, LANES)
    eye = jnp.eye(N_KV, dtype=F32)
    w1r = w1.reshape(2, CMP_STRIDE, HEAD_DIM, CMP_HIDDEN)
    w1big = jnp.einsum('hldc,gk->hlgdkc', w1r, eye).reshape(2, CMP_STRIDE * LANES, N_KV * CMP_HIDDEN).astype(BF16)
    w2big = jnp.einsum('cd,gk->gckd', w2, eye)
    w2big = jnp.concatenate([w2big, w2big], axis=-1).reshape(N_KV * CMP_HIDDEN, 4 * HEAD_DIM).astype(BF16)
    pe_big = jnp.broadcast_to(pe.reshape(2, CMP_STRIDE, 1, HEAD_DIM), (2, CMP_STRIDE, N_KV, HEAD_DIM))
    pe_big = pe_big.reshape(2, 1, CMP_STRIDE * LANES)
    fix2 = lambda b: (0, 0)
    fix3 = lambda b: (0, 0, 0)
    return pl.pallas_call(
        functools.partial(_compress_kernel, do_norm=do_norm),
        grid=(BATCH,),
        in_specs=[pl.BlockSpec((1, N_CMP, CMP_STRIDE, LANES), lambda b: (b, 0, 0, 0)),
                  pl.BlockSpec((2, 1, CMP_STRIDE * LANES), fix3),
                  pl.BlockSpec((2, CMP_STRIDE * LANES, N_KV * CMP_HIDDEN), fix3),
                  pl.BlockSpec((N_KV * CMP_HIDDEN, 256), fix2),
                  pl.BlockSpec((256, 256), fix2), pl.BlockSpec((1, 256), fix2)],
        out_specs=pl.BlockSpec((1, N_CMP, 256), lambda b: (b, 0, 0)),
        out_shape=jax.ShapeDtypeStruct((BATCH, N_CMP, 256), BF16),
        compiler_params=_cparams("arbitrary"),
        name="compress_k" if do_norm else "compress_v",
    )(r, pe_big, w1big, w2big, _block_ones(256), jnp.tile(gain, 4).reshape(1, -1))


ATT_TQ = 1024
RANK_CHUNK = 16


def _head_variants(qb):
    lane = lax.broadcasted_iota(I32, qb.shape, 1)
    z = jnp.zeros_like(qb)
    return jnp.where(lane < HEAD_DIM, qb, z), jnp.where(lane < HEAD_DIM, z, qb)


def _cmp_kernel(q_ref, kc_ref, vc_ref, ov_ref, o_ref, sel_ref, vrank_ref):
    tq = ATT_TQ
    qi = pl.program_id(1)
    tpos = qi * tq + lax.broadcasted_iota(I32, (tq, N_CMP), 0)
    nidx = lax.broadcasted_iota(I32, (tq, N_CMP), 1)
    mask = (CMP_STRIDE * nidx + (CMP_BLOCK - 1)) <= tpos
    lane_lo = lax.broadcasted_iota(I32, (tq, LANES), 1) < HEAD_DIM
    for g in range(N_KV):
        kd = kc_ref[0, :, g * LANES:(g + 1) * LANES]
        vd = vc_ref[0, :, g * LANES:(g + 1) * LANES]
        psum = jnp.zeros((tq, N_CMP), F32)
        for jb in range(2):
            blk = 2 * g + jb
            pv = []
            for qv in _head_variants(q_ref[:, blk * LANES:(blk + 1) * LANES]):
                s = jnp.where(mask, _dot_nt(qv, kd), NEG)
                m = jnp.max(s, axis=-1, keepdims=True)
                e = jnp.where(mask, jnp.exp2(s - m), 0.0)
                l = jnp.sum(e, axis=-1, keepdims=True)
                p = e / jnp.where(l > 0.0, l, 1.0)
                psum = psum + p
                pv.append(_dot(p.astype(BF16), vd))
            o_ref[:, blk * LANES:(blk + 1) * LANES] = jnp.where(lane_lo, pv[0], pv[1]).astype(BF16)
        imp = _split_dot(psum, ov_ref[...])
        imp_t = imp.T[:N_SEL_BLOCKS]
        j = lax.broadcasted_iota(I32, (N_SEL_BLOCKS, tq), 0)
        cur = jnp.right_shift(qi * tq + lax.broadcasted_iota(I32, (N_SEL_BLOCKS, tq), 1),
                              SEL_BLOCK.bit_length() - 1)
        forced = (j == 0) | (j == cur) | (j == cur - 1)
        v = jnp.where(forced, jnp.inf, jnp.where(j <= cur, imp_t, -jnp.inf))
        vrank_ref[...] = jnp.zeros((N_SEL_BLOCKS, tq), F32)
        n_live = (qi + 1) * (tq // SEL_BLOCK)
        for c0 in range(0, N_SEL_BLOCKS, RANK_CHUNK):
            @pl.when(c0 < n_live)
            def _():
                rank = vrank_ref[...]
                for jp in range(c0, c0 + RANK_CHUNK):
                    row = v[jp:jp + 1, :]
                    tie = jnp.where(j > jp, 1.0, 0.0)
                    rank = rank + jnp.where(row > v, 1.0, jnp.where(row == v, tie, 0.0))
                vrank_ref[...] = rank
        rank = vrank_ref[...]
        sel_ref[g * N_SEL_BLOCKS:(g + 1) * N_SEL_BLOCKS, :] = jnp.where(rank < float(N_SELECT), 0.0, NEG)


def _cmp_attn(q, kcn, vcn):
    nc = np.arange(N_CMP)
    sb = np.arange(LANES)
    ov = ((CMP_STRIDE * nc[:, None] < SEL_BLOCK * sb[None, :] + SEL_BLOCK)
          & (CMP_STRIDE * nc[:, None] + CMP_BLOCK > SEL_BLOCK * sb[None, :])
          & (nc[:, None] < N_CMP - 1) & (sb[None, :] < N_SEL_BLOCKS))
    ov = jnp.asarray(ov, BF16)
    tq = ATT_TQ
    nq = SEQ // tq
    row = lambda b, i: (b * nq + i, 0)
    return pl.pallas_call(
        _cmp_kernel,
        grid=(BATCH, nq),
        in_specs=[pl.BlockSpec((tq, NSA_WIDTH), row),
                  pl.BlockSpec((1, N_CMP, 256), lambda b, i: (b, 0, 0)),
                  pl.BlockSpec((1, N_CMP, 256), lambda b, i: (b, 0, 0)),
                  pl.BlockSpec((N_CMP, LANES), lambda b, i: (0, 0))],
        out_specs=[pl.BlockSpec((tq, NSA_WIDTH), row),
                   pl.BlockSpec((N_KV * N_SEL_BLOCKS, tq), lambda b, i: (0, b * nq + i))],
        out_shape=[jax.ShapeDtypeStruct((N_TOK, NSA_WIDTH), BF16),
                   jax.ShapeDtypeStruct((N_KV * N_SEL_BLOCKS, N_TOK), F32)],
        scratch_shapes=[pltpu.VMEM((N_SEL_BLOCKS, tq), F32)],
        compiler_params=_cparams("arbitrary", "arbitrary"),
        name="cmp_attn",
    )(q, kcn, vcn, ov)


M_INIT = -1e29


SUM_ROWS = 16
SW_TQ = 512
SEL_TK = 512


def _selwin_kernel(q_ref, ks_ref, kw_ref, vst_ref, vwt_ref, selb_ref, osel_ref, owin_ref, m_ref, acc_ref):
    tq = tk = SW_TQ
    qi = pl.program_id(1)
    krow = lax.broadcasted_iota(I32, (tk, tq), 0)
    qcol = lax.broadcasted_iota(I32, (tk, tq), 1)
    causal_bias = jnp.where(krow <= qcol, 0.0, NEG)
    far_bias = jnp.where(qcol < krow, 0.0, NEG)

    def reset():
        m_ref[...] = jnp.full(m_ref.shape, M_INIT, F32)
        acc_ref[...] = jnp.zeros(acc_ref.shape, F32)

    def update(g, k_ref, vt_ref, kt, bias, nk=tk):
        k0 = pl.multiple_of(kt * nk, nk)
        kd = k_ref[0, pl.ds(k0, nk), g * LANES:(g + 1) * LANES]
        vt = vt_ref[g * HEAD_DIM:(g + 1) * HEAD_DIM, pl.ds(k0, nk)]
        s = _dot_nt(kd, qvars[g])
        if bias is not None:
            s = s + jnp.concatenate([bias] * 4, axis=1)
        m_old = m_ref[g]
        m_new = jnp.maximum(m_old, jnp.max(s, axis=0, keepdims=True))
        alpha = jnp.exp2(m_old - m_new)
        p = jnp.exp2(s - m_new)
        m_ref[g] = m_new
        vte = jnp.concatenate([vt, jnp.ones((SUM_ROWS, nk), BF16)], axis=0)
        acc_ref[g] = alpha * acc_ref[g] + _dot(vte, p.astype(BF16))

    def finish(out_ref, g):
        o = acc_ref[g, :HEAD_DIM, :] / acc_ref[g, HEAD_DIM:HEAD_DIM + 1, :]
        for jb in range(2):
            blk = 2 * g + jb
            pair = jnp.concatenate([o[:, 2 * jb * tq:(2 * jb + 1) * tq], o[:, (2 * jb + 1) * tq:(2 * jb + 2) * tq]],
                                   axis=0)
            out_ref[:, blk * LANES:(blk + 1) * LANES] = pair.T.astype(BF16)

    def sel_bias(g, kt):
        rows = [jnp.broadcast_to(selb_ref[pl.ds(g * N_SEL_BLOCKS + kt * (SEL_TK // SEL_BLOCK) + r, 1), :],
                                 (SEL_BLOCK, tq)) for r in range(SEL_TK // SEL_BLOCK)]
        return jnp.concatenate(rows, axis=0)

    qvars = []
    for g in range(N_KV):
        heads = []
        for jb in range(2):
            heads.extend(_head_variants(q_ref[:, (2 * g + jb) * LANES:(2 * g + jb + 1) * LANES]))
        qvars.append(jnp.concatenate(heads, axis=0))
    groups = range(N_KV)

    reset()
    last_tile = qi // (SEL_TK // tq)

    def sel_step(kt, carry):
        for g in groups:
            update(g, ks_ref, vst_ref, kt, sel_bias(g, kt), SEL_TK)
        return carry

    lax.fori_loop(0, last_tile, sel_step, 0)
    q_first = qi * tq - last_tile * SEL_TK
    visible = (lax.broadcasted_iota(I32, (SEL_TK, tq), 0)
               <= lax.broadcasted_iota(I32, (SEL_TK, tq), 1) + q_first)
    diag_bias = jnp.where(visible, 0.0, NEG)
    for g in groups:
        update(g, ks_ref, vst_ref, last_tile, sel_bias(g, last_tile) + diag_bias, SEL_TK)
    for g in groups:
        finish(osel_ref, g)

    reset()
    back = WINDOW // tk

    @pl.when(qi >= back)
    def _():
        for g in groups:
            update(g, kw_ref, vwt_ref, qi - back, far_bias)

    for d in range(back - 1, 0, -1):
        @pl.when(qi >= d)
        def _():
            for g in groups:
                update(g, kw_ref, vwt_ref, qi - d, None)

    for g in groups:
        update(g, kw_ref, vwt_ref, qi, causal_bias)
    for g in groups:
        finish(owin_ref, g)


def _selwin(q, ks, kw, vst, vwt, selb):
    tq = SW_TQ
    nq = SEQ // tq
    assert WINDOW % SW_TQ == 0 and SEL_TK % SW_TQ == 0
    row = lambda b, i: (b * nq + i, 0)
    keys = pl.BlockSpec((1, SEQ, 256), lambda b, i: (b, 0, 0))
    vals = pl.BlockSpec((LANES, SEQ), lambda b, i: (0, b))
    r3 = lambda a: a.reshape(BATCH, SEQ, 256)
    return pl.pallas_call(
        _selwin_kernel,
        grid=(BATCH, nq),
        in_specs=[pl.BlockSpec((tq, NSA_WIDTH), row), keys, keys, vals, vals,
                  pl.BlockSpec((N_KV * N_SEL_BLOCKS, tq), lambda b, i: (0, b * nq + i))],
        out_specs=[pl.BlockSpec((tq, NSA_WIDTH), row)] * 2,
        out_shape=[jax.ShapeDtypeStruct((N_TOK, NSA_WIDTH), BF16)] * 2,
        scratch_shapes=[pltpu.VMEM((N_KV, 1, 4 * tq), F32),
                        pltpu.VMEM((N_KV, HEAD_DIM + SUM_ROWS, 4 * tq), F32)],
        compiler_params=_cparams("arbitrary", "arbitrary"),
        name="selwin",
    )(q, r3(ks), r3(kw), vst, vwt, selb)


def _s5_param_kernel(are_ref, aim_ref, ldt_ref, cre_ref, cim_ref, bre_ref, bim_ref,
                     clre_ref, clim_ref, wbre_ref, wbim_ref, bbre_ref, bbim_ref, ltre_ref, ltim_ref):
    are, aim = are_ref[...], aim_ref[...]
    dt = jnp.exp(ldt_ref[...])
    cre, cim = cre_ref[...], cim_ref[...]

    def lam_pow(tau):
        mag = jnp.exp(are * dt * float(tau))
        ang = aim * dt * float(tau)
        return mag * jnp.cos(ang), mag * jnp.sin(ang)

    lre, lim = lam_pow(1)
    den = are * are + aim * aim
    qre = ((lre - 1.0) * are + lim * aim) / den
    qim = (lim * are - (lre - 1.0) * aim) / den
    bre, bim = bre_ref[...], bim_ref[...]
    bbre = qre * bre - qim * bim
    bbim = qre * bim + qim * bre
    bbre_ref[...] = bbre
    bbim_ref[...] = bbim
    for tau in range(S5_T + 1):
        pr, pi = lam_pow(tau)
        clre_ref[tau] = cre * pr - cim * pi
        clim_ref[tau] = cre * pi + cim * pr
        if tau < S5_T:
            k = S5_T - 1 - tau
            wbre_ref[k] = pr * bbre - pi * bbim
            wbim_ref[k] = pr * bbim + pi * bbre
        else:
            ltre_ref[...] = pr
            ltim_ref[...] = pi


def _s5_kmat_kernel(l_ref, r_ref, o_ref):
    o_ref[0] = jnp.dot(l_ref[0], r_ref[0], preferred_element_type=F32, precision=HIGHEST)


def _s5_params(a_re, a_im, log_dt, b_re, b_im, c_re, c_im):
    T = S5_T
    pn = GROUP * STATE
    tile_p = lambda a: jnp.tile(a, (1, GROUP))
    args = (tile_p(a_re), tile_p(a_im), jnp.broadcast_to(log_dt[:, None], (N_GROUPS, pn)),
            c_re.reshape(N_GROUPS, pn), c_im.reshape(N_GROUPS, pn),
            jnp.swapaxes(b_re, 1, 2).reshape(N_GROUPS, pn), jnp.swapaxes(b_im, 1, 2).reshape(N_GROUPS, pn))
    full2 = pl.BlockSpec((N_GROUPS, pn), lambda: (0, 0))
    clre, clim, wbre, wbim, bbre, bbim, ltre, ltim = pl.pallas_call(
        _s5_param_kernel,
        in_specs=[full2] * 7,
        out_specs=[pl.BlockSpec((T + 1, N_GROUPS, pn), lambda: (0, 0, 0))] * 2
                  + [pl.BlockSpec((T, N_GROUPS, pn), lambda: (0, 0, 0))] * 2 + [full2] * 4,
        out_shape=[jax.ShapeDtypeStruct((T + 1, N_GROUPS, pn), F32)] * 2
                  + [jax.ShapeDtypeStruct((T, N_GROUPS, pn), F32)] * 2
                  + [jax.ShapeDtypeStruct((N_GROUPS, pn), F32)] * 4,
        name="s5_params",
    )(*args)

    r5 = lambda a, t: a[:t].reshape(t, N_GROUPS, GROUP, STATE)
    lhs = jnp.concatenate([r5(clre, T), -r5(clim, T)], axis=-1)
    lhs = jnp.transpose(lhs, (1, 0, 2, 3)).reshape(N_GROUPS, T * GROUP, 2 * STATE)
    bb = lambda a: jnp.swapaxes(a.reshape(N_GROUPS, GROUP, STATE), 1, 2)
    rhs = jnp.concatenate([bb(bbre), bb(bbim)], axis=1)
    kmat = pl.pallas_call(
        _s5_kmat_kernel,
        grid=(N_GROUPS,),
        in_specs=[pl.BlockSpec((1, T * GROUP, 2 * STATE), lambda g: (g, 0, 0)),
                  pl.BlockSpec((1, 2 * STATE, GROUP), lambda g: (g, 0, 0))],
        out_specs=pl.BlockSpec((1, T * GROUP, GROUP), lambda g: (g, 0, 0)),
        out_shape=jax.ShapeDtypeStruct((N_GROUPS, T * GROUP, GROUP), F32),
        compiler_params=_cparams("arbitrary"),
        name="s5_kmat",
    )(lhs, rhs)

    eye = jnp.eye(S5_GL, dtype=F32)
    kt = kmat.reshape(S5_SG, S5_GL, T, GROUP, GROUP)
    kbd = jnp.einsum('sgtpq,gh->stgqhp', kt, eye).reshape(S5_SG, T, LANES, LANES)
    krev = kbd[:, ::-1].reshape(S5_SG, T * LANES, LANES).astype(BF16)
    krev = jnp.pad(krev, ((0, 0), (0, LANES), (0, 0)))
    r6 = lambda a: a.reshape(T, S5_SG, S5_GL, GROUP, STATE)
    wb = jnp.stack([r6(wbre), r6(wbim)], axis=-2)
    wb = jnp.einsum('ksgpin,gh->skgpihn', wb, eye).reshape(S5_SG, T * LANES, S5_NSTATE).astype(BF16)
    wc = jnp.stack([r6(clre[1:]), -r6(clim[1:])], axis=-2)
    wc = jnp.einsum('tsgpin,gh->signthp', wc, eye).reshape(S5_SG, S5_NSTATE, T * LANES).astype(BF16)
    lt = lambda a: a.reshape(N_GROUPS, GROUP, STATE)[:, 0].reshape(S5_SG, 1, S5_GL * STATE)
    return krev, wb, wc, lt(ltre), lt(ltim)


S5_TC = 256


def _s5_lane_block(sg):
    return pl.ds(pl.multiple_of(sg * LANES, LANES), LANES)


def _s5_chunk_inputs(x_ref, sg):
    return jnp.concatenate([x_ref[:, t, _s5_lane_block(sg)] for t in range(S5_T)], axis=1).astype(BF16)


def _s5_state_kernel(x_ref, wb_ref, e_ref):
    e_ref[0] = _dot(_s5_chunk_inputs(x_ref, pl.program_id(1)), wb_ref[0])


def _s5_scan_kernel(e_ref, ltre_ref, ltim_ref, xs_ref):
    lr, li = ltre_ref[0], ltim_ref[0]
    half = S5_NSTATE // 2

    def step(c, carry):
        new = []
        for b in range(BATCH):
            xr, xi = carry[b]
            row = b * S5_CH_PER_BATCH + c
            xs_ref[0, pl.ds(row, 1), :half] = xr
            xs_ref[0, pl.ds(row, 1), half:] = xi
            e = e_ref[0, pl.ds(row, 1), :]
            new.append((lr * xr - li * xi + e[:, :half], lr * xi + li * xr + e[:, half:]))
        return tuple(new)

    zero = jnp.zeros((1, half), F32)
    lax.fori_loop(0, S5_CH_PER_BATCH, step, tuple((zero, zero) for _ in range(BATCH)))


def _s5_out_kernel(x_ref, xs_ref, krev_ref, wc_ref, y_ref):
    sg = pl.program_id(1)
    x = _s5_chunk_inputs(x_ref, sg)
    xsb = xs_ref[0].astype(BF16)
    for t in range(0, S5_T, 2):
        n_in = (t + 2) * LANES
        first = (S5_T - 1 - t) * LANES
        taps = jnp.concatenate([krev_ref[0, first:first + n_in, :], krev_ref[0, first - LANES:first - LANES + n_in, :]],
                               axis=1)
        pair = _dot(x[:, :n_in], taps) + _dot(xsb, wc_ref[0, :, t * LANES:(t + 2) * LANES])
        y_ref[:, t, _s5_lane_block(sg)] = pair[:, :LANES]
        y_ref[:, t + 1, _s5_lane_block(sg)] = pair[:, LANES:]


def _s5(u, krev, wb, wc, ltre, ltim):
    T, tc = S5_T, S5_TC
    xn = u.reshape(S5_CH, T, SSM_WIDTH)
    grid = (S5_CH // tc, S5_SG)
    natural = pl.BlockSpec((tc, T, SSM_WIDTH), lambda i, s: (i, 0, 0))
    rows = lambda i, s: (s, i, 0)
    per_sg = lambda i, s: (s, 0, 0)
    e = pl.pallas_call(
        _s5_state_kernel, grid=grid,
        in_specs=[natural, pl.BlockSpec((1, T * LANES, S5_NSTATE), per_sg)],
        out_specs=pl.BlockSpec((1, tc, S5_NSTATE), rows),
        out_shape=jax.ShapeDtypeStruct((S5_SG, S5_CH, S5_NSTATE), F32),
        compiler_params=_cparams("arbitrary", "arbitrary"), name="s5_state",
    )(xn, wb)
    sg1 = lambda s: (s, 0, 0)
    xstart = pl.pallas_call(
        _s5_scan_kernel, grid=(S5_SG,),
        in_specs=[pl.BlockSpec((1, S5_CH, S5_NSTATE), sg1),
                  pl.BlockSpec((1, 1, S5_NSTATE // 2), sg1), pl.BlockSpec((1, 1, S5_NSTATE // 2), sg1)],
        out_specs=pl.BlockSpec((1, S5_CH, S5_NSTATE), sg1),
        out_shape=jax.ShapeDtypeStruct((S5_SG, S5_CH, S5_NSTATE), F32),
        compiler_params=_cparams("arbitrary"), name="s5_scan",
    )(e, ltre, ltim)
    y = pl.pallas_call(
        _s5_out_kernel, grid=grid,
        in_specs=[natural, pl.BlockSpec((1, tc, S5_NSTATE), rows),
                  pl.BlockSpec((1, (T + 1) * LANES, LANES), per_sg), pl.BlockSpec((1, S5_NSTATE, T * LANES), per_sg)],
        out_specs=natural,
        out_shape=jax.ShapeDtypeStruct((S5_CH, T, SSM_WIDTH), F32),
        compiler_params=_cparams("arbitrary", "arbitrary", vmem=56 * 1024 * 1024), name="s5_out",
    )(xn, xstart, krev, wc)
    return y.reshape(N_TOK, SSM_WIDTH)


MERGE_TM = 512


def _merge_kernel(ocmp_ref, osel_ref, owin_ref, gn_ref, yssm_ref, u_ref, ga_ref, gs_ref, x_ref, mod_ref,
                  eg_ref, dskip_ref, wglu_ref, bglu_ref, wua_ref, wus_ref, wout_ref, g2_ref,
                  wrhi_ref, wrlo_ref, wsgu_ref, wsd_ref,
                  xpart_ref, h2_ref, logit_ref):
    mod = mod_ref[0]
    gnb = gn_ref[...].astype(BF16)
    o_nsa = (_dot(gnb, eg_ref[0]) * ocmp_ref[...].astype(F32)
             + _dot(gnb, eg_ref[1]) * osel_ref[...].astype(F32)
             + _dot(gnb, eg_ref[2]) * owin_ref[...].astype(F32))
    attn = _dot(o_nsa.astype(BF16), wua_ref[...])
    z = _gelu(yssm_ref[...] + dskip_ref[...] * u_ref[...])
    y_ssm = z * jax.nn.sigmoid(_dot(z.astype(BF16), wglu_ref[...]) + bglu_ref[...])
    ssm = _dot(y_ssm.astype(BF16), wus_ref[...])
    merged = ga_ref[...].astype(F32) * attn + gs_ref[...].astype(F32) * ssm
    x1 = x_ref[...] + mod[2:3] * _dot(merged.astype(BF16), wout_ref[...])

    ms = jnp.mean(x1 * x1, axis=-1, keepdims=True)
    h2 = (x1 * lax.rsqrt(ms + EPS) * g2_ref[...]) * (1.0 + mod[4:5]) + mod[3:4]
    hi = h2.astype(BF16)
    lo = (h2 - hi.astype(F32)).astype(BF16)
    h2_ref[...] = _pack_bf16_pairs(h2)
    logit_ref[...] = _dot_nt(wrhi_ref[...], hi) + _dot_nt(wrhi_ref[...], lo) + _dot_nt(wrlo_ref[...], hi)
    gu = _dot(hi, wsgu_ref[...])
    shared = _dot((_silu(gu[:, :D_EXPERT]) * gu[:, D_EXPERT:]).astype(BF16), wsd_ref[...])
    xpart_ref[...] = x1 + mod[5:6] * shared


def _merge(ocmp, osel, owin, gn, yssm, u, ga, gs, x2, mod, d_skip, w_glu, b_glu, w_up_attn, w_up_ssm, w_out,
           g_norm2, w_router, ws_gate, ws_up, ws_down):
    tm = MERGE_TM
    eg = np.zeros((3, LANES, NSA_WIDTH), np.float32)
    for j in range(3):
        for h in range(N_HEADS):
            eg[j, 3 * h + j, h * HEAD_DIM:(h + 1) * HEAD_DIM] = 1.0
    wr_t = w_router.T
    wr_hi = wr_t.astype(BF16)
    wr_lo = (wr_t - wr_hi.astype(F32)).astype(BF16)
    row = lambda i: (i, 0)
    fix2 = lambda i: (0, 0)
    wspec = lambda a: pl.BlockSpec(a.shape, (lambda i: (0,) * a.ndim))
    weights = [jnp.asarray(eg, BF16), d_skip.reshape(1, -1), w_glu.astype(BF16), b_glu.reshape(1, -1),
               w_up_attn.astype(BF16), w_up_ssm.astype(BF16), w_out.astype(BF16), g_norm2.reshape(1, -1),
               wr_hi, wr_lo, jnp.concatenate([ws_gate, ws_up], axis=1).astype(BF16), ws_down.astype(BF16)]
    acts = [(ocmp, 512), (osel, 512), (owin, 512), (gn, 128), (yssm, 512), (u, 512), (ga, 1024), (gs, 1024),
            (x2, 1024)]
    return pl.pallas_call(
        _merge_kernel,
        grid=(N_TOK // tm,),
        in_specs=[pl.BlockSpec((tm, wd), row) for _, wd in acts]
                 + [pl.BlockSpec((1, 6, D_MODEL), lambda i: (i // (SEQ // tm), 0, 0))]
                 + [wspec(w) for w in weights],
        out_specs=[pl.BlockSpec((tm, D_MODEL), row), pl.BlockSpec((tm, HALF), row),
                   pl.BlockSpec((N_EXPERTS, tm), lambda i: (0, i))],
        out_shape=[jax.ShapeDtypeStruct((N_TOK, D_MODEL), F32), jax.ShapeDtypeStruct((N_TOK, HALF), I32),
                   jax.ShapeDtypeStruct((N_EXPERTS, N_TOK), F32)],
        compiler_params=_cparams("arbitrary", vmem=56 * 1024 * 1024),
        name="merge",
    )(*[a for a, _ in acts], mod, *weights)


ROUTE_TN = 1024


def _route_kernel(logit_ref, bias_ref, eidx_ref, w_ref, count_ref, gscore_ref, masked_ref):
    tn = ROUTE_TN

    @pl.when(pl.program_id(0) == 0)
    def _():
        count_ref[...] = jnp.zeros(count_ref.shape, F32)

    sc = jax.nn.sigmoid(logit_ref[...])
    biased = sc + bias_ref[...]
    gi = lax.broadcasted_iota(I32, (EXPERTS_PER_GROUP, tn), 0).astype(F32)
    for g in range(N_EXPERT_GROUPS):
        blk = biased[g * EXPERTS_PER_GROUP:(g + 1) * EXPERTS_PER_GROUP]
        m1 = jnp.max(blk, axis=0, keepdims=True)
        i1 = jnp.min(jnp.where(blk == m1, gi, float(EXPERTS_PER_GROUP)), axis=0, keepdims=True)
        m2 = jnp.max(jnp.where(gi == i1, -jnp.inf, blk), axis=0, keepdims=True)
        gscore_ref[g:g + 1, :] = m1 + m2
    gs = gscore_ref[...]
    gidx = lax.broadcasted_iota(I32, (N_EXPERT_GROUPS, tn), 0)
    grank = jnp.zeros((N_EXPERT_GROUPS, tn), F32)
    for gp in range(N_EXPERT_GROUPS):
        row = gs[gp:gp + 1, :]
        tie = jnp.where(gidx > gp, 1.0, 0.0)
        grank = grank + jnp.where(row > gs, 1.0, jnp.where(row == gs, tie, 0.0))
    for g in range(N_EXPERT_GROUPS):
        keep = grank[g:g + 1, :] < float(TOPK_GROUPS)
        sl = slice(g * EXPERTS_PER_GROUP, (g + 1) * EXPERTS_PER_GROUP)
        masked_ref[sl, :] = jnp.where(keep, biased[sl], -jnp.inf)
    cur = masked_ref[...]
    eidx = lax.broadcasted_iota(I32, (N_EXPERTS, tn), 0).astype(F32)
    wsum = jnp.zeros((1, tn), F32)
    hits = jnp.zeros((N_EXPERTS, tn), F32)
    for k in range(TOP_K):
        m = jnp.max(cur, axis=0, keepdims=True)
        idx = jnp.min(jnp.where(cur == m, eidx, float(N_EXPERTS)), axis=0, keepdims=True)
        hit = eidx == idx
        wk = jnp.sum(jnp.where(hit, sc, 0.0), axis=0, keepdims=True)
        cur = jnp.where(hit, -jnp.inf, cur)
        hits = hits + jnp.where(hit, 1.0, 0.0)
        eidx_ref[k:k + 1, :] = idx.astype(I32)
        w_ref[k:k + 1, :] = wk
        wsum = wsum + wk
    w_ref[...] = w_ref[...] / wsum * ROUTE_SCALE
    count_ref[...] = count_ref[...] + jnp.sum(hits, axis=1, keepdims=True)


def _route(logits_t, router_bias):
    tn = ROUTE_TN
    return pl.pallas_call(
        _route_kernel,
        grid=(N_TOK // tn,),
        in_specs=[pl.BlockSpec((N_EXPERTS, tn), lambda i: (0, i)), pl.BlockSpec((N_EXPERTS, 1), lambda i: (0, 0))],
        out_specs=[pl.BlockSpec((TOP_K, tn), lambda i: (0, i))] * 2 + [pl.BlockSpec((N_EXPERTS, 1), lambda i: (0, 0))],
        out_shape=[jax.ShapeDtypeStruct((TOP_K, N_TOK), I32), jax.ShapeDtypeStruct((TOP_K, N_TOK), F32),
                   jax.ShapeDtypeStruct((N_EXPERTS, 1), F32)],
        scratch_shapes=[pltpu.VMEM((N_EXPERT_GROUPS, tn), F32), pltpu.VMEM((N_EXPERTS, tn), F32)],
        compiler_params=_cparams("arbitrary"),
        name="route",
    )(logits_t, router_bias.reshape(-1, 1))


N_MOE_BLK = NK // DISPATCH_BLOCK
N_ITEMS = N_MOE_BLK + N_EXPERTS


def _dispatch_plan(eidx, counts):
    e_flat = eidx.reshape(-1)
    key = jnp.sort(e_flat * NK + jnp.arange(NK, dtype=I32))
    order = key & (NK - 1)
    counts = counts.reshape(-1).astype(I32)
    start = jnp.cumsum(counts) - counts
    cuts = jnp.sort(jnp.concatenate([jnp.arange(N_MOE_BLK, dtype=I32) * DISPATCH_BLOCK, start]))
    lo = cuts
    hi = jnp.concatenate([cuts[1:], jnp.full((1,), NK, I32)])
    blk = jnp.minimum(lo // DISPATCH_BLOCK, N_MOE_BLK - 1)
    expert = jnp.clip(jnp.sum((start[None, :] <= lo[:, None]).astype(I32), axis=1) - 1, 0, N_EXPERTS - 1)
    one = jnp.ones((1,), I32)
    first = jnp.concatenate([one, (blk[1:] != blk[:-1]).astype(I32)])
    last = jnp.concatenate([(blk[1:] != blk[:-1]).astype(I32), one])
    new_expert = jnp.concatenate([one, (expert[1:] != expert[:-1]).astype(I32)])
    run_id = jnp.cumsum(new_expert) - 1
    n_runs = run_id[-1] + 1
    item = jnp.arange(N_ITEMS, dtype=I32)
    run_first_item = jnp.sort(jnp.where(new_expert == 1, item, N_ITEMS))
    run_expert = expert[jnp.minimum(run_first_item, N_ITEMS - 1)]
    ahead = run_id + (WEIGHT_RING - 1)
    ahead_expert = run_expert[jnp.minimum(ahead, N_ITEMS - 1)]
    ahead_valid = (ahead < n_runs).astype(I32)
    second_expert = run_expert[1:2]
    prologue = jnp.concatenate([second_expert, (n_runs > 1).astype(I32).reshape(1)])
    tok = jnp.right_shift(order, TOP_K.bit_length() - 1)
    home = (order & (TOP_K - 1)) * N_TOK + tok
    return tok, home, (blk, expert, lo - blk * DISPATCH_BLOCK, hi - blk * DISPATCH_BLOCK, first, last, new_expert,
                      run_id % WEIGHT_RING, ahead_expert, ahead_valid, prologue)


SC_CORES = 2
SC_SUBCORES = 16
SC_CHUNK = 128


def _sc_move_rows(table, idx, scatter):
    n = idx.shape[0]
    workers = SC_CORES * SC_SUBCORES
    per_worker = n // workers
    n_chunks = per_worker // SC_CHUNK
    assert per_worker * workers == n and n_chunks * SC_CHUNK == per_worker
    mesh = plsc.VectorSubcoreMesh(core_axis_name="c", subcore_axis_name="s",
                                  num_cores=SC_CORES, num_subcores=SC_SUBCORES)

    def body(table_hbm, idx_hbm, out_hbm, idx_v, rows_v, sem):
        wid = lax.axis_index("s") * SC_CORES + lax.axis_index("c")
        base = wid * per_worker

        @pl.loop(0, n_chunks)
        def _(j):
            off = base + j * SC_CHUNK
            pltpu.sync_copy(idx_hbm.at[pl.ds(off, SC_CHUNK)], idx_v)
            if scatter:
                pltpu.sync_copy(table_hbm.at[pl.ds(off, SC_CHUNK)], rows_v)
                pltpu.async_copy(rows_v, out_hbm.at[idx_v], sem).wait()
            else:
                pltpu.async_copy(table_hbm.at[idx_v], rows_v, sem).wait()
                pltpu.sync_copy(rows_v, out_hbm.at[pl.ds(off, SC_CHUNK)])

    return pl.kernel(
        body,
        out_type=jax.ShapeDtypeStruct((n, table.shape[1]), table.dtype),
        mesh=mesh,
        scratch_types=[pltpu.VMEM((SC_CHUNK,), I32), pltpu.VMEM((SC_CHUNK, table.shape[1]), table.dtype),
                       pltpu.SemaphoreType.DMA],
        name="sc_scatter_rows" if scatter else "sc_gather_rows",
    )(table, idx)


WEIGHT_RING = 3
WEIGHT_CHUNKS = 4


def _expert_weight_copies(w_hbm, wbuf, sem, expert, slot):
    rows = w_hbm.shape[1] // WEIGHT_CHUNKS
    return [pltpu.make_async_copy(w_hbm.at[expert, pl.ds(c * rows, rows)],
                                  wbuf.at[slot, pl.ds(c * rows, rows)], sem.at[slot])
            for c in range(WEIGHT_CHUNKS)]


def _moe_kernel(blk_ref, exp_ref, lo_ref, hi_ref, first_ref, last_ref, newexp_ref,
                slot_ref, ahead_exp_ref, ahead_ok_ref, prologue_ref,
                x_ref, wg_hbm, wu_hbm, wd_hbm, y_ref,
                acc_ref, wgf_ref, wuf_ref, wdf_ref, wgb_ref, wub_ref, wdb_ref, wsem):
    it = pl.program_id(0)
    lo, hi = lo_ref[it], hi_ref[it]
    streams = ((wg_hbm, wgf_ref), (wu_hbm, wuf_ref), (wd_hbm, wdf_ref))

    def request(expert, slot):
        for w_hbm, wbuf in streams:
            for cp in _expert_weight_copies(w_hbm, wbuf, wsem, expert, slot):
                cp.start()

    @pl.when(it == 0)
    def _():
        request(exp_ref[0], 0)

        @pl.when(prologue_ref[1] == 1)
        def _():
            request(prologue_ref[0], 1)

    @pl.when(newexp_ref[it] == 1)
    def _():
        slot = slot_ref[it]
        for w_hbm, wbuf in streams:
            for cp in _expert_weight_copies(w_hbm, wbuf, wsem, 0, slot):
                cp.wait()
        wgb_ref[...] = wgf_ref[slot].astype(BF16)
        wub_ref[...] = wuf_ref[slot].astype(BF16)
        wdb_ref[...] = wdf_ref[slot].astype(BF16)

        @pl.when(ahead_ok_ref[it] == 1)
        def _():
            ahead_slot = slot + (WEIGHT_RING - 1)
            request(ahead_exp_ref[it], jnp.where(ahead_slot >= WEIGHT_RING, ahead_slot - WEIGHT_RING, ahead_slot))

    @pl.when(first_ref[it] == 1)
    def _():
        acc_ref[...] = jnp.zeros(acc_ref.shape, F32)

    def expert_pass(r0, nrows):
        rows = slice(r0, r0 + nrows)
        ridx = r0 + lax.broadcasted_iota(I32, (nrows, HALF), 0)
        mine = (ridx >= lo) & (ridx < hi)
        xlo, xhi = _unpack_bf16_pairs(jnp.where(mine, x_ref[rows, :], 0))
        xlo, xhi = xlo.astype(BF16), xhi.astype(BF16)
        gate = _dot(xlo, wgb_ref[:HALF]) + _dot(xhi, wgb_ref[HALF:])
        up = _dot(xlo, wub_ref[:HALF]) + _dot(xhi, wub_ref[HALF:])
        acc_ref[rows, :] = acc_ref[rows, :] + _dot((_silu(gate) * up).astype(BF16), wdb_ref[...])

    mid = DISPATCH_BLOCK // 2
    pl.when((lo < mid) & (hi > mid))(lambda: expert_pass(0, DISPATCH_BLOCK))
    pl.when((hi > lo) & (hi <= mid))(lambda: expert_pass(0, mid))
    pl.when((hi > lo) & (lo >= mid))(lambda: expert_pass(mid, mid))

    @pl.when(last_ref[it] == 1)
    def _():
        y_ref[...] = _pack_bf16_pairs(acc_ref[...])


def _moe(xs, items, w_gate, w_up, w_down):
    by_blk = lambda it, blk, *_: (blk[it], 0)
    any_space = pl.BlockSpec(memory_space=pl.ANY)
    grid_spec = pltpu.PrefetchScalarGridSpec(
        num_scalar_prefetch=len(items),
        grid=(N_ITEMS,),
        in_specs=[pl.BlockSpec((DISPATCH_BLOCK, HALF), by_blk), any_space, any_space, any_space],
        out_specs=pl.BlockSpec((DISPATCH_BLOCK, HALF), by_blk),
        scratch_shapes=[pltpu.VMEM((DISPATCH_BLOCK, D_MODEL), F32),
                        pltpu.VMEM((WEIGHT_RING, D_MODEL, D_EXPERT), F32),
                        pltpu.VMEM((WEIGHT_RING, D_MODEL, D_EXPERT), F32),
                        pltpu.VMEM((WEIGHT_RING, D_EXPERT, D_MODEL), F32),
                        pltpu.VMEM((D_MODEL, D_EXPERT), BF16), pltpu.VMEM((D_MODEL, D_EXPERT), BF16),
                        pltpu.VMEM((D_EXPERT, D_MODEL), BF16),
                        pltpu.SemaphoreType.DMA((WEIGHT_RING,))],
    )
    return pl.pallas_call(
        _moe_kernel,
        grid_spec=grid_spec,
        out_shape=jax.ShapeDtypeStruct((NK, HALF), I32),
        compiler_params=_cparams("arbitrary"),
        name="moe",
    )(*items, xs, w_gate, w_up, w_down)


COMB_TC = 512


def _combine_kernel(slots_ref, w_ref, xpart_ref, mod_ref, out_ref):
    w = w_ref[...]
    lo = jnp.zeros((w.shape[0], HALF), F32)
    hi = jnp.zeros((w.shape[0], HALF), F32)
    for k in range(TOP_K):
        klo, khi = _unpack_bf16_pairs(slots_ref[k])
        lo = lo + w[:, k:k + 1] * klo
        hi = hi + w[:, k:k + 1] * khi
    gate2 = mod_ref[0][5:6]
    out_ref[:, :HALF] = xpart_ref[:, :HALF] + gate2[:, :HALF] * lo
    out_ref[:, HALF:] = xpart_ref[:, HALF:] + gate2[:, HALF:] * hi


def _combine(xpart, mod, slots, w):
    tc = COMB_TC
    row = lambda i: (i, 0)
    return pl.pallas_call(
        _combine_kernel,
        grid=(N_TOK // tc,),
        in_specs=[pl.BlockSpec((TOP_K, tc, HALF), lambda i: (0, i, 0)),
                  pl.BlockSpec((tc, TOP_K), row),
                  pl.BlockSpec((tc, D_MODEL), row),
                  pl.BlockSpec((1, 6, D_MODEL), lambda i: (i // (SEQ // tc), 0, 0))],
        out_specs=pl.BlockSpec((tc, D_MODEL), row),
        out_shape=jax.ShapeDtypeStruct((N_TOK, D_MODEL), F32),
        compiler_params=_cparams("arbitrary"),
        name="combine",
    )(slots.reshape(TOP_K, N_TOK, HALF), w, xpart, mod)


def _layer(x, c, w_ada, b_ada, g_norm1, g_norm2, w_in, q_gain, kc_gain, ks_gain, kw_gain,
           pe_k, pe_v, w_cmp_k1, w_cmp_k2, w_cmp_v1, w_cmp_v2,
           a_re, a_im, log_dt, b_re, b_im, c_re, c_im, d_skip, w_glu, b_glu,
           w_up_attn, w_up_ssm, w_out, w_router, router_bias,
           w_gate, w_up, w_down, ws_gate, ws_up, ws_down):
    x2 = x.reshape(N_TOK, D_MODEL)
    mod = _ada(c, w_ada, b_ada)
    q, kc_raw, vc_raw, ks, kw, vst, vwt, gn, u, ga, gs = _proj(x2, mod, g_norm1, w_in, q_gain, ks_gain, kw_gain)
    kcn = _compress(kc_raw, pe_k, w_cmp_k1, w_cmp_k2, kc_gain, True)
    vcn = _compress(vc_raw, pe_v, w_cmp_v1, w_cmp_v2, kc_gain, False)
    ocmp, selb = _cmp_attn(q, kcn, vcn)
    osel, owin = _selwin(q, ks, kw, vst, vwt, selb)
    yssm = _s5(u, *_s5_params(a_re, a_im, log_dt, b_re, b_im, c_re, c_im))
    xpart, h2, logits_t = _merge(ocmp, osel, owin, gn, yssm, u, ga, gs, x2, mod, d_skip, w_glu, b_glu,
                                  w_up_attn, w_up_ssm, w_out, g_norm2, w_router, ws_gate, ws_up, ws_down)
    eidx_t, w_t, counts = _route(logits_t, router_bias)
    tok, home, items = _dispatch_plan(eidx_t.T, counts)
    y = _moe(_sc_move_rows(h2, tok, scatter=False), items, w_gate, w_up, w_down)
    slots = _sc_move_rows(y, home, scatter=True)
    return _combine(xpart, mod, slots, w_t.T).reshape(BATCH, SEQ, D_MODEL)


def kernel(x, c, w_ada, b_ada, g_norm1, g_norm2, w_in, q_gain, kc_gain, ks_gain, kw_gain, pe_k, pe_v, w_cmp_k1,
           w_cmp_k2, w_cmp_v1, w_cmp_v2, a_re, a_im, log_dt, b_re, b_im, c_re, c_im, d_skip, w_glu, b_glu,
           w_up_attn, w_up_ssm, w_out, w_router, router_bias, w_gate, w_up, w_down, ws_gate, ws_up, ws_down):
    params = (w_ada, b_ada, g_norm1, g_norm2, w_in, q_gain, kc_gain, ks_gain, kw_gain, pe_k, pe_v, w_cmp_k1,
              w_cmp_k2, w_cmp_v1, w_cmp_v2, a_re, a_im, log_dt, b_re, b_im, c_re, c_im, d_skip, w_glu, b_glu,
              w_up_attn, w_up_ssm, w_out, w_router, router_bias, w_gate, w_up, w_down, ws_gate, ws_up, ws_down)
    depth = w_ada.shape[0]
    for layer in range(depth):
        x = _layer(x, c, *[p[layer] for p in params])
    return x
```

```python
import functools

import jax
import jax.numpy as jnp
import numpy as np
from jax import lax
from jax.experimental import pallas as pl
from jax.experimental.pallas import tpu as pltpu
from jax.experimental.pallas import tpu_sc as plsc

F32 = jnp.float32
BF16 = jnp.bfloat16
I32 = jnp.int32
HIGHEST = lax.Precision.HIGHEST

D_MODEL = 1024
BATCH = 4
SEQ = 4096
N_TOK = BATCH * SEQ
N_HEADS = 8
HEAD_DIM = 64
N_KV = 2
CMP_BLOCK = 32
CMP_STRIDE = 16
CMP_HIDDEN = 256
N_CMP = 256
SEL_BLOCK = 64
N_SEL_BLOCKS = SEQ // SEL_BLOCK
N_SELECT = 16
WINDOW = 512
ATTN_SCALE = HEAD_DIM ** -0.5
LOG2E = 1.4426950408889634
NSA_WIDTH = N_HEADS * HEAD_DIM
SSM_WIDTH = 512
GROUP = 16
N_GROUPS = SSM_WIDTH // GROUP
STATE = 64
N_EXPERTS = 256
TOP_K = 8
D_EXPERT = 256
N_EXPERT_GROUPS = 8
EXPERTS_PER_GROUP = N_EXPERTS // N_EXPERT_GROUPS
TOPK_GROUPS = 4
ROUTE_SCALE = 2.5
DISPATCH_BLOCK = 512
EPS = 1e-6
NEG = -1e30

LANES = 128
S5_T = 16
S5_SG = 4
S5_GL = N_GROUPS // S5_SG
S5_CH = N_TOK // S5_T
S5_CH_PER_BATCH = SEQ // S5_T
S5_NSTATE = S5_GL * STATE * 2

NK = N_TOK * TOP_K
HALF = D_MODEL // 2

VMEM_LIMIT = 48 * 1024 * 1024


def _cparams(*sem, vmem=VMEM_LIMIT):
    return pltpu.CompilerParams(dimension_semantics=tuple(sem), vmem_limit_bytes=vmem)


def _dot(a, b):
    return jnp.dot(a, b, preferred_element_type=F32)


def _dot_nt(a, b):
    return lax.dot_general(a, b, (((1,), (1,)), ((), ())), preferred_element_type=F32)


def _split_dot(v, w):
    hi = v.astype(BF16)
    lo = (v - hi.astype(F32)).astype(BF16)
    return _dot(hi, w) + _dot(lo, w)


def _seg_rms(v, bd, gain):
    ss = _split_dot(v * v, bd)
    return v * lax.rsqrt(ss * (1.0 / HEAD_DIM) + EPS) * gain


def _gelu(x):
    return 0.5 * x * (1.0 + jnp.tanh(0.7978845608028654 * (x + 0.044715 * (x * x * x))))


def _silu(x):
    return x * jax.nn.sigmoid(x)


def _pack_bf16_pairs(v):
    h = v.shape[1] // 2
    return pltpu.pack_elementwise([v[:, :h], v[:, h:]], packed_dtype=BF16)


def _unpack_bf16_pairs(word):
    return (pltpu.unpack_elementwise(word, index=0, packed_dtype=BF16, unpacked_dtype=F32),
            pltpu.unpack_elementwise(word, index=1, packed_dtype=BF16, unpacked_dtype=F32))


def _ada_kernel(c_ref, w_ref, b_ref, o_ref):
    c = c_ref[...]
    o_ref[...] = jnp.dot(_silu(c), w_ref[...], preferred_element_type=F32, precision=HIGHEST) + b_ref[...]


def _ada(c, w_ada, b_ada):
    cp = jnp.pad(c, ((0, 8 - BATCH), (0, 0)))
    tn = 1536
    out = pl.pallas_call(
        _ada_kernel,
        grid=(6 * D_MODEL // tn,),
        in_specs=[pl.BlockSpec((8, D_MODEL), lambda j: (0, 0)),
                  pl.BlockSpec((D_MODEL, tn), lambda j: (0, j)),
                  pl.BlockSpec((1, tn), lambda j: (0, j))],
        out_specs=pl.BlockSpec((8, tn), lambda j: (0, j)),
        out_shape=jax.ShapeDtypeStruct((8, 6 * D_MODEL), F32),
        compiler_params=_cparams("arbitrary"),
        name="ada",
    )(cp, w_ada, b_ada.reshape(1, -1))
    return out.reshape(8, 6, D_MODEL)


_C_Q = 0
_C_KC = 512
_C_VC = 640
_C_KS = 768
_C_KW = 1024
_C_GN = 1280
_C_U = 1408
_C_GA = 1920
_C_GS = 2944
_C_END = 3968
PROJ_TM = 512


def _proj_kernel(x_ref, mod_ref, g1_ref, w_ref, wvt_ref, qg_ref, ksg_ref, kwg_ref, bd512_ref, bd256_ref,
                 q_ref, kc_ref, vc_ref, ks_ref, kw_ref, vst_ref, vwt_ref, gn_ref, u_ref, ga_ref, gs_ref):
    x = x_ref[...]
    ms = jnp.mean(x * x, axis=-1, keepdims=True)
    mod = mod_ref[0]
    h = (x * lax.rsqrt(ms + EPS) * g1_ref[...]) * (1.0 + mod[1:2]) + mod[0:1]
    hb = h.astype(BF16)

    def p(lo, hi):
        return _dot(hb, w_ref[:, lo:hi])

    q_ref[...] = _seg_rms(p(_C_Q, _C_KC), bd512_ref[...], qg_ref[...] * (ATTN_SCALE * LOG2E)).astype(BF16)
    kvc = p(_C_KC, _C_KS)
    kc_ref[...] = kvc[:, :LANES]
    vc_ref[...] = kvc[:, LANES:]
    ks_ref[...] = _seg_rms(p(_C_KS, _C_KW), bd256_ref[...], ksg_ref[...]).astype(BF16)
    kw_ref[...] = _seg_rms(p(_C_KW, _C_GN), bd256_ref[...], kwg_ref[...]).astype(BF16)
    vt = _dot_nt(wvt_ref[...], hb)
    vst_ref[...] = vt[:LANES].astype(BF16)
    vwt_ref[...] = vt[LANES:].astype(BF16)
    gnu = p(_C_GN, _C_GA)
    gn_ref[...] = jax.nn.sigmoid(gnu[:, :LANES])
    u_ref[...] = gnu[:, LANES:]
    ga_ref[...] = jax.nn.sigmoid(p(_C_GA, _C_GS)).astype(BF16)
    gs_ref[...] = jax.nn.sigmoid(p(_C_GS, _C_END)).astype(BF16)


def _dup_cols(w):
    return jnp.concatenate([w[:, :64], w[:, :64], w[:, 64:], w[:, 64:]], axis=1)


def _block_ones(n):
    return jnp.kron(jnp.eye(n // HEAD_DIM, dtype=F32), jnp.ones((HEAD_DIM, HEAD_DIM), F32)).astype(BF16)


def _proj(x2, mod, g_norm1, w_in, q_gain, ks_gain, kw_gain):
    o = np.cumsum((0, 512, 128, 128, 128, 128, 128, 128, 24, 512, 1024, 1024))
    parts = [w_in[:, o[i]:o[i + 1]] for i in range(11)]
    wq, wkc, wvc, wks, wvs, wkw, wvw, wgn, wu, wga, wgs = parts
    w = jnp.concatenate([wq, wkc, wvc, _dup_cols(wks), _dup_cols(wkw),
                         jnp.pad(wgn, ((0, 0), (0, LANES - 24))), wu, wga, wgs], axis=1).astype(BF16)
    wvt = jnp.concatenate([wvs, wvw], axis=1).T.astype(BF16)
    tm = PROJ_TM
    row = lambda i: (i, 0)
    col = lambda i: (0, i)
    fix = lambda i: (0, 0)
    outs = [(512, BF16, row), (128, F32, row), (128, F32, row), (256, BF16, row), (256, BF16, row),
            (LANES, BF16, col), (LANES, BF16, col),
            (128, F32, row), (512, F32, row), (1024, BF16, row), (1024, BF16, row)]
    ospec = lambda wd, m: pl.BlockSpec((tm, wd), m) if m is row else pl.BlockSpec((wd, tm), m)
    oshape = lambda wd, dt, m: jax.ShapeDtypeStruct((N_TOK, wd) if m is row else (wd, N_TOK), dt)
    return pl.pallas_call(
        _proj_kernel,
        grid=(N_TOK // tm,),
        in_specs=[pl.BlockSpec((tm, D_MODEL), row),
                  pl.BlockSpec((1, 6, D_MODEL), lambda i: (i // (SEQ // tm), 0, 0)),
                  pl.BlockSpec((1, D_MODEL), fix),
                  pl.BlockSpec((D_MODEL, _C_END), fix),
                  pl.BlockSpec((2 * LANES, D_MODEL), fix),
                  pl.BlockSpec((1, 512), fix), pl.BlockSpec((1, 256), fix), pl.BlockSpec((1, 256), fix),
                  pl.BlockSpec((512, 512), fix), pl.BlockSpec((256, 256), fix)],
        out_specs=[ospec(wd, m) for wd, _, m in outs],
        out_shape=[oshape(wd, dt, m) for wd, dt, m in outs],
        compiler_params=_cparams("arbitrary"),
        name="proj",
    )(x2, mod, g_norm1.reshape(1, -1), w, wvt,
      jnp.tile(q_gain, N_HEADS).reshape(1, -1), jnp.tile(ks_gain, 4).reshape(1, -1),
      jnp.tile(kw_gain, 4).reshape(1, -1), _block_ones(512), _block_ones(256))


def _compress_kernel(r_ref, pe_ref, w1_ref, w2_ref, bd_ref, gain_ref, o_ref, *, do_norm):
    r = jnp.concatenate([r_ref[0, :, l, :] for l in range(CMP_STRIDE)], axis=1)
    p0 = _dot((r + pe_ref[0]).astype(BF16), w1_ref[0])
    p1 = _dot((r + pe_ref[1]).astype(BF16), w1_ref[1])
    hid = p0 + pltpu.roll(p1, N_CMP - 1, 0)
    c = _dot(_gelu(hid).astype(BF16), w2_ref[...])
    if do_norm:
        c = _seg_rms(c, bd_ref[...], gain_ref[...])
    o_ref[0] = c.astype(BF16)


def _compress(raw, pe, w1, w2, gain, do_norm):
    r = raw.reshape(BATCH, SEQ // CMP_STRIDE, CMP_STRIDE, LANES)
    eye = jnp.eye(N_KV, dtype=F32)
    w1r = w1.reshape(2, CMP_STRIDE, HEAD_DIM, CMP_HIDDEN)
    w1big = jnp.einsum('hldc,gk->hlgdkc', w1r, eye).reshape(2, CMP_STRIDE * LANES, N_KV * CMP_HIDDEN).astype(BF16)
    w2big = jnp.einsum('cd,gk->gckd', w2, eye)
    w2big = jnp.concatenate([w2big, w2big], axis=-1).reshape(N_KV * CMP_HIDDEN, 4 * HEAD_DIM).astype(BF16)
    pe_big = jnp.broadcast_to(pe.reshape(2, CMP_STRIDE, 1, HEAD_DIM), (2, CMP_STRIDE, N_KV, HEAD_DIM))
    pe_big = pe_big.reshape(2, 1, CMP_STRIDE * LANES)
    fix2 = lambda b: (0, 0)
    fix3 = lambda b: (0, 0, 0)
    return pl.pallas_call(
        functools.partial(_compress_kernel, do_norm=do_norm),
        grid=(BATCH,),
        in_specs=[pl.BlockSpec((1, N_CMP, CMP_STRIDE, LANES), lambda b: (b, 0, 0, 0)),
                  pl.BlockSpec((2, 1, CMP_STRIDE * LANES), fix3),
                  pl.BlockSpec((2, CMP_STRIDE * LANES, N_KV * CMP_HIDDEN), fix3),
                  pl.BlockSpec((N_KV * CMP_HIDDEN, 256), fix2),
                  pl.BlockSpec((256, 256), fix2), pl.BlockSpec((1, 256), fix2)],
        out_specs=pl.BlockSpec((1, N_CMP, 256), lambda b: (b, 0, 0)),
        out_shape=jax.ShapeDtypeStruct((BATCH, N_CMP, 256), BF16),
        compiler_params=_cparams("arbitrary"),
        name="compress_k" if do_norm else "compress_v",
    )(r, pe_big, w1big, w2big, _block_ones(256), jnp.tile(gain, 4).reshape(1, -1))


ATT_TQ = 1024
RANK_CHUNK = 16


def _head_variants(qb):
    lane = lax.broadcasted_iota(I32, qb.shape, 1)
    z = jnp.zeros_like(qb)
    return jnp.where(lane < HEAD_DIM, qb, z), jnp.where(lane < HEAD_DIM, z, qb)


def _cmp_kernel(q_ref, kc_ref, vc_ref, ov_ref, o_ref, sel_ref, vrank_ref):
    tq = ATT_TQ
    qi = pl.program_id(1)
    tpos = qi * tq + lax.broadcasted_iota(I32, (tq, N_CMP), 0)
    nidx = lax.broadcasted_iota(I32, (tq, N_CMP), 1)
    mask = (CMP_STRIDE * nidx + (CMP_BLOCK - 1)) <= tpos
    lane_lo = lax.broadcasted_iota(I32, (tq, LANES), 1) < HEAD_DIM
    for g in range(N_KV):
        kd = kc_ref[0, :, g * LANES:(g + 1) * LANES]
        vd = vc_ref[0, :, g * LANES:(g + 1) * LANES]
        psum = jnp.zeros((tq, N_CMP), F32)
        for jb in range(2):
            blk = 2 * g + jb
            pv = []
            for qv in _head_variants(q_ref[:, blk * LANES:(blk + 1) * LANES]):
                s = jnp.where(mask, _dot_nt(qv, kd), NEG)
                m = jnp.max(s, axis=-1, keepdims=True)
                e = jnp.where(mask, jnp.exp2(s - m), 0.0)
                l = jnp.sum(e, axis=-1, keepdims=True)
                p = e / jnp.where(l > 0.0, l, 1.0)
                psum = psum + p
                pv.append(_dot(p.astype(BF16), vd))
            o_ref[:, blk * LANES:(blk + 1) * LANES] = jnp.where(lane_lo, pv[0], pv[1]).astype(BF16)
        imp = _split_dot(psum, ov_ref[...])
        imp_t = imp.T[:N_SEL_BLOCKS]
        j = lax.broadcasted_iota(I32, (N_SEL_BLOCKS, tq), 0)
        cur = jnp.right_shift(qi * tq + lax.broadcasted_iota(I32, (N_SEL_BLOCKS, tq), 1),
                              SEL_BLOCK.bit_length() - 1)
        forced = (j == 0) | (j == cur) | (j == cur - 1)
        v = jnp.where(forced, jnp.inf, jnp.where(j <= cur, imp_t, -jnp.inf))
        vrank_ref[...] = jnp.zeros((N_SEL_BLOCKS, tq), F32)
        n_live = (qi + 1) * (tq // SEL_BLOCK)
        for c0 in range(0, N_SEL_BLOCKS, RANK_CHUNK):
            @pl.when(c0 < n_live)
            def _():
                rank = vrank_ref[...]
                for jp in range(c0, c0 + RANK_CHUNK):
                    row = v[jp:jp + 1, :]
                    tie = jnp.where(j > jp, 1.0, 0.0)
                    rank = rank + jnp.where(row > v, 1.0, jnp.where(row == v, tie, 0.0))
                vrank_ref[...] = rank
        rank = vrank_ref[...]
        sel_ref[g * N_SEL_BLOCKS:(g + 1) * N_SEL_BLOCKS, :] = jnp.where(rank < float(N_SELECT), 0.0, NEG)


def _cmp_attn(q, kcn, vcn):
    nc = np.arange(N_CMP)
    sb = np.arange(LANES)
    ov = ((CMP_STRIDE * nc[:, None] < SEL_BLOCK * sb[None, :] + SEL_BLOCK)
          & (CMP_STRIDE * nc[:, None] + CMP_BLOCK > SEL_BLOCK * sb[None, :])
          & (nc[:, None] < N_CMP - 1) & (sb[None, :] < N_SEL_BLOCKS))
    ov = jnp.asarray(ov, BF16)
    tq = ATT_TQ
    nq = SEQ // tq
    row = lambda b, i: (b * nq + i, 0)
    return pl.pallas_call(
        _cmp_kernel,
        grid=(BATCH, nq),
        in_specs=[pl.BlockSpec((tq, NSA_WIDTH), row),
                  pl.BlockSpec((1, N_CMP, 256), lambda b, i: (b, 0, 0)),
                  pl.BlockSpec((1, N_CMP, 256), lambda b, i: (b, 0, 0)),
                  pl.BlockSpec((N_CMP, LANES), lambda b, i: (0, 0))],
        out_specs=[pl.BlockSpec((tq, NSA_WIDTH), row),
                   pl.BlockSpec((N_KV * N_SEL_BLOCKS, tq), lambda b, i: (0, b * nq + i))],
        out_shape=[jax.ShapeDtypeStruct((N_TOK, NSA_WIDTH), BF16),
                   jax.ShapeDtypeStruct((N_KV * N_SEL_BLOCKS, N_TOK), F32)],
        scratch_shapes=[pltpu.VMEM((N_SEL_BLOCKS, tq), F32)],
        compiler_params=_cparams("arbitrary", "arbitrary"),
        name="cmp_attn",
    )(q, kcn, vcn, ov)


M_INIT = -1e29


SUM_ROWS = 16
SW_TQ = 512
SEL_TK = 512


def _selwin_kernel(q_ref, ks_ref, kw_ref, vst_ref, vwt_ref, selb_ref, osel_ref, owin_ref, m_ref, acc_ref):
    tq = tk = SW_TQ
    qi = pl.program_id(1)
    krow = lax.broadcasted_iota(I32, (tk, tq), 0)
    qcol = lax.broadcasted_iota(I32, (tk, tq), 1)
    causal_bias = jnp.where(krow <= qcol, 0.0, NEG)
    far_bias = jnp.where(qcol < krow, 0.0, NEG)

    def reset():
        m_ref[...] = jnp.full(m_ref.shape, M_INIT, F32)
        acc_ref[...] = jnp.zeros(acc_ref.shape, F32)

    def update(g, k_ref, vt_ref, kt, bias, nk=tk):
        k0 = pl.multiple_of(kt * nk, nk)
        kd = k_ref[0, pl.ds(k0, nk), g * LANES:(g + 1) * LANES]
        vt = vt_ref[g * HEAD_DIM:(g + 1) * HEAD_DIM, pl.ds(k0, nk)]
        s = _dot_nt(kd, qvars[g])
        if bias is not None:
            s = s + jnp.concatenate([bias] * 4, axis=1)
        m_old = m_ref[g]
        m_new = jnp.maximum(m_old, jnp.max(s, axis=0, keepdims=True))
        alpha = jnp.exp2(m_old - m_new)
        p = jnp.exp2(s - m_new)
        m_ref[g] = m_new
        vte = jnp.concatenate([vt, jnp.ones((SUM_ROWS, nk), BF16)], axis=0)
        acc_ref[g] = alpha * acc_ref[g] + _dot(vte, p.astype(BF16))

    def finish(out_ref, g):
        o = acc_ref[g, :HEAD_DIM, :] / acc_ref[g, HEAD_DIM:HEAD_DIM + 1, :]
        for jb in range(2):
            blk = 2 * g + jb
            pair = jnp.concatenate([o[:, 2 * jb * tq:(2 * jb + 1) * tq], o[:, (2 * jb + 1) * tq:(2 * jb + 2) * tq]],
                                   axis=0)
            out_ref[:, blk * LANES:(blk + 1) * LANES] = pair.T.astype(BF16)

    def sel_bias(g, kt):
        rows = [jnp.broadcast_to(selb_ref[pl.ds(g * N_SEL_BLOCKS + kt * (SEL_TK // SEL_BLOCK) + r, 1), :],
                                 (SEL_BLOCK, tq)) for r in range(SEL_TK // SEL_BLOCK)]
        return jnp.concatenate(rows, axis=0)

    qvars = []
    for g in range(N_KV):
        heads = []
        for jb in range(2):
            heads.extend(_head_variants(q_ref[:, (2 * g + jb) * LANES:(2 * g + jb + 1) * LANES]))
        qvars.append(jnp.concatenate(heads, axis=0))
    groups = range(N_KV)

    reset()
    last_tile = qi // (SEL_TK // tq)

    def sel_step(kt, carry):
        for g in groups:
            update(g, ks_ref, vst_ref, kt, sel_bias(g, kt), SEL_TK)
        return carry

    lax.fori_loop(0, last_tile, sel_step, 0)
    q_first = qi * tq - last_tile * SEL_TK
    visible = (lax.broadcasted_iota(I32, (SEL_TK, tq), 0)
               <= lax.broadcasted_iota(I32, (SEL_TK, tq), 1) + q_first)
    diag_bias = jnp.where(visible, 0.0, NEG)
    for g in groups:
        update(g, ks_ref, vst_ref, last_tile, sel_bias(g, last_tile) + diag_bias, SEL_TK)
    for g in groups:
        finish(osel_ref, g)

    reset()
    back = WINDOW // tk

    @pl.when(qi >= back)
    def _():
        for g in groups:
            update(g, kw_ref, vwt_ref, qi - back, far_bias)

    for d in range(back - 1, 0, -1):
        @pl.when(qi >= d)
        def _():
            for g in groups:
                update(g, kw_ref, vwt_ref, qi - d, None)

    for g in groups:
        update(g, kw_ref, vwt_ref, qi, causal_bias)
    for g in groups:
        finish(owin_ref, g)


def _selwin(q, ks, kw, vst, vwt, selb):
    tq = SW_TQ
    nq = SEQ // tq
    assert WINDOW % SW_TQ == 0 and SEL_TK % SW_TQ == 0
    row = lambda b, i: (b * nq + i, 0)
    keys = pl.BlockSpec((1, SEQ, 256), lambda b, i: (b, 0, 0))
    vals = pl.BlockSpec((LANES, SEQ), lambda b, i: (0, b))
    r3 = lambda a: a.reshape(BATCH, SEQ, 256)
    return pl.pallas_call(
        _selwin_kernel,
        grid=(BATCH, nq),
        in_specs=[pl.BlockSpec((tq, NSA_WIDTH), row), keys, keys, vals, vals,
                  pl.BlockSpec((N_KV * N_SEL_BLOCKS, tq), lambda b, i: (0, b * nq + i))],
        out_specs=[pl.BlockSpec((tq, NSA_WIDTH), row)] * 2,
        out_shape=[jax.ShapeDtypeStruct((N_TOK, NSA_WIDTH), BF16)] * 2,
        scratch_shapes=[pltpu.VMEM((N_KV, 1, 4 * tq), F32),
                        pltpu.VMEM((N_KV, HEAD_DIM + SUM_ROWS, 4 * tq), F32)],
        compiler_params=_cparams("arbitrary", "arbitrary"),
        name="selwin",
    )(q, r3(ks), r3(kw), vst, vwt, selb)


def _s5_param_kernel(are_ref, aim_ref, ldt_ref, cre_ref, cim_ref, bre_ref, bim_ref,
                     clre_ref, clim_ref, wbre_ref, wbim_ref, bbre_ref, bbim_ref, ltre_ref, ltim_ref):
    are, aim = are_ref[...], aim_ref[...]
    dt = jnp.exp(ldt_ref[...])
    cre, cim = cre_ref[...], cim_ref[...]

    def lam_pow(tau):
        mag = jnp.exp(are * dt * float(tau))
        ang = aim * dt * float(tau)
        return mag * jnp.cos(ang), mag * jnp.sin(ang)

    lre, lim = lam_pow(1)
    den = are * are + aim * aim
    qre = ((lre - 1.0) * are + lim * aim) / den
    qim = (lim * are - (lre - 1.0) * aim) / den
    bre, bim = bre_ref[...], bim_ref[...]
    bbre = qre * bre - qim * bim
    bbim = qre * bim + qim * bre
    bbre_ref[...] = bbre
    bbim_ref[...] = bbim
    for tau in range(S5_T + 1):
        pr, pi = lam_pow(tau)
        clre_ref[tau] = cre * pr - cim * pi
        clim_ref[tau] = cre * pi + cim * pr
        if tau < S5_T:
            k = S5_T - 1 - tau
            wbre_ref[k] = pr * bbre - pi * bbim
            wbim_ref[k] = pr * bbim + pi * bbre
        else:
            ltre_ref[...] = pr
            ltim_ref[...] = pi


def _s5_kmat_kernel(l_ref, r_ref, o_ref):
    o_ref[0] = jnp.dot(l_ref[0], r_ref[0], preferred_element_type=F32, precision=HIGHEST)


def _s5_params(a_re, a_im, log_dt, b_re, b_im, c_re, c_im):
    T = S5_T
    pn = GROUP * STATE
    tile_p = lambda a: jnp.tile(a, (1, GROUP))
    args = (tile_p(a_re), tile_p(a_im), jnp.broadcast_to(log_dt[:, None], (N_GROUPS, pn)),
            c_re.reshape(N_GROUPS, pn), c_im.reshape(N_GROUPS, pn),
            jnp.swapaxes(b_re, 1, 2).reshape(N_GROUPS, pn), jnp.swapaxes(b_im, 1, 2).reshape(N_GROUPS, pn))
    full2 = pl.BlockSpec((N_GROUPS, pn), lambda: (0, 0))
    clre, clim, wbre, wbim, bbre, bbim, ltre, ltim = pl.pallas_call(
        _s5_param_kernel,
        in_specs=[full2] * 7,
        out_specs=[pl.BlockSpec((T + 1, N_GROUPS, pn), lambda: (0, 0, 0))] * 2
                  + [pl.BlockSpec((T, N_GROUPS, pn), lambda: (0, 0, 0))] * 2 + [full2] * 4,
        out_shape=[jax.ShapeDtypeStruct((T + 1, N_GROUPS, pn), F32)] * 2
                  + [jax.ShapeDtypeStruct((T, N_GROUPS, pn), F32)] * 2
                  + [jax.ShapeDtypeStruct((N_GROUPS, pn), F32)] * 4,
        name="s5_params",
    )(*args)

    r5 = lambda a, t: a[:t].reshape(t, N_GROUPS, GROUP, STATE)
    lhs = jnp.concatenate([r5(clre, T), -r5(clim, T)], axis=-1)
    lhs = jnp.transpose(lhs, (1, 0, 2, 3)).reshape(N_GROUPS, T * GROUP, 2 * STATE)
    bb = lambda a: jnp.swapaxes(a.reshape(N_GROUPS, GROUP, STATE), 1, 2)
    rhs = jnp.concatenate([bb(bbre), bb(bbim)], axis=1)
    kmat = pl.pallas_call(
        _s5_kmat_kernel,
        grid=(N_GROUPS,),
        in_specs=[pl.BlockSpec((1, T * GROUP, 2 * STATE), lambda g: (g, 0, 0)),
                  pl.BlockSpec((1, 2 * STATE, GROUP), lambda g: (g, 0, 0))],
        out_specs=pl.BlockSpec((1, T * GROUP, GROUP), lambda g: (g, 0, 0)),
        out_shape=jax.ShapeDtypeStruct((N_GROUPS, T * GROUP, GROUP), F32),
        compiler_params=_cparams("arbitrary"),
        name="s5_kmat",
    )(lhs, rhs)

    eye = jnp.eye(S5_GL, dtype=F32)
    kt = kmat.reshape(S5_SG, S5_GL, T, GROUP, GROUP)
    kbd = jnp.einsum('sgtpq,gh->stgqhp', kt, eye).reshape(S5_SG, T, LANES, LANES)
    krev = kbd[:, ::-1].reshape(S5_SG, T * LANES, LANES).astype(BF16)
    krev = jnp.pad(krev, ((0, 0), (0, LANES), (0, 0)))
    r6 = lambda a: a.reshape(T, S5_SG, S5_GL, GROUP, STATE)
    row_group = (jnp.arange(T * LANES, dtype=I32) // GROUP) % S5_GL

    def group_diagonal(re, im):
        a = jnp.stack([re, im], axis=-2)
        a = jnp.transpose(a, (1, 0, 2, 3, 4, 5)).reshape(S5_SG, T * LANES, 2 * STATE).astype(BF16)
        zero = jnp.zeros_like(a)
        return jnp.concatenate([jnp.where((row_group == h)[None, :, None], a, zero) for h in range(S5_GL)], axis=-1)

    wb = group_diagonal(r6(wbre), r6(wbim))
    wc = jnp.swapaxes(group_diagonal(r6(clre[1:]), -r6(clim[1:])), 1, 2)
    lt = lambda a: a.reshape(N_GROUPS, GROUP, STATE)[:, 0].reshape(S5_SG, S5_GL, 1, STATE)
    lam_re = jnp.broadcast_to(lt(ltre), (S5_SG, S5_GL, 2, STATE)).reshape(S5_SG, 1, S5_NSTATE)
    lam_im = jnp.concatenate([-lt(ltim), lt(ltim)], axis=2).reshape(S5_SG, 1, S5_NSTATE)
    return krev, wb, wc, lam_re, lam_im


S5_TC = 256


def _s5_lane_block(sg):
    return pl.ds(pl.multiple_of(sg * LANES, LANES), LANES)


def _s5_chunk_inputs(x_ref, sg):
    return jnp.concatenate([x_ref[:, t, _s5_lane_block(sg)] for t in range(S5_T)], axis=1).astype(BF16)


def _s5_state_kernel(x_ref, wb_ref, e_ref):
    e_ref[0] = _dot(_s5_chunk_inputs(x_ref, pl.program_id(1)), wb_ref[0])


def _s5_scan_kernel(e_ref, ltre_ref, ltim_ref, xs_ref):
    lr, li = ltre_ref[0], ltim_ref[0]
    im_lane = (lax.broadcasted_iota(I32, (1, S5_NSTATE), 1) & STATE) != 0

    def step(c, carry):
        new = []
        for b in range(BATCH):
            x = carry[b]
            row = b * S5_CH_PER_BATCH + c
            xs_ref[0, pl.ds(row, 1), :] = x
            swapped = jnp.where(im_lane, pltpu.roll(x, STATE, 1), pltpu.roll(x, S5_NSTATE - STATE, 1))
            new.append(lr * x + li * swapped + e_ref[0, pl.ds(row, 1), :])
        return tuple(new)

    zero = jnp.zeros((1, S5_NSTATE), F32)
    lax.fori_loop(0, S5_CH_PER_BATCH, step, tuple(zero for _ in range(BATCH)))


def _s5_out_kernel(x_ref, xs_ref, krev_ref, wc_ref, y_ref):
    sg = pl.program_id(1)
    x = _s5_chunk_inputs(x_ref, sg)
    xsb = xs_ref[0].astype(BF16)
    for t in range(0, S5_T, 2):
        n_in = (t + 2) * LANES
        first = (S5_T - 1 - t) * LANES
        taps = jnp.concatenate([krev_ref[0, first:first + n_in, :], krev_ref[0, first - LANES:first - LANES + n_in, :]],
                               axis=1)
        pair = _dot(x[:, :n_in], taps) + _dot(xsb, wc_ref[0, :, t * LANES:(t + 2) * LANES])
        y_ref[:, t, _s5_lane_block(sg)] = pair[:, :LANES]
        y_ref[:, t + 1, _s5_lane_block(sg)] = pair[:, LANES:]


def _s5(u, krev, wb, wc, ltre, ltim):
    T, tc = S5_T, S5_TC
    xn = u.reshape(S5_CH, T, SSM_WIDTH)
    grid = (S5_CH // tc, S5_SG)
    natural = pl.BlockSpec((tc, T, SSM_WIDTH), lambda i, s: (i, 0, 0))
    rows = lambda i, s: (s, i, 0)
    per_sg = lambda i, s: (s, 0, 0)
    e = pl.pallas_call(
        _s5_state_kernel, grid=grid,
        in_specs=[natural, pl.BlockSpec((1, T * LANES, S5_NSTATE), per_sg)],
        out_specs=pl.BlockSpec((1, tc, S5_NSTATE), rows),
        out_shape=jax.ShapeDtypeStruct((S5_SG, S5_CH, S5_NSTATE), F32),
        compiler_params=_cparams("arbitrary", "arbitrary"), name="s5_state",
    )(xn, wb)
    sg1 = lambda s: (s, 0, 0)
    xstart = pl.pallas_call(
        _s5_scan_kernel, grid=(S5_SG,),
        in_specs=[pl.BlockSpec((1, S5_CH, S5_NSTATE), sg1),
                  pl.BlockSpec((1, 1, S5_NSTATE), sg1), pl.BlockSpec((1, 1, S5_NSTATE), sg1)],
        out_specs=pl.BlockSpec((1, S5_CH, S5_NSTATE), sg1),
        out_shape=jax.ShapeDtypeStruct((S5_SG, S5_CH, S5_NSTATE), F32),
        compiler_params=_cparams("arbitrary"), name="s5_scan",
    )(e, ltre, ltim)
    y = pl.pallas_call(
        _s5_out_kernel, grid=grid,
        in_specs=[natural, pl.BlockSpec((1, tc, S5_NSTATE), rows),
                  pl.BlockSpec((1, (T + 1) * LANES, LANES), per_sg), pl.BlockSpec((1, S5_NSTATE, T * LANES), per_sg)],
        out_specs=natural,
        out_shape=jax.ShapeDtypeStruct((S5_CH, T, SSM_WIDTH), F32),
        compiler_params=_cparams("arbitrary", "arbitrary", vmem=56 * 1024 * 1024), name="s5_out",
    )(xn, xstart, krev, wc)
    return y.reshape(N_TOK, SSM_WIDTH)


MERGE_TM = 512


def _merge_kernel(ocmp_ref, osel_ref, owin_ref, gn_ref, yssm_ref, u_ref, ga_ref, gs_ref, x_ref, mod_ref,
                  eg_ref, dskip_ref, wglu_ref, bglu_ref, wua_ref, wus_ref, wout_ref, g2_ref,
                  wrhi_ref, wrlo_ref, wsgu_ref, wsd_ref,
                  xpart_ref, h2_ref, logit_ref):
    mod = mod_ref[0]
    gnb = gn_ref[...].astype(BF16)
    o_nsa = (_dot(gnb, eg_ref[0]) * ocmp_ref[...].astype(F32)
             + _dot(gnb, eg_ref[1]) * osel_ref[...].astype(F32)
             + _dot(gnb, eg_ref[2]) * owin_ref[...].astype(F32))
    attn = _dot(o_nsa.astype(BF16), wua_ref[...])
    z = _gelu(yssm_ref[...] + dskip_ref[...] * u_ref[...])
    y_ssm = z * jax.nn.sigmoid(_dot(z.astype(BF16), wglu_ref[...]) + bglu_ref[...])
    ssm = _dot(y_ssm.astype(BF16), wus_ref[...])
    merged = ga_ref[...].astype(F32) * attn + gs_ref[...].astype(F32) * ssm
    x1 = x_ref[...] + mod[2:3] * _dot(merged.astype(BF16), wout_ref[...])

    ms = jnp.mean(x1 * x1, axis=-1, keepdims=True)
    h2 = (x1 * lax.rsqrt(ms + EPS) * g2_ref[...]) * (1.0 + mod[4:5]) + mod[3:4]
    hi = h2.astype(BF16)
    lo = (h2 - hi.astype(F32)).astype(BF16)
    h2_ref[...] = _pack_bf16_pairs(h2)
    logit_ref[...] = _dot_nt(wrhi_ref[...], hi) + _dot_nt(wrhi_ref[...], lo) + _dot_nt(wrlo_ref[...], hi)
    gu = _dot(hi, wsgu_ref[...])
    shared = _dot((_silu(gu[:, :D_EXPERT]) * gu[:, D_EXPERT:]).astype(BF16), wsd_ref[...])
    xpart_ref[...] = x1 + mod[5:6] * shared


def _merge(ocmp, osel, owin, gn, yssm, u, ga, gs, x2, mod, d_skip, w_glu, b_glu, w_up_attn, w_up_ssm, w_out,
           g_norm2, w_router, ws_gate, ws_up, ws_down):
    tm = MERGE_TM
    eg = np.zeros((3, LANES, NSA_WIDTH), np.float32)
    for j in range(3):
        for h in range(N_HEADS):
            eg[j, 3 * h + j, h * HEAD_DIM:(h + 1) * HEAD_DIM] = 1.0
    wr_t = w_router.T
    wr_hi = wr_t.astype(BF16)
    wr_lo = (wr_t - wr_hi.astype(F32)).astype(BF16)
    row = lambda i: (i, 0)
    fix2 = lambda i: (0, 0)
    wspec = lambda a: pl.BlockSpec(a.shape, (lambda i: (0,) * a.ndim))
    weights = [jnp.asarray(eg, BF16), d_skip.reshape(1, -1), w_glu.astype(BF16), b_glu.reshape(1, -1),
               w_up_attn.astype(BF16), w_up_ssm.astype(BF16), w_out.astype(BF16), g_norm2.reshape(1, -1),
               wr_hi, wr_lo, jnp.concatenate([ws_gate, ws_up], axis=1).astype(BF16), ws_down.astype(BF16)]
    acts = [(ocmp, 512), (osel, 512), (owin, 512), (gn, 128), (yssm, 512), (u, 512), (ga, 1024), (gs, 1024),
            (x2, 1024)]
    return pl.pallas_call(
        _merge_kernel,
        grid=(N_TOK // tm,),
        in_specs=[pl.BlockSpec((tm, wd), row) for _, wd in acts]
                 + [pl.BlockSpec((1, 6, D_MODEL), lambda i: (i // (SEQ // tm), 0, 0))]
                 + [wspec(w) for w in weights],
        out_specs=[pl.BlockSpec((tm, D_MODEL), row), pl.BlockSpec((tm, HALF), row),
                   pl.BlockSpec((N_EXPERTS, tm), lambda i: (0, i))],
        out_shape=[jax.ShapeDtypeStruct((N_TOK, D_MODEL), F32), jax.ShapeDtypeStruct((N_TOK, HALF), I32),
                   jax.ShapeDtypeStruct((N_EXPERTS, N_TOK), F32)],
        compiler_params=_cparams("arbitrary", vmem=56 * 1024 * 1024),
        name="merge",
    )(*[a for a, _ in acts], mod, *weights)


ROUTE_TN = 1024


def _route_kernel(logit_ref, bias_ref, eidx_ref, w_ref, count_ref, gscore_ref, masked_ref):
    tn = ROUTE_TN

    @pl.when(pl.program_id(0) == 0)
    def _():
        count_ref[...] = jnp.zeros(count_ref.shape, F32)

    sc = jax.nn.sigmoid(logit_ref[...])
    biased = sc + bias_ref[...]
    gi = lax.broadcasted_iota(I32, (EXPERTS_PER_GROUP, tn), 0).astype(F32)
    for g in range(N_EXPERT_GROUPS):
        blk = biased[g * EXPERTS_PER_GROUP:(g + 1) * EXPERTS_PER_GROUP]
        m1 = jnp.max(blk, axis=0, keepdims=True)
        i1 = jnp.min(jnp.where(blk == m1, gi, float(EXPERTS_PER_GROUP)), axis=0, keepdims=True)
        m2 = jnp.max(jnp.where(gi == i1, -jnp.inf, blk), axis=0, keepdims=True)
        gscore_ref[g:g + 1, :] = m1 + m2
    gs = gscore_ref[...]
    gidx = lax.broadcasted_iota(I32, (N_EXPERT_GROUPS, tn), 0)
    grank = jnp.zeros((N_EXPERT_GROUPS, tn), F32)
    for gp in range(N_EXPERT_GROUPS):
        row = gs[gp:gp + 1, :]
        tie = jnp.where(gidx > gp, 1.0, 0.0)
        grank = grank + jnp.where(row > gs, 1.0, jnp.where(row == gs, tie, 0.0))
    for g in range(N_EXPERT_GROUPS):
        keep = grank[g:g + 1, :] < float(TOPK_GROUPS)
        sl = slice(g * EXPERTS_PER_GROUP, (g + 1) * EXPERTS_PER_GROUP)
        masked_ref[sl, :] = jnp.where(keep, biased[sl], -jnp.inf)
    cur = masked_ref[...]
    eidx = lax.broadcasted_iota(I32, (N_EXPERTS, tn), 0).astype(F32)
    wsum = jnp.zeros((1, tn), F32)
    hits = jnp.zeros((N_EXPERTS, tn), F32)
    for k in range(TOP_K):
        m = jnp.max(cur, axis=0, keepdims=True)
        idx = jnp.min(jnp.where(cur == m, eidx, float(N_EXPERTS)), axis=0, keepdims=True)
        hit = eidx == idx
        wk = jnp.sum(jnp.where(hit, sc, 0.0), axis=0, keepdims=True)
        cur = jnp.where(hit, -jnp.inf, cur)
        hits = hits + jnp.where(hit, 1.0, 0.0)
        eidx_ref[k:k + 1, :] = idx.astype(I32)
        w_ref[k:k + 1, :] = wk
        wsum = wsum + wk
    w_ref[...] = w_ref[...] / wsum * ROUTE_SCALE
    count_ref[...] = count_ref[...] + jnp.sum(hits, axis=1, keepdims=True)


def _route(logits_t, router_bias):
    tn = ROUTE_TN
    return pl.pallas_call(
        _route_kernel,
        grid=(N_TOK // tn,),
        in_specs=[pl.BlockSpec((N_EXPERTS, tn), lambda i: (0, i)), pl.BlockSpec((N_EXPERTS, 1), lambda i: (0, 0))],
        out_specs=[pl.BlockSpec((TOP_K, tn), lambda i: (0, i))] * 2 + [pl.BlockSpec((N_EXPERTS, 1), lambda i: (0, 0))],
        out_shape=[jax.ShapeDtypeStruct((TOP_K, N_TOK), I32), jax.ShapeDtypeStruct((TOP_K, N_TOK), F32),
                   jax.ShapeDtypeStruct((N_EXPERTS, 1), F32)],
        scratch_shapes=[pltpu.VMEM((N_EXPERT_GROUPS, tn), F32), pltpu.VMEM((N_EXPERTS, tn), F32)],
        compiler_params=_cparams("arbitrary"),
        name="route",
    )(logits_t, router_bias.reshape(-1, 1))


N_MOE_BLK = NK // DISPATCH_BLOCK
N_ITEMS = N_MOE_BLK + N_EXPERTS


def _dispatch_plan(eidx, counts):
    e_flat = eidx.reshape(-1)
    key = jnp.sort(e_flat * NK + jnp.arange(NK, dtype=I32))
    order = key & (NK - 1)
    counts = counts.reshape(-1).astype(I32)
    start = jnp.cumsum(counts) - counts
    cuts = jnp.sort(jnp.concatenate([jnp.arange(N_MOE_BLK, dtype=I32) * DISPATCH_BLOCK, start]))
    lo = cuts
    hi = jnp.concatenate([cuts[1:], jnp.full((1,), NK, I32)])
    blk = jnp.minimum(lo // DISPATCH_BLOCK, N_MOE_BLK - 1)
    expert = jnp.clip(jnp.sum((start[None, :] <= lo[:, None]).astype(I32), axis=1) - 1, 0, N_EXPERTS - 1)
    one = jnp.ones((1,), I32)
    first = jnp.concatenate([one, (blk[1:] != blk[:-1]).astype(I32)])
    last = jnp.concatenate([(blk[1:] != blk[:-1]).astype(I32), one])
    new_expert = jnp.concatenate([one, (expert[1:] != expert[:-1]).astype(I32)])
    run_id = jnp.cumsum(new_expert) - 1
    n_runs = run_id[-1] + 1
    item = jnp.arange(N_ITEMS, dtype=I32)
    run_first_item = jnp.sort(jnp.where(new_expert == 1, item, N_ITEMS))
    run_expert = expert[jnp.minimum(run_first_item, N_ITEMS - 1)]
    ahead = run_id + (WEIGHT_RING - 1)
    ahead_expert = run_expert[jnp.minimum(ahead, N_ITEMS - 1)]
    ahead_valid = (ahead < n_runs).astype(I32)
    second_expert = run_expert[1:2]
    prologue = jnp.concatenate([second_expert, (n_runs > 1).astype(I32).reshape(1)])
    tok = jnp.right_shift(order, TOP_K.bit_length() - 1)
    home = (order & (TOP_K - 1)) * N_TOK + tok
    return tok, home, (blk, expert, lo - blk * DISPATCH_BLOCK, hi - blk * DISPATCH_BLOCK, first, last, new_expert,
                      run_id % WEIGHT_RING, ahead_expert, ahead_valid, prologue)


SC_CORES = 2
SC_SUBCORES = 16
SC_CHUNK = 128


def _sc_move_rows(table, idx, scatter):
    n = idx.shape[0]
    workers = SC_CORES * SC_SUBCORES
    per_worker = n // workers
    n_chunks = per_worker // SC_CHUNK
    assert per_worker * workers == n and n_chunks * SC_CHUNK == per_worker
    mesh = plsc.VectorSubcoreMesh(core_axis_name="c", subcore_axis_name="s",
                                  num_cores=SC_CORES, num_subcores=SC_SUBCORES)

    def body(table_hbm, idx_hbm, out_hbm, idx_v, rows_v, sem):
        wid = lax.axis_index("s") * SC_CORES + lax.axis_index("c")
        base = wid * per_worker

        @pl.loop(0, n_chunks)
        def _(j):
            off = base + j * SC_CHUNK
            pltpu.sync_copy(idx_hbm.at[pl.ds(off, SC_CHUNK)], idx_v)
            if scatter:
                pltpu.sync_copy(table_hbm.at[pl.ds(off, SC_CHUNK)], rows_v)
                pltpu.async_copy(rows_v, out_hbm.at[idx_v], sem).wait()
            else:
                pltpu.async_copy(table_hbm.at[idx_v], rows_v, sem).wait()
                pltpu.sync_copy(rows_v, out_hbm.at[pl.ds(off, SC_CHUNK)])

    return pl.kernel(
        body,
        out_type=jax.ShapeDtypeStruct((n, table.shape[1]), table.dtype),
        mesh=mesh,
        scratch_types=[pltpu.VMEM((SC_CHUNK,), I32), pltpu.VMEM((SC_CHUNK, table.shape[1]), table.dtype),
                       pltpu.SemaphoreType.DMA],
        name="sc_scatter_rows" if scatter else "sc_gather_rows",
    )(table, idx)


WEIGHT_RING = 3
WEIGHT_CHUNKS = 4


def _expert_weight_copies(w_hbm, wbuf, sem, expert, slot):
    rows = w_hbm.shape[1] // WEIGHT_CHUNKS
    return [pltpu.make_async_copy(w_hbm.at[expert, pl.ds(c * rows, rows)],
                                  wbuf.at[slot, pl.ds(c * rows, rows)], sem.at[slot])
            for c in range(WEIGHT_CHUNKS)]


def _moe_kernel(blk_ref, exp_ref, lo_ref, hi_ref, first_ref, last_ref, newexp_ref,
                slot_ref, ahead_exp_ref, ahead_ok_ref, prologue_ref,
                x_ref, wg_hbm, wu_hbm, wd_hbm, y_ref,
                acc_ref, wgf_ref, wuf_ref, wdf_ref, wgb_ref, wub_ref, wdb_ref, wsem):
    it = pl.program_id(0)
    lo, hi = lo_ref[it], hi_ref[it]
    streams = ((wg_hbm, wgf_ref), (wu_hbm, wuf_ref), (wd_hbm, wdf_ref))

    def request(expert, slot):
        for w_hbm, wbuf in streams:
            for cp in _expert_weight_copies(w_hbm, wbuf, wsem, expert, slot):
                cp.start()

    @pl.when(it == 0)
    def _():
        request(exp_ref[0], 0)

        @pl.when(prologue_ref[1] == 1)
        def _():
            request(prologue_ref[0], 1)

    @pl.when(newexp_ref[it] == 1)
    def _():
        slot = slot_ref[it]
        for w_hbm, wbuf in streams:
            for cp in _expert_weight_copies(w_hbm, wbuf, wsem, 0, slot):
                cp.wait()
        wgb_ref[...] = wgf_ref[slot].astype(BF16)
        wub_ref[...] = wuf_ref[slot].astype(BF16)
        wdb_ref[...] = wdf_ref[slot].astype(BF16)

        @pl.when(ahead_ok_ref[it] == 1)
        def _():
            ahead_slot = slot + (WEIGHT_RING - 1)
            request(ahead_exp_ref[it], jnp.where(ahead_slot >= WEIGHT_RING, ahead_slot - WEIGHT_RING, ahead_slot))

    @pl.when(first_ref[it] == 1)
    def _():
        acc_ref[...] = jnp.zeros(acc_ref.shape, F32)

    def expert_pass(r0, nrows):
        rows = slice(r0, r0 + nrows)
        ridx = r0 + lax.broadcasted_iota(I32, (nrows, HALF), 0)
        mine = (ridx >= lo) & (ridx < hi)
        xlo, xhi = _unpack_bf16_pairs(jnp.where(mine, x_ref[rows, :], 0))
        xlo, xhi = xlo.astype(BF16), xhi.astype(BF16)
        gate = _dot(xlo, wgb_ref[:HALF]) + _dot(xhi, wgb_ref[HALF:])
        up = _dot(xlo, wub_ref[:HALF]) + _dot(xhi, wub_ref[HALF:])
        acc_ref[rows, :] = acc_ref[rows, :] + _dot((_silu(gate) * up).astype(BF16), wdb_ref[...])

    mid = DISPATCH_BLOCK // 2
    pl.when((lo < mid) & (hi > mid))(lambda: expert_pass(0, DISPATCH_BLOCK))
    pl.when((hi > lo) & (hi <= mid))(lambda: expert_pass(0, mid))
    pl.when((hi > lo) & (lo >= mid))(lambda: expert_pass(mid, mid))

    @pl.when(last_ref[it] == 1)
    def _():
        y_ref[...] = _pack_bf16_pairs(acc_ref[...])


def _moe(xs, items, w_gate, w_up, w_down):
    by_blk = lambda it, blk, *_: (blk[it], 0)
    any_space = pl.BlockSpec(memory_space=pl.ANY)
    grid_spec = pltpu.PrefetchScalarGridSpec(
        num_scalar_prefetch=len(items),
        grid=(N_ITEMS,),
        in_specs=[pl.BlockSpec((DISPATCH_BLOCK, HALF), by_blk), any_space, any_space, any_space],
        out_specs=pl.BlockSpec((DISPATCH_BLOCK, HALF), by_blk),
        scratch_shapes=[pltpu.VMEM((DISPATCH_BLOCK, D_MODEL), F32),
                        pltpu.VMEM((WEIGHT_RING, D_MODEL, D_EXPERT), F32),
                        pltpu.VMEM((WEIGHT_RING, D_MODEL, D_EXPERT), F32),
                        pltpu.VMEM((WEIGHT_RING, D_EXPERT, D_MODEL), F32),
                        pltpu.VMEM((D_MODEL, D_EXPERT), BF16), pltpu.VMEM((D_MODEL, D_EXPERT), BF16),
                        pltpu.VMEM((D_EXPERT, D_MODEL), BF16),
                        pltpu.SemaphoreType.DMA((WEIGHT_RING,))],
    )
    return pl.pallas_call(
        _moe_kernel,
        grid_spec=grid_spec,
        out_shape=jax.ShapeDtypeStruct((NK, HALF), I32),
        compiler_params=_cparams("arbitrary"),
        name="moe",
    )(*items, xs, w_gate, w_up, w_down)


COMB_TC = 512


def _combine_kernel(slots_ref, w_ref, xpart_ref, mod_ref, out_ref):
    w = w_ref[...]
    lo = jnp.zeros((w.shape[0], HALF), F32)
    hi = jnp.zeros((w.shape[0], HALF), F32)
    for k in range(TOP_K):
        klo, khi = _unpack_bf16_pairs(slots_ref[k])
        lo = lo + w[:, k:k + 1] * klo
        hi = hi + w[:, k:k + 1] * khi
    gate2 = mod_ref[0][5:6]
    out_ref[:, :HALF] = xpart_ref[:, :HALF] + gate2[:, :HALF] * lo
    out_ref[:, HALF:] = xpart_ref[:, HALF:] + gate2[:, HALF:] * hi


def _combine(xpart, mod, slots, w):
    tc = COMB_TC
    row = lambda i: (i, 0)
    return pl.pallas_call(
        _combine_kernel,
        grid=(N_TOK // tc,),
        in_specs=[pl.BlockSpec((TOP_K, tc, HALF), lambda i: (0, i, 0)),
                  pl.BlockSpec((tc, TOP_K), row),
                  pl.BlockSpec((tc, D_MODEL), row),
                  pl.BlockSpec((1, 6, D_MODEL), lambda i: (i // (SEQ // tc), 0, 0))],
        out_specs=pl.BlockSpec((tc, D_MODEL), row),
        out_shape=jax.ShapeDtypeStruct((N_TOK, D_MODEL), F32),
        compiler_params=_cparams("arbitrary"),
        name="combine",
    )(slots.reshape(TOP_K, N_TOK, HALF), w, xpart, mod)


def _layer(x, c, w_ada, b_ada, g_norm1, g_norm2, w_in, q_gain, kc_gain, ks_gain, kw_gain,
           pe_k, pe_v, w_cmp_k1, w_cmp_k2, w_cmp_v1, w_cmp_v2,
           a_re, a_im, log_dt, b_re, b_im, c_re, c_im, d_skip, w_glu, b_glu,
           w_up_attn, w_up_ssm, w_out, w_router, router_bias,
           w_gate, w_up, w_down, ws_gate, ws_up, ws_down):
    x2 = x.reshape(N_TOK, D_MODEL)
    mod = _ada(c, w_ada, b_ada)
    q, kc_raw, vc_raw, ks, kw, vst, vwt, gn, u, ga, gs = _proj(x2, mod, g_norm1, w_in, q_gain, ks_gain, kw_gain)
    kcn = _compress(kc_raw, pe_k, w_cmp_k1, w_cmp_k2, kc_gain, True)
    vcn = _compress(vc_raw, pe_v, w_cmp_v1, w_cmp_v2, kc_gain, False)
    ocmp, selb = _cmp_attn(q, kcn, vcn)
    osel, owin = _selwin(q, ks, kw, vst, vwt, selb)
    yssm = _s5(u, *_s5_params(a_re, a_im, log_dt, b_re, b_im, c_re, c_im))
    xpart, h2, logits_t = _merge(ocmp, osel, owin, gn, yssm, u, ga, gs, x2, mod, d_skip, w_glu, b_glu,
                                  w_up_attn, w_up_ssm, w_out, g_norm2, w_router, ws_gate, ws_up, ws_down)
    eidx_t, w_t, counts = _route(logits_t, router_bias)
    tok, home, items = _dispatch_plan(eidx_t.T, counts)
    y = _moe(_sc_move_rows(h2, tok, scatter=False), items, w_gate, w_up, w_down)
    slots = _sc_move_rows(y, home, scatter=True)
    return _combine(xpart, mod, slots, w_t.T).reshape(BATCH, SEQ, D_MODEL)


def kernel(x, c, w_ada, b_ada, g_norm1, g_norm2, w_in, q_gain, kc_gain, ks_gain, kw_gain, pe_k, pe_v, w_cmp_k1,
           w_cmp_k2, w_cmp_v1, w_cmp_v2, a_re, a_im, log_dt, b_re, b_im, c_re, c_im, d_skip, w_glu, b_glu,
           w_up_attn, w_up_ssm, w_out, w_router, router_bias, w_gate, w_up, w_down, ws_gate, ws_up, ws_down):
    params = (w_ada, b_ada, g_norm1, g_norm2, w_in, q_gain, kc_gain, ks_gain, kw_gain, pe_k, pe_v, w_cmp_k1,
              w_cmp_k2, w_cmp_v1, w_cmp_v2, a_re, a_im, log_dt, b_re, b_im, c_re, c_im, d_skip, w_glu, b_glu,
              w_up_attn, w_up_ssm, w_out, w_router, router_bias, w_gate, w_up, w_down, ws_gate, ws_up, ws_down)
    depth = w_ada.shape[0]
    for layer in range(depth):
        x = _layer(x, c, *[p[layer] for p in params])
    return x
```

```python
import functools

import jax
import jax.numpy as jnp
import numpy as np
from jax import lax
from jax.experimental import pallas as pl
from jax.experimental.pallas import tpu as pltpu
from jax.experimental.pallas import tpu_sc as plsc

F32 = jnp.float32
BF16 = jnp.bfloat16
I32 = jnp.int32
HIGHEST = lax.Precision.HIGHEST

D_MODEL = 1024
BATCH = 4
SEQ = 4096
N_TOK = BATCH * SEQ
N_HEADS = 8
HEAD_DIM = 64
N_KV = 2
CMP_BLOCK = 32
CMP_STRIDE = 16
CMP_HIDDEN = 256
N_CMP = 256
SEL_BLOCK = 64
N_SEL_BLOCKS = SEQ // SEL_BLOCK
N_SELECT = 16
WINDOW = 512
ATTN_SCALE = HEAD_DIM ** -0.5
LOG2E = 1.4426950408889634
NSA_WIDTH = N_HEADS * HEAD_DIM
SSM_WIDTH = 512
GROUP = 16
N_GROUPS = SSM_WIDTH // GROUP
STATE = 64
N_EXPERTS = 256
TOP_K = 8
D_EXPERT = 256
N_EXPERT_GROUPS = 8
EXPERTS_PER_GROUP = N_EXPERTS // N_EXPERT_GROUPS
TOPK_GROUPS = 4
ROUTE_SCALE = 2.5
DISPATCH_BLOCK = 512
EPS = 1e-6
NEG = -1e30

LANES = 128
S5_T = 16
S5_SG = 4
S5_GL = N_GROUPS // S5_SG
S5_CH = N_TOK // S5_T
S5_CH_PER_BATCH = SEQ // S5_T
S5_NSTATE = S5_GL * STATE * 2

NK = N_TOK * TOP_K
HALF = D_MODEL // 2

VMEM_LIMIT = 48 * 1024 * 1024


def _cparams(*sem, vmem=VMEM_LIMIT):
    return pltpu.CompilerParams(dimension_semantics=tuple(sem), vmem_limit_bytes=vmem)


def _dot(a, b):
    return jnp.dot(a, b, preferred_element_type=F32)


def _dot_nt(a, b):
    return lax.dot_general(a, b, (((1,), (1,)), ((), ())), preferred_element_type=F32)


def _split_dot(v, w):
    hi = v.astype(BF16)
    lo = (v - hi.astype(F32)).astype(BF16)
    return _dot(hi, w) + _dot(lo, w)


def _seg_rms(v, bd, gain):
    ss = _split_dot(v * v, bd)
    return v * lax.rsqrt(ss * (1.0 / HEAD_DIM) + EPS) * gain


def _gelu(x):
    return 0.5 * x * (1.0 + jnp.tanh(0.7978845608028654 * (x + 0.044715 * (x * x * x))))


def _silu(x):
    return x * jax.nn.sigmoid(x)


def _pack_bf16_pairs(v):
    h = v.shape[1] // 2
    return pltpu.pack_elementwise([v[:, :h], v[:, h:]], packed_dtype=BF16)


def _unpack_bf16_pairs(word):
    return (pltpu.unpack_elementwise(word, index=0, packed_dtype=BF16, unpacked_dtype=F32),
            pltpu.unpack_elementwise(word, index=1, packed_dtype=BF16, unpacked_dtype=F32))


def _ada_kernel(c_ref, w_ref, b_ref, o_ref):
    c = c_ref[...]
    o_ref[...] = jnp.dot(_silu(c), w_ref[...], preferred_element_type=F32, precision=HIGHEST) + b_ref[...]


def _ada(c, w_ada, b_ada):
    cp = jnp.pad(c, ((0, 8 - BATCH), (0, 0)))
    tn = 1536
    out = pl.pallas_call(
        _ada_kernel,
        grid=(6 * D_MODEL // tn,),
        in_specs=[pl.BlockSpec((8, D_MODEL), lambda j: (0, 0)),
                  pl.BlockSpec((D_MODEL, tn), lambda j: (0, j)),
                  pl.BlockSpec((1, tn), lambda j: (0, j))],
        out_specs=pl.BlockSpec((8, tn), lambda j: (0, j)),
        out_shape=jax.ShapeDtypeStruct((8, 6 * D_MODEL), F32),
        compiler_params=_cparams("arbitrary"),
        name="ada",
    )(cp, w_ada, b_ada.reshape(1, -1))
    return out.reshape(8, 6, D_MODEL)


_C_Q = 0
_C_KC = 512
_C_VC = 640
_C_KS = 768
_C_KW = 1024
_C_GN = 1280
_C_U = 1408
_C_GA = 1920
_C_GS = 2944
_C_END = 3968
PROJ_TM = 512


def _proj_kernel(x_ref, mod_ref, g1_ref, w_ref, wvt_ref, qg_ref, ksg_ref, kwg_ref, bd512_ref, bd256_ref,
                 q_ref, kc_ref, vc_ref, ks_ref, kw_ref, vst_ref, vwt_ref, gn_ref, u_ref, ga_ref, gs_ref):
    x = x_ref[...]
    ms = jnp.mean(x * x, axis=-1, keepdims=True)
    mod = mod_ref[0]
    h = (x * lax.rsqrt(ms + EPS) * g1_ref[...]) * (1.0 + mod[1:2]) + mod[0:1]
    hb = h.astype(BF16)

    def p(lo, hi):
        return _dot(hb, w_ref[:, lo:hi])

    q_ref[...] = _seg_rms(p(_C_Q, _C_KC), bd512_ref[...], qg_ref[...] * (ATTN_SCALE * LOG2E)).astype(BF16)
    kvc = p(_C_KC, _C_KS)
    kc_ref[...] = kvc[:, :LANES]
    vc_ref[...] = kvc[:, LANES:]
    ks_ref[...] = _seg_rms(p(_C_KS, _C_KW), bd256_ref[...], ksg_ref[...]).astype(BF16)
    kw_ref[...] = _seg_rms(p(_C_KW, _C_GN), bd256_ref[...], kwg_ref[...]).astype(BF16)
    vt = _dot_nt(wvt_ref[...], hb)
    vst_ref[...] = vt[:LANES].astype(BF16)
    vwt_ref[...] = vt[LANES:].astype(BF16)
    gnu = p(_C_GN, _C_GA)
    gn_ref[...] = jax.nn.sigmoid(gnu[:, :LANES])
    u_ref[...] = gnu[:, LANES:]
    ga_ref[...] = jax.nn.sigmoid(p(_C_GA, _C_GS)).astype(BF16)
    gs_ref[...] = jax.nn.sigmoid(p(_C_GS, _C_END)).astype(BF16)


def _dup_cols(w):
    return jnp.concatenate([w[:, :64], w[:, :64], w[:, 64:], w[:, 64:]], axis=1)


def _block_ones(n):
    return jnp.kron(jnp.eye(n // HEAD_DIM, dtype=F32), jnp.ones((HEAD_DIM, HEAD_DIM), F32)).astype(BF16)


def _proj(x2, mod, g_norm1, w_in, q_gain, ks_gain, kw_gain):
    o = np.cumsum((0, 512, 128, 128, 128, 128, 128, 128, 24, 512, 1024, 1024))
    parts = [w_in[:, o[i]:o[i + 1]] for i in range(11)]
    wq, wkc, wvc, wks, wvs, wkw, wvw, wgn, wu, wga, wgs = parts
    w = jnp.concatenate([wq, wkc, wvc, _dup_cols(wks), _dup_cols(wkw),
                         jnp.pad(wgn, ((0, 0), (0, LANES - 24))), wu, wga, wgs], axis=1).astype(BF16)
    wvt = jnp.concatenate([wvs, wvw], axis=1).T.astype(BF16)
    tm = PROJ_TM
    row = lambda i: (i, 0)
    col = lambda i: (0, i)
    fix = lambda i: (0, 0)
    outs = [(512, BF16, row), (128, F32, row), (128, F32, row), (256, BF16, row), (256, BF16, row),
            (LANES, BF16, col), (LANES, BF16, col),
            (128, F32, row), (512, F32, row), (1024, BF16, row), (1024, BF16, row)]
    ospec = lambda wd, m: pl.BlockSpec((tm, wd), m) if m is row else pl.BlockSpec((wd, tm), m)
    oshape = lambda wd, dt, m: jax.ShapeDtypeStruct((N_TOK, wd) if m is row else (wd, N_TOK), dt)
    return pl.pallas_call(
        _proj_kernel,
        grid=(N_TOK // tm,),
        in_specs=[pl.BlockSpec((tm, D_MODEL), row),
                  pl.BlockSpec((1, 6, D_MODEL), lambda i: (i // (SEQ // tm), 0, 0)),
                  pl.BlockSpec((1, D_MODEL), fix),
                  pl.BlockSpec((D_MODEL, _C_END), fix),
                  pl.BlockSpec((2 * LANES, D_MODEL), fix),
                  pl.BlockSpec((1, 512), fix), pl.BlockSpec((1, 256), fix), pl.BlockSpec((1, 256), fix),
                  pl.BlockSpec((512, 512), fix), pl.BlockSpec((256, 256), fix)],
        out_specs=[ospec(wd, m) for wd, _, m in outs],
        out_shape=[oshape(wd, dt, m) for wd, dt, m in outs],
        compiler_params=_cparams("arbitrary"),
        name="proj",
    )(x2, mod, g_norm1.reshape(1, -1), w, wvt,
      jnp.tile(q_gain, N_HEADS).reshape(1, -1), jnp.tile(ks_gain, 4).reshape(1, -1),
      jnp.tile(kw_gain, 4).reshape(1, -1), _block_ones(512), _block_ones(256))


def _compress_kernel(r_ref, pe_ref, w1_ref, w2_ref, bd_ref, gain_ref, o_ref, *, do_norm):
    r = jnp.concatenate([r_ref[0, :, l, :] for l in range(CMP_STRIDE)], axis=1)
    p0 = _dot((r + pe_ref[0]).astype(BF16), w1_ref[0])
    p1 = _dot((r + pe_ref[1]).astype(BF16), w1_ref[1])
    hid = p0 + pltpu.roll(p1, N_CMP - 1, 0)
    c = _dot(_gelu(hid).astype(BF16), w2_ref[...])
    if do_norm:
        c = _seg_rms(c, bd_ref[...], gain_ref[...])
    o_ref[0] = c.astype(BF16)


def _compress(raw, pe, w1, w2, gain, do_norm):
    r = raw.reshape(BATCH, SEQ // CMP_STRIDE, CMP_STRIDE, LANES)
    eye = jnp.eye(N_KV, dtype=F32)
    w1r = w1.reshape(2, CMP_STRIDE, HEAD_DIM, CMP_HIDDEN)
    w1big = jnp.einsum('hldc,gk->hlgdkc', w1r, eye).reshape(2, CMP_STRIDE * LANES, N_KV * CMP_HIDDEN).astype(BF16)
    w2big = jnp.einsum('cd,gk->gckd', w2, eye)
    w2big = jnp.concatenate([w2big, w2big], axis=-1).reshape(N_KV * CMP_HIDDEN, 4 * HEAD_DIM).astype(BF16)
    pe_big = jnp.broadcast_to(pe.reshape(2, CMP_STRIDE, 1, HEAD_DIM), (2, CMP_STRIDE, N_KV, HEAD_DIM))
    pe_big = pe_big.reshape(2, 1, CMP_STRIDE * LANES)
    fix2 = lambda b: (0, 0)
    fix3 = lambda b: (0, 0, 0)
    return pl.pallas_call(
        functools.partial(_compress_kernel, do_norm=do_norm),
        grid=(BATCH,),
        in_specs=[pl.BlockSpec((1, N_CMP, CMP_STRIDE, LANES), lambda b: (b, 0, 0, 0)),
                  pl.BlockSpec((2, 1, CMP_STRIDE * LANES), fix3),
                  pl.BlockSpec((2, CMP_STRIDE * LANES, N_KV * CMP_HIDDEN), fix3),
                  pl.BlockSpec((N_KV * CMP_HIDDEN, 256), fix2),
                  pl.BlockSpec((256, 256), fix2), pl.BlockSpec((1, 256), fix2)],
        out_specs=pl.BlockSpec((1, N_CMP, 256), lambda b: (b, 0, 0)),
        out_shape=jax.ShapeDtypeStruct((BATCH, N_CMP, 256), BF16),
        compiler_params=_cparams("arbitrary"),
        name="compress_k" if do_norm else "compress_v",
    )(r, pe_big, w1big, w2big, _block_ones(256), jnp.tile(gain, 4).reshape(1, -1))


ATT_TQ = 1024
RANK_CHUNK = 16


def _head_variants(qb):
    lane = lax.broadcasted_iota(I32, qb.shape, 1)
    z = jnp.zeros_like(qb)
    return jnp.where(lane < HEAD_DIM, qb, z), jnp.where(lane < HEAD_DIM, z, qb)


def _cmp_kernel(q_ref, kc_ref, vc_ref, ov_ref, o_ref, sel_ref, vrank_ref):
    tq = ATT_TQ
    qi = pl.program_id(1)
    tpos = qi * tq + lax.broadcasted_iota(I32, (tq, N_CMP), 0)
    nidx = lax.broadcasted_iota(I32, (tq, N_CMP), 1)
    mask = (CMP_STRIDE * nidx + (CMP_BLOCK - 1)) <= tpos
    lane_lo = lax.broadcasted_iota(I32, (tq, LANES), 1) < HEAD_DIM
    for g in range(N_KV):
        kd = kc_ref[0, :, g * LANES:(g + 1) * LANES]
        vd = vc_ref[0, :, g * LANES:(g + 1) * LANES]
        psum = jnp.zeros((tq, N_CMP), F32)
        for jb in range(2):
            blk = 2 * g + jb
            pv = []
            for qv in _head_variants(q_ref[:, blk * LANES:(blk + 1) * LANES]):
                s = jnp.where(mask, _dot_nt(qv, kd), NEG)
                m = jnp.max(s, axis=-1, keepdims=True)
                e = jnp.where(mask, jnp.exp2(s - m), 0.0)
                l = jnp.sum(e, axis=-1, keepdims=True)
                p = e / jnp.where(l > 0.0, l, 1.0)
                psum = psum + p
                pv.append(_dot(p.astype(BF16), vd))
            o_ref[:, blk * LANES:(blk + 1) * LANES] = jnp.where(lane_lo, pv[0], pv[1]).astype(BF16)
        imp = _split_dot(psum, ov_ref[...])
        imp_t = imp.T[:N_SEL_BLOCKS]
        j = lax.broadcasted_iota(I32, (N_SEL_BLOCKS, tq), 0)
        cur = jnp.right_shift(qi * tq + lax.broadcasted_iota(I32, (N_SEL_BLOCKS, tq), 1),
                              SEL_BLOCK.bit_length() - 1)
        forced = (j == 0) | (j == cur) | (j == cur - 1)
        v = jnp.where(forced, jnp.inf, jnp.where(j <= cur, imp_t, -jnp.inf))
        vrank_ref[...] = jnp.zeros((N_SEL_BLOCKS, tq), F32)
        n_live = (qi + 1) * (tq // SEL_BLOCK)
        for c0 in range(0, N_SEL_BLOCKS, RANK_CHUNK):
            @pl.when(c0 < n_live)
            def _():
                rank = vrank_ref[...]
                for jp in range(c0, c0 + RANK_CHUNK):
                    row = v[jp:jp + 1, :]
                    tie = jnp.where(j > jp, 1.0, 0.0)
                    rank = rank + jnp.where(row > v, 1.0, jnp.where(row == v, tie, 0.0))
                vrank_ref[...] = rank
        rank = vrank_ref[...]
        sel_ref[g * N_SEL_BLOCKS:(g + 1) * N_SEL_BLOCKS, :] = jnp.where(rank < float(N_SELECT), 0.0, NEG)


def _cmp_attn(q, kcn, vcn):
    nc = np.arange(N_CMP)
    sb = np.arange(LANES)
    ov = ((CMP_STRIDE * nc[:, None] < SEL_BLOCK * sb[None, :] + SEL_BLOCK)
          & (CMP_STRIDE * nc[:, None] + CMP_BLOCK > SEL_BLOCK * sb[None, :])
          & (nc[:, None] < N_CMP - 1) & (sb[None, :] < N_SEL_BLOCKS))
    ov = jnp.asarray(ov, BF16)
    tq = ATT_TQ
    nq = SEQ // tq
    row = lambda b, i: (b * nq + i, 0)
    return pl.pallas_call(
        _cmp_kernel,
        grid=(BATCH, nq),
        in_specs=[pl.BlockSpec((tq, NSA_WIDTH), row),
                  pl.BlockSpec((1, N_CMP, 256), lambda b, i: (b, 0, 0)),
                  pl.BlockSpec((1, N_CMP, 256), lambda b, i: (b, 0, 0)),
                  pl.BlockSpec((N_CMP, LANES), lambda b, i: (0, 0))],
        out_specs=[pl.BlockSpec((tq, NSA_WIDTH), row),
                   pl.BlockSpec((N_KV * N_SEL_BLOCKS, tq), lambda b, i: (0, b * nq + i))],
        out_shape=[jax.ShapeDtypeStruct((N_TOK, NSA_WIDTH), BF16),
                   jax.ShapeDtypeStruct((N_KV * N_SEL_BLOCKS, N_TOK), F32)],
        scratch_shapes=[pltpu.VMEM((N_SEL_BLOCKS, tq), F32)],
        compiler_params=_cparams("arbitrary", "arbitrary"),
        name="cmp_attn",
    )(q, kcn, vcn, ov)


M_INIT = -1e29


SUM_ROWS = 16
SW_TQ = 512
SEL_TK = 512


def _selwin_kernel(q_ref, ks_ref, kw_ref, vst_ref, vwt_ref, selb_ref, osel_ref, owin_ref, m_ref, acc_ref):
    tq = tk = SW_TQ
    qi = pl.program_id(1)
    krow = lax.broadcasted_iota(I32, (tk, tq), 0)
    qcol = lax.broadcasted_iota(I32, (tk, tq), 1)
    causal_bias = jnp.where(krow <= qcol, 0.0, NEG)
    far_bias = jnp.where(qcol < krow, 0.0, NEG)

    def reset():
        m_ref[...] = jnp.full(m_ref.shape, M_INIT, F32)
        acc_ref[...] = jnp.zeros(acc_ref.shape, F32)

    def update(g, k_ref, vt_ref, kt, bias, nk=tk):
        k0 = pl.multiple_of(kt * nk, nk)
        kd = k_ref[0, pl.ds(k0, nk), g * LANES:(g + 1) * LANES]
        vt = vt_ref[g * HEAD_DIM:(g + 1) * HEAD_DIM, pl.ds(k0, nk)]
        s = _dot_nt(kd, qvars[g])
        if bias is not None:
            s = s + jnp.concatenate([bias] * 4, axis=1)
        m_old = m_ref[g]
        m_new = jnp.maximum(m_old, jnp.max(s, axis=0, keepdims=True))
        alpha = jnp.exp2(m_old - m_new)
        p = jnp.exp2(s - m_new)
        m_ref[g] = m_new
        vte = jnp.concatenate([vt, jnp.ones((SUM_ROWS, nk), BF16)], axis=0)
        acc_ref[g] = alpha * acc_ref[g] + _dot(vte, p.astype(BF16))

    def finish(out_ref, g):
        o = acc_ref[g, :HEAD_DIM, :] / acc_ref[g, HEAD_DIM:HEAD_DIM + 1, :]
        for jb in range(2):
            blk = 2 * g + jb
            pair = jnp.concatenate([o[:, 2 * jb * tq:(2 * jb + 1) * tq], o[:, (2 * jb + 1) * tq:(2 * jb + 2) * tq]],
                                   axis=0)
            out_ref[:, blk * LANES:(blk + 1) * LANES] = pair.T.astype(BF16)

    def sel_bias(g, kt):
        rows = [jnp.broadcast_to(selb_ref[pl.ds(g * N_SEL_BLOCKS + kt * (SEL_TK // SEL_BLOCK) + r, 1), :],
                                 (SEL_BLOCK, tq)) for r in range(SEL_TK // SEL_BLOCK)]
        return jnp.concatenate(rows, axis=0)

    qvars = []
    for g in range(N_KV):
        heads = []
        for jb in range(2):
            heads.extend(_head_variants(q_ref[:, (2 * g + jb) * LANES:(2 * g + jb + 1) * LANES]))
        qvars.append(jnp.concatenate(heads, axis=0))
    groups = range(N_KV)

    reset()
    last_tile = qi // (SEL_TK // tq)

    def sel_step(kt, carry):
        for g in groups:
            update(g, ks_ref, vst_ref, kt, sel_bias(g, kt), SEL_TK)
        return carry

    lax.fori_loop(0, last_tile, sel_step, 0)
    q_first = qi * tq - last_tile * SEL_TK
    visible = (lax.broadcasted_iota(I32, (SEL_TK, tq), 0)
               <= lax.broadcasted_iota(I32, (SEL_TK, tq), 1) + q_first)
    diag_bias = jnp.where(visible, 0.0, NEG)
    for g in groups:
        update(g, ks_ref, vst_ref, last_tile, sel_bias(g, last_tile) + diag_bias, SEL_TK)
    for g in groups:
        finish(osel_ref, g)

    reset()
    back = WINDOW // tk

    @pl.when(qi >= back)
    def _():
        for g in groups:
            update(g, kw_ref, vwt_ref, qi - back, far_bias)

    for d in range(back - 1, 0, -1):
        @pl.when(qi >= d)
        def _():
            for g in groups:
                update(g, kw_ref, vwt_ref, qi - d, None)

    for g in groups:
        update(g, kw_ref, vwt_ref, qi, causal_bias)
    for g in groups:
        finish(owin_ref, g)


def _selwin(q, ks, kw, vst, vwt, selb):
    tq = SW_TQ
    nq = SEQ // tq
    assert WINDOW % SW_TQ == 0 and SEL_TK % SW_TQ == 0
    row = lambda b, i: (b * nq + i, 0)
    keys = pl.BlockSpec((1, SEQ, 256), lambda b, i: (b, 0, 0))
    vals = pl.BlockSpec((LANES, SEQ), lambda b, i: (0, b))
    r3 = lambda a: a.reshape(BATCH, SEQ, 256)
    return pl.pallas_call(
        _selwin_kernel,
        grid=(BATCH, nq),
        in_specs=[pl.BlockSpec((tq, NSA_WIDTH), row), keys, keys, vals, vals,
                  pl.BlockSpec((N_KV * N_SEL_BLOCKS, tq), lambda b, i: (0, b * nq + i))],
        out_specs=[pl.BlockSpec((tq, NSA_WIDTH), row)] * 2,
        out_shape=[jax.ShapeDtypeStruct((N_TOK, NSA_WIDTH), BF16)] * 2,
        scratch_shapes=[pltpu.VMEM((N_KV, 1, 4 * tq), F32),
                        pltpu.VMEM((N_KV, HEAD_DIM + SUM_ROWS, 4 * tq), F32)],
        compiler_params=_cparams("arbitrary", "arbitrary"),
        name="selwin",
    )(q, r3(ks), r3(kw), vst, vwt, selb)


def _s5_param_kernel(are_ref, aim_ref, ldt_ref, cre_ref, cim_ref, bre_ref, bim_ref,
                     clre_ref, clim_ref, wbre_ref, wbim_ref, bbre_ref, bbim_ref, ltre_ref, ltim_ref):
    are, aim = are_ref[...], aim_ref[...]
    dt = jnp.exp(ldt_ref[...])
    cre, cim = cre_ref[...], cim_ref[...]

    def lam_pow(tau):
        mag = jnp.exp(are * dt * float(tau))
        ang = aim * dt * float(tau)
        return mag * jnp.cos(ang), mag * jnp.sin(ang)

    lre, lim = lam_pow(1)
    den = are * are + aim * aim
    qre = ((lre - 1.0) * are + lim * aim) / den
    qim = (lim * are - (lre - 1.0) * aim) / den
    bre, bim = bre_ref[...], bim_ref[...]
    bbre = qre * bre - qim * bim
    bbim = qre * bim + qim * bre
    bbre_ref[...] = bbre
    bbim_ref[...] = bbim
    for tau in range(S5_T + 1):
        pr, pi = lam_pow(tau)
        clre_ref[tau] = cre * pr - cim * pi
        clim_ref[tau] = cre * pi + cim * pr
        if tau < S5_T:
            k = S5_T - 1 - tau
            wbre_ref[k] = pr * bbre - pi * bbim
            wbim_ref[k] = pr * bbim + pi * bbre
        else:
            ltre_ref[...] = pr
            ltim_ref[...] = pi


def _s5_kmat_kernel(l_ref, r_ref, o_ref):
    o_ref[0] = jnp.dot(l_ref[0], r_ref[0], preferred_element_type=F32, precision=HIGHEST)


def _s5_params(a_re, a_im, log_dt, b_re, b_im, c_re, c_im):
    T = S5_T
    pn = GROUP * STATE
    tile_p = lambda a: jnp.tile(a, (1, GROUP))
    args = (tile_p(a_re), tile_p(a_im), jnp.broadcast_to(log_dt[:, None], (N_GROUPS, pn)),
            c_re.reshape(N_GROUPS, pn), c_im.reshape(N_GROUPS, pn),
            jnp.swapaxes(b_re, 1, 2).reshape(N_GROUPS, pn), jnp.swapaxes(b_im, 1, 2).reshape(N_GROUPS, pn))
    full2 = pl.BlockSpec((N_GROUPS, pn), lambda: (0, 0))
    clre, clim, wbre, wbim, bbre, bbim, ltre, ltim = pl.pallas_call(
        _s5_param_kernel,
        in_specs=[full2] * 7,
        out_specs=[pl.BlockSpec((T + 1, N_GROUPS, pn), lambda: (0, 0, 0))] * 2
                  + [pl.BlockSpec((T, N_GROUPS, pn), lambda: (0, 0, 0))] * 2 + [full2] * 4,
        out_shape=[jax.ShapeDtypeStruct((T + 1, N_GROUPS, pn), F32)] * 2
                  + [jax.ShapeDtypeStruct((T, N_GROUPS, pn), F32)] * 2
                  + [jax.ShapeDtypeStruct((N_GROUPS, pn), F32)] * 4,
        name="s5_params",
    )(*args)

    r5 = lambda a, t: a[:t].reshape(t, N_GROUPS, GROUP, STATE)
    lhs = jnp.concatenate([r5(clre, T), -r5(clim, T)], axis=-1)
    lhs = jnp.transpose(lhs, (1, 0, 2, 3)).reshape(N_GROUPS, T * GROUP, 2 * STATE)
    bb = lambda a: jnp.swapaxes(a.reshape(N_GROUPS, GROUP, STATE), 1, 2)
    rhs = jnp.concatenate([bb(bbre), bb(bbim)], axis=1)
    kmat = pl.pallas_call(
        _s5_kmat_kernel,
        grid=(N_GROUPS,),
        in_specs=[pl.BlockSpec((1, T * GROUP, 2 * STATE), lambda g: (g, 0, 0)),
                  pl.BlockSpec((1, 2 * STATE, GROUP), lambda g: (g, 0, 0))],
        out_specs=pl.BlockSpec((1, T * GROUP, GROUP), lambda g: (g, 0, 0)),
        out_shape=jax.ShapeDtypeStruct((N_GROUPS, T * GROUP, GROUP), F32),
        compiler_params=_cparams("arbitrary"),
        name="s5_kmat",
    )(lhs, rhs)

    eye = jnp.eye(S5_GL, dtype=F32)
    kt = kmat.reshape(S5_SG, S5_GL, T, GROUP, GROUP)
    kbd = jnp.einsum('sgtpq,gh->stgqhp', kt, eye).reshape(S5_SG, T, LANES, LANES)
    krev = kbd[:, ::-1].reshape(S5_SG, T * LANES, LANES).astype(BF16)
    krev = jnp.pad(krev, ((0, 0), (0, LANES), (0, 0)))
    r6 = lambda a: a.reshape(T, S5_SG, S5_GL, GROUP, STATE)
    row_group = (jnp.arange(T * LANES, dtype=I32) // GROUP) % S5_GL
    lane_half = jnp.arange(LANES, dtype=I32) // STATE

    def group_diagonal(re, im):
        halves = []
        for part in (re, im):
            a = jnp.transpose(part, (1, 0, 2, 3, 4)).reshape(S5_SG, T * LANES, STATE).astype(BF16)
            a = jnp.concatenate([a, a], axis=-1)
            own_half = lane_half[None, None, :] == (row_group % 2)[None, :, None]
            halves.append(jnp.where(own_half, a, jnp.zeros_like(a)))
        tiles = [jnp.where((row_group // 2 == gp)[None, :, None], halves[ri], jnp.zeros_like(halves[ri]))
                 for gp in range(S5_GL // 2) for ri in range(2)]
        return jnp.concatenate(tiles, axis=-1)

    wb = group_diagonal(r6(wbre), r6(wbim))
    wc = jnp.swapaxes(group_diagonal(r6(clre[1:]), -r6(clim[1:])), 1, 2)
    lt = lambda a: a.reshape(N_GROUPS, GROUP, STATE)[:, 0].reshape(S5_SG, S5_GL // 2, 1, 2, STATE)
    lam_re = jnp.broadcast_to(lt(ltre), (S5_SG, S5_GL // 2, 2, 2, STATE)).reshape(S5_SG, 1, S5_NSTATE)
    lam_im = jnp.concatenate([-lt(ltim), lt(ltim)], axis=2).reshape(S5_SG, 1, S5_NSTATE)
    return krev, wb, wc, lam_re, lam_im


S5_TC = 256


def _s5_lane_block(sg):
    return pl.ds(pl.multiple_of(sg * LANES, LANES), LANES)


def _s5_chunk_inputs(x_ref, sg):
    return jnp.concatenate([x_ref[:, t, _s5_lane_block(sg)] for t in range(S5_T)], axis=1).astype(BF16)


def _s5_state_kernel(x_ref, wb_ref, e_ref):
    e_ref[0] = _dot(_s5_chunk_inputs(x_ref, pl.program_id(1)), wb_ref[0])


def _s5_scan_kernel(e_ref, ltre_ref, ltim_ref, xs_ref):
    lr, li = ltre_ref[0], ltim_ref[0]

    def swap_tiles(x):
        t = [x[:, j * LANES:(j + 1) * LANES] for j in range(S5_NSTATE // LANES)]
        return jnp.concatenate([t[j ^ 1] for j in range(len(t))], axis=1)

    def step(c, carry):
        new = []
        for b in range(BATCH):
            x = carry[b]
            row = b * S5_CH_PER_BATCH + c
            xs_ref[0, pl.ds(row, 1), :] = x
            new.append(lr * x + li * swap_tiles(x) + e_ref[0, pl.ds(row, 1), :])
        return tuple(new)

    zero = jnp.zeros((1, S5_NSTATE), F32)
    lax.fori_loop(0, S5_CH_PER_BATCH, step, tuple(zero for _ in range(BATCH)))


def _s5_out_kernel(x_ref, xs_ref, krev_ref, wc_ref, y_ref):
    sg = pl.program_id(1)
    x = _s5_chunk_inputs(x_ref, sg)
    xsb = xs_ref[0].astype(BF16)
    for t in range(0, S5_T, 2):
        n_in = (t + 2) * LANES
        first = (S5_T - 1 - t) * LANES
        taps = jnp.concatenate([krev_ref[0, first:first + n_in, :], krev_ref[0, first - LANES:first - LANES + n_in, :]],
                               axis=1)
        pair = _dot(x[:, :n_in], taps) + _dot(xsb, wc_ref[0, :, t * LANES:(t + 2) * LANES])
        y_ref[:, t, _s5_lane_block(sg)] = pair[:, :LANES]
        y_ref[:, t + 1, _s5_lane_block(sg)] = pair[:, LANES:]


def _s5(u, krev, wb, wc, ltre, ltim):
    T, tc = S5_T, S5_TC
    xn = u.reshape(S5_CH, T, SSM_WIDTH)
    grid = (S5_CH // tc, S5_SG)
    natural = pl.BlockSpec((tc, T, SSM_WIDTH), lambda i, s: (i, 0, 0))
    rows = lambda i, s: (s, i, 0)
    per_sg = lambda i, s: (s, 0, 0)
    e = pl.pallas_call(
        _s5_state_kernel, grid=grid,
        in_specs=[natural, pl.BlockSpec((1, T * LANES, S5_NSTATE), per_sg)],
        out_specs=pl.BlockSpec((1, tc, S5_NSTATE), rows),
        out_shape=jax.ShapeDtypeStruct((S5_SG, S5_CH, S5_NSTATE), F32),
        compiler_params=_cparams("arbitrary", "arbitrary"), name="s5_state",
    )(xn, wb)
    sg1 = lambda s: (s, 0, 0)
    xstart = pl.pallas_call(
        _s5_scan_kernel, grid=(S5_SG,),
        in_specs=[pl.BlockSpec((1, S5_CH, S5_NSTATE), sg1),
                  pl.BlockSpec((1, 1, S5_NSTATE), sg1), pl.BlockSpec((1, 1, S5_NSTATE), sg1)],
        out_specs=pl.BlockSpec((1, S5_CH, S5_NSTATE), sg1),
        out_shape=jax.ShapeDtypeStruct((S5_SG, S5_CH, S5_NSTATE), F32),
        compiler_params=_cparams("arbitrary"), name="s5_scan",
    )(e, ltre, ltim)
    y = pl.pallas_call(
        _s5_out_kernel, grid=grid,
        in_specs=[natural, pl.BlockSpec((1, tc, S5_NSTATE), rows),
                  pl.BlockSpec((1, (T + 1) * LANES, LANES), per_sg), pl.BlockSpec((1, S5_NSTATE, T * LANES), per_sg)],
        out_specs=natural,
        out_shape=jax.ShapeDtypeStruct((S5_CH, T, SSM_WIDTH), F32),
        compiler_params=_cparams("arbitrary", "arbitrary", vmem=56 * 1024 * 1024), name="s5_out",
    )(xn, xstart, krev, wc)
    return y.reshape(N_TOK, SSM_WIDTH)


MERGE_TM = 512


def _merge_kernel(ocmp_ref, osel_ref, owin_ref, gn_ref, yssm_ref, u_ref, ga_ref, gs_ref, x_ref, mod_ref,
                  eg_ref, dskip_ref, wglu_ref, bglu_ref, wua_ref, wus_ref, wout_ref, g2_ref,
                  wrhi_ref, wrlo_ref, wsgu_ref, wsd_ref,
                  xpart_ref, h2_ref, logit_ref):
    mod = mod_ref[0]
    gnb = gn_ref[...].astype(BF16)
    o_nsa = (_dot(gnb, eg_ref[0]) * ocmp_ref[...].astype(F32)
             + _dot(gnb, eg_ref[1]) * osel_ref[...].astype(F32)
             + _dot(gnb, eg_ref[2]) * owin_ref[...].astype(F32))
    attn = _dot(o_nsa.astype(BF16), wua_ref[...])
    z = _gelu(yssm_ref[...] + dskip_ref[...] * u_ref[...])
    y_ssm = z * jax.nn.sigmoid(_dot(z.astype(BF16), wglu_ref[...]) + bglu_ref[...])
    ssm = _dot(y_ssm.astype(BF16), wus_ref[...])
    merged = ga_ref[...].astype(F32) * attn + gs_ref[...].astype(F32) * ssm
    x1 = x_ref[...] + mod[2:3] * _dot(merged.astype(BF16), wout_ref[...])

    ms = jnp.mean(x1 * x1, axis=-1, keepdims=True)
    h2 = (x1 * lax.rsqrt(ms + EPS) * g2_ref[...]) * (1.0 + mod[4:5]) + mod[3:4]
    hi = h2.astype(BF16)
    lo = (h2 - hi.astype(F32)).astype(BF16)
    h2_ref[...] = _pack_bf16_pairs(h2)
    logit_ref[...] = _dot_nt(wrhi_ref[...], hi) + _dot_nt(wrhi_ref[...], lo) + _dot_nt(wrlo_ref[...], hi)
    gu = _dot(hi, wsgu_ref[...])
    shared = _dot((_silu(gu[:, :D_EXPERT]) * gu[:, D_EXPERT:]).astype(BF16), wsd_ref[...])
    xpart_ref[...] = x1 + mod[5:6] * shared


def _merge(ocmp, osel, owin, gn, yssm, u, ga, gs, x2, mod, d_skip, w_glu, b_glu, w_up_attn, w_up_ssm, w_out,
           g_norm2, w_router, ws_gate, ws_up, ws_down):
    tm = MERGE_TM
    eg = np.zeros((3, LANES, NSA_WIDTH), np.float32)
    for j in range(3):
        for h in range(N_HEADS):
            eg[j, 3 * h + j, h * HEAD_DIM:(h + 1) * HEAD_DIM] = 1.0
    wr_t = w_router.T
    wr_hi = wr_t.astype(BF16)
    wr_lo = (wr_t - wr_hi.astype(F32)).astype(BF16)
    row = lambda i: (i, 0)
    fix2 = lambda i: (0, 0)
    wspec = lambda a: pl.BlockSpec(a.shape, (lambda i: (0,) * a.ndim))
    weights = [jnp.asarray(eg, BF16), d_skip.reshape(1, -1), w_glu.astype(BF16), b_glu.reshape(1, -1),
               w_up_attn.astype(BF16), w_up_ssm.astype(BF16), w_out.astype(BF16), g_norm2.reshape(1, -1),
               wr_hi, wr_lo, jnp.concatenate([ws_gate, ws_up], axis=1).astype(BF16), ws_down.astype(BF16)]
    acts = [(ocmp, 512), (osel, 512), (owin, 512), (gn, 128), (yssm, 512), (u, 512), (ga, 1024), (gs, 1024),
            (x2, 1024)]
    return pl.pallas_call(
        _merge_kernel,
        grid=(N_TOK // tm,),
        in_specs=[pl.BlockSpec((tm, wd), row) for _, wd in acts]
                 + [pl.BlockSpec((1, 6, D_MODEL), lambda i: (i // (SEQ // tm), 0, 0))]
                 + [wspec(w) for w in weights],
        out_specs=[pl.BlockSpec((tm, D_MODEL), row), pl.BlockSpec((tm, HALF), row),
                   pl.BlockSpec((N_EXPERTS, tm), lambda i: (0, i))],
        out_shape=[jax.ShapeDtypeStruct((N_TOK, D_MODEL), F32), jax.ShapeDtypeStruct((N_TOK, HALF), I32),
                   jax.ShapeDtypeStruct((N_EXPERTS, N_TOK), F32)],
        compiler_params=_cparams("arbitrary", vmem=56 * 1024 * 1024),
        name="merge",
    )(*[a for a, _ in acts], mod, *weights)


ROUTE_TN = 1024


def _route_kernel(logit_ref, bias_ref, eidx_ref, w_ref, count_ref, gscore_ref, masked_ref):
    tn = ROUTE_TN

    @pl.when(pl.program_id(0) == 0)
    def _():
        count_ref[...] = jnp.zeros(count_ref.shape, F32)

    sc = jax.nn.sigmoid(logit_ref[...])
    biased = sc + bias_ref[...]
    gi = lax.broadcasted_iota(I32, (EXPERTS_PER_GROUP, tn), 0).astype(F32)
    for g in range(N_EXPERT_GROUPS):
        blk = biased[g * EXPERTS_PER_GROUP:(g + 1) * EXPERTS_PER_GROUP]
        m1 = jnp.max(blk, axis=0, keepdims=True)
        i1 = jnp.min(jnp.where(blk == m1, gi, float(EXPERTS_PER_GROUP)), axis=0, keepdims=True)
        m2 = jnp.max(jnp.where(gi == i1, -jnp.inf, blk), axis=0, keepdims=True)
        gscore_ref[g:g + 1, :] = m1 + m2
    gs = gscore_ref[...]
    gidx = lax.broadcasted_iota(I32, (N_EXPERT_GROUPS, tn), 0)
    grank = jnp.zeros((N_EXPERT_GROUPS, tn), F32)
    for gp in range(N_EXPERT_GROUPS):
        row = gs[gp:gp + 1, :]
        tie = jnp.where(gidx > gp, 1.0, 0.0)
        grank = grank + jnp.where(row > gs, 1.0, jnp.where(row == gs, tie, 0.0))
    for g in range(N_EXPERT_GROUPS):
        keep = grank[g:g + 1, :] < float(TOPK_GROUPS)
        sl = slice(g * EXPERTS_PER_GROUP, (g + 1) * EXPERTS_PER_GROUP)
        masked_ref[sl, :] = jnp.where(keep, biased[sl], -jnp.inf)
    cur = masked_ref[...]
    eidx = lax.broadcasted_iota(I32, (N_EXPERTS, tn), 0).astype(F32)
    wsum = jnp.zeros((1, tn), F32)
    hits = jnp.zeros((N_EXPERTS, tn), F32)
    for k in range(TOP_K):
        m = jnp.max(cur, axis=0, keepdims=True)
        idx = jnp.min(jnp.where(cur == m, eidx, float(N_EXPERTS)), axis=0, keepdims=True)
        hit = eidx == idx
        wk = jnp.sum(jnp.where(hit, sc, 0.0), axis=0, keepdims=True)
        cur = jnp.where(hit, -jnp.inf, cur)
        hits = hits + jnp.where(hit, 1.0, 0.0)
        eidx_ref[k:k + 1, :] = idx.astype(I32)
        w_ref[k:k + 1, :] = wk
        wsum = wsum + wk
    w_ref[...] = w_ref[...] / wsum * ROUTE_SCALE
    count_ref[...] = count_ref[...] + jnp.sum(hits, axis=1, keepdims=True)


def _route(logits_t, router_bias):
    tn = ROUTE_TN
    return pl.pallas_call(
        _route_kernel,
        grid=(N_TOK // tn,),
        in_specs=[pl.BlockSpec((N_EXPERTS, tn), lambda i: (0, i)), pl.BlockSpec((N_EXPERTS, 1), lambda i: (0, 0))],
        out_specs=[pl.BlockSpec((TOP_K, tn), lambda i: (0, i))] * 2 + [pl.BlockSpec((N_EXPERTS, 1), lambda i: (0, 0))],
        out_shape=[jax.ShapeDtypeStruct((TOP_K, N_TOK), I32), jax.ShapeDtypeStruct((TOP_K, N_TOK), F32),
                   jax.ShapeDtypeStruct((N_EXPERTS, 1), F32)],
        scratch_shapes=[pltpu.VMEM((N_EXPERT_GROUPS, tn), F32), pltpu.VMEM((N_EXPERTS, tn), F32)],
        compiler_params=_cparams("arbitrary"),
        name="route",
    )(logits_t, router_bias.reshape(-1, 1))


N_MOE_BLK = NK // DISPATCH_BLOCK
N_ITEMS = N_MOE_BLK + N_EXPERTS


def _dispatch_plan(eidx, counts):
    e_flat = eidx.reshape(-1)
    key = jnp.sort(e_flat * NK + jnp.arange(NK, dtype=I32))
    order = key & (NK - 1)
    counts = counts.reshape(-1).astype(I32)
    start = jnp.cumsum(counts) - counts
    cuts = jnp.sort(jnp.concatenate([jnp.arange(N_MOE_BLK, dtype=I32) * DISPATCH_BLOCK, start]))
    lo = cuts
    hi = jnp.concatenate([cuts[1:], jnp.full((1,), NK, I32)])
    blk = jnp.minimum(lo // DISPATCH_BLOCK, N_MOE_BLK - 1)
    expert = jnp.clip(jnp.sum((start[None, :] <= lo[:, None]).astype(I32), axis=1) - 1, 0, N_EXPERTS - 1)
    one = jnp.ones((1,), I32)
    first = jnp.concatenate([one, (blk[1:] != blk[:-1]).astype(I32)])
    last = jnp.concatenate([(blk[1:] != blk[:-1]).astype(I32), one])
    new_expert = jnp.concatenate([one, (expert[1:] != expert[:-1]).astype(I32)])
    run_id = jnp.cumsum(new_expert) - 1
    n_runs = run_id[-1] + 1
    item = jnp.arange(N_ITEMS, dtype=I32)
    run_first_item = jnp.sort(jnp.where(new_expert == 1, item, N_ITEMS))
    run_expert = expert[jnp.minimum(run_first_item, N_ITEMS - 1)]
    ahead = run_id + (WEIGHT_RING - 1)
    ahead_expert = run_expert[jnp.minimum(ahead, N_ITEMS - 1)]
    ahead_valid = (ahead < n_runs).astype(I32)
    second_expert = run_expert[1:2]
    prologue = jnp.concatenate([second_expert, (n_runs > 1).astype(I32).reshape(1)])
    tok = jnp.right_shift(order, TOP_K.bit_length() - 1)
    home = (order & (TOP_K - 1)) * N_TOK + tok
    return tok, home, (blk, expert, lo - blk * DISPATCH_BLOCK, hi - blk * DISPATCH_BLOCK, first, last, new_expert,
                      run_id % WEIGHT_RING, ahead_expert, ahead_valid, prologue)


SC_CORES = 2
SC_SUBCORES = 16
SC_CHUNK = 128


def _sc_move_rows(table, idx, scatter):
    n = idx.shape[0]
    workers = SC_CORES * SC_SUBCORES
    per_worker = n // workers
    n_chunks = per_worker // SC_CHUNK
    assert per_worker * workers == n and n_chunks * SC_CHUNK == per_worker
    mesh = plsc.VectorSubcoreMesh(core_axis_name="c", subcore_axis_name="s",
                                  num_cores=SC_CORES, num_subcores=SC_SUBCORES)

    def body(table_hbm, idx_hbm, out_hbm, idx_v, rows_v, sem):
        wid = lax.axis_index("s") * SC_CORES + lax.axis_index("c")
        base = wid * per_worker

        @pl.loop(0, n_chunks)
        def _(j):
            off = base + j * SC_CHUNK
            pltpu.sync_copy(idx_hbm.at[pl.ds(off, SC_CHUNK)], idx_v)
            if scatter:
                pltpu.sync_copy(table_hbm.at[pl.ds(off, SC_CHUNK)], rows_v)
                pltpu.async_copy(rows_v, out_hbm.at[idx_v], sem).wait()
            else:
                pltpu.async_copy(table_hbm.at[idx_v], rows_v, sem).wait()
                pltpu.sync_copy(rows_v, out_hbm.at[pl.ds(off, SC_CHUNK)])

    return pl.kernel(
        body,
        out_type=jax.ShapeDtypeStruct((n, table.shape[1]), table.dtype),
        mesh=mesh,
        scratch_types=[pltpu.VMEM((SC_CHUNK,), I32), pltpu.VMEM((SC_CHUNK, table.shape[1]), table.dtype),
                       pltpu.SemaphoreType.DMA],
        name="sc_scatter_rows" if scatter else "sc_gather_rows",
    )(table, idx)


WEIGHT_RING = 3
WEIGHT_CHUNKS = 4


def _expert_weight_copies(w_hbm, wbuf, sem, expert, slot):
    rows = w_hbm.shape[1] // WEIGHT_CHUNKS
    return [pltpu.make_async_copy(w_hbm.at[expert, pl.ds(c * rows, rows)],
                                  wbuf.at[slot, pl.ds(c * rows, rows)], sem.at[slot])
            for c in range(WEIGHT_CHUNKS)]


def _moe_kernel(blk_ref, exp_ref, lo_ref, hi_ref, first_ref, last_ref, newexp_ref,
                slot_ref, ahead_exp_ref, ahead_ok_ref, prologue_ref,
                x_ref, wg_hbm, wu_hbm, wd_hbm, y_ref,
                acc_ref, wgf_ref, wuf_ref, wdf_ref, wgb_ref, wub_ref, wdb_ref, wsem):
    it = pl.program_id(0)
    lo, hi = lo_ref[it], hi_ref[it]
    streams = ((wg_hbm, wgf_ref), (wu_hbm, wuf_ref), (wd_hbm, wdf_ref))

    def request(expert, slot):
        for w_hbm, wbuf in streams:
            for cp in _expert_weight_copies(w_hbm, wbuf, wsem, expert, slot):
                cp.start()

    @pl.when(it == 0)
    def _():
        request(exp_ref[0], 0)

        @pl.when(prologue_ref[1] == 1)
        def _():
            request(prologue_ref[0], 1)

    @pl.when(newexp_ref[it] == 1)
    def _():
        slot = slot_ref[it]
        for w_hbm, wbuf in streams:
            for cp in _expert_weight_copies(w_hbm, wbuf, wsem, 0, slot):
                cp.wait()
        wgb_ref[...] = wgf_ref[slot].astype(BF16)
        wub_ref[...] = wuf_ref[slot].astype(BF16)
        wdb_ref[...] = wdf_ref[slot].astype(BF16)

        @pl.when(ahead_ok_ref[it] == 1)
        def _():
            ahead_slot = slot + (WEIGHT_RING - 1)
            request(ahead_exp_ref[it], jnp.where(ahead_slot >= WEIGHT_RING, ahead_slot - WEIGHT_RING, ahead_slot))

    @pl.when(first_ref[it] == 1)
    def _():
        acc_ref[...] = jnp.zeros(acc_ref.shape, F32)

    def expert_pass(r0, nrows):
        rows = slice(r0, r0 + nrows)
        ridx = r0 + lax.broadcasted_iota(I32, (nrows, HALF), 0)
        mine = (ridx >= lo) & (ridx < hi)
        xlo, xhi = _unpack_bf16_pairs(jnp.where(mine, x_ref[rows, :], 0))
        xlo, xhi = xlo.astype(BF16), xhi.astype(BF16)
        gate = _dot(xlo, wgb_ref[:HALF]) + _dot(xhi, wgb_ref[HALF:])
        up = _dot(xlo, wub_ref[:HALF]) + _dot(xhi, wub_ref[HALF:])
        acc_ref[rows, :] = acc_ref[rows, :] + _dot((_silu(gate) * up).astype(BF16), wdb_ref[...])

    mid = DISPATCH_BLOCK // 2
    pl.when((lo < mid) & (hi > mid))(lambda: expert_pass(0, DISPATCH_BLOCK))
    pl.when((hi > lo) & (hi <= mid))(lambda: expert_pass(0, mid))
    pl.when((hi > lo) & (lo >= mid))(lambda: expert_pass(mid, mid))

    @pl.when(last_ref[it] == 1)
    def _():
        y_ref[...] = _pack_bf16_pairs(acc_ref[...])


def _moe(xs, items, w_gate, w_up, w_down):
    by_blk = lambda it, blk, *_: (blk[it], 0)
    any_space = pl.BlockSpec(memory_space=pl.ANY)
    grid_spec = pltpu.PrefetchScalarGridSpec(
        num_scalar_prefetch=len(items),
        grid=(N_ITEMS,),
        in_specs=[pl.BlockSpec((DISPATCH_BLOCK, HALF), by_blk), any_space, any_space, any_space],
        out_specs=pl.BlockSpec((DISPATCH_BLOCK, HALF), by_blk),
        scratch_shapes=[pltpu.VMEM((DISPATCH_BLOCK, D_MODEL), F32),
                        pltpu.VMEM((WEIGHT_RING, D_MODEL, D_EXPERT), F32),
                        pltpu.VMEM((WEIGHT_RING, D_MODEL, D_EXPERT), F32),
                        pltpu.VMEM((WEIGHT_RING, D_EXPERT, D_MODEL), F32),
                        pltpu.VMEM((D_MODEL, D_EXPERT), BF16), pltpu.VMEM((D_MODEL, D_EXPERT), BF16),
                        pltpu.VMEM((D_EXPERT, D_MODEL), BF16),
                        pltpu.SemaphoreType.DMA((WEIGHT_RING,))],
    )
    return pl.pallas_call(
        _moe_kernel,
        grid_spec=grid_spec,
        out_shape=jax.ShapeDtypeStruct((NK, HALF), I32),
        compiler_params=_cparams("arbitrary"),
        name="moe",
    )(*items, xs, w_gate, w_up, w_down)


COMB_TC = 512


def _combine_kernel(slots_ref, w_ref, xpart_ref, mod_ref, out_ref):
    w = w_ref[...]
    lo = jnp.zeros((w.shape[0], HALF), F32)
    hi = jnp.zeros((w.shape[0], HALF), F32)
    for k in range(TOP_K):
        klo, khi = _unpack_bf16_pairs(slots_ref[k])
        lo = lo + w[:, k:k + 1] * klo
        hi = hi + w[:, k:k + 1] * khi
    gate2 = mod_ref[0][5:6]
    out_ref[:, :HALF] = xpart_ref[:, :HALF] + gate2[:, :HALF] * lo
    out_ref[:, HALF:] = xpart_ref[:, HALF:] + gate2[:, HALF:] * hi


def _combine(xpart, mod, slots, w):
    tc = COMB_TC
    row = lambda i: (i, 0)
    return pl.pallas_call(
        _combine_kernel,
        grid=(N_TOK // tc,),
        in_specs=[pl.BlockSpec((TOP_K, tc, HALF), lambda i: (0, i, 0)),
                  pl.BlockSpec((tc, TOP_K), row),
                  pl.BlockSpec((tc, D_MODEL), row),
                  pl.BlockSpec((1, 6, D_MODEL), lambda i: (i // (SEQ // tc), 0, 0))],
        out_specs=pl.BlockSpec((tc, D_MODEL), row),
        out_shape=jax.ShapeDtypeStruct((N_TOK, D_MODEL), F32),
        compiler_params=_cparams("arbitrary"),
        name="combine",
    )(slots.reshape(TOP_K, N_TOK, HALF), w, xpart, mod)


def _layer(x, c, w_ada, b_ada, g_norm1, g_norm2, w_in, q_gain, kc_gain, ks_gain, kw_gain,
           pe_k, pe_v, w_cmp_k1, w_cmp_k2, w_cmp_v1, w_cmp_v2,
           a_re, a_im, log_dt, b_re, b_im, c_re, c_im, d_skip, w_glu, b_glu,
           w_up_attn, w_up_ssm, w_out, w_router, router_bias,
           w_gate, w_up, w_down, ws_gate, ws_up, ws_down):
    x2 = x.reshape(N_TOK, D_MODEL)
    mod = _ada(c, w_ada, b_ada)
    q, kc_raw, vc_raw, ks, kw, vst, vwt, gn, u, ga, gs = _proj(x2, mod, g_norm1, w_in, q_gain, ks_gain, kw_gain)
    kcn = _compress(kc_raw, pe_k, w_cmp_k1, w_cmp_k2, kc_gain, True)
    vcn = _compress(vc_raw, pe_v, w_cmp_v1, w_cmp_v2, kc_gain, False)
    ocmp, selb = _cmp_attn(q, kcn, vcn)
    osel, owin = _selwin(q, ks, kw, vst, vwt, selb)
    yssm = _s5(u, *_s5_params(a_re, a_im, log_dt, b_re, b_im, c_re, c_im))
    xpart, h2, logits_t = _merge(ocmp, osel, owin, gn, yssm, u, ga, gs, x2, mod, d_skip, w_glu, b_glu,
                                  w_up_attn, w_up_ssm, w_out, g_norm2, w_router, ws_gate, ws_up, ws_down)
    eidx_t, w_t, counts = _route(logits_t, router_bias)
    tok, home, items = _dispatch_plan(eidx_t.T, counts)
    y = _moe(_sc_move_rows(h2, tok, scatter=False), items, w_gate, w_up, w_down)
    slots = _sc_move_rows(y, home, scatter=True)
    return _combine(xpart, mod, slots, w_t.T).reshape(BATCH, SEQ, D_MODEL)


def kernel(x, c, w_ada, b_ada, g_norm1, g_norm2, w_in, q_gain, kc_gain, ks_gain, kw_gain, pe_k, pe_v, w_cmp_k1,
           w_cmp_k2, w_cmp_v1, w_cmp_v2, a_re, a_im, log_dt, b_re, b_im, c_re, c_im, d_skip, w_glu, b_glu,
           w_up_attn, w_up_ssm, w_out, w_router, router_bias, w_gate, w_up, w_down, ws_gate, ws_up, ws_down):
    params = (w_ada, b_ada, g_norm1, g_norm2, w_in, q_gain, kc_gain, ks_gain, kw_gain, pe_k, pe_v, w_cmp_k1,
              w_cmp_k2, w_cmp_v1, w_cmp_v2, a_re, a_im, log_dt, b_re, b_im, c_re, c_im, d_skip, w_glu, b_glu,
              w_up_attn, w_up_ssm, w_out, w_router, router_bias, w_gate, w_up, w_down, ws_gate, ws_up, ws_down)
    depth = w_ada.shape[0]
    for layer in range(depth):
        x = _layer(x, c, *[p[layer] for p in params])
    return x
```
